```python
import jax, jax.numpy as jnp
from jax import lax
import numpy as np

D_MODEL = 1024
BATCH = 8
SEQ = 4096
DEPTH = 1

N_HEADS = 8
QK_NOPE_DIM = 128
QK_ROPE_DIM = 64
V_HEAD_DIM = 128
Q_LORA_RANK = 3 * D_MODEL // 8
KV_LORA_RANK = D_MODEL // 4
ROPE_THETA = 10000.0
Q_BLOCK = 128
CONV_CHANNELS = D_MODEL
CONV_WIDTH = 31
D_FF = 4 * D_MODEL
N_BRANCHES = 2
EPS = 1e-6
IN_SIZES = (Q_LORA_RANK, KV_LORA_RANK, QK_ROPE_DIM, 2 * CONV_CHANNELS, N_BRANCHES * D_MODEL)
IN_OFFSETS = tuple(int(o) for o in np.cumsum(IN_SIZES)[:-1])
D_IN = int(sum(IN_SIZES))

kernel_name = "hybrid_mla_conformer_conv_gated_block"


def rms_norm(x, g):
    x32 = x.astype(jnp.float32)
    y = x32 * lax.rsqrt(jnp.mean(x32 * x32, axis=-1, keepdims=True) + EPS)
    return y.astype(x.dtype) * g


def layer_norm(x, g, b):
    x32 = x.astype(jnp.float32)
    mu = jnp.mean(x32, axis=-1, keepdims=True)
    xc = x32 - mu
    var = jnp.mean(xc * xc, axis=-1, keepdims=True)
    y = xc * lax.rsqrt(var + EPS)
    return y.astype(x.dtype) * g + b


def rope_tables(positions):
    inv_freq = ROPE_THETA ** (-jnp.arange(0, QK_ROPE_DIM, 2, dtype=jnp.float32) / QK_ROPE_DIM)
    ang = positions.astype(jnp.float32)[..., None] * inv_freq
    return jnp.cos(ang), jnp.sin(ang)


def apply_rope(x, cos, sin):
    half = x.shape[-1] // 2
    x1, x2 = x[..., :half], x[..., half:]
    cos = cos.astype(x.dtype)
    sin = sin.astype(x.dtype)
    return jnp.concatenate([x1 * cos - x2 * sin, x2 * cos + x1 * sin], axis=-1)


def causal_block_attention(q_nope, q_rope, k_nope, k_rope, v):
    B, H, S, Dn = q_nope.shape
    Dr = q_rope.shape[-1]
    Dv = v.shape[-1]
    n_blk = S // Q_BLOCK
    qn = q_nope.reshape(B, H, n_blk, Q_BLOCK, Dn).transpose(2, 0, 1, 3, 4)
    qr = q_rope.reshape(B, H, n_blk, Q_BLOCK, Dr).transpose(2, 0, 1, 3, 4)
    scale = (Dn + Dr) ** -0.5
    key_idx = jnp.arange(S)

    def one_block(args):
        qn_b, qr_b, blk = args
        s = jnp.einsum('bhqd,bhkd->bhqk', qn_b, k_nope) + jnp.einsum('bhqd,bkd->bhqk', qr_b, k_rope)
        s = s.astype(jnp.float32) * scale
        q_idx = blk * Q_BLOCK + jnp.arange(Q_BLOCK)
        mask = key_idx[None, :] <= q_idx[:, None]
        s = jnp.where(mask, s, -jnp.inf)
        p = jax.nn.softmax(s, axis=-1).astype(v.dtype)
        return jnp.einsum('bhqk,bhkd->bhqd', p, v)

    out = lax.map(one_block, (qn, qr, jnp.arange(n_blk)))
    return out.transpose(1, 0, 3, 2, 4).reshape(B, S, H * Dv)


def hybrid_mixer(h, cos, sin, w_in, q_norm, w_uq, kv_norm, w_uk, w_uv, w_o_attn,
                 conv_w, conv_b, conv_ln_g, conv_ln_b, w_pw2, b_pw2, w_out):
    B, S, _ = h.shape
    z = h @ w_in
    c_q, c_kv, k_r, u_glu, gate_logits = jnp.split(z, IN_OFFSETS, axis=-1)

    q = (rms_norm(c_q, q_norm) @ w_uq).reshape(B, S, N_HEADS, QK_NOPE_DIM + QK_ROPE_DIM)
    q_nope = q[..., :QK_NOPE_DIM]
    q_rope = apply_rope(q[..., QK_NOPE_DIM:], cos[:, :, None, :], sin[:, :, None, :])
    c_kv_n = rms_norm(c_kv, kv_norm)
    k_nope = jnp.einsum('bsc,chd->bhsd', c_kv_n, w_uk)
    v = jnp.einsum('bsc,chd->bhsd', c_kv_n, w_uv)
    k_rope = apply_rope(k_r, cos, sin)
    attn = causal_block_attention(q_nope.transpose(0, 2, 1, 3), q_rope.transpose(0, 2, 1, 3),
                                  k_nope, k_rope, v)
    y_attn = attn @ w_o_attn

    a, b = jnp.split(u_glu, 2, axis=-1)
    u = a * jax.nn.sigmoid(b)
    u = lax.conv_general_dilated(u, conv_w, window_strides=(1,),
                                 padding=[(CONV_WIDTH - 1, 0)],
                                 dimension_numbers=('NWC', 'WIO', 'NWC'),
                                 feature_group_count=CONV_CHANNELS) + conv_b
    u = jax.nn.silu(layer_norm(u, conv_ln_g, conv_ln_b))
    y_conv = u @ w_pw2 + b_pw2

    g_attn, g_conv = jnp.split(gate_logits, N_BRANCHES, axis=-1)
    merged = jax.nn.sigmoid(g_attn) * y_attn + jax.nn.sigmoid(g_conv) * y_conv
    return merged @ w_out


def squared_relu_mlp(h, w_ff1, w_ff2):
    return jnp.square(jax.nn.relu(h @ w_ff1)) @ w_ff2


def _fwd_setup_inputs(seed: int = 0) -> dict:
    key = jax.random.key(seed)
    ks = jax.random.split(key, 24)
    f32 = jnp.float32

    def nrm(k, shape, scale):
        return jax.random.normal(k, shape, f32) * scale

    def gain(k, shape):
        return 1.0 + 0.01 * jax.random.normal(k, shape, f32)

    L = DEPTH
    x = jax.random.normal(ks[0], (BATCH, SEQ, D_MODEL), f32)
    offset = jax.random.randint(ks[1], (BATCH, 1), 0, 1024, dtype=jnp.int32)
    positions = (offset + jnp.arange(SEQ, dtype=jnp.int32)[None, :]).astype(jnp.int32)
    return {
        "x": x,
        "positions": positions,
        "norm_mix_pre": gain(ks[2], (L, D_MODEL)),
        "w_in": nrm(ks[3], (L, D_MODEL, D_IN), D_MODEL ** -0.5),
        "q_norm": gain(ks[4], (L, Q_LORA_RANK)),
        "w_uq": nrm(ks[5], (L, Q_LORA_RANK, N_HEADS * (QK_NOPE_DIM + QK_ROPE_DIM)), Q_LORA_RANK ** -0.5),
        "kv_norm": gain(ks[6], (L, KV_LORA_RANK)),
        "w_uk": nrm(ks[7], (L, KV_LORA_RANK, N_HEADS, QK_NOPE_DIM), KV_LORA_RANK ** -0.5),
        "w_uv": nrm(ks[8], (L, KV_LORA_RANK, N_HEADS, V_HEAD_DIM), KV_LORA_RANK ** -0.5),
        "w_o_attn": nrm(ks[9], (L, N_HEADS * V_HEAD_DIM, D_MODEL), (N_HEADS * V_HEAD_DIM) ** -0.5),
        "conv_w": nrm(ks[10], (L, CONV_WIDTH, 1, CONV_CHANNELS), CONV_WIDTH ** -0.5),
        "conv_b": nrm(ks[11], (L, CONV_CHANNELS), 0.01),
        "conv_ln_g": gain(ks[12], (L, CONV_CHANNELS)),
        "conv_ln_b": nrm(ks[13], (L, CONV_CHANNELS), 0.01),
        "w_pw2": nrm(ks[14], (L, CONV_CHANNELS, D_MODEL), CONV_CHANNELS ** -0.5),
        "b_pw2": nrm(ks[15], (L, D_MODEL), 0.01),
        "w_out": nrm(ks[16], (L, D_MODEL, D_MODEL), D_MODEL ** -0.5),
        "norm_mix_post": gain(ks[17], (L, D_MODEL)),
        "norm_mlp_pre": gain(ks[18], (L, D_MODEL)),
        "w_ff1": nrm(ks[19], (L, D_MODEL, D_FF), D_MODEL ** -0.5),
        "w_ff2": nrm(ks[20], (L, D_FF, D_MODEL), D_FF ** -0.5),
        "norm_mlp_post": gain(ks[21], (L, D_MODEL)),
    }


def _fwd_reference(x, positions, norm_mix_pre, w_in, q_norm, w_uq, kv_norm, w_uk, w_uv, w_o_attn,
              conv_w, conv_b, conv_ln_g, conv_ln_b, w_pw2, b_pw2, w_out, norm_mix_post,
              norm_mlp_pre, w_ff1, w_ff2, norm_mlp_post):
    cos, sin = rope_tables(positions)
    for l in range(DEPTH):
        h = rms_norm(x, norm_mix_pre[l])
        m = hybrid_mixer(h, cos, sin, w_in[l], q_norm[l], w_uq[l], kv_norm[l], w_uk[l], w_uv[l],
                         w_o_attn[l], conv_w[l], conv_b[l], conv_ln_g[l], conv_ln_b[l],
                         w_pw2[l], b_pw2[l], w_out[l])
        x = x + rms_norm(m, norm_mix_post[l])
        f = squared_relu_mlp(rms_norm(x, norm_mlp_pre[l]), w_ff1[l], w_ff2[l])
        x = x + rms_norm(f, norm_mlp_post[l])
    return x


import jax as _jax
import jax.numpy as _jnp

TWIN_FORMAT = 'train_step'
FWD_PARAMS = ['x', 'positions', 'norm_mix_pre', 'w_in', 'q_norm', 'w_uq', 'kv_norm', 'w_uk', 'w_uv', 'w_o_attn', 'conv_w', 'conv_b', 'conv_ln_g', 'conv_ln_b', 'w_pw2', 'b_pw2', 'w_out', 'norm_mix_post', 'norm_mlp_pre', 'w_ff1', 'w_ff2', 'norm_mlp_post']
TWIN_WEIGHTS = ['norm_mix_pre', 'w_in', 'q_norm', 'w_uq', 'kv_norm', 'w_uk', 'w_uv', 'w_o_attn', 'conv_w', 'conv_b', 'conv_ln_g', 'conv_ln_b', 'w_pw2', 'b_pw2', 'w_out', 'norm_mix_post', 'norm_mlp_pre', 'w_ff1', 'w_ff2', 'norm_mlp_post']
TWIN_DIFF_INPUT = 'x'
TWIN_INPUTS = ['x', 'positions', 'norm_mix_pre', 'w_in', 'q_norm', 'w_uq', 'kv_norm', 'w_uk', 'w_uv', 'w_o_attn', 'conv_w', 'conv_b', 'conv_ln_g', 'conv_ln_b', 'w_pw2', 'b_pw2', 'w_out', 'norm_mix_post', 'norm_mlp_pre', 'w_ff1', 'w_ff2', 'norm_mlp_post', 'loss_target', 'm_norm_mix_pre', 'm_w_in', 'm_q_norm', 'm_w_uq', 'm_kv_norm', 'm_w_uk', 'm_w_uv', 'm_w_o_attn', 'm_conv_w', 'm_conv_b', 'm_conv_ln_g', 'm_conv_ln_b', 'm_w_pw2', 'm_b_pw2', 'm_w_out', 'm_norm_mix_post', 'm_norm_mlp_pre', 'm_w_ff1', 'm_w_ff2', 'm_norm_mlp_post', 'v_norm_mix_pre', 'v_w_in', 'v_q_norm', 'v_w_uq', 'v_kv_norm', 'v_w_uk', 'v_w_uv', 'v_w_o_attn', 'v_conv_w', 'v_conv_b', 'v_conv_ln_g', 'v_conv_ln_b', 'v_w_pw2', 'v_b_pw2', 'v_w_out', 'v_norm_mix_post', 'v_norm_mlp_pre', 'v_w_ff1', 'v_w_ff2', 'v_norm_mlp_post']
TWIN_OUTPUTS = ['loss', 'grad_x', 'grad_norm_mix_pre', 'grad_w_in', 'grad_q_norm', 'grad_w_uq', 'grad_kv_norm', 'grad_w_uk', 'grad_w_uv', 'grad_w_o_attn', 'grad_conv_w', 'grad_conv_b', 'grad_conv_ln_g', 'grad_conv_ln_b', 'grad_w_pw2', 'grad_b_pw2', 'grad_w_out', 'grad_norm_mix_post', 'grad_norm_mlp_pre', 'grad_w_ff1', 'grad_w_ff2', 'grad_norm_mlp_post', 'delta_norm_mix_pre', 'delta_w_in', 'delta_q_norm', 'delta_w_uq', 'delta_kv_norm', 'delta_w_uk', 'delta_w_uv', 'delta_w_o_attn', 'delta_conv_w', 'delta_conv_b', 'delta_conv_ln_g', 'delta_conv_ln_b', 'delta_w_pw2', 'delta_b_pw2', 'delta_w_out', 'delta_norm_mix_post', 'delta_norm_mlp_pre', 'delta_w_ff1', 'delta_w_ff2', 'delta_norm_mlp_post', 'new_m_norm_mix_pre', 'new_m_w_in', 'new_m_q_norm', 'new_m_w_uq', 'new_m_kv_norm', 'new_m_w_uk', 'new_m_w_uv', 'new_m_w_o_attn', 'new_m_conv_w', 'new_m_conv_b', 'new_m_conv_ln_g', 'new_m_conv_ln_b', 'new_m_w_pw2', 'new_m_b_pw2', 'new_m_w_out', 'new_m_norm_mix_post', 'new_m_norm_mlp_pre', 'new_m_w_ff1', 'new_m_w_ff2', 'new_m_norm_mlp_post', 'new_v_norm_mix_pre', 'new_v_w_in', 'new_v_q_norm', 'new_v_w_uq', 'new_v_kv_norm', 'new_v_w_uk', 'new_v_w_uv', 'new_v_w_o_attn', 'new_v_conv_w', 'new_v_conv_b', 'new_v_conv_ln_g', 'new_v_conv_ln_b', 'new_v_w_pw2', 'new_v_b_pw2', 'new_v_w_out', 'new_v_norm_mix_post', 'new_v_norm_mlp_pre', 'new_v_w_ff1', 'new_v_w_ff2', 'new_v_norm_mlp_post']
TWIN_LEAF_KINDS = {'loss': 'loss', 'grad_x': 'grad_x', 'grad_norm_mix_pre': 'grad_w', 'grad_w_in': 'grad_w', 'grad_q_norm': 'grad_w', 'grad_w_uq': 'grad_w', 'grad_kv_norm': 'grad_w', 'grad_w_uk': 'grad_w', 'grad_w_uv': 'grad_w', 'grad_w_o_attn': 'grad_w', 'grad_conv_w': 'grad_w', 'grad_conv_b': 'grad_w', 'grad_conv_ln_g': 'grad_w', 'grad_conv_ln_b': 'grad_w', 'grad_w_pw2': 'grad_w', 'grad_b_pw2': 'grad_w', 'grad_w_out': 'grad_w', 'grad_norm_mix_post': 'grad_w', 'grad_norm_mlp_pre': 'grad_w', 'grad_w_ff1': 'grad_w', 'grad_w_ff2': 'grad_w', 'grad_norm_mlp_post': 'grad_w', 'delta_norm_mix_pre': 'delta_w', 'delta_w_in': 'delta_w', 'delta_q_norm': 'delta_w', 'delta_w_uq': 'delta_w', 'delta_kv_norm': 'delta_w', 'delta_w_uk': 'delta_w', 'delta_w_uv': 'delta_w', 'delta_w_o_attn': 'delta_w', 'delta_conv_w': 'delta_w', 'delta_conv_b': 'delta_w', 'delta_conv_ln_g': 'delta_w', 'delta_conv_ln_b': 'delta_w', 'delta_w_pw2': 'delta_w', 'delta_b_pw2': 'delta_w', 'delta_w_out': 'delta_w', 'delta_norm_mix_post': 'delta_w', 'delta_norm_mlp_pre': 'delta_w', 'delta_w_ff1': 'delta_w', 'delta_w_ff2': 'delta_w', 'delta_norm_mlp_post': 'delta_w', 'new_m_norm_mix_pre': 'new_m', 'new_m_w_in': 'new_m', 'new_m_q_norm': 'new_m', 'new_m_w_uq': 'new_m', 'new_m_kv_norm': 'new_m', 'new_m_w_uk': 'new_m', 'new_m_w_uv': 'new_m', 'new_m_w_o_attn': 'new_m', 'new_m_conv_w': 'new_m', 'new_m_conv_b': 'new_m', 'new_m_conv_ln_g': 'new_m', 'new_m_conv_ln_b': 'new_m', 'new_m_w_pw2': 'new_m', 'new_m_b_pw2': 'new_m', 'new_m_w_out': 'new_m', 'new_m_norm_mix_post': 'new_m', 'new_m_norm_mlp_pre': 'new_m', 'new_m_w_ff1': 'new_m', 'new_m_w_ff2': 'new_m', 'new_m_norm_mlp_post': 'new_m', 'new_v_norm_mix_pre': 'new_v', 'new_v_w_in': 'new_v', 'new_v_q_norm': 'new_v', 'new_v_w_uq': 'new_v', 'new_v_kv_norm': 'new_v', 'new_v_w_uk': 'new_v', 'new_v_w_uv': 'new_v', 'new_v_w_o_attn': 'new_v', 'new_v_conv_w': 'new_v', 'new_v_conv_b': 'new_v', 'new_v_conv_ln_g': 'new_v', 'new_v_conv_ln_b': 'new_v', 'new_v_w_pw2': 'new_v', 'new_v_b_pw2': 'new_v', 'new_v_w_out': 'new_v', 'new_v_norm_mix_post': 'new_v', 'new_v_norm_mlp_pre': 'new_v', 'new_v_w_ff1': 'new_v', 'new_v_w_ff2': 'new_v', 'new_v_norm_mlp_post': 'new_v'}


def _forward(args):
    return _fwd_reference(*[args[k] for k in FWD_PARAMS])


def _output_shape():
    def fwd():
        inp = _fwd_setup_inputs(0)
        return _fwd_reference(*[inp[k] for k in FWD_PARAMS])
    out = _jax.eval_shape(fwd)
    return out.shape, out.dtype

N_MICROBATCH = 1
ADAM_LR = 0.001
ADAM_B1 = 0.9
ADAM_B2 = 0.999
ADAM_EPS = 1e-08
ADAM_WD = 0.01
ADAM_STEP = 10
PER_EXAMPLE_BATCH_AXIS = {'x': 0, 'positions': 0, 'loss_target': 0}
SHARED_INPUTS = []
_WEIGHT_DTYPES = {'norm_mix_pre': _jnp.float32, 'w_in': _jnp.float32, 'q_norm': _jnp.float32, 'w_uq': _jnp.float32, 'kv_norm': _jnp.float32, 'w_uk': _jnp.float32, 'w_uv': _jnp.float32, 'w_o_attn': _jnp.float32, 'conv_w': _jnp.float32, 'conv_b': _jnp.float32, 'conv_ln_g': _jnp.float32, 'conv_ln_b': _jnp.float32, 'w_pw2': _jnp.float32, 'b_pw2': _jnp.float32, 'w_out': _jnp.float32, 'norm_mix_post': _jnp.float32, 'norm_mlp_pre': _jnp.float32, 'w_ff1': _jnp.float32, 'w_ff2': _jnp.float32, 'norm_mlp_post': _jnp.float32}
MOMENT_SCALE = {'norm_mix_pre': 6.019235e-01, 'w_in': 2.631750e-01, 'q_norm': 1.911216e-01, 'w_uq': 9.135913e-02, 'kv_norm': 4.761463e-01, 'w_uk': 9.456724e-02, 'w_uv': 2.066215e-01, 'w_o_attn': 2.074199e-01, 'conv_w': 1.233259e+00, 'conv_b': 2.038837e+01, 'conv_ln_g': 7.358912e+00, 'conv_ln_b': 1.041775e+01, 'w_pw2': 4.709660e+00, 'b_pw2': 2.139310e+01, 'w_out': 4.994501e+00, 'norm_mix_post': 3.295454e+01, 'norm_mlp_pre': 1.976832e+00, 'w_ff1': 9.942266e-01, 'w_ff2': 4.827363e+00, 'norm_mlp_post': 3.359551e+01}


def _to_microbatches(a, axis):
    t = _jnp.moveaxis(a, axis, 0)
    t = t.reshape((N_MICROBATCH, t.shape[0] // N_MICROBATCH) + t.shape[1:])
    return _jnp.moveaxis(t, 1, axis + 1)


def setup_inputs(seed: int = 0) -> dict:
    inp = _fwd_setup_inputs(seed)
    key = _jax.random.fold_in(_jax.random.key(seed), 7919)
    shape, _ = _output_shape()
    out = dict(inp)
    out["loss_target"] = _jax.random.normal(_jax.random.fold_in(key, 0), shape, _jnp.float32)
    for i, name in enumerate(TWIN_WEIGHTS):
        w = inp[name].astype(_jnp.float32)
        if MOMENT_SCALE is None:
            s = _jnp.sqrt(_jnp.mean(_jnp.square(w)) + 1e-30)
        else:
            s = MOMENT_SCALE[name]
        km, kv = _jax.random.split(_jax.random.fold_in(key, i + 1))
        out[name] = w
        out["m_" + name] = s * _jax.random.normal(km, w.shape, _jnp.float32)
        out["v_" + name] = (s * s) * _jax.random.uniform(kv, w.shape, _jnp.float32, 0.5, 1.5)
    if N_MICROBATCH > 1:
        for name, axis in PER_EXAMPLE_BATCH_AXIS.items():
            out[name] = _to_microbatches(out[name], axis)
    return {'x': out['x'], 'positions': out['positions'], 'norm_mix_pre': out['norm_mix_pre'], 'w_in': out['w_in'], 'q_norm': out['q_norm'], 'w_uq': out['w_uq'], 'kv_norm': out['kv_norm'], 'w_uk': out['w_uk'], 'w_uv': out['w_uv'], 'w_o_attn': out['w_o_attn'], 'conv_w': out['conv_w'], 'conv_b': out['conv_b'], 'conv_ln_g': out['conv_ln_g'], 'conv_ln_b': out['conv_ln_b'], 'w_pw2': out['w_pw2'], 'b_pw2': out['b_pw2'], 'w_out': out['w_out'], 'norm_mix_post': out['norm_mix_post'], 'norm_mlp_pre': out['norm_mlp_pre'], 'w_ff1': out['w_ff1'], 'w_ff2': out['w_ff2'], 'norm_mlp_post': out['norm_mlp_post'], 'loss_target': out['loss_target'], 'm_norm_mix_pre': out['m_norm_mix_pre'], 'm_w_in': out['m_w_in'], 'm_q_norm': out['m_q_norm'], 'm_w_uq': out['m_w_uq'], 'm_kv_norm': out['m_kv_norm'], 'm_w_uk': out['m_w_uk'], 'm_w_uv': out['m_w_uv'], 'm_w_o_attn': out['m_w_o_attn'], 'm_conv_w': out['m_conv_w'], 'm_conv_b': out['m_conv_b'], 'm_conv_ln_g': out['m_conv_ln_g'], 'm_conv_ln_b': out['m_conv_ln_b'], 'm_w_pw2': out['m_w_pw2'], 'm_b_pw2': out['m_b_pw2'], 'm_w_out': out['m_w_out'], 'm_norm_mix_post': out['m_norm_mix_post'], 'm_norm_mlp_pre': out['m_norm_mlp_pre'], 'm_w_ff1': out['m_w_ff1'], 'm_w_ff2': out['m_w_ff2'], 'm_norm_mlp_post': out['m_norm_mlp_post'], 'v_norm_mix_pre': out['v_norm_mix_pre'], 'v_w_in': out['v_w_in'], 'v_q_norm': out['v_q_norm'], 'v_w_uq': out['v_w_uq'], 'v_kv_norm': out['v_kv_norm'], 'v_w_uk': out['v_w_uk'], 'v_w_uv': out['v_w_uv'], 'v_w_o_attn': out['v_w_o_attn'], 'v_conv_w': out['v_conv_w'], 'v_conv_b': out['v_conv_b'], 'v_conv_ln_g': out['v_conv_ln_g'], 'v_conv_ln_b': out['v_conv_ln_b'], 'v_w_pw2': out['v_w_pw2'], 'v_b_pw2': out['v_b_pw2'], 'v_w_out': out['v_w_out'], 'v_norm_mix_post': out['v_norm_mix_post'], 'v_norm_mlp_pre': out['v_norm_mlp_pre'], 'v_w_ff1': out['v_w_ff1'], 'v_w_ff2': out['v_w_ff2'], 'v_norm_mlp_post': out['v_norm_mlp_post']}


def _loss(weights, diff, rest, loss_target):
    with _jax.named_scope("forward"):
        args = {**rest, TWIN_DIFF_INPUT: diff, **{k: w.astype(_WEIGHT_DTYPES[k]) for k, w in weights.items()}}
        y = _forward(args)
    with _jax.named_scope("loss_head"):
        err = _jnp.square(y.astype(_jnp.float32) - loss_target)
        return 0.5 * _jnp.sum(_jnp.mean(err, axis=-1)) if err.ndim else 0.5 * err


def _adamw(w, g, m, v):
    m = ADAM_B1 * m + (1.0 - ADAM_B1) * g
    v = ADAM_B2 * v + (1.0 - ADAM_B2) * _jnp.square(g)
    m_hat = m / (1.0 - ADAM_B1 ** ADAM_STEP)
    v_hat = v / (1.0 - ADAM_B2 ** ADAM_STEP)
    delta = -ADAM_LR * (m_hat / (_jnp.sqrt(v_hat) + ADAM_EPS) + ADAM_WD * w)
    return delta, m, v


def reference(x, positions, norm_mix_pre, w_in, q_norm, w_uq, kv_norm, w_uk, w_uv, w_o_attn, conv_w, conv_b, conv_ln_g, conv_ln_b, w_pw2, b_pw2, w_out, norm_mix_post, norm_mlp_pre, w_ff1, w_ff2, norm_mlp_post, loss_target, m_norm_mix_pre, m_w_in, m_q_norm, m_w_uq, m_kv_norm, m_w_uk, m_w_uv, m_w_o_attn, m_conv_w, m_conv_b, m_conv_ln_g, m_conv_ln_b, m_w_pw2, m_b_pw2, m_w_out, m_norm_mix_post, m_norm_mlp_pre, m_w_ff1, m_w_ff2, m_norm_mlp_post, v_norm_mix_pre, v_w_in, v_q_norm, v_w_uq, v_kv_norm, v_w_uk, v_w_uv, v_w_o_attn, v_conv_w, v_conv_b, v_conv_ln_g, v_conv_ln_b, v_w_pw2, v_b_pw2, v_w_out, v_norm_mix_post, v_norm_mlp_pre, v_w_ff1, v_w_ff2, v_norm_mlp_post):
    given = dict(x=x, positions=positions, norm_mix_pre=norm_mix_pre, w_in=w_in, q_norm=q_norm, w_uq=w_uq, kv_norm=kv_norm, w_uk=w_uk, w_uv=w_uv, w_o_attn=w_o_attn, conv_w=conv_w, conv_b=conv_b, conv_ln_g=conv_ln_g, conv_ln_b=conv_ln_b, w_pw2=w_pw2, b_pw2=b_pw2, w_out=w_out, norm_mix_post=norm_mix_post, norm_mlp_pre=norm_mlp_pre, w_ff1=w_ff1, w_ff2=w_ff2, norm_mlp_post=norm_mlp_post, loss_target=loss_target, m_norm_mix_pre=m_norm_mix_pre, m_w_in=m_w_in, m_q_norm=m_q_norm, m_w_uq=m_w_uq, m_kv_norm=m_kv_norm, m_w_uk=m_w_uk, m_w_uv=m_w_uv, m_w_o_attn=m_w_o_attn, m_conv_w=m_conv_w, m_conv_b=m_conv_b, m_conv_ln_g=m_conv_ln_g, m_conv_ln_b=m_conv_ln_b, m_w_pw2=m_w_pw2, m_b_pw2=m_b_pw2, m_w_out=m_w_out, m_norm_mix_post=m_norm_mix_post, m_norm_mlp_pre=m_norm_mlp_pre, m_w_ff1=m_w_ff1, m_w_ff2=m_w_ff2, m_norm_mlp_post=m_norm_mlp_post, v_norm_mix_pre=v_norm_mix_pre, v_w_in=v_w_in, v_q_norm=v_q_norm, v_w_uq=v_w_uq, v_kv_norm=v_kv_norm, v_w_uk=v_w_uk, v_w_uv=v_w_uv, v_w_o_attn=v_w_o_attn, v_conv_w=v_conv_w, v_conv_b=v_conv_b, v_conv_ln_g=v_conv_ln_g, v_conv_ln_b=v_conv_ln_b, v_w_pw2=v_w_pw2, v_b_pw2=v_b_pw2, v_w_out=v_w_out, v_norm_mix_post=v_norm_mix_post, v_norm_mlp_pre=v_norm_mlp_pre, v_w_ff1=v_w_ff1, v_w_ff2=v_w_ff2, v_norm_mlp_post=v_norm_mlp_post)
    weights = {n: given[n] for n in TWIN_WEIGHTS}
    shared = {n: given[n] for n in SHARED_INPUTS}
    per_example = {n: given[n] for n in ['x', 'positions']}
    grad_fn = _jax.value_and_grad(_loss, argnums=(0, 1))

    def one_microbatch(ex, loss_target):
        ex = dict(ex)
        diff = ex.pop(TWIN_DIFF_INPUT)
        return grad_fn(weights, diff, {**shared, **ex}, loss_target)

    if N_MICROBATCH == 1:
        loss, (grad_w, grad_x) = one_microbatch(per_example, given["loss_target"])
    else:
        def body(carry, xs):
            loss_sum, grad_sum = carry
            l_k, (gw_k, gx_k) = one_microbatch(xs[0], xs[1])
            with _jax.named_scope("update"):
                return (loss_sum + l_k, _jax.tree.map(_jnp.add, grad_sum, gw_k)), gx_k

        init = (_jnp.zeros((), _jnp.float32), _jax.tree.map(_jnp.zeros_like, weights))
        (loss, grad_w), grad_x = _jax.lax.scan(body, init, (per_example, given["loss_target"]))
    with _jax.named_scope("update"):
        delta_w, new_m, new_v = {}, {}, {}
        for n in TWIN_WEIGHTS:
            delta_w[n], new_m[n], new_v[n] = _adamw(weights[n], grad_w[n], given["m_" + n], given["v_" + n])
    return (loss, grad_x, *[grad_w[n] for n in TWIN_WEIGHTS], *[delta_w[n] for n in TWIN_WEIGHTS],
            *[new_m[n] for n in TWIN_WEIGHTS], *[new_v[n] for n in TWIN_WEIGHTS])
```

```python
import functools

import numpy as np
import jax
import jax.numpy as jnp
from jax import lax
from jax.experimental import pallas as pl
from jax.experimental.pallas import tpu as pltpu

F32 = jnp.float32
BF16 = jnp.bfloat16
MESH = pl.DeviceIdType.MESH

D_MODEL = 1024
N_HEADS = 8
NOPE = 128
ROPE = 64
QK_PAD = 256
Q_RANK = 384
KV_RANK = 256
CONV_W = 31
CONV_TAPS_PAD = 32
HALO = 32
D_FF = 4096
EPS = 1e-6
ROPE_THETA = 10000.0
SCALE = float((NOPE + ROPE) ** -0.5)
Z_SMALL = 1024
Z_COLS = Z_SMALL + 4 * D_MODEL
N_CHIPS = 4
N_DEV = 8

ADAM_LR = 0.001
ADAM_B1 = 0.9
ADAM_B2 = 0.999
ADAM_EPS = 1e-08
ADAM_WD = 0.01
ADAM_STEP = 10

VMEM_LIMIT = 48 * 1024 * 1024
NEG = -1e30

PACK = (("w_in", 1200), ("w_uq", 144), ("w_uk", 64), ("w_uv", 64), ("w_o_attn", 256),
        ("w_pw2", 256), ("w_out", 256), ("w_ff1", 1024), ("w_ff2", 1024))
PACK_ROWS = 4320
PACK_HALF = PACK_ROWS // 2
ADD_ROWS = 240

SMALL_VECS = ("norm_mix_pre", "conv_b", "conv_ln_g", "conv_ln_b", "b_pw2", "norm_mix_post",
              "norm_mlp_pre", "norm_mlp_post", "q_norm", "kv_norm")
SMALL_LOSS_ROW = 10
SMALL_CONVW_ROW = 16
SMALL_ROWS = 48


def _params(*sem):
    return pltpu.CompilerParams(dimension_semantics=sem, vmem_limit_bytes=VMEM_LIMIT)


def _tile(n, cap):
    if n <= cap:
        return n
    t = (cap // 128) * 128
    while n % t:
        t -= 128
    return t


def _row_tile(n, cap):
    if n <= cap:
        return n
    t = (cap // 8) * 8
    while n % t:
        t -= 8
    return t


def _mm(a, b, *, name, ta=False, tb=False, out_dtype=F32, caps=(512, 1024, 1024)):
    if ta:
        K, M = a.shape
    else:
        M, K = a.shape
    if tb:
        N, K2 = b.shape
    else:
        K2, N = b.shape
    assert K == K2, (a.shape, b.shape, ta, tb)
    tm, tn, tk = _tile(M, caps[0]), _tile(N, caps[1]), _tile(K, caps[2])
    nk = K // tk
    dims = (((0 if ta else 1,), (1 if tb else 0,)), ((), ()))

    def body(a_ref, b_ref, o_ref, acc_ref):
        k = pl.program_id(2)

        @pl.when(k == 0)
        def _():
            acc_ref[...] = jnp.zeros_like(acc_ref)

        acc_ref[...] += lax.dot_general(a_ref[...].astype(BF16), b_ref[...].astype(BF16), dims,
                                        preferred_element_type=F32)

        @pl.when(k == nk - 1)
        def _():
            o_ref[...] = acc_ref[...].astype(out_dtype)

    a_spec = (pl.BlockSpec((tk, tm), lambda i, j, k: (k, i)) if ta
              else pl.BlockSpec((tm, tk), lambda i, j, k: (i, k)))
    b_spec = (pl.BlockSpec((tn, tk), lambda i, j, k: (j, k)) if tb
              else pl.BlockSpec((tk, tn), lambda i, j, k: (k, j)))
    return pl.pallas_call(
        body, name=name,
        out_shape=jax.ShapeDtypeStruct((M, N), out_dtype),
        grid=(M // tm, N // tn, nk),
        in_specs=[a_spec, b_spec],
        out_specs=pl.BlockSpec((tm, tn), lambda i, j, k: (i, j)),
        scratch_shapes=[pltpu.VMEM((tm, tn), F32)],
        compiler_params=_params("parallel", "parallel", "arbitrary"),
    )(a, b)


def _rowwise(fn, rows, consts, outs, reds=(), *, name, tm=256):
    T = rows[0][0].shape[0]
    tm = min(tm, T)
    n_row, n_const, n_out = len(rows), len(consts), len(outs)

    def body(*refs):
        i = pl.program_id(0)
        vals = [r[...] for r in refs[:n_row + n_const]]
        res = fn(*vals)
        out_refs = refs[n_row + n_const:]
        for k in range(n_out):
            out_refs[k][...] = res[k].astype(out_refs[k].dtype)
        for k in range(len(reds)):
            ref = out_refs[n_out + k]

            @pl.when(i == 0)
            def _(ref=ref):
                ref[...] = jnp.zeros_like(ref)

            ref[...] += res[n_out + k]

    in_specs = [pl.BlockSpec((tm, w), functools.partial(lambda i, cb: (i, cb), cb=cb))
                for (_, w, cb) in rows]
    in_specs += [pl.BlockSpec(c.shape, lambda i: (0, 0)) for c in consts]
    out_specs = [pl.BlockSpec((tm, w), lambda i: (i, 0)) for (w, _) in outs]
    out_specs += [pl.BlockSpec(s, lambda i: (0, 0)) for s in reds]
    out_shape = [jax.ShapeDtypeStruct((T, w), dt) for (w, dt) in outs]
    out_shape += [jax.ShapeDtypeStruct(s, F32) for s in reds]
    return pl.pallas_call(
        body, name=name, out_shape=out_shape, grid=(T // tm,),
        in_specs=in_specs, out_specs=out_specs,
        compiler_params=_params("arbitrary"),
    )(*[r[0] for r in rows], *consts)


def _rms(x, g):
    r = lax.rsqrt(jnp.mean(x * x, axis=-1, keepdims=True) + EPS)
    return x * r * g


def _rms_bwd(x, g, dy):
    r = lax.rsqrt(jnp.mean(x * x, axis=-1, keepdims=True) + EPS)
    n = x * r
    dyg = dy * g
    dx = r * (dyg - n * jnp.mean(dyg * n, axis=-1, keepdims=True))
    return dx, jnp.sum(dy * n, axis=0, keepdims=True)


def _sigmoid(x):
    return 1.0 / (1.0 + jnp.exp(-x))


def _rope(x, cs, sa, sb):
    return x * cs + pltpu.roll(x, 32, 1) * sa + pltpu.roll(x, 96, 1) * sb


def _rope_t(dy, cs, sa, sb):
    return dy * cs + pltpu.roll(dy * sa, 96, 1) + pltpu.roll(dy * sb, 32, 1)


def _causal_mask(t):
    row = lax.broadcasted_iota(jnp.int32, (t, t), 0)
    col = lax.broadcasted_iota(jnp.int32, (t, t), 1)
    return col <= row


NT_DIMS = (((1,), (1,)), ((), ()))
TN_DIMS = (((0,), (0,)), ((), ()))


def _flash_fwd(q, k, v, *, tq=256):
    T = q.shape[0]
    tq = min(tq, T)

    def body(q_ref, k_ref, v_ref, o_ref, lse_ref):
        i = pl.program_id(1)
        qv = q_ref[...]

        def step(j, carry, masked):
            m, l, acc = carry
            sl = pl.ds(pl.multiple_of(j * tq, tq), tq)
            s = lax.dot_general(qv, k_ref[sl, :], NT_DIMS, preferred_element_type=F32) * SCALE
            if masked:
                s = jnp.where(_causal_mask(tq), s, NEG)
            m_new = jnp.maximum(m, jnp.max(s, axis=-1, keepdims=True))
            alpha = jnp.exp(m - m_new)
            p = jnp.exp(s - m_new)
            l = alpha * l + jnp.sum(p, axis=-1, keepdims=True)
            acc = alpha * acc + jnp.dot(p.astype(BF16), v_ref[sl, :], preferred_element_type=F32)
            return m_new, l, acc

        init = (jnp.full((tq, 1), NEG, F32), jnp.zeros((tq, 1), F32), jnp.zeros((tq, NOPE), F32))
        carry = lax.fori_loop(0, i, lambda j, c: step(j, c, False), init)
        m, l, acc = step(i, carry, True)
        o_ref[...] = (acc / l).astype(BF16)
        lse_ref[...] = jnp.broadcast_to(m + jnp.log(l), (tq, NOPE))

    return pl.pallas_call(
        body, name="flash_fwd",
        out_shape=[jax.ShapeDtypeStruct((T, N_HEADS * NOPE), BF16),
                   jax.ShapeDtypeStruct((T, N_HEADS * NOPE), F32)],
        grid=(N_HEADS, T // tq),
        in_specs=[pl.BlockSpec((tq, QK_PAD), lambda h, i: (i, h)),
                  pl.BlockSpec((T, QK_PAD), lambda h, i: (0, h)),
                  pl.BlockSpec((T, NOPE), lambda h, i: (0, h))],
        out_specs=[pl.BlockSpec((tq, NOPE), lambda h, i: (i, h)),
                   pl.BlockSpec((tq, NOPE), lambda h, i: (i, h))],
        compiler_params=_params("parallel", "arbitrary"),
    )(q, k, v)


def _flash_bwd(q, k, v, do, lse, delta, *, tq=256):
    T = q.shape[0]
    tq = min(tq, T)
    nq = T // tq

    def body(q_ref, k_ref, v_ref, do_ref, lse_ref, dl_ref, dq_ref, dk_ref, dv_ref):
        j = pl.program_id(1)

        @pl.when(j == 0)
        def _():
            dq_ref[...] = jnp.zeros_like(dq_ref)

        kj = k_ref[...]
        vj = v_ref[...]

        def step(i, carry, masked):
            dk, dv = carry
            sl = pl.ds(pl.multiple_of(i * tq, tq), tq)
            qi = q_ref[sl, :]
            doi = do_ref[sl, :]
            lse_i = lse_ref[sl, :][:, :1]
            dl_i = dl_ref[sl, :][:, :1]
            s = lax.dot_general(qi, kj, NT_DIMS, preferred_element_type=F32) * SCALE
            p = jnp.exp(s - lse_i)
            if masked:
                p = jnp.where(_causal_mask(tq), p, 0.0)
            dv = dv + lax.dot_general(p.astype(BF16), doi, TN_DIMS, preferred_element_type=F32)
            dp = lax.dot_general(doi, vj, NT_DIMS, preferred_element_type=F32)
            ds = (p * (dp - dl_i) * SCALE).astype(BF16)
            dk = dk + lax.dot_general(ds, qi, TN_DIMS, preferred_element_type=F32)
            dq_ref[sl, :] += jnp.dot(ds, kj, preferred_element_type=F32)
            return dk, dv

        carry = step(j, (jnp.zeros((tq, QK_PAD), F32), jnp.zeros((tq, NOPE), F32)), True)
        dk, dv = lax.fori_loop(j + 1, nq, lambda i, c: step(i, c, False), carry)
        dk_ref[...] = dk
        dv_ref[...] = dv

    return pl.pallas_call(
        body, name="flash_bwd",
        out_shape=[jax.ShapeDtypeStruct((T, N_HEADS * QK_PAD), F32),
                   jax.ShapeDtypeStruct((T, N_HEADS * QK_PAD), F32),
                   jax.ShapeDtypeStruct((T, N_HEADS * NOPE), F32)],
        grid=(N_HEADS, nq),
        in_specs=[pl.BlockSpec((T, QK_PAD), lambda h, j: (0, h)),
                  pl.BlockSpec((tq, QK_PAD), lambda h, j: (j, h)),
                  pl.BlockSpec((tq, NOPE), lambda h, j: (j, h)),
                  pl.BlockSpec((T, NOPE), lambda h, j: (0, h)),
                  pl.BlockSpec((T, NOPE), lambda h, j: (0, h)),
                  pl.BlockSpec((T, NOPE), lambda h, j: (0, h))],
        out_specs=[pl.BlockSpec((T, QK_PAD), lambda h, j: (0, h)),
                   pl.BlockSpec((tq, QK_PAD), lambda h, j: (j, h)),
                   pl.BlockSpec((tq, NOPE), lambda h, j: (j, h))],
        compiler_params=_params("arbitrary", "arbitrary"),
    )(q, k, v, do, lse, delta)


def _conv_fwd(u, w, bias, *, tm=256, ct=256):
    T, C = u.shape
    tm = min(tm, T)
    hb = tm // HALO

    def body(cur_ref, halo_ref, w_ref, b_ref, o_ref, buf):
        t = pl.program_id(1)
        buf[0:HALO, :] = jnp.where(t > 0, halo_ref[...], 0.0)
        buf[HALO:, :] = cur_ref[...]
        acc = jnp.broadcast_to(b_ref[...], (tm, ct))
        for kk in range(CONV_W):
            acc = acc + buf[pl.ds(HALO - (CONV_W - 1) + kk, tm), :] * w_ref[kk:kk + 1, :]
        o_ref[...] = acc

    return pl.pallas_call(
        body, name="conv_fwd",
        out_shape=jax.ShapeDtypeStruct((T, C), F32),
        grid=(C // ct, T // tm),
        in_specs=[pl.BlockSpec((tm, ct), lambda c, t: (t, c)),
                  pl.BlockSpec((HALO, ct), lambda c, t: (jnp.maximum(t * hb - 1, 0), c)),
                  pl.BlockSpec((CONV_TAPS_PAD, ct), lambda c, t: (0, c)),
                  pl.BlockSpec((1, ct), lambda c, t: (0, c))],
        out_specs=pl.BlockSpec((tm, ct), lambda c, t: (t, c)),
        scratch_shapes=[pltpu.VMEM((HALO + tm, ct), F32)],
        compiler_params=_params("parallel", "arbitrary"),
    )(u, u, w, bias)


def _conv_bwd(u, du1, w, *, tm=256, ct=256):
    T, C = u.shape
    tm = min(tm, T)
    hb = tm // HALO
    nt = T // tm
    last_halo = T // HALO - 1

    def body(u_ref, uh_ref, d_ref, dh_ref, w_ref, du_ref, dw_ref, ubuf, dbuf):
        t = pl.program_id(1)
        ubuf[0:HALO, :] = jnp.where(t > 0, uh_ref[...], 0.0)
        ubuf[HALO:, :] = u_ref[...]
        dcur = d_ref[...]
        dbuf[0:tm, :] = dcur
        dbuf[tm:, :] = jnp.where(t < nt - 1, dh_ref[...], 0.0)

        @pl.when(t == 0)
        def _():
            dw_ref[...] = jnp.zeros_like(dw_ref)

        acc = jnp.zeros((tm, ct), F32)
        for kk in range(CONV_W):
            acc = acc + dbuf[pl.ds(CONV_W - 1 - kk, tm), :] * w_ref[kk:kk + 1, :]
            dw_ref[kk:kk + 1, :] += jnp.sum(
                dcur * ubuf[pl.ds(HALO - (CONV_W - 1) + kk, tm), :], axis=0, keepdims=True)
        du_ref[...] = acc

    return pl.pallas_call(
        body, name="conv_bwd",
        out_shape=[jax.ShapeDtypeStruct((T, C), F32), jax.ShapeDtypeStruct((CONV_TAPS_PAD, C), F32)],
        grid=(C // ct, nt),
        in_specs=[pl.BlockSpec((tm, ct), lambda c, t: (t, c)),
                  pl.BlockSpec((HALO, ct), lambda c, t: (jnp.maximum(t * hb - 1, 0), c)),
                  pl.BlockSpec((tm, ct), lambda c, t: (t, c)),
                  pl.BlockSpec((HALO, ct), lambda c, t: (jnp.minimum((t + 1) * hb, last_halo), c)),
                  pl.BlockSpec((CONV_TAPS_PAD, ct), lambda c, t: (0, c))],
        out_specs=[pl.BlockSpec((tm, ct), lambda c, t: (t, c)),
                   pl.BlockSpec((CONV_TAPS_PAD, ct), lambda c, t: (0, c))],
        scratch_shapes=[pltpu.VMEM((HALO + tm, ct), F32), pltpu.VMEM((tm + HALO, ct), F32)],
        compiler_params=_params("parallel", "arbitrary"),
    )(u, u, du1, du1, w)


def _adamw(w, g, m, v, *, name):
    R, C = w.shape
    tr = _row_tile(R, max(8, (1 << 18) // C // 8 * 8))

    def body(w_ref, g_ref, m_ref, v_ref, d_ref, nm_ref, nv_ref):
        gv = g_ref[...]
        nm = ADAM_B1 * m_ref[...] + (1.0 - ADAM_B1) * gv
        nv = ADAM_B2 * v_ref[...] + (1.0 - ADAM_B2) * (gv * gv)
        m_hat = nm / (1.0 - ADAM_B1 ** ADAM_STEP)
        v_hat = nv / (1.0 - ADAM_B2 ** ADAM_STEP)
        d_ref[...] = -ADAM_LR * (m_hat / (jnp.sqrt(v_hat) + ADAM_EPS) + ADAM_WD * w_ref[...])
        nm_ref[...] = nm
        nv_ref[...] = nv

    spec = pl.BlockSpec((tr, C), lambda i: (i, 0))
    return pl.pallas_call(
        body, name=name, out_shape=[jax.ShapeDtypeStruct((R, C), F32)] * 3,
        grid=(R // tr,), in_specs=[spec] * 4, out_specs=[spec] * 3,
        compiler_params=_params("parallel"),
    )(w, g, m, v)


ANY = pl.BlockSpec(memory_space=pl.ANY)


def _coords():
    return lax.axis_index("x"), lax.axis_index("y"), lax.axis_index("c")


def _remote(src, dst, ssem, rsem, to):
    return pltpu.make_async_remote_copy(src_ref=src, dst_ref=dst, send_sem=ssem, recv_sem=rsem,
                                        device_id=to, device_id_type=MESH)


def _half(c):
    return pl.ds(pl.multiple_of(c * PACK_HALF, 16), PACK_HALF)


def _gather_weights(wpack, cw):
    def body(w_ref, cw_ref, full_ref, cwf_ref, ssem, rsem, lsem):
        x, y, c = _coords()
        q = 2 * x + y
        sib = (x, y, 1 - c)
        chips = [(1 - x, y), (x, 1 - y), (1 - x, 1 - y)]
        mine, other = _half(c), _half(1 - c)
        own_w = pltpu.make_async_copy(w_ref, full_ref.at[q], lsem.at[0])
        own_c = pltpu.make_async_copy(cw_ref, cwf_ref.at[q], lsem.at[1])
        own_w.start()
        own_c.start()
        first = []
        for j, (px, py) in enumerate(chips):
            first.append(_remote(w_ref.at[mine], full_ref.at[q, mine], ssem.at[j], rsem.at[j], (px, py, c)))
            first.append(_remote(cw_ref, cwf_ref.at[q], ssem.at[6 + j], rsem.at[6 + j], (px, py, c)))
        for cp in first:
            cp.start()
        passed = []
        for j, (px, py) in enumerate(chips):
            pq = 2 * px + py
            _remote(w_ref.at[mine], full_ref.at[pq, mine], ssem.at[j], rsem.at[j], (px, py, c)).wait_recv()
            fw = _remote(full_ref.at[pq, mine], full_ref.at[pq, mine], ssem.at[3 + j], rsem.at[3 + j], sib)
            fw.start()
            passed.append(fw)
        for j, (px, py) in enumerate(chips):
            pq = 2 * px + py
            _remote(full_ref.at[pq, other], full_ref.at[pq, other], ssem.at[3 + j], rsem.at[3 + j], sib).wait_recv()
            _remote(cw_ref, cwf_ref.at[pq], ssem.at[6 + j], rsem.at[6 + j], (px, py, c)).wait_recv()
        for cp in first + passed:
            cp.wait_send()
        own_w.wait()
        own_c.wait()

    return pl.pallas_call(
        body, name="gather_weights",
        out_shape=[jax.ShapeDtypeStruct((N_CHIPS,) + wpack.shape, wpack.dtype),
                   jax.ShapeDtypeStruct((N_CHIPS,) + cw.shape, cw.dtype)],
        in_specs=[ANY, ANY], out_specs=[ANY, ANY],
        scratch_shapes=[pltpu.SemaphoreType.DMA((9,)), pltpu.SemaphoreType.DMA((9,)),
                        pltpu.SemaphoreType.DMA((2,))],
    )(wpack, cw)


def _swap_halves(gpack):
    def body(g_ref, recv_ref, ssem, rsem):
        x, y, c = _coords()
        sib = (x, y, 1 - c)
        cps = [_remote(g_ref.at[j, _half(1 - c)], recv_ref.at[j], ssem.at[j], rsem.at[j], sib)
               for j in range(N_CHIPS)]
        for cp in cps:
            cp.start()
        for cp in cps:
            cp.wait_recv()
        for cp in cps:
            cp.wait_send()

    return pl.pallas_call(
        body, name="grad_swap_halves",
        out_shape=jax.ShapeDtypeStruct((N_CHIPS, PACK_HALF, 1024), F32),
        in_specs=[ANY], out_specs=ANY,
        scratch_shapes=[pltpu.SemaphoreType.DMA((N_CHIPS,)), pltpu.SemaphoreType.DMA((N_CHIPS,))],
    )(gpack)


def _chip_sum(place, gpack, recv1):
    nb = PACK_HALF // ADD_ROWS

    def body(place_ref, g_ref, r_ref, o_ref):
        o_ref[...] = (g_ref[...] + r_ref[...]).astype(BF16)

    return pl.pallas_call(
        body, name="grad_chip_sum",
        out_shape=jax.ShapeDtypeStruct((N_CHIPS, PACK_HALF, 1024), BF16),
        grid_spec=pltpu.PrefetchScalarGridSpec(
            num_scalar_prefetch=1, grid=(N_CHIPS, nb),
            in_specs=[pl.BlockSpec((None, ADD_ROWS, 1024), lambda j, i, s: (j, s[1] * nb + i, 0)),
                      pl.BlockSpec((None, ADD_ROWS, 1024), lambda j, i, s: (j, i, 0))],
            out_specs=pl.BlockSpec((None, ADD_ROWS, 1024), lambda j, i, s: (j, i, 0))),
        compiler_params=_params("parallel", "parallel"),
    )(place, gpack, recv1)


def _exchange_chips(part):
    def body(p_ref, recv_ref, ssem, rsem):
        x, y, c = _coords()
        chips = [(1 - x, y), (x, 1 - y), (1 - x, 1 - y)]
        cps = [_remote(p_ref.at[2 * px + py], recv_ref.at[k], ssem.at[k], rsem.at[k], (px, py, c))
               for k, (px, py) in enumerate(chips)]
        for cp in cps:
            cp.start()
        for cp in cps:
            cp.wait_recv()
        for cp in cps:
            cp.wait_send()

    return pl.pallas_call(
        body, name="grad_exchange_chips",
        out_shape=jax.ShapeDtypeStruct((3, PACK_HALF, 1024), BF16),
        in_specs=[ANY], out_specs=ANY,
        scratch_shapes=[pltpu.SemaphoreType.DMA((3,)), pltpu.SemaphoreType.DMA((3,))],
    )(part)


def _mesh_sum(place, gpack, recv1, recv2):
    nb = PACK_HALF // ADD_ROWS

    def body(place_ref, g_ref, r1_ref, r2_ref, o_ref):
        o_ref[...] = ((g_ref[...] + r1_ref[...]) + r2_ref[0].astype(F32)
                      + r2_ref[1].astype(F32) + r2_ref[2].astype(F32))

    return pl.pallas_call(
        body, name="grad_mesh_sum",
        out_shape=jax.ShapeDtypeStruct((PACK_HALF, 1024), F32),
        grid_spec=pltpu.PrefetchScalarGridSpec(
            num_scalar_prefetch=1, grid=(nb,),
            in_specs=[pl.BlockSpec((None, ADD_ROWS, 1024), lambda i, s: (s[0], s[1] * nb + i, 0)),
                      pl.BlockSpec((None, ADD_ROWS, 1024), lambda i, s: (s[0], i, 0)),
                      pl.BlockSpec((3, ADD_ROWS, 1024), lambda i, s: (0, i, 0))],
            out_specs=pl.BlockSpec((ADD_ROWS, 1024), lambda i, s: (i, 0))),
        compiler_params=_params("parallel"),
    )(place, gpack, recv1, recv2)


def _join_halves(red):
    def body(r_ref, full_ref, ssem, rsem, lsem):
        x, y, c = _coords()
        own = pltpu.make_async_copy(r_ref, full_ref.at[_half(c)], lsem)
        own.start()
        cp = _remote(r_ref, full_ref.at[_half(c)], ssem, rsem, (x, y, 1 - c))
        cp.start()
        _remote(r_ref, full_ref.at[_half(1 - c)], ssem, rsem, (x, y, 1 - c)).wait_recv()
        cp.wait_send()
        own.wait()

    return pl.pallas_call(
        body, name="grad_join_halves",
        out_shape=jax.ShapeDtypeStruct((PACK_ROWS, 1024), F32),
        in_specs=[ANY], out_specs=ANY,
        scratch_shapes=[pltpu.SemaphoreType.DMA, pltpu.SemaphoreType.DMA, pltpu.SemaphoreType.DMA],
    )(red)


def _share_small(svec):
    def body(s_ref, all_ref, ssem, rsem, lsem):
        x, y, c = _coords()
        me = 4 * x + 2 * y + c
        own = pltpu.make_async_copy(s_ref, all_ref.at[me], lsem)
        own.start()
        peers = []
        for r in range(1, N_DEV):
            px = 1 - x if r & 4 else x
            py = 1 - y if r & 2 else y
            pc = 1 - c if r & 1 else c
            peers.append((px, py, pc))
        cps = [_remote(s_ref, all_ref.at[me], ssem.at[k], rsem.at[k], p) for k, p in enumerate(peers)]
        for cp in cps:
            cp.start()
        for k, (px, py, pc) in enumerate(peers):
            _remote(s_ref, all_ref.at[4 * px + 2 * py + pc], ssem.at[k], rsem.at[k], (px, py, pc)).wait_recv()
        for cp in cps:
            cp.wait_send()
        own.wait()

    return pl.pallas_call(
        body, name="share_small",
        out_shape=jax.ShapeDtypeStruct((N_DEV,) + svec.shape, F32),
        in_specs=[ANY], out_specs=ANY,
        scratch_shapes=[pltpu.SemaphoreType.DMA((N_DEV - 1,)), pltpu.SemaphoreType.DMA((N_DEV - 1,)),
                        pltpu.SemaphoreType.DMA],
    )(svec)


def _sum_small(allv):
    def body(a_ref, o_ref):
        acc = a_ref[0]
        for d in range(1, N_DEV):
            acc = acc + a_ref[d]
        o_ref[...] = acc

    return pl.pallas_call(
        body, name="sum_small",
        out_shape=jax.ShapeDtypeStruct(allv.shape[1:], F32),
    )(allv)


def _unpack_full(full):
    out, off = {}, 0
    for name, rows in PACK:
        out[name] = full[:, off:off + rows]
        off += rows
    w = {}
    w["w_in"] = out["w_in"].reshape(4, 1024, 1200).transpose(1, 0, 2).reshape(1024, 4800)
    w["w_uq"] = out["w_uq"].reshape(4, 384, 384).transpose(1, 0, 2).reshape(384, 1536)
    w["w_uk"] = out["w_uk"].reshape(256, 1024)
    w["w_uv"] = out["w_uv"].reshape(256, 1024)
    for n in ("w_o_attn", "w_pw2", "w_out"):
        w[n] = out[n].reshape(1024, 1024)
    w["w_ff1"] = out["w_ff1"].transpose(1, 0, 2).reshape(1024, 4096)
    w["w_ff2"] = out["w_ff2"].reshape(4096, 1024)
    return w


def _pack_grads(g):
    parts = [
        g["w_in"].reshape(1024, 4, 1200).transpose(1, 0, 2).reshape(4, 1200, 1024),
        g["w_uq"].reshape(384, 4, 384).transpose(1, 0, 2).reshape(4, 144, 1024),
        g["w_uk"].reshape(4, 64, 1024),
        g["w_uv"].reshape(4, 64, 1024),
        g["w_o_attn"].reshape(4, 256, 1024),
        g["w_pw2"].reshape(4, 256, 1024),
        g["w_out"].reshape(4, 256, 1024),
        g["w_ff1"].reshape(1024, 4, 1024).transpose(1, 0, 2),
        g["w_ff2"].reshape(4, 1024, 1024),
    ]
    used = sum(r for _, r in PACK)
    parts.append(jnp.zeros((4, PACK_ROWS - used, 1024), F32))
    return jnp.concatenate(parts, axis=1)


def _pad_w_in(w):
    return jnp.concatenate([w[:, :704], jnp.zeros((1024, Z_SMALL - 704), w.dtype), w[:, 704:]], axis=1)


def _pad_w_uq(w):
    w = w.reshape(Q_RANK, N_HEADS, NOPE + ROPE)
    return jnp.pad(w, ((0, 0), (0, 0), (0, QK_PAD - NOPE - ROPE))).reshape(Q_RANK, N_HEADS * QK_PAD)


def _local_step(xs, pos, tgt, W, cw, vec):
    D = D_MODEL
    w_in = _pad_w_in(W["w_in"])
    w_uq = _pad_w_uq(W["w_uq"])
    w_ukv = jnp.concatenate([W["w_uk"], W["w_uv"]], axis=1)
    inv = ROPE_THETA ** (-jnp.arange(0, ROPE, 2, dtype=F32) / ROPE)
    inv_freq = jnp.concatenate([inv, inv, jnp.zeros((128 - ROPE,), F32)])[None, :]

    (h,) = _rowwise(lambda x, g: (_rms(x, g),), [(xs, D, 0)], [vec["norm_mix_pre"]], [(D, BF16)],
                    name="rms_pre")
    z = _mm(h, w_in, name="mm_in")

    def qkv_prep(zs, p, qn, kvn, invf):
        ang = p.astype(F32) * invf
        cosv, sinv = jnp.cos(ang), jnp.sin(ang)
        ln = lax.broadcasted_iota(jnp.int32, ang.shape, 1)
        cs = jnp.where(ln < ROPE, cosv, 0.0)
        sa = jnp.where((ln >= ROPE // 2) & (ln < ROPE), sinv, 0.0)
        sb = jnp.where(ln < ROPE // 2, -sinv, 0.0)
        cqn = _rms(zs[:, :Q_RANK], qn)
        ckvn = _rms(zs[:, Q_RANK:Q_RANK + KV_RANK], kvn)
        kr = _rope(zs[:, 640:768], cs, sa, sb)
        return cqn, ckvn, kr, cs, sa, sb

    cqn, ckvn, krot, cs, sa, sb = _rowwise(
        qkv_prep, [(z, Z_SMALL, 0), (pos, 1, 0)], [vec["q_norm"], vec["kv_norm"], inv_freq],
        [(Q_RANK, BF16), (KV_RANK, BF16), (128, BF16), (128, F32), (128, F32), (128, F32)],
        name="qkv_prep")
    q_pre = _mm(cqn, w_uq, name="mm_uq")
    kv = _mm(ckvn, w_ukv, name="mm_ukv")

    def qk_rope(qp, kn, vv, kr, cs_, sa_, sb_):
        qs, ks = [], []
        for hd in range(N_HEADS):
            qs.append(qp[:, hd * QK_PAD:hd * QK_PAD + NOPE])
            qs.append(_rope(qp[:, hd * QK_PAD + NOPE:(hd + 1) * QK_PAD], cs_, sa_, sb_))
            ks.append(kn[:, hd * NOPE:(hd + 1) * NOPE])
            ks.append(kr.astype(F32))
        return jnp.concatenate(qs, axis=1), jnp.concatenate(ks, axis=1), vv

    q, k, v = _rowwise(
        qk_rope, [(q_pre, 2048, 0), (kv, 1024, 0), (kv, 1024, 1), (krot, 128, 0), (cs, 128, 0),
                  (sa, 128, 0), (sb, 128, 0)], [],
        [(2048, BF16), (2048, BF16), (1024, BF16)], name="qk_rope")
    attn, lse = _flash_fwd(q, k, v)
    y_attn = _mm(attn, W["w_o_attn"], name="mm_o_attn")

    (u0,) = _rowwise(lambda a, b: (a * _sigmoid(b),), [(z, D, 1), (z, D, 2)], [], [(D, F32)], name="glu")
    u1 = _conv_fwd(u0, cw, vec["conv_b"])

    def ln_parts(u, g, b):
        mu = jnp.mean(u, axis=-1, keepdims=True)
        xc = u - mu
        rs = lax.rsqrt(jnp.mean(xc * xc, axis=-1, keepdims=True) + EPS)
        nh = xc * rs
        return nh, rs, nh * g + b

    def ln_silu(u, g, b):
        y = ln_parts(u, g, b)[2]
        return (y * _sigmoid(y),)

    (u3,) = _rowwise(ln_silu, [(u1, D, 0)], [vec["conv_ln_g"], vec["conv_ln_b"]], [(D, BF16)],
                     name="ln_silu")
    y_conv = _mm(u3, W["w_pw2"], name="mm_pw2")

    def merge(ga, gc, ya, yc, b2):
        return (_sigmoid(ga) * ya + _sigmoid(gc) * (yc + b2),)

    (merged,) = _rowwise(merge, [(z, D, 3), (z, D, 4), (y_attn, D, 0), (y_conv, D, 0)], [vec["b_pw2"]],
                         [(D, BF16)], name="merge")
    mo = _mm(merged, W["w_out"], name="mm_out")

    def post1(x, m, g1, g2):
        x1 = x + _rms(m, g1)
        return x1, _rms(x1, g2)

    x1, h2 = _rowwise(post1, [(xs, D, 0), (mo, D, 0)], [vec["norm_mix_post"], vec["norm_mlp_pre"]],
                      [(D, F32), (D, BF16)], name="post_mix")
    f1 = _mm(h2, W["w_ff1"], name="mm_ff1")

    def sqrelu(f):
        r = jnp.maximum(f, 0.0)
        return (r * r,)

    (a2,) = _rowwise(sqrelu, [(f1, D_FF, 0)], [], [(D_FF, BF16)], name="sqrelu", tm=128)
    f = _mm(a2, W["w_ff2"], name="mm_ff2")

    def post2(x1_, f_, t_, g):
        e = x1_ + _rms(f_, g) - t_
        dy = e * (1.0 / D)
        df, dg = _rms_bwd(f_, g, dy)
        loss = jnp.broadcast_to(jnp.sum(e * e, keepdims=True) * (0.5 / D), (1, 128))
        return dy, df, dg, loss

    dy, df, g_norm_mlp_post, loss = _rowwise(
        post2, [(x1, D, 0), (f, D, 0), (tgt, D, 0)], [vec["norm_mlp_post"]],
        [(D, F32), (D, BF16)], [(1, D), (1, 128)], name="post_mlp_loss")
    g_w_ff2 = _mm(a2, df, ta=True, name="mm_dw_ff2")
    da2 = _mm(df, W["w_ff2"], tb=True, name="mm_da2")

    def dsqrelu(d, f_):
        return (d * (2.0 * jnp.maximum(f_, 0.0)),)

    (df1,) = _rowwise(dsqrelu, [(da2, D_FF, 0), (f1, D_FF, 0)], [], [(D_FF, BF16)], name="dsqrelu", tm=128)
    g_w_ff1 = _mm(h2, df1, ta=True, name="mm_dw_ff1")
    dh2 = _mm(df1, W["w_ff1"], tb=True, name="mm_dh2")

    def bwd_mid(dy_, dh2_, x1_, m, g2, g1):
        d1, dg2 = _rms_bwd(x1_, g2, dh2_)
        dx1_ = dy_ + d1
        dm, dg1 = _rms_bwd(m, g1, dx1_)
        return dx1_, dm, dg2, dg1

    dx1, dmo, g_norm_mlp_pre, g_norm_mix_post = _rowwise(
        bwd_mid, [(dy, D, 0), (dh2, D, 0), (x1, D, 0), (mo, D, 0)],
        [vec["norm_mlp_pre"], vec["norm_mix_post"]], [(D, F32), (D, BF16)], [(1, D), (1, D)],
        name="bwd_mid")
    g_w_out = _mm(merged, dmo, ta=True, name="mm_dw_out")
    dmerged = _mm(dmo, W["w_out"], tb=True, name="mm_dmerged")

    def dmerge(dm, ga, gc, ya, yc, b2):
        sga, sgc = _sigmoid(ga), _sigmoid(gc)
        dya = dm * sga
        dyc = dm * sgc
        dga = dm * ya * sga * (1.0 - sga)
        dgc = dm * (yc + b2) * sgc * (1.0 - sgc)
        return dya, dyc, dga, dgc, jnp.sum(dyc, axis=0, keepdims=True)

    dya, dyc, dga, dgc, g_b_pw2 = _rowwise(
        dmerge, [(dmerged, D, 0), (z, D, 3), (z, D, 4), (y_attn, D, 0), (y_conv, D, 0)], [vec["b_pw2"]],
        [(D, BF16), (D, BF16), (D, BF16), (D, BF16)], [(1, D)], name="dmerge")

    g_w_o = _mm(attn, dya, ta=True, name="mm_dw_o_attn")
    dattn = _mm(dya, W["w_o_attn"], tb=True, out_dtype=BF16, name="mm_dattn")

    def delta_fn(do_, o_):
        pr = do_.astype(F32) * o_.astype(F32)
        return (jnp.concatenate(
            [jnp.broadcast_to(jnp.sum(pr[:, hd * NOPE:(hd + 1) * NOPE], axis=-1, keepdims=True),
                              (pr.shape[0], NOPE)) for hd in range(N_HEADS)], axis=1),)

    (delta,) = _rowwise(delta_fn, [(dattn, D, 0), (attn, D, 0)], [], [(D, F32)], name="attn_delta")
    dq, dk, dv = _flash_bwd(q, k, v, dattn, lse, delta)

    def unrope(dq_, dk_, dv_, cs_, sa_, sb_):
        qs, kn = [], []
        dkr = jnp.zeros_like(cs_)
        for hd in range(N_HEADS):
            qs.append(dq_[:, hd * QK_PAD:hd * QK_PAD + NOPE])
            qs.append(_rope_t(dq_[:, hd * QK_PAD + NOPE:(hd + 1) * QK_PAD], cs_, sa_, sb_))
            kn.append(dk_[:, hd * QK_PAD:hd * QK_PAD + NOPE])
            dkr = dkr + dk_[:, hd * QK_PAD + NOPE:(hd + 1) * QK_PAD]
        return jnp.concatenate(qs, axis=1), jnp.concatenate(kn + [dv_], axis=1), _rope_t(dkr, cs_, sa_, sb_)

    dq_pre, dkv, dkr = _rowwise(
        unrope, [(dq, 2048, 0), (dk, 2048, 0), (dv, 1024, 0), (cs, 128, 0), (sa, 128, 0), (sb, 128, 0)], [],
        [(2048, BF16), (2048, BF16), (128, F32)], name="unrope")
    g_w_uq = _mm(cqn, dq_pre, ta=True, name="mm_dw_uq")
    dcqn = _mm(dq_pre, w_uq, tb=True, name="mm_dcqn")
    g_w_ukv = _mm(ckvn, dkv, ta=True, name="mm_dw_ukv")
    dckvn = _mm(dkv, w_ukv, tb=True, name="mm_dckvn")

    def small_bwd(zs, dcq_, dckv_, dkr_, qn, kvn):
        dcq, dqn = _rms_bwd(zs[:, :Q_RANK], qn, dcq_)
        dckv, dkvn = _rms_bwd(zs[:, Q_RANK:Q_RANK + KV_RANK], kvn, dckv_)
        pad = jnp.zeros((zs.shape[0], Z_SMALL - 768), F32)
        return jnp.concatenate([dcq, dckv, dkr_, pad], axis=1), dqn, dkvn

    dz_small, g_q_norm, g_kv_norm = _rowwise(
        small_bwd, [(z, Z_SMALL, 0), (dcqn, Q_RANK, 0), (dckvn, KV_RANK, 0), (dkr, 128, 0)],
        [vec["q_norm"], vec["kv_norm"]], [(Z_SMALL, BF16)], [(1, Q_RANK), (1, KV_RANK)], name="small_bwd")

    g_w_pw2 = _mm(u3, dyc, ta=True, name="mm_dw_pw2")
    du3 = _mm(dyc, W["w_pw2"], tb=True, name="mm_du3")

    def ln_silu_bwd(d3, u, g, b):
        nh, rs, y = ln_parts(u, g, b)
        sg = _sigmoid(y)
        dyv = d3 * (sg * (1.0 + y * (1.0 - sg)))
        dnh = dyv * g
        du = rs * (dnh - jnp.mean(dnh, axis=-1, keepdims=True)
                   - nh * jnp.mean(dnh * nh, axis=-1, keepdims=True))
        return (du, jnp.sum(dyv * nh, axis=0, keepdims=True), jnp.sum(dyv, axis=0, keepdims=True),
                jnp.sum(du, axis=0, keepdims=True))

    du1, g_ln_g, g_ln_b, g_conv_b = _rowwise(
        ln_silu_bwd, [(du3, D, 0), (u1, D, 0)], [vec["conv_ln_g"], vec["conv_ln_b"]], [(D, F32)],
        [(1, D), (1, D), (1, D)], name="ln_silu_bwd")
    du0, g_conv_w = _conv_bwd(u0, du1, cw)

    def glu_bwd(d0, a, b):
        sg = _sigmoid(b)
        return d0 * sg, d0 * a * sg * (1.0 - sg)

    dza, dzb = _rowwise(glu_bwd, [(du0, D, 0), (z, D, 1), (z, D, 2)], [], [(D, BF16), (D, BF16)],
                        name="glu_bwd")

    dz = jnp.concatenate([dz_small, dza, dzb, dga, dgc], axis=1)
    g_w_in = _mm(h, dz, ta=True, name="mm_dw_in")
    dh = _mm(dz, w_in, tb=True, name="mm_dh")

    def final(dx1_, dh_, x, g):
        d, dg = _rms_bwd(x, g, dh_)
        return dx1_ + d, dg

    grad_x, g_norm_mix_pre = _rowwise(final, [(dx1, D, 0), (dh, D, 0), (xs, D, 0)], [vec["norm_mix_pre"]],
                                      [(D, F32)], [(1, D)], name="bwd_final")

    gw = {
        "w_in": jnp.concatenate([g_w_in[:, :704], g_w_in[:, Z_SMALL:]], axis=1),
        "w_uq": g_w_uq.reshape(Q_RANK, N_HEADS, QK_PAD)[:, :, :NOPE + ROPE].reshape(Q_RANK, 1536),
        "w_uk": g_w_ukv[:, :1024], "w_uv": g_w_ukv[:, 1024:],
        "w_o_attn": g_w_o, "w_pw2": g_w_pw2, "w_out": g_w_out, "w_ff1": g_w_ff1, "w_ff2": g_w_ff2,
    }
    gs = {
        "norm_mix_pre": g_norm_mix_pre, "conv_b": g_conv_b, "conv_ln_g": g_ln_g, "conv_ln_b": g_ln_b,
        "b_pw2": g_b_pw2, "norm_mix_post": g_norm_mix_post, "norm_mlp_pre": g_norm_mlp_pre,
        "norm_mlp_post": g_norm_mlp_post, "q_norm": g_q_norm, "kv_norm": g_kv_norm,
    }
    return grad_x, gw, gs, g_conv_w, loss


def _row1024(a):
    return jnp.pad(a, ((0, 0), (0, 1024 - a.shape[1])))


BIG = ("w_in", "w_uq", "w_uk", "w_uv", "w_o_attn", "w_pw2", "w_out", "w_ff1", "w_ff2")
ORDER = ("norm_mix_pre", "w_in", "q_norm", "w_uq", "kv_norm", "w_uk", "w_uv", "w_o_attn", "conv_w",
         "conv_b", "conv_ln_g", "conv_ln_b", "w_pw2", "b_pw2", "w_out", "norm_mix_post", "norm_mlp_pre",
         "w_ff1", "w_ff2", "norm_mlp_post")


def kernel(x, positions, norm_mix_pre, w_in, q_norm, w_uq, kv_norm, w_uk, w_uv, w_o_attn, conv_w, conv_b, conv_ln_g, conv_ln_b, w_pw2, b_pw2, w_out, norm_mix_post, norm_mlp_pre, w_ff1, w_ff2, norm_mlp_post, loss_target, m_norm_mix_pre, m_w_in, m_q_norm, m_w_uq, m_kv_norm, m_w_uk, m_w_uv, m_w_o_attn, m_conv_w, m_conv_b, m_conv_ln_g, m_conv_ln_b, m_w_pw2, m_b_pw2, m_w_out, m_norm_mix_post, m_norm_mlp_pre, m_w_ff1, m_w_ff2, m_norm_mlp_post, v_norm_mix_pre, v_w_in, v_q_norm, v_w_uq, v_kv_norm, v_w_uk, v_w_uv, v_w_o_attn, v_conv_w, v_conv_b, v_conv_ln_g, v_conv_ln_b, v_w_pw2, v_b_pw2, v_w_out, v_norm_mix_post, v_norm_mlp_pre, v_w_ff1, v_w_ff2, v_norm_mlp_post):
    args = dict(locals())
    wts = {n: args[n] for n in ORDER}
    mom = {n: args["m_" + n] for n in ORDER}
    var = {n: args["v_" + n] for n in ORDER}
    T = x.shape[1]
    place = jnp.stack([2 * lax.axis_index("x") + lax.axis_index("y"), lax.axis_index("c")]).astype(jnp.int32)

    used = sum(r for _, r in PACK)
    shards = [wts[n].astype(BF16).reshape(-1, 1024) for n in BIG]
    wpack = jnp.concatenate(shards + [jnp.zeros((PACK_ROWS - used, 1024), BF16)], axis=0)
    cw_own = jnp.pad(conv_w.reshape(CONV_W, -1), ((0, CONV_TAPS_PAD - CONV_W), (0, 0)))
    full, cw_all = _gather_weights(wpack, cw_own)
    W = _unpack_full(full)
    cw = cw_all.transpose(1, 0, 2).reshape(CONV_TAPS_PAD, D_MODEL)
    vec = {n: wts[n] for n in SMALL_VECS}

    grad_x, gw, gs, g_conv_w, loss = _local_step(x[0], positions.reshape(T, 1), loss_target[0], W, cw, vec)

    gpack = _pack_grads(gw)
    recv1 = _swap_halves(gpack)
    part = _chip_sum(place, gpack, recv1)
    recv2 = _exchange_chips(part)
    red = _mesh_sum(place, gpack, recv1, recv2)
    gfull = _join_halves(red)

    svec = jnp.concatenate(
        [_row1024(gs[n]) for n in SMALL_VECS] + [_row1024(loss)]
        + [jnp.zeros((SMALL_CONVW_ROW - SMALL_LOSS_ROW - 1, 1024), F32), g_conv_w], axis=0)
    ssum = _sum_small(_share_small(svec))

    grads, off = {}, 0
    for name, rows in PACK:
        grads[name] = gfull[off:off + rows].reshape(wts[name].shape)
        off += rows
    for r, n in enumerate(SMALL_VECS):
        grads[n] = ssum[r:r + 1, :wts[n].shape[1]]
    cw_sum = ssum[SMALL_CONVW_ROW:SMALL_CONVW_ROW + CONV_W]
    grads["conv_w"] = lax.dynamic_slice_in_dim(cw_sum, place[0] * 256, 256, axis=1).reshape(conv_w.shape)
    loss_out = ssum[SMALL_LOSS_ROW, 0]

    delta, new_m, new_v = {}, {}, {}

    def as2d(a):
        return a.reshape(-1, a.shape[-1]) if a.ndim != 4 or a.shape[2] == 1 else a.reshape(a.shape[1], -1)

    for n in BIG + ("conv_w",):
        shp = wts[n].shape
        outs = _adamw(as2d(wts[n]), as2d(grads[n]), as2d(mom[n]), as2d(var[n]), name="adamw_" + n)
        delta[n], new_m[n], new_v[n] = (o.reshape(shp) for o in outs)
    small = [jnp.concatenate([_row1024(t[n]) for n in SMALL_VECS], axis=0) for t in (wts, grads, mom, var)]
    outs = _adamw(*small, name="adamw_small")
    for r, n in enumerate(SMALL_VECS):
        delta[n], new_m[n], new_v[n] = (o[r:r + 1, :wts[n].shape[1]] for o in outs)

    return (loss_out, grad_x[None], *[grads[n] for n in ORDER], *[delta[n] for n in ORDER],
            *[new_m[n] for n in ORDER], *[new_v[n] for n in ORDER])
```

```python
import functools

import numpy as np
import jax
import jax.numpy as jnp
from jax import lax
from jax.experimental import pallas as pl
from jax.experimental.pallas import tpu as pltpu

F32 = jnp.float32
BF16 = jnp.bfloat16
MESH = pl.DeviceIdType.MESH

D_MODEL = 1024
N_HEADS = 8
NOPE = 128
ROPE = 64
QK_PAD = 256
Q_RANK = 384
KV_RANK = 256
CONV_W = 31
CONV_TAPS_PAD = 32
HALO = 32
D_FF = 4096
EPS = 1e-6
ROPE_THETA = 10000.0
SCALE = float((NOPE + ROPE) ** -0.5)
Q_SCALE = SCALE * float(np.log2(np.e))
K_UNSCALE = float(np.log(2.0))
Z_SMALL = 1024
Z_COLS = Z_SMALL + 4 * D_MODEL
N_CHIPS = 4
N_DEV = 8

ADAM_LR = 0.001
ADAM_B1 = 0.9
ADAM_B2 = 0.999
ADAM_EPS = 1e-08
ADAM_WD = 0.01
ADAM_STEP = 10

VMEM_LIMIT = 48 * 1024 * 1024
NEG = -1e30

PACK = (("w_in", 1200), ("w_uq", 144), ("w_uk", 64), ("w_uv", 64), ("w_o_attn", 256),
        ("w_pw2", 256), ("w_out", 256), ("w_ff1", 1024), ("w_ff2", 1024))
PACK_ROWS = 4320
PACK_HALF = PACK_ROWS // 2
ADD_ROWS = 240

SMALL_VECS = ("norm_mix_pre", "conv_b", "conv_ln_g", "conv_ln_b", "b_pw2", "norm_mix_post",
              "norm_mlp_pre", "norm_mlp_post", "q_norm", "kv_norm")
SMALL_LOSS_ROW = 10
SMALL_CONVW_ROW = 16
SMALL_ROWS = 48


def _params(*sem):
    return pltpu.CompilerParams(dimension_semantics=sem, vmem_limit_bytes=VMEM_LIMIT)


def _tile(n, cap):
    if n <= cap:
        return n
    t = (cap // 128) * 128
    while n % t:
        t -= 128
    return t


def _row_tile(n, cap):
    if n <= cap:
        return n
    t = (cap // 8) * 8
    while n % t:
        t -= 8
    return t


def _mm(a, b, *, name, ta=False, tb=False, out_dtype=F32, caps=(1024, 1024, 1024), extras=(),
        epilogue=None, out_dtypes=None):
    if ta:
        K, M = a.shape
    else:
        M, K = a.shape
    if tb:
        N, K2 = b.shape
    else:
        K2, N = b.shape
    assert K == K2, (a.shape, b.shape, ta, tb)
    tm, tn, tk = _tile(M, caps[0]), _tile(N, caps[1]), _tile(K, caps[2])
    nk = K // tk
    dims = (((0 if ta else 1,), (1 if tb else 0,)), ((), ()))
    out_dtypes = tuple(out_dtypes or (out_dtype,))
    n_ex, n_out = len(extras), len(out_dtypes)

    def body(*refs):
        a_ref, b_ref = refs[:2]
        ex_refs = refs[2:2 + n_ex]
        o_refs = refs[2 + n_ex:2 + n_ex + n_out]
        k = pl.program_id(2)
        part = lax.dot_general(a_ref[...].astype(BF16), b_ref[...].astype(BF16), dims,
                               preferred_element_type=F32)

        def finish(acc):
            res = epilogue(acc, *[r[...] for r in ex_refs]) if epilogue else (acc,)
            for o_ref, r in zip(o_refs, res):
                o_ref[...] = r.astype(o_ref.dtype)

        if nk == 1:
            finish(part)
        else:
            acc_ref = refs[-1]

            @pl.when(k == 0)
            def _():
                acc_ref[...] = part

            @pl.when((k > 0) & (k < nk - 1))
            def _():
                acc_ref[...] += part

            @pl.when(k == nk - 1)
            def _():
                finish(acc_ref[...] + part)

    a_spec = (pl.BlockSpec((tk, tm), lambda i, j, k: (k, i)) if ta
              else pl.BlockSpec((tm, tk), lambda i, j, k: (i, k)))
    b_spec = (pl.BlockSpec((tn, tk), lambda i, j, k: (j, k)) if tb
              else pl.BlockSpec((tk, tn), lambda i, j, k: (k, j)))
    o_spec = pl.BlockSpec((tm, tn), lambda i, j, k: (i, j))
    res = pl.pallas_call(
        body, name=name,
        out_shape=[jax.ShapeDtypeStruct((M, N), dt) for dt in out_dtypes],
        grid=(M // tm, N // tn, nk),
        in_specs=[a_spec, b_spec] + [o_spec] * n_ex,
        out_specs=[o_spec] * n_out,
        scratch_shapes=[pltpu.VMEM((tm, tn), F32)] if nk > 1 else [],
        compiler_params=_params("parallel", "parallel", "arbitrary"),
    )(a, b, *extras)
    return res[0] if n_out == 1 else res


def _rowwise(fn, rows, consts, outs, reds=(), *, name, tm=256):
    T = rows[0][0].shape[0]
    tm = min(tm, T)
    n_row, n_const, n_out = len(rows), len(consts), len(outs)

    def body(*refs):
        i = pl.program_id(0)
        vals = [r[...] for r in refs[:n_row + n_const]]
        res = fn(*vals)
        out_refs = refs[n_row + n_const:]
        for k in range(n_out):
            out_refs[k][...] = res[k].astype(out_refs[k].dtype)
        for k in range(len(reds)):
            ref = out_refs[n_out + k]

            @pl.when(i == 0)
            def _(ref=ref):
                ref[...] = jnp.zeros_like(ref)

            ref[...] += res[n_out + k]

    in_specs = [pl.BlockSpec((tm, w), functools.partial(lambda i, cb: (i, cb), cb=cb))
                for (_, w, cb) in rows]
    in_specs += [pl.BlockSpec(c.shape, lambda i: (0, 0)) for c in consts]
    out_specs = [pl.BlockSpec((tm, w), lambda i: (i, 0)) for (w, _) in outs]
    out_specs += [pl.BlockSpec(s, lambda i: (0, 0)) for s in reds]
    out_shape = [jax.ShapeDtypeStruct((T, w), dt) for (w, dt) in outs]
    out_shape += [jax.ShapeDtypeStruct(s, F32) for s in reds]
    return pl.pallas_call(
        body, name=name, out_shape=out_shape, grid=(T // tm,),
        in_specs=in_specs, out_specs=out_specs,
        compiler_params=_params("arbitrary"),
    )(*[r[0] for r in rows], *consts)


def _rms(x, g):
    r = lax.rsqrt(jnp.mean(x * x, axis=-1, keepdims=True) + EPS)
    return x * r * g


def _rms_bwd(x, g, dy):
    r = lax.rsqrt(jnp.mean(x * x, axis=-1, keepdims=True) + EPS)
    n = x * r
    dyg = dy * g
    dx = r * (dyg - n * jnp.mean(dyg * n, axis=-1, keepdims=True))
    return dx, jnp.sum(dy * n, axis=0, keepdims=True)


def _sigmoid(x):
    return 1.0 / (1.0 + jnp.exp(-x))


def _rope(x, cs, sa, sb):
    return x * cs + pltpu.roll(x, 32, 1) * sa + pltpu.roll(x, 96, 1) * sb


def _rope_t(dy, cs, sa, sb):
    return dy * cs + pltpu.roll(dy * sa, 96, 1) + pltpu.roll(dy * sb, 32, 1)


def _causal_mask(t):
    row = lax.broadcasted_iota(jnp.int32, (t, t), 0)
    col = lax.broadcasted_iota(jnp.int32, (t, t), 1)
    return col <= row


NT_DIMS = (((1,), (1,)), ((), ()))
TN_DIMS = (((0,), (0,)), ((), ()))


def _flash_fwd(q, k, v, *, tq=512):
    T = q.shape[0]
    tq = min(tq, T)

    def body(q_ref, k_ref, v_ref, o_ref, lse_ref):
        i = pl.program_id(1)
        qv = q_ref[...]

        def step(j, carry, masked):
            m, acc = carry
            sl = pl.ds(pl.multiple_of(j * tq, tq), tq)
            s = lax.dot_general(qv, k_ref[sl, :], NT_DIMS, preferred_element_type=F32)
            if masked:
                s = jnp.where(_causal_mask(tq), s, NEG)
            m_new = jnp.maximum(m, jnp.max(s, axis=-1, keepdims=True))
            p = jnp.exp2(s - m_new)
            acc = jnp.exp2(m - m_new) * acc + jnp.dot(p.astype(BF16), v_ref[sl, :],
                                                      preferred_element_type=F32)
            return m_new, acc

        init = (jnp.full((tq, 1), NEG, F32), jnp.zeros((tq, QK_PAD), F32))
        carry = lax.fori_loop(0, i, lambda j, c: step(j, c, False), init)
        m, acc = step(i, carry, True)
        l = acc[:, NOPE:]
        o_ref[...] = (acc[:, :NOPE] / l).astype(BF16)
        lse_ref[...] = m + jnp.log2(l)

    return pl.pallas_call(
        body, name="flash_fwd",
        out_shape=[jax.ShapeDtypeStruct((T, N_HEADS * NOPE), BF16),
                   jax.ShapeDtypeStruct((T, N_HEADS * NOPE), F32)],
        grid=(N_HEADS, T // tq),
        in_specs=[pl.BlockSpec((tq, QK_PAD), lambda h, i: (i, h)),
                  pl.BlockSpec((T, QK_PAD), lambda h, i: (0, h)),
                  pl.BlockSpec((T, QK_PAD), lambda h, i: (0, h))],
        out_specs=[pl.BlockSpec((tq, NOPE), lambda h, i: (i, h)),
                   pl.BlockSpec((tq, NOPE), lambda h, i: (i, h))],
        compiler_params=_params("parallel", "arbitrary"),
    )(q, k, v)


def _flash_bwd(q, k, v, do, lse, *, tq=512):
    T = q.shape[0]
    tq = min(tq, T)
    nq = T // tq

    def body(q_ref, k_ref, v_ref, do_ref, lse_ref, dq_ref, dk_ref, dv_ref):
        j = pl.program_id(1)

        @pl.when(j == 0)
        def _():
            dq_ref[...] = jnp.zeros_like(dq_ref)

        kj = k_ref[...]
        vj = v_ref[...]

        def step(i, carry, masked):
            dk, dv = carry
            sl = pl.ds(pl.multiple_of(i * tq, tq), tq)
            qi = q_ref[sl, :]
            doi = do_ref[sl, :]
            s = lax.dot_general(qi, kj, NT_DIMS, preferred_element_type=F32)
            p = jnp.exp2(s - lse_ref[sl, :][:, :1])
            if masked:
                p = jnp.where(_causal_mask(tq), p, 0.0)
            dv = dv + lax.dot_general(p.astype(BF16), doi, TN_DIMS, preferred_element_type=F32)
            ds = (p * lax.dot_general(doi, vj, NT_DIMS, preferred_element_type=F32)).astype(BF16)
            dk = dk + lax.dot_general(ds, qi, TN_DIMS, preferred_element_type=F32)
            dq_ref[sl, :] += jnp.dot(ds, kj, preferred_element_type=F32)
            return dk, dv

        zero = jnp.zeros((tq, QK_PAD), F32)
        carry = step(j, (zero, zero), True)
        dk, dv = lax.fori_loop(j + 1, nq, lambda i, c: step(i, c, False), carry)
        dk_ref[...] = dk
        dv_ref[...] = dv[:, :NOPE]

    return pl.pallas_call(
        body, name="flash_bwd",
        out_shape=[jax.ShapeDtypeStruct((T, N_HEADS * QK_PAD), F32),
                   jax.ShapeDtypeStruct((T, N_HEADS * QK_PAD), F32),
                   jax.ShapeDtypeStruct((T, N_HEADS * NOPE), F32)],
        grid=(N_HEADS, nq),
        in_specs=[pl.BlockSpec((T, QK_PAD), lambda h, j: (0, h)),
                  pl.BlockSpec((tq, QK_PAD), lambda h, j: (j, h)),
                  pl.BlockSpec((tq, QK_PAD), lambda h, j: (j, h)),
                  pl.BlockSpec((T, QK_PAD), lambda h, j: (0, h)),
                  pl.BlockSpec((T, NOPE), lambda h, j: (0, h))],
        out_specs=[pl.BlockSpec((T, QK_PAD), lambda h, j: (0, h)),
                   pl.BlockSpec((tq, QK_PAD), lambda h, j: (j, h)),
                   pl.BlockSpec((tq, NOPE), lambda h, j: (j, h))],
        compiler_params=_params("arbitrary", "arbitrary"),
    )(q, k, v, do, lse)


def _conv_fwd(u, w, bias, *, tm=256, ct=256):
    T, C = u.shape
    tm = min(tm, T)
    hb = tm // HALO

    def body(cur_ref, halo_ref, w_ref, b_ref, o_ref, buf):
        t = pl.program_id(1)
        buf[0:HALO, :] = jnp.where(t > 0, halo_ref[...], 0.0)
        buf[HALO:, :] = cur_ref[...]
        acc = jnp.broadcast_to(b_ref[...], (tm, ct))
        for kk in range(CONV_W):
            acc = acc + buf[pl.ds(HALO - (CONV_W - 1) + kk, tm), :] * w_ref[kk:kk + 1, :]
        o_ref[...] = acc

    return pl.pallas_call(
        body, name="conv_fwd",
        out_shape=jax.ShapeDtypeStruct((T, C), F32),
        grid=(C // ct, T // tm),
        in_specs=[pl.BlockSpec((tm, ct), lambda c, t: (t, c)),
                  pl.BlockSpec((HALO, ct), lambda c, t: (jnp.maximum(t * hb - 1, 0), c)),
                  pl.BlockSpec((CONV_TAPS_PAD, ct), lambda c, t: (0, c)),
                  pl.BlockSpec((1, ct), lambda c, t: (0, c))],
        out_specs=pl.BlockSpec((tm, ct), lambda c, t: (t, c)),
        scratch_shapes=[pltpu.VMEM((HALO + tm, ct), F32)],
        compiler_params=_params("parallel", "arbitrary"),
    )(u, u, w, bias)


def _conv_bwd(u, du1, w, *, tm=256, ct=256):
    T, C = u.shape
    tm = min(tm, T)
    hb = tm // HALO
    nt = T // tm
    last_halo = T // HALO - 1

    def body(u_ref, uh_ref, d_ref, dh_ref, w_ref, du_ref, dw_ref, ubuf, dbuf):
        t = pl.program_id(1)
        ubuf[0:HALO, :] = jnp.where(t > 0, uh_ref[...], 0.0)
        ubuf[HALO:, :] = u_ref[...]
        dcur = d_ref[...]
        dbuf[0:tm, :] = dcur
        dbuf[tm:, :] = jnp.where(t < nt - 1, dh_ref[...], 0.0)

        @pl.when(t == 0)
        def _():
            dw_ref[...] = jnp.zeros_like(dw_ref)

        acc = jnp.zeros((tm, ct), F32)
        for kk in range(CONV_W):
            acc = acc + dbuf[pl.ds(CONV_W - 1 - kk, tm), :] * w_ref[kk:kk + 1, :]
            dw_ref[kk:kk + 1, :] += jnp.sum(
                dcur * ubuf[pl.ds(HALO - (CONV_W - 1) + kk, tm), :], axis=0, keepdims=True)
        du_ref[...] = acc

    return pl.pallas_call(
        body, name="conv_bwd",
        out_shape=[jax.ShapeDtypeStruct((T, C), F32), jax.ShapeDtypeStruct((CONV_TAPS_PAD, C), F32)],
        grid=(C // ct, nt),
        in_specs=[pl.BlockSpec((tm, ct), lambda c, t: (t, c)),
                  pl.BlockSpec((HALO, ct), lambda c, t: (jnp.maximum(t * hb - 1, 0), c)),
                  pl.BlockSpec((tm, ct), lambda c, t: (t, c)),
                  pl.BlockSpec((HALO, ct), lambda c, t: (jnp.minimum((t + 1) * hb, last_halo), c)),
                  pl.BlockSpec((CONV_TAPS_PAD, ct), lambda c, t: (0, c))],
        out_specs=[pl.BlockSpec((tm, ct), lambda c, t: (t, c)),
                   pl.BlockSpec((CONV_TAPS_PAD, ct), lambda c, t: (0, c))],
        scratch_shapes=[pltpu.VMEM((HALO + tm, ct), F32), pltpu.VMEM((tm + HALO, ct), F32)],
        compiler_params=_params("parallel", "arbitrary"),
    )(u, u, du1, du1, w)


def _adamw(w, g, m, v, *, name):
    R, C = w.shape
    tr = _row_tile(R, max(8, (1 << 18) // C // 8 * 8))

    def body(w_ref, g_ref, m_ref, v_ref, d_ref, nm_ref, nv_ref):
        gv = g_ref[...]
        nm = ADAM_B1 * m_ref[...] + (1.0 - ADAM_B1) * gv
        nv = ADAM_B2 * v_ref[...] + (1.0 - ADAM_B2) * (gv * gv)
        m_hat = nm / (1.0 - ADAM_B1 ** ADAM_STEP)
        v_hat = nv / (1.0 - ADAM_B2 ** ADAM_STEP)
        d_ref[...] = -ADAM_LR * (m_hat / (jnp.sqrt(v_hat) + ADAM_EPS) + ADAM_WD * w_ref[...])
        nm_ref[...] = nm
        nv_ref[...] = nv

    spec = pl.BlockSpec((tr, C), lambda i: (i, 0))
    return pl.pallas_call(
        body, name=name, out_shape=[jax.ShapeDtypeStruct((R, C), F32)] * 3,
        grid=(R // tr,), in_specs=[spec] * 4, out_specs=[spec] * 3,
        compiler_params=_params("parallel"),
    )(w, g, m, v)


ANY = pl.BlockSpec(memory_space=pl.ANY)


def _coords():
    return lax.axis_index("x"), lax.axis_index("y"), lax.axis_index("c")


def _remote(src, dst, ssem, rsem, to):
    return pltpu.make_async_remote_copy(src_ref=src, dst_ref=dst, send_sem=ssem, recv_sem=rsem,
                                        device_id=to, device_id_type=MESH)


def _half(c):
    return pl.ds(pl.multiple_of(c * PACK_HALF, 16), PACK_HALF)


def _gather_weights(wpack, cw):
    def body(w_ref, cw_ref, full_ref, cwf_ref, ssem, rsem):
        x, y, c = _coords()
        q = 2 * x + y
        sib = (x, y, 1 - c)
        chips = [(1 - x, y), (x, 1 - y), (1 - x, 1 - y)]
        mine, other = _half(c), _half(1 - c)
        first = []
        for j, (px, py) in enumerate(chips):
            first.append(_remote(w_ref.at[mine], full_ref.at[q, mine], ssem.at[j], rsem.at[j], (px, py, c)))
            first.append(_remote(cw_ref, cwf_ref.at[q], ssem.at[6 + j], rsem.at[6 + j], (px, py, c)))
        for cp in first:
            cp.start()
        passed = []
        for j, (px, py) in enumerate(chips):
            pq = 2 * px + py
            _remote(w_ref.at[mine], full_ref.at[pq, mine], ssem.at[j], rsem.at[j], (px, py, c)).wait_recv()
            fw = _remote(full_ref.at[pq, mine], full_ref.at[pq, mine], ssem.at[3 + j], rsem.at[3 + j], sib)
            fw.start()
            passed.append(fw)
        for j, (px, py) in enumerate(chips):
            pq = 2 * px + py
            _remote(full_ref.at[pq, other], full_ref.at[pq, other], ssem.at[3 + j], rsem.at[3 + j], sib).wait_recv()
            _remote(cw_ref, cwf_ref.at[pq], ssem.at[6 + j], rsem.at[6 + j], (px, py, c)).wait_recv()
        for cp in first + passed:
            cp.wait_send()

    return pl.pallas_call(
        body, name="gather_weights",
        out_shape=[jax.ShapeDtypeStruct((N_CHIPS,) + wpack.shape, wpack.dtype),
                   jax.ShapeDtypeStruct((N_CHIPS,) + cw.shape, cw.dtype)],
        in_specs=[ANY, ANY], out_specs=[ANY, ANY],
        scratch_shapes=[pltpu.SemaphoreType.DMA((9,)), pltpu.SemaphoreType.DMA((9,))],
    )(wpack, cw)


def _swap_halves(gpack):
    def body(g_ref, recv_ref, ssem, rsem):
        x, y, c = _coords()
        sib = (x, y, 1 - c)
        cps = [_remote(g_ref.at[j, _half(1 - c)], recv_ref.at[j], ssem.at[j], rsem.at[j], sib)
               for j in range(N_CHIPS)]
        for cp in cps:
            cp.start()
        for cp in cps:
            cp.wait_recv()
        for cp in cps:
            cp.wait_send()

    return pl.pallas_call(
        body, name="grad_swap_halves",
        out_shape=jax.ShapeDtypeStruct((N_CHIPS, PACK_HALF, 1024), F32),
        in_specs=[ANY], out_specs=ANY,
        scratch_shapes=[pltpu.SemaphoreType.DMA((N_CHIPS,)), pltpu.SemaphoreType.DMA((N_CHIPS,))],
    )(gpack)


def _chip_sum(place, gpack, recv1):
    nb = PACK_HALF // ADD_ROWS

    def body(place_ref, g_ref, r_ref, o_ref):
        o_ref[...] = (g_ref[...] + r_ref[...]).astype(BF16)

    return pl.pallas_call(
        body, name="grad_chip_sum",
        out_shape=jax.ShapeDtypeStruct((N_CHIPS, PACK_HALF, 1024), BF16),
        grid_spec=pltpu.PrefetchScalarGridSpec(
            num_scalar_prefetch=1, grid=(N_CHIPS, nb),
            in_specs=[pl.BlockSpec((None, ADD_ROWS, 1024), lambda j, i, s: (j, s[1] * nb + i, 0)),
                      pl.BlockSpec((None, ADD_ROWS, 1024), lambda j, i, s: (j, i, 0))],
            out_specs=pl.BlockSpec((None, ADD_ROWS, 1024), lambda j, i, s: (j, i, 0))),
        compiler_params=_params("parallel", "parallel"),
    )(place, gpack, recv1)


def _exchange_chips(part):
    def body(p_ref, recv_ref, ssem, rsem):
        x, y, c = _coords()
        chips = [(1 - x, y), (x, 1 - y), (1 - x, 1 - y)]
        cps = [_remote(p_ref.at[2 * px + py], recv_ref.at[k], ssem.at[k], rsem.at[k], (px, py, c))
               for k, (px, py) in enumerate(chips)]
        for cp in cps:
            cp.start()
        for cp in cps:
            cp.wait_recv()
        for cp in cps:
            cp.wait_send()

    return pl.pallas_call(
        body, name="grad_exchange_chips",
        out_shape=jax.ShapeDtypeStruct((3, PACK_HALF, 1024), BF16),
        in_specs=[ANY], out_specs=ANY,
        scratch_shapes=[pltpu.SemaphoreType.DMA((3,)), pltpu.SemaphoreType.DMA((3,))],
    )(part)


def _mesh_sum(place, gpack, recv1, recv2):
    nb = PACK_HALF // ADD_ROWS

    def body(place_ref, g_ref, r1_ref, r2_ref, o_ref):
        o_ref[...] = ((g_ref[...] + r1_ref[...]) + r2_ref[0].astype(F32)
                      + r2_ref[1].astype(F32) + r2_ref[2].astype(F32))

    return pl.pallas_call(
        body, name="grad_mesh_sum",
        out_shape=jax.ShapeDtypeStruct((PACK_HALF, 1024), F32),
        grid_spec=pltpu.PrefetchScalarGridSpec(
            num_scalar_prefetch=1, grid=(nb,),
            in_specs=[pl.BlockSpec((None, ADD_ROWS, 1024), lambda i, s: (s[0], s[1] * nb + i, 0)),
                      pl.BlockSpec((None, ADD_ROWS, 1024), lambda i, s: (s[0], i, 0)),
                      pl.BlockSpec((3, ADD_ROWS, 1024), lambda i, s: (0, i, 0))],
            out_specs=pl.BlockSpec((ADD_ROWS, 1024), lambda i, s: (i, 0))),
        compiler_params=_params("parallel"),
    )(place, gpack, recv1, recv2)


def _join_halves(red):
    def body(r_ref, o_ref, ssem, rsem):
        x, y, c = _coords()
        cp = _remote(r_ref, o_ref, ssem, rsem, (x, y, 1 - c))
        cp.start()
        cp.wait_recv()
        cp.wait_send()

    return pl.pallas_call(
        body, name="grad_join_halves",
        out_shape=jax.ShapeDtypeStruct((PACK_HALF, 1024), F32),
        in_specs=[ANY], out_specs=ANY,
        scratch_shapes=[pltpu.SemaphoreType.DMA, pltpu.SemaphoreType.DMA],
    )(red)


def _share_small(svec):
    def body(s_ref, all_ref, ssem, rsem):
        x, y, c = _coords()
        me = 4 * x + 2 * y + c
        peers = []
        for r in range(1, N_DEV):
            px = 1 - x if r & 4 else x
            py = 1 - y if r & 2 else y
            pc = 1 - c if r & 1 else c
            peers.append((px, py, pc))
        cps = [_remote(s_ref, all_ref.at[me], ssem.at[k], rsem.at[k], p) for k, p in enumerate(peers)]
        for cp in cps:
            cp.start()
        for k, (px, py, pc) in enumerate(peers):
            _remote(s_ref, all_ref.at[4 * px + 2 * py + pc], ssem.at[k], rsem.at[k], (px, py, pc)).wait_recv()
        for cp in cps:
            cp.wait_send()

    return pl.pallas_call(
        body, name="share_small",
        out_shape=jax.ShapeDtypeStruct((N_DEV,) + svec.shape, F32),
        in_specs=[ANY], out_specs=ANY,
        scratch_shapes=[pltpu.SemaphoreType.DMA((N_DEV - 1,)), pltpu.SemaphoreType.DMA((N_DEV - 1,))],
    )(svec)


def _sum_small(allv):
    def body(a_ref, o_ref):
        acc = a_ref[0]
        for d in range(1, N_DEV):
            acc = acc + a_ref[d]
        o_ref[...] = acc

    return pl.pallas_call(
        body, name="sum_small",
        out_shape=jax.ShapeDtypeStruct(allv.shape[1:], F32),
    )(allv)


def _unpack_full(full):
    out, off = {}, 0
    for name, rows in PACK:
        out[name] = full[:, off:off + rows]
        off += rows
    w = {}
    w["w_in"] = out["w_in"].reshape(4, 1024, 1200).transpose(1, 0, 2).reshape(1024, 4800)
    w["w_uq"] = out["w_uq"].reshape(4, 384, 384).transpose(1, 0, 2).reshape(384, 1536)
    w["w_uk"] = out["w_uk"].reshape(256, 1024)
    w["w_uv"] = out["w_uv"].reshape(256, 1024)
    for n in ("w_o_attn", "w_pw2", "w_out"):
        w[n] = out[n].reshape(1024, 1024)
    w["w_ff1"] = out["w_ff1"].transpose(1, 0, 2).reshape(1024, 4096)
    w["w_ff2"] = out["w_ff2"].reshape(4096, 1024)
    return w


def _pack_grads(g):
    parts = [
        g["w_in"].reshape(1024, 4, 1200).transpose(1, 0, 2).reshape(4, 1200, 1024),
        g["w_uq"].reshape(384, 4, 384).transpose(1, 0, 2).reshape(4, 144, 1024),
        g["w_uk"].reshape(4, 64, 1024),
        g["w_uv"].reshape(4, 64, 1024),
        g["w_o_attn"].reshape(4, 256, 1024),
        g["w_pw2"].reshape(4, 256, 1024),
        g["w_out"].reshape(4, 256, 1024),
        g["w_ff1"].reshape(1024, 4, 1024).transpose(1, 0, 2),
        g["w_ff2"].reshape(4, 1024, 1024),
    ]
    used = sum(r for _, r in PACK)
    parts.append(jnp.zeros((4, PACK_ROWS - used, 1024), F32))
    return jnp.concatenate(parts, axis=1)


def _pad_w_in(w):
    return jnp.concatenate([w[:, :704], jnp.zeros((1024, Z_SMALL - 704), w.dtype), w[:, 704:]], axis=1)


def _pad_w_uq(w):
    w = w.reshape(Q_RANK, N_HEADS, NOPE + ROPE)
    return jnp.pad(w, ((0, 0), (0, 0), (0, QK_PAD - NOPE - ROPE))).reshape(Q_RANK, N_HEADS * QK_PAD)


def _local_step(xs, pos, tgt, W, cw, vec):
    D = D_MODEL
    w_in = _pad_w_in(W["w_in"])
    w_uq = _pad_w_uq(W["w_uq"])
    w_ukv = jnp.concatenate([W["w_uk"], W["w_uv"]], axis=1)
    inv = ROPE_THETA ** (-jnp.arange(0, ROPE, 2, dtype=F32) / ROPE)
    inv_freq = jnp.concatenate([inv, inv, jnp.zeros((128 - ROPE,), F32)])[None, :]

    (h,) = _rowwise(lambda x, g: (_rms(x, g),), [(xs, D, 0)], [vec["norm_mix_pre"]], [(D, BF16)],
                    name="rms_pre")
    z = _mm(h, w_in, name="mm_in")

    def qkv_prep(zs, p, qn, kvn, invf):
        ang = p.astype(F32) * invf
        cosv, sinv = jnp.cos(ang), jnp.sin(ang)
        ln = lax.broadcasted_iota(jnp.int32, ang.shape, 1)
        cs = jnp.where(ln < ROPE, cosv, 0.0)
        sa = jnp.where((ln >= ROPE // 2) & (ln < ROPE), sinv, 0.0)
        sb = jnp.where(ln < ROPE // 2, -sinv, 0.0)
        cqn = _rms(zs[:, :Q_RANK], qn)
        ckvn = _rms(zs[:, Q_RANK:Q_RANK + KV_RANK], kvn)
        kr = _rope(zs[:, 640:768], cs, sa, sb)
        return cqn, ckvn, kr, cs, sa, sb

    cqn, ckvn, krot, cs, sa, sb = _rowwise(
        qkv_prep, [(z, Z_SMALL, 0), (pos, 1, 0)], [vec["q_norm"], vec["kv_norm"], inv_freq],
        [(Q_RANK, BF16), (KV_RANK, BF16), (128, BF16), (128, F32), (128, F32), (128, F32)],
        name="qkv_prep")
    q_pre = _mm(cqn, w_uq, name="mm_uq")
    kv = _mm(ckvn, w_ukv, name="mm_ukv")

    def qk_rope(qp, kn, vv, kr, cs_, sa_, sb_):
        qs, ks, vs = [], [], []
        ones = jnp.ones((qp.shape[0], NOPE), F32)
        for hd in range(N_HEADS):
            qs.append(qp[:, hd * QK_PAD:hd * QK_PAD + NOPE])
            qs.append(_rope(qp[:, hd * QK_PAD + NOPE:(hd + 1) * QK_PAD], cs_, sa_, sb_))
            ks.append(kn[:, hd * NOPE:(hd + 1) * NOPE])
            ks.append(kr.astype(F32))
            vs.append(vv[:, hd * NOPE:(hd + 1) * NOPE])
            vs.append(ones)
        return (jnp.concatenate(qs, axis=1) * Q_SCALE, jnp.concatenate(ks, axis=1),
                jnp.concatenate(vs, axis=1))

    q, k, v = _rowwise(
        qk_rope, [(q_pre, 2048, 0), (kv, 1024, 0), (kv, 1024, 1), (krot, 128, 0), (cs, 128, 0),
                  (sa, 128, 0), (sb, 128, 0)], [],
        [(2048, BF16), (2048, BF16), (2048, BF16)], name="qk_rope")
    attn, lse = _flash_fwd(q, k, v)
    y_attn = _mm(attn, W["w_o_attn"], name="mm_o_attn")

    (u0,) = _rowwise(lambda a, b: (a * _sigmoid(b),), [(z, D, 1), (z, D, 2)], [], [(D, F32)], name="glu")
    u1 = _conv_fwd(u0, cw, vec["conv_b"])

    def ln_parts(u, g, b):
        mu = jnp.mean(u, axis=-1, keepdims=True)
        xc = u - mu
        rs = lax.rsqrt(jnp.mean(xc * xc, axis=-1, keepdims=True) + EPS)
        nh = xc * rs
        return nh, rs, nh * g + b

    def ln_silu(u, g, b):
        y = ln_parts(u, g, b)[2]
        return (y * _sigmoid(y),)

    (u3,) = _rowwise(ln_silu, [(u1, D, 0)], [vec["conv_ln_g"], vec["conv_ln_b"]], [(D, BF16)],
                     name="ln_silu")
    y_conv = _mm(u3, W["w_pw2"], name="mm_pw2")

    def merge(ga, gc, ya, yc, b2):
        return (_sigmoid(ga) * ya + _sigmoid(gc) * (yc + b2),)

    (merged,) = _rowwise(merge, [(z, D, 3), (z, D, 4), (y_attn, D, 0), (y_conv, D, 0)], [vec["b_pw2"]],
                         [(D, BF16)], name="merge")
    mo = _mm(merged, W["w_out"], name="mm_out")

    def post1(x, m, g1, g2):
        x1 = x + _rms(m, g1)
        return x1, _rms(x1, g2)

    x1, h2 = _rowwise(post1, [(xs, D, 0), (mo, D, 0)], [vec["norm_mix_post"], vec["norm_mlp_pre"]],
                      [(D, F32), (D, BF16)], name="post_mix")

    def sqrelu(acc):
        r = jnp.maximum(acc, 0.0)
        return r * r, r

    a2, r1 = _mm(h2, W["w_ff1"], name="mm_ff1", epilogue=sqrelu, out_dtypes=(BF16, BF16))
    f = _mm(a2, W["w_ff2"], name="mm_ff2")

    def post2(x1_, f_, t_, g):
        e = x1_ + _rms(f_, g) - t_
        dy = e * (1.0 / D)
        df, dg = _rms_bwd(f_, g, dy)
        loss = jnp.broadcast_to(jnp.sum(e * e, keepdims=True) * (0.5 / D), (1, 128))
        return dy, df, dg, loss

    dy, df, g_norm_mlp_post, loss = _rowwise(
        post2, [(x1, D, 0), (f, D, 0), (tgt, D, 0)], [vec["norm_mlp_post"]],
        [(D, F32), (D, BF16)], [(1, D), (1, 128)], name="post_mlp_loss")
    g_w_ff2 = _mm(a2, df, ta=True, name="mm_dw_ff2")
    df1 = _mm(df, W["w_ff2"], tb=True, name="mm_df1", extras=[r1], out_dtypes=(BF16,),
              epilogue=lambda acc, r: (acc * (2.0 * r.astype(F32)),))
    g_w_ff1 = _mm(h2, df1, ta=True, name="mm_dw_ff1")
    dh2 = _mm(df1, W["w_ff1"], tb=True, name="mm_dh2")

    def bwd_mid(dy_, dh2_, x1_, m, g2, g1):
        d1, dg2 = _rms_bwd(x1_, g2, dh2_)
        dx1_ = dy_ + d1
        dm, dg1 = _rms_bwd(m, g1, dx1_)
        return dx1_, dm, dg2, dg1

    dx1, dmo, g_norm_mlp_pre, g_norm_mix_post = _rowwise(
        bwd_mid, [(dy, D, 0), (dh2, D, 0), (x1, D, 0), (mo, D, 0)],
        [vec["norm_mlp_pre"], vec["norm_mix_post"]], [(D, F32), (D, BF16)], [(1, D), (1, D)],
        name="bwd_mid")
    g_w_out = _mm(merged, dmo, ta=True, name="mm_dw_out")
    dmerged = _mm(dmo, W["w_out"], tb=True, name="mm_dmerged")

    def dmerge(dm, ga, gc, ya, yc, b2):
        sga, sgc = _sigmoid(ga), _sigmoid(gc)
        dya = dm * sga
        dyc = dm * sgc
        dga = dm * ya * sga * (1.0 - sga)
        dgc = dm * (yc + b2) * sgc * (1.0 - sgc)
        return dya, dyc, dga, dgc, jnp.sum(dyc, axis=0, keepdims=True)

    dya, dyc, dga, dgc, g_b_pw2 = _rowwise(
        dmerge, [(dmerged, D, 0), (z, D, 3), (z, D, 4), (y_attn, D, 0), (y_conv, D, 0)], [vec["b_pw2"]],
        [(D, BF16), (D, BF16), (D, BF16), (D, BF16)], [(1, D)], name="dmerge")

    g_w_o = _mm(attn, dya, ta=True, name="mm_dw_o_attn")
    dattn = _mm(dya, W["w_o_attn"], tb=True, out_dtype=BF16, name="mm_dattn")

    def delta_fn(do_, o_):
        do32 = do_.astype(F32)
        pr = do32 * o_.astype(F32)
        ln = lax.broadcasted_iota(jnp.int32, (pr.shape[0], NOPE), 1)
        cols = []
        for hd in range(N_HEADS):
            dl = jnp.sum(pr[:, hd * NOPE:(hd + 1) * NOPE], axis=-1, keepdims=True)
            hi = dl.astype(BF16).astype(F32)
            cols.append(do32[:, hd * NOPE:(hd + 1) * NOPE])
            cols.append(jnp.where(ln == 0, -hi, jnp.where(ln == 1, hi - dl, 0.0)))
        return (jnp.concatenate(cols, axis=1),)

    (do_ext,) = _rowwise(delta_fn, [(dattn, D, 0), (attn, D, 0)], [], [(2048, BF16)], name="attn_delta")
    dq, dk, dv = _flash_bwd(q, k, v, do_ext, lse)

    def unrope(dq_, dk_, dv_, cs_, sa_, sb_):
        qs, kn = [], []
        dkr = jnp.zeros_like(cs_)
        for hd in range(N_HEADS):
            qs.append(dq_[:, hd * QK_PAD:hd * QK_PAD + NOPE] * SCALE)
            qs.append(_rope_t(dq_[:, hd * QK_PAD + NOPE:(hd + 1) * QK_PAD] * SCALE, cs_, sa_, sb_))
            kn.append(dk_[:, hd * QK_PAD:hd * QK_PAD + NOPE] * K_UNSCALE)
            dkr = dkr + dk_[:, hd * QK_PAD + NOPE:(hd + 1) * QK_PAD]
        return (jnp.concatenate(qs, axis=1), jnp.concatenate(kn + [dv_], axis=1),
                _rope_t(dkr * K_UNSCALE, cs_, sa_, sb_))

    dq_pre, dkv, dkr = _rowwise(
        unrope, [(dq, 2048, 0), (dk, 2048, 0), (dv, 1024, 0), (cs, 128, 0), (sa, 128, 0), (sb, 128, 0)], [],
        [(2048, BF16), (2048, BF16), (128, F32)], name="unrope")
    g_w_uq = _mm(cqn, dq_pre, ta=True, name="mm_dw_uq")
    dcqn = _mm(dq_pre, w_uq, tb=True, name="mm_dcqn")
    g_w_ukv = _mm(ckvn, dkv, ta=True, name="mm_dw_ukv")
    dckvn = _mm(dkv, w_ukv, tb=True, name="mm_dckvn")

    def small_bwd(zs, dcq_, dckv_, dkr_, qn, kvn):
        dcq, dqn = _rms_bwd(zs[:, :Q_RANK], qn, dcq_)
        dckv, dkvn = _rms_bwd(zs[:, Q_RANK:Q_RANK + KV_RANK], kvn, dckv_)
        pad = jnp.zeros((zs.shape[0], Z_SMALL - 768), F32)
        return jnp.concatenate([dcq, dckv, dkr_, pad], axis=1), dqn, dkvn

    dz_small, g_q_norm, g_kv_norm = _rowwise(
        small_bwd, [(z, Z_SMALL, 0), (dcqn, Q_RANK, 0), (dckvn, KV_RANK, 0), (dkr, 128, 0)],
        [vec["q_norm"], vec["kv_norm"]], [(Z_SMALL, BF16)], [(1, Q_RANK), (1, KV_RANK)], name="small_bwd")

    g_w_pw2 = _mm(u3, dyc, ta=True, name="mm_dw_pw2")
    du3 = _mm(dyc, W["w_pw2"], tb=True, name="mm_du3")

    def ln_silu_bwd(d3, u, g, b):
        nh, rs, y = ln_parts(u, g, b)
        sg = _sigmoid(y)
        dyv = d3 * (sg * (1.0 + y * (1.0 - sg)))
        dnh = dyv * g
        du = rs * (dnh - jnp.mean(dnh, axis=-1, keepdims=True)
                   - nh * jnp.mean(dnh * nh, axis=-1, keepdims=True))
        return (du, jnp.sum(dyv * nh, axis=0, keepdims=True), jnp.sum(dyv, axis=0, keepdims=True),
                jnp.sum(du, axis=0, keepdims=True))

    du1, g_ln_g, g_ln_b, g_conv_b = _rowwise(
        ln_silu_bwd, [(du3, D, 0), (u1, D, 0)], [vec["conv_ln_g"], vec["conv_ln_b"]], [(D, F32)],
        [(1, D), (1, D), (1, D)], name="ln_silu_bwd")
    du0, g_conv_w = _conv_bwd(u0, du1, cw)

    def glu_bwd(d0, a, b):
        sg = _sigmoid(b)
        return d0 * sg, d0 * a * sg * (1.0 - sg)

    dza, dzb = _rowwise(glu_bwd, [(du0, D, 0), (z, D, 1), (z, D, 2)], [], [(D, BF16), (D, BF16)],
                        name="glu_bwd")

    dz = jnp.concatenate([dz_small, dza, dzb, dga, dgc], axis=1)
    g_w_in = _mm(h, dz, ta=True, name="mm_dw_in")
    dh = _mm(dz, w_in, tb=True, name="mm_dh")

    def final(dx1_, dh_, x, g):
        d, dg = _rms_bwd(x, g, dh_)
        return dx1_ + d, dg

    grad_x, g_norm_mix_pre = _rowwise(final, [(dx1, D, 0), (dh, D, 0), (xs, D, 0)], [vec["norm_mix_pre"]],
                                      [(D, F32)], [(1, D)], name="bwd_final")

    gw = {
        "w_in": jnp.concatenate([g_w_in[:, :704], g_w_in[:, Z_SMALL:]], axis=1),
        "w_uq": g_w_uq.reshape(Q_RANK, N_HEADS, QK_PAD)[:, :, :NOPE + ROPE].reshape(Q_RANK, 1536),
        "w_uk": g_w_ukv[:, :1024], "w_uv": g_w_ukv[:, 1024:],
        "w_o_attn": g_w_o, "w_pw2": g_w_pw2, "w_out": g_w_out, "w_ff1": g_w_ff1, "w_ff2": g_w_ff2,
    }
    gs = {
        "norm_mix_pre": g_norm_mix_pre, "conv_b": g_conv_b, "conv_ln_g": g_ln_g, "conv_ln_b": g_ln_b,
        "b_pw2": g_b_pw2, "norm_mix_post": g_norm_mix_post, "norm_mlp_pre": g_norm_mlp_pre,
        "norm_mlp_post": g_norm_mlp_post, "q_norm": g_q_norm, "kv_norm": g_kv_norm,
    }
    return grad_x, gw, gs, g_conv_w, loss


def _row1024(a):
    return jnp.pad(a, ((0, 0), (0, 1024 - a.shape[1])))


BIG = ("w_in", "w_uq", "w_uk", "w_uv", "w_o_attn", "w_pw2", "w_out", "w_ff1", "w_ff2")
ORDER = ("norm_mix_pre", "w_in", "q_norm", "w_uq", "kv_norm", "w_uk", "w_uv", "w_o_attn", "conv_w",
         "conv_b", "conv_ln_g", "conv_ln_b", "w_pw2", "b_pw2", "w_out", "norm_mix_post", "norm_mlp_pre",
         "w_ff1", "w_ff2", "norm_mlp_post")


def kernel(x, positions, norm_mix_pre, w_in, q_norm, w_uq, kv_norm, w_uk, w_uv, w_o_attn, conv_w, conv_b, conv_ln_g, conv_ln_b, w_pw2, b_pw2, w_out, norm_mix_post, norm_mlp_pre, w_ff1, w_ff2, norm_mlp_post, loss_target, m_norm_mix_pre, m_w_in, m_q_norm, m_w_uq, m_kv_norm, m_w_uk, m_w_uv, m_w_o_attn, m_conv_w, m_conv_b, m_conv_ln_g, m_conv_ln_b, m_w_pw2, m_b_pw2, m_w_out, m_norm_mix_post, m_norm_mlp_pre, m_w_ff1, m_w_ff2, m_norm_mlp_post, v_norm_mix_pre, v_w_in, v_q_norm, v_w_uq, v_kv_norm, v_w_uk, v_w_uv, v_w_o_attn, v_conv_w, v_conv_b, v_conv_ln_g, v_conv_ln_b, v_w_pw2, v_b_pw2, v_w_out, v_norm_mix_post, v_norm_mlp_pre, v_w_ff1, v_w_ff2, v_norm_mlp_post):
    args = dict(locals())
    wts = {n: args[n] for n in ORDER}
    mom = {n: args["m_" + n] for n in ORDER}
    var = {n: args["v_" + n] for n in ORDER}
    T = x.shape[1]
    place = jnp.stack([2 * lax.axis_index("x") + lax.axis_index("y"), lax.axis_index("c")]).astype(jnp.int32)

    used = sum(r for _, r in PACK)
    shards = [wts[n].astype(BF16).reshape(-1, 1024) for n in BIG]
    wpack = jnp.concatenate(shards + [jnp.zeros((PACK_ROWS - used, 1024), BF16)], axis=0)
    cw_own = jnp.pad(conv_w.reshape(CONV_W, -1), ((0, CONV_TAPS_PAD - CONV_W), (0, 0)))
    full, cw_all = _gather_weights(wpack, cw_own)
    full = lax.dynamic_update_slice(full, wpack[None], (place[0], 0, 0))
    cw_all = lax.dynamic_update_slice(cw_all, cw_own[None], (place[0], 0, 0))
    W = _unpack_full(full)
    cw = cw_all.transpose(1, 0, 2).reshape(CONV_TAPS_PAD, D_MODEL)
    vec = {n: wts[n] for n in SMALL_VECS}

    grad_x, gw, gs, g_conv_w, loss = _local_step(x[0], positions.reshape(T, 1), loss_target[0], W, cw, vec)

    gpack = _pack_grads(gw)
    recv1 = _swap_halves(gpack)
    part = _chip_sum(place, gpack, recv1)
    recv2 = _exchange_chips(part)
    red = _mesh_sum(place, gpack, recv1, recv2)
    theirs = _join_halves(red)
    gfull = jnp.where(place[1] == 0, jnp.concatenate([red, theirs], axis=0),
                      jnp.concatenate([theirs, red], axis=0))

    svec = jnp.concatenate(
        [_row1024(gs[n]) for n in SMALL_VECS] + [_row1024(loss)]
        + [jnp.zeros((SMALL_CONVW_ROW - SMALL_LOSS_ROW - 1, 1024), F32), g_conv_w], axis=0)
    me = 2 * place[0] + place[1]
    ssum = _sum_small(lax.dynamic_update_slice(_share_small(svec), svec[None], (me, 0, 0)))

    grads, off = {}, 0
    for name, rows in PACK:
        grads[name] = gfull[off:off + rows].reshape(wts[name].shape)
        off += rows
    for r, n in enumerate(SMALL_VECS):
        grads[n] = ssum[r:r + 1, :wts[n].shape[1]]
    cw_sum = ssum[SMALL_CONVW_ROW:SMALL_CONVW_ROW + CONV_W]
    grads["conv_w"] = lax.dynamic_slice_in_dim(cw_sum, place[0] * 256, 256, axis=1).reshape(conv_w.shape)
    loss_out = ssum[SMALL_LOSS_ROW, 0]

    delta, new_m, new_v = {}, {}, {}

    def as2d(a):
        return a.reshape(-1, a.shape[-1]) if a.ndim != 4 or a.shape[2] == 1 else a.reshape(a.shape[1], -1)

    for n in BIG + ("conv_w",):
        shp = wts[n].shape
        outs = _adamw(as2d(wts[n]), as2d(grads[n]), as2d(mom[n]), as2d(var[n]), name="adamw_" + n)
        delta[n], new_m[n], new_v[n] = (o.reshape(shp) for o in outs)
    small = [jnp.concatenate([_row1024(t[n]) for n in SMALL_VECS], axis=0) for t in (wts, grads, mom, var)]
    outs = _adamw(*small, name="adamw_small")
    for r, n in enumerate(SMALL_VECS):
        delta[n], new_m[n], new_v[n] = (o[r:r + 1, :wts[n].shape[1]] for o in outs)

    return (loss_out, grad_x[None], *[grads[n] for n in ORDER], *[delta[n] for n in ORDER],
            *[new_m[n] for n in ORDER], *[new_v[n] for n in ORDER])
```

```python
import functools

import numpy as np
import jax
import jax.numpy as jnp
from jax import lax
from jax.experimental import pallas as pl
from jax.experimental.pallas import tpu as pltpu

F32 = jnp.float32
BF16 = jnp.bfloat16
MESH = pl.DeviceIdType.MESH

D_MODEL = 1024
N_HEADS = 8
NOPE = 128
ROPE = 64
QK_PAD = 256
Q_RANK = 384
KV_RANK = 256
CONV_W = 31
CONV_TAPS_PAD = 32
HALO = 32
D_FF = 4096
EPS = 1e-6
ROPE_THETA = 10000.0
SCALE = float((NOPE + ROPE) ** -0.5)
Q_SCALE = SCALE * float(np.log2(np.e))
K_UNSCALE = float(np.log(2.0))
Z_SMALL = 1024
Z_COLS = Z_SMALL + 4 * D_MODEL
N_CHIPS = 4
N_DEV = 8

ADAM_LR = 0.001
ADAM_B1 = 0.9
ADAM_B2 = 0.999
ADAM_EPS = 1e-08
ADAM_WD = 0.01
ADAM_STEP = 10

VMEM_LIMIT = 48 * 1024 * 1024
NEG = -1e30

PACK = (("w_in", 1200), ("w_uq", 144), ("w_uk", 64), ("w_uv", 64), ("w_o_attn", 256),
        ("w_pw2", 256), ("w_out", 256), ("w_ff1", 1024), ("w_ff2", 1024))
PACK_ROWS = 4320
PACK_HALF = PACK_ROWS // 2
ADD_ROWS = 240

SMALL_VECS = ("norm_mix_pre", "conv_b", "conv_ln_g", "conv_ln_b", "b_pw2", "norm_mix_post",
              "norm_mlp_pre", "norm_mlp_post", "q_norm", "kv_norm")
SMALL_LOSS_ROW = 10
SMALL_CONVW_ROW = 16
SMALL_ROWS = 48


def _params(*sem):
    return pltpu.CompilerParams(dimension_semantics=sem, vmem_limit_bytes=VMEM_LIMIT)


def _tile(n, cap):
    if n <= cap:
        return n
    t = (cap // 128) * 128
    while n % t:
        t -= 128
    return t


def _row_tile(n, cap):
    if n <= cap:
        return n
    t = (cap // 8) * 8
    while n % t:
        t -= 8
    return t


def _mm(a, b, *, name, ta=False, tb=False, out_dtype=F32, caps=(1024, 1024, 1024), extras=(),
        epilogue=None, out_dtypes=None):
    if ta:
        K, M = a.shape
    else:
        M, K = a.shape
    if tb:
        N, K2 = b.shape
    else:
        K2, N = b.shape
    assert K == K2, (a.shape, b.shape, ta, tb)
    tm, tn, tk = _tile(M, caps[0]), _tile(N, caps[1]), _tile(K, caps[2])
    nk = K // tk
    dims = (((0 if ta else 1,), (1 if tb else 0,)), ((), ()))
    out_dtypes = tuple(out_dtypes or (out_dtype,))
    n_ex, n_out = len(extras), len(out_dtypes)

    def body(*refs):
        a_ref, b_ref = refs[:2]
        ex_refs = refs[2:2 + n_ex]
        o_refs = refs[2 + n_ex:2 + n_ex + n_out]
        k = pl.program_id(2)
        part = lax.dot_general(a_ref[...].astype(BF16), b_ref[...].astype(BF16), dims,
                               preferred_element_type=F32)

        def finish(acc):
            res = epilogue(acc, *[r[...] for r in ex_refs]) if epilogue else (acc,)
            for o_ref, r in zip(o_refs, res):
                o_ref[...] = r.astype(o_ref.dtype)

        if nk == 1:
            finish(part)
        else:
            acc_ref = refs[-1]

            @pl.when(k == 0)
            def _():
                acc_ref[...] = part

            @pl.when((k > 0) & (k < nk - 1))
            def _():
                acc_ref[...] += part

            @pl.when(k == nk - 1)
            def _():
                finish(acc_ref[...] + part)

    a_spec = (pl.BlockSpec((tk, tm), lambda i, j, k: (k, i)) if ta
              else pl.BlockSpec((tm, tk), lambda i, j, k: (i, k)))
    b_spec = (pl.BlockSpec((tn, tk), lambda i, j, k: (j, k)) if tb
              else pl.BlockSpec((tk, tn), lambda i, j, k: (k, j)))
    o_spec = pl.BlockSpec((tm, tn), lambda i, j, k: (i, j))
    res = pl.pallas_call(
        body, name=name,
        out_shape=[jax.ShapeDtypeStruct((M, N), dt) for dt in out_dtypes],
        grid=(M // tm, N // tn, nk),
        in_specs=[a_spec, b_spec] + [o_spec] * n_ex,
        out_specs=[o_spec] * n_out,
        scratch_shapes=[pltpu.VMEM((tm, tn), F32)] if nk > 1 else [],
        compiler_params=_params("parallel", "parallel", "arbitrary"),
    )(a, b, *extras)
    return res[0] if n_out == 1 else res


def _rowwise(fn, rows, consts, outs, reds=(), *, name, tm=256):
    T = rows[0][0].shape[0]
    tm = min(tm, T)
    n_row, n_const, n_out = len(rows), len(consts), len(outs)

    def body(*refs):
        i = pl.program_id(0)
        vals = [r[...] for r in refs[:n_row + n_const]]
        res = fn(*vals)
        out_refs = refs[n_row + n_const:]
        for k in range(n_out):
            out_refs[k][...] = res[k].astype(out_refs[k].dtype)
        for k in range(len(reds)):
            ref = out_refs[n_out + k]

            @pl.when(i == 0)
            def _(ref=ref):
                ref[...] = jnp.zeros_like(ref)

            ref[...] += res[n_out + k]

    in_specs = [pl.BlockSpec((tm, w), functools.partial(lambda i, cb: (i, cb), cb=cb))
                for (_, w, cb) in rows]
    in_specs += [pl.BlockSpec(c.shape, lambda i: (0, 0)) for c in consts]
    out_specs = [pl.BlockSpec((tm, w), lambda i: (i, 0)) for (w, _) in outs]
    out_specs += [pl.BlockSpec(s, lambda i: (0, 0)) for s in reds]
    out_shape = [jax.ShapeDtypeStruct((T, w), dt) for (w, dt) in outs]
    out_shape += [jax.ShapeDtypeStruct(s, F32) for s in reds]
    return pl.pallas_call(
        body, name=name, out_shape=out_shape, grid=(T // tm,),
        in_specs=in_specs, out_specs=out_specs,
        compiler_params=_params("arbitrary"),
    )(*[r[0] for r in rows], *consts)


def _rms(x, g):
    r = lax.rsqrt(jnp.mean(x * x, axis=-1, keepdims=True) + EPS)
    return x * r * g


def _rms_bwd(x, g, dy):
    r = lax.rsqrt(jnp.mean(x * x, axis=-1, keepdims=True) + EPS)
    n = x * r
    dyg = dy * g
    dx = r * (dyg - n * jnp.mean(dyg * n, axis=-1, keepdims=True))
    return dx, jnp.sum(dy * n, axis=0, keepdims=True)


def _sigmoid(x):
    return 1.0 / (1.0 + jnp.exp(-x))


def _rope(x, cs, sa, sb):
    return x * cs + pltpu.roll(x, 32, 1) * sa + pltpu.roll(x, 96, 1) * sb


def _rope_t(dy, cs, sa, sb):
    return dy * cs + pltpu.roll(dy * sa, 96, 1) + pltpu.roll(dy * sb, 32, 1)


def _causal_mask(t):
    row = lax.broadcasted_iota(jnp.int32, (t, t), 0)
    col = lax.broadcasted_iota(jnp.int32, (t, t), 1)
    return col <= row


NT_DIMS = (((1,), (1,)), ((), ()))
TN_DIMS = (((0,), (0,)), ((), ()))


def _flash_fwd(q, k, v, *, tq=512):
    T = q.shape[0]
    tq = min(tq, T)

    def body(q_ref, k_ref, v_ref, o_ref, lse_ref):
        i = pl.program_id(1)
        qv = q_ref[...]

        def step(j, carry, masked):
            m, acc = carry
            sl = pl.ds(pl.multiple_of(j * tq, tq), tq)
            s = lax.dot_general(qv, k_ref[sl, :], NT_DIMS, preferred_element_type=F32)
            if masked:
                s = jnp.where(_causal_mask(tq), s, NEG)
            m_new = jnp.maximum(m, jnp.max(s, axis=-1, keepdims=True))
            p = jnp.exp2(s - m_new)
            acc = jnp.exp2(m - m_new) * acc + jnp.dot(p.astype(BF16), v_ref[sl, :],
                                                      preferred_element_type=F32)
            return m_new, acc

        init = (jnp.full((tq, 1), NEG, F32), jnp.zeros((tq, QK_PAD), F32))
        carry = lax.fori_loop(0, i, lambda j, c: step(j, c, False), init)
        m, acc = step(i, carry, True)
        l = acc[:, NOPE:]
        o_ref[...] = (acc[:, :NOPE] / l).astype(BF16)
        lse_ref[...] = m + jnp.log2(l)

    return pl.pallas_call(
        body, name="flash_fwd",
        out_shape=[jax.ShapeDtypeStruct((T, N_HEADS * NOPE), BF16),
                   jax.ShapeDtypeStruct((T, N_HEADS * NOPE), F32)],
        grid=(N_HEADS, T // tq),
        in_specs=[pl.BlockSpec((tq, QK_PAD), lambda h, i: (i, h)),
                  pl.BlockSpec((T, QK_PAD), lambda h, i: (0, h)),
                  pl.BlockSpec((T, QK_PAD), lambda h, i: (0, h))],
        out_specs=[pl.BlockSpec((tq, NOPE), lambda h, i: (i, h)),
                   pl.BlockSpec((tq, NOPE), lambda h, i: (i, h))],
        compiler_params=_params("parallel", "arbitrary"),
    )(q, k, v)


def _flash_bwd(q, k, v, do, lse, *, tq=512):
    T = q.shape[0]
    tq = min(tq, T)
    nq = T // tq

    def body(q_ref, k_ref, v_ref, do_ref, lse_ref, dq_ref, dk_ref, dv_ref):
        j = pl.program_id(1)

        @pl.when(j == 0)
        def _():
            dq_ref[...] = jnp.zeros_like(dq_ref)

        kj = k_ref[...]
        vj = v_ref[...]

        def step(i, carry, masked):
            dk, dv = carry
            sl = pl.ds(pl.multiple_of(i * tq, tq), tq)
            qi = q_ref[sl, :]
            doi = do_ref[sl, :]
            s = lax.dot_general(qi, kj, NT_DIMS, preferred_element_type=F32)
            p = jnp.exp2(s - lse_ref[sl, :][:, :1])
            if masked:
                p = jnp.where(_causal_mask(tq), p, 0.0)
            dv = dv + lax.dot_general(p.astype(BF16), doi, TN_DIMS, preferred_element_type=F32)
            ds = (p * lax.dot_general(doi, vj, NT_DIMS, preferred_element_type=F32)).astype(BF16)
            dk = dk + lax.dot_general(ds, qi, TN_DIMS, preferred_element_type=F32)
            dq_ref[sl, :] += jnp.dot(ds, kj, preferred_element_type=F32)
            return dk, dv

        zero = jnp.zeros((tq, QK_PAD), F32)
        carry = step(j, (zero, zero), True)
        dk, dv = lax.fori_loop(j + 1, nq, lambda i, c: step(i, c, False), carry)
        dk_ref[...] = dk
        dv_ref[...] = dv[:, :NOPE]

    return pl.pallas_call(
        body, name="flash_bwd",
        out_shape=[jax.ShapeDtypeStruct((T, N_HEADS * QK_PAD), F32),
                   jax.ShapeDtypeStruct((T, N_HEADS * QK_PAD), F32),
                   jax.ShapeDtypeStruct((T, N_HEADS * NOPE), F32)],
        grid=(N_HEADS, nq),
        in_specs=[pl.BlockSpec((T, QK_PAD), lambda h, j: (0, h)),
                  pl.BlockSpec((tq, QK_PAD), lambda h, j: (j, h)),
                  pl.BlockSpec((tq, QK_PAD), lambda h, j: (j, h)),
                  pl.BlockSpec((T, QK_PAD), lambda h, j: (0, h)),
                  pl.BlockSpec((T, NOPE), lambda h, j: (0, h))],
        out_specs=[pl.BlockSpec((T, QK_PAD), lambda h, j: (0, h)),
                   pl.BlockSpec((tq, QK_PAD), lambda h, j: (j, h)),
                   pl.BlockSpec((tq, NOPE), lambda h, j: (j, h))],
        compiler_params=_params("arbitrary", "arbitrary"),
    )(q, k, v, do, lse)


def _conv_fwd(u, w, bias, *, tm=256, ct=256):
    T, C = u.shape
    tm = min(tm, T)
    hb = tm // HALO

    def body(cur_ref, halo_ref, w_ref, b_ref, o_ref, buf):
        t = pl.program_id(1)
        buf[0:HALO, :] = jnp.where(t > 0, halo_ref[...], 0.0)
        buf[HALO:, :] = cur_ref[...]
        acc = jnp.broadcast_to(b_ref[...], (tm, ct))
        for kk in range(CONV_W):
            acc = acc + buf[pl.ds(HALO - (CONV_W - 1) + kk, tm), :] * w_ref[kk:kk + 1, :]
        o_ref[...] = acc

    return pl.pallas_call(
        body, name="conv_fwd",
        out_shape=jax.ShapeDtypeStruct((T, C), F32),
        grid=(C // ct, T // tm),
        in_specs=[pl.BlockSpec((tm, ct), lambda c, t: (t, c)),
                  pl.BlockSpec((HALO, ct), lambda c, t: (jnp.maximum(t * hb - 1, 0), c)),
                  pl.BlockSpec((CONV_TAPS_PAD, ct), lambda c, t: (0, c)),
                  pl.BlockSpec((1, ct), lambda c, t: (0, c))],
        out_specs=pl.BlockSpec((tm, ct), lambda c, t: (t, c)),
        scratch_shapes=[pltpu.VMEM((HALO + tm, ct), F32)],
        compiler_params=_params("parallel", "arbitrary"),
    )(u, u, w, bias)


def _conv_bwd(u, du1, w, *, tm=256, ct=256):
    T, C = u.shape
    tm = min(tm, T)
    hb = tm // HALO
    nt = T // tm
    last_halo = T // HALO - 1

    def body(u_ref, uh_ref, d_ref, dh_ref, w_ref, du_ref, dw_ref, ubuf, dbuf):
        t = pl.program_id(1)
        ubuf[0:HALO, :] = jnp.where(t > 0, uh_ref[...], 0.0)
        ubuf[HALO:, :] = u_ref[...]
        dcur = d_ref[...]
        dbuf[0:tm, :] = dcur
        dbuf[tm:, :] = jnp.where(t < nt - 1, dh_ref[...], 0.0)

        @pl.when(t == 0)
        def _():
            dw_ref[...] = jnp.zeros_like(dw_ref)

        acc = jnp.zeros((tm, ct), F32)
        for kk in range(CONV_W):
            acc = acc + dbuf[pl.ds(CONV_W - 1 - kk, tm), :] * w_ref[kk:kk + 1, :]
            dw_ref[kk:kk + 1, :] += jnp.sum(
                dcur * ubuf[pl.ds(HALO - (CONV_W - 1) + kk, tm), :], axis=0, keepdims=True)
        du_ref[...] = acc

    return pl.pallas_call(
        body, name="conv_bwd",
        out_shape=[jax.ShapeDtypeStruct((T, C), F32), jax.ShapeDtypeStruct((CONV_TAPS_PAD, C), F32)],
        grid=(C // ct, nt),
        in_specs=[pl.BlockSpec((tm, ct), lambda c, t: (t, c)),
                  pl.BlockSpec((HALO, ct), lambda c, t: (jnp.maximum(t * hb - 1, 0), c)),
                  pl.BlockSpec((tm, ct), lambda c, t: (t, c)),
                  pl.BlockSpec((HALO, ct), lambda c, t: (jnp.minimum((t + 1) * hb, last_halo), c)),
                  pl.BlockSpec((CONV_TAPS_PAD, ct), lambda c, t: (0, c))],
        out_specs=[pl.BlockSpec((tm, ct), lambda c, t: (t, c)),
                   pl.BlockSpec((CONV_TAPS_PAD, ct), lambda c, t: (0, c))],
        scratch_shapes=[pltpu.VMEM((HALO + tm, ct), F32), pltpu.VMEM((tm + HALO, ct), F32)],
        compiler_params=_params("parallel", "arbitrary"),
    )(u, u, du1, du1, w)


def _adamw(w, g, m, v, *, name):
    R, C = w.shape
    tr = _row_tile(R, max(8, (1 << 18) // C // 8 * 8))

    def body(w_ref, g_ref, m_ref, v_ref, d_ref, nm_ref, nv_ref):
        gv = g_ref[...]
        nm = ADAM_B1 * m_ref[...] + (1.0 - ADAM_B1) * gv
        nv = ADAM_B2 * v_ref[...] + (1.0 - ADAM_B2) * (gv * gv)
        m_hat = nm / (1.0 - ADAM_B1 ** ADAM_STEP)
        v_hat = nv / (1.0 - ADAM_B2 ** ADAM_STEP)
        d_ref[...] = -ADAM_LR * (m_hat / (jnp.sqrt(v_hat) + ADAM_EPS) + ADAM_WD * w_ref[...])
        nm_ref[...] = nm
        nv_ref[...] = nv

    spec = pl.BlockSpec((tr, C), lambda i: (i, 0))
    return pl.pallas_call(
        body, name=name, out_shape=[jax.ShapeDtypeStruct((R, C), F32)] * 3,
        grid=(R // tr,), in_specs=[spec] * 4, out_specs=[spec] * 3,
        compiler_params=_params("parallel"),
    )(w, g, m, v)


ANY = pl.BlockSpec(memory_space=pl.ANY)


def _coords():
    return lax.axis_index("x"), lax.axis_index("y"), lax.axis_index("c")


def _remote(src, dst, ssem, rsem, to):
    return pltpu.make_async_remote_copy(src_ref=src, dst_ref=dst, send_sem=ssem, recv_sem=rsem,
                                        device_id=to, device_id_type=MESH)


def _half(c, rows=PACK_ROWS):
    return pl.ds(pl.multiple_of(c * (rows // 2), 16), rows // 2)


def _gather_weights(wpack, cw):
    def body(w_ref, cw_ref, full_ref, cwf_ref, ssem, rsem):
        x, y, c = _coords()
        q = 2 * x + y
        sib = (x, y, 1 - c)
        chips = [(1 - x, y), (x, 1 - y), (1 - x, 1 - y)]
        mine, other = _half(c, w_ref.shape[0]), _half(1 - c, w_ref.shape[0])
        first = []
        for j, (px, py) in enumerate(chips):
            first.append(_remote(w_ref.at[mine], full_ref.at[q, mine], ssem.at[j], rsem.at[j], (px, py, c)))
            first.append(_remote(cw_ref, cwf_ref.at[q], ssem.at[6 + j], rsem.at[6 + j], (px, py, c)))
        for cp in first:
            cp.start()
        passed = []
        for j, (px, py) in enumerate(chips):
            pq = 2 * px + py
            _remote(w_ref.at[mine], full_ref.at[pq, mine], ssem.at[j], rsem.at[j], (px, py, c)).wait_recv()
            fw = _remote(full_ref.at[pq, mine], full_ref.at[pq, mine], ssem.at[3 + j], rsem.at[3 + j], sib)
            fw.start()
            passed.append(fw)
        for j, (px, py) in enumerate(chips):
            pq = 2 * px + py
            _remote(full_ref.at[pq, other], full_ref.at[pq, other], ssem.at[3 + j], rsem.at[3 + j], sib).wait_recv()
            _remote(cw_ref, cwf_ref.at[pq], ssem.at[6 + j], rsem.at[6 + j], (px, py, c)).wait_recv()
        for cp in first + passed:
            cp.wait_send()

    return pl.pallas_call(
        body, name="gather_weights",
        out_shape=[jax.ShapeDtypeStruct((N_CHIPS,) + wpack.shape, wpack.dtype),
                   jax.ShapeDtypeStruct((N_CHIPS,) + cw.shape, cw.dtype)],
        in_specs=[ANY, ANY], out_specs=[ANY, ANY],
        scratch_shapes=[pltpu.SemaphoreType.DMA((9,)), pltpu.SemaphoreType.DMA((9,))],
    )(wpack, cw)


SEM = pl.BlockSpec(memory_space=pltpu.SEMAPHORE)
HBM = pl.BlockSpec(memory_space=pltpu.HBM)
N_LATE = 6


def _late_peers(x, y, c):
    out = []
    for j, (px, py) in enumerate([(1 - x, y), (x, 1 - y), (1 - x, 1 - y)]):
        for t in (0, 1):
            out.append((2 * j + t, px, py, c if t == 0 else 1 - c))
    return out


def _late_gather_start(w, after):
    rows = w.shape[0]

    def body(w_ref, land_ref, after_ref, *outs):
        sems = outs[:2 * N_LATE]
        token = outs[2 * N_LATE + 2]
        x, y, c = _coords()
        q = 2 * x + y
        mine = _half(c, rows)
        for k, px, py, pc in _late_peers(x, y, c):
            _remote(w_ref.at[mine], land_ref.at[q, mine], sems[k], sems[N_LATE + k], (px, py, pc)).start()
        token[...] = jnp.zeros_like(token)

    res = pl.pallas_call(
        body, name="late_gather_start",
        out_shape=tuple([pltpu.SemaphoreType.DMA(())] * (2 * N_LATE)) + (
            pltpu.HBM(w.shape, w.dtype), pltpu.HBM((N_CHIPS,) + w.shape, w.dtype),
            jax.ShapeDtypeStruct((8, 128), F32)),
        in_specs=(HBM, HBM, ANY),
        out_specs=tuple([SEM] * (2 * N_LATE)) + (HBM, HBM, pl.BlockSpec(memory_space=pltpu.VMEM)),
        input_output_aliases={0: 2 * N_LATE, 1: 2 * N_LATE + 1},
        compiler_params=pltpu.CompilerParams(has_side_effects=pltpu.SideEffectType.DATAFLOW_SIDE_EFFECTING),
    )(pltpu.with_memory_space_constraint(w, pltpu.HBM),
      pltpu.with_memory_space_constraint(lax.empty((N_CHIPS,) + w.shape, w.dtype), pltpu.HBM), after)
    return res[:2 * N_LATE], res[2 * N_LATE], res[2 * N_LATE + 1], res[2 * N_LATE + 2]


def _late_gather_wait(sems, w_thru, land_thru, after):
    rows = w_thru.shape[0]

    def body(w_ref, land_ref, *rest):
        sems = rest[:2 * N_LATE]
        x, y, c = _coords()
        for k, px, py, pc in _late_peers(x, y, c):
            cp = _remote(w_ref.at[_half(pc, rows)], land_ref.at[2 * px + py, _half(pc, rows)],
                         sems[k], sems[N_LATE + k], (px, py, pc))
            cp.wait_send()
            cp.wait_recv()

    return pl.pallas_call(
        body, name="late_gather_wait",
        out_shape=(pltpu.HBM(w_thru.shape, w_thru.dtype), pltpu.HBM(land_thru.shape, land_thru.dtype)),
        in_specs=(HBM, HBM) + tuple([SEM] * (2 * N_LATE)) + (ANY,),
        out_specs=(HBM, HBM), input_output_aliases={0: 0, 1: 1},
        compiler_params=pltpu.CompilerParams(has_side_effects=pltpu.SideEffectType.DATAFLOW_SIDE_EFFECTING),
    )(w_thru, land_thru, *sems, after)[1]


def _swap_halves(gpack):
    def body(g_ref, recv_ref, ssem, rsem):
        x, y, c = _coords()
        sib = (x, y, 1 - c)
        cps = [_remote(g_ref.at[j, _half(1 - c)], recv_ref.at[j], ssem.at[j], rsem.at[j], sib)
               for j in range(N_CHIPS)]
        for cp in cps:
            cp.start()
        for cp in cps:
            cp.wait_recv()
        for cp in cps:
            cp.wait_send()

    return pl.pallas_call(
        body, name="grad_swap_halves",
        out_shape=jax.ShapeDtypeStruct((N_CHIPS, PACK_HALF, 1024), F32),
        in_specs=[ANY], out_specs=ANY,
        scratch_shapes=[pltpu.SemaphoreType.DMA((N_CHIPS,)), pltpu.SemaphoreType.DMA((N_CHIPS,))],
    )(gpack)


def _chip_sum(place, gpack, recv1):
    nb = PACK_HALF // ADD_ROWS

    def body(place_ref, g_ref, r_ref, o_ref):
        o_ref[...] = (g_ref[...] + r_ref[...]).astype(BF16)

    return pl.pallas_call(
        body, name="grad_chip_sum",
        out_shape=jax.ShapeDtypeStruct((N_CHIPS, PACK_HALF, 1024), BF16),
        grid_spec=pltpu.PrefetchScalarGridSpec(
            num_scalar_prefetch=1, grid=(N_CHIPS, nb),
            in_specs=[pl.BlockSpec((None, ADD_ROWS, 1024), lambda j, i, s: (j, s[1] * nb + i, 0)),
                      pl.BlockSpec((None, ADD_ROWS, 1024), lambda j, i, s: (j, i, 0))],
            out_specs=pl.BlockSpec((None, ADD_ROWS, 1024), lambda j, i, s: (j, i, 0))),
        compiler_params=_params("parallel", "parallel"),
    )(place, gpack, recv1)


def _exchange_chips(part):
    def body(p_ref, recv_ref, ssem, rsem):
        x, y, c = _coords()
        chips = [(1 - x, y), (x, 1 - y), (1 - x, 1 - y)]
        cps = [_remote(p_ref.at[2 * px + py], recv_ref.at[k], ssem.at[k], rsem.at[k], (px, py, c))
               for k, (px, py) in enumerate(chips)]
        for cp in cps:
            cp.start()
        for cp in cps:
            cp.wait_recv()
        for cp in cps:
            cp.wait_send()

    return pl.pallas_call(
        body, name="grad_exchange_chips",
        out_shape=jax.ShapeDtypeStruct((3, PACK_HALF, 1024), BF16),
        in_specs=[ANY], out_specs=ANY,
        scratch_shapes=[pltpu.SemaphoreType.DMA((3,)), pltpu.SemaphoreType.DMA((3,))],
    )(part)


def _mesh_sum(place, gpack, recv1, recv2):
    nb = PACK_HALF // ADD_ROWS

    def body(place_ref, g_ref, r1_ref, r2_ref, o_ref):
        o_ref[...] = ((g_ref[...] + r1_ref[...]) + r2_ref[0].astype(F32)
                      + r2_ref[1].astype(F32) + r2_ref[2].astype(F32))

    return pl.pallas_call(
        body, name="grad_mesh_sum",
        out_shape=jax.ShapeDtypeStruct((PACK_HALF, 1024), F32),
        grid_spec=pltpu.PrefetchScalarGridSpec(
            num_scalar_prefetch=1, grid=(nb,),
            in_specs=[pl.BlockSpec((None, ADD_ROWS, 1024), lambda i, s: (s[0], s[1] * nb + i, 0)),
                      pl.BlockSpec((None, ADD_ROWS, 1024), lambda i, s: (s[0], i, 0)),
                      pl.BlockSpec((3, ADD_ROWS, 1024), lambda i, s: (0, i, 0))],
            out_specs=pl.BlockSpec((ADD_ROWS, 1024), lambda i, s: (i, 0))),
        compiler_params=_params("parallel"),
    )(place, gpack, recv1, recv2)


def _join_halves(red):
    def body(r_ref, o_ref, ssem, rsem):
        x, y, c = _coords()
        cp = _remote(r_ref, o_ref, ssem, rsem, (x, y, 1 - c))
        cp.start()
        cp.wait_recv()
        cp.wait_send()

    return pl.pallas_call(
        body, name="grad_join_halves",
        out_shape=jax.ShapeDtypeStruct((PACK_HALF, 1024), F32),
        in_specs=[ANY], out_specs=ANY,
        scratch_shapes=[pltpu.SemaphoreType.DMA, pltpu.SemaphoreType.DMA],
    )(red)


def _share_small(svec):
    def body(s_ref, all_ref, ssem, rsem):
        x, y, c = _coords()
        me = 4 * x + 2 * y + c
        peers = []
        for r in range(1, N_DEV):
            px = 1 - x if r & 4 else x
            py = 1 - y if r & 2 else y
            pc = 1 - c if r & 1 else c
            peers.append((px, py, pc))
        cps = [_remote(s_ref, all_ref.at[me], ssem.at[k], rsem.at[k], p) for k, p in enumerate(peers)]
        for cp in cps:
            cp.start()
        for k, (px, py, pc) in enumerate(peers):
            _remote(s_ref, all_ref.at[4 * px + 2 * py + pc], ssem.at[k], rsem.at[k], (px, py, pc)).wait_recv()
        for cp in cps:
            cp.wait_send()

    return pl.pallas_call(
        body, name="share_small",
        out_shape=jax.ShapeDtypeStruct((N_DEV,) + svec.shape, F32),
        in_specs=[ANY], out_specs=ANY,
        scratch_shapes=[pltpu.SemaphoreType.DMA((N_DEV - 1,)), pltpu.SemaphoreType.DMA((N_DEV - 1,))],
    )(svec)


def _sum_small(allv):
    def body(a_ref, o_ref):
        acc = a_ref[0]
        for d in range(1, N_DEV):
            acc = acc + a_ref[d]
        o_ref[...] = acc

    return pl.pallas_call(
        body, name="sum_small",
        out_shape=jax.ShapeDtypeStruct(allv.shape[1:], F32),
    )(allv)


EARLY_ROWS = 1472


def _unpack_early(full):
    w = {}
    w["w_in"] = full[:, :1200].reshape(4, 1024, 1200).transpose(1, 0, 2).reshape(1024, 4800)
    w["w_uq"] = full[:, 1200:1344].reshape(4, 384, 384).transpose(1, 0, 2).reshape(384, 1536)
    w["w_uk"] = full[:, 1344:1408].reshape(256, 1024)
    w["w_uv"] = full[:, 1408:1472].reshape(256, 1024)
    return w


def _unpack_late(full):
    out, off = {}, 0
    for name, rows in PACK[4:]:
        out[name] = full[:, off:off + rows]
        off += rows
    w = {}
    for n in ("w_o_attn", "w_pw2", "w_out"):
        w[n] = out[n].reshape(1024, 1024)
    w["w_ff1"] = out["w_ff1"].transpose(1, 0, 2).reshape(1024, 4096)
    w["w_ff2"] = out["w_ff2"].reshape(4096, 1024)
    return w


def _pack_grads(g):
    parts = [
        g["w_in"].reshape(1024, 4, 1200).transpose(1, 0, 2).reshape(4, 1200, 1024),
        g["w_uq"].reshape(384, 4, 384).transpose(1, 0, 2).reshape(4, 144, 1024),
        g["w_uk"].reshape(4, 64, 1024),
        g["w_uv"].reshape(4, 64, 1024),
        g["w_o_attn"].reshape(4, 256, 1024),
        g["w_pw2"].reshape(4, 256, 1024),
        g["w_out"].reshape(4, 256, 1024),
        g["w_ff1"].reshape(1024, 4, 1024).transpose(1, 0, 2),
        g["w_ff2"].reshape(4, 1024, 1024),
    ]
    used = sum(r for _, r in PACK)
    parts.append(jnp.zeros((4, PACK_ROWS - used, 1024), F32))
    return jnp.concatenate(parts, axis=1)


def _pad_w_in(w):
    return jnp.concatenate([w[:, :704], jnp.zeros((1024, Z_SMALL - 704), w.dtype), w[:, 704:]], axis=1)


def _pad_w_uq(w):
    w = w.reshape(Q_RANK, N_HEADS, NOPE + ROPE)
    return jnp.pad(w, ((0, 0), (0, 0), (0, QK_PAD - NOPE - ROPE))).reshape(Q_RANK, N_HEADS * QK_PAD)


def _local_step(xs, pos, tgt, W, late_weights, cw, vec):
    D = D_MODEL
    w_in = _pad_w_in(W["w_in"])
    w_uq = _pad_w_uq(W["w_uq"])
    w_ukv = jnp.concatenate([W["w_uk"], W["w_uv"]], axis=1)
    inv = ROPE_THETA ** (-jnp.arange(0, ROPE, 2, dtype=F32) / ROPE)
    inv_freq = jnp.concatenate([inv, inv, jnp.zeros((128 - ROPE,), F32)])[None, :]

    (h,) = _rowwise(lambda x, g: (_rms(x, g),), [(xs, D, 0)], [vec["norm_mix_pre"]], [(D, BF16)],
                    name="rms_pre")
    z = _mm(h, w_in, name="mm_in")

    def qkv_prep(zs, p, qn, kvn, invf):
        ang = p.astype(F32) * invf
        cosv, sinv = jnp.cos(ang), jnp.sin(ang)
        ln = lax.broadcasted_iota(jnp.int32, ang.shape, 1)
        cs = jnp.where(ln < ROPE, cosv, 0.0)
        sa = jnp.where((ln >= ROPE // 2) & (ln < ROPE), sinv, 0.0)
        sb = jnp.where(ln < ROPE // 2, -sinv, 0.0)
        cqn = _rms(zs[:, :Q_RANK], qn)
        ckvn = _rms(zs[:, Q_RANK:Q_RANK + KV_RANK], kvn)
        kr = _rope(zs[:, 640:768], cs, sa, sb)
        return cqn, ckvn, kr, cs, sa, sb

    cqn, ckvn, krot, cs, sa, sb = _rowwise(
        qkv_prep, [(z, Z_SMALL, 0), (pos, 1, 0)], [vec["q_norm"], vec["kv_norm"], inv_freq],
        [(Q_RANK, BF16), (KV_RANK, BF16), (128, BF16), (128, F32), (128, F32), (128, F32)],
        name="qkv_prep")
    q_pre = _mm(cqn, w_uq, name="mm_uq")
    kv = _mm(ckvn, w_ukv, name="mm_ukv")

    def qk_rope(qp, kn, vv, kr, cs_, sa_, sb_):
        qs, ks, vs = [], [], []
        ones = jnp.ones((qp.shape[0], NOPE), F32)
        for hd in range(N_HEADS):
            qs.append(qp[:, hd * QK_PAD:hd * QK_PAD + NOPE])
            qs.append(_rope(qp[:, hd * QK_PAD + NOPE:(hd + 1) * QK_PAD], cs_, sa_, sb_))
            ks.append(kn[:, hd * NOPE:(hd + 1) * NOPE])
            ks.append(kr.astype(F32))
            vs.append(vv[:, hd * NOPE:(hd + 1) * NOPE])
            vs.append(ones)
        return (jnp.concatenate(qs, axis=1) * Q_SCALE, jnp.concatenate(ks, axis=1),
                jnp.concatenate(vs, axis=1))

    q, k, v = _rowwise(
        qk_rope, [(q_pre, 2048, 0), (kv, 1024, 0), (kv, 1024, 1), (krot, 128, 0), (cs, 128, 0),
                  (sa, 128, 0), (sb, 128, 0)], [],
        [(2048, BF16), (2048, BF16), (2048, BF16)], name="qk_rope")
    attn, lse = _flash_fwd(q, k, v)
    W = {**W, **late_weights(attn)}
    y_attn = _mm(attn, W["w_o_attn"], name="mm_o_attn")

    (u0,) = _rowwise(lambda a, b: (a * _sigmoid(b),), [(z, D, 1), (z, D, 2)], [], [(D, F32)], name="glu")
    u1 = _conv_fwd(u0, cw, vec["conv_b"])

    def ln_parts(u, g, b):
        mu = jnp.mean(u, axis=-1, keepdims=True)
        xc = u - mu
        rs = lax.rsqrt(jnp.mean(xc * xc, axis=-1, keepdims=True) + EPS)
        nh = xc * rs
        return nh, rs, nh * g + b

    def ln_silu(u, g, b):
        y = ln_parts(u, g, b)[2]
        return (y * _sigmoid(y),)

    (u3,) = _rowwise(ln_silu, [(u1, D, 0)], [vec["conv_ln_g"], vec["conv_ln_b"]], [(D, BF16)],
                     name="ln_silu")
    y_conv = _mm(u3, W["w_pw2"], name="mm_pw2")

    def merge(ga, gc, ya, yc, b2):
        return (_sigmoid(ga) * ya + _sigmoid(gc) * (yc + b2),)

    (merged,) = _rowwise(merge, [(z, D, 3), (z, D, 4), (y_attn, D, 0), (y_conv, D, 0)], [vec["b_pw2"]],
                         [(D, BF16)], name="merge")
    mo = _mm(merged, W["w_out"], name="mm_out")

    def post1(x, m, g1, g2):
        x1 = x + _rms(m, g1)
        return x1, _rms(x1, g2)

    x1, h2 = _rowwise(post1, [(xs, D, 0), (mo, D, 0)], [vec["norm_mix_post"], vec["norm_mlp_pre"]],
                      [(D, F32), (D, BF16)], name="post_mix")

    def sqrelu(acc):
        r = jnp.maximum(acc, 0.0)
        return r * r, r

    a2, r1 = _mm(h2, W["w_ff1"], name="mm_ff1", epilogue=sqrelu, out_dtypes=(BF16, BF16))
    f = _mm(a2, W["w_ff2"], name="mm_ff2")

    def post2(x1_, f_, t_, g):
        e = x1_ + _rms(f_, g) - t_
        dy = e * (1.0 / D)
        df, dg = _rms_bwd(f_, g, dy)
        loss = jnp.broadcast_to(jnp.sum(e * e, keepdims=True) * (0.5 / D), (1, 128))
        return dy, df, dg, loss

    dy, df, g_norm_mlp_post, loss = _rowwise(
        post2, [(x1, D, 0), (f, D, 0), (tgt, D, 0)], [vec["norm_mlp_post"]],
        [(D, F32), (D, BF16)], [(1, D), (1, 128)], name="post_mlp_loss")
    g_w_ff2 = _mm(a2, df, ta=True, name="mm_dw_ff2")
    df1 = _mm(df, W["w_ff2"], tb=True, name="mm_df1", extras=[r1], out_dtypes=(BF16,),
              epilogue=lambda acc, r: (acc * (2.0 * r.astype(F32)),))
    g_w_ff1 = _mm(h2, df1, ta=True, name="mm_dw_ff1")
    dh2 = _mm(df1, W["w_ff1"], tb=True, name="mm_dh2")

    def bwd_mid(dy_, dh2_, x1_, m, g2, g1):
        d1, dg2 = _rms_bwd(x1_, g2, dh2_)
        dx1_ = dy_ + d1
        dm, dg1 = _rms_bwd(m, g1, dx1_)
        return dx1_, dm, dg2, dg1

    dx1, dmo, g_norm_mlp_pre, g_norm_mix_post = _rowwise(
        bwd_mid, [(dy, D, 0), (dh2, D, 0), (x1, D, 0), (mo, D, 0)],
        [vec["norm_mlp_pre"], vec["norm_mix_post"]], [(D, F32), (D, BF16)], [(1, D), (1, D)],
        name="bwd_mid")
    g_w_out = _mm(merged, dmo, ta=True, name="mm_dw_out")
    dmerged = _mm(dmo, W["w_out"], tb=True, name="mm_dmerged")

    def dmerge(dm, ga, gc, ya, yc, b2):
        sga, sgc = _sigmoid(ga), _sigmoid(gc)
        dya = dm * sga
        dyc = dm * sgc
        dga = dm * ya * sga * (1.0 - sga)
        dgc = dm * (yc + b2) * sgc * (1.0 - sgc)
        return dya, dyc, dga, dgc, jnp.sum(dyc, axis=0, keepdims=True)

    dya, dyc, dga, dgc, g_b_pw2 = _rowwise(
        dmerge, [(dmerged, D, 0), (z, D, 3), (z, D, 4), (y_attn, D, 0), (y_conv, D, 0)], [vec["b_pw2"]],
        [(D, BF16), (D, BF16), (D, BF16), (D, BF16)], [(1, D)], name="dmerge")

    g_w_o = _mm(attn, dya, ta=True, name="mm_dw_o_attn")
    dattn = _mm(dya, W["w_o_attn"], tb=True, out_dtype=BF16, name="mm_dattn")

    def delta_fn(do_, o_):
        do32 = do_.astype(F32)
        pr = do32 * o_.astype(F32)
        ln = lax.broadcasted_iota(jnp.int32, (pr.shape[0], NOPE), 1)
        cols = []
        for hd in range(N_HEADS):
            dl = jnp.sum(pr[:, hd * NOPE:(hd + 1) * NOPE], axis=-1, keepdims=True)
            hi = dl.astype(BF16).astype(F32)
            cols.append(do32[:, hd * NOPE:(hd + 1) * NOPE])
            cols.append(jnp.where(ln == 0, -hi, jnp.where(ln == 1, hi - dl, 0.0)))
        return (jnp.concatenate(cols, axis=1),)

    (do_ext,) = _rowwise(delta_fn, [(dattn, D, 0), (attn, D, 0)], [], [(2048, BF16)], name="attn_delta")
    dq, dk, dv = _flash_bwd(q, k, v, do_ext, lse)

    def unrope(dq_, dk_, dv_, cs_, sa_, sb_):
        qs, kn = [], []
        dkr = jnp.zeros_like(cs_)
        for hd in range(N_HEADS):
            qs.append(dq_[:, hd * QK_PAD:hd * QK_PAD + NOPE] * SCALE)
            qs.append(_rope_t(dq_[:, hd * QK_PAD + NOPE:(hd + 1) * QK_PAD] * SCALE, cs_, sa_, sb_))
            kn.append(dk_[:, hd * QK_PAD:hd * QK_PAD + NOPE] * K_UNSCALE)
            dkr = dkr + dk_[:, hd * QK_PAD + NOPE:(hd + 1) * QK_PAD]
        return (jnp.concatenate(qs, axis=1), jnp.concatenate(kn + [dv_], axis=1),
                _rope_t(dkr * K_UNSCALE, cs_, sa_, sb_))

    dq_pre, dkv, dkr = _rowwise(
        unrope, [(dq, 2048, 0), (dk, 2048, 0), (dv, 1024, 0), (cs, 128, 0), (sa, 128, 0), (sb, 128, 0)], [],
        [(2048, BF16), (2048, BF16), (128, F32)], name="unrope")
    g_w_uq = _mm(cqn, dq_pre, ta=True, name="mm_dw_uq")
    dcqn = _mm(dq_pre, w_uq, tb=True, name="mm_dcqn")
    g_w_ukv = _mm(ckvn, dkv, ta=True, name="mm_dw_ukv")
    dckvn = _mm(dkv, w_ukv, tb=True, name="mm_dckvn")

    def small_bwd(zs, dcq_, dckv_, dkr_, qn, kvn):
        dcq, dqn = _rms_bwd(zs[:, :Q_RANK], qn, dcq_)
        dckv, dkvn = _rms_bwd(zs[:, Q_RANK:Q_RANK + KV_RANK], kvn, dckv_)
        pad = jnp.zeros((zs.shape[0], Z_SMALL - 768), F32)
        return jnp.concatenate([dcq, dckv, dkr_, pad], axis=1), dqn, dkvn

    dz_small, g_q_norm, g_kv_norm = _rowwise(
        small_bwd, [(z, Z_SMALL, 0), (dcqn, Q_RANK, 0), (dckvn, KV_RANK, 0), (dkr, 128, 0)],
        [vec["q_norm"], vec["kv_norm"]], [(Z_SMALL, BF16)], [(1, Q_RANK), (1, KV_RANK)], name="small_bwd")

    g_w_pw2 = _mm(u3, dyc, ta=True, name="mm_dw_pw2")
    du3 = _mm(dyc, W["w_pw2"], tb=True, name="mm_du3")

    def ln_silu_bwd(d3, u, g, b):
        nh, rs, y = ln_parts(u, g, b)
        sg = _sigmoid(y)
        dyv = d3 * (sg * (1.0 + y * (1.0 - sg)))
        dnh = dyv * g
        du = rs * (dnh - jnp.mean(dnh, axis=-1, keepdims=True)
                   - nh * jnp.mean(dnh * nh, axis=-1, keepdims=True))
        return (du, jnp.sum(dyv * nh, axis=0, keepdims=True), jnp.sum(dyv, axis=0, keepdims=True),
                jnp.sum(du, axis=0, keepdims=True))

    du1, g_ln_g, g_ln_b, g_conv_b = _rowwise(
        ln_silu_bwd, [(du3, D, 0), (u1, D, 0)], [vec["conv_ln_g"], vec["conv_ln_b"]], [(D, F32)],
        [(1, D), (1, D), (1, D)], name="ln_silu_bwd")
    du0, g_conv_w = _conv_bwd(u0, du1, cw)

    def glu_bwd(d0, a, b):
        sg = _sigmoid(b)
        return d0 * sg, d0 * a * sg * (1.0 - sg)

    dza, dzb = _rowwise(glu_bwd, [(du0, D, 0), (z, D, 1), (z, D, 2)], [], [(D, BF16), (D, BF16)],
                        name="glu_bwd")

    dz = jnp.concatenate([dz_small, dza, dzb, dga, dgc], axis=1)
    g_w_in = _mm(h, dz, ta=True, name="mm_dw_in")
    dh = _mm(dz, w_in, tb=True, name="mm_dh")

    def final(dx1_, dh_, x, g):
        d, dg = _rms_bwd(x, g, dh_)
        return dx1_ + d, dg

    grad_x, g_norm_mix_pre = _rowwise(final, [(dx1, D, 0), (dh, D, 0), (xs, D, 0)], [vec["norm_mix_pre"]],
                                      [(D, F32)], [(1, D)], name="bwd_final")

    gw = {
        "w_in": jnp.concatenate([g_w_in[:, :704], g_w_in[:, Z_SMALL:]], axis=1),
        "w_uq": g_w_uq.reshape(Q_RANK, N_HEADS, QK_PAD)[:, :, :NOPE + ROPE].reshape(Q_RANK, 1536),
        "w_uk": g_w_ukv[:, :1024], "w_uv": g_w_ukv[:, 1024:],
        "w_o_attn": g_w_o, "w_pw2": g_w_pw2, "w_out": g_w_out, "w_ff1": g_w_ff1, "w_ff2": g_w_ff2,
    }
    gs = {
        "norm_mix_pre": g_norm_mix_pre, "conv_b": g_conv_b, "conv_ln_g": g_ln_g, "conv_ln_b": g_ln_b,
        "b_pw2": g_b_pw2, "norm_mix_post": g_norm_mix_post, "norm_mlp_pre": g_norm_mlp_pre,
        "norm_mlp_post": g_norm_mlp_post, "q_norm": g_q_norm, "kv_norm": g_kv_norm,
    }
    return grad_x, gw, gs, g_conv_w, loss


def _row1024(a):
    return jnp.pad(a, ((0, 0), (0, 1024 - a.shape[1])))


BIG = ("w_in", "w_uq", "w_uk", "w_uv", "w_o_attn", "w_pw2", "w_out", "w_ff1", "w_ff2")
ORDER = ("norm_mix_pre", "w_in", "q_norm", "w_uq", "kv_norm", "w_uk", "w_uv", "w_o_attn", "conv_w",
         "conv_b", "conv_ln_g", "conv_ln_b", "w_pw2", "b_pw2", "w_out", "norm_mix_post", "norm_mlp_pre",
         "w_ff1", "w_ff2", "norm_mlp_post")


def kernel(x, positions, norm_mix_pre, w_in, q_norm, w_uq, kv_norm, w_uk, w_uv, w_o_attn, conv_w, conv_b, conv_ln_g, conv_ln_b, w_pw2, b_pw2, w_out, norm_mix_post, norm_mlp_pre, w_ff1, w_ff2, norm_mlp_post, loss_target, m_norm_mix_pre, m_w_in, m_q_norm, m_w_uq, m_kv_norm, m_w_uk, m_w_uv, m_w_o_attn, m_conv_w, m_conv_b, m_conv_ln_g, m_conv_ln_b, m_w_pw2, m_b_pw2, m_w_out, m_norm_mix_post, m_norm_mlp_pre, m_w_ff1, m_w_ff2, m_norm_mlp_post, v_norm_mix_pre, v_w_in, v_q_norm, v_w_uq, v_kv_norm, v_w_uk, v_w_uv, v_w_o_attn, v_conv_w, v_conv_b, v_conv_ln_g, v_conv_ln_b, v_w_pw2, v_b_pw2, v_w_out, v_norm_mix_post, v_norm_mlp_pre, v_w_ff1, v_w_ff2, v_norm_mlp_post):
    args = dict(locals())
    wts = {n: args[n] for n in ORDER}
    mom = {n: args["m_" + n] for n in ORDER}
    var = {n: args["v_" + n] for n in ORDER}
    T = x.shape[1]
    place = jnp.stack([2 * lax.axis_index("x") + lax.axis_index("y"), lax.axis_index("c")]).astype(jnp.int32)

    used = sum(r for _, r in PACK)
    shards = [wts[n].astype(BF16).reshape(-1, 1024) for n in BIG]
    wpack = jnp.concatenate(shards + [jnp.zeros((PACK_ROWS - used, 1024), BF16)], axis=0)
    cw_own = jnp.pad(conv_w.reshape(CONV_W, -1), ((0, CONV_TAPS_PAD - CONV_W), (0, 0)))
    w_early, w_late = wpack[:EARLY_ROWS], wpack[EARLY_ROWS:]
    full, cw_all = _gather_weights(w_early, cw_own)
    full = lax.dynamic_update_slice(full, w_early[None], (place[0], 0, 0))
    cw_all = lax.dynamic_update_slice(cw_all, cw_own[None], (place[0], 0, 0))
    W = _unpack_early(full)
    cw = cw_all.transpose(1, 0, 2).reshape(CONV_TAPS_PAD, D_MODEL)
    sems, w_thru, land, token = _late_gather_start(w_late, full)
    vec = {n: wts[n] for n in SMALL_VECS}
    vec["norm_mix_pre"] = vec["norm_mix_pre"] + token[:1, :1]

    def late_weights(after):
        landed = _late_gather_wait(sems, w_thru, land, after)
        return _unpack_late(lax.dynamic_update_slice(landed, w_late[None], (place[0], 0, 0)))

    grad_x, gw, gs, g_conv_w, loss = _local_step(x[0], positions.reshape(T, 1), loss_target[0], W,
                                                 late_weights, cw, vec)

    gpack = _pack_grads(gw)
    recv1 = _swap_halves(gpack)
    part = _chip_sum(place, gpack, recv1)
    recv2 = _exchange_chips(part)
    red = _mesh_sum(place, gpack, recv1, recv2)
    theirs = _join_halves(red)
    gfull = jnp.where(place[1] == 0, jnp.concatenate([red, theirs], axis=0),
                      jnp.concatenate([theirs, red], axis=0))

    svec = jnp.concatenate(
        [_row1024(gs[n]) for n in SMALL_VECS] + [_row1024(loss)]
        + [jnp.zeros((SMALL_CONVW_ROW - SMALL_LOSS_ROW - 1, 1024), F32), g_conv_w], axis=0)
    me = 2 * place[0] + place[1]
    ssum = _sum_small(lax.dynamic_update_slice(_share_small(svec), svec[None], (me, 0, 0)))

    grads, off = {}, 0
    for name, rows in PACK:
        grads[name] = gfull[off:off + rows].reshape(wts[name].shape)
        off += rows
    for r, n in enumerate(SMALL_VECS):
        grads[n] = ssum[r:r + 1, :wts[n].shape[1]]
    cw_sum = ssum[SMALL_CONVW_ROW:SMALL_CONVW_ROW + CONV_W]
    grads["conv_w"] = lax.dynamic_slice_in_dim(cw_sum, place[0] * 256, 256, axis=1).reshape(conv_w.shape)
    loss_out = ssum[SMALL_LOSS_ROW, 0]

    delta, new_m, new_v = {}, {}, {}

    def as2d(a):
        return a.reshape(-1, a.shape[-1]) if a.ndim != 4 or a.shape[2] == 1 else a.reshape(a.shape[1], -1)

    for n in BIG + ("conv_w",):
        shp = wts[n].shape
        outs = _adamw(as2d(wts[n]), as2d(grads[n]), as2d(mom[n]), as2d(var[n]), name="adamw_" + n)
        delta[n], new_m[n], new_v[n] = (o.reshape(shp) for o in outs)
    small = [jnp.concatenate([_row1024(t[n]) for n in SMALL_VECS], axis=0) for t in (wts, grads, mom, var)]
    outs = _adamw(*small, name="adamw_small")
    for r, n in enumerate(SMALL_VECS):
        delta[n], new_m[n], new_v[n] = (o[r:r + 1, :wts[n].shape[1]] for o in outs)

    return (loss_out, grad_x[None], *[grads[n] for n in ORDER], *[delta[n] for n in ORDER],
            *[new_m[n] for n in ORDER], *[new_v[n] for n in ORDER])
```

```python
import functools

import numpy as np
import jax
import jax.numpy as jnp
from jax import lax
from jax.experimental import pallas as pl
from jax.experimental.pallas import tpu as pltpu

F32 = jnp.float32
BF16 = jnp.bfloat16
MESH = pl.DeviceIdType.MESH

D_MODEL = 1024
N_HEADS = 8
NOPE = 128
ROPE = 64
QK_PAD = 256
Q_RANK = 384
KV_RANK = 256
CONV_W = 31
CONV_TAPS_PAD = 32
HALO = 32
D_FF = 4096
EPS = 1e-6
ROPE_THETA = 10000.0
SCALE = float((NOPE + ROPE) ** -0.5)
Q_SCALE = SCALE * float(np.log2(np.e))
K_UNSCALE = float(np.log(2.0))
Z_SMALL = 1024
Z_COLS = Z_SMALL + 4 * D_MODEL
N_CHIPS = 4
N_DEV = 8

ADAM_LR = 0.001
ADAM_B1 = 0.9
ADAM_B2 = 0.999
ADAM_EPS = 1e-08
ADAM_WD = 0.01
ADAM_STEP = 10

VMEM_LIMIT = 48 * 1024 * 1024
NEG = -1e30

EARLY = (("w_in", 1200), ("w_uq", 144), ("w_uk", 64), ("w_uv", 64))
LATE = (("w_o_attn", 256), ("w_pw2", 256), ("w_out", 256), ("w_ff1", 1024), ("w_ff2", 1024))
EARLY_ROWS = 1536
LATE_ROWS = 2880
SUM_TILE_EARLY = 256
SUM_TILE_LATE = 288

SMALL_VECS = ("norm_mix_pre", "conv_b", "conv_ln_g", "conv_ln_b", "b_pw2", "norm_mix_post",
              "norm_mlp_pre", "norm_mlp_post", "q_norm", "kv_norm")
SMALL_LOSS_ROW = 10
SMALL_CONVW_ROW = 16
SMALL_ROWS = 48


ANY = pl.BlockSpec(memory_space=pl.ANY)


def _params(*sem):
    return pltpu.CompilerParams(dimension_semantics=sem, vmem_limit_bytes=VMEM_LIMIT)


def _tile(n, cap):
    if n <= cap:
        return n
    t = (cap // 128) * 128
    while n % t:
        t -= 128
    return t


def _row_tile(n, cap):
    if n <= cap:
        return n
    t = (cap // 8) * 8
    while n % t:
        t -= 8
    return t


def _mm(a, b, *, name, ta=False, tb=False, out_dtype=F32, caps=(1024, 1024, 1024), extras=(),
        epilogue=None, out_dtypes=None, after=()):
    if ta:
        K, M = a.shape
    else:
        M, K = a.shape
    if tb:
        N, K2 = b.shape
    else:
        K2, N = b.shape
    assert K == K2, (a.shape, b.shape, ta, tb)
    tm, tn, tk = _tile(M, caps[0]), _tile(N, caps[1]), _tile(K, caps[2])
    nk = K // tk
    dims = (((0 if ta else 1,), (1 if tb else 0,)), ((), ()))
    out_dtypes = tuple(out_dtypes or (out_dtype,))
    n_ex, n_out = len(extras), len(out_dtypes)

    def body(*refs):
        a_ref, b_ref = refs[:2]
        ex_refs = refs[2:2 + n_ex]
        o_refs = refs[2 + n_ex + len(after):2 + n_ex + len(after) + n_out]
        k = pl.program_id(2)
        part = lax.dot_general(a_ref[...].astype(BF16), b_ref[...].astype(BF16), dims,
                               preferred_element_type=F32)

        def finish(acc):
            res = epilogue(acc, *[r[...] for r in ex_refs]) if epilogue else (acc,)
            for o_ref, r in zip(o_refs, res):
                o_ref[...] = r.astype(o_ref.dtype)

        if nk == 1:
            finish(part)
        else:
            acc_ref = refs[-1]

            @pl.when(k == 0)
            def _():
                acc_ref[...] = part

            @pl.when((k > 0) & (k < nk - 1))
            def _():
                acc_ref[...] += part

            @pl.when(k == nk - 1)
            def _():
                finish(acc_ref[...] + part)

    a_spec = (pl.BlockSpec((tk, tm), lambda i, j, k: (k, i)) if ta
              else pl.BlockSpec((tm, tk), lambda i, j, k: (i, k)))
    b_spec = (pl.BlockSpec((tn, tk), lambda i, j, k: (j, k)) if tb
              else pl.BlockSpec((tk, tn), lambda i, j, k: (k, j)))
    o_spec = pl.BlockSpec((tm, tn), lambda i, j, k: (i, j))
    res = pl.pallas_call(
        body, name=name,
        out_shape=[jax.ShapeDtypeStruct((M, N), dt) for dt in out_dtypes],
        grid=(M // tm, N // tn, nk),
        in_specs=[a_spec, b_spec] + [o_spec] * n_ex + [ANY] * len(after),
        out_specs=[o_spec] * n_out,
        scratch_shapes=[pltpu.VMEM((tm, tn), F32)] if nk > 1 else [],
        compiler_params=_params("parallel", "parallel", "arbitrary"),
    )(a, b, *extras, *after)
    return res[0] if n_out == 1 else res


def _rowwise(fn, rows, consts, outs, reds=(), *, name, tm=256, after=()):
    T = rows[0][0].shape[0]
    tm = min(tm, T)
    n_row, n_const, n_out = len(rows), len(consts), len(outs)

    def body(*refs):
        i = pl.program_id(0)
        vals = [r[...] for r in refs[:n_row + n_const]]
        res = fn(*vals)
        out_refs = refs[n_row + n_const + len(after):]
        for k in range(n_out):
            out_refs[k][...] = res[k].astype(out_refs[k].dtype)
        for k in range(len(reds)):
            ref = out_refs[n_out + k]

            @pl.when(i == 0)
            def _(ref=ref):
                ref[...] = jnp.zeros_like(ref)

            ref[...] += res[n_out + k]

    in_specs = [pl.BlockSpec((tm, w), functools.partial(lambda i, cb: (i, cb), cb=cb))
                for (_, w, cb) in rows]
    in_specs += [pl.BlockSpec(c.shape, lambda i: (0, 0)) for c in consts]
    in_specs += [ANY] * len(after)
    out_specs = [pl.BlockSpec((tm, w), lambda i: (i, 0)) for (w, _) in outs]
    out_specs += [pl.BlockSpec(s, lambda i: (0, 0)) for s in reds]
    out_shape = [jax.ShapeDtypeStruct((T, w), dt) for (w, dt) in outs]
    out_shape += [jax.ShapeDtypeStruct(s, F32) for s in reds]
    return pl.pallas_call(
        body, name=name, out_shape=out_shape, grid=(T // tm,),
        in_specs=in_specs, out_specs=out_specs,
        compiler_params=_params("arbitrary"),
    )(*[r[0] for r in rows], *consts, *after)


def _rms(x, g):
    r = lax.rsqrt(jnp.mean(x * x, axis=-1, keepdims=True) + EPS)
    return x * r * g


def _rms_bwd(x, g, dy):
    r = lax.rsqrt(jnp.mean(x * x, axis=-1, keepdims=True) + EPS)
    n = x * r
    dyg = dy * g
    dx = r * (dyg - n * jnp.mean(dyg * n, axis=-1, keepdims=True))
    return dx, jnp.sum(dy * n, axis=0, keepdims=True)


def _sigmoid(x):
    return 1.0 / (1.0 + jnp.exp(-x))


def _rope(x, cs, sa, sb):
    return x * cs + pltpu.roll(x, 32, 1) * sa + pltpu.roll(x, 96, 1) * sb


def _rope_t(dy, cs, sa, sb):
    return dy * cs + pltpu.roll(dy * sa, 96, 1) + pltpu.roll(dy * sb, 32, 1)


def _causal_mask(t):
    row = lax.broadcasted_iota(jnp.int32, (t, t), 0)
    col = lax.broadcasted_iota(jnp.int32, (t, t), 1)
    return col <= row


NT_DIMS = (((1,), (1,)), ((), ()))
TN_DIMS = (((0,), (0,)), ((), ()))


def _flash_fwd(q, k, v, *, tq=512):
    T = q.shape[0]
    tq = min(tq, T)

    def body(q_ref, k_ref, v_ref, o_ref, lse_ref):
        i = pl.program_id(1)
        qv = q_ref[...]

        def step(j, carry, masked):
            m, acc = carry
            sl = pl.ds(pl.multiple_of(j * tq, tq), tq)
            s = lax.dot_general(qv, k_ref[sl, :], NT_DIMS, preferred_element_type=F32)
            if masked:
                s = jnp.where(_causal_mask(tq), s, NEG)
            m_new = jnp.maximum(m, jnp.max(s, axis=-1, keepdims=True))
            p = jnp.exp2(s - m_new)
            acc = jnp.exp2(m - m_new) * acc + jnp.dot(p.astype(BF16), v_ref[sl, :],
                                                      preferred_element_type=F32)
            return m_new, acc

        init = (jnp.full((tq, 1), NEG, F32), jnp.zeros((tq, QK_PAD), F32))
        carry = lax.fori_loop(0, i, lambda j, c: step(j, c, False), init)
        m, acc = step(i, carry, True)
        l = acc[:, NOPE:]
        o_ref[...] = (acc[:, :NOPE] / l).astype(BF16)
        lse_ref[...] = m + jnp.log2(l)

    return pl.pallas_call(
        body, name="flash_fwd",
        out_shape=[jax.ShapeDtypeStruct((T, N_HEADS * NOPE), BF16),
                   jax.ShapeDtypeStruct((T, N_HEADS * NOPE), F32)],
        grid=(N_HEADS, T // tq),
        in_specs=[pl.BlockSpec((tq, QK_PAD), lambda h, i: (i, h)),
                  pl.BlockSpec((T, QK_PAD), lambda h, i: (0, h)),
                  pl.BlockSpec((T, QK_PAD), lambda h, i: (0, h))],
        out_specs=[pl.BlockSpec((tq, NOPE), lambda h, i: (i, h)),
                   pl.BlockSpec((tq, NOPE), lambda h, i: (i, h))],
        compiler_params=_params("parallel", "arbitrary"),
    )(q, k, v)


def _flash_bwd(q, k, v, do, lse, *, tq=512):
    T = q.shape[0]
    tq = min(tq, T)
    nq = T // tq

    def body(q_ref, k_ref, v_ref, do_ref, lse_ref, dq_ref, dk_ref, dv_ref):
        j = pl.program_id(1)

        @pl.when(j == 0)
        def _():
            dq_ref[...] = jnp.zeros_like(dq_ref)

        kj = k_ref[...]
        vj = v_ref[...]

        def step(i, carry, masked):
            dk, dv = carry
            sl = pl.ds(pl.multiple_of(i * tq, tq), tq)
            qi = q_ref[sl, :]
            doi = do_ref[sl, :]
            s = lax.dot_general(qi, kj, NT_DIMS, preferred_element_type=F32)
            p = jnp.exp2(s - lse_ref[sl, :][:, :1])
            if masked:
                p = jnp.where(_causal_mask(tq), p, 0.0)
            dv = dv + lax.dot_general(p.astype(BF16), doi, TN_DIMS, preferred_element_type=F32)
            ds = (p * lax.dot_general(doi, vj, NT_DIMS, preferred_element_type=F32)).astype(BF16)
            dk = dk + lax.dot_general(ds, qi, TN_DIMS, preferred_element_type=F32)
            dq_ref[sl, :] += jnp.dot(ds, kj, preferred_element_type=F32)
            return dk, dv

        zero = jnp.zeros((tq, QK_PAD), F32)
        carry = step(j, (zero, zero), True)
        dk, dv = lax.fori_loop(j + 1, nq, lambda i, c: step(i, c, False), carry)
        dk_ref[...] = dk
        dv_ref[...] = dv[:, :NOPE]

    return pl.pallas_call(
        body, name="flash_bwd",
        out_shape=[jax.ShapeDtypeStruct((T, N_HEADS * QK_PAD), F32),
                   jax.ShapeDtypeStruct((T, N_HEADS * QK_PAD), F32),
                   jax.ShapeDtypeStruct((T, N_HEADS * NOPE), F32)],
        grid=(N_HEADS, nq),
        in_specs=[pl.BlockSpec((T, QK_PAD), lambda h, j: (0, h)),
                  pl.BlockSpec((tq, QK_PAD), lambda h, j: (j, h)),
                  pl.BlockSpec((tq, QK_PAD), lambda h, j: (j, h)),
                  pl.BlockSpec((T, QK_PAD), lambda h, j: (0, h)),
                  pl.BlockSpec((T, NOPE), lambda h, j: (0, h))],
        out_specs=[pl.BlockSpec((T, QK_PAD), lambda h, j: (0, h)),
                   pl.BlockSpec((tq, QK_PAD), lambda h, j: (j, h)),
                   pl.BlockSpec((tq, NOPE), lambda h, j: (j, h))],
        compiler_params=_params("arbitrary", "arbitrary"),
    )(q, k, v, do, lse)


def _conv_fwd(u, w, bias, *, tm=256, ct=256):
    T, C = u.shape
    tm = min(tm, T)
    hb = tm // HALO

    def body(cur_ref, halo_ref, w_ref, b_ref, o_ref, buf):
        t = pl.program_id(1)
        buf[0:HALO, :] = jnp.where(t > 0, halo_ref[...], 0.0)
        buf[HALO:, :] = cur_ref[...]
        acc = jnp.broadcast_to(b_ref[...], (tm, ct))
        for kk in range(CONV_W):
            acc = acc + buf[pl.ds(HALO - (CONV_W - 1) + kk, tm), :] * w_ref[kk:kk + 1, :]
        o_ref[...] = acc

    return pl.pallas_call(
        body, name="conv_fwd",
        out_shape=jax.ShapeDtypeStruct((T, C), F32),
        grid=(C // ct, T // tm),
        in_specs=[pl.BlockSpec((tm, ct), lambda c, t: (t, c)),
                  pl.BlockSpec((HALO, ct), lambda c, t: (jnp.maximum(t * hb - 1, 0), c)),
                  pl.BlockSpec((CONV_TAPS_PAD, ct), lambda c, t: (0, c)),
                  pl.BlockSpec((1, ct), lambda c, t: (0, c))],
        out_specs=pl.BlockSpec((tm, ct), lambda c, t: (t, c)),
        scratch_shapes=[pltpu.VMEM((HALO + tm, ct), F32)],
        compiler_params=_params("parallel", "arbitrary"),
    )(u, u, w, bias)


def _conv_bwd(u, du1, w, *, tm=256, ct=256):
    T, C = u.shape
    tm = min(tm, T)
    hb = tm // HALO
    nt = T // tm
    last_halo = T // HALO - 1

    def body(u_ref, uh_ref, d_ref, dh_ref, w_ref, du_ref, dw_ref, ubuf, dbuf):
        t = pl.program_id(1)
        ubuf[0:HALO, :] = jnp.where(t > 0, uh_ref[...], 0.0)
        ubuf[HALO:, :] = u_ref[...]
        dcur = d_ref[...]
        dbuf[0:tm, :] = dcur
        dbuf[tm:, :] = jnp.where(t < nt - 1, dh_ref[...], 0.0)

        @pl.when(t == 0)
        def _():
            dw_ref[...] = jnp.zeros_like(dw_ref)

        acc = jnp.zeros((tm, ct), F32)
        for kk in range(CONV_W):
            acc = acc + dbuf[pl.ds(CONV_W - 1 - kk, tm), :] * w_ref[kk:kk + 1, :]
            dw_ref[kk:kk + 1, :] += jnp.sum(
                dcur * ubuf[pl.ds(HALO - (CONV_W - 1) + kk, tm), :], axis=0, keepdims=True)
        du_ref[...] = acc

    return pl.pallas_call(
        body, name="conv_bwd",
        out_shape=[jax.ShapeDtypeStruct((T, C), F32), jax.ShapeDtypeStruct((CONV_TAPS_PAD, C), F32)],
        grid=(C // ct, nt),
        in_specs=[pl.BlockSpec((tm, ct), lambda c, t: (t, c)),
                  pl.BlockSpec((HALO, ct), lambda c, t: (jnp.maximum(t * hb - 1, 0), c)),
                  pl.BlockSpec((tm, ct), lambda c, t: (t, c)),
                  pl.BlockSpec((HALO, ct), lambda c, t: (jnp.minimum((t + 1) * hb, last_halo), c)),
                  pl.BlockSpec((CONV_TAPS_PAD, ct), lambda c, t: (0, c))],
        out_specs=[pl.BlockSpec((tm, ct), lambda c, t: (t, c)),
                   pl.BlockSpec((CONV_TAPS_PAD, ct), lambda c, t: (0, c))],
        scratch_shapes=[pltpu.VMEM((HALO + tm, ct), F32), pltpu.VMEM((tm + HALO, ct), F32)],
        compiler_params=_params("parallel", "arbitrary"),
    )(u, u, du1, du1, w)


def _adamw(w, g, m, v, *, name):
    R, C = w.shape
    tr = _row_tile(R, max(8, (1 << 18) // C // 8 * 8))

    def body(w_ref, g_ref, m_ref, v_ref, d_ref, nm_ref, nv_ref):
        gv = g_ref[...]
        nm = ADAM_B1 * m_ref[...] + (1.0 - ADAM_B1) * gv
        nv = ADAM_B2 * v_ref[...] + (1.0 - ADAM_B2) * (gv * gv)
        m_hat = nm / (1.0 - ADAM_B1 ** ADAM_STEP)
        v_hat = nv / (1.0 - ADAM_B2 ** ADAM_STEP)
        d_ref[...] = -ADAM_LR * (m_hat / (jnp.sqrt(v_hat) + ADAM_EPS) + ADAM_WD * w_ref[...])
        nm_ref[...] = nm
        nv_ref[...] = nv

    spec = pl.BlockSpec((tr, C), lambda i: (i, 0))
    return pl.pallas_call(
        body, name=name, out_shape=[jax.ShapeDtypeStruct((R, C), F32)] * 3,
        grid=(R // tr,), in_specs=[spec] * 4, out_specs=[spec] * 3,
        compiler_params=_params("parallel"),
    )(w, g, m, v)


def _coords():
    return lax.axis_index("x"), lax.axis_index("y"), lax.axis_index("c")


def _remote(src, dst, ssem, rsem, to):
    return pltpu.make_async_remote_copy(src_ref=src, dst_ref=dst, send_sem=ssem, recv_sem=rsem,
                                        device_id=to, device_id_type=MESH)


def _half(c, rows):
    return pl.ds(pl.multiple_of(c * (rows // 2), 16), rows // 2)


def _gather_weights(wpack, cw):
    def body(w_ref, cw_ref, full_ref, cwf_ref, ssem, rsem):
        x, y, c = _coords()
        q = 2 * x + y
        sib = (x, y, 1 - c)
        chips = [(1 - x, y), (x, 1 - y), (1 - x, 1 - y)]
        mine, other = _half(c, w_ref.shape[0]), _half(1 - c, w_ref.shape[0])
        first = []
        for j, (px, py) in enumerate(chips):
            first.append(_remote(w_ref.at[mine], full_ref.at[q, mine], ssem.at[j], rsem.at[j], (px, py, c)))
            first.append(_remote(cw_ref, cwf_ref.at[q], ssem.at[6 + j], rsem.at[6 + j], (px, py, c)))
        for cp in first:
            cp.start()
        passed = []
        for j, (px, py) in enumerate(chips):
            pq = 2 * px + py
            _remote(w_ref.at[mine], full_ref.at[pq, mine], ssem.at[j], rsem.at[j], (px, py, c)).wait_recv()
            fw = _remote(full_ref.at[pq, mine], full_ref.at[pq, mine], ssem.at[3 + j], rsem.at[3 + j], sib)
            fw.start()
            passed.append(fw)
        for j, (px, py) in enumerate(chips):
            pq = 2 * px + py
            _remote(full_ref.at[pq, other], full_ref.at[pq, other], ssem.at[3 + j], rsem.at[3 + j], sib).wait_recv()
            _remote(cw_ref, cwf_ref.at[pq], ssem.at[6 + j], rsem.at[6 + j], (px, py, c)).wait_recv()
        for cp in first + passed:
            cp.wait_send()

    return pl.pallas_call(
        body, name="gather_weights",
        out_shape=[jax.ShapeDtypeStruct((N_CHIPS,) + wpack.shape, wpack.dtype),
                   jax.ShapeDtypeStruct((N_CHIPS,) + cw.shape, cw.dtype)],
        in_specs=[ANY, ANY], out_specs=[ANY, ANY],
        scratch_shapes=[pltpu.SemaphoreType.DMA((9,)), pltpu.SemaphoreType.DMA((9,))],
    )(wpack, cw)


SEM = pl.BlockSpec(memory_space=pltpu.SEMAPHORE)
HBM = pl.BlockSpec(memory_space=pltpu.HBM)
N_LATE = 6


def _late_peers(x, y, c):
    out = []
    for j, (px, py) in enumerate([(1 - x, y), (x, 1 - y), (1 - x, 1 - y)]):
        for t in (0, 1):
            out.append((2 * j + t, px, py, c if t == 0 else 1 - c))
    return out


def _late_gather_start(w, after):
    rows = w.shape[0]

    def body(w_ref, land_ref, after_ref, *outs):
        sems = outs[:2 * N_LATE]
        token = outs[2 * N_LATE + 2]
        x, y, c = _coords()
        q = 2 * x + y
        mine = _half(c, rows)
        for k, px, py, pc in _late_peers(x, y, c):
            _remote(w_ref.at[mine], land_ref.at[q, mine], sems[k], sems[N_LATE + k], (px, py, pc)).start()
        token[...] = jnp.zeros_like(token)

    res = pl.pallas_call(
        body, name="late_gather_start",
        out_shape=tuple([pltpu.SemaphoreType.DMA(())] * (2 * N_LATE)) + (
            pltpu.HBM(w.shape, w.dtype), pltpu.HBM((N_CHIPS,) + w.shape, w.dtype),
            jax.ShapeDtypeStruct((8, 128), F32)),
        in_specs=(HBM, HBM, ANY),
        out_specs=tuple([SEM] * (2 * N_LATE)) + (HBM, HBM, pl.BlockSpec(memory_space=pltpu.VMEM)),
        input_output_aliases={0: 2 * N_LATE, 1: 2 * N_LATE + 1},
        compiler_params=pltpu.CompilerParams(has_side_effects=pltpu.SideEffectType.DATAFLOW_SIDE_EFFECTING),
    )(pltpu.with_memory_space_constraint(w, pltpu.HBM),
      pltpu.with_memory_space_constraint(lax.empty((N_CHIPS,) + w.shape, w.dtype), pltpu.HBM), after)
    return res[:2 * N_LATE], res[2 * N_LATE], res[2 * N_LATE + 1], res[2 * N_LATE + 2]


def _late_gather_wait(sems, w_thru, land_thru, after):
    rows = w_thru.shape[0]

    def body(w_ref, land_ref, *rest):
        sems = rest[:2 * N_LATE]
        x, y, c = _coords()
        for k, px, py, pc in _late_peers(x, y, c):
            cp = _remote(w_ref.at[_half(pc, rows)], land_ref.at[2 * px + py, _half(pc, rows)],
                         sems[k], sems[N_LATE + k], (px, py, pc))
            cp.wait_send()
            cp.wait_recv()

    return pl.pallas_call(
        body, name="late_gather_wait",
        out_shape=(pltpu.HBM(w_thru.shape, w_thru.dtype), pltpu.HBM(land_thru.shape, land_thru.dtype)),
        in_specs=(HBM, HBM) + tuple([SEM] * (2 * N_LATE)) + (ANY,),
        out_specs=(HBM, HBM), input_output_aliases={0: 0, 1: 1},
        compiler_params=pltpu.CompilerParams(has_side_effects=pltpu.SideEffectType.DATAFLOW_SIDE_EFFECTING),
    )(w_thru, land_thru, *sems, after)[1]


N_PEERS = N_DEV - 1


def _peers(x, y, c):
    out = []
    for r in range(1, N_DEV):
        out.append((r - 1, 1 - x if r & 4 else x, 1 - y if r & 2 else y, 1 - c if r & 1 else c))
    return out


def _reduce_start(g16, after, *, name):
    rows = g16.shape[1]

    def body(g_ref, land_ref, after_ref, *outs):
        sems = outs[:2 * N_PEERS]
        token = outs[2 * N_PEERS + 2]
        x, y, c = _coords()
        for k, px, py, pc in _peers(x, y, c):
            _remote(g_ref.at[2 * px + py, _half(pc, rows)], land_ref.at[k], sems[k], sems[N_PEERS + k],
                    (px, py, pc)).start()
        token[...] = jnp.zeros_like(token)

    land_shape = (N_PEERS, rows // 2, 1024)
    res = pl.pallas_call(
        body, name=name,
        out_shape=tuple([pltpu.SemaphoreType.DMA(())] * (2 * N_PEERS)) + (
            pltpu.HBM(g16.shape, g16.dtype), pltpu.HBM(land_shape, g16.dtype),
            jax.ShapeDtypeStruct((8, 128), F32)),
        in_specs=(HBM, HBM, ANY),
        out_specs=tuple([SEM] * (2 * N_PEERS)) + (HBM, HBM, pl.BlockSpec(memory_space=pltpu.VMEM)),
        input_output_aliases={0: 2 * N_PEERS, 1: 2 * N_PEERS + 1},
        compiler_params=pltpu.CompilerParams(has_side_effects=pltpu.SideEffectType.DATAFLOW_SIDE_EFFECTING),
    )(pltpu.with_memory_space_constraint(g16, pltpu.HBM),
      pltpu.with_memory_space_constraint(lax.empty(land_shape, g16.dtype), pltpu.HBM), after)
    return res[:2 * N_PEERS], res[2 * N_PEERS], res[2 * N_PEERS + 1], res[2 * N_PEERS + 2]


def _reduce_wait(sems, g_thru, land_thru, after, *, name):
    rows = g_thru.shape[1]

    def body(g_ref, land_ref, *rest):
        sems = rest[:2 * N_PEERS]
        x, y, c = _coords()
        for k, px, py, pc in _peers(x, y, c):
            cp = _remote(g_ref.at[2 * px + py, _half(pc, rows)], land_ref.at[k], sems[k], sems[N_PEERS + k],
                         (px, py, pc))
            cp.wait_send()
            cp.wait_recv()

    return pl.pallas_call(
        body, name=name,
        out_shape=(pltpu.HBM(g_thru.shape, g_thru.dtype), pltpu.HBM(land_thru.shape, land_thru.dtype)),
        in_specs=(HBM, HBM) + tuple([SEM] * (2 * N_PEERS)) + tuple([ANY] * len(after)),
        out_specs=(HBM, HBM), input_output_aliases={0: 0, 1: 1},
        compiler_params=pltpu.CompilerParams(has_side_effects=pltpu.SideEffectType.DATAFLOW_SIDE_EFFECTING),
    )(g_thru, land_thru, *sems, *after)[1]


def _reduce_sum(place, g32, land, *, tile, name):
    half = g32.shape[1] // 2
    nb = half // tile

    def body(place_ref, g_ref, l_ref, o_ref):
        acc = g_ref[...]
        for k in range(N_PEERS):
            acc = acc + l_ref[k].astype(F32)
        o_ref[...] = acc

    return pl.pallas_call(
        body, name=name,
        out_shape=jax.ShapeDtypeStruct((half, 1024), F32),
        grid_spec=pltpu.PrefetchScalarGridSpec(
            num_scalar_prefetch=1, grid=(nb,),
            in_specs=[pl.BlockSpec((None, tile, 1024), lambda i, s: (s[0], s[1] * nb + i, 0)),
                      pl.BlockSpec((N_PEERS, tile, 1024), lambda i, s: (0, i, 0))],
            out_specs=pl.BlockSpec((tile, 1024), lambda i, s: (i, 0))),
        compiler_params=_params("parallel"),
    )(place, g32, land)


def _join_halves(red, *, name):
    def body(r_ref, o_ref, ssem, rsem):
        x, y, c = _coords()
        cp = _remote(r_ref, o_ref, ssem, rsem, (x, y, 1 - c))
        cp.start()
        cp.wait_recv()
        cp.wait_send()

    return pl.pallas_call(
        body, name=name,
        out_shape=jax.ShapeDtypeStruct(red.shape, F32),
        in_specs=[ANY], out_specs=ANY,
        scratch_shapes=[pltpu.SemaphoreType.DMA, pltpu.SemaphoreType.DMA],
    )(red)


def _share_small(svec):
    def body(s_ref, all_ref, ssem, rsem):
        x, y, c = _coords()
        me = 4 * x + 2 * y + c
        peers = []
        for r in range(1, N_DEV):
            px = 1 - x if r & 4 else x
            py = 1 - y if r & 2 else y
            pc = 1 - c if r & 1 else c
            peers.append((px, py, pc))
        cps = [_remote(s_ref, all_ref.at[me], ssem.at[k], rsem.at[k], p) for k, p in enumerate(peers)]
        for cp in cps:
            cp.start()
        for k, (px, py, pc) in enumerate(peers):
            _remote(s_ref, all_ref.at[4 * px + 2 * py + pc], ssem.at[k], rsem.at[k], (px, py, pc)).wait_recv()
        for cp in cps:
            cp.wait_send()

    return pl.pallas_call(
        body, name="share_small",
        out_shape=jax.ShapeDtypeStruct((N_DEV,) + svec.shape, F32),
        in_specs=[ANY], out_specs=ANY,
        scratch_shapes=[pltpu.SemaphoreType.DMA((N_DEV - 1,)), pltpu.SemaphoreType.DMA((N_DEV - 1,))],
    )(svec)


def _sum_small(allv):
    def body(a_ref, o_ref):
        acc = a_ref[0]
        for d in range(1, N_DEV):
            acc = acc + a_ref[d]
        o_ref[...] = acc

    return pl.pallas_call(
        body, name="sum_small",
        out_shape=jax.ShapeDtypeStruct(allv.shape[1:], F32),
    )(allv)


def _pack_shards(wts, group, rows):
    used = sum(r for _, r in group)
    parts = [wts[n].astype(BF16).reshape(-1, 1024) for n, _ in group]
    return jnp.concatenate(parts + [jnp.zeros((rows - used, 1024), BF16)], axis=0)


def _unpack_early(full):
    w = {}
    w["w_in"] = full[:, :1200].reshape(4, 1024, 1200).transpose(1, 0, 2).reshape(1024, 4800)
    w["w_uq"] = full[:, 1200:1344].reshape(4, 384, 384).transpose(1, 0, 2).reshape(384, 1536)
    w["w_uk"] = full[:, 1344:1408].reshape(256, 1024)
    w["w_uv"] = full[:, 1408:1472].reshape(256, 1024)
    return w


def _unpack_late(full):
    out, off = {}, 0
    for name, rows in LATE:
        out[name] = full[:, off:off + rows]
        off += rows
    w = {}
    for n in ("w_o_attn", "w_pw2", "w_out"):
        w[n] = out[n].reshape(1024, 1024)
    w["w_ff1"] = out["w_ff1"].transpose(1, 0, 2).reshape(1024, 4096)
    w["w_ff2"] = out["w_ff2"].reshape(4096, 1024)
    return w


def _pack_early_grads(g):
    parts = [
        g["w_in"].reshape(1024, 4, 1200).transpose(1, 0, 2).reshape(4, 1200, 1024),
        g["w_uq"].reshape(384, 4, 384).transpose(1, 0, 2).reshape(4, 144, 1024),
        g["w_uk"].reshape(4, 64, 1024),
        g["w_uv"].reshape(4, 64, 1024),
        jnp.zeros((4, EARLY_ROWS - sum(r for _, r in EARLY), 1024), F32),
    ]
    return jnp.concatenate(parts, axis=1)


def _pack_late_grads(g):
    parts = [
        g["w_o_attn"].reshape(4, 256, 1024),
        g["w_pw2"].reshape(4, 256, 1024),
        g["w_out"].reshape(4, 256, 1024),
        g["w_ff1"].reshape(1024, 4, 1024).transpose(1, 0, 2),
        g["w_ff2"].reshape(4, 1024, 1024),
        jnp.zeros((4, LATE_ROWS - sum(r for _, r in LATE), 1024), F32),
    ]
    return jnp.concatenate(parts, axis=1)


def _pad_w_in(w):
    return jnp.concatenate([w[:, :704], jnp.zeros((1024, Z_SMALL - 704), w.dtype), w[:, 704:]], axis=1)


def _pad_w_uq(w):
    w = w.reshape(Q_RANK, N_HEADS, NOPE + ROPE)
    return jnp.pad(w, ((0, 0), (0, 0), (0, QK_PAD - NOPE - ROPE))).reshape(Q_RANK, N_HEADS * QK_PAD)


def _local_step(xs, pos, tgt, W, late_weights, send_late_grads, send_early_grads, cw, vec):
    D = D_MODEL
    w_in = _pad_w_in(W["w_in"])
    w_uq = _pad_w_uq(W["w_uq"])
    w_ukv = jnp.concatenate([W["w_uk"], W["w_uv"]], axis=1)
    inv = ROPE_THETA ** (-jnp.arange(0, ROPE, 2, dtype=F32) / ROPE)
    inv_freq = jnp.concatenate([inv, inv, jnp.zeros((128 - ROPE,), F32)])[None, :]

    (h,) = _rowwise(lambda x, g: (_rms(x, g),), [(xs, D, 0)], [vec["norm_mix_pre"]], [(D, BF16)],
                    name="rms_pre")
    z = _mm(h, w_in, name="mm_in")

    def qkv_prep(zs, p, qn, kvn, invf):
        ang = p.astype(F32) * invf
        cosv, sinv = jnp.cos(ang), jnp.sin(ang)
        ln = lax.broadcasted_iota(jnp.int32, ang.shape, 1)
        cs = jnp.where(ln < ROPE, cosv, 0.0)
        sa = jnp.where((ln >= ROPE // 2) & (ln < ROPE), sinv, 0.0)
        sb = jnp.where(ln < ROPE // 2, -sinv, 0.0)
        cqn = _rms(zs[:, :Q_RANK], qn)
        ckvn = _rms(zs[:, Q_RANK:Q_RANK + KV_RANK], kvn)
        kr = _rope(zs[:, 640:768], cs, sa, sb)
        return cqn, ckvn, kr, cs, sa, sb

    cqn, ckvn, krot, cs, sa, sb = _rowwise(
        qkv_prep, [(z, Z_SMALL, 0), (pos, 1, 0)], [vec["q_norm"], vec["kv_norm"], inv_freq],
        [(Q_RANK, BF16), (KV_RANK, BF16), (128, BF16), (128, F32), (128, F32), (128, F32)],
        name="qkv_prep")
    q_pre = _mm(cqn, w_uq, name="mm_uq")
    kv = _mm(ckvn, w_ukv, name="mm_ukv")

    def qk_rope(qp, kn, vv, kr, cs_, sa_, sb_):
        qs, ks, vs = [], [], []
        ones = jnp.ones((qp.shape[0], NOPE), F32)
        for hd in range(N_HEADS):
            qs.append(qp[:, hd * QK_PAD:hd * QK_PAD + NOPE])
            qs.append(_rope(qp[:, hd * QK_PAD + NOPE:(hd + 1) * QK_PAD], cs_, sa_, sb_))
            ks.append(kn[:, hd * NOPE:(hd + 1) * NOPE])
            ks.append(kr.astype(F32))
            vs.append(vv[:, hd * NOPE:(hd + 1) * NOPE])
            vs.append(ones)
        return (jnp.concatenate(qs, axis=1) * Q_SCALE, jnp.concatenate(ks, axis=1),
                jnp.concatenate(vs, axis=1))

    q, k, v = _rowwise(
        qk_rope, [(q_pre, 2048, 0), (kv, 1024, 0), (kv, 1024, 1), (krot, 128, 0), (cs, 128, 0),
                  (sa, 128, 0), (sb, 128, 0)], [],
        [(2048, BF16), (2048, BF16), (2048, BF16)], name="qk_rope")
    attn, lse = _flash_fwd(q, k, v)
    W = {**W, **late_weights(attn)}
    y_attn = _mm(attn, W["w_o_attn"], name="mm_o_attn")

    (u0,) = _rowwise(lambda a, b: (a * _sigmoid(b),), [(z, D, 1), (z, D, 2)], [], [(D, F32)], name="glu")
    u1 = _conv_fwd(u0, cw, vec["conv_b"])

    def ln_parts(u, g, b):
        mu = jnp.mean(u, axis=-1, keepdims=True)
        xc = u - mu
        rs = lax.rsqrt(jnp.mean(xc * xc, axis=-1, keepdims=True) + EPS)
        nh = xc * rs
        return nh, rs, nh * g + b

    def ln_silu(u, g, b):
        y = ln_parts(u, g, b)[2]
        return (y * _sigmoid(y),)

    (u3,) = _rowwise(ln_silu, [(u1, D, 0)], [vec["conv_ln_g"], vec["conv_ln_b"]], [(D, BF16)],
                     name="ln_silu")
    y_conv = _mm(u3, W["w_pw2"], name="mm_pw2")

    def merge(ga, gc, ya, yc, b2):
        return (_sigmoid(ga) * ya + _sigmoid(gc) * (yc + b2),)

    (merged,) = _rowwise(merge, [(z, D, 3), (z, D, 4), (y_attn, D, 0), (y_conv, D, 0)], [vec["b_pw2"]],
                         [(D, BF16)], name="merge")
    mo = _mm(merged, W["w_out"], name="mm_out")

    def post1(x, m, g1, g2):
        x1 = x + _rms(m, g1)
        return x1, _rms(x1, g2)

    x1, h2 = _rowwise(post1, [(xs, D, 0), (mo, D, 0)], [vec["norm_mix_post"], vec["norm_mlp_pre"]],
                      [(D, F32), (D, BF16)], name="post_mix")

    def sqrelu(acc):
        r = jnp.maximum(acc, 0.0)
        return r * r, r

    a2, r1 = _mm(h2, W["w_ff1"], name="mm_ff1", epilogue=sqrelu, out_dtypes=(BF16, BF16))
    f = _mm(a2, W["w_ff2"], name="mm_ff2")

    def post2(x1_, f_, t_, g):
        e = x1_ + _rms(f_, g) - t_
        dy = e * (1.0 / D)
        df, dg = _rms_bwd(f_, g, dy)
        loss = jnp.broadcast_to(jnp.sum(e * e, keepdims=True) * (0.5 / D), (1, 128))
        return dy, df, dg, loss

    dy, df, g_norm_mlp_post, loss = _rowwise(
        post2, [(x1, D, 0), (f, D, 0), (tgt, D, 0)], [vec["norm_mlp_post"]],
        [(D, F32), (D, BF16)], [(1, D), (1, 128)], name="post_mlp_loss")
    g_w_ff2 = _mm(a2, df, ta=True, name="mm_dw_ff2")
    df1 = _mm(df, W["w_ff2"], tb=True, name="mm_df1", extras=[r1], out_dtypes=(BF16,),
              epilogue=lambda acc, r: (acc * (2.0 * r.astype(F32)),))
    g_w_ff1 = _mm(h2, df1, ta=True, name="mm_dw_ff1")
    dh2 = _mm(df1, W["w_ff1"], tb=True, name="mm_dh2")

    def bwd_mid(dy_, dh2_, x1_, m, g2, g1):
        d1, dg2 = _rms_bwd(x1_, g2, dh2_)
        dx1_ = dy_ + d1
        dm, dg1 = _rms_bwd(m, g1, dx1_)
        return dx1_, dm, dg2, dg1

    dx1, dmo, g_norm_mlp_pre, g_norm_mix_post = _rowwise(
        bwd_mid, [(dy, D, 0), (dh2, D, 0), (x1, D, 0), (mo, D, 0)],
        [vec["norm_mlp_pre"], vec["norm_mix_post"]], [(D, F32), (D, BF16)], [(1, D), (1, D)],
        name="bwd_mid")
    g_w_out = _mm(merged, dmo, ta=True, name="mm_dw_out")
    dmerged = _mm(dmo, W["w_out"], tb=True, name="mm_dmerged")

    def dmerge(dm, ga, gc, ya, yc, b2):
        sga, sgc = _sigmoid(ga), _sigmoid(gc)
        dya = dm * sga
        dyc = dm * sgc
        dga = dm * ya * sga * (1.0 - sga)
        dgc = dm * (yc + b2) * sgc * (1.0 - sgc)
        return dya, dyc, dga, dgc, jnp.sum(dyc, axis=0, keepdims=True)

    dya, dyc, dga, dgc, g_b_pw2 = _rowwise(
        dmerge, [(dmerged, D, 0), (z, D, 3), (z, D, 4), (y_attn, D, 0), (y_conv, D, 0)], [vec["b_pw2"]],
        [(D, BF16), (D, BF16), (D, BF16), (D, BF16)], [(1, D)], name="dmerge")

    g_w_o = _mm(attn, dya, ta=True, name="mm_dw_o_attn")
    g_w_pw2 = _mm(u3, dyc, ta=True, name="mm_dw_pw2")
    late_token = send_late_grads({"w_o_attn": g_w_o, "w_pw2": g_w_pw2, "w_out": g_w_out,
                                  "w_ff1": g_w_ff1, "w_ff2": g_w_ff2})

    dattn = _mm(dya, W["w_o_attn"], tb=True, out_dtype=BF16, name="mm_dattn")

    def delta_fn(do_, o_):
        do32 = do_.astype(F32)
        pr = do32 * o_.astype(F32)
        ln = lax.broadcasted_iota(jnp.int32, (pr.shape[0], NOPE), 1)
        cols = []
        for hd in range(N_HEADS):
            dl = jnp.sum(pr[:, hd * NOPE:(hd + 1) * NOPE], axis=-1, keepdims=True)
            hi = dl.astype(BF16).astype(F32)
            cols.append(do32[:, hd * NOPE:(hd + 1) * NOPE])
            cols.append(jnp.where(ln == 0, -hi, jnp.where(ln == 1, hi - dl, 0.0)))
        return (jnp.concatenate(cols, axis=1),)

    (do_ext,) = _rowwise(delta_fn, [(dattn, D, 0), (attn, D, 0)], [], [(2048, BF16)], name="attn_delta",
                         after=[late_token])
    dq, dk, dv = _flash_bwd(q, k, v, do_ext, lse)

    def unrope(dq_, dk_, dv_, cs_, sa_, sb_):
        qs, kn = [], []
        dkr = jnp.zeros_like(cs_)
        for hd in range(N_HEADS):
            qs.append(dq_[:, hd * QK_PAD:hd * QK_PAD + NOPE] * SCALE)
            qs.append(_rope_t(dq_[:, hd * QK_PAD + NOPE:(hd + 1) * QK_PAD] * SCALE, cs_, sa_, sb_))
            kn.append(dk_[:, hd * QK_PAD:hd * QK_PAD + NOPE] * K_UNSCALE)
            dkr = dkr + dk_[:, hd * QK_PAD + NOPE:(hd + 1) * QK_PAD]
        return (jnp.concatenate(qs, axis=1), jnp.concatenate(kn + [dv_], axis=1),
                _rope_t(dkr * K_UNSCALE, cs_, sa_, sb_))

    dq_pre, dkv, dkr = _rowwise(
        unrope, [(dq, 2048, 0), (dk, 2048, 0), (dv, 1024, 0), (cs, 128, 0), (sa, 128, 0), (sb, 128, 0)], [],
        [(2048, BF16), (2048, BF16), (128, F32)], name="unrope")
    g_w_uq = _mm(cqn, dq_pre, ta=True, name="mm_dw_uq")
    dcqn = _mm(dq_pre, w_uq, tb=True, name="mm_dcqn")
    g_w_ukv = _mm(ckvn, dkv, ta=True, name="mm_dw_ukv")
    dckvn = _mm(dkv, w_ukv, tb=True, name="mm_dckvn")

    def small_bwd(zs, dcq_, dckv_, dkr_, qn, kvn):
        dcq, dqn = _rms_bwd(zs[:, :Q_RANK], qn, dcq_)
        dckv, dkvn = _rms_bwd(zs[:, Q_RANK:Q_RANK + KV_RANK], kvn, dckv_)
        pad = jnp.zeros((zs.shape[0], Z_SMALL - 768), F32)
        return jnp.concatenate([dcq, dckv, dkr_, pad], axis=1), dqn, dkvn

    dz_small, g_q_norm, g_kv_norm = _rowwise(
        small_bwd, [(z, Z_SMALL, 0), (dcqn, Q_RANK, 0), (dckvn, KV_RANK, 0), (dkr, 128, 0)],
        [vec["q_norm"], vec["kv_norm"]], [(Z_SMALL, BF16)], [(1, Q_RANK), (1, KV_RANK)], name="small_bwd")

    du3 = _mm(dyc, W["w_pw2"], tb=True, name="mm_du3")

    def ln_silu_bwd(d3, u, g, b):
        nh, rs, y = ln_parts(u, g, b)
        sg = _sigmoid(y)
        dyv = d3 * (sg * (1.0 + y * (1.0 - sg)))
        dnh = dyv * g
        du = rs * (dnh - jnp.mean(dnh, axis=-1, keepdims=True)
                   - nh * jnp.mean(dnh * nh, axis=-1, keepdims=True))
        return (du, jnp.sum(dyv * nh, axis=0, keepdims=True), jnp.sum(dyv, axis=0, keepdims=True),
                jnp.sum(du, axis=0, keepdims=True))

    du1, g_ln_g, g_ln_b, g_conv_b = _rowwise(
        ln_silu_bwd, [(du3, D, 0), (u1, D, 0)], [vec["conv_ln_g"], vec["conv_ln_b"]], [(D, F32)],
        [(1, D), (1, D), (1, D)], name="ln_silu_bwd")
    du0, g_conv_w = _conv_bwd(u0, du1, cw)

    def glu_bwd(d0, a, b):
        sg = _sigmoid(b)
        return d0 * sg, d0 * a * sg * (1.0 - sg)

    dza, dzb = _rowwise(glu_bwd, [(du0, D, 0), (z, D, 1), (z, D, 2)], [], [(D, BF16), (D, BF16)],
                        name="glu_bwd")

    dz = jnp.concatenate([dz_small, dza, dzb, dga, dgc], axis=1)
    g_w_in = _mm(h, dz, ta=True, name="mm_dw_in")
    early_token = send_early_grads({
        "w_in": jnp.concatenate([g_w_in[:, :704], g_w_in[:, Z_SMALL:]], axis=1),
        "w_uq": g_w_uq.reshape(Q_RANK, N_HEADS, QK_PAD)[:, :, :NOPE + ROPE].reshape(Q_RANK, 1536),
        "w_uk": g_w_ukv[:, :1024], "w_uv": g_w_ukv[:, 1024:]})
    dh = _mm(dz, w_in, tb=True, name="mm_dh", after=[early_token])

    def final(dx1_, dh_, x, g):
        d, dg = _rms_bwd(x, g, dh_)
        return dx1_ + d, dg

    grad_x, g_norm_mix_pre = _rowwise(final, [(dx1, D, 0), (dh, D, 0), (xs, D, 0)], [vec["norm_mix_pre"]],
                                      [(D, F32)], [(1, D)], name="bwd_final")

    gs = {
        "norm_mix_pre": g_norm_mix_pre, "conv_b": g_conv_b, "conv_ln_g": g_ln_g, "conv_ln_b": g_ln_b,
        "b_pw2": g_b_pw2, "norm_mix_post": g_norm_mix_post, "norm_mlp_pre": g_norm_mlp_pre,
        "norm_mlp_post": g_norm_mlp_post, "q_norm": g_q_norm, "kv_norm": g_kv_norm,
    }
    return grad_x, gs, g_conv_w, loss


def _row1024(a):
    return jnp.pad(a, ((0, 0), (0, 1024 - a.shape[1])))


BIG = tuple(n for n, _ in EARLY + LATE)
ORDER = ("norm_mix_pre", "w_in", "q_norm", "w_uq", "kv_norm", "w_uk", "w_uv", "w_o_attn", "conv_w",
         "conv_b", "conv_ln_g", "conv_ln_b", "w_pw2", "b_pw2", "w_out", "norm_mix_post", "norm_mlp_pre",
         "w_ff1", "w_ff2", "norm_mlp_post")


def kernel(x, positions, norm_mix_pre, w_in, q_norm, w_uq, kv_norm, w_uk, w_uv, w_o_attn, conv_w, conv_b, conv_ln_g, conv_ln_b, w_pw2, b_pw2, w_out, norm_mix_post, norm_mlp_pre, w_ff1, w_ff2, norm_mlp_post, loss_target, m_norm_mix_pre, m_w_in, m_q_norm, m_w_uq, m_kv_norm, m_w_uk, m_w_uv, m_w_o_attn, m_conv_w, m_conv_b, m_conv_ln_g, m_conv_ln_b, m_w_pw2, m_b_pw2, m_w_out, m_norm_mix_post, m_norm_mlp_pre, m_w_ff1, m_w_ff2, m_norm_mlp_post, v_norm_mix_pre, v_w_in, v_q_norm, v_w_uq, v_kv_norm, v_w_uk, v_w_uv, v_w_o_attn, v_conv_w, v_conv_b, v_conv_ln_g, v_conv_ln_b, v_w_pw2, v_b_pw2, v_w_out, v_norm_mix_post, v_norm_mlp_pre, v_w_ff1, v_w_ff2, v_norm_mlp_post):
    args = dict(locals())
    wts = {n: args[n] for n in ORDER}
    mom = {n: args["m_" + n] for n in ORDER}
    var = {n: args["v_" + n] for n in ORDER}
    T = x.shape[1]
    place = jnp.stack([2 * lax.axis_index("x") + lax.axis_index("y"), lax.axis_index("c")]).astype(jnp.int32)

    w_early = _pack_shards(wts, EARLY, EARLY_ROWS)
    w_late = _pack_shards(wts, LATE, LATE_ROWS)
    cw_own = jnp.pad(conv_w.reshape(CONV_W, -1), ((0, CONV_TAPS_PAD - CONV_W), (0, 0)))
    full, cw_all = _gather_weights(w_early, cw_own)
    full = lax.dynamic_update_slice(full, w_early[None], (place[0], 0, 0))
    cw_all = lax.dynamic_update_slice(cw_all, cw_own[None], (place[0], 0, 0))
    W = _unpack_early(full)
    cw = cw_all.transpose(1, 0, 2).reshape(CONV_TAPS_PAD, D_MODEL)
    sems, w_thru, land, token = _late_gather_start(w_late, full)
    vec = {n: wts[n] for n in SMALL_VECS}
    vec["norm_mix_pre"] = vec["norm_mix_pre"] + token[:1, :1]

    def late_weights(after):
        landed = _late_gather_wait(sems, w_thru, land, after)
        return _unpack_late(lax.dynamic_update_slice(landed, w_late[None], (place[0], 0, 0)))

    sent = {}

    def send_late_grads(g):
        g32 = _pack_late_grads(g)
        sent["late"] = (g32,) + _reduce_start(g32.astype(BF16), place, name="late_grads_start")
        return sent["late"][4]

    def send_early_grads(g):
        g32 = _pack_early_grads(g)
        sent["early"] = (g32,) + _reduce_start(g32.astype(BF16), place, name="early_grads_start")
        return sent["early"][4]

    grad_x, gs, g_conv_w, loss = _local_step(x[0], positions.reshape(T, 1), loss_target[0], W, late_weights,
                                             send_late_grads, send_early_grads, cw, vec)

    def finish(group, after, tile):
        g32, sems_, g_thru, land_, _ = sent[group]
        landed = _reduce_wait(sems_, g_thru, land_, after, name=group + "_grads_wait")
        red = _reduce_sum(place, g32, landed, tile=tile, name=group + "_grads_sum")
        theirs = _join_halves(red, name=group + "_grads_join")
        return jnp.where(place[1] == 0, jnp.concatenate([red, theirs], axis=0),
                         jnp.concatenate([theirs, red], axis=0))

    def as2d(a):
        return a.reshape(-1, a.shape[-1]) if a.ndim != 4 or a.shape[2] == 1 else a.reshape(a.shape[1], -1)

    grads, delta, new_m, new_v = {}, {}, {}, {}

    def update(n):
        shp = wts[n].shape
        outs = _adamw(as2d(wts[n]), as2d(grads[n]), as2d(mom[n]), as2d(var[n]), name="adamw_" + n)
        delta[n], new_m[n], new_v[n] = (o.reshape(shp) for o in outs)

    g_late, off = finish("late", [sent["early"][4]], SUM_TILE_LATE), 0
    for name, rows in LATE:
        grads[name] = g_late[off:off + rows].reshape(wts[name].shape)
        off += rows
        update(name)

    svec = jnp.concatenate(
        [_row1024(gs[n]) for n in SMALL_VECS] + [_row1024(loss)]
        + [jnp.zeros((SMALL_CONVW_ROW - SMALL_LOSS_ROW - 1, 1024), F32), g_conv_w], axis=0)
    me = 2 * place[0] + place[1]
    ssum = _sum_small(lax.dynamic_update_slice(_share_small(svec), svec[None], (me, 0, 0)))
    for r, n in enumerate(SMALL_VECS):
        grads[n] = ssum[r:r + 1, :wts[n].shape[1]]
    cw_sum = ssum[SMALL_CONVW_ROW:SMALL_CONVW_ROW + CONV_W]
    grads["conv_w"] = lax.dynamic_slice_in_dim(cw_sum, place[0] * 256, 256, axis=1).reshape(conv_w.shape)
    loss_out = ssum[SMALL_LOSS_ROW, 0]
    update("conv_w")
    small = [jnp.concatenate([_row1024(t[n]) for n in SMALL_VECS], axis=0) for t in (wts, grads, mom, var)]
    outs = _adamw(*small, name="adamw_small")
    for r, n in enumerate(SMALL_VECS):
        delta[n], new_m[n], new_v[n] = (o[r:r + 1, :wts[n].shape[1]] for o in outs)

    g_early, off = finish("early", [grad_x, outs[0], delta["w_ff2"]], SUM_TILE_EARLY), 0
    for name, rows in EARLY:
        grads[name] = g_early[off:off + rows].reshape(wts[name].shape)
        off += rows
        update(name)

    return (loss_out, grad_x[None], *[grads[n] for n in ORDER], *[delta[n] for n in ORDER],
            *[new_m[n] for n in ORDER], *[new_v[n] for n in ORDER])
```

```python
import functools

import numpy as np
import jax
import jax.numpy as jnp
from jax import lax
from jax.experimental import pallas as pl
from jax.experimental.pallas import tpu as pltpu

F32 = jnp.float32
BF16 = jnp.bfloat16
MESH = pl.DeviceIdType.MESH

D_MODEL = 1024
N_HEADS = 8
NOPE = 128
ROPE = 64
QK_PAD = 256
Q_RANK = 384
KV_RANK = 256
CONV_W = 31
CONV_TAPS_PAD = 32
HALO = 32
D_FF = 4096
EPS = 1e-6
ROPE_THETA = 10000.0
SCALE = float((NOPE + ROPE) ** -0.5)
Q_SCALE = SCALE * float(np.log2(np.e))
K_UNSCALE = float(np.log(2.0))
Z_SMALL = 1024
Z_COLS = Z_SMALL + 4 * D_MODEL
N_CHIPS = 4
N_DEV = 8

ADAM_LR = 0.001
ADAM_B1 = 0.9
ADAM_B2 = 0.999
ADAM_EPS = 1e-08
ADAM_WD = 0.01
ADAM_STEP = 10

VMEM_LIMIT = 48 * 1024 * 1024
NEG = -1e30

EARLY = (("w_in", 1200), ("w_uq", 144), ("w_uk", 64), ("w_uv", 64))
LATE = (("w_o_attn", 256), ("w_pw2", 256), ("w_out", 256), ("w_ff1", 1024), ("w_ff2", 1024))
EARLY_ROWS = 1536
LATE_ROWS = 2880
SUM_TILE_EARLY = 256
SUM_TILE_LATE = 288

SMALL_VECS = ("norm_mix_pre", "conv_b", "conv_ln_g", "conv_ln_b", "b_pw2", "norm_mix_post",
              "norm_mlp_pre", "norm_mlp_post", "q_norm", "kv_norm")
SMALL_LOSS_ROW = 10
SMALL_CONVW_ROW = 16
SMALL_ROWS = 48


ANY = pl.BlockSpec(memory_space=pl.ANY)


def _params(*sem):
    return pltpu.CompilerParams(dimension_semantics=sem, vmem_limit_bytes=VMEM_LIMIT)


def _tile(n, cap):
    if n <= cap:
        return n
    t = (cap // 128) * 128
    while n % t:
        t -= 128
    return t


def _row_tile(n, cap):
    if n <= cap:
        return n
    t = (cap // 8) * 8
    while n % t:
        t -= 8
    return t


MM_VMEM_BUDGET = 40 * 1024 * 1024


def _mm_tiles(M, N, K, in_bytes, out_bytes):
    tn = _tile(N, 1024)
    tk = K
    while True:
        for cap in (1024, 512, 256):
            tm = _tile(M, cap)
            need = 2 * tk * (tm * in_bytes[0] + tn * in_bytes[1]) + 2 * tm * tn * out_bytes
            need += tm * tn * 4 if tk < K else 0
            if need <= MM_VMEM_BUDGET:
                return tm, tn, tk
        tk = _tile(K, tk - 128)


def _mm(a, b, *, name, ta=False, tb=False, out_dtype=F32, extras=(),
        epilogue=None, out_dtypes=None, after=()):
    if ta:
        K, M = a.shape
    else:
        M, K = a.shape
    if tb:
        N, K2 = b.shape
    else:
        K2, N = b.shape
    assert K == K2, (a.shape, b.shape, ta, tb)
    out_dtypes = tuple(out_dtypes or (out_dtype,))
    out_bytes = sum(jnp.dtype(d).itemsize for d in out_dtypes) + sum(e.dtype.itemsize for e in extras)
    tm, tn, tk = _mm_tiles(M, N, K, (a.dtype.itemsize, b.dtype.itemsize), out_bytes)
    nk = K // tk
    dims = (((0 if ta else 1,), (1 if tb else 0,)), ((), ()))
    n_ex, n_out = len(extras), len(out_dtypes)

    def body(*refs):
        a_ref, b_ref = refs[:2]
        ex_refs = refs[2:2 + n_ex]
        o_refs = refs[2 + n_ex + len(after):2 + n_ex + len(after) + n_out]
        k = pl.program_id(2)
        part = lax.dot_general(a_ref[...].astype(BF16), b_ref[...].astype(BF16), dims,
                               preferred_element_type=F32)

        def finish(acc):
            res = epilogue(acc, *[r[...] for r in ex_refs]) if epilogue else (acc,)
            for o_ref, r in zip(o_refs, res):
                o_ref[...] = r.astype(o_ref.dtype)

        if nk == 1:
            finish(part)
        else:
            acc_ref = refs[-1]

            @pl.when(k == 0)
            def _():
                acc_ref[...] = part

            @pl.when((k > 0) & (k < nk - 1))
            def _():
                acc_ref[...] += part

            @pl.when(k == nk - 1)
            def _():
                finish(acc_ref[...] + part)

    a_spec = (pl.BlockSpec((tk, tm), lambda i, j, k: (k, i)) if ta
              else pl.BlockSpec((tm, tk), lambda i, j, k: (i, k)))
    b_spec = (pl.BlockSpec((tn, tk), lambda i, j, k: (j, k)) if tb
              else pl.BlockSpec((tk, tn), lambda i, j, k: (k, j)))
    o_spec = pl.BlockSpec((tm, tn), lambda i, j, k: (i, j))
    res = pl.pallas_call(
        body, name=name,
        out_shape=[jax.ShapeDtypeStruct((M, N), dt) for dt in out_dtypes],
        grid=(M // tm, N // tn, nk),
        in_specs=[a_spec, b_spec] + [o_spec] * n_ex + [ANY] * len(after),
        out_specs=[o_spec] * n_out,
        scratch_shapes=[pltpu.VMEM((tm, tn), F32)] if nk > 1 else [],
        compiler_params=_params("parallel", "parallel", "arbitrary"),
    )(a, b, *extras, *after)
    return res[0] if n_out == 1 else res


def _rowwise(fn, rows, consts, outs, reds=(), *, name, tm=256, after=()):
    T = rows[0][0].shape[0]
    tm = min(tm, T)
    n_row, n_const, n_out = len(rows), len(consts), len(outs)

    def body(*refs):
        i = pl.program_id(0)
        vals = [r[...] for r in refs[:n_row + n_const]]
        res = fn(*vals)
        out_refs = refs[n_row + n_const + len(after):]
        for k in range(n_out):
            out_refs[k][...] = res[k].astype(out_refs[k].dtype)
        for k in range(len(reds)):
            ref = out_refs[n_out + k]

            @pl.when(i == 0)
            def _(ref=ref):
                ref[...] = jnp.zeros_like(ref)

            ref[...] += res[n_out + k]

    in_specs = [pl.BlockSpec((tm, w), functools.partial(lambda i, cb: (i, cb), cb=cb))
                for (_, w, cb) in rows]
    in_specs += [pl.BlockSpec(c.shape, lambda i: (0, 0)) for c in consts]
    in_specs += [ANY] * len(after)
    out_specs = [pl.BlockSpec((tm, w), lambda i: (i, 0)) for (w, _) in outs]
    out_specs += [pl.BlockSpec(s, lambda i: (0, 0)) for s in reds]
    out_shape = [jax.ShapeDtypeStruct((T, w), dt) for (w, dt) in outs]
    out_shape += [jax.ShapeDtypeStruct(s, F32) for s in reds]
    return pl.pallas_call(
        body, name=name, out_shape=out_shape, grid=(T // tm,),
        in_specs=in_specs, out_specs=out_specs,
        compiler_params=_params("arbitrary"),
    )(*[r[0] for r in rows], *consts, *after)


def _rms(x, g):
    r = lax.rsqrt(jnp.mean(x * x, axis=-1, keepdims=True) + EPS)
    return x * r * g


def _rms_bwd(x, g, dy):
    r = lax.rsqrt(jnp.mean(x * x, axis=-1, keepdims=True) + EPS)
    n = x * r
    dyg = dy * g
    dx = r * (dyg - n * jnp.mean(dyg * n, axis=-1, keepdims=True))
    return dx, jnp.sum(dy * n, axis=0, keepdims=True)


def _sigmoid(x):
    return 1.0 / (1.0 + jnp.exp(-x))


def _rope(x, cs, sa, sb):
    return x * cs + pltpu.roll(x, 32, 1) * sa + pltpu.roll(x, 96, 1) * sb


def _rope_t(dy, cs, sa, sb):
    return dy * cs + pltpu.roll(dy * sa, 96, 1) + pltpu.roll(dy * sb, 32, 1)


def _causal_mask(t):
    row = lax.broadcasted_iota(jnp.int32, (t, t), 0)
    col = lax.broadcasted_iota(jnp.int32, (t, t), 1)
    return col <= row


NT_DIMS = (((1,), (1,)), ((), ()))
TN_DIMS = (((0,), (0,)), ((), ()))


def _flash_fwd(q, k, v, *, tq=512):
    T = q.shape[0]
    tq = min(tq, T)

    def body(q_ref, k_ref, v_ref, o_ref, lse_ref):
        i = pl.program_id(1)
        qv = q_ref[...]

        def step(j, carry, masked):
            m, acc = carry
            sl = pl.ds(pl.multiple_of(j * tq, tq), tq)
            s = lax.dot_general(qv, k_ref[sl, :], NT_DIMS, preferred_element_type=F32)
            if masked:
                s = jnp.where(_causal_mask(tq), s, NEG)
            m_new = jnp.maximum(m, jnp.max(s, axis=-1, keepdims=True))
            p = jnp.exp2(s - m_new)
            acc = jnp.exp2(m - m_new) * acc + jnp.dot(p.astype(BF16), v_ref[sl, :],
                                                      preferred_element_type=F32)
            return m_new, acc

        init = (jnp.full((tq, 1), NEG, F32), jnp.zeros((tq, QK_PAD), F32))
        carry = lax.fori_loop(0, i, lambda j, c: step(j, c, False), init)
        m, acc = step(i, carry, True)
        l = acc[:, NOPE:]
        o_ref[...] = (acc[:, :NOPE] / l).astype(BF16)
        lse_ref[...] = m + jnp.log2(l)

    return pl.pallas_call(
        body, name="flash_fwd",
        out_shape=[jax.ShapeDtypeStruct((T, N_HEADS * NOPE), BF16),
                   jax.ShapeDtypeStruct((T, N_HEADS * NOPE), F32)],
        grid=(N_HEADS, T // tq),
        in_specs=[pl.BlockSpec((tq, QK_PAD), lambda h, i: (i, h)),
                  pl.BlockSpec((T, QK_PAD), lambda h, i: (0, h)),
                  pl.BlockSpec((T, QK_PAD), lambda h, i: (0, h))],
        out_specs=[pl.BlockSpec((tq, NOPE), lambda h, i: (i, h)),
                   pl.BlockSpec((tq, NOPE), lambda h, i: (i, h))],
        compiler_params=_params("parallel", "arbitrary"),
    )(q, k, v)


def _flash_bwd(q, k, v, do, lse, *, tq=512):
    T = q.shape[0]
    tq = min(tq, T)
    nq = T // tq

    def body(q_ref, k_ref, v_ref, do_ref, lse_ref, dq_ref, dk_ref, dv_ref):
        j = pl.program_id(1)

        @pl.when(j == 0)
        def _():
            dq_ref[...] = jnp.zeros_like(dq_ref)

        kj = k_ref[...]
        vj = v_ref[...]

        def step(i, carry, masked):
            dk, dv = carry
            sl = pl.ds(pl.multiple_of(i * tq, tq), tq)
            qi = q_ref[sl, :]
            doi = do_ref[sl, :]
            s = lax.dot_general(qi, kj, NT_DIMS, preferred_element_type=F32)
            p = jnp.exp2(s - lse_ref[sl, :][:, :1])
            if masked:
                p = jnp.where(_causal_mask(tq), p, 0.0)
            dv = dv + lax.dot_general(p.astype(BF16), doi, TN_DIMS, preferred_element_type=F32)
            ds = (p * lax.dot_general(doi, vj, NT_DIMS, preferred_element_type=F32)).astype(BF16)
            dk = dk + lax.dot_general(ds, qi, TN_DIMS, preferred_element_type=F32)
            dq_ref[sl, :] += jnp.dot(ds, kj, preferred_element_type=F32)
            return dk, dv

        zero = jnp.zeros((tq, QK_PAD), F32)
        carry = step(j, (zero, zero), True)
        dk, dv = lax.fori_loop(j + 1, nq, lambda i, c: step(i, c, False), carry)
        dk_ref[...] = dk
        dv_ref[...] = dv[:, :NOPE]

    return pl.pallas_call(
        body, name="flash_bwd",
        out_shape=[jax.ShapeDtypeStruct((T, N_HEADS * QK_PAD), F32),
                   jax.ShapeDtypeStruct((T, N_HEADS * QK_PAD), F32),
                   jax.ShapeDtypeStruct((T, N_HEADS * NOPE), F32)],
        grid=(N_HEADS, nq),
        in_specs=[pl.BlockSpec((T, QK_PAD), lambda h, j: (0, h)),
                  pl.BlockSpec((tq, QK_PAD), lambda h, j: (j, h)),
                  pl.BlockSpec((tq, QK_PAD), lambda h, j: (j, h)),
                  pl.BlockSpec((T, QK_PAD), lambda h, j: (0, h)),
                  pl.BlockSpec((T, NOPE), lambda h, j: (0, h))],
        out_specs=[pl.BlockSpec((T, QK_PAD), lambda h, j: (0, h)),
                   pl.BlockSpec((tq, QK_PAD), lambda h, j: (j, h)),
                   pl.BlockSpec((tq, NOPE), lambda h, j: (j, h))],
        compiler_params=_params("arbitrary", "arbitrary"),
    )(q, k, v, do, lse)


SUB = 128


def _windows(buf, t0, c0, first, last):
    span = SUB + 8 * ((last + 7) // 8)
    base = buf[t0:t0 + span, c0:c0 + SUB]
    for r in range(8):
        offs = [o for o in range(first, last + 1) if o % 8 == r]
        if offs:
            win = base if r == 0 else pltpu.roll(base, span - r, 0)
            for o in offs:
                yield o, win[o - r:o - r + SUB, :]


def _conv_fwd(u, w, bias, *, tm=256, ct=256):
    T, C = u.shape
    tm = min(tm, T)
    hb = tm // HALO
    lead = HALO - (CONV_W - 1)

    def body(cur_ref, halo_ref, w_ref, b_ref, o_ref, buf):
        t = pl.program_id(1)
        buf[0:HALO, :] = jnp.where(t > 0, halo_ref[...], 0.0)
        buf[HALO:, :] = cur_ref[...]
        for c0 in range(0, ct, SUB):
            for t0 in range(0, tm, SUB):
                acc = jnp.broadcast_to(b_ref[:, c0:c0 + SUB], (SUB, SUB))
                for o, win in _windows(buf, t0, c0, lead, lead + CONV_W - 1):
                    acc = acc + win * w_ref[o - lead:o - lead + 1, c0:c0 + SUB]
                o_ref[t0:t0 + SUB, c0:c0 + SUB] = acc

    return pl.pallas_call(
        body, name="conv_fwd",
        out_shape=jax.ShapeDtypeStruct((T, C), F32),
        grid=(C // ct, T // tm),
        in_specs=[pl.BlockSpec((tm, ct), lambda c, t: (t, c)),
                  pl.BlockSpec((HALO, ct), lambda c, t: (jnp.maximum(t * hb - 1, 0), c)),
                  pl.BlockSpec((CONV_TAPS_PAD, ct), lambda c, t: (0, c)),
                  pl.BlockSpec((1, ct), lambda c, t: (0, c))],
        out_specs=pl.BlockSpec((tm, ct), lambda c, t: (t, c)),
        scratch_shapes=[pltpu.VMEM((HALO + tm, ct), F32)],
        compiler_params=_params("parallel", "arbitrary"),
    )(u, u, w, bias)


def _conv_bwd(u, du1, w, *, tm=256, ct=256):
    T, C = u.shape
    tm = min(tm, T)
    hb = tm // HALO
    nt = T // tm
    last_halo = T // HALO - 1
    lead = HALO - (CONV_W - 1)

    def body(u_ref, uh_ref, d_ref, dh_ref, w_ref, du_ref, dw_ref, ubuf, dbuf, dw8):
        t = pl.program_id(1)
        ubuf[0:HALO, :] = jnp.where(t > 0, uh_ref[...], 0.0)
        ubuf[HALO:, :] = u_ref[...]
        dbuf[0:tm, :] = d_ref[...]
        dbuf[tm:, :] = jnp.where(t < nt - 1, dh_ref[...], 0.0)

        @pl.when(t == 0)
        def _():
            dw8[...] = jnp.zeros_like(dw8)

        for c0 in range(0, ct, SUB):
            for t0 in range(0, tm, SUB):
                acc = jnp.zeros((SUB, SUB), F32)
                for o, win in _windows(dbuf, t0, c0, 0, CONV_W - 1):
                    kk = CONV_W - 1 - o
                    acc = acc + win * w_ref[kk:kk + 1, c0:c0 + SUB]
                du_ref[t0:t0 + SUB, c0:c0 + SUB] = acc
                dsub = d_ref[t0:t0 + SUB, c0:c0 + SUB]
                for o, win in _windows(ubuf, t0, c0, lead, lead + CONV_W - 1):
                    kk = o - lead
                    dw8[8 * kk:8 * kk + 8, c0:c0 + SUB] += jnp.sum(
                        (dsub * win).reshape(SUB // 8, 8, SUB), axis=0)

        @pl.when(t == nt - 1)
        def _():
            dw_ref[...] = jnp.sum(dw8[...].reshape(CONV_TAPS_PAD, 8, ct), axis=1)

    return pl.pallas_call(
        body, name="conv_bwd",
        out_shape=[jax.ShapeDtypeStruct((T, C), F32), jax.ShapeDtypeStruct((CONV_TAPS_PAD, C), F32)],
        grid=(C // ct, nt),
        in_specs=[pl.BlockSpec((tm, ct), lambda c, t: (t, c)),
                  pl.BlockSpec((HALO, ct), lambda c, t: (jnp.maximum(t * hb - 1, 0), c)),
                  pl.BlockSpec((tm, ct), lambda c, t: (t, c)),
                  pl.BlockSpec((HALO, ct), lambda c, t: (jnp.minimum((t + 1) * hb, last_halo), c)),
                  pl.BlockSpec((CONV_TAPS_PAD, ct), lambda c, t: (0, c))],
        out_specs=[pl.BlockSpec((tm, ct), lambda c, t: (t, c)),
                   pl.BlockSpec((CONV_TAPS_PAD, ct), lambda c, t: (0, c))],
        scratch_shapes=[pltpu.VMEM((HALO + tm, ct), F32), pltpu.VMEM((tm + HALO, ct), F32),
                        pltpu.VMEM((CONV_TAPS_PAD * 8, ct), F32)],
        compiler_params=_params("parallel", "arbitrary"),
    )(u, u, du1, du1, w)


def _adamw(w, g, m, v, *, name):
    R, C = w.shape
    tr = _row_tile(R, max(8, (1 << 18) // C // 8 * 8))

    def body(w_ref, g_ref, m_ref, v_ref, d_ref, nm_ref, nv_ref):
        gv = g_ref[...]
        nm = ADAM_B1 * m_ref[...] + (1.0 - ADAM_B1) * gv
        nv = ADAM_B2 * v_ref[...] + (1.0 - ADAM_B2) * (gv * gv)
        m_hat = nm / (1.0 - ADAM_B1 ** ADAM_STEP)
        v_hat = nv / (1.0 - ADAM_B2 ** ADAM_STEP)
        d_ref[...] = -ADAM_LR * (m_hat / (jnp.sqrt(v_hat) + ADAM_EPS) + ADAM_WD * w_ref[...])
        nm_ref[...] = nm
        nv_ref[...] = nv

    spec = pl.BlockSpec((tr, C), lambda i: (i, 0))
    return pl.pallas_call(
        body, name=name, out_shape=[jax.ShapeDtypeStruct((R, C), F32)] * 3,
        grid=(R // tr,), in_specs=[spec] * 4, out_specs=[spec] * 3,
        compiler_params=_params("parallel"),
    )(w, g, m, v)


def _coords():
    return lax.axis_index("x"), lax.axis_index("y"), lax.axis_index("c")


def _remote(src, dst, ssem, rsem, to):
    return pltpu.make_async_remote_copy(src_ref=src, dst_ref=dst, send_sem=ssem, recv_sem=rsem,
                                        device_id=to, device_id_type=MESH)


def _half(c, rows):
    return pl.ds(pl.multiple_of(c * (rows // 2), 16), rows // 2)


def _gather_weights(wpack, cw):
    def body(w_ref, cw_ref, full_ref, cwf_ref, ssem, rsem):
        x, y, c = _coords()
        q = 2 * x + y
        sib = (x, y, 1 - c)
        chips = [(1 - x, y), (x, 1 - y), (1 - x, 1 - y)]
        mine, other = _half(c, w_ref.shape[0]), _half(1 - c, w_ref.shape[0])
        first = []
        for j, (px, py) in enumerate(chips):
            first.append(_remote(w_ref.at[mine], full_ref.at[q, mine], ssem.at[j], rsem.at[j], (px, py, c)))
            first.append(_remote(cw_ref, cwf_ref.at[q], ssem.at[6 + j], rsem.at[6 + j], (px, py, c)))
        for cp in first:
            cp.start()
        passed = []
        for j, (px, py) in enumerate(chips):
            pq = 2 * px + py
            _remote(w_ref.at[mine], full_ref.at[pq, mine], ssem.at[j], rsem.at[j], (px, py, c)).wait_recv()
            fw = _remote(full_ref.at[pq, mine], full_ref.at[pq, mine], ssem.at[3 + j], rsem.at[3 + j], sib)
            fw.start()
            passed.append(fw)
        for j, (px, py) in enumerate(chips):
            pq = 2 * px + py
            _remote(full_ref.at[pq, other], full_ref.at[pq, other], ssem.at[3 + j], rsem.at[3 + j], sib).wait_recv()
            _remote(cw_ref, cwf_ref.at[pq], ssem.at[6 + j], rsem.at[6 + j], (px, py, c)).wait_recv()
        for cp in first + passed:
            cp.wait_send()

    return pl.pallas_call(
        body, name="gather_weights",
        out_shape=[jax.ShapeDtypeStruct((N_CHIPS,) + wpack.shape, wpack.dtype),
                   jax.ShapeDtypeStruct((N_CHIPS,) + cw.shape, cw.dtype)],
        in_specs=[ANY, ANY], out_specs=[ANY, ANY],
        scratch_shapes=[pltpu.SemaphoreType.DMA((9,)), pltpu.SemaphoreType.DMA((9,))],
    )(wpack, cw)


SEM = pl.BlockSpec(memory_space=pltpu.SEMAPHORE)
HBM = pl.BlockSpec(memory_space=pltpu.HBM)
N_LATE = 6


def _late_peers(x, y, c):
    out = []
    for j, (px, py) in enumerate([(1 - x, y), (x, 1 - y), (1 - x, 1 - y)]):
        for t in (0, 1):
            out.append((2 * j + t, px, py, c if t == 0 else 1 - c))
    return out


def _late_gather_start(w, after):
    rows = w.shape[0]

    def body(w_ref, land_ref, after_ref, *outs):
        sems = outs[:2 * N_LATE]
        token = outs[2 * N_LATE + 2]
        x, y, c = _coords()
        q = 2 * x + y
        mine = _half(c, rows)
        for k, px, py, pc in _late_peers(x, y, c):
            _remote(w_ref.at[mine], land_ref.at[q, mine], sems[k], sems[N_LATE + k], (px, py, pc)).start()
        token[...] = jnp.zeros_like(token)

    res = pl.pallas_call(
        body, name="late_gather_start",
        out_shape=tuple([pltpu.SemaphoreType.DMA(())] * (2 * N_LATE)) + (
            pltpu.HBM(w.shape, w.dtype), pltpu.HBM((N_CHIPS,) + w.shape, w.dtype),
            jax.ShapeDtypeStruct((8, 128), F32)),
        in_specs=(HBM, HBM, ANY),
        out_specs=tuple([SEM] * (2 * N_LATE)) + (HBM, HBM, pl.BlockSpec(memory_space=pltpu.VMEM)),
        input_output_aliases={0: 2 * N_LATE, 1: 2 * N_LATE + 1},
        compiler_params=pltpu.CompilerParams(has_side_effects=pltpu.SideEffectType.DATAFLOW_SIDE_EFFECTING),
    )(pltpu.with_memory_space_constraint(w, pltpu.HBM),
      pltpu.with_memory_space_constraint(lax.empty((N_CHIPS,) + w.shape, w.dtype), pltpu.HBM), after)
    return res[:2 * N_LATE], res[2 * N_LATE], res[2 * N_LATE + 1], res[2 * N_LATE + 2]


def _late_gather_wait(sems, w_thru, land_thru, after):
    rows = w_thru.shape[0]

    def body(w_ref, land_ref, *rest):
        sems = rest[:2 * N_LATE]
        x, y, c = _coords()
        for k, px, py, pc in _late_peers(x, y, c):
            cp = _remote(w_ref.at[_half(pc, rows)], land_ref.at[2 * px + py, _half(pc, rows)],
                         sems[k], sems[N_LATE + k], (px, py, pc))
            cp.wait_send()
            cp.wait_recv()

    return pl.pallas_call(
        body, name="late_gather_wait",
        out_shape=(pltpu.HBM(w_thru.shape, w_thru.dtype), pltpu.HBM(land_thru.shape, land_thru.dtype)),
        in_specs=(HBM, HBM) + tuple([SEM] * (2 * N_LATE)) + (ANY,),
        out_specs=(HBM, HBM), input_output_aliases={0: 0, 1: 1},
        compiler_params=pltpu.CompilerParams(has_side_effects=pltpu.SideEffectType.DATAFLOW_SIDE_EFFECTING),
    )(w_thru, land_thru, *sems, after)[1]


N_PEERS = N_DEV - 1


def _peers(x, y, c):
    out = []
    for r in range(1, N_DEV):
        out.append((r - 1, 1 - x if r & 4 else x, 1 - y if r & 2 else y, 1 - c if r & 1 else c))
    return out


def _reduce_start(g16, after, *, name):
    rows = g16.shape[1]

    def body(g_ref, land_ref, after_ref, *outs):
        sems = outs[:2 * N_PEERS]
        token = outs[2 * N_PEERS + 2]
        x, y, c = _coords()
        for k, px, py, pc in _peers(x, y, c):
            _remote(g_ref.at[2 * px + py, _half(pc, rows)], land_ref.at[k], sems[k], sems[N_PEERS + k],
                    (px, py, pc)).start()
        token[...] = jnp.zeros_like(token)

    land_shape = (N_PEERS, rows // 2, 1024)
    res = pl.pallas_call(
        body, name=name,
        out_shape=tuple([pltpu.SemaphoreType.DMA(())] * (2 * N_PEERS)) + (
            pltpu.HBM(g16.shape, g16.dtype), pltpu.HBM(land_shape, g16.dtype),
            jax.ShapeDtypeStruct((8, 128), F32)),
        in_specs=(HBM, HBM, ANY),
        out_specs=tuple([SEM] * (2 * N_PEERS)) + (HBM, HBM, pl.BlockSpec(memory_space=pltpu.VMEM)),
        input_output_aliases={0: 2 * N_PEERS, 1: 2 * N_PEERS + 1},
        compiler_params=pltpu.CompilerParams(has_side_effects=pltpu.SideEffectType.DATAFLOW_SIDE_EFFECTING),
    )(pltpu.with_memory_space_constraint(g16, pltpu.HBM),
      pltpu.with_memory_space_constraint(lax.empty(land_shape, g16.dtype), pltpu.HBM), after)
    return res[:2 * N_PEERS], res[2 * N_PEERS], res[2 * N_PEERS + 1], res[2 * N_PEERS + 2]


def _reduce_wait(sems, g_thru, land_thru, after, *, name):
    rows = g_thru.shape[1]

    def body(g_ref, land_ref, *rest):
        sems = rest[:2 * N_PEERS]
        x, y, c = _coords()
        for k, px, py, pc in _peers(x, y, c):
            cp = _remote(g_ref.at[2 * px + py, _half(pc, rows)], land_ref.at[k], sems[k], sems[N_PEERS + k],
                         (px, py, pc))
            cp.wait_send()
            cp.wait_recv()

    return pl.pallas_call(
        body, name=name,
        out_shape=(pltpu.HBM(g_thru.shape, g_thru.dtype), pltpu.HBM(land_thru.shape, land_thru.dtype)),
        in_specs=(HBM, HBM) + tuple([SEM] * (2 * N_PEERS)) + tuple([ANY] * len(after)),
        out_specs=(HBM, HBM), input_output_aliases={0: 0, 1: 1},
        compiler_params=pltpu.CompilerParams(has_side_effects=pltpu.SideEffectType.DATAFLOW_SIDE_EFFECTING),
    )(g_thru, land_thru, *sems, *after)[1]


def _reduce_sum(place, g32, land, *, tile, name):
    half = g32.shape[1] // 2
    nb = half // tile

    def body(place_ref, g_ref, l_ref, o_ref):
        acc = g_ref[...]
        for k in range(N_PEERS):
            acc = acc + l_ref[k].astype(F32)
        o_ref[...] = acc

    return pl.pallas_call(
        body, name=name,
        out_shape=jax.ShapeDtypeStruct((half, 1024), F32),
        grid_spec=pltpu.PrefetchScalarGridSpec(
            num_scalar_prefetch=1, grid=(nb,),
            in_specs=[pl.BlockSpec((None, tile, 1024), lambda i, s: (s[0], s[1] * nb + i, 0)),
                      pl.BlockSpec((N_PEERS, tile, 1024), lambda i, s: (0, i, 0))],
            out_specs=pl.BlockSpec((tile, 1024), lambda i, s: (i, 0))),
        compiler_params=_params("parallel"),
    )(place, g32, land)


def _join_halves(red, *, name):
    def body(r_ref, o_ref, ssem, rsem):
        x, y, c = _coords()
        cp = _remote(r_ref, o_ref, ssem, rsem, (x, y, 1 - c))
        cp.start()
        cp.wait_recv()
        cp.wait_send()

    return pl.pallas_call(
        body, name=name,
        out_shape=jax.ShapeDtypeStruct(red.shape, F32),
        in_specs=[ANY], out_specs=ANY,
        scratch_shapes=[pltpu.SemaphoreType.DMA, pltpu.SemaphoreType.DMA],
    )(red)


def _share_small(svec):
    def body(s_ref, all_ref, ssem, rsem):
        x, y, c = _coords()
        me = 4 * x + 2 * y + c
        peers = []
        for r in range(1, N_DEV):
            px = 1 - x if r & 4 else x
            py = 1 - y if r & 2 else y
            pc = 1 - c if r & 1 else c
            peers.append((px, py, pc))
        cps = [_remote(s_ref, all_ref.at[me], ssem.at[k], rsem.at[k], p) for k, p in enumerate(peers)]
        for cp in cps:
            cp.start()
        for k, (px, py, pc) in enumerate(peers):
            _remote(s_ref, all_ref.at[4 * px + 2 * py + pc], ssem.at[k], rsem.at[k], (px, py, pc)).wait_recv()
        for cp in cps:
            cp.wait_send()

    return pl.pallas_call(
        body, name="share_small",
        out_shape=jax.ShapeDtypeStruct((N_DEV,) + svec.shape, F32),
        in_specs=[ANY], out_specs=ANY,
        scratch_shapes=[pltpu.SemaphoreType.DMA((N_DEV - 1,)), pltpu.SemaphoreType.DMA((N_DEV - 1,))],
    )(svec)


def _sum_small(allv):
    def body(a_ref, o_ref):
        acc = a_ref[0]
        for d in range(1, N_DEV):
            acc = acc + a_ref[d]
        o_ref[...] = acc

    return pl.pallas_call(
        body, name="sum_small",
        out_shape=jax.ShapeDtypeStruct(allv.shape[1:], F32),
    )(allv)


def _pack_shards(wts, group, rows):
    used = sum(r for _, r in group)
    parts = [wts[n].astype(BF16).reshape(-1, 1024) for n, _ in group]
    return jnp.concatenate(parts + [jnp.zeros((rows - used, 1024), BF16)], axis=0)


def _unpack_early(full):
    w = {}
    w["w_in"] = full[:, :1200].reshape(4, 1024, 1200).transpose(1, 0, 2).reshape(1024, 4800)
    w["w_uq"] = full[:, 1200:1344].reshape(4, 384, 384).transpose(1, 0, 2).reshape(384, 1536)
    w["w_uk"] = full[:, 1344:1408].reshape(256, 1024)
    w["w_uv"] = full[:, 1408:1472].reshape(256, 1024)
    return w


def _unpack_late(full):
    out, off = {}, 0
    for name, rows in LATE:
        out[name] = full[:, off:off + rows]
        off += rows
    w = {}
    for n in ("w_o_attn", "w_pw2", "w_out"):
        w[n] = out[n].reshape(1024, 1024)
    w["w_ff1"] = out["w_ff1"].transpose(1, 0, 2).reshape(1024, 4096)
    w["w_ff2"] = out["w_ff2"].reshape(4096, 1024)
    return w


def _pack_early_grads(g):
    parts = [
        g["w_in"].reshape(1024, 4, 1200).transpose(1, 0, 2).reshape(4, 1200, 1024),
        g["w_uq"].reshape(384, 4, 384).transpose(1, 0, 2).reshape(4, 144, 1024),
        g["w_uk"].reshape(4, 64, 1024),
        g["w_uv"].reshape(4, 64, 1024),
        jnp.zeros((4, EARLY_ROWS - sum(r for _, r in EARLY), 1024), F32),
    ]
    return jnp.concatenate(parts, axis=1)


def _pack_late_grads(g):
    parts = [
        g["w_o_attn"].reshape(4, 256, 1024),
        g["w_pw2"].reshape(4, 256, 1024),
        g["w_out"].reshape(4, 256, 1024),
        g["w_ff1"].reshape(1024, 4, 1024).transpose(1, 0, 2),
        g["w_ff2"].reshape(4, 1024, 1024),
        jnp.zeros((4, LATE_ROWS - sum(r for _, r in LATE), 1024), F32),
    ]
    return jnp.concatenate(parts, axis=1)


def _pad_w_in(w):
    return jnp.concatenate([w[:, :704], jnp.zeros((1024, Z_SMALL - 704), w.dtype), w[:, 704:]], axis=1)


def _pad_w_uq(w):
    w = w.reshape(Q_RANK, N_HEADS, NOPE + ROPE)
    return jnp.pad(w, ((0, 0), (0, 0), (0, QK_PAD - NOPE - ROPE))).reshape(Q_RANK, N_HEADS * QK_PAD)


def _local_step(xs, pos, tgt, W, late_weights, send_late_grads, send_early_grads, cw, vec):
    D = D_MODEL
    w_in = _pad_w_in(W["w_in"])
    w_uq = _pad_w_uq(W["w_uq"])
    w_ukv = jnp.concatenate([W["w_uk"], W["w_uv"]], axis=1)
    inv = ROPE_THETA ** (-jnp.arange(0, ROPE, 2, dtype=F32) / ROPE)
    inv_freq = jnp.concatenate([inv, inv, jnp.zeros((128 - ROPE,), F32)])[None, :]

    (h,) = _rowwise(lambda x, g: (_rms(x, g),), [(xs, D, 0)], [vec["norm_mix_pre"]], [(D, BF16)],
                    name="rms_pre")
    z = _mm(h, w_in, name="mm_in")

    def qkv_prep(zs, p, qn, kvn, invf):
        ang = p.astype(F32) * invf
        cosv, sinv = jnp.cos(ang), jnp.sin(ang)
        ln = lax.broadcasted_iota(jnp.int32, ang.shape, 1)
        cs = jnp.where(ln < ROPE, cosv, 0.0)
        sa = jnp.where((ln >= ROPE // 2) & (ln < ROPE), sinv, 0.0)
        sb = jnp.where(ln < ROPE // 2, -sinv, 0.0)
        cqn = _rms(zs[:, :Q_RANK], qn)
        ckvn = _rms(zs[:, Q_RANK:Q_RANK + KV_RANK], kvn)
        kr = _rope(zs[:, 640:768], cs, sa, sb)
        return cqn, ckvn, kr, cs, sa, sb

    cqn, ckvn, krot, cs, sa, sb = _rowwise(
        qkv_prep, [(z, Z_SMALL, 0), (pos, 1, 0)], [vec["q_norm"], vec["kv_norm"], inv_freq],
        [(Q_RANK, BF16), (KV_RANK, BF16), (128, BF16), (128, F32), (128, F32), (128, F32)],
        name="qkv_prep")
    q_pre = _mm(cqn, w_uq, name="mm_uq")
    kv = _mm(ckvn, w_ukv, name="mm_ukv")

    def qk_rope(qp, kn, vv, kr, cs_, sa_, sb_):
        qs, ks, vs = [], [], []
        ones = jnp.ones((qp.shape[0], NOPE), F32)
        for hd in range(N_HEADS):
            qs.append(qp[:, hd * QK_PAD:hd * QK_PAD + NOPE])
            qs.append(_rope(qp[:, hd * QK_PAD + NOPE:(hd + 1) * QK_PAD], cs_, sa_, sb_))
            ks.append(kn[:, hd * NOPE:(hd + 1) * NOPE])
            ks.append(kr.astype(F32))
            vs.append(vv[:, hd * NOPE:(hd + 1) * NOPE])
            vs.append(ones)
        return (jnp.concatenate(qs, axis=1) * Q_SCALE, jnp.concatenate(ks, axis=1),
                jnp.concatenate(vs, axis=1))

    q, k, v = _rowwise(
        qk_rope, [(q_pre, 2048, 0), (kv, 1024, 0), (kv, 1024, 1), (krot, 128, 0), (cs, 128, 0),
                  (sa, 128, 0), (sb, 128, 0)], [],
        [(2048, BF16), (2048, BF16), (2048, BF16)], name="qk_rope")
    attn, lse = _flash_fwd(q, k, v)
    W = {**W, **late_weights(attn)}
    y_attn = _mm(attn, W["w_o_attn"], name="mm_o_attn")

    (u0,) = _rowwise(lambda a, b: (a * _sigmoid(b),), [(z, D, 1), (z, D, 2)], [], [(D, F32)], name="glu")
    u1 = _conv_fwd(u0, cw, vec["conv_b"])

    def ln_parts(u, g, b):
        mu = jnp.mean(u, axis=-1, keepdims=True)
        xc = u - mu
        rs = lax.rsqrt(jnp.mean(xc * xc, axis=-1, keepdims=True) + EPS)
        nh = xc * rs
        return nh, rs, nh * g + b

    def ln_silu(u, g, b):
        y = ln_parts(u, g, b)[2]
        return (y * _sigmoid(y),)

    (u3,) = _rowwise(ln_silu, [(u1, D, 0)], [vec["conv_ln_g"], vec["conv_ln_b"]], [(D, BF16)],
                     name="ln_silu")
    y_conv = _mm(u3, W["w_pw2"], name="mm_pw2")

    def merge(ga, gc, ya, yc, b2):
        return (_sigmoid(ga) * ya + _sigmoid(gc) * (yc + b2),)

    (merged,) = _rowwise(merge, [(z, D, 3), (z, D, 4), (y_attn, D, 0), (y_conv, D, 0)], [vec["b_pw2"]],
                         [(D, BF16)], name="merge")
    mo = _mm(merged, W["w_out"], name="mm_out")

    def post1(x, m, g1, g2):
        x1 = x + _rms(m, g1)
        return x1, _rms(x1, g2)

    x1, h2 = _rowwise(post1, [(xs, D, 0), (mo, D, 0)], [vec["norm_mix_post"], vec["norm_mlp_pre"]],
                      [(D, F32), (D, BF16)], name="post_mix")

    def sqrelu(acc):
        r = jnp.maximum(acc, 0.0)
        return r * r, r

    a2, r1 = _mm(h2, W["w_ff1"], name="mm_ff1", epilogue=sqrelu, out_dtypes=(BF16, BF16))
    f = _mm(a2, W["w_ff2"], name="mm_ff2")

    def post2(x1_, f_, t_, g):
        e = x1_ + _rms(f_, g) - t_
        dy = e * (1.0 / D)
        df, dg = _rms_bwd(f_, g, dy)
        loss = jnp.broadcast_to(jnp.sum(e * e, keepdims=True) * (0.5 / D), (1, 128))
        return dy, df, dg, loss

    dy, df, g_norm_mlp_post, loss = _rowwise(
        post2, [(x1, D, 0), (f, D, 0), (tgt, D, 0)], [vec["norm_mlp_post"]],
        [(D, F32), (D, BF16)], [(1, D), (1, 128)], name="post_mlp_loss")
    g_w_ff2 = _mm(a2, df, ta=True, name="mm_dw_ff2")
    df1 = _mm(df, W["w_ff2"], tb=True, name="mm_df1", extras=[r1], out_dtypes=(BF16,),
              epilogue=lambda acc, r: (acc * (2.0 * r.astype(F32)),))
    g_w_ff1 = _mm(h2, df1, ta=True, name="mm_dw_ff1")
    dh2 = _mm(df1, W["w_ff1"], tb=True, name="mm_dh2")

    def bwd_mid(dy_, dh2_, x1_, m, g2, g1):
        d1, dg2 = _rms_bwd(x1_, g2, dh2_)
        dx1_ = dy_ + d1
        dm, dg1 = _rms_bwd(m, g1, dx1_)
        return dx1_, dm, dg2, dg1

    dx1, dmo, g_norm_mlp_pre, g_norm_mix_post = _rowwise(
        bwd_mid, [(dy, D, 0), (dh2, D, 0), (x1, D, 0), (mo, D, 0)],
        [vec["norm_mlp_pre"], vec["norm_mix_post"]], [(D, F32), (D, BF16)], [(1, D), (1, D)],
        name="bwd_mid")
    g_w_out = _mm(merged, dmo, ta=True, name="mm_dw_out")
    dmerged = _mm(dmo, W["w_out"], tb=True, name="mm_dmerged")

    def dmerge(dm, ga, gc, ya, yc, b2):
        sga, sgc = _sigmoid(ga), _sigmoid(gc)
        dya = dm * sga
        dyc = dm * sgc
        dga = dm * ya * sga * (1.0 - sga)
        dgc = dm * (yc + b2) * sgc * (1.0 - sgc)
        return dya, dyc, dga, dgc, jnp.sum(dyc, axis=0, keepdims=True)

    dya, dyc, dga, dgc, g_b_pw2 = _rowwise(
        dmerge, [(dmerged, D, 0), (z, D, 3), (z, D, 4), (y_attn, D, 0), (y_conv, D, 0)], [vec["b_pw2"]],
        [(D, BF16), (D, BF16), (D, BF16), (D, BF16)], [(1, D)], name="dmerge")

    g_w_o = _mm(attn, dya, ta=True, name="mm_dw_o_attn")
    g_w_pw2 = _mm(u3, dyc, ta=True, name="mm_dw_pw2")
    late_token = send_late_grads({"w_o_attn": g_w_o, "w_pw2": g_w_pw2, "w_out": g_w_out,
                                  "w_ff1": g_w_ff1, "w_ff2": g_w_ff2})

    dattn = _mm(dya, W["w_o_attn"], tb=True, out_dtype=BF16, name="mm_dattn")

    def delta_fn(do_, o_):
        do32 = do_.astype(F32)
        pr = do32 * o_.astype(F32)
        ln = lax.broadcasted_iota(jnp.int32, (pr.shape[0], NOPE), 1)
        cols = []
        for hd in range(N_HEADS):
            dl = jnp.sum(pr[:, hd * NOPE:(hd + 1) * NOPE], axis=-1, keepdims=True)
            hi = dl.astype(BF16).astype(F32)
            cols.append(do32[:, hd * NOPE:(hd + 1) * NOPE])
            cols.append(jnp.where(ln == 0, -hi, jnp.where(ln == 1, hi - dl, 0.0)))
        return (jnp.concatenate(cols, axis=1),)

    (do_ext,) = _rowwise(delta_fn, [(dattn, D, 0), (attn, D, 0)], [], [(2048, BF16)], name="attn_delta",
                         after=[late_token])
    dq, dk, dv = _flash_bwd(q, k, v, do_ext, lse)

    def unrope(dq_, dk_, dv_, cs_, sa_, sb_):
        qs, kn = [], []
        dkr = jnp.zeros_like(cs_)
        for hd in range(N_HEADS):
            qs.append(dq_[:, hd * QK_PAD:hd * QK_PAD + NOPE] * SCALE)
            qs.append(_rope_t(dq_[:, hd * QK_PAD + NOPE:(hd + 1) * QK_PAD] * SCALE, cs_, sa_, sb_))
            kn.append(dk_[:, hd * QK_PAD:hd * QK_PAD + NOPE] * K_UNSCALE)
            dkr = dkr + dk_[:, hd * QK_PAD + NOPE:(hd + 1) * QK_PAD]
        return (jnp.concatenate(qs, axis=1), jnp.concatenate(kn + [dv_], axis=1),
                _rope_t(dkr * K_UNSCALE, cs_, sa_, sb_))

    dq_pre, dkv, dkr = _rowwise(
        unrope, [(dq, 2048, 0), (dk, 2048, 0), (dv, 1024, 0), (cs, 128, 0), (sa, 128, 0), (sb, 128, 0)], [],
        [(2048, BF16), (2048, BF16), (128, F32)], name="unrope")
    g_w_uq = _mm(cqn, dq_pre, ta=True, name="mm_dw_uq")
    dcqn = _mm(dq_pre, w_uq, tb=True, name="mm_dcqn")
    g_w_ukv = _mm(ckvn, dkv, ta=True, name="mm_dw_ukv")
    dckvn = _mm(dkv, w_ukv, tb=True, name="mm_dckvn")

    def small_bwd(zs, dcq_, dckv_, dkr_, qn, kvn):
        dcq, dqn = _rms_bwd(zs[:, :Q_RANK], qn, dcq_)
        dckv, dkvn = _rms_bwd(zs[:, Q_RANK:Q_RANK + KV_RANK], kvn, dckv_)
        pad = jnp.zeros((zs.shape[0], Z_SMALL - 768), F32)
        return jnp.concatenate([dcq, dckv, dkr_, pad], axis=1), dqn, dkvn

    dz_small, g_q_norm, g_kv_norm = _rowwise(
        small_bwd, [(z, Z_SMALL, 0), (dcqn, Q_RANK, 0), (dckvn, KV_RANK, 0), (dkr, 128, 0)],
        [vec["q_norm"], vec["kv_norm"]], [(Z_SMALL, BF16)], [(1, Q_RANK), (1, KV_RANK)], name="small_bwd")

    du3 = _mm(dyc, W["w_pw2"], tb=True, name="mm_du3")

    def ln_silu_bwd(d3, u, g, b):
        nh, rs, y = ln_parts(u, g, b)
        sg = _sigmoid(y)
        dyv = d3 * (sg * (1.0 + y * (1.0 - sg)))
        dnh = dyv * g
        du = rs * (dnh - jnp.mean(dnh, axis=-1, keepdims=True)
                   - nh * jnp.mean(dnh * nh, axis=-1, keepdims=True))
        return (du, jnp.sum(dyv * nh, axis=0, keepdims=True), jnp.sum(dyv, axis=0, keepdims=True),
                jnp.sum(du, axis=0, keepdims=True))

    du1, g_ln_g, g_ln_b, g_conv_b = _rowwise(
        ln_silu_bwd, [(du3, D, 0), (u1, D, 0)], [vec["conv_ln_g"], vec["conv_ln_b"]], [(D, F32)],
        [(1, D), (1, D), (1, D)], name="ln_silu_bwd")
    du0, g_conv_w = _conv_bwd(u0, du1, cw)

    def glu_bwd(d0, a, b):
        sg = _sigmoid(b)
        return d0 * sg, d0 * a * sg * (1.0 - sg)

    dza, dzb = _rowwise(glu_bwd, [(du0, D, 0), (z, D, 1), (z, D, 2)], [], [(D, BF16), (D, BF16)],
                        name="glu_bwd")

    dz = jnp.concatenate([dz_small, dza, dzb, dga, dgc], axis=1)
    g_w_in = _mm(h, dz, ta=True, name="mm_dw_in")
    early_token = send_early_grads({
        "w_in": jnp.concatenate([g_w_in[:, :704], g_w_in[:, Z_SMALL:]], axis=1),
        "w_uq": g_w_uq.reshape(Q_RANK, N_HEADS, QK_PAD)[:, :, :NOPE + ROPE].reshape(Q_RANK, 1536),
        "w_uk": g_w_ukv[:, :1024], "w_uv": g_w_ukv[:, 1024:]})
    dh = _mm(dz, w_in, tb=True, name="mm_dh", after=[early_token])

    def final(dx1_, dh_, x, g):
        d, dg = _rms_bwd(x, g, dh_)
        return dx1_ + d, dg

    grad_x, g_norm_mix_pre = _rowwise(final, [(dx1, D, 0), (dh, D, 0), (xs, D, 0)], [vec["norm_mix_pre"]],
                                      [(D, F32)], [(1, D)], name="bwd_final")

    gs = {
        "norm_mix_pre": g_norm_mix_pre, "conv_b": g_conv_b, "conv_ln_g": g_ln_g, "conv_ln_b": g_ln_b,
        "b_pw2": g_b_pw2, "norm_mix_post": g_norm_mix_post, "norm_mlp_pre": g_norm_mlp_pre,
        "norm_mlp_post": g_norm_mlp_post, "q_norm": g_q_norm, "kv_norm": g_kv_norm,
    }
    return grad_x, gs, g_conv_w, loss


def _row1024(a):
    return jnp.pad(a, ((0, 0), (0, 1024 - a.shape[1])))


BIG = tuple(n for n, _ in EARLY + LATE)
ORDER = ("norm_mix_pre", "w_in", "q_norm", "w_uq", "kv_norm", "w_uk", "w_uv", "w_o_attn", "conv_w",
         "conv_b", "conv_ln_g", "conv_ln_b", "w_pw2", "b_pw2", "w_out", "norm_mix_post", "norm_mlp_pre",
         "w_ff1", "w_ff2", "norm_mlp_post")


def kernel(x, positions, norm_mix_pre, w_in, q_norm, w_uq, kv_norm, w_uk, w_uv, w_o_attn, conv_w, conv_b, conv_ln_g, conv_ln_b, w_pw2, b_pw2, w_out, norm_mix_post, norm_mlp_pre, w_ff1, w_ff2, norm_mlp_post, loss_target, m_norm_mix_pre, m_w_in, m_q_norm, m_w_uq, m_kv_norm, m_w_uk, m_w_uv, m_w_o_attn, m_conv_w, m_conv_b, m_conv_ln_g, m_conv_ln_b, m_w_pw2, m_b_pw2, m_w_out, m_norm_mix_post, m_norm_mlp_pre, m_w_ff1, m_w_ff2, m_norm_mlp_post, v_norm_mix_pre, v_w_in, v_q_norm, v_w_uq, v_kv_norm, v_w_uk, v_w_uv, v_w_o_attn, v_conv_w, v_conv_b, v_conv_ln_g, v_conv_ln_b, v_w_pw2, v_b_pw2, v_w_out, v_norm_mix_post, v_norm_mlp_pre, v_w_ff1, v_w_ff2, v_norm_mlp_post):
    args = dict(locals())
    wts = {n: args[n] for n in ORDER}
    mom = {n: args["m_" + n] for n in ORDER}
    var = {n: args["v_" + n] for n in ORDER}
    T = x.shape[1]
    place = jnp.stack([2 * lax.axis_index("x") + lax.axis_index("y"), lax.axis_index("c")]).astype(jnp.int32)

    w_early = _pack_shards(wts, EARLY, EARLY_ROWS)
    w_late = _pack_shards(wts, LATE, LATE_ROWS)
    cw_own = jnp.pad(conv_w.reshape(CONV_W, -1), ((0, CONV_TAPS_PAD - CONV_W), (0, 0)))
    full, cw_all = _gather_weights(w_early, cw_own)
    full = lax.dynamic_update_slice(full, w_early[None], (place[0], 0, 0))
    cw_all = lax.dynamic_update_slice(cw_all, cw_own[None], (place[0], 0, 0))
    W = _unpack_early(full)
    cw = cw_all.transpose(1, 0, 2).reshape(CONV_TAPS_PAD, D_MODEL)
    sems, w_thru, land, token = _late_gather_start(w_late, full)
    vec = {n: wts[n] for n in SMALL_VECS}
    vec["norm_mix_pre"] = vec["norm_mix_pre"] + token[:1, :1]

    def late_weights(after):
        landed = _late_gather_wait(sems, w_thru, land, after)
        return _unpack_late(lax.dynamic_update_slice(landed, w_late[None], (place[0], 0, 0)))

    sent = {}

    def send_late_grads(g):
        g32 = _pack_late_grads(g)
        sent["late"] = (g32,) + _reduce_start(g32.astype(BF16), place, name="late_grads_start")
        return sent["late"][4]

    def send_early_grads(g):
        g32 = _pack_early_grads(g)
        sent["early"] = (g32,) + _reduce_start(g32.astype(BF16), place, name="early_grads_start")
        return sent["early"][4]

    grad_x, gs, g_conv_w, loss = _local_step(x[0], positions.reshape(T, 1), loss_target[0], W, late_weights,
                                             send_late_grads, send_early_grads, cw, vec)

    def finish(group, after, tile):
        g32, sems_, g_thru, land_, _ = sent[group]
        landed = _reduce_wait(sems_, g_thru, land_, after, name=group + "_grads_wait")
        red = _reduce_sum(place, g32, landed, tile=tile, name=group + "_grads_sum")
        theirs = _join_halves(red, name=group + "_grads_join")
        return jnp.where(place[1] == 0, jnp.concatenate([red, theirs], axis=0),
                         jnp.concatenate([theirs, red], axis=0))

    def as2d(a):
        return a.reshape(-1, a.shape[-1]) if a.ndim != 4 or a.shape[2] == 1 else a.reshape(a.shape[1], -1)

    grads, delta, new_m, new_v = {}, {}, {}, {}

    def update(n):
        shp = wts[n].shape
        outs = _adamw(as2d(wts[n]), as2d(grads[n]), as2d(mom[n]), as2d(var[n]), name="adamw_" + n)
        delta[n], new_m[n], new_v[n] = (o.reshape(shp) for o in outs)

    g_late, off = finish("late", [sent["early"][4]], SUM_TILE_LATE), 0
    for name, rows in LATE:
        grads[name] = g_late[off:off + rows].reshape(wts[name].shape)
        off += rows
        update(name)

    svec = jnp.concatenate(
        [_row1024(gs[n]) for n in SMALL_VECS] + [_row1024(loss)]
        + [jnp.zeros((SMALL_CONVW_ROW - SMALL_LOSS_ROW - 1, 1024), F32), g_conv_w], axis=0)
    me = 2 * place[0] + place[1]
    ssum = _sum_small(lax.dynamic_update_slice(_share_small(svec), svec[None], (me, 0, 0)))
    for r, n in enumerate(SMALL_VECS):
        grads[n] = ssum[r:r + 1, :wts[n].shape[1]]
    cw_sum = ssum[SMALL_CONVW_ROW:SMALL_CONVW_ROW + CONV_W]
    grads["conv_w"] = lax.dynamic_slice_in_dim(cw_sum, place[0] * 256, 256, axis=1).reshape(conv_w.shape)
    loss_out = ssum[SMALL_LOSS_ROW, 0]
    update("conv_w")
    small = [jnp.concatenate([_row1024(t[n]) for n in SMALL_VECS], axis=0) for t in (wts, grads, mom, var)]
    outs = _adamw(*small, name="adamw_small")
    for r, n in enumerate(SMALL_VECS):
        delta[n], new_m[n], new_v[n] = (o[r:r + 1, :wts[n].shape[1]] for o in outs)

    g_early, off = finish("early", [grad_x, outs[0], delta["w_ff2"]], SUM_TILE_EARLY), 0
    for name, rows in EARLY:
        grads[name] = g_early[off:off + rows].reshape(wts[name].shape)
        off += rows
        update(name)

    return (loss_out, grad_x[None], *[grads[n] for n in ORDER], *[delta[n] for n in ORDER],
            *[new_m[n] for n in ORDER], *[new_v[n] for n in ORDER])
```

```python
import functools

import numpy as np
import jax
import jax.numpy as jnp
from jax import lax
from jax.experimental import pallas as pl
from jax.experimental.pallas import tpu as pltpu

F32 = jnp.float32
BF16 = jnp.bfloat16
MESH = pl.DeviceIdType.MESH

D_MODEL = 1024
N_HEADS = 8
NOPE = 128
ROPE = 64
QK_PAD = 256
Q_RANK = 384
KV_RANK = 256
CONV_W = 31
CONV_TAPS_PAD = 32
HALO = 32
D_FF = 4096
EPS = 1e-6
ROPE_THETA = 10000.0
SCALE = float((NOPE + ROPE) ** -0.5)
Q_SCALE = SCALE * float(np.log2(np.e))
K_UNSCALE = float(np.log(2.0))
Z_SMALL = 1024
Z_COLS = Z_SMALL + 4 * D_MODEL
N_CHIPS = 4
N_DEV = 8

ADAM_LR = 0.001
ADAM_B1 = 0.9
ADAM_B2 = 0.999
ADAM_EPS = 1e-08
ADAM_WD = 0.01
ADAM_STEP = 10

VMEM_LIMIT = 48 * 1024 * 1024
NEG = -1e30

EARLY = (("w_in", 1200), ("w_uq", 144), ("w_uk", 64), ("w_uv", 64))
EARLY_ROWS = 1536
LATE = (("w_ff1", 0, 1024), ("w_o_attn", 1024, 256), ("w_pw2", 1536, 256), ("w_out", 1792, 256),
        ("w_ff2", 2048, 1024))
LATE_OFF = {n: off for n, off, _ in LATE}
LATE_ROWS = 3072
SUM_TILE = 256

SMALL_VECS = ("norm_mix_pre", "conv_b", "conv_ln_g", "conv_ln_b", "b_pw2", "norm_mix_post",
              "norm_mlp_pre", "norm_mlp_post", "q_norm", "kv_norm")
SMALL_LOSS_ROW = 10
SMALL_CONVW_ROW = 16
SMALL_ROWS = 48


ANY = pl.BlockSpec(memory_space=pl.ANY)


def _params(*sem):
    return pltpu.CompilerParams(dimension_semantics=sem, vmem_limit_bytes=VMEM_LIMIT)


def _tile(n, cap):
    if n <= cap:
        return n
    t = (cap // 128) * 128
    while n % t:
        t -= 128
    return t


def _row_tile(n, cap):
    if n <= cap:
        return n
    t = (cap // 8) * 8
    while n % t:
        t -= 8
    return t


MM_VMEM_BUDGET = 40 * 1024 * 1024


def _mm_tiles(M, N, K, in_bytes, out_bytes):
    tn = _tile(N, 1024)
    tk = K
    while True:
        for cap in (1024, 512, 256):
            tm = _tile(M, cap)
            need = 2 * tk * (tm * in_bytes[0] + tn * in_bytes[1]) + 2 * tm * tn * out_bytes
            need += tm * tn * 4 if tk < K else 0
            if need <= MM_VMEM_BUDGET:
                return tm, tn, tk
        tk = _tile(K, tk - 128)


def _mm(a, b, *, name, ta=False, tb=False, out_dtype=F32, extras=(), epilogue=None, out_dtypes=None,
        after=(), shape=None, tiles=None, b_spec=None, into=None):
    if shape is None:
        M, K = (a.shape[1], a.shape[0]) if ta else a.shape
        N = b.shape[0] if tb else b.shape[1]
        assert K == (b.shape[1] if tb else b.shape[0]), (a.shape, b.shape, ta, tb)
    else:
        M, N, K = shape
    out_dtypes = tuple(out_dtypes or (out_dtype,))
    if into is not None:
        out_dtypes = tuple(buf.dtype for buf, _ in into)
    if tiles is None:
        out_bytes = sum(jnp.dtype(d).itemsize for d in out_dtypes) + sum(e.dtype.itemsize for e in extras)
        tiles = _mm_tiles(M, N, K, (a.dtype.itemsize, b.dtype.itemsize), out_bytes)
    tm, tn, tk = tiles
    nk = K // tk
    dims = (((0 if ta else 1,), (1 if tb else 0,)), ((), ()))
    n_ex, n_out = len(extras), len(out_dtypes)
    n_pass = len(after) + (n_out if into is not None else 0)

    def body(*refs):
        a_ref, b_ref = refs[:2]
        ex_refs = refs[2:2 + n_ex]
        o_refs = refs[2 + n_ex + n_pass:2 + n_ex + n_pass + n_out]
        k = pl.program_id(2)
        bv = b_ref[...]
        if bv.ndim == 3:
            bv = bv.reshape(-1, bv.shape[-1])
        part = lax.dot_general(a_ref[...].astype(BF16), bv.astype(BF16), dims, preferred_element_type=F32)

        def finish(acc):
            res = epilogue(acc, *[r[...] for r in ex_refs]) if epilogue else (acc,) * n_out
            for o_ref, r in zip(o_refs, res):
                o_ref[...] = r.astype(o_ref.dtype)

        if nk == 1:
            finish(part)
        else:
            acc_ref = refs[-1]

            @pl.when(k == 0)
            def _():
                acc_ref[...] = part

            @pl.when((k > 0) & (k < nk - 1))
            def _():
                acc_ref[...] += part

            @pl.when(k == nk - 1)
            def _():
                finish(acc_ref[...] + part)

    a_spec = (pl.BlockSpec((tk, tm), lambda i, j, k: (k, i)) if ta
              else pl.BlockSpec((tm, tk), lambda i, j, k: (i, k)))
    if b_spec is None:
        b_spec = (pl.BlockSpec((tn, tk), lambda i, j, k: (j, k)) if tb
                  else pl.BlockSpec((tk, tn), lambda i, j, k: (k, j)))
    o_spec = pl.BlockSpec((tm, tn), lambda i, j, k: (i, j))
    if into is None:
        out_shape = [jax.ShapeDtypeStruct((M, N), dt) for dt in out_dtypes]
        out_specs, passed, aliases = [o_spec] * n_out, list(after), {}
    else:
        out_shape = [jax.ShapeDtypeStruct(buf.shape, buf.dtype) for buf, _ in into]
        out_specs = [spec for _, spec in into]
        passed = list(after) + [buf for buf, _ in into]
        aliases = {2 + n_ex + len(after) + o: o for o in range(n_out)}
    res = pl.pallas_call(
        body, name=name, out_shape=out_shape,
        grid=(M // tm, N // tn, nk),
        in_specs=[a_spec, b_spec] + [o_spec] * n_ex + [ANY] * len(passed),
        out_specs=out_specs, input_output_aliases=aliases,
        scratch_shapes=[pltpu.VMEM((tm, tn), F32)] if nk > 1 else [],
        compiler_params=_params("parallel", "parallel", "arbitrary"),
    )(a, b, *extras, *passed)
    return res[0] if n_out == 1 else res


class _LateWeights:
    def __init__(self, pack):
        self.pack = pack

    def fwd(self, x, n, *, name, **kw):
        T, off = x.shape[0], LATE_OFF[n]
        tm = _tile(T, 512)
        if n == "w_ff1":
            return _mm(x, self.pack, name=name, shape=(T, 4096, 1024), tiles=(_tile(T, 1024), 1024, 1024),
                       b_spec=pl.BlockSpec((None, 1024, 1024), lambda i, j, k: (j, 0, 0)), **kw)
        if n == "w_ff2":
            view = self.pack.reshape(N_CHIPS, LATE_ROWS // 1024, 1024, 1024)
            return _mm(x, view, name=name, shape=(T, 1024, 4096), tiles=(tm, 1024, 4096),
                       b_spec=pl.BlockSpec((N_CHIPS, None, 1024, 1024), lambda i, j, k: (0, off // 1024, 0, 0)),
                       **kw)
        return _mm(x, self.pack, name=name, shape=(T, 1024, 1024), tiles=(_tile(T, 1024), 1024, 256),
                   b_spec=pl.BlockSpec((None, 256, 1024), lambda i, j, k: (k, off // 256, 0)), **kw)

    def bwd(self, dy, n, *, name, **kw):
        T, off = dy.shape[0], LATE_OFF[n]
        tm = _tile(T, 1024)
        if n == "w_ff1":
            return _mm(dy, self.pack, tb=True, name=name, shape=(T, 1024, 4096), tiles=(tm, 1024, 1024),
                       b_spec=pl.BlockSpec((None, 1024, 1024), lambda i, j, k: (k, 0, 0)), **kw)
        if n == "w_ff2":
            return _mm(dy, self.pack, tb=True, name=name, shape=(T, 4096, 1024), tiles=(tm, 1024, 1024),
                       b_spec=pl.BlockSpec((None, 1024, 1024), lambda i, j, k: (j, off // 1024, 0)), **kw)
        return _mm(dy, self.pack, tb=True, name=name, shape=(T, 1024, 1024), tiles=(tm, 256, 1024),
                   b_spec=pl.BlockSpec((None, 256, 1024), lambda i, j, k: (j, off // 256, 0)), **kw)


def _late_grad(x, dy, n, bufs, *, name):
    T, off = x.shape[0], LATE_OFF[n]
    if n == "w_ff1":
        spec = pl.BlockSpec((None, 512, 1024), lambda i, j, k: (j, i, 0))
        shape, tiles = (1024, 4096, T), (512, 1024, T)
    elif n == "w_ff2":
        spec = pl.BlockSpec((None, 512, 1024), lambda i, j, k: (i // 2, off // 512 + i % 2, 0))
        shape, tiles = (4096, 1024, T), (512, 1024, T)
    else:
        spec = pl.BlockSpec((None, 256, 1024), lambda i, j, k: (i, off // 256, 0))
        shape, tiles = (1024, 1024, T), (256, 1024, T)
    return _mm(x, dy, ta=True, name=name, shape=shape, tiles=tiles, into=[(b, spec) for b in bufs])


def _rowwise(fn, rows, consts, outs, reds=(), *, name, tm=256, after=()):
    T = rows[0][0].shape[0]
    tm = min(tm, T)
    n_row, n_const, n_out = len(rows), len(consts), len(outs)

    def body(*refs):
        i = pl.program_id(0)
        vals = [r[...] for r in refs[:n_row + n_const]]
        res = fn(*vals)
        out_refs = refs[n_row + n_const + len(after):]
        for k in range(n_out):
            out_refs[k][...] = res[k].astype(out_refs[k].dtype)
        for k in range(len(reds)):
            ref = out_refs[n_out + k]

            @pl.when(i == 0)
            def _(ref=ref):
                ref[...] = jnp.zeros_like(ref)

            ref[...] += res[n_out + k]

    in_specs = [pl.BlockSpec((tm, w), functools.partial(lambda i, cb: (i, cb), cb=cb))
                for (_, w, cb) in rows]
    in_specs += [pl.BlockSpec(c.shape, lambda i: (0, 0)) for c in consts]
    in_specs += [ANY] * len(after)
    out_specs = [pl.BlockSpec((tm, w), lambda i: (i, 0)) for (w, _) in outs]
    out_specs += [pl.BlockSpec(s, lambda i: (0, 0)) for s in reds]
    out_shape = [jax.ShapeDtypeStruct((T, w), dt) for (w, dt) in outs]
    out_shape += [jax.ShapeDtypeStruct(s, F32) for s in reds]
    return pl.pallas_call(
        body, name=name, out_shape=out_shape, grid=(T // tm,),
        in_specs=in_specs, out_specs=out_specs,
        compiler_params=_params("arbitrary"),
    )(*[r[0] for r in rows], *consts, *after)


def _rms(x, g):
    r = lax.rsqrt(jnp.mean(x * x, axis=-1, keepdims=True) + EPS)
    return x * r * g


def _rms_bwd(x, g, dy):
    r = lax.rsqrt(jnp.mean(x * x, axis=-1, keepdims=True) + EPS)
    n = x * r
    dyg = dy * g
    dx = r * (dyg - n * jnp.mean(dyg * n, axis=-1, keepdims=True))
    return dx, jnp.sum(dy * n, axis=0, keepdims=True)


def _sigmoid(x):
    return 1.0 / (1.0 + jnp.exp(-x))


def _rope(x, cs, sa, sb):
    return x * cs + pltpu.roll(x, 32, 1) * sa + pltpu.roll(x, 96, 1) * sb


def _rope_t(dy, cs, sa, sb):
    return dy * cs + pltpu.roll(dy * sa, 96, 1) + pltpu.roll(dy * sb, 32, 1)


def _causal_mask(t):
    row = lax.broadcasted_iota(jnp.int32, (t, t), 0)
    col = lax.broadcasted_iota(jnp.int32, (t, t), 1)
    return col <= row


FLASH_FWD_TQ = 1024
FLASH_FWD_TK = 1024
FLASH_BWD_TQ = 1024
FLASH_BWD_TK = 512
NT_DIMS = (((1,), (1,)), ((), ()))
TN_DIMS = (((0,), (0,)), ((), ()))


def _flash_fwd(q, k, v, *, tq=FLASH_FWD_TQ, tk=FLASH_FWD_TK):
    T = q.shape[0]
    tq, tk = min(tq, T), min(tk, T)
    n_diag = tq // tk
    assert n_diag >= 1

    def body(q_ref, k_ref, v_ref, o_ref, lse_ref):
        i = pl.program_id(1)
        qv = q_ref[...]

        def logits(j):
            sl = pl.ds(pl.multiple_of(j * tk, tk), tk)
            return lax.dot_general(qv, k_ref[sl, :], NT_DIMS, preferred_element_type=F32)

        def update(j, s, m, acc, masked):
            if masked:
                row = i * tq + lax.broadcasted_iota(jnp.int32, (tq, tk), 0)
                col = j * tk + lax.broadcasted_iota(jnp.int32, (tq, tk), 1)
                s = jnp.where(col <= row, s, NEG)
            m_new = jnp.maximum(m, jnp.max(s, axis=-1, keepdims=True))
            p = jnp.exp2(s - m_new)
            sl = pl.ds(pl.multiple_of(j * tk, tk), tk)
            acc = jnp.exp2(m - m_new) * acc + jnp.dot(p.astype(BF16), v_ref[sl, :],
                                                      preferred_element_type=F32)
            return m_new, acc

        def step(j, carry):
            s, m, acc = carry
            s_next = logits(j + 1)
            m, acc = update(j, s, m, acc, False)
            return s_next, m, acc

        first = i * n_diag
        init = (logits(0), jnp.full((tq, 1), NEG, F32), jnp.zeros((tq, QK_PAD), F32))
        s, m, acc = lax.fori_loop(0, first, step, init)
        for d in range(n_diag):
            m, acc = update(first + d, s, m, acc, True)
            if d + 1 < n_diag:
                s = logits(first + d + 1)
        l = acc[:, NOPE:]
        o_ref[...] = (acc[:, :NOPE] / l).astype(BF16)
        lse_ref[...] = m + jnp.log2(l)

    return pl.pallas_call(
        body, name="flash_fwd",
        out_shape=[jax.ShapeDtypeStruct((T, N_HEADS * NOPE), BF16),
                   jax.ShapeDtypeStruct((T, N_HEADS * NOPE), F32)],
        grid=(N_HEADS, T // tq),
        in_specs=[pl.BlockSpec((tq, QK_PAD), lambda h, i: (i, h)),
                  pl.BlockSpec((T, QK_PAD), lambda h, i: (0, h)),
                  pl.BlockSpec((T, QK_PAD), lambda h, i: (0, h))],
        out_specs=[pl.BlockSpec((tq, NOPE), lambda h, i: (i, h)),
                   pl.BlockSpec((tq, NOPE), lambda h, i: (i, h))],
        compiler_params=_params("parallel", "arbitrary"),
    )(q, k, v)


def _flash_bwd(q, k, v, do, lse, *, tq=FLASH_BWD_TQ, tk=FLASH_BWD_TK):
    T = q.shape[0]
    tq, tk = min(tq, T), min(tk, T)
    nq = T // tq
    n_diag = max(1, tk // tq)

    def body(q_ref, k_ref, v_ref, do_ref, lse_ref, dq_ref, dk_ref, dv_ref):
        j = pl.program_id(1)

        @pl.when(j == 0)
        def _():
            dq_ref[...] = jnp.zeros_like(dq_ref)

        kj = k_ref[...]
        vj = v_ref[...]

        def step(i, carry, masked):
            dk, dv = carry
            sl = pl.ds(pl.multiple_of(i * tq, tq), tq)
            qi = q_ref[sl, :]
            doi = do_ref[sl, :]
            s = lax.dot_general(qi, kj, NT_DIMS, preferred_element_type=F32)
            p = jnp.exp2(s - lse_ref[sl, :][:, :1])
            if masked:
                row = i * tq + lax.broadcasted_iota(jnp.int32, (tq, tk), 0)
                col = j * tk + lax.broadcasted_iota(jnp.int32, (tq, tk), 1)
                p = jnp.where(col <= row, p, 0.0)
            dv = dv + lax.dot_general(p.astype(BF16), doi, TN_DIMS, preferred_element_type=F32)
            ds = (p * lax.dot_general(doi, vj, NT_DIMS, preferred_element_type=F32)).astype(BF16)
            dk = dk + lax.dot_general(ds, qi, TN_DIMS, preferred_element_type=F32)
            dq_ref[sl, :] += jnp.dot(ds, kj, preferred_element_type=F32)
            return dk, dv

        first = (j * tk) // tq
        carry = (jnp.zeros((tk, QK_PAD), F32), jnp.zeros((tk, QK_PAD), F32))
        for d in range(n_diag):
            carry = step(first + d, carry, True)
        dk, dv = lax.fori_loop(first + n_diag, nq, lambda i, c: step(i, c, False), carry)
        dk_ref[...] = dk
        dv_ref[...] = dv[:, :NOPE]

    return pl.pallas_call(
        body, name="flash_bwd",
        out_shape=[jax.ShapeDtypeStruct((T, N_HEADS * QK_PAD), F32),
                   jax.ShapeDtypeStruct((T, N_HEADS * QK_PAD), F32),
                   jax.ShapeDtypeStruct((T, N_HEADS * NOPE), F32)],
        grid=(N_HEADS, T // tk),
        in_specs=[pl.BlockSpec((T, QK_PAD), lambda h, j: (0, h)),
                  pl.BlockSpec((tk, QK_PAD), lambda h, j: (j, h)),
                  pl.BlockSpec((tk, QK_PAD), lambda h, j: (j, h)),
                  pl.BlockSpec((T, QK_PAD), lambda h, j: (0, h)),
                  pl.BlockSpec((T, NOPE), lambda h, j: (0, h))],
        out_specs=[pl.BlockSpec((T, QK_PAD), lambda h, j: (0, h)),
                   pl.BlockSpec((tk, QK_PAD), lambda h, j: (j, h)),
                   pl.BlockSpec((tk, NOPE), lambda h, j: (j, h))],
        compiler_params=_params("arbitrary", "arbitrary"),
    )(q, k, v, do, lse)


SUB = 128


def _windows(buf, t0, c0, first, last):
    span = SUB + 8 * ((last + 7) // 8)
    base = buf[t0:t0 + span, c0:c0 + SUB]
    for r in range(8):
        offs = [o for o in range(first, last + 1) if o % 8 == r]
        if offs:
            win = base if r == 0 else pltpu.roll(base, span - r, 0)
            for o in offs:
                yield o, win[o - r:o - r + SUB, :]


def _conv_fwd(u, w, bias, *, tm=256, ct=256):
    T, C = u.shape
    tm = min(tm, T)
    hb = tm // HALO
    lead = HALO - (CONV_W - 1)

    def body(cur_ref, halo_ref, w_ref, b_ref, o_ref, buf):
        t = pl.program_id(1)
        buf[0:HALO, :] = jnp.where(t > 0, halo_ref[...], 0.0)
        buf[HALO:, :] = cur_ref[...]
        for c0 in range(0, ct, SUB):
            for t0 in range(0, tm, SUB):
                acc = jnp.broadcast_to(b_ref[:, c0:c0 + SUB], (SUB, SUB))
                for o, win in _windows(buf, t0, c0, lead, lead + CONV_W - 1):
                    acc = acc + win * w_ref[o - lead:o - lead + 1, c0:c0 + SUB]
                o_ref[t0:t0 + SUB, c0:c0 + SUB] = acc

    return pl.pallas_call(
        body, name="conv_fwd",
        out_shape=jax.ShapeDtypeStruct((T, C), F32),
        grid=(C // ct, T // tm),
        in_specs=[pl.BlockSpec((tm, ct), lambda c, t: (t, c)),
                  pl.BlockSpec((HALO, ct), lambda c, t: (jnp.maximum(t * hb - 1, 0), c)),
                  pl.BlockSpec((CONV_TAPS_PAD, ct), lambda c, t: (0, c)),
                  pl.BlockSpec((1, ct), lambda c, t: (0, c))],
        out_specs=pl.BlockSpec((tm, ct), lambda c, t: (t, c)),
        scratch_shapes=[pltpu.VMEM((HALO + tm, ct), F32)],
        compiler_params=_params("parallel", "arbitrary"),
    )(u, u, w, bias)


def _conv_bwd(u, du1, w, *, tm=256, ct=256):
    T, C = u.shape
    tm = min(tm, T)
    hb = tm // HALO
    nt = T // tm
    last_halo = T // HALO - 1
    lead = HALO - (CONV_W - 1)

    def body(u_ref, uh_ref, d_ref, dh_ref, w_ref, du_ref, dw_ref, ubuf, dbuf, dw8):
        t = pl.program_id(1)
        ubuf[0:HALO, :] = jnp.where(t > 0, uh_ref[...], 0.0)
        ubuf[HALO:, :] = u_ref[...]
        dbuf[0:tm, :] = d_ref[...]
        dbuf[tm:, :] = jnp.where(t < nt - 1, dh_ref[...], 0.0)

        @pl.when(t == 0)
        def _():
            dw8[...] = jnp.zeros_like(dw8)

        for c0 in range(0, ct, SUB):
            for t0 in range(0, tm, SUB):
                acc = jnp.zeros((SUB, SUB), F32)
                for o, win in _windows(dbuf, t0, c0, 0, CONV_W - 1):
                    kk = CONV_W - 1 - o
                    acc = acc + win * w_ref[kk:kk + 1, c0:c0 + SUB]
                du_ref[t0:t0 + SUB, c0:c0 + SUB] = acc
                dsub = d_ref[t0:t0 + SUB, c0:c0 + SUB]
                for o, win in _windows(ubuf, t0, c0, lead, lead + CONV_W - 1):
                    kk = o - lead
                    dw8[8 * kk:8 * kk + 8, c0:c0 + SUB] += jnp.sum(
                        (dsub * win).reshape(SUB // 8, 8, SUB), axis=0)

        @pl.when(t == nt - 1)
        def _():
            dw_ref[...] = jnp.sum(dw8[...].reshape(CONV_TAPS_PAD, 8, ct), axis=1)

    return pl.pallas_call(
        body, name="conv_bwd",
        out_shape=[jax.ShapeDtypeStruct((T, C), F32), jax.ShapeDtypeStruct((CONV_TAPS_PAD, C), F32)],
        grid=(C // ct, nt),
        in_specs=[pl.BlockSpec((tm, ct), lambda c, t: (t, c)),
                  pl.BlockSpec((HALO, ct), lambda c, t: (jnp.maximum(t * hb - 1, 0), c)),
                  pl.BlockSpec((tm, ct), lambda c, t: (t, c)),
                  pl.BlockSpec((HALO, ct), lambda c, t: (jnp.minimum((t + 1) * hb, last_halo), c)),
                  pl.BlockSpec((CONV_TAPS_PAD, ct), lambda c, t: (0, c))],
        out_specs=[pl.BlockSpec((tm, ct), lambda c, t: (t, c)),
                   pl.BlockSpec((CONV_TAPS_PAD, ct), lambda c, t: (0, c))],
        scratch_shapes=[pltpu.VMEM((HALO + tm, ct), F32), pltpu.VMEM((tm + HALO, ct), F32),
                        pltpu.VMEM((CONV_TAPS_PAD * 8, ct), F32)],
        compiler_params=_params("parallel", "arbitrary"),
    )(u, u, du1, du1, w)


def _adamw(w, g, m, v, *, name, g_row=None):
    R, C = w.shape
    tr = _row_tile(R, max(8, (1 << 18) // C // 8 * 8))
    first = 0 if g_row is None else g_row // tr
    assert g_row is None or g_row % tr == 0

    def body(w_ref, g_ref, m_ref, v_ref, d_ref, nm_ref, nv_ref, *g_out):
        gv = g_ref[...]
        nm = ADAM_B1 * m_ref[...] + (1.0 - ADAM_B1) * gv
        nv = ADAM_B2 * v_ref[...] + (1.0 - ADAM_B2) * (gv * gv)
        m_hat = nm / (1.0 - ADAM_B1 ** ADAM_STEP)
        v_hat = nv / (1.0 - ADAM_B2 ** ADAM_STEP)
        d_ref[...] = -ADAM_LR * (m_hat / (jnp.sqrt(v_hat) + ADAM_EPS) + ADAM_WD * w_ref[...])
        nm_ref[...] = nm
        nv_ref[...] = nv
        for ref in g_out:
            ref[...] = gv

    spec = pl.BlockSpec((tr, C), lambda i: (i, 0))
    g_spec = pl.BlockSpec((tr, C), lambda i: (first + i, 0))
    n_out = 3 if g_row is None else 4
    return pl.pallas_call(
        body, name=name, out_shape=[jax.ShapeDtypeStruct((R, C), F32)] * n_out,
        grid=(R // tr,), in_specs=[spec, g_spec, spec, spec], out_specs=[spec] * n_out,
        compiler_params=_params("parallel"),
    )(w, g, m, v)


def _coords():
    return lax.axis_index("x"), lax.axis_index("y"), lax.axis_index("c")


def _remote(src, dst, ssem, rsem, to):
    return pltpu.make_async_remote_copy(src_ref=src, dst_ref=dst, send_sem=ssem, recv_sem=rsem,
                                        device_id=to, device_id_type=MESH)


def _half(c, rows):
    return pl.ds(pl.multiple_of(c * (rows // 2), 16), rows // 2)


def _gather_weights(wpack, cw):
    def body(w_ref, cw_ref, full_ref, cwf_ref, ssem, rsem):
        x, y, c = _coords()
        q = 2 * x + y
        sib = (x, y, 1 - c)
        chips = [(1 - x, y), (x, 1 - y), (1 - x, 1 - y)]
        mine, other = _half(c, w_ref.shape[0]), _half(1 - c, w_ref.shape[0])
        first = []
        for j, (px, py) in enumerate(chips):
            first.append(_remote(w_ref.at[mine], full_ref.at[q, mine], ssem.at[j], rsem.at[j], (px, py, c)))
            first.append(_remote(cw_ref, cwf_ref.at[q], ssem.at[6 + j], rsem.at[6 + j], (px, py, c)))
        for cp in first:
            cp.start()
        passed = []
        for j, (px, py) in enumerate(chips):
            pq = 2 * px + py
            _remote(w_ref.at[mine], full_ref.at[pq, mine], ssem.at[j], rsem.at[j], (px, py, c)).wait_recv()
            fw = _remote(full_ref.at[pq, mine], full_ref.at[pq, mine], ssem.at[3 + j], rsem.at[3 + j], sib)
            fw.start()
            passed.append(fw)
        for j, (px, py) in enumerate(chips):
            pq = 2 * px + py
            _remote(full_ref.at[pq, other], full_ref.at[pq, other], ssem.at[3 + j], rsem.at[3 + j], sib).wait_recv()
            _remote(cw_ref, cwf_ref.at[pq], ssem.at[6 + j], rsem.at[6 + j], (px, py, c)).wait_recv()
        for cp in first + passed:
            cp.wait_send()

    return pl.pallas_call(
        body, name="gather_weights",
        out_shape=[jax.ShapeDtypeStruct((N_CHIPS,) + wpack.shape, wpack.dtype),
                   jax.ShapeDtypeStruct((N_CHIPS,) + cw.shape, cw.dtype)],
        in_specs=[ANY, ANY], out_specs=[ANY, ANY],
        scratch_shapes=[pltpu.SemaphoreType.DMA((9,)), pltpu.SemaphoreType.DMA((9,))],
    )(wpack, cw)


SEM = pl.BlockSpec(memory_space=pltpu.SEMAPHORE)
HBM = pl.BlockSpec(memory_space=pltpu.HBM)
N_LATE = 6


def _late_peers(x, y, c):
    out = []
    for j, (px, py) in enumerate([(1 - x, y), (x, 1 - y), (1 - x, 1 - y)]):
        for t in (0, 1):
            out.append((2 * j + t, px, py, c if t == 0 else 1 - c))
    return out


def _late_gather_start(w, after):
    rows = w.shape[0]

    def body(w_ref, land_ref, after_ref, *outs):
        sems = outs[:2 * N_LATE]
        token = outs[2 * N_LATE + 2]
        x, y, c = _coords()
        q = 2 * x + y
        mine = _half(c, rows)
        for k, px, py, pc in _late_peers(x, y, c):
            _remote(w_ref.at[mine], land_ref.at[q, mine], sems[k], sems[N_LATE + k], (px, py, pc)).start()
        token[...] = jnp.zeros_like(token)

    res = pl.pallas_call(
        body, name="late_gather_start",
        out_shape=tuple([pltpu.SemaphoreType.DMA(())] * (2 * N_LATE)) + (
            pltpu.HBM(w.shape, w.dtype), pltpu.HBM((N_CHIPS,) + w.shape, w.dtype),
            jax.ShapeDtypeStruct((8, 128), F32)),
        in_specs=(HBM, HBM, ANY),
        out_specs=tuple([SEM] * (2 * N_LATE)) + (HBM, HBM, pl.BlockSpec(memory_space=pltpu.VMEM)),
        input_output_aliases={0: 2 * N_LATE, 1: 2 * N_LATE + 1},
        compiler_params=pltpu.CompilerParams(has_side_effects=pltpu.SideEffectType.DATAFLOW_SIDE_EFFECTING),
    )(pltpu.with_memory_space_constraint(w, pltpu.HBM),
      pltpu.with_memory_space_constraint(lax.empty((N_CHIPS,) + w.shape, w.dtype), pltpu.HBM), after)
    return res[:2 * N_LATE], res[2 * N_LATE], res[2 * N_LATE + 1], res[2 * N_LATE + 2]


def _late_gather_wait(sems, w_thru, land_thru, after):
    rows = w_thru.shape[0]

    def body(w_ref, land_ref, *rest):
        sems = rest[:2 * N_LATE]
        x, y, c = _coords()
        for k, px, py, pc in _late_peers(x, y, c):
            cp = _remote(w_ref.at[_half(pc, rows)], land_ref.at[2 * px + py, _half(pc, rows)],
                         sems[k], sems[N_LATE + k], (px, py, pc))
            cp.wait_send()
            cp.wait_recv()

    return pl.pallas_call(
        body, name="late_gather_wait",
        out_shape=(pltpu.HBM(w_thru.shape, w_thru.dtype), pltpu.HBM(land_thru.shape, land_thru.dtype)),
        in_specs=(HBM, HBM) + tuple([SEM] * (2 * N_LATE)) + (ANY,),
        out_specs=(HBM, HBM), input_output_aliases={0: 0, 1: 1},
        compiler_params=pltpu.CompilerParams(has_side_effects=pltpu.SideEffectType.DATAFLOW_SIDE_EFFECTING),
    )(w_thru, land_thru, *sems, after)[1]


N_PEERS = N_DEV - 1


def _peers(x, y, c):
    out = []
    for r in range(1, N_DEV):
        out.append((r - 1, 1 - x if r & 4 else x, 1 - y if r & 2 else y, 1 - c if r & 1 else c))
    return out


def _reduce_start(g16, after, *, name):
    rows = g16.shape[1]

    def body(g_ref, land_ref, after_ref, *outs):
        sems = outs[:2 * N_PEERS]
        token = outs[2 * N_PEERS + 2]
        x, y, c = _coords()
        for k, px, py, pc in _peers(x, y, c):
            _remote(g_ref.at[2 * px + py, _half(pc, rows)], land_ref.at[k], sems[k], sems[N_PEERS + k],
                    (px, py, pc)).start()
        token[...] = jnp.zeros_like(token)

    land_shape = (N_PEERS, rows // 2, 1024)
    res = pl.pallas_call(
        body, name=name,
        out_shape=tuple([pltpu.SemaphoreType.DMA(())] * (2 * N_PEERS)) + (
            pltpu.HBM(g16.shape, g16.dtype), pltpu.HBM(land_shape, g16.dtype),
            jax.ShapeDtypeStruct((8, 128), F32)),
        in_specs=(HBM, HBM, ANY),
        out_specs=tuple([SEM] * (2 * N_PEERS)) + (HBM, HBM, pl.BlockSpec(memory_space=pltpu.VMEM)),
        input_output_aliases={0: 2 * N_PEERS, 1: 2 * N_PEERS + 1},
        compiler_params=pltpu.CompilerParams(has_side_effects=pltpu.SideEffectType.DATAFLOW_SIDE_EFFECTING),
    )(pltpu.with_memory_space_constraint(g16, pltpu.HBM),
      pltpu.with_memory_space_constraint(lax.empty(land_shape, g16.dtype), pltpu.HBM), after)
    return res[:2 * N_PEERS], res[2 * N_PEERS], res[2 * N_PEERS + 1], res[2 * N_PEERS + 2]


def _reduce_wait(sems, g_thru, land_thru, after, *, name):
    rows = g_thru.shape[1]

    def body(g_ref, land_ref, *rest):
        sems = rest[:2 * N_PEERS]
        x, y, c = _coords()
        for k, px, py, pc in _peers(x, y, c):
            cp = _remote(g_ref.at[2 * px + py, _half(pc, rows)], land_ref.at[k], sems[k], sems[N_PEERS + k],
                         (px, py, pc))
            cp.wait_send()
            cp.wait_recv()

    return pl.pallas_call(
        body, name=name,
        out_shape=(pltpu.HBM(g_thru.shape, g_thru.dtype), pltpu.HBM(land_thru.shape, land_thru.dtype)),
        in_specs=(HBM, HBM) + tuple([SEM] * (2 * N_PEERS)) + tuple([ANY] * len(after)),
        out_specs=(HBM, HBM), input_output_aliases={0: 0, 1: 1},
        compiler_params=pltpu.CompilerParams(has_side_effects=pltpu.SideEffectType.DATAFLOW_SIDE_EFFECTING),
    )(g_thru, land_thru, *sems, *after)[1]


def _reduce_sum(place, g32, land, *, name):
    rows = g32.shape[1]
    nb = rows // 2 // SUM_TILE

    def body(place_ref, g_ref, l_ref, o_ref):
        acc = g_ref[...]
        for k in range(N_PEERS):
            acc = acc + l_ref[k].astype(F32)
        o_ref[...] = acc

    return pl.pallas_call(
        body, name=name,
        out_shape=jax.ShapeDtypeStruct((rows, 1024), F32),
        grid_spec=pltpu.PrefetchScalarGridSpec(
            num_scalar_prefetch=1, grid=(nb,),
            in_specs=[pl.BlockSpec((None, SUM_TILE, 1024), lambda i, s: (s[0], s[1] * nb + i, 0)),
                      pl.BlockSpec((N_PEERS, SUM_TILE, 1024), lambda i, s: (0, i, 0))],
            out_specs=pl.BlockSpec((SUM_TILE, 1024), lambda i, s: (s[1] * nb + i, 0))),
        compiler_params=_params("parallel"),
    )(place, g32, land)


def _join_halves(red, *, name):
    rows = red.shape[0]

    def body(r_ref, o_ref, ssem, rsem):
        x, y, c = _coords()
        cp = _remote(r_ref.at[_half(c, rows)], o_ref.at[_half(c, rows)], ssem, rsem, (x, y, 1 - c))
        cp.start()
        _remote(r_ref.at[_half(1 - c, rows)], o_ref.at[_half(1 - c, rows)], ssem, rsem,
                (x, y, 1 - c)).wait_recv()
        cp.wait_send()

    return pl.pallas_call(
        body, name=name,
        out_shape=jax.ShapeDtypeStruct(red.shape, F32),
        in_specs=[ANY], out_specs=ANY, input_output_aliases={0: 0},
        scratch_shapes=[pltpu.SemaphoreType.DMA, pltpu.SemaphoreType.DMA],
    )(red)


def _share_small(svec):
    def body(s_ref, all_ref, ssem, rsem):
        x, y, c = _coords()
        me = 4 * x + 2 * y + c
        peers = []
        for r in range(1, N_DEV):
            px = 1 - x if r & 4 else x
            py = 1 - y if r & 2 else y
            pc = 1 - c if r & 1 else c
            peers.append((px, py, pc))
        cps = [_remote(s_ref, all_ref.at[me], ssem.at[k], rsem.at[k], p) for k, p in enumerate(peers)]
        for cp in cps:
            cp.start()
        for k, (px, py, pc) in enumerate(peers):
            _remote(s_ref, all_ref.at[4 * px + 2 * py + pc], ssem.at[k], rsem.at[k], (px, py, pc)).wait_recv()
        for cp in cps:
            cp.wait_send()

    return pl.pallas_call(
        body, name="share_small",
        out_shape=jax.ShapeDtypeStruct((N_DEV,) + svec.shape, F32),
        in_specs=[ANY], out_specs=ANY,
        scratch_shapes=[pltpu.SemaphoreType.DMA((N_DEV - 1,)), pltpu.SemaphoreType.DMA((N_DEV - 1,))],
    )(svec)


def _sum_small(allv):
    def body(a_ref, o_ref):
        acc = a_ref[0]
        for d in range(1, N_DEV):
            acc = acc + a_ref[d]
        o_ref[...] = acc

    return pl.pallas_call(
        body, name="sum_small",
        out_shape=jax.ShapeDtypeStruct(allv.shape[1:], F32),
    )(allv)


def _pack_early_shards(wts):
    parts = [wts["w_in"][0].T.astype(BF16)] + [wts[n].astype(BF16).reshape(-1, 1024) for n, _ in EARLY[1:]]
    used = sum(r for _, r in EARLY)
    return jnp.concatenate(parts + [jnp.zeros((EARLY_ROWS - used, 1024), BF16)], axis=0)


def _pack_late_shards(wts):
    parts, at = [], 0
    for n, off, rows in LATE:
        if off > at:
            parts.append(jnp.zeros((off - at, 1024), BF16))
        parts.append(wts[n].astype(BF16).reshape(rows, 1024))
        at = off + rows
    return jnp.concatenate(parts, axis=0)


def _unpack_early(full):
    w = {}
    w["w_in_t"] = jnp.concatenate(
        [full[0, :704], jnp.zeros((Z_SMALL - 704, 1024), full.dtype), full[0, 704:1200]]
        + [full[p, :1200] for p in range(1, N_CHIPS)], axis=0)
    w["w_uq"] = full[:, 1200:1344].reshape(4, 384, 384).transpose(1, 0, 2).reshape(384, 1536)
    w["w_uk"] = full[:, 1344:1408].reshape(256, 1024)
    w["w_uv"] = full[:, 1408:1472].reshape(256, 1024)
    return w


def _pack_early_grads(g):
    gt = g["w_in_t"]
    w_in = jnp.stack([jnp.concatenate([gt[:704], gt[Z_SMALL:Z_SMALL + 496]], axis=0)]
                     + [gt[Z_SMALL + 496 + 1200 * (p - 1):Z_SMALL + 496 + 1200 * p] for p in range(1, N_CHIPS)])
    parts = [
        w_in,
        g["w_uq"].reshape(384, 4, 384).transpose(1, 0, 2).reshape(4, 144, 1024),
        g["w_uk"].reshape(4, 64, 1024),
        g["w_uv"].reshape(4, 64, 1024),
        jnp.zeros((4, EARLY_ROWS - sum(r for _, r in EARLY), 1024), F32),
    ]
    return jnp.concatenate(parts, axis=1)


def _pad_w_uq(w):
    w = w.reshape(Q_RANK, N_HEADS, NOPE + ROPE)
    return jnp.pad(w, ((0, 0), (0, 0), (0, QK_PAD - NOPE - ROPE))).reshape(Q_RANK, N_HEADS * QK_PAD)


def _local_step(xs, pos, tgt, W, late_weights, late_bufs, send_late_grads, send_early_grads, cw, vec):
    D = D_MODEL
    w_in_t = W["w_in_t"]
    w_uq = _pad_w_uq(W["w_uq"])
    w_ukv = jnp.concatenate([W["w_uk"], W["w_uv"]], axis=1)
    inv = ROPE_THETA ** (-jnp.arange(0, ROPE, 2, dtype=F32) / ROPE)
    inv_freq = jnp.concatenate([inv, inv, jnp.zeros((128 - ROPE,), F32)])[None, :]

    (h,) = _rowwise(lambda x, g: (_rms(x, g),), [(xs, D, 0)], [vec["norm_mix_pre"]], [(D, BF16)],
                    name="rms_pre")
    z = _mm(h, w_in_t, tb=True, name="mm_in")

    def qkv_prep(zs, p, qn, kvn, invf):
        ang = p.astype(F32) * invf
        cosv, sinv = jnp.cos(ang), jnp.sin(ang)
        ln = lax.broadcasted_iota(jnp.int32, ang.shape, 1)
        cs = jnp.where(ln < ROPE, cosv, 0.0)
        sa = jnp.where((ln >= ROPE // 2) & (ln < ROPE), sinv, 0.0)
        sb = jnp.where(ln < ROPE // 2, -sinv, 0.0)
        cqn = _rms(zs[:, :Q_RANK], qn)
        ckvn = _rms(zs[:, Q_RANK:Q_RANK + KV_RANK], kvn)
        kr = _rope(zs[:, 640:768], cs, sa, sb)
        return cqn, ckvn, kr, cs, sa, sb

    cqn, ckvn, krot, cs, sa, sb = _rowwise(
        qkv_prep, [(z, Z_SMALL, 0), (pos, 1, 0)], [vec["q_norm"], vec["kv_norm"], inv_freq],
        [(Q_RANK, BF16), (KV_RANK, BF16), (128, BF16), (128, F32), (128, F32), (128, F32)],
        name="qkv_prep")
    q_pre = _mm(cqn, w_uq, name="mm_uq")
    kv = _mm(ckvn, w_ukv, name="mm_ukv")

    def qk_rope(qp, kn, vv, kr, cs_, sa_, sb_):
        qs, ks, vs = [], [], []
        ones = jnp.ones((qp.shape[0], NOPE), F32)
        for hd in range(N_HEADS):
            qs.append(qp[:, hd * QK_PAD:hd * QK_PAD + NOPE])
            qs.append(_rope(qp[:, hd * QK_PAD + NOPE:(hd + 1) * QK_PAD], cs_, sa_, sb_))
            ks.append(kn[:, hd * NOPE:(hd + 1) * NOPE])
            ks.append(kr.astype(F32))
            vs.append(vv[:, hd * NOPE:(hd + 1) * NOPE])
            vs.append(ones)
        return (jnp.concatenate(qs, axis=1) * Q_SCALE, jnp.concatenate(ks, axis=1),
                jnp.concatenate(vs, axis=1))

    q, k, v = _rowwise(
        qk_rope, [(q_pre, 2048, 0), (kv, 1024, 0), (kv, 1024, 1), (krot, 128, 0), (cs, 128, 0),
                  (sa, 128, 0), (sb, 128, 0)], [],
        [(2048, BF16), (2048, BF16), (2048, BF16)], name="qk_rope")
    attn, lse = _flash_fwd(q, k, v)
    LW = late_weights(attn)
    y_attn = LW.fwd(attn, "w_o_attn", name="mm_o_attn")

    (u0,) = _rowwise(lambda a, b: (a * _sigmoid(b),), [(z, D, 1), (z, D, 2)], [], [(D, F32)], name="glu")
    u1 = _conv_fwd(u0, cw, vec["conv_b"])

    def ln_parts(u, g, b):
        mu = jnp.mean(u, axis=-1, keepdims=True)
        xc = u - mu
        rs = lax.rsqrt(jnp.mean(xc * xc, axis=-1, keepdims=True) + EPS)
        nh = xc * rs
        return nh, rs, nh * g + b

    def ln_silu(u, g, b):
        y = ln_parts(u, g, b)[2]
        return (y * _sigmoid(y),)

    (u3,) = _rowwise(ln_silu, [(u1, D, 0)], [vec["conv_ln_g"], vec["conv_ln_b"]], [(D, BF16)],
                     name="ln_silu")
    y_conv = LW.fwd(u3, "w_pw2", name="mm_pw2")

    def merge(ga, gc, ya, yc, b2):
        return (_sigmoid(ga) * ya + _sigmoid(gc) * (yc + b2),)

    (merged,) = _rowwise(merge, [(z, D, 3), (z, D, 4), (y_attn, D, 0), (y_conv, D, 0)], [vec["b_pw2"]],
                         [(D, BF16)], name="merge")
    mo = LW.fwd(merged, "w_out", name="mm_out")

    def post1(x, m, g1, g2):
        x1 = x + _rms(m, g1)
        return x1, _rms(x1, g2)

    x1, h2 = _rowwise(post1, [(xs, D, 0), (mo, D, 0)], [vec["norm_mix_post"], vec["norm_mlp_pre"]],
                      [(D, F32), (D, BF16)], name="post_mix")

    def sqrelu(acc):
        r = jnp.maximum(acc, 0.0)
        return r * r, r

    a2, r1 = LW.fwd(h2, "w_ff1", name="mm_ff1", epilogue=sqrelu, out_dtypes=(BF16, BF16))
    f = LW.fwd(a2, "w_ff2", name="mm_ff2")

    def post2(x1_, f_, t_, g):
        e = x1_ + _rms(f_, g) - t_
        dy = e * (1.0 / D)
        df, dg = _rms_bwd(f_, g, dy)
        loss = jnp.broadcast_to(jnp.sum(e * e, keepdims=True) * (0.5 / D), (1, 128))
        return dy, df, dg, loss

    dy, df, g_norm_mlp_post, loss = _rowwise(
        post2, [(x1, D, 0), (f, D, 0), (tgt, D, 0)], [vec["norm_mlp_post"]],
        [(D, F32), (D, BF16)], [(1, D), (1, 128)], name="post_mlp_loss")
    late_bufs = _late_grad(a2, df, "w_ff2", late_bufs, name="mm_dw_ff2")
    df1 = LW.bwd(df, "w_ff2", name="mm_df1", extras=[r1], out_dtypes=(BF16,),
              epilogue=lambda acc, r: (acc * (2.0 * r.astype(F32)),))
    late_bufs = _late_grad(h2, df1, "w_ff1", late_bufs, name="mm_dw_ff1")
    dh2 = LW.bwd(df1, "w_ff1", name="mm_dh2")

    def bwd_mid(dy_, dh2_, x1_, m, g2, g1):
        d1, dg2 = _rms_bwd(x1_, g2, dh2_)
        dx1_ = dy_ + d1
        dm, dg1 = _rms_bwd(m, g1, dx1_)
        return dx1_, dm, dg2, dg1

    dx1, dmo, g_norm_mlp_pre, g_norm_mix_post = _rowwise(
        bwd_mid, [(dy, D, 0), (dh2, D, 0), (x1, D, 0), (mo, D, 0)],
        [vec["norm_mlp_pre"], vec["norm_mix_post"]], [(D, F32), (D, BF16)], [(1, D), (1, D)],
        name="bwd_mid")
    late_bufs = _late_grad(merged, dmo, "w_out", late_bufs, name="mm_dw_out")
    dmerged = LW.bwd(dmo, "w_out", name="mm_dmerged")

    def dmerge(dm, ga, gc, ya, yc, b2):
        sga, sgc = _sigmoid(ga), _sigmoid(gc)
        dya = dm * sga
        dyc = dm * sgc
        dga = dm * ya * sga * (1.0 - sga)
        dgc = dm * (yc + b2) * sgc * (1.0 - sgc)
        return dya, dyc, dga, dgc, jnp.sum(dyc, axis=0, keepdims=True)

    dya, dyc, dga, dgc, g_b_pw2 = _rowwise(
        dmerge, [(dmerged, D, 0), (z, D, 3), (z, D, 4), (y_attn, D, 0), (y_conv, D, 0)], [vec["b_pw2"]],
        [(D, BF16), (D, BF16), (D, BF16), (D, BF16)], [(1, D)], name="dmerge")

    late_bufs = _late_grad(attn, dya, "w_o_attn", late_bufs, name="mm_dw_o_attn")
    late_bufs = _late_grad(u3, dyc, "w_pw2", late_bufs, name="mm_dw_pw2")
    late_token = send_late_grads(late_bufs)

    dattn = LW.bwd(dya, "w_o_attn", out_dtype=BF16, name="mm_dattn")

    def delta_fn(do_, o_):
        do32 = do_.astype(F32)
        pr = do32 * o_.astype(F32)
        ln = lax.broadcasted_iota(jnp.int32, (pr.shape[0], NOPE), 1)
        cols = []
        for hd in range(N_HEADS):
            dl = jnp.sum(pr[:, hd * NOPE:(hd + 1) * NOPE], axis=-1, keepdims=True)
            hi = dl.astype(BF16).astype(F32)
            cols.append(do32[:, hd * NOPE:(hd + 1) * NOPE])
            cols.append(jnp.where(ln == 0, -hi, jnp.where(ln == 1, hi - dl, 0.0)))
        return (jnp.concatenate(cols, axis=1),)

    (do_ext,) = _rowwise(delta_fn, [(dattn, D, 0), (attn, D, 0)], [], [(2048, BF16)], name="attn_delta",
                         after=[late_token])
    dq, dk, dv = _flash_bwd(q, k, v, do_ext, lse)

    def unrope(dq_, dk_, dv_, cs_, sa_, sb_):
        qs, kn = [], []
        dkr = jnp.zeros_like(cs_)
        for hd in range(N_HEADS):
            qs.append(dq_[:, hd * QK_PAD:hd * QK_PAD + NOPE] * SCALE)
            qs.append(_rope_t(dq_[:, hd * QK_PAD + NOPE:(hd + 1) * QK_PAD] * SCALE, cs_, sa_, sb_))
            kn.append(dk_[:, hd * QK_PAD:hd * QK_PAD + NOPE] * K_UNSCALE)
            dkr = dkr + dk_[:, hd * QK_PAD + NOPE:(hd + 1) * QK_PAD]
        return (jnp.concatenate(qs, axis=1), jnp.concatenate(kn + [dv_], axis=1),
                _rope_t(dkr * K_UNSCALE, cs_, sa_, sb_))

    dq_pre, dkv, dkr = _rowwise(
        unrope, [(dq, 2048, 0), (dk, 2048, 0), (dv, 1024, 0), (cs, 128, 0), (sa, 128, 0), (sb, 128, 0)], [],
        [(2048, BF16), (2048, BF16), (128, F32)], name="unrope")
    g_w_uq = _mm(cqn, dq_pre, ta=True, name="mm_dw_uq")
    dcqn = _mm(dq_pre, w_uq, tb=True, name="mm_dcqn")
    g_w_ukv = _mm(ckvn, dkv, ta=True, name="mm_dw_ukv")
    dckvn = _mm(dkv, w_ukv, tb=True, name="mm_dckvn")

    def small_bwd(zs, dcq_, dckv_, dkr_, qn, kvn):
        dcq, dqn = _rms_bwd(zs[:, :Q_RANK], qn, dcq_)
        dckv, dkvn = _rms_bwd(zs[:, Q_RANK:Q_RANK + KV_RANK], kvn, dckv_)
        pad = jnp.zeros((zs.shape[0], Z_SMALL - 768), F32)
        return jnp.concatenate([dcq, dckv, dkr_, pad], axis=1), dqn, dkvn

    dz_small, g_q_norm, g_kv_norm = _rowwise(
        small_bwd, [(z, Z_SMALL, 0), (dcqn, Q_RANK, 0), (dckvn, KV_RANK, 0), (dkr, 128, 0)],
        [vec["q_norm"], vec["kv_norm"]], [(Z_SMALL, BF16)], [(1, Q_RANK), (1, KV_RANK)], name="small_bwd")

    du3 = LW.bwd(dyc, "w_pw2", name="mm_du3")

    def ln_silu_bwd(d3, u, g, b):
        nh, rs, y = ln_parts(u, g, b)
        sg = _sigmoid(y)
        dyv = d3 * (sg * (1.0 + y * (1.0 - sg)))
        dnh = dyv * g
        du = rs * (dnh - jnp.mean(dnh, axis=-1, keepdims=True)
                   - nh * jnp.mean(dnh * nh, axis=-1, keepdims=True))
        return (du, jnp.sum(dyv * nh, axis=0, keepdims=True), jnp.sum(dyv, axis=0, keepdims=True),
                jnp.sum(du, axis=0, keepdims=True))

    du1, g_ln_g, g_ln_b, g_conv_b = _rowwise(
        ln_silu_bwd, [(du3, D, 0), (u1, D, 0)], [vec["conv_ln_g"], vec["conv_ln_b"]], [(D, F32)],
        [(1, D), (1, D), (1, D)], name="ln_silu_bwd")
    du0, g_conv_w = _conv_bwd(u0, du1, cw)

    def glu_bwd(d0, a, b):
        sg = _sigmoid(b)
        return d0 * sg, d0 * a * sg * (1.0 - sg)

    dza, dzb = _rowwise(glu_bwd, [(du0, D, 0), (z, D, 1), (z, D, 2)], [], [(D, BF16), (D, BF16)],
                        name="glu_bwd")

    dz = jnp.concatenate([dz_small, dza, dzb, dga, dgc], axis=1)
    g_w_in_t = _mm(dz, h, ta=True, name="mm_dw_in")
    early_token = send_early_grads({
        "w_in_t": g_w_in_t,
        "w_uq": g_w_uq.reshape(Q_RANK, N_HEADS, QK_PAD)[:, :, :NOPE + ROPE].reshape(Q_RANK, 1536),
        "w_uk": g_w_ukv[:, :1024], "w_uv": g_w_ukv[:, 1024:]})
    dh = _mm(dz, w_in_t, name="mm_dh", after=[early_token])

    def final(dx1_, dh_, x, g):
        d, dg = _rms_bwd(x, g, dh_)
        return dx1_ + d, dg

    grad_x, g_norm_mix_pre = _rowwise(final, [(dx1, D, 0), (dh, D, 0), (xs, D, 0)], [vec["norm_mix_pre"]],
                                      [(D, F32)], [(1, D)], name="bwd_final")

    gs = {
        "norm_mix_pre": g_norm_mix_pre, "conv_b": g_conv_b, "conv_ln_g": g_ln_g, "conv_ln_b": g_ln_b,
        "b_pw2": g_b_pw2, "norm_mix_post": g_norm_mix_post, "norm_mlp_pre": g_norm_mlp_pre,
        "norm_mlp_post": g_norm_mlp_post, "q_norm": g_q_norm, "kv_norm": g_kv_norm,
    }
    return grad_x, gs, g_conv_w, loss


def _row1024(a):
    return jnp.pad(a, ((0, 0), (0, 1024 - a.shape[1])))


BIG = tuple(n for n, _ in EARLY) + tuple(n for n, _, _ in LATE)
ORDER = ("norm_mix_pre", "w_in", "q_norm", "w_uq", "kv_norm", "w_uk", "w_uv", "w_o_attn", "conv_w",
         "conv_b", "conv_ln_g", "conv_ln_b", "w_pw2", "b_pw2", "w_out", "norm_mix_post", "norm_mlp_pre",
         "w_ff1", "w_ff2", "norm_mlp_post")


def kernel(x, positions, norm_mix_pre, w_in, q_norm, w_uq, kv_norm, w_uk, w_uv, w_o_attn, conv_w, conv_b, conv_ln_g, conv_ln_b, w_pw2, b_pw2, w_out, norm_mix_post, norm_mlp_pre, w_ff1, w_ff2, norm_mlp_post, loss_target, m_norm_mix_pre, m_w_in, m_q_norm, m_w_uq, m_kv_norm, m_w_uk, m_w_uv, m_w_o_attn, m_conv_w, m_conv_b, m_conv_ln_g, m_conv_ln_b, m_w_pw2, m_b_pw2, m_w_out, m_norm_mix_post, m_norm_mlp_pre, m_w_ff1, m_w_ff2, m_norm_mlp_post, v_norm_mix_pre, v_w_in, v_q_norm, v_w_uq, v_kv_norm, v_w_uk, v_w_uv, v_w_o_attn, v_conv_w, v_conv_b, v_conv_ln_g, v_conv_ln_b, v_w_pw2, v_b_pw2, v_w_out, v_norm_mix_post, v_norm_mlp_pre, v_w_ff1, v_w_ff2, v_norm_mlp_post):
    args = dict(locals())
    wts = {n: args[n] for n in ORDER}
    mom = {n: args["m_" + n] for n in ORDER}
    var = {n: args["v_" + n] for n in ORDER}
    T = x.shape[1]
    place = jnp.stack([2 * lax.axis_index("x") + lax.axis_index("y"), lax.axis_index("c")]).astype(jnp.int32)

    w_early = _pack_early_shards(wts)
    w_late = _pack_late_shards(wts)
    cw_own = jnp.pad(conv_w.reshape(CONV_W, -1), ((0, CONV_TAPS_PAD - CONV_W), (0, 0)))
    full, cw_all = _gather_weights(w_early, cw_own)
    full = lax.dynamic_update_slice(full, w_early[None], (place[0], 0, 0))
    cw_all = lax.dynamic_update_slice(cw_all, cw_own[None], (place[0], 0, 0))
    W = _unpack_early(full)
    cw = cw_all.transpose(1, 0, 2).reshape(CONV_TAPS_PAD, D_MODEL)
    sems, w_thru, land, token = _late_gather_start(w_late, full)
    vec = {n: wts[n] for n in SMALL_VECS}
    vec["norm_mix_pre"] = vec["norm_mix_pre"] + token[:1, :1]

    def late_weights(after):
        landed = _late_gather_wait(sems, w_thru, land, after)
        return _LateWeights(lax.dynamic_update_slice(landed, w_late[None], (place[0], 0, 0)))

    sent = {}

    def send_late_grads(bufs):
        sent["late"] = (bufs[0],) + _reduce_start(bufs[1], place, name="late_grads_start")
        return sent["late"][4]

    def send_early_grads(g):
        g32 = _pack_early_grads(g)
        sent["early"] = (g32,) + _reduce_start(g32.astype(BF16), place, name="early_grads_start")
        return sent["early"][4]

    late_bufs = (jnp.zeros((N_CHIPS, LATE_ROWS, 1024), F32), jnp.zeros((N_CHIPS, LATE_ROWS, 1024), BF16))
    grad_x, gs, g_conv_w, loss = _local_step(x[0], positions.reshape(T, 1), loss_target[0], W, late_weights,
                                             late_bufs, send_late_grads, send_early_grads, cw, vec)

    def finish(group, after):
        g32, sems_, g_thru, land_, _ = sent[group]
        landed = _reduce_wait(sems_, g_thru, land_, after, name=group + "_grads_wait")
        red = _reduce_sum(place, g32, landed, name=group + "_grads_sum")
        return _join_halves(red, name=group + "_grads_join")

    def as2d(a):
        return a.reshape(-1, a.shape[-1]) if a.ndim != 4 or a.shape[2] == 1 else a.reshape(a.shape[1], -1)

    grads, delta, new_m, new_v = {}, {}, {}, {}

    def update(n):
        shp = wts[n].shape
        outs = _adamw(as2d(wts[n]), as2d(grads[n]), as2d(mom[n]), as2d(var[n]), name="adamw_" + n)
        delta[n], new_m[n], new_v[n] = (o.reshape(shp) for o in outs)

    g_late = finish("late", [sent["early"][4]])
    for n, off, rows in LATE:
        shp = wts[n].shape
        outs = _adamw(as2d(wts[n]), g_late, as2d(mom[n]), as2d(var[n]), g_row=off, name="adamw_" + n)
        delta[n], new_m[n], new_v[n], grads[n] = (o.reshape(shp) for o in outs)

    svec = jnp.concatenate(
        [_row1024(gs[n]) for n in SMALL_VECS] + [_row1024(loss)]
        + [jnp.zeros((SMALL_CONVW_ROW - SMALL_LOSS_ROW - 1, 1024), F32), g_conv_w], axis=0)
    me = 2 * place[0] + place[1]
    ssum = _sum_small(lax.dynamic_update_slice(_share_small(svec), svec[None], (me, 0, 0)))
    for r, n in enumerate(SMALL_VECS):
        grads[n] = ssum[r:r + 1, :wts[n].shape[1]]
    cw_sum = ssum[SMALL_CONVW_ROW:SMALL_CONVW_ROW + CONV_W]
    grads["conv_w"] = lax.dynamic_slice_in_dim(cw_sum, place[0] * 256, 256, axis=1).reshape(conv_w.shape)
    loss_out = ssum[SMALL_LOSS_ROW, 0]
    update("conv_w")
    small = [jnp.concatenate([_row1024(t[n]) for n in SMALL_VECS], axis=0) for t in (wts, grads, mom, var)]
    outs = _adamw(*small, name="adamw_small")
    for r, n in enumerate(SMALL_VECS):
        delta[n], new_m[n], new_v[n] = (o[r:r + 1, :wts[n].shape[1]] for o in outs)

    g_early = finish("early", [grad_x, outs[0], delta["w_ff2"]])
    grads["w_in"] = g_early[:1200].T[None]
    off = 1200
    for n, rows in EARLY[1:]:
        grads[n] = g_early[off:off + rows].reshape(wts[n].shape)
        off += rows
    for n, _ in EARLY:
        update(n)

    return (loss_out, grad_x[None], *[grads[n] for n in ORDER], *[delta[n] for n in ORDER],
            *[new_m[n] for n in ORDER], *[new_v[n] for n in ORDER])
```

```python
import functools

import numpy as np
import jax
import jax.numpy as jnp
from jax import lax
from jax.experimental import pallas as pl
from jax.experimental.pallas import tpu as pltpu

F32 = jnp.float32
BF16 = jnp.bfloat16
MESH = pl.DeviceIdType.MESH

D_MODEL = 1024
N_HEADS = 8
NOPE = 128
ROPE = 64
QK_PAD = 256
Q_RANK = 384
KV_RANK = 256
CONV_W = 31
CONV_TAPS_PAD = 32
HALO = 32
D_FF = 4096
EPS = 1e-6
ROPE_THETA = 10000.0
SCALE = float((NOPE + ROPE) ** -0.5)
Q_SCALE = SCALE * float(np.log2(np.e))
K_UNSCALE = float(np.log(2.0))
Z_SMALL = 1024
Z_COLS = Z_SMALL + 4 * D_MODEL
N_CHIPS = 4
N_DEV = 8

ADAM_LR = 0.001
ADAM_B1 = 0.9
ADAM_B2 = 0.999
ADAM_EPS = 1e-08
ADAM_WD = 0.01
ADAM_STEP = 10

VMEM_LIMIT = 48 * 1024 * 1024
NEG = -1e30

EARLY = (("w_in", 1200), ("w_uq", 144), ("w_uk", 64), ("w_uv", 64))
EARLY_ROWS = 1536
LATE = (("w_ff1", 0, 1024), ("w_o_attn", 1024, 256), ("w_pw2", 1536, 256), ("w_out", 1792, 256),
        ("w_ff2", 2048, 1024))
LATE_OFF = {n: off for n, off, _ in LATE}
LATE_ROWS = 3072
SUM_TILE = 256

SMALL_VECS = ("norm_mix_pre", "conv_b", "conv_ln_g", "conv_ln_b", "b_pw2", "norm_mix_post",
              "norm_mlp_pre", "norm_mlp_post", "q_norm", "kv_norm")
SMALL_LOSS_ROW = 10
SMALL_CONVW_ROW = 16
SMALL_ROWS = 48


ANY = pl.BlockSpec(memory_space=pl.ANY)


def _params(*sem):
    return pltpu.CompilerParams(dimension_semantics=sem, vmem_limit_bytes=VMEM_LIMIT)


def _tile(n, cap):
    if n <= cap:
        return n
    t = (cap // 128) * 128
    while n % t:
        t -= 128
    return t


def _row_tile(n, cap):
    if n <= cap:
        return n
    t = (cap // 8) * 8
    while n % t:
        t -= 8
    return t


MM_VMEM_BUDGET = 40 * 1024 * 1024


def _mm_tiles(M, N, K, in_bytes, out_bytes):
    tn = _tile(N, 1024)
    tk = K
    while True:
        for cap in (1024, 512, 256):
            tm = _tile(M, cap)
            need = 2 * tk * (tm * in_bytes[0] + tn * in_bytes[1]) + 2 * tm * tn * out_bytes
            need += tm * tn * 4 if tk < K else 0
            if need <= MM_VMEM_BUDGET:
                return tm, tn, tk
        tk = _tile(K, tk - 128)


def _mm(a, b, *, name, ta=False, tb=False, out_dtype=F32, extras=(), epilogue=None, out_dtypes=None,
        after=(), shape=None, tiles=None, b_spec=None, into=None):
    if shape is None:
        M, K = (a.shape[1], a.shape[0]) if ta else a.shape
        N = b.shape[0] if tb else b.shape[1]
        assert K == (b.shape[1] if tb else b.shape[0]), (a.shape, b.shape, ta, tb)
    else:
        M, N, K = shape
    out_dtypes = tuple(out_dtypes or (out_dtype,))
    if into is not None:
        out_dtypes = tuple(buf.dtype for buf, _ in into)
    if tiles is None:
        out_bytes = sum(jnp.dtype(d).itemsize for d in out_dtypes) + sum(e.dtype.itemsize for e in extras)
        tiles = _mm_tiles(M, N, K, (a.dtype.itemsize, b.dtype.itemsize), out_bytes)
    tm, tn, tk = tiles
    nk = K // tk
    dims = (((0 if ta else 1,), (1 if tb else 0,)), ((), ()))
    n_ex, n_out = len(extras), len(out_dtypes)
    n_pass = len(after) + (n_out if into is not None else 0)

    def body(*refs):
        a_ref, b_ref = refs[:2]
        ex_refs = refs[2:2 + n_ex]
        o_refs = refs[2 + n_ex + n_pass:2 + n_ex + n_pass + n_out]
        k = pl.program_id(2)
        bv = b_ref[...]
        if bv.ndim == 3:
            bv = bv.reshape(-1, bv.shape[-1])
        part = lax.dot_general(a_ref[...].astype(BF16), bv.astype(BF16), dims, preferred_element_type=F32)

        def finish(acc):
            res = epilogue(acc, *[r[...] for r in ex_refs]) if epilogue else (acc,) * n_out
            for o_ref, r in zip(o_refs, res):
                o_ref[...] = r.astype(o_ref.dtype)

        if nk == 1:
            finish(part)
        else:
            acc_ref = refs[-1]

            @pl.when(k == 0)
            def _():
                acc_ref[...] = part

            @pl.when((k > 0) & (k < nk - 1))
            def _():
                acc_ref[...] += part

            @pl.when(k == nk - 1)
            def _():
                finish(acc_ref[...] + part)

    a_spec = (pl.BlockSpec((tk, tm), lambda i, j, k: (k, i)) if ta
              else pl.BlockSpec((tm, tk), lambda i, j, k: (i, k)))
    if b_spec is None:
        b_spec = (pl.BlockSpec((tn, tk), lambda i, j, k: (j, k)) if tb
                  else pl.BlockSpec((tk, tn), lambda i, j, k: (k, j)))
    o_spec = pl.BlockSpec((tm, tn), lambda i, j, k: (i, j))
    if into is None:
        out_shape = [jax.ShapeDtypeStruct((M, N), dt) for dt in out_dtypes]
        out_specs, passed, aliases = [o_spec] * n_out, list(after), {}
    else:
        out_shape = [jax.ShapeDtypeStruct(buf.shape, buf.dtype) for buf, _ in into]
        out_specs = [spec for _, spec in into]
        passed = list(after) + [buf for buf, _ in into]
        aliases = {2 + n_ex + len(after) + o: o for o in range(n_out)}
    res = pl.pallas_call(
        body, name=name, out_shape=out_shape,
        grid=(M // tm, N // tn, nk),
        in_specs=[a_spec, b_spec] + [o_spec] * n_ex + [ANY] * len(passed),
        out_specs=out_specs, input_output_aliases=aliases,
        scratch_shapes=[pltpu.VMEM((tm, tn), F32)] if nk > 1 else [],
        compiler_params=_params("parallel", "parallel", "arbitrary"),
    )(a, b, *extras, *passed)
    return res[0] if n_out == 1 else res


class _LateWeights:
    def __init__(self, pack):
        self.pack = pack

    def fwd(self, x, n, *, name, **kw):
        T, off = x.shape[0], LATE_OFF[n]
        tm = _tile(T, 512)
        if n == "w_ff1":
            return _mm(x, self.pack, name=name, shape=(T, 4096, 1024), tiles=(_tile(T, 1024), 1024, 1024),
                       b_spec=pl.BlockSpec((None, 1024, 1024), lambda i, j, k: (j, 0, 0)), **kw)
        if n == "w_ff2":
            view = self.pack.reshape(N_CHIPS, LATE_ROWS // 1024, 1024, 1024)
            return _mm(x, view, name=name, shape=(T, 1024, 4096), tiles=(tm, 1024, 4096),
                       b_spec=pl.BlockSpec((N_CHIPS, None, 1024, 1024), lambda i, j, k: (0, off // 1024, 0, 0)),
                       **kw)
        return _mm(x, self.pack.reshape(N_CHIPS, LATE_ROWS // 256, 256, 1024), name=name,
                   shape=(T, 1024, 1024), tiles=(_tile(T, 1024), 1024, 1024), b_spec=self._whole(off), **kw)

    @staticmethod
    def _whole(off):
        return pl.BlockSpec((N_CHIPS, None, 256, 1024), lambda i, j, k: (0, off // 256, 0, 0))

    def bwd(self, dy, n, *, name, **kw):
        T, off = dy.shape[0], LATE_OFF[n]
        tm = _tile(T, 1024)
        if n == "w_ff1":
            return _mm(dy, self.pack, tb=True, name=name, shape=(T, 1024, 4096), tiles=(tm, 1024, 1024),
                       b_spec=pl.BlockSpec((None, 1024, 1024), lambda i, j, k: (k, 0, 0)), **kw)
        if n == "w_ff2":
            return _mm(dy, self.pack, tb=True, name=name, shape=(T, 4096, 1024), tiles=(tm, 1024, 1024),
                       b_spec=pl.BlockSpec((None, 1024, 1024), lambda i, j, k: (j, off // 1024, 0)), **kw)
        return _mm(dy, self.pack.reshape(N_CHIPS, LATE_ROWS // 256, 256, 1024), tb=True, name=name,
                   shape=(T, 1024, 1024), tiles=(tm, 1024, 1024), b_spec=self._whole(off), **kw)


def _late_grad(x, dy, n, bufs, *, name):
    T, off = x.shape[0], LATE_OFF[n]
    if n == "w_ff1":
        spec = pl.BlockSpec((None, 512, 1024), lambda i, j, k: (j, i, 0))
        shape, tiles = (1024, 4096, T), (512, 1024, T)
    elif n == "w_ff2":
        spec = pl.BlockSpec((None, 512, 1024), lambda i, j, k: (i // 2, off // 512 + i % 2, 0))
        shape, tiles = (4096, 1024, T), (512, 1024, T)
    else:
        spec = pl.BlockSpec((None, 256, 1024), lambda i, j, k: (i, off // 256, 0))
        shape, tiles = (1024, 1024, T), (256, 1024, T)
    return _mm(x, dy, ta=True, name=name, shape=shape, tiles=tiles, into=[(b, spec) for b in bufs])


def _rowwise(fn, rows, consts, outs, reds=(), *, name, tm=256, after=()):
    T = rows[0][0].shape[0]
    tm = min(tm, T)
    n_row, n_const, n_out = len(rows), len(consts), len(outs)

    def body(*refs):
        i = pl.program_id(0)
        vals = [r[...] for r in refs[:n_row + n_const]]
        res = fn(*vals)
        out_refs = refs[n_row + n_const + len(after):]
        for k in range(n_out):
            out_refs[k][...] = res[k].astype(out_refs[k].dtype)
        for k in range(len(reds)):
            ref = out_refs[n_out + k]

            @pl.when(i == 0)
            def _(ref=ref):
                ref[...] = jnp.zeros_like(ref)

            ref[...] += res[n_out + k]

    in_specs = [pl.BlockSpec((tm, w), functools.partial(lambda i, cb: (i, cb), cb=cb))
                for (_, w, cb) in rows]
    in_specs += [pl.BlockSpec(c.shape, lambda i: (0, 0)) for c in consts]
    in_specs += [ANY] * len(after)
    out_specs = [pl.BlockSpec((tm, w), lambda i: (i, 0)) for (w, _) in outs]
    out_specs += [pl.BlockSpec(s, lambda i: (0, 0)) for s in reds]
    out_shape = [jax.ShapeDtypeStruct((T, w), dt) for (w, dt) in outs]
    out_shape += [jax.ShapeDtypeStruct(s, F32) for s in reds]
    return pl.pallas_call(
        body, name=name, out_shape=out_shape, grid=(T // tm,),
        in_specs=in_specs, out_specs=out_specs,
        compiler_params=_params("arbitrary"),
    )(*[r[0] for r in rows], *consts, *after)


def _rms(x, g):
    r = lax.rsqrt(jnp.mean(x * x, axis=-1, keepdims=True) + EPS)
    return x * r * g


def _rms_bwd(x, g, dy):
    r = lax.rsqrt(jnp.mean(x * x, axis=-1, keepdims=True) + EPS)
    n = x * r
    dyg = dy * g
    dx = r * (dyg - n * jnp.mean(dyg * n, axis=-1, keepdims=True))
    return dx, jnp.sum(dy * n, axis=0, keepdims=True)


def _sigmoid(x):
    return 1.0 / (1.0 + jnp.exp(-x))


def _rope(x, cs, sa, sb):
    return x * cs + pltpu.roll(x, 32, 1) * sa + pltpu.roll(x, 96, 1) * sb


def _rope_t(dy, cs, sa, sb):
    return dy * cs + pltpu.roll(dy * sa, 96, 1) + pltpu.roll(dy * sb, 32, 1)


def _causal_mask(t):
    row = lax.broadcasted_iota(jnp.int32, (t, t), 0)
    col = lax.broadcasted_iota(jnp.int32, (t, t), 1)
    return col <= row


FLASH_FWD_TQ = 1024
FLASH_FWD_TK = 1024
FLASH_BWD_TQ = 1024
FLASH_BWD_TK = 512
NT_DIMS = (((1,), (1,)), ((), ()))
TN_DIMS = (((0,), (0,)), ((), ()))


def _flash_fwd(q, k, v, *, tq=FLASH_FWD_TQ, tk=FLASH_FWD_TK):
    T = q.shape[0]
    tq, tk = min(tq, T), min(tk, T)
    n_diag = tq // tk
    assert n_diag >= 1

    def body(q_ref, k_ref, v_ref, o_ref, lse_ref):
        i = pl.program_id(1)
        qv = q_ref[...]

        def logits(j):
            sl = pl.ds(pl.multiple_of(j * tk, tk), tk)
            return lax.dot_general(qv, k_ref[sl, :], NT_DIMS, preferred_element_type=F32)

        def update(j, s, m, acc, masked):
            if masked:
                row = i * tq + lax.broadcasted_iota(jnp.int32, (tq, tk), 0)
                col = j * tk + lax.broadcasted_iota(jnp.int32, (tq, tk), 1)
                s = jnp.where(col <= row, s, NEG)
            m_new = jnp.maximum(m, jnp.max(s, axis=-1, keepdims=True))
            p = jnp.exp2(s - m_new)
            sl = pl.ds(pl.multiple_of(j * tk, tk), tk)
            acc = jnp.exp2(m - m_new) * acc + jnp.dot(p.astype(BF16), v_ref[sl, :],
                                                      preferred_element_type=F32)
            return m_new, acc

        def step(j, carry):
            s, m, acc = carry
            s_next = logits(j + 1)
            m, acc = update(j, s, m, acc, False)
            return s_next, m, acc

        first = i * n_diag
        init = (logits(0), jnp.full((tq, 1), NEG, F32), jnp.zeros((tq, QK_PAD), F32))
        s, m, acc = lax.fori_loop(0, first, step, init)
        for d in range(n_diag):
            m, acc = update(first + d, s, m, acc, True)
            if d + 1 < n_diag:
                s = logits(first + d + 1)
        l = acc[:, NOPE:]
        o_ref[...] = (acc[:, :NOPE] / l).astype(BF16)
        lse_ref[...] = m + jnp.log2(l)

    return pl.pallas_call(
        body, name="flash_fwd",
        out_shape=[jax.ShapeDtypeStruct((T, N_HEADS * NOPE), BF16),
                   jax.ShapeDtypeStruct((T, N_HEADS * NOPE), F32)],
        grid=(N_HEADS, T // tq),
        in_specs=[pl.BlockSpec((tq, QK_PAD), lambda h, i: (i, h)),
                  pl.BlockSpec((T, QK_PAD), lambda h, i: (0, h)),
                  pl.BlockSpec((T, QK_PAD), lambda h, i: (0, h))],
        out_specs=[pl.BlockSpec((tq, NOPE), lambda h, i: (i, h)),
                   pl.BlockSpec((tq, NOPE), lambda h, i: (i, h))],
        compiler_params=_params("parallel", "arbitrary"),
    )(q, k, v)


def _flash_bwd(q, k, v, do, lse, *, tq=FLASH_BWD_TQ, tk=FLASH_BWD_TK):
    T = q.shape[0]
    tq, tk = min(tq, T), min(tk, T)
    nq = T // tq
    n_diag = max(1, tk // tq)

    def body(q_ref, k_ref, v_ref, do_ref, lse_ref, dq_ref, dk_ref, dv_ref):
        j = pl.program_id(1)

        @pl.when(j == 0)
        def _():
            dq_ref[...] = jnp.zeros_like(dq_ref)

        kj = k_ref[...]
        vj = v_ref[...]

        def step(i, carry, masked):
            dk, dv = carry
            sl = pl.ds(pl.multiple_of(i * tq, tq), tq)
            qi = q_ref[sl, :]
            doi = do_ref[sl, :]
            s = lax.dot_general(qi, kj, NT_DIMS, preferred_element_type=F32)
            p = jnp.exp2(s - lse_ref[sl, :][:, :1])
            if masked:
                row = i * tq + lax.broadcasted_iota(jnp.int32, (tq, tk), 0)
                col = j * tk + lax.broadcasted_iota(jnp.int32, (tq, tk), 1)
                p = jnp.where(col <= row, p, 0.0)
            dv = dv + lax.dot_general(p.astype(BF16), doi, TN_DIMS, preferred_element_type=F32)
            ds = (p * lax.dot_general(doi, vj, NT_DIMS, preferred_element_type=F32)).astype(BF16)
            dk = dk + lax.dot_general(ds, qi, TN_DIMS, preferred_element_type=F32)
            dq_ref[sl, :] += jnp.dot(ds, kj, preferred_element_type=F32)
            return dk, dv

        first = (j * tk) // tq
        carry = (jnp.zeros((tk, QK_PAD), F32), jnp.zeros((tk, QK_PAD), F32))
        for d in range(n_diag):
            carry = step(first + d, carry, True)
        dk, dv = lax.fori_loop(first + n_diag, nq, lambda i, c: step(i, c, False), carry)
        dk_ref[...] = dk
        dv_ref[...] = dv[:, :NOPE]

    return pl.pallas_call(
        body, name="flash_bwd",
        out_shape=[jax.ShapeDtypeStruct((T, N_HEADS * QK_PAD), F32),
                   jax.ShapeDtypeStruct((T, N_HEADS * QK_PAD), F32),
                   jax.ShapeDtypeStruct((T, N_HEADS * NOPE), F32)],
        grid=(N_HEADS, T // tk),
        in_specs=[pl.BlockSpec((T, QK_PAD), lambda h, j: (0, h)),
                  pl.BlockSpec((tk, QK_PAD), lambda h, j: (j, h)),
                  pl.BlockSpec((tk, QK_PAD), lambda h, j: (j, h)),
                  pl.BlockSpec((T, QK_PAD), lambda h, j: (0, h)),
                  pl.BlockSpec((T, NOPE), lambda h, j: (0, h))],
        out_specs=[pl.BlockSpec((T, QK_PAD), lambda h, j: (0, h)),
                   pl.BlockSpec((tk, QK_PAD), lambda h, j: (j, h)),
                   pl.BlockSpec((tk, NOPE), lambda h, j: (j, h))],
        compiler_params=_params("arbitrary", "arbitrary"),
    )(q, k, v, do, lse)


SUB = 128


def _windows(buf, t0, c0, first, last):
    span = SUB + 8 * ((last + 7) // 8)
    base = buf[t0:t0 + span, c0:c0 + SUB]
    for r in range(8):
        offs = [o for o in range(first, last + 1) if o % 8 == r]
        if offs:
            win = base if r == 0 else pltpu.roll(base, span - r, 0)
            for o in offs:
                yield o, win[o - r:o - r + SUB, :]


def _conv_fwd(u, w, bias, *, tm=256, ct=256):
    T, C = u.shape
    tm = min(tm, T)
    hb = tm // HALO
    lead = HALO - (CONV_W - 1)

    def body(cur_ref, halo_ref, w_ref, b_ref, o_ref, buf):
        t = pl.program_id(1)
        buf[0:HALO, :] = jnp.where(t > 0, halo_ref[...], 0.0)
        buf[HALO:, :] = cur_ref[...]
        for c0 in range(0, ct, SUB):
            for t0 in range(0, tm, SUB):
                acc = jnp.broadcast_to(b_ref[:, c0:c0 + SUB], (SUB, SUB))
                for o, win in _windows(buf, t0, c0, lead, lead + CONV_W - 1):
                    acc = acc + win * w_ref[o - lead:o - lead + 1, c0:c0 + SUB]
                o_ref[t0:t0 + SUB, c0:c0 + SUB] = acc

    return pl.pallas_call(
        body, name="conv_fwd",
        out_shape=jax.ShapeDtypeStruct((T, C), F32),
        grid=(C // ct, T // tm),
        in_specs=[pl.BlockSpec((tm, ct), lambda c, t: (t, c)),
                  pl.BlockSpec((HALO, ct), lambda c, t: (jnp.maximum(t * hb - 1, 0), c)),
                  pl.BlockSpec((CONV_TAPS_PAD, ct), lambda c, t: (0, c)),
                  pl.BlockSpec((1, ct), lambda c, t: (0, c))],
        out_specs=pl.BlockSpec((tm, ct), lambda c, t: (t, c)),
        scratch_shapes=[pltpu.VMEM((HALO + tm, ct), F32)],
        compiler_params=_params("parallel", "arbitrary"),
    )(u, u, w, bias)


def _conv_bwd(u, du1, w, *, tm=256, ct=256):
    T, C = u.shape
    tm = min(tm, T)
    hb = tm // HALO
    nt = T // tm
    last_halo = T // HALO - 1
    lead = HALO - (CONV_W - 1)

    def body(u_ref, uh_ref, d_ref, dh_ref, w_ref, du_ref, dw_ref, ubuf, dbuf, dw8):
        t = pl.program_id(1)
        ubuf[0:HALO, :] = jnp.where(t > 0, uh_ref[...], 0.0)
        ubuf[HALO:, :] = u_ref[...]
        dbuf[0:tm, :] = d_ref[...]
        dbuf[tm:, :] = jnp.where(t < nt - 1, dh_ref[...], 0.0)

        @pl.when(t == 0)
        def _():
            dw8[...] = jnp.zeros_like(dw8)

        for c0 in range(0, ct, SUB):
            for t0 in range(0, tm, SUB):
                acc = jnp.zeros((SUB, SUB), F32)
                for o, win in _windows(dbuf, t0, c0, 0, CONV_W - 1):
                    kk = CONV_W - 1 - o
                    acc = acc + win * w_ref[kk:kk + 1, c0:c0 + SUB]
                du_ref[t0:t0 + SUB, c0:c0 + SUB] = acc
                dsub = d_ref[t0:t0 + SUB, c0:c0 + SUB]
                for o, win in _windows(ubuf, t0, c0, lead, lead + CONV_W - 1):
                    kk = o - lead
                    dw8[8 * kk:8 * kk + 8, c0:c0 + SUB] += jnp.sum(
                        (dsub * win).reshape(SUB // 8, 8, SUB), axis=0)

        @pl.when(t == nt - 1)
        def _():
            dw_ref[...] = jnp.sum(dw8[...].reshape(CONV_TAPS_PAD, 8, ct), axis=1)

    return pl.pallas_call(
        body, name="conv_bwd",
        out_shape=[jax.ShapeDtypeStruct((T, C), F32), jax.ShapeDtypeStruct((CONV_TAPS_PAD, C), F32)],
        grid=(C // ct, nt),
        in_specs=[pl.BlockSpec((tm, ct), lambda c, t: (t, c)),
                  pl.BlockSpec((HALO, ct), lambda c, t: (jnp.maximum(t * hb - 1, 0), c)),
                  pl.BlockSpec((tm, ct), lambda c, t: (t, c)),
                  pl.BlockSpec((HALO, ct), lambda c, t: (jnp.minimum((t + 1) * hb, last_halo), c)),
                  pl.BlockSpec((CONV_TAPS_PAD, ct), lambda c, t: (0, c))],
        out_specs=[pl.BlockSpec((tm, ct), lambda c, t: (t, c)),
                   pl.BlockSpec((CONV_TAPS_PAD, ct), lambda c, t: (0, c))],
        scratch_shapes=[pltpu.VMEM((HALO + tm, ct), F32), pltpu.VMEM((tm + HALO, ct), F32),
                        pltpu.VMEM((CONV_TAPS_PAD * 8, ct), F32)],
        compiler_params=_params("parallel", "arbitrary"),
    )(u, u, du1, du1, w)


def _adamw(w, g, m, v, *, name, g_row=None):
    R, C = w.shape
    tr = _row_tile(R, max(8, (1 << 18) // C // 8 * 8))
    first = 0 if g_row is None else g_row // tr
    assert g_row is None or g_row % tr == 0

    def body(w_ref, g_ref, m_ref, v_ref, d_ref, nm_ref, nv_ref, *g_out):
        gv = g_ref[...]
        nm = ADAM_B1 * m_ref[...] + (1.0 - ADAM_B1) * gv
        nv = ADAM_B2 * v_ref[...] + (1.0 - ADAM_B2) * (gv * gv)
        m_hat = nm / (1.0 - ADAM_B1 ** ADAM_STEP)
        v_hat = nv / (1.0 - ADAM_B2 ** ADAM_STEP)
        d_ref[...] = -ADAM_LR * (m_hat / (jnp.sqrt(v_hat) + ADAM_EPS) + ADAM_WD * w_ref[...])
        nm_ref[...] = nm
        nv_ref[...] = nv
        for ref in g_out:
            ref[...] = gv

    spec = pl.BlockSpec((tr, C), lambda i: (i, 0))
    g_spec = pl.BlockSpec((tr, C), lambda i: (first + i, 0))
    n_out = 3 if g_row is None else 4
    return pl.pallas_call(
        body, name=name, out_shape=[jax.ShapeDtypeStruct((R, C), F32)] * n_out,
        grid=(R // tr,), in_specs=[spec, g_spec, spec, spec], out_specs=[spec] * n_out,
        compiler_params=_params("parallel"),
    )(w, g, m, v)


def _coords():
    return lax.axis_index("x"), lax.axis_index("y"), lax.axis_index("c")


def _remote(src, dst, ssem, rsem, to):
    return pltpu.make_async_remote_copy(src_ref=src, dst_ref=dst, send_sem=ssem, recv_sem=rsem,
                                        device_id=to, device_id_type=MESH)


def _half(c, rows):
    return pl.ds(pl.multiple_of(c * (rows // 2), 16), rows // 2)


def _gather_weights(wpack, cw):
    def body(w_ref, cw_ref, full_ref, cwf_ref, ssem, rsem):
        x, y, c = _coords()
        q = 2 * x + y
        sib = (x, y, 1 - c)
        chips = [(1 - x, y), (x, 1 - y), (1 - x, 1 - y)]
        mine, other = _half(c, w_ref.shape[0]), _half(1 - c, w_ref.shape[0])
        first = []
        for j, (px, py) in enumerate(chips):
            first.append(_remote(w_ref.at[mine], full_ref.at[q, mine], ssem.at[j], rsem.at[j], (px, py, c)))
            first.append(_remote(cw_ref, cwf_ref.at[q], ssem.at[6 + j], rsem.at[6 + j], (px, py, c)))
        for cp in first:
            cp.start()
        passed = []
        for j, (px, py) in enumerate(chips):
            pq = 2 * px + py
            _remote(w_ref.at[mine], full_ref.at[pq, mine], ssem.at[j], rsem.at[j], (px, py, c)).wait_recv()
            fw = _remote(full_ref.at[pq, mine], full_ref.at[pq, mine], ssem.at[3 + j], rsem.at[3 + j], sib)
            fw.start()
            passed.append(fw)
        for j, (px, py) in enumerate(chips):
            pq = 2 * px + py
            _remote(full_ref.at[pq, other], full_ref.at[pq, other], ssem.at[3 + j], rsem.at[3 + j], sib).wait_recv()
            _remote(cw_ref, cwf_ref.at[pq], ssem.at[6 + j], rsem.at[6 + j], (px, py, c)).wait_recv()
        for cp in first + passed:
            cp.wait_send()

    return pl.pallas_call(
        body, name="gather_weights",
        out_shape=[jax.ShapeDtypeStruct((N_CHIPS,) + wpack.shape, wpack.dtype),
                   jax.ShapeDtypeStruct((N_CHIPS,) + cw.shape, cw.dtype)],
        in_specs=[ANY, ANY], out_specs=[ANY, ANY],
        scratch_shapes=[pltpu.SemaphoreType.DMA((9,)), pltpu.SemaphoreType.DMA((9,))],
    )(wpack, cw)


SEM = pl.BlockSpec(memory_space=pltpu.SEMAPHORE)
HBM = pl.BlockSpec(memory_space=pltpu.HBM)
N_LATE = 6


def _late_peers(x, y, c):
    out = []
    for j, (px, py) in enumerate([(1 - x, y), (x, 1 - y), (1 - x, 1 - y)]):
        for t in (0, 1):
            out.append((2 * j + t, px, py, c if t == 0 else 1 - c))
    return out


def _late_gather_start(w, after):
    rows = w.shape[0]

    def body(w_ref, land_ref, after_ref, *outs):
        sems = outs[:2 * N_LATE]
        token = outs[2 * N_LATE + 2]
        x, y, c = _coords()
        q = 2 * x + y
        mine = _half(c, rows)
        for k, px, py, pc in _late_peers(x, y, c):
            _remote(w_ref.at[mine], land_ref.at[q, mine], sems[k], sems[N_LATE + k], (px, py, pc)).start()
        token[...] = jnp.zeros_like(token)

    res = pl.pallas_call(
        body, name="late_gather_start",
        out_shape=tuple([pltpu.SemaphoreType.DMA(())] * (2 * N_LATE)) + (
            pltpu.HBM(w.shape, w.dtype), pltpu.HBM((N_CHIPS,) + w.shape, w.dtype),
            jax.ShapeDtypeStruct((8, 128), F32)),
        in_specs=(HBM, HBM, ANY),
        out_specs=tuple([SEM] * (2 * N_LATE)) + (HBM, HBM, pl.BlockSpec(memory_space=pltpu.VMEM)),
        input_output_aliases={0: 2 * N_LATE, 1: 2 * N_LATE + 1},
        compiler_params=pltpu.CompilerParams(has_side_effects=pltpu.SideEffectType.DATAFLOW_SIDE_EFFECTING),
    )(pltpu.with_memory_space_constraint(w, pltpu.HBM),
      pltpu.with_memory_space_constraint(lax.empty((N_CHIPS,) + w.shape, w.dtype), pltpu.HBM), after)
    return res[:2 * N_LATE], res[2 * N_LATE], res[2 * N_LATE + 1], res[2 * N_LATE + 2]


def _late_gather_wait(sems, w_thru, land_thru, after):
    rows = w_thru.shape[0]

    def body(w_ref, land_ref, *rest):
        sems = rest[:2 * N_LATE]
        x, y, c = _coords()
        for k, px, py, pc in _late_peers(x, y, c):
            cp = _remote(w_ref.at[_half(pc, rows)], land_ref.at[2 * px + py, _half(pc, rows)],
                         sems[k], sems[N_LATE + k], (px, py, pc))
            cp.wait_send()
            cp.wait_recv()

    return pl.pallas_call(
        body, name="late_gather_wait",
        out_shape=(pltpu.HBM(w_thru.shape, w_thru.dtype), pltpu.HBM(land_thru.shape, land_thru.dtype)),
        in_specs=(HBM, HBM) + tuple([SEM] * (2 * N_LATE)) + (ANY,),
        out_specs=(HBM, HBM), input_output_aliases={0: 0, 1: 1},
        compiler_params=pltpu.CompilerParams(has_side_effects=pltpu.SideEffectType.DATAFLOW_SIDE_EFFECTING),
    )(w_thru, land_thru, *sems, after)[1]


N_PEERS = N_DEV - 1


def _peers(x, y, c):
    out = []
    for r in range(1, N_DEV):
        out.append((r - 1, 1 - x if r & 4 else x, 1 - y if r & 2 else y, 1 - c if r & 1 else c))
    return out


def _piece(g_ref, px, py, pc, whole):
    return g_ref if whole else g_ref.at[2 * px + py, _half(pc, g_ref.shape[1])]


def _reduce_start(g16, after, *, name, whole=False):
    def body(g_ref, land_ref, after_ref, *outs):
        sems = outs[:2 * N_PEERS]
        token = outs[2 * N_PEERS + 2]
        x, y, c = _coords()
        for k, px, py, pc in _peers(x, y, c):
            _remote(_piece(g_ref, px, py, pc, whole), land_ref.at[k], sems[k], sems[N_PEERS + k],
                    (px, py, pc)).start()
        token[...] = jnp.zeros_like(token)

    land_shape = (N_PEERS,) + (g16.shape if whole else (g16.shape[1] // 2, 1024))
    res = pl.pallas_call(
        body, name=name,
        out_shape=tuple([pltpu.SemaphoreType.DMA(())] * (2 * N_PEERS)) + (
            pltpu.HBM(g16.shape, g16.dtype), pltpu.HBM(land_shape, g16.dtype),
            jax.ShapeDtypeStruct((8, 128), F32)),
        in_specs=(HBM, HBM, ANY),
        out_specs=tuple([SEM] * (2 * N_PEERS)) + (HBM, HBM, pl.BlockSpec(memory_space=pltpu.VMEM)),
        input_output_aliases={0: 2 * N_PEERS, 1: 2 * N_PEERS + 1},
        compiler_params=pltpu.CompilerParams(has_side_effects=pltpu.SideEffectType.DATAFLOW_SIDE_EFFECTING),
    )(pltpu.with_memory_space_constraint(g16, pltpu.HBM),
      pltpu.with_memory_space_constraint(lax.empty(land_shape, g16.dtype), pltpu.HBM), after)
    return res[:2 * N_PEERS], res[2 * N_PEERS], res[2 * N_PEERS + 1], res[2 * N_PEERS + 2]


def _reduce_wait(sems, g_thru, land_thru, after, *, name, whole=False):
    def body(g_ref, land_ref, *rest):
        sems = rest[:2 * N_PEERS]
        x, y, c = _coords()
        for k, px, py, pc in _peers(x, y, c):
            cp = _remote(_piece(g_ref, px, py, pc, whole), land_ref.at[k], sems[k], sems[N_PEERS + k],
                         (px, py, pc))
            cp.wait_send()
            cp.wait_recv()

    return pl.pallas_call(
        body, name=name,
        out_shape=(pltpu.HBM(g_thru.shape, g_thru.dtype), pltpu.HBM(land_thru.shape, land_thru.dtype)),
        in_specs=(HBM, HBM) + tuple([SEM] * (2 * N_PEERS)) + tuple([ANY] * len(after)),
        out_specs=(HBM, HBM), input_output_aliases={0: 0, 1: 1},
        compiler_params=pltpu.CompilerParams(has_side_effects=pltpu.SideEffectType.DATAFLOW_SIDE_EFFECTING),
    )(g_thru, land_thru, *sems, *after)[1]


def _reduce_sum(place, g32, land, *, name):
    rows = g32.shape[1]
    nb = rows // 2 // SUM_TILE

    def body(place_ref, g_ref, l_ref, o_ref):
        acc = g_ref[...]
        for k in range(N_PEERS):
            acc = acc + l_ref[k].astype(F32)
        o_ref[...] = acc

    return pl.pallas_call(
        body, name=name,
        out_shape=jax.ShapeDtypeStruct((rows, 1024), F32),
        grid_spec=pltpu.PrefetchScalarGridSpec(
            num_scalar_prefetch=1, grid=(nb,),
            in_specs=[pl.BlockSpec((None, SUM_TILE, 1024), lambda i, s: (s[0], s[1] * nb + i, 0)),
                      pl.BlockSpec((N_PEERS, SUM_TILE, 1024), lambda i, s: (0, i, 0))],
            out_specs=pl.BlockSpec((SUM_TILE, 1024), lambda i, s: (s[1] * nb + i, 0))),
        compiler_params=_params("parallel"),
    )(place, g32, land)


def _join_halves(red, *, name):
    rows = red.shape[0]

    def body(r_ref, o_ref, ssem, rsem):
        x, y, c = _coords()
        cp = _remote(r_ref.at[_half(c, rows)], o_ref.at[_half(c, rows)], ssem, rsem, (x, y, 1 - c))
        cp.start()
        _remote(r_ref.at[_half(1 - c, rows)], o_ref.at[_half(1 - c, rows)], ssem, rsem,
                (x, y, 1 - c)).wait_recv()
        cp.wait_send()

    return pl.pallas_call(
        body, name=name,
        out_shape=jax.ShapeDtypeStruct(red.shape, F32),
        in_specs=[ANY], out_specs=ANY, input_output_aliases={0: 0},
        scratch_shapes=[pltpu.SemaphoreType.DMA, pltpu.SemaphoreType.DMA],
    )(red)


def _sum_small(me, svec, land):
    def body(me_ref, s_ref, l_ref, o_ref):
        mine = me_ref[0]
        acc = None
        for d in range(N_DEV):
            theirs = l_ref[jnp.maximum(jnp.bitwise_xor(mine, d) - 1, 0)]
            v = jnp.where(mine == d, s_ref[...], theirs)
            acc = v if acc is None else acc + v
        o_ref[...] = acc

    return pl.pallas_call(
        body, name="sum_small",
        out_shape=jax.ShapeDtypeStruct(svec.shape, F32),
        grid_spec=pltpu.PrefetchScalarGridSpec(
            num_scalar_prefetch=1, grid=(1,),
            in_specs=[pl.BlockSpec(svec.shape, lambda i, s: (0, 0)),
                      pl.BlockSpec(land.shape, lambda i, s: (0, 0, 0))],
            out_specs=pl.BlockSpec(svec.shape, lambda i, s: (0, 0))),
        compiler_params=_params("arbitrary"),
    )(me, svec, land)


def _pack_early_shards(wts):
    parts = [wts["w_in"][0].T.astype(BF16)] + [wts[n].astype(BF16).reshape(-1, 1024) for n, _ in EARLY[1:]]
    used = sum(r for _, r in EARLY)
    return jnp.concatenate(parts + [jnp.zeros((EARLY_ROWS - used, 1024), BF16)], axis=0)


def _pack_late_shards(wts):
    parts, at = [], 0
    for n, off, rows in LATE:
        if off > at:
            parts.append(jnp.zeros((off - at, 1024), BF16))
        parts.append(wts[n].astype(BF16).reshape(rows, 1024))
        at = off + rows
    return jnp.concatenate(parts, axis=0)


def _unpack_early(full):
    w = {}
    w["w_in_t"] = jnp.concatenate(
        [full[0, :704], jnp.zeros((Z_SMALL - 704, 1024), full.dtype), full[0, 704:1200]]
        + [full[p, :1200] for p in range(1, N_CHIPS)], axis=0)
    w["w_uq"] = full[:, 1200:1344].reshape(4, 384, 384).transpose(1, 0, 2).reshape(384, 1536)
    w["w_uk"] = full[:, 1344:1408].reshape(256, 1024)
    w["w_uv"] = full[:, 1408:1472].reshape(256, 1024)
    return w


def _pack_early_grads(g):
    gt = g["w_in_t"]
    w_in = jnp.stack([jnp.concatenate([gt[:704], gt[Z_SMALL:Z_SMALL + 496]], axis=0)]
                     + [gt[Z_SMALL + 496 + 1200 * (p - 1):Z_SMALL + 496 + 1200 * p] for p in range(1, N_CHIPS)])
    parts = [
        w_in,
        g["w_uq"].reshape(384, 4, 384).transpose(1, 0, 2).reshape(4, 144, 1024),
        g["w_uk"].reshape(4, 64, 1024),
        g["w_uv"].reshape(4, 64, 1024),
        jnp.zeros((4, EARLY_ROWS - sum(r for _, r in EARLY), 1024), F32),
    ]
    return jnp.concatenate(parts, axis=1)


def _pad_w_uq(w):
    w = w.reshape(Q_RANK, N_HEADS, NOPE + ROPE)
    return jnp.pad(w, ((0, 0), (0, 0), (0, QK_PAD - NOPE - ROPE))).reshape(Q_RANK, N_HEADS * QK_PAD)


def _local_step(xs, pos, tgt, W, late_weights, late_bufs, send_late_grads, send_early_grads, cw, vec):
    D = D_MODEL
    w_in_t = W["w_in_t"]
    w_uq = _pad_w_uq(W["w_uq"])
    w_ukv = jnp.concatenate([W["w_uk"], W["w_uv"]], axis=1)
    inv = ROPE_THETA ** (-jnp.arange(0, ROPE, 2, dtype=F32) / ROPE)
    inv_freq = jnp.concatenate([inv, inv, jnp.zeros((128 - ROPE,), F32)])[None, :]

    (h,) = _rowwise(lambda x, g: (_rms(x, g),), [(xs, D, 0)], [vec["norm_mix_pre"]], [(D, BF16)],
                    name="rms_pre")
    z = _mm(h, w_in_t, tb=True, name="mm_in")

    def qkv_prep(zs, p, qn, kvn, invf):
        ang = p.astype(F32) * invf
        cosv, sinv = jnp.cos(ang), jnp.sin(ang)
        ln = lax.broadcasted_iota(jnp.int32, ang.shape, 1)
        cs = jnp.where(ln < ROPE, cosv, 0.0)
        sa = jnp.where((ln >= ROPE // 2) & (ln < ROPE), sinv, 0.0)
        sb = jnp.where(ln < ROPE // 2, -sinv, 0.0)
        cqn = _rms(zs[:, :Q_RANK], qn)
        ckvn = _rms(zs[:, Q_RANK:Q_RANK + KV_RANK], kvn)
        kr = _rope(zs[:, 640:768], cs, sa, sb)
        return cqn, ckvn, kr, cs, sa, sb

    cqn, ckvn, krot, cs, sa, sb = _rowwise(
        qkv_prep, [(z, Z_SMALL, 0), (pos, 1, 0)], [vec["q_norm"], vec["kv_norm"], inv_freq],
        [(Q_RANK, BF16), (KV_RANK, BF16), (128, BF16), (128, F32), (128, F32), (128, F32)],
        name="qkv_prep")
    q_pre = _mm(cqn, w_uq, name="mm_uq")
    kv = _mm(ckvn, w_ukv, name="mm_ukv")

    def qk_rope(qp, kn, vv, kr, cs_, sa_, sb_):
        qs, ks, vs = [], [], []
        ones = jnp.ones((qp.shape[0], NOPE), F32)
        for hd in range(N_HEADS):
            qs.append(qp[:, hd * QK_PAD:hd * QK_PAD + NOPE])
            qs.append(_rope(qp[:, hd * QK_PAD + NOPE:(hd + 1) * QK_PAD], cs_, sa_, sb_))
            ks.append(kn[:, hd * NOPE:(hd + 1) * NOPE])
            ks.append(kr.astype(F32))
            vs.append(vv[:, hd * NOPE:(hd + 1) * NOPE])
            vs.append(ones)
        return (jnp.concatenate(qs, axis=1) * Q_SCALE, jnp.concatenate(ks, axis=1),
                jnp.concatenate(vs, axis=1))

    q, k, v = _rowwise(
        qk_rope, [(q_pre, 2048, 0), (kv, 1024, 0), (kv, 1024, 1), (krot, 128, 0), (cs, 128, 0),
                  (sa, 128, 0), (sb, 128, 0)], [],
        [(2048, BF16), (2048, BF16), (2048, BF16)], name="qk_rope")
    attn, lse = _flash_fwd(q, k, v)
    LW = late_weights(attn)
    y_attn = LW.fwd(attn, "w_o_attn", name="mm_o_attn")

    (u0,) = _rowwise(lambda a, b: (a * _sigmoid(b),), [(z, D, 1), (z, D, 2)], [], [(D, F32)], name="glu")
    u1 = _conv_fwd(u0, cw, vec["conv_b"])

    def ln_parts(u, g, b):
        mu = jnp.mean(u, axis=-1, keepdims=True)
        xc = u - mu
        rs = lax.rsqrt(jnp.mean(xc * xc, axis=-1, keepdims=True) + EPS)
        nh = xc * rs
        return nh, rs, nh * g + b

    def ln_silu(u, g, b):
        y = ln_parts(u, g, b)[2]
        return (y * _sigmoid(y),)

    (u3,) = _rowwise(ln_silu, [(u1, D, 0)], [vec["conv_ln_g"], vec["conv_ln_b"]], [(D, BF16)],
                     name="ln_silu")
    y_conv = LW.fwd(u3, "w_pw2", name="mm_pw2")

    def merge(ga, gc, ya, yc, b2):
        return (_sigmoid(ga) * ya + _sigmoid(gc) * (yc + b2),)

    (merged,) = _rowwise(merge, [(z, D, 3), (z, D, 4), (y_attn, D, 0), (y_conv, D, 0)], [vec["b_pw2"]],
                         [(D, BF16)], name="merge")
    mo = LW.fwd(merged, "w_out", name="mm_out")

    def post1(x, m, g1, g2):
        x1 = x + _rms(m, g1)
        return x1, _rms(x1, g2)

    x1, h2 = _rowwise(post1, [(xs, D, 0), (mo, D, 0)], [vec["norm_mix_post"], vec["norm_mlp_pre"]],
                      [(D, F32), (D, BF16)], name="post_mix")

    def sqrelu(acc):
        r = jnp.maximum(acc, 0.0)
        return r * r, r

    a2, r1 = LW.fwd(h2, "w_ff1", name="mm_ff1", epilogue=sqrelu, out_dtypes=(BF16, BF16))
    f = LW.fwd(a2, "w_ff2", name="mm_ff2")

    def post2(x1_, f_, t_, g):
        e = x1_ + _rms(f_, g) - t_
        dy = e * (1.0 / D)
        df, dg = _rms_bwd(f_, g, dy)
        loss = jnp.broadcast_to(jnp.sum(e * e, keepdims=True) * (0.5 / D), (1, 128))
        return dy, df, dg, loss

    dy, df, g_norm_mlp_post, loss = _rowwise(
        post2, [(x1, D, 0), (f, D, 0), (tgt, D, 0)], [vec["norm_mlp_post"]],
        [(D, F32), (D, BF16)], [(1, D), (1, 128)], name="post_mlp_loss")
    late_bufs = _late_grad(a2, df, "w_ff2", late_bufs, name="mm_dw_ff2")
    df1 = LW.bwd(df, "w_ff2", name="mm_df1", extras=[r1], out_dtypes=(BF16,),
              epilogue=lambda acc, r: (acc * (2.0 * r.astype(F32)),))
    late_bufs = _late_grad(h2, df1, "w_ff1", late_bufs, name="mm_dw_ff1")
    dh2 = LW.bwd(df1, "w_ff1", name="mm_dh2")

    def bwd_mid(dy_, dh2_, x1_, m, g2, g1):
        d1, dg2 = _rms_bwd(x1_, g2, dh2_)
        dx1_ = dy_ + d1
        dm, dg1 = _rms_bwd(m, g1, dx1_)
        return dx1_, dm, dg2, dg1

    dx1, dmo, g_norm_mlp_pre, g_norm_mix_post = _rowwise(
        bwd_mid, [(dy, D, 0), (dh2, D, 0), (x1, D, 0), (mo, D, 0)],
        [vec["norm_mlp_pre"], vec["norm_mix_post"]], [(D, F32), (D, BF16)], [(1, D), (1, D)],
        name="bwd_mid")
    late_bufs = _late_grad(merged, dmo, "w_out", late_bufs, name="mm_dw_out")
    dmerged = LW.bwd(dmo, "w_out", name="mm_dmerged")

    def dmerge(dm, ga, gc, ya, yc, b2):
        sga, sgc = _sigmoid(ga), _sigmoid(gc)
        dya = dm * sga
        dyc = dm * sgc
        dga = dm * ya * sga * (1.0 - sga)
        dgc = dm * (yc + b2) * sgc * (1.0 - sgc)
        return dya, dyc, dga, dgc, jnp.sum(dyc, axis=0, keepdims=True)

    dya, dyc, dga, dgc, g_b_pw2 = _rowwise(
        dmerge, [(dmerged, D, 0), (z, D, 3), (z, D, 4), (y_attn, D, 0), (y_conv, D, 0)], [vec["b_pw2"]],
        [(D, BF16), (D, BF16), (D, BF16), (D, BF16)], [(1, D)], name="dmerge")

    late_bufs = _late_grad(attn, dya, "w_o_attn", late_bufs, name="mm_dw_o_attn")
    late_bufs = _late_grad(u3, dyc, "w_pw2", late_bufs, name="mm_dw_pw2")
    late_token = send_late_grads(late_bufs)

    dattn = LW.bwd(dya, "w_o_attn", out_dtype=BF16, name="mm_dattn")

    def delta_fn(do_, o_):
        do32 = do_.astype(F32)
        pr = do32 * o_.astype(F32)
        ln = lax.broadcasted_iota(jnp.int32, (pr.shape[0], NOPE), 1)
        cols = []
        for hd in range(N_HEADS):
            dl = jnp.sum(pr[:, hd * NOPE:(hd + 1) * NOPE], axis=-1, keepdims=True)
            hi = dl.astype(BF16).astype(F32)
            cols.append(do32[:, hd * NOPE:(hd + 1) * NOPE])
            cols.append(jnp.where(ln == 0, -hi, jnp.where(ln == 1, hi - dl, 0.0)))
        return (jnp.concatenate(cols, axis=1),)

    (do_ext,) = _rowwise(delta_fn, [(dattn, D, 0), (attn, D, 0)], [], [(2048, BF16)], name="attn_delta",
                         after=[late_token])
    dq, dk, dv = _flash_bwd(q, k, v, do_ext, lse)

    def unrope(dq_, dk_, dv_, cs_, sa_, sb_):
        qs, kn = [], []
        dkr = jnp.zeros_like(cs_)
        for hd in range(N_HEADS):
            qs.append(dq_[:, hd * QK_PAD:hd * QK_PAD + NOPE] * SCALE)
            qs.append(_rope_t(dq_[:, hd * QK_PAD + NOPE:(hd + 1) * QK_PAD] * SCALE, cs_, sa_, sb_))
            kn.append(dk_[:, hd * QK_PAD:hd * QK_PAD + NOPE] * K_UNSCALE)
            dkr = dkr + dk_[:, hd * QK_PAD + NOPE:(hd + 1) * QK_PAD]
        return (jnp.concatenate(qs, axis=1), jnp.concatenate(kn + [dv_], axis=1),
                _rope_t(dkr * K_UNSCALE, cs_, sa_, sb_))

    dq_pre, dkv, dkr = _rowwise(
        unrope, [(dq, 2048, 0), (dk, 2048, 0), (dv, 1024, 0), (cs, 128, 0), (sa, 128, 0), (sb, 128, 0)], [],
        [(2048, BF16), (2048, BF16), (128, F32)], name="unrope")
    g_w_uq = _mm(cqn, dq_pre, ta=True, name="mm_dw_uq")
    dcqn = _mm(dq_pre, w_uq, tb=True, name="mm_dcqn")
    g_w_ukv = _mm(ckvn, dkv, ta=True, name="mm_dw_ukv")
    dckvn = _mm(dkv, w_ukv, tb=True, name="mm_dckvn")

    def small_bwd(zs, dcq_, dckv_, dkr_, qn, kvn):
        dcq, dqn = _rms_bwd(zs[:, :Q_RANK], qn, dcq_)
        dckv, dkvn = _rms_bwd(zs[:, Q_RANK:Q_RANK + KV_RANK], kvn, dckv_)
        pad = jnp.zeros((zs.shape[0], Z_SMALL - 768), F32)
        return jnp.concatenate([dcq, dckv, dkr_, pad], axis=1), dqn, dkvn

    dz_small, g_q_norm, g_kv_norm = _rowwise(
        small_bwd, [(z, Z_SMALL, 0), (dcqn, Q_RANK, 0), (dckvn, KV_RANK, 0), (dkr, 128, 0)],
        [vec["q_norm"], vec["kv_norm"]], [(Z_SMALL, BF16)], [(1, Q_RANK), (1, KV_RANK)], name="small_bwd")

    du3 = LW.bwd(dyc, "w_pw2", name="mm_du3")

    def ln_silu_bwd(d3, u, g, b):
        nh, rs, y = ln_parts(u, g, b)
        sg = _sigmoid(y)
        dyv = d3 * (sg * (1.0 + y * (1.0 - sg)))
        dnh = dyv * g
        du = rs * (dnh - jnp.mean(dnh, axis=-1, keepdims=True)
                   - nh * jnp.mean(dnh * nh, axis=-1, keepdims=True))
        return (du, jnp.sum(dyv * nh, axis=0, keepdims=True), jnp.sum(dyv, axis=0, keepdims=True),
                jnp.sum(du, axis=0, keepdims=True))

    du1, g_ln_g, g_ln_b, g_conv_b = _rowwise(
        ln_silu_bwd, [(du3, D, 0), (u1, D, 0)], [vec["conv_ln_g"], vec["conv_ln_b"]], [(D, F32)],
        [(1, D), (1, D), (1, D)], name="ln_silu_bwd")
    du0, g_conv_w = _conv_bwd(u0, du1, cw)

    def glu_bwd(d0, a, b):
        sg = _sigmoid(b)
        return d0 * sg, d0 * a * sg * (1.0 - sg)

    dza, dzb = _rowwise(glu_bwd, [(du0, D, 0), (z, D, 1), (z, D, 2)], [], [(D, BF16), (D, BF16)],
                        name="glu_bwd")

    dz = jnp.concatenate([dz_small, dza, dzb, dga, dgc], axis=1)
    g_w_in_t = _mm(dz, h, ta=True, name="mm_dw_in")
    early_token = send_early_grads({
        "w_in_t": g_w_in_t,
        "w_uq": g_w_uq.reshape(Q_RANK, N_HEADS, QK_PAD)[:, :, :NOPE + ROPE].reshape(Q_RANK, 1536),
        "w_uk": g_w_ukv[:, :1024], "w_uv": g_w_ukv[:, 1024:]})
    dh = _mm(dz, w_in_t, name="mm_dh", after=[early_token])

    def final(dx1_, dh_, x, g):
        d, dg = _rms_bwd(x, g, dh_)
        return dx1_ + d, dg

    grad_x, g_norm_mix_pre = _rowwise(final, [(dx1, D, 0), (dh, D, 0), (xs, D, 0)], [vec["norm_mix_pre"]],
                                      [(D, F32)], [(1, D)], name="bwd_final")

    gs = {
        "norm_mix_pre": g_norm_mix_pre, "conv_b": g_conv_b, "conv_ln_g": g_ln_g, "conv_ln_b": g_ln_b,
        "b_pw2": g_b_pw2, "norm_mix_post": g_norm_mix_post, "norm_mlp_pre": g_norm_mlp_pre,
        "norm_mlp_post": g_norm_mlp_post, "q_norm": g_q_norm, "kv_norm": g_kv_norm,
    }
    return grad_x, gs, g_conv_w, loss


def _row1024(a):
    return jnp.pad(a, ((0, 0), (0, 1024 - a.shape[1])))


BIG = tuple(n for n, _ in EARLY) + tuple(n for n, _, _ in LATE)
ORDER = ("norm_mix_pre", "w_in", "q_norm", "w_uq", "kv_norm", "w_uk", "w_uv", "w_o_attn", "conv_w",
         "conv_b", "conv_ln_g", "conv_ln_b", "w_pw2", "b_pw2", "w_out", "norm_mix_post", "norm_mlp_pre",
         "w_ff1", "w_ff2", "norm_mlp_post")


def kernel(x, positions, norm_mix_pre, w_in, q_norm, w_uq, kv_norm, w_uk, w_uv, w_o_attn, conv_w, conv_b, conv_ln_g, conv_ln_b, w_pw2, b_pw2, w_out, norm_mix_post, norm_mlp_pre, w_ff1, w_ff2, norm_mlp_post, loss_target, m_norm_mix_pre, m_w_in, m_q_norm, m_w_uq, m_kv_norm, m_w_uk, m_w_uv, m_w_o_attn, m_conv_w, m_conv_b, m_conv_ln_g, m_conv_ln_b, m_w_pw2, m_b_pw2, m_w_out, m_norm_mix_post, m_norm_mlp_pre, m_w_ff1, m_w_ff2, m_norm_mlp_post, v_norm_mix_pre, v_w_in, v_q_norm, v_w_uq, v_kv_norm, v_w_uk, v_w_uv, v_w_o_attn, v_conv_w, v_conv_b, v_conv_ln_g, v_conv_ln_b, v_w_pw2, v_b_pw2, v_w_out, v_norm_mix_post, v_norm_mlp_pre, v_w_ff1, v_w_ff2, v_norm_mlp_post):
    args = dict(locals())
    wts = {n: args[n] for n in ORDER}
    mom = {n: args["m_" + n] for n in ORDER}
    var = {n: args["v_" + n] for n in ORDER}
    T = x.shape[1]
    place = jnp.stack([2 * lax.axis_index("x") + lax.axis_index("y"), lax.axis_index("c")]).astype(jnp.int32)

    w_early = _pack_early_shards(wts)
    w_late = _pack_late_shards(wts)
    cw_own = jnp.pad(conv_w.reshape(CONV_W, -1), ((0, CONV_TAPS_PAD - CONV_W), (0, 0)))
    full, cw_all = _gather_weights(w_early, cw_own)
    full = lax.dynamic_update_slice(full, w_early[None], (place[0], 0, 0))
    cw_all = lax.dynamic_update_slice(cw_all, cw_own[None], (place[0], 0, 0))
    W = _unpack_early(full)
    cw = cw_all.transpose(1, 0, 2).reshape(CONV_TAPS_PAD, D_MODEL)
    sems, w_thru, land, token = _late_gather_start(w_late, full)
    vec = {n: wts[n] for n in SMALL_VECS}
    vec["norm_mix_pre"] = vec["norm_mix_pre"] + token[:1, :1]

    def late_weights(after):
        landed = _late_gather_wait(sems, w_thru, land, after)
        return _LateWeights(lax.dynamic_update_slice(landed, w_late[None], (place[0], 0, 0)))

    sent = {}

    def send_late_grads(bufs):
        sent["late"] = (bufs[0],) + _reduce_start(bufs[1], place, name="late_grads_start")
        return sent["late"][4]

    def send_early_grads(g):
        g32 = _pack_early_grads(g)
        sent["early"] = (g32,) + _reduce_start(g32.astype(BF16), place, name="early_grads_start")
        return sent["early"][4]

    late_bufs = (lax.empty((N_CHIPS, LATE_ROWS, 1024), F32), lax.empty((N_CHIPS, LATE_ROWS, 1024), BF16))
    grad_x, gs, g_conv_w, loss = _local_step(x[0], positions.reshape(T, 1), loss_target[0], W, late_weights,
                                             late_bufs, send_late_grads, send_early_grads, cw, vec)

    def finish(group, after):
        g32, sems_, g_thru, land_, _ = sent[group]
        landed = _reduce_wait(sems_, g_thru, land_, after, name=group + "_grads_wait")
        red = _reduce_sum(place, g32, landed, name=group + "_grads_sum")
        return _join_halves(red, name=group + "_grads_join")

    def as2d(a):
        return a.reshape(-1, a.shape[-1]) if a.ndim != 4 or a.shape[2] == 1 else a.reshape(a.shape[1], -1)

    grads, delta, new_m, new_v = {}, {}, {}, {}

    def update(n):
        shp = wts[n].shape
        outs = _adamw(as2d(wts[n]), as2d(grads[n]), as2d(mom[n]), as2d(var[n]), name="adamw_" + n)
        delta[n], new_m[n], new_v[n] = (o.reshape(shp) for o in outs)

    g_late = finish("late", [sent["early"][4]])
    for n, off, rows in LATE:
        shp = wts[n].shape
        outs = _adamw(as2d(wts[n]), g_late, as2d(mom[n]), as2d(var[n]), g_row=off, name="adamw_" + n)
        delta[n], new_m[n], new_v[n], grads[n] = (o.reshape(shp) for o in outs)

    svec = jnp.concatenate(
        [_row1024(gs[n]) for n in SMALL_VECS] + [_row1024(loss)]
        + [jnp.zeros((SMALL_CONVW_ROW - SMALL_LOSS_ROW - 1, 1024), F32), g_conv_w], axis=0)
    small_sent = _reduce_start(svec, place, name="small_grads_start", whole=True)
    landed = _reduce_wait(small_sent[0], small_sent[1], small_sent[2], [delta["w_ff2"]],
                          name="small_grads_wait", whole=True)
    ssum = _sum_small((2 * place[0] + place[1])[None], svec, landed)
    for r, n in enumerate(SMALL_VECS):
        grads[n] = ssum[r:r + 1, :wts[n].shape[1]]
    cw_sum = ssum[SMALL_CONVW_ROW:SMALL_CONVW_ROW + CONV_W]
    grads["conv_w"] = lax.dynamic_slice_in_dim(cw_sum, place[0] * 256, 256, axis=1).reshape(conv_w.shape)
    loss_out = ssum[SMALL_LOSS_ROW, 0]
    update("conv_w")
    small = [jnp.concatenate([_row1024(t[n]) for n in SMALL_VECS], axis=0) for t in (wts, grads, mom, var)]
    outs = _adamw(*small, name="adamw_small")
    for r, n in enumerate(SMALL_VECS):
        delta[n], new_m[n], new_v[n] = (o[r:r + 1, :wts[n].shape[1]] for o in outs)

    g_early = finish("early", [grad_x, outs[0], delta["w_ff2"]])
    grads["w_in"] = g_early[:1200].T[None]
    off = 1200
    for n, rows in EARLY[1:]:
        grads[n] = g_early[off:off + rows].reshape(wts[n].shape)
        off += rows
    for n, _ in EARLY:
        update(n)

    return (loss_out, grad_x[None], *[grads[n] for n in ORDER], *[delta[n] for n in ORDER],
            *[new_m[n] for n in ORDER], *[new_v[n] for n in ORDER])
```

```python
import functools

import numpy as np
import jax
import jax.numpy as jnp
from jax import lax
from jax.experimental import pallas as pl
from jax.experimental.pallas import tpu as pltpu

F32 = jnp.float32
BF16 = jnp.bfloat16
MESH = pl.DeviceIdType.MESH

D_MODEL = 1024
N_HEADS = 8
NOPE = 128
ROPE = 64
QK_PAD = 256
Q_RANK = 384
KV_RANK = 256
CONV_W = 31
CONV_TAPS_PAD = 32
HALO = 32
D_FF = 4096
EPS = 1e-6
ROPE_THETA = 10000.0
SCALE = float((NOPE + ROPE) ** -0.5)
Q_SCALE = SCALE * float(np.log2(np.e))
K_UNSCALE = float(np.log(2.0))
Z_SMALL = 1024
Z_COLS = Z_SMALL + 4 * D_MODEL
N_CHIPS = 4
N_DEV = 8

ADAM_LR = 0.001
ADAM_B1 = 0.9
ADAM_B2 = 0.999
ADAM_EPS = 1e-08
ADAM_WD = 0.01
ADAM_STEP = 10

VMEM_LIMIT = 48 * 1024 * 1024
NEG = -1e30

EARLY = (("w_in", 1200), ("w_uq", 144), ("w_uk", 64), ("w_uv", 64))
EARLY_ROWS = 1536
LATE = (("w_ff1", 0, 1024), ("w_o_attn", 1024, 256), ("w_pw2", 1536, 256), ("w_out", 1792, 256),
        ("w_ff2", 2048, 1024))
LATE_OFF = {n: off for n, off, _ in LATE}
LATE_ROWS = 3072
SUM_TILE = 256

SMALL_VECS = ("norm_mix_pre", "conv_b", "conv_ln_g", "conv_ln_b", "b_pw2", "norm_mix_post",
              "norm_mlp_pre", "norm_mlp_post", "q_norm", "kv_norm")
SMALL_LOSS_ROW = 10
SMALL_CONVW_ROW = 16
SMALL_ROWS = 48


ANY = pl.BlockSpec(memory_space=pl.ANY)


def _params(*sem):
    return pltpu.CompilerParams(dimension_semantics=sem, vmem_limit_bytes=VMEM_LIMIT)


def _tile(n, cap):
    if n <= cap:
        return n
    t = (cap // 128) * 128
    while n % t:
        t -= 128
    return t


def _row_tile(n, cap):
    if n <= cap:
        return n
    t = (cap // 8) * 8
    while n % t:
        t -= 8
    return t


MM_VMEM_BUDGET = 40 * 1024 * 1024


def _mm_tiles(M, N, K, in_bytes, out_bytes):
    tn = _tile(N, 1024)
    tk = K
    while True:
        for cap in (1024, 512, 256):
            tm = _tile(M, cap)
            need = 2 * tk * (tm * in_bytes[0] + tn * in_bytes[1]) + 2 * tm * tn * out_bytes
            need += tm * tn * 4 if tk < K else 0
            if need <= MM_VMEM_BUDGET:
                return tm, tn, tk
        tk = _tile(K, tk - 128)


def _mm(a, b, *, name, ta=False, tb=False, out_dtype=F32, extras=(), epilogue=None, out_dtypes=None,
        after=(), shape=None, tiles=None, b_spec=None, into=None):
    if shape is None:
        M, K = (a.shape[1], a.shape[0]) if ta else a.shape
        N = b.shape[0] if tb else b.shape[1]
        assert K == (b.shape[1] if tb else b.shape[0]), (a.shape, b.shape, ta, tb)
    else:
        M, N, K = shape
    out_dtypes = tuple(out_dtypes or (out_dtype,))
    if into is not None:
        out_dtypes = tuple(buf.dtype for buf, _ in into)
    if tiles is None:
        out_bytes = sum(jnp.dtype(d).itemsize for d in out_dtypes) + sum(e.dtype.itemsize for e in extras)
        tiles = _mm_tiles(M, N, K, (a.dtype.itemsize, b.dtype.itemsize), out_bytes)
    tm, tn, tk = tiles
    nk = K // tk
    dims = (((0 if ta else 1,), (1 if tb else 0,)), ((), ()))
    n_ex, n_out = len(extras), len(out_dtypes)
    n_pass = len(after) + (n_out if into is not None else 0)

    def body(*refs):
        a_ref, b_ref = refs[:2]
        ex_refs = refs[2:2 + n_ex]
        o_refs = refs[2 + n_ex + n_pass:2 + n_ex + n_pass + n_out]
        k = pl.program_id(2)
        bv = b_ref[...]
        if bv.ndim == 3:
            bv = bv.reshape(-1, bv.shape[-1])
        part = lax.dot_general(a_ref[...].astype(BF16), bv.astype(BF16), dims, preferred_element_type=F32)

        def finish(acc):
            res = epilogue(acc, *[r[...] for r in ex_refs]) if epilogue else (acc,) * n_out
            for o_ref, r in zip(o_refs, res):
                o_ref[...] = r.astype(o_ref.dtype)

        if nk == 1:
            finish(part)
        else:
            acc_ref = refs[-1]

            @pl.when(k == 0)
            def _():
                acc_ref[...] = part

            @pl.when((k > 0) & (k < nk - 1))
            def _():
                acc_ref[...] += part

            @pl.when(k == nk - 1)
            def _():
                finish(acc_ref[...] + part)

    a_spec = (pl.BlockSpec((tk, tm), lambda i, j, k: (k, i)) if ta
              else pl.BlockSpec((tm, tk), lambda i, j, k: (i, k)))
    if b_spec is None:
        b_spec = (pl.BlockSpec((tn, tk), lambda i, j, k: (j, k)) if tb
                  else pl.BlockSpec((tk, tn), lambda i, j, k: (k, j)))
    o_spec = pl.BlockSpec((tm, tn), lambda i, j, k: (i, j))
    if into is None:
        out_shape = [jax.ShapeDtypeStruct((M, N), dt) for dt in out_dtypes]
        out_specs, passed, aliases = [o_spec] * n_out, list(after), {}
    else:
        out_shape = [jax.ShapeDtypeStruct(buf.shape, buf.dtype) for buf, _ in into]
        out_specs = [spec for _, spec in into]
        passed = list(after) + [buf for buf, _ in into]
        aliases = {2 + n_ex + len(after) + o: o for o in range(n_out)}
    res = pl.pallas_call(
        body, name=name, out_shape=out_shape,
        grid=(M // tm, N // tn, nk),
        in_specs=[a_spec, b_spec] + [o_spec] * n_ex + [ANY] * len(passed),
        out_specs=out_specs, input_output_aliases=aliases,
        scratch_shapes=[pltpu.VMEM((tm, tn), F32)] if nk > 1 else [],
        compiler_params=_params("parallel", "parallel", "arbitrary"),
    )(a, b, *extras, *passed)
    return res[0] if n_out == 1 else res


class _LateWeights:
    def __init__(self, pack):
        self.pack = pack

    def fwd(self, x, n, *, name, **kw):
        T, off = x.shape[0], LATE_OFF[n]
        tm = _tile(T, 512)
        if n == "w_ff1":
            return _mm(x, self.pack, name=name, shape=(T, 4096, 1024), tiles=(_tile(T, 1024), 1024, 1024),
                       b_spec=pl.BlockSpec((None, 1024, 1024), lambda i, j, k: (j, 0, 0)), **kw)
        if n == "w_ff2":
            view = self.pack.reshape(N_CHIPS, LATE_ROWS // 1024, 1024, 1024)
            return _mm(x, view, name=name, shape=(T, 1024, 4096), tiles=(tm, 1024, 4096),
                       b_spec=pl.BlockSpec((N_CHIPS, None, 1024, 1024), lambda i, j, k: (0, off // 1024, 0, 0)),
                       **kw)
        return _mm(x, self.pack.reshape(N_CHIPS, LATE_ROWS // 256, 256, 1024), name=name,
                   shape=(T, 1024, 1024), tiles=(_tile(T, 1024), 1024, 1024), b_spec=self._whole(off), **kw)

    @staticmethod
    def _whole(off):
        return pl.BlockSpec((N_CHIPS, None, 256, 1024), lambda i, j, k: (0, off // 256, 0, 0))

    def bwd(self, dy, n, *, name, **kw):
        T, off = dy.shape[0], LATE_OFF[n]
        tm = _tile(T, 1024)
        if n == "w_ff1":
            return _mm(dy, self.pack, tb=True, name=name, shape=(T, 1024, 4096), tiles=(tm, 1024, 1024),
                       b_spec=pl.BlockSpec((None, 1024, 1024), lambda i, j, k: (k, 0, 0)), **kw)
        if n == "w_ff2":
            return _mm(dy, self.pack, tb=True, name=name, shape=(T, 4096, 1024), tiles=(tm, 1024, 1024),
                       b_spec=pl.BlockSpec((None, 1024, 1024), lambda i, j, k: (j, off // 1024, 0)), **kw)
        return _mm(dy, self.pack.reshape(N_CHIPS, LATE_ROWS // 256, 256, 1024), tb=True, name=name,
                   shape=(T, 1024, 1024), tiles=(tm, 1024, 1024), b_spec=self._whole(off), **kw)


def _late_grad(x, dy, n, bufs, *, name):
    T, off = x.shape[0], LATE_OFF[n]
    if n == "w_ff1":
        spec = pl.BlockSpec((None, 512, 1024), lambda i, j, k: (j, i, 0))
        shape, tiles = (1024, 4096, T), (512, 1024, T)
    elif n == "w_ff2":
        spec = pl.BlockSpec((None, 512, 1024), lambda i, j, k: (i // 2, off // 512 + i % 2, 0))
        shape, tiles = (4096, 1024, T), (512, 1024, T)
    else:
        spec = pl.BlockSpec((None, 256, 1024), lambda i, j, k: (i, off // 256, 0))
        shape, tiles = (1024, 1024, T), (256, 1024, T)
    return _mm(x, dy, ta=True, name=name, shape=shape, tiles=tiles, into=[(b, spec) for b in bufs])


def _rowwise(fn, rows, consts, outs, reds=(), *, name, tm=256, after=()):
    T = rows[0][0].shape[0]
    tm = min(tm, T)
    n_row, n_const, n_out = len(rows), len(consts), len(outs)

    def body(*refs):
        i = pl.program_id(0)
        vals = [r[...] for r in refs[:n_row + n_const]]
        res = fn(*vals)
        out_refs = refs[n_row + n_const + len(after):]
        for k in range(n_out):
            out_refs[k][...] = res[k].astype(out_refs[k].dtype)
        for k in range(len(reds)):
            ref = out_refs[n_out + k]

            @pl.when(i == 0)
            def _(ref=ref):
                ref[...] = jnp.zeros_like(ref)

            ref[...] += res[n_out + k]

    in_specs = [pl.BlockSpec((tm, w), functools.partial(lambda i, cb: (i, cb), cb=cb))
                for (_, w, cb) in rows]
    in_specs += [pl.BlockSpec(c.shape, lambda i: (0, 0)) for c in consts]
    in_specs += [ANY] * len(after)
    out_specs = [pl.BlockSpec((tm, w), lambda i: (i, 0)) for (w, _) in outs]
    out_specs += [pl.BlockSpec(s, lambda i: (0, 0)) for s in reds]
    out_shape = [jax.ShapeDtypeStruct((T, w), dt) for (w, dt) in outs]
    out_shape += [jax.ShapeDtypeStruct(s, F32) for s in reds]
    return pl.pallas_call(
        body, name=name, out_shape=out_shape, grid=(T // tm,),
        in_specs=in_specs, out_specs=out_specs,
        compiler_params=_params("arbitrary"),
    )(*[r[0] for r in rows], *consts, *after)


def _rms(x, g):
    r = lax.rsqrt(jnp.mean(x * x, axis=-1, keepdims=True) + EPS)
    return x * r * g


def _rms_bwd(x, g, dy):
    r = lax.rsqrt(jnp.mean(x * x, axis=-1, keepdims=True) + EPS)
    n = x * r
    dyg = dy * g
    dx = r * (dyg - n * jnp.mean(dyg * n, axis=-1, keepdims=True))
    return dx, jnp.sum(dy * n, axis=0, keepdims=True)


def _sigmoid(x):
    return 1.0 / (1.0 + jnp.exp(-x))


def _rope(x, cs, sa, sb):
    return x * cs + pltpu.roll(x, 32, 1) * sa + pltpu.roll(x, 96, 1) * sb


def _rope_t(dy, cs, sa, sb):
    return dy * cs + pltpu.roll(dy * sa, 96, 1) + pltpu.roll(dy * sb, 32, 1)


def _causal_mask(t):
    row = lax.broadcasted_iota(jnp.int32, (t, t), 0)
    col = lax.broadcasted_iota(jnp.int32, (t, t), 1)
    return col <= row


FLASH_FWD_TQ = 4096
FLASH_FWD_TK = 1024
FLASH_FWD_HEADS = 1
FLASH_BWD_TQ = 1024
FLASH_BWD_TK = 512
NT_DIMS = (((1,), (1,)), ((), ()))
TN_DIMS = (((0,), (0,)), ((), ()))


def _flash_fwd(q, k, v, *, tq=FLASH_FWD_TQ, tk=FLASH_FWD_TK, hp=FLASH_FWD_HEADS):
    T = q.shape[0]
    tq, tk = min(tq, T), min(tk, T)
    n_diag = tq // tk
    assert n_diag >= 1 and N_HEADS % hp == 0
    heads = range(hp)

    def body(q_ref, k_ref, v_ref, o_ref, lse_ref):
        i = pl.program_id(1)

        def cols(ref, rows, h, width=QK_PAD):
            return ref[rows, h * width:(h + 1) * width]

        def logits(j, r0=0):
            sl = pl.ds(pl.multiple_of(j * tk, tk), tk)
            return tuple(lax.dot_general(cols(q_ref, slice(r0, tq), h), cols(k_ref, sl, h), NT_DIMS,
                                         preferred_element_type=F32) for h in heads)

        def update(j, s, m, acc, masked, r0=0):
            sl = pl.ds(pl.multiple_of(j * tk, tk), tk)
            m_out, acc_out = [], []
            for h in heads:
                sh = s[h]
                if masked:
                    row = i * tq + r0 + lax.broadcasted_iota(jnp.int32, (tq - r0, tk), 0)
                    col = j * tk + lax.broadcasted_iota(jnp.int32, (tq - r0, tk), 1)
                    sh = jnp.where(col <= row, sh, NEG)
                m_old, acc_old = m[h][r0:], acc[h][r0:]
                m_new = jnp.maximum(m_old, jnp.max(sh, axis=-1, keepdims=True))
                p = jnp.exp2(sh - m_new)
                acc_new = (jnp.exp2(m_old - m_new) * acc_old
                           + jnp.dot(p.astype(BF16), cols(v_ref, sl, h), preferred_element_type=F32))
                if r0:
                    m_new = jnp.concatenate([m[h][:r0], m_new], axis=0)
                    acc_new = jnp.concatenate([acc[h][:r0], acc_new], axis=0)
                m_out.append(m_new)
                acc_out.append(acc_new)
            return tuple(m_out), tuple(acc_out)

        def step(j, carry):
            s, m, acc = carry
            s_next = logits(j + 1)
            m, acc = update(j, s, m, acc, False)
            return s_next, m, acc

        first = i * n_diag
        init = (logits(0), tuple(jnp.full((tq, 1), NEG, F32) for _ in heads),
                tuple(jnp.zeros((tq, QK_PAD), F32) for _ in heads))
        s, m, acc = lax.fori_loop(0, first, step, init)
        for d in range(n_diag):
            m, acc = update(first + d, s, m, acc, True, r0=d * tk)
            if d + 1 < n_diag:
                s = logits(first + d + 1, r0=(d + 1) * tk)
        for h in heads:
            l = acc[h][:, NOPE:]
            o_ref[:, h * NOPE:(h + 1) * NOPE] = (acc[h][:, :NOPE] / l).astype(BF16)
            lse_ref[:, h * NOPE:(h + 1) * NOPE] = m[h] + jnp.log2(l)

    return pl.pallas_call(
        body, name="flash_fwd",
        out_shape=[jax.ShapeDtypeStruct((T, N_HEADS * NOPE), BF16),
                   jax.ShapeDtypeStruct((T, N_HEADS * NOPE), F32)],
        grid=(N_HEADS // hp, T // tq),
        in_specs=[pl.BlockSpec((tq, hp * QK_PAD), lambda h, i: (i, h)),
                  pl.BlockSpec((T, hp * QK_PAD), lambda h, i: (0, h)),
                  pl.BlockSpec((T, hp * QK_PAD), lambda h, i: (0, h))],
        out_specs=[pl.BlockSpec((tq, hp * NOPE), lambda h, i: (i, h)),
                   pl.BlockSpec((tq, hp * NOPE), lambda h, i: (i, h))],
        compiler_params=_params("parallel", "arbitrary"),
    )(q, k, v)


def _flash_bwd(q, k, v, do, lse, *, tq=FLASH_BWD_TQ, tk=FLASH_BWD_TK):
    T = q.shape[0]
    tq, tk = min(tq, T), min(tk, T)

    def body(q_ref, k_ref, v_ref, do_ref, lse_ref, dq_ref, dk_ref, dv_ref, dq_acc):
        dq_acc[...] = jnp.zeros_like(dq_acc)
        for j in range(T // tk):
            k0 = j * tk
            kj = k_ref[k0:k0 + tk, :]
            vj = v_ref[k0:k0 + tk, :]
            dk = jnp.zeros((tk, QK_PAD), F32)
            dv = jnp.zeros((tk, NOPE), F32)
            for i in range(k0 // tq, T // tq):
                r0 = max(i * tq, k0)
                r1 = (i + 1) * tq
                qi = q_ref[r0:r1, :]
                doi = do_ref[r0:r1, :]
                s = lax.dot_general(qi, kj, NT_DIMS, preferred_element_type=F32)
                p = jnp.exp2(s - lse_ref[r0:r1, :][:, :1])
                if r0 < k0 + tk:
                    row = r0 + lax.broadcasted_iota(jnp.int32, (r1 - r0, tk), 0)
                    col = k0 + lax.broadcasted_iota(jnp.int32, (r1 - r0, tk), 1)
                    p = jnp.where(col <= row, p, 0.0)
                dv = dv + lax.dot_general(p.astype(BF16), doi[:, :NOPE], TN_DIMS, preferred_element_type=F32)
                ds = (p * lax.dot_general(doi, vj, NT_DIMS, preferred_element_type=F32)).astype(BF16)
                dk = dk + lax.dot_general(ds, qi, TN_DIMS, preferred_element_type=F32)
                dq_acc[r0:r1, :] += jnp.dot(ds, kj, preferred_element_type=F32)
            dk_ref[k0:k0 + tk, :] = dk.astype(BF16)
            dv_ref[k0:k0 + tk, :] = dv.astype(BF16)
        dq_ref[...] = dq_acc[...].astype(BF16)

    return pl.pallas_call(
        body, name="flash_bwd",
        out_shape=[jax.ShapeDtypeStruct((T, N_HEADS * QK_PAD), BF16),
                   jax.ShapeDtypeStruct((T, N_HEADS * QK_PAD), BF16),
                   jax.ShapeDtypeStruct((T, N_HEADS * NOPE), BF16)],
        grid=(N_HEADS,),
        in_specs=[pl.BlockSpec((T, QK_PAD), lambda h: (0, h)),
                  pl.BlockSpec((T, QK_PAD), lambda h: (0, h)),
                  pl.BlockSpec((T, QK_PAD), lambda h: (0, h)),
                  pl.BlockSpec((T, QK_PAD), lambda h: (0, h)),
                  pl.BlockSpec((T, NOPE), lambda h: (0, h))],
        out_specs=[pl.BlockSpec((T, QK_PAD), lambda h: (0, h)),
                   pl.BlockSpec((T, QK_PAD), lambda h: (0, h)),
                   pl.BlockSpec((T, NOPE), lambda h: (0, h))],
        scratch_shapes=[pltpu.VMEM((T, QK_PAD), F32)],
        compiler_params=_params("arbitrary"),
    )(q, k, v, do, lse)


SUB = 128


def _windows(buf, t0, c0, first, last):
    span = SUB + 8 * ((last + 7) // 8)
    base = buf[t0:t0 + span, c0:c0 + SUB]
    for r in range(8):
        offs = [o for o in range(first, last + 1) if o % 8 == r]
        if offs:
            win = base if r == 0 else pltpu.roll(base, span - r, 0)
            for o in offs:
                yield o, win[o - r:o - r + SUB, :]


def _conv_fwd(u, w, bias, *, tm=256, ct=256):
    T, C = u.shape
    tm = min(tm, T)
    hb = tm // HALO
    lead = HALO - (CONV_W - 1)

    def body(cur_ref, halo_ref, w_ref, b_ref, o_ref, buf):
        t = pl.program_id(1)
        buf[0:HALO, :] = jnp.where(t > 0, halo_ref[...], 0.0)
        buf[HALO:, :] = cur_ref[...]
        for c0 in range(0, ct, SUB):
            for t0 in range(0, tm, SUB):
                acc = jnp.broadcast_to(b_ref[:, c0:c0 + SUB], (SUB, SUB))
                for o, win in _windows(buf, t0, c0, lead, lead + CONV_W - 1):
                    acc = acc + win * w_ref[o - lead:o - lead + 1, c0:c0 + SUB]
                o_ref[t0:t0 + SUB, c0:c0 + SUB] = acc

    return pl.pallas_call(
        body, name="conv_fwd",
        out_shape=jax.ShapeDtypeStruct((T, C), F32),
        grid=(C // ct, T // tm),
        in_specs=[pl.BlockSpec((tm, ct), lambda c, t: (t, c)),
                  pl.BlockSpec((HALO, ct), lambda c, t: (jnp.maximum(t * hb - 1, 0), c)),
                  pl.BlockSpec((CONV_TAPS_PAD, ct), lambda c, t: (0, c)),
                  pl.BlockSpec((1, ct), lambda c, t: (0, c))],
        out_specs=pl.BlockSpec((tm, ct), lambda c, t: (t, c)),
        scratch_shapes=[pltpu.VMEM((HALO + tm, ct), F32)],
        compiler_params=_params("parallel", "arbitrary"),
    )(u, u, w, bias)


def _conv_bwd(u, du1, w, *, tm=256, ct=256):
    T, C = u.shape
    tm = min(tm, T)
    hb = tm // HALO
    nt = T // tm
    last_halo = T // HALO - 1
    lead = HALO - (CONV_W - 1)

    def body(u_ref, uh_ref, d_ref, dh_ref, w_ref, du_ref, dw_ref, ubuf, dbuf, dw8):
        t = pl.program_id(1)
        ubuf[0:HALO, :] = jnp.where(t > 0, uh_ref[...], 0.0)
        ubuf[HALO:, :] = u_ref[...]
        dbuf[0:tm, :] = d_ref[...]
        dbuf[tm:, :] = jnp.where(t < nt - 1, dh_ref[...], 0.0)

        @pl.when(t == 0)
        def _():
            dw8[...] = jnp.zeros_like(dw8)

        for c0 in range(0, ct, SUB):
            for t0 in range(0, tm, SUB):
                acc = jnp.zeros((SUB, SUB), F32)
                for o, win in _windows(dbuf, t0, c0, 0, CONV_W - 1):
                    kk = CONV_W - 1 - o
                    acc = acc + win * w_ref[kk:kk + 1, c0:c0 + SUB]
                du_ref[t0:t0 + SUB, c0:c0 + SUB] = acc
                dsub = d_ref[t0:t0 + SUB, c0:c0 + SUB]
                for o, win in _windows(ubuf, t0, c0, lead, lead + CONV_W - 1):
                    kk = o - lead
                    dw8[8 * kk:8 * kk + 8, c0:c0 + SUB] += jnp.sum(
                        (dsub * win).reshape(SUB // 8, 8, SUB), axis=0)

        @pl.when(t == nt - 1)
        def _():
            dw_ref[...] = jnp.sum(dw8[...].reshape(CONV_TAPS_PAD, 8, ct), axis=1)

    return pl.pallas_call(
        body, name="conv_bwd",
        out_shape=[jax.ShapeDtypeStruct((T, C), F32), jax.ShapeDtypeStruct((CONV_TAPS_PAD, C), F32)],
        grid=(C // ct, nt),
        in_specs=[pl.BlockSpec((tm, ct), lambda c, t: (t, c)),
                  pl.BlockSpec((HALO, ct), lambda c, t: (jnp.maximum(t * hb - 1, 0), c)),
                  pl.BlockSpec((tm, ct), lambda c, t: (t, c)),
                  pl.BlockSpec((HALO, ct), lambda c, t: (jnp.minimum((t + 1) * hb, last_halo), c)),
                  pl.BlockSpec((CONV_TAPS_PAD, ct), lambda c, t: (0, c))],
        out_specs=[pl.BlockSpec((tm, ct), lambda c, t: (t, c)),
                   pl.BlockSpec((CONV_TAPS_PAD, ct), lambda c, t: (0, c))],
        scratch_shapes=[pltpu.VMEM((HALO + tm, ct), F32), pltpu.VMEM((tm + HALO, ct), F32),
                        pltpu.VMEM((CONV_TAPS_PAD * 8, ct), F32)],
        compiler_params=_params("parallel", "arbitrary"),
    )(u, u, du1, du1, w)


def _adamw(w, g, m, v, *, name, g_row=None):
    R, C = w.shape
    tr = _row_tile(R, max(8, (1 << 18) // C // 8 * 8))
    first = 0 if g_row is None else g_row // tr
    assert g_row is None or g_row % tr == 0

    def body(w_ref, g_ref, m_ref, v_ref, d_ref, nm_ref, nv_ref, *g_out):
        gv = g_ref[...]
        nm = ADAM_B1 * m_ref[...] + (1.0 - ADAM_B1) * gv
        nv = ADAM_B2 * v_ref[...] + (1.0 - ADAM_B2) * (gv * gv)
        m_hat = nm / (1.0 - ADAM_B1 ** ADAM_STEP)
        v_hat = nv / (1.0 - ADAM_B2 ** ADAM_STEP)
        d_ref[...] = -ADAM_LR * (m_hat / (jnp.sqrt(v_hat) + ADAM_EPS) + ADAM_WD * w_ref[...])
        nm_ref[...] = nm
        nv_ref[...] = nv
        for ref in g_out:
            ref[...] = gv

    spec = pl.BlockSpec((tr, C), lambda i: (i, 0))
    g_spec = pl.BlockSpec((tr, C), lambda i: (first + i, 0))
    n_out = 3 if g_row is None else 4
    return pl.pallas_call(
        body, name=name, out_shape=[jax.ShapeDtypeStruct((R, C), F32)] * n_out,
        grid=(R // tr,), in_specs=[spec, g_spec, spec, spec], out_specs=[spec] * n_out,
        compiler_params=_params("parallel"),
    )(w, g, m, v)


def _coords():
    return lax.axis_index("x"), lax.axis_index("y"), lax.axis_index("c")


def _remote(src, dst, ssem, rsem, to):
    return pltpu.make_async_remote_copy(src_ref=src, dst_ref=dst, send_sem=ssem, recv_sem=rsem,
                                        device_id=to, device_id_type=MESH)


def _half(c, rows):
    return pl.ds(pl.multiple_of(c * (rows // 2), 16), rows // 2)


def _gather_weights(wpack, cw):
    def body(w_ref, cw_ref, full_ref, cwf_ref, ssem, rsem):
        x, y, c = _coords()
        q = 2 * x + y
        sib = (x, y, 1 - c)
        chips = [(1 - x, y), (x, 1 - y), (1 - x, 1 - y)]
        mine, other = _half(c, w_ref.shape[0]), _half(1 - c, w_ref.shape[0])
        first = []
        for j, (px, py) in enumerate(chips):
            first.append(_remote(w_ref.at[mine], full_ref.at[q, mine], ssem.at[j], rsem.at[j], (px, py, c)))
            first.append(_remote(cw_ref, cwf_ref.at[q], ssem.at[6 + j], rsem.at[6 + j], (px, py, c)))
        for cp in first:
            cp.start()
        passed = []
        for j, (px, py) in enumerate(chips):
            pq = 2 * px + py
            _remote(w_ref.at[mine], full_ref.at[pq, mine], ssem.at[j], rsem.at[j], (px, py, c)).wait_recv()
            fw = _remote(full_ref.at[pq, mine], full_ref.at[pq, mine], ssem.at[3 + j], rsem.at[3 + j], sib)
            fw.start()
            passed.append(fw)
        for j, (px, py) in enumerate(chips):
            pq = 2 * px + py
            _remote(full_ref.at[pq, other], full_ref.at[pq, other], ssem.at[3 + j], rsem.at[3 + j], sib).wait_recv()
            _remote(cw_ref, cwf_ref.at[pq], ssem.at[6 + j], rsem.at[6 + j], (px, py, c)).wait_recv()
        for cp in first + passed:
            cp.wait_send()

    return pl.pallas_call(
        body, name="gather_weights",
        out_shape=[jax.ShapeDtypeStruct((N_CHIPS,) + wpack.shape, wpack.dtype),
                   jax.ShapeDtypeStruct((N_CHIPS,) + cw.shape, cw.dtype)],
        in_specs=[ANY, ANY], out_specs=[ANY, ANY],
        scratch_shapes=[pltpu.SemaphoreType.DMA((9,)), pltpu.SemaphoreType.DMA((9,))],
    )(wpack, cw)


SEM = pl.BlockSpec(memory_space=pltpu.SEMAPHORE)
HBM = pl.BlockSpec(memory_space=pltpu.HBM)
N_LATE = 6


def _late_peers(x, y, c):
    out = []
    for j, (px, py) in enumerate([(1 - x, y), (x, 1 - y), (1 - x, 1 - y)]):
        for t in (0, 1):
            out.append((2 * j + t, px, py, c if t == 0 else 1 - c))
    return out


def _late_gather_start(w, after):
    rows = w.shape[0]

    def body(w_ref, land_ref, after_ref, *outs):
        sems = outs[:2 * N_LATE]
        token = outs[2 * N_LATE + 2]
        x, y, c = _coords()
        q = 2 * x + y
        mine = _half(c, rows)
        for k, px, py, pc in _late_peers(x, y, c):
            _remote(w_ref.at[mine], land_ref.at[q, mine], sems[k], sems[N_LATE + k], (px, py, pc)).start()
        token[...] = jnp.zeros_like(token)

    res = pl.pallas_call(
        body, name="late_gather_start",
        out_shape=tuple([pltpu.SemaphoreType.DMA(())] * (2 * N_LATE)) + (
            pltpu.HBM(w.shape, w.dtype), pltpu.HBM((N_CHIPS,) + w.shape, w.dtype),
            jax.ShapeDtypeStruct((8, 128), F32)),
        in_specs=(HBM, HBM, ANY),
        out_specs=tuple([SEM] * (2 * N_LATE)) + (HBM, HBM, pl.BlockSpec(memory_space=pltpu.VMEM)),
        input_output_aliases={0: 2 * N_LATE, 1: 2 * N_LATE + 1},
        compiler_params=pltpu.CompilerParams(has_side_effects=pltpu.SideEffectType.DATAFLOW_SIDE_EFFECTING),
    )(pltpu.with_memory_space_constraint(w, pltpu.HBM),
      pltpu.with_memory_space_constraint(lax.empty((N_CHIPS,) + w.shape, w.dtype), pltpu.HBM), after)
    return res[:2 * N_LATE], res[2 * N_LATE], res[2 * N_LATE + 1], res[2 * N_LATE + 2]


def _late_gather_wait(sems, w_thru, land_thru, after):
    rows = w_thru.shape[0]

    def body(w_ref, land_ref, *rest):
        sems = rest[:2 * N_LATE]
        x, y, c = _coords()
        for k, px, py, pc in _late_peers(x, y, c):
            cp = _remote(w_ref.at[_half(pc, rows)], land_ref.at[2 * px + py, _half(pc, rows)],
                         sems[k], sems[N_LATE + k], (px, py, pc))
            cp.wait_send()
            cp.wait_recv()

    return pl.pallas_call(
        body, name="late_gather_wait",
        out_shape=(pltpu.HBM(w_thru.shape, w_thru.dtype), pltpu.HBM(land_thru.shape, land_thru.dtype)),
        in_specs=(HBM, HBM) + tuple([SEM] * (2 * N_LATE)) + (ANY,),
        out_specs=(HBM, HBM), input_output_aliases={0: 0, 1: 1},
        compiler_params=pltpu.CompilerParams(has_side_effects=pltpu.SideEffectType.DATAFLOW_SIDE_EFFECTING),
    )(w_thru, land_thru, *sems, after)[1]


N_PEERS = N_DEV - 1


def _peers(x, y, c):
    out = []
    for r in range(1, N_DEV):
        out.append((r - 1, 1 - x if r & 4 else x, 1 - y if r & 2 else y, 1 - c if r & 1 else c))
    return out


def _piece(g_ref, px, py, pc, whole):
    return g_ref if whole else g_ref.at[2 * px + py, _half(pc, g_ref.shape[1])]


def _reduce_start(g16, after, *, name, whole=False):
    def body(g_ref, land_ref, after_ref, *outs):
        sems = outs[:2 * N_PEERS]
        token = outs[2 * N_PEERS + 2]
        x, y, c = _coords()
        for k, px, py, pc in _peers(x, y, c):
            _remote(_piece(g_ref, px, py, pc, whole), land_ref.at[k], sems[k], sems[N_PEERS + k],
                    (px, py, pc)).start()
        token[...] = jnp.zeros_like(token)

    land_shape = (N_PEERS,) + (g16.shape if whole else (g16.shape[1] // 2, 1024))
    res = pl.pallas_call(
        body, name=name,
        out_shape=tuple([pltpu.SemaphoreType.DMA(())] * (2 * N_PEERS)) + (
            pltpu.HBM(g16.shape, g16.dtype), pltpu.HBM(land_shape, g16.dtype),
            jax.ShapeDtypeStruct((8, 128), F32)),
        in_specs=(HBM, HBM, ANY),
        out_specs=tuple([SEM] * (2 * N_PEERS)) + (HBM, HBM, pl.BlockSpec(memory_space=pltpu.VMEM)),
        input_output_aliases={0: 2 * N_PEERS, 1: 2 * N_PEERS + 1},
        compiler_params=pltpu.CompilerParams(has_side_effects=pltpu.SideEffectType.DATAFLOW_SIDE_EFFECTING),
    )(pltpu.with_memory_space_constraint(g16, pltpu.HBM),
      pltpu.with_memory_space_constraint(lax.empty(land_shape, g16.dtype), pltpu.HBM), after)
    return res[:2 * N_PEERS], res[2 * N_PEERS], res[2 * N_PEERS + 1], res[2 * N_PEERS + 2]


def _reduce_wait(sems, g_thru, land_thru, after, *, name, whole=False):
    def body(g_ref, land_ref, *rest):
        sems = rest[:2 * N_PEERS]
        x, y, c = _coords()
        for k, px, py, pc in _peers(x, y, c):
            cp = _remote(_piece(g_ref, px, py, pc, whole), land_ref.at[k], sems[k], sems[N_PEERS + k],
                         (px, py, pc))
            cp.wait_send()
            cp.wait_recv()

    return pl.pallas_call(
        body, name=name,
        out_shape=(pltpu.HBM(g_thru.shape, g_thru.dtype), pltpu.HBM(land_thru.shape, land_thru.dtype)),
        in_specs=(HBM, HBM) + tuple([SEM] * (2 * N_PEERS)) + tuple([ANY] * len(after)),
        out_specs=(HBM, HBM), input_output_aliases={0: 0, 1: 1},
        compiler_params=pltpu.CompilerParams(has_side_effects=pltpu.SideEffectType.DATAFLOW_SIDE_EFFECTING),
    )(g_thru, land_thru, *sems, *after)[1]


def _reduce_sum(place, g32, land, *, name):
    rows = g32.shape[1]
    nb = rows // 2 // SUM_TILE

    def body(place_ref, g_ref, l_ref, o_ref):
        acc = g_ref[...]
        for k in range(N_PEERS):
            acc = acc + l_ref[k].astype(F32)
        o_ref[...] = acc

    return pl.pallas_call(
        body, name=name,
        out_shape=jax.ShapeDtypeStruct((rows, 1024), F32),
        grid_spec=pltpu.PrefetchScalarGridSpec(
            num_scalar_prefetch=1, grid=(nb,),
            in_specs=[pl.BlockSpec((None, SUM_TILE, 1024), lambda i, s: (s[0], s[1] * nb + i, 0)),
                      pl.BlockSpec((N_PEERS, SUM_TILE, 1024), lambda i, s: (0, i, 0))],
            out_specs=pl.BlockSpec((SUM_TILE, 1024), lambda i, s: (s[1] * nb + i, 0))),
        compiler_params=_params("parallel"),
    )(place, g32, land)


def _join_halves(red, *, name):
    rows = red.shape[0]

    def body(r_ref, o_ref, ssem, rsem):
        x, y, c = _coords()
        cp = _remote(r_ref.at[_half(c, rows)], o_ref.at[_half(c, rows)], ssem, rsem, (x, y, 1 - c))
        cp.start()
        _remote(r_ref.at[_half(1 - c, rows)], o_ref.at[_half(1 - c, rows)], ssem, rsem,
                (x, y, 1 - c)).wait_recv()
        cp.wait_send()

    return pl.pallas_call(
        body, name=name,
        out_shape=jax.ShapeDtypeStruct(red.shape, F32),
        in_specs=[ANY], out_specs=ANY, input_output_aliases={0: 0},
        scratch_shapes=[pltpu.SemaphoreType.DMA, pltpu.SemaphoreType.DMA],
    )(red)


def _sum_small(me, svec, land):
    def body(me_ref, s_ref, l_ref, o_ref):
        mine = me_ref[0]
        acc = None
        for d in range(N_DEV):
            theirs = l_ref[jnp.maximum(jnp.bitwise_xor(mine, d) - 1, 0)]
            v = jnp.where(mine == d, s_ref[...], theirs)
            acc = v if acc is None else acc + v
        o_ref[...] = acc

    return pl.pallas_call(
        body, name="sum_small",
        out_shape=jax.ShapeDtypeStruct(svec.shape, F32),
        grid_spec=pltpu.PrefetchScalarGridSpec(
            num_scalar_prefetch=1, grid=(1,),
            in_specs=[pl.BlockSpec(svec.shape, lambda i, s: (0, 0)),
                      pl.BlockSpec(land.shape, lambda i, s: (0, 0, 0))],
            out_specs=pl.BlockSpec(svec.shape, lambda i, s: (0, 0))),
        compiler_params=_params("arbitrary"),
    )(me, svec, land)


def _pack_early_shards(wts):
    parts = [wts["w_in"][0].T.astype(BF16)] + [wts[n].astype(BF16).reshape(-1, 1024) for n, _ in EARLY[1:]]
    used = sum(r for _, r in EARLY)
    return jnp.concatenate(parts + [jnp.zeros((EARLY_ROWS - used, 1024), BF16)], axis=0)


def _pack_late_shards(wts):
    parts, at = [], 0
    for n, off, rows in LATE:
        if off > at:
            parts.append(jnp.zeros((off - at, 1024), BF16))
        parts.append(wts[n].astype(BF16).reshape(rows, 1024))
        at = off + rows
    return jnp.concatenate(parts, axis=0)


def _unpack_early(full):
    w = {}
    w["w_in_t"] = jnp.concatenate(
        [full[0, :704], jnp.zeros((Z_SMALL - 704, 1024), full.dtype), full[0, 704:1200]]
        + [full[p, :1200] for p in range(1, N_CHIPS)], axis=0)
    w["w_uq"] = full[:, 1200:1344].reshape(4, 384, 384).transpose(1, 0, 2).reshape(384, 1536)
    w["w_uk"] = full[:, 1344:1408].reshape(256, 1024)
    w["w_uv"] = full[:, 1408:1472].reshape(256, 1024)
    return w


def _pack_early_grads(g):
    gt = g["w_in_t"]
    w_in = jnp.stack([jnp.concatenate([gt[:704], gt[Z_SMALL:Z_SMALL + 496]], axis=0)]
                     + [gt[Z_SMALL + 496 + 1200 * (p - 1):Z_SMALL + 496 + 1200 * p] for p in range(1, N_CHIPS)])
    parts = [
        w_in,
        g["w_uq"].reshape(384, 4, 384).transpose(1, 0, 2).reshape(4, 144, 1024),
        g["w_uk"].reshape(4, 64, 1024),
        g["w_uv"].reshape(4, 64, 1024),
        jnp.zeros((4, EARLY_ROWS - sum(r for _, r in EARLY), 1024), F32),
    ]
    return jnp.concatenate(parts, axis=1)


def _pad_w_uq(w):
    w = w.reshape(Q_RANK, N_HEADS, NOPE + ROPE)
    return jnp.pad(w, ((0, 0), (0, 0), (0, QK_PAD - NOPE - ROPE))).reshape(Q_RANK, N_HEADS * QK_PAD)


def _local_step(xs, pos, tgt, W, late_weights, late_bufs, send_late_grads, send_early_grads, cw, vec):
    D = D_MODEL
    w_in_t = W["w_in_t"]
    w_uq = _pad_w_uq(W["w_uq"])
    w_ukv = jnp.concatenate([W["w_uk"], W["w_uv"]], axis=1)
    inv = ROPE_THETA ** (-jnp.arange(0, ROPE, 2, dtype=F32) / ROPE)
    inv_freq = jnp.concatenate([inv, inv, jnp.zeros((128 - ROPE,), F32)])[None, :]

    (h,) = _rowwise(lambda x, g: (_rms(x, g),), [(xs, D, 0)], [vec["norm_mix_pre"]], [(D, BF16)],
                    name="rms_pre")
    z = _mm(h, w_in_t, tb=True, name="mm_in")

    def qkv_prep(zs, p, qn, kvn, invf):
        ang = p.astype(F32) * invf
        cosv, sinv = jnp.cos(ang), jnp.sin(ang)
        ln = lax.broadcasted_iota(jnp.int32, ang.shape, 1)
        cs = jnp.where(ln < ROPE, cosv, 0.0)
        sa = jnp.where((ln >= ROPE // 2) & (ln < ROPE), sinv, 0.0)
        sb = jnp.where(ln < ROPE // 2, -sinv, 0.0)
        cqn = _rms(zs[:, :Q_RANK], qn)
        ckvn = _rms(zs[:, Q_RANK:Q_RANK + KV_RANK], kvn)
        kr = _rope(zs[:, 640:768], cs, sa, sb)
        return cqn, ckvn, kr, cs, sa, sb

    cqn, ckvn, krot, cs, sa, sb = _rowwise(
        qkv_prep, [(z, Z_SMALL, 0), (pos, 1, 0)], [vec["q_norm"], vec["kv_norm"], inv_freq],
        [(Q_RANK, BF16), (KV_RANK, BF16), (128, BF16), (128, F32), (128, F32), (128, F32)],
        name="qkv_prep")
    q_pre = _mm(cqn, w_uq, name="mm_uq")
    kv = _mm(ckvn, w_ukv, name="mm_ukv")

    def qk_rope(qp, kn, vv, kr, cs_, sa_, sb_):
        qs, ks, vs = [], [], []
        ones = jnp.ones((qp.shape[0], NOPE), F32)
        for hd in range(N_HEADS):
            qs.append(qp[:, hd * QK_PAD:hd * QK_PAD + NOPE])
            qs.append(_rope(qp[:, hd * QK_PAD + NOPE:(hd + 1) * QK_PAD], cs_, sa_, sb_))
            ks.append(kn[:, hd * NOPE:(hd + 1) * NOPE])
            ks.append(kr.astype(F32))
            vs.append(vv[:, hd * NOPE:(hd + 1) * NOPE])
            vs.append(ones)
        return (jnp.concatenate(qs, axis=1) * Q_SCALE, jnp.concatenate(ks, axis=1),
                jnp.concatenate(vs, axis=1))

    q, k, v = _rowwise(
        qk_rope, [(q_pre, 2048, 0), (kv, 1024, 0), (kv, 1024, 1), (krot, 128, 0), (cs, 128, 0),
                  (sa, 128, 0), (sb, 128, 0)], [],
        [(2048, BF16), (2048, BF16), (2048, BF16)], name="qk_rope")
    attn, lse = _flash_fwd(q, k, v)
    LW = late_weights(attn)
    y_attn = LW.fwd(attn, "w_o_attn", name="mm_o_attn")

    (u0,) = _rowwise(lambda a, b: (a * _sigmoid(b),), [(z, D, 1), (z, D, 2)], [], [(D, F32)], name="glu")
    u1 = _conv_fwd(u0, cw, vec["conv_b"])

    def ln_parts(u, g, b):
        mu = jnp.mean(u, axis=-1, keepdims=True)
        xc = u - mu
        rs = lax.rsqrt(jnp.mean(xc * xc, axis=-1, keepdims=True) + EPS)
        nh = xc * rs
        return nh, rs, nh * g + b

    def ln_silu(u, g, b):
        y = ln_parts(u, g, b)[2]
        return (y * _sigmoid(y),)

    (u3,) = _rowwise(ln_silu, [(u1, D, 0)], [vec["conv_ln_g"], vec["conv_ln_b"]], [(D, BF16)],
                     name="ln_silu")
    y_conv = LW.fwd(u3, "w_pw2", name="mm_pw2")

    def merge(ga, gc, ya, yc, b2):
        return (_sigmoid(ga) * ya + _sigmoid(gc) * (yc + b2),)

    (merged,) = _rowwise(merge, [(z, D, 3), (z, D, 4), (y_attn, D, 0), (y_conv, D, 0)], [vec["b_pw2"]],
                         [(D, BF16)], name="merge")
    mo = LW.fwd(merged, "w_out", name="mm_out")

    def post1(x, m, g1, g2):
        x1 = x + _rms(m, g1)
        return x1, _rms(x1, g2)

    x1, h2 = _rowwise(post1, [(xs, D, 0), (mo, D, 0)], [vec["norm_mix_post"], vec["norm_mlp_pre"]],
                      [(D, F32), (D, BF16)], name="post_mix")

    def sqrelu(acc):
        r = jnp.maximum(acc, 0.0)
        return r * r, r

    a2, r1 = LW.fwd(h2, "w_ff1", name="mm_ff1", epilogue=sqrelu, out_dtypes=(BF16, BF16))
    f = LW.fwd(a2, "w_ff2", name="mm_ff2")

    def post2(x1_, f_, t_, g):
        e = x1_ + _rms(f_, g) - t_
        dy = e * (1.0 / D)
        df, dg = _rms_bwd(f_, g, dy)
        loss = jnp.broadcast_to(jnp.sum(e * e, keepdims=True) * (0.5 / D), (1, 128))
        return dy, df, dg, loss

    dy, df, g_norm_mlp_post, loss = _rowwise(
        post2, [(x1, D, 0), (f, D, 0), (tgt, D, 0)], [vec["norm_mlp_post"]],
        [(D, F32), (D, BF16)], [(1, D), (1, 128)], name="post_mlp_loss")
    late_bufs = _late_grad(a2, df, "w_ff2", late_bufs, name="mm_dw_ff2")
    df1 = LW.bwd(df, "w_ff2", name="mm_df1", extras=[r1], out_dtypes=(BF16,),
              epilogue=lambda acc, r: (acc * (2.0 * r.astype(F32)),))
    late_bufs = _late_grad(h2, df1, "w_ff1", late_bufs, name="mm_dw_ff1")
    dh2 = LW.bwd(df1, "w_ff1", name="mm_dh2")

    def bwd_mid(dy_, dh2_, x1_, m, g2, g1):
        d1, dg2 = _rms_bwd(x1_, g2, dh2_)
        dx1_ = dy_ + d1
        dm, dg1 = _rms_bwd(m, g1, dx1_)
        return dx1_, dm, dg2, dg1

    dx1, dmo, g_norm_mlp_pre, g_norm_mix_post = _rowwise(
        bwd_mid, [(dy, D, 0), (dh2, D, 0), (x1, D, 0), (mo, D, 0)],
        [vec["norm_mlp_pre"], vec["norm_mix_post"]], [(D, F32), (D, BF16)], [(1, D), (1, D)],
        name="bwd_mid")
    late_bufs = _late_grad(merged, dmo, "w_out", late_bufs, name="mm_dw_out")
    dmerged = LW.bwd(dmo, "w_out", name="mm_dmerged")

    def dmerge(dm, ga, gc, ya, yc, b2):
        sga, sgc = _sigmoid(ga), _sigmoid(gc)
        dya = dm * sga
        dyc = dm * sgc
        dga = dm * ya * sga * (1.0 - sga)
        dgc = dm * (yc + b2) * sgc * (1.0 - sgc)
        return dya, dyc, dga, dgc, jnp.sum(dyc, axis=0, keepdims=True)

    dya, dyc, dga, dgc, g_b_pw2 = _rowwise(
        dmerge, [(dmerged, D, 0), (z, D, 3), (z, D, 4), (y_attn, D, 0), (y_conv, D, 0)], [vec["b_pw2"]],
        [(D, BF16), (D, BF16), (D, BF16), (D, BF16)], [(1, D)], name="dmerge")

    late_bufs = _late_grad(attn, dya, "w_o_attn", late_bufs, name="mm_dw_o_attn")
    late_bufs = _late_grad(u3, dyc, "w_pw2", late_bufs, name="mm_dw_pw2")
    late_token = send_late_grads(late_bufs)

    dattn = LW.bwd(dya, "w_o_attn", out_dtype=BF16, name="mm_dattn")

    def delta_fn(do_, o_):
        do32 = do_.astype(F32)
        pr = do32 * o_.astype(F32)
        ln = lax.broadcasted_iota(jnp.int32, (pr.shape[0], NOPE), 1)
        cols = []
        for hd in range(N_HEADS):
            dl = jnp.sum(pr[:, hd * NOPE:(hd + 1) * NOPE], axis=-1, keepdims=True)
            hi = dl.astype(BF16).astype(F32)
            cols.append(do32[:, hd * NOPE:(hd + 1) * NOPE])
            cols.append(jnp.where(ln == 0, -hi, jnp.where(ln == 1, hi - dl, 0.0)))
        return (jnp.concatenate(cols, axis=1),)

    (do_ext,) = _rowwise(delta_fn, [(dattn, D, 0), (attn, D, 0)], [], [(2048, BF16)], name="attn_delta",
                         after=[late_token])
    dq, dk, dv = _flash_bwd(q, k, v, do_ext, lse)

    def unrope(dq_, dk_, dv_, cs_, sa_, sb_):
        dq_, dk_ = dq_.astype(F32), dk_.astype(F32)
        qs, kn = [], []
        dkr = jnp.zeros_like(cs_)
        for hd in range(N_HEADS):
            qs.append(dq_[:, hd * QK_PAD:hd * QK_PAD + NOPE] * SCALE)
            qs.append(_rope_t(dq_[:, hd * QK_PAD + NOPE:(hd + 1) * QK_PAD] * SCALE, cs_, sa_, sb_))
            kn.append(dk_[:, hd * QK_PAD:hd * QK_PAD + NOPE] * K_UNSCALE)
            dkr = dkr + dk_[:, hd * QK_PAD + NOPE:(hd + 1) * QK_PAD]
        return (jnp.concatenate(qs, axis=1), jnp.concatenate(kn + [dv_], axis=1),
                _rope_t(dkr * K_UNSCALE, cs_, sa_, sb_))

    dq_pre, dkv, dkr = _rowwise(
        unrope, [(dq, 2048, 0), (dk, 2048, 0), (dv, 1024, 0), (cs, 128, 0), (sa, 128, 0), (sb, 128, 0)], [],
        [(2048, BF16), (2048, BF16), (128, F32)], name="unrope")
    g_w_uq = _mm(cqn, dq_pre, ta=True, name="mm_dw_uq")
    dcqn = _mm(dq_pre, w_uq, tb=True, name="mm_dcqn")
    g_w_ukv = _mm(ckvn, dkv, ta=True, name="mm_dw_ukv")
    dckvn = _mm(dkv, w_ukv, tb=True, name="mm_dckvn")

    def small_bwd(zs, dcq_, dckv_, dkr_, qn, kvn):
        dcq, dqn = _rms_bwd(zs[:, :Q_RANK], qn, dcq_)
        dckv, dkvn = _rms_bwd(zs[:, Q_RANK:Q_RANK + KV_RANK], kvn, dckv_)
        pad = jnp.zeros((zs.shape[0], Z_SMALL - 768), F32)
        return jnp.concatenate([dcq, dckv, dkr_, pad], axis=1), dqn, dkvn

    dz_small, g_q_norm, g_kv_norm = _rowwise(
        small_bwd, [(z, Z_SMALL, 0), (dcqn, Q_RANK, 0), (dckvn, KV_RANK, 0), (dkr, 128, 0)],
        [vec["q_norm"], vec["kv_norm"]], [(Z_SMALL, BF16)], [(1, Q_RANK), (1, KV_RANK)], name="small_bwd")

    du3 = LW.bwd(dyc, "w_pw2", name="mm_du3")

    def ln_silu_bwd(d3, u, g, b):
        nh, rs, y = ln_parts(u, g, b)
        sg = _sigmoid(y)
        dyv = d3 * (sg * (1.0 + y * (1.0 - sg)))
        dnh = dyv * g
        du = rs * (dnh - jnp.mean(dnh, axis=-1, keepdims=True)
                   - nh * jnp.mean(dnh * nh, axis=-1, keepdims=True))
        return (du, jnp.sum(dyv * nh, axis=0, keepdims=True), jnp.sum(dyv, axis=0, keepdims=True),
                jnp.sum(du, axis=0, keepdims=True))

    du1, g_ln_g, g_ln_b, g_conv_b = _rowwise(
        ln_silu_bwd, [(du3, D, 0), (u1, D, 0)], [vec["conv_ln_g"], vec["conv_ln_b"]], [(D, F32)],
        [(1, D), (1, D), (1, D)], name="ln_silu_bwd")
    du0, g_conv_w = _conv_bwd(u0, du1, cw)

    def glu_bwd(d0, a, b):
        sg = _sigmoid(b)
        return d0 * sg, d0 * a * sg * (1.0 - sg)

    dza, dzb = _rowwise(glu_bwd, [(du0, D, 0), (z, D, 1), (z, D, 2)], [], [(D, BF16), (D, BF16)],
                        name="glu_bwd")

    dz = jnp.concatenate([dz_small, dza, dzb, dga, dgc], axis=1)
    g_w_in_t = _mm(dz, h, ta=True, name="mm_dw_in")
    early_token = send_early_grads({
        "w_in_t": g_w_in_t,
        "w_uq": g_w_uq.reshape(Q_RANK, N_HEADS, QK_PAD)[:, :, :NOPE + ROPE].reshape(Q_RANK, 1536),
        "w_uk": g_w_ukv[:, :1024], "w_uv": g_w_ukv[:, 1024:]})
    dh = _mm(dz, w_in_t, name="mm_dh", after=[early_token])

    def final(dx1_, dh_, x, g):
        d, dg = _rms_bwd(x, g, dh_)
        return dx1_ + d, dg

    grad_x, g_norm_mix_pre = _rowwise(final, [(dx1, D, 0), (dh, D, 0), (xs, D, 0)], [vec["norm_mix_pre"]],
                                      [(D, F32)], [(1, D)], name="bwd_final")

    gs = {
        "norm_mix_pre": g_norm_mix_pre, "conv_b": g_conv_b, "conv_ln_g": g_ln_g, "conv_ln_b": g_ln_b,
        "b_pw2": g_b_pw2, "norm_mix_post": g_norm_mix_post, "norm_mlp_pre": g_norm_mlp_pre,
        "norm_mlp_post": g_norm_mlp_post, "q_norm": g_q_norm, "kv_norm": g_kv_norm,
    }
    return grad_x, gs, g_conv_w, loss


def _row1024(a):
    return jnp.pad(a, ((0, 0), (0, 1024 - a.shape[1])))


BIG = tuple(n for n, _ in EARLY) + tuple(n for n, _, _ in LATE)
ORDER = ("norm_mix_pre", "w_in", "q_norm", "w_uq", "kv_norm", "w_uk", "w_uv", "w_o_attn", "conv_w",
         "conv_b", "conv_ln_g", "conv_ln_b", "w_pw2", "b_pw2", "w_out", "norm_mix_post", "norm_mlp_pre",
         "w_ff1", "w_ff2", "norm_mlp_post")


def kernel(x, positions, norm_mix_pre, w_in, q_norm, w_uq, kv_norm, w_uk, w_uv, w_o_attn, conv_w, conv_b, conv_ln_g, conv_ln_b, w_pw2, b_pw2, w_out, norm_mix_post, norm_mlp_pre, w_ff1, w_ff2, norm_mlp_post, loss_target, m_norm_mix_pre, m_w_in, m_q_norm, m_w_uq, m_kv_norm, m_w_uk, m_w_uv, m_w_o_attn, m_conv_w, m_conv_b, m_conv_ln_g, m_conv_ln_b, m_w_pw2, m_b_pw2, m_w_out, m_norm_mix_post, m_norm_mlp_pre, m_w_ff1, m_w_ff2, m_norm_mlp_post, v_norm_mix_pre, v_w_in, v_q_norm, v_w_uq, v_kv_norm, v_w_uk, v_w_uv, v_w_o_attn, v_conv_w, v_conv_b, v_conv_ln_g, v_conv_ln_b, v_w_pw2, v_b_pw2, v_w_out, v_norm_mix_post, v_norm_mlp_pre, v_w_ff1, v_w_ff2, v_norm_mlp_post):
    args = dict(locals())
    wts = {n: args[n] for n in ORDER}
    mom = {n: args["m_" + n] for n in ORDER}
    var = {n: args["v_" + n] for n in ORDER}
    T = x.shape[1]
    place = jnp.stack([2 * lax.axis_index("x") + lax.axis_index("y"), lax.axis_index("c")]).astype(jnp.int32)

    w_early = _pack_early_shards(wts)
    w_late = _pack_late_shards(wts)
    cw_own = jnp.pad(conv_w.reshape(CONV_W, -1), ((0, CONV_TAPS_PAD - CONV_W), (0, 0)))
    full, cw_all = _gather_weights(w_early, cw_own)
    full = lax.dynamic_update_slice(full, w_early[None], (place[0], 0, 0))
    cw_all = lax.dynamic_update_slice(cw_all, cw_own[None], (place[0], 0, 0))
    W = _unpack_early(full)
    cw = cw_all.transpose(1, 0, 2).reshape(CONV_TAPS_PAD, D_MODEL)
    sems, w_thru, land, token = _late_gather_start(w_late, full)
    vec = {n: wts[n] for n in SMALL_VECS}
    vec["norm_mix_pre"] = vec["norm_mix_pre"] + token[:1, :1]

    def late_weights(after):
        landed = _late_gather_wait(sems, w_thru, land, after)
        return _LateWeights(lax.dynamic_update_slice(landed, w_late[None], (place[0], 0, 0)))

    sent = {}

    def send_late_grads(bufs):
        sent["late"] = (bufs[0],) + _reduce_start(bufs[1], place, name="late_grads_start")
        return sent["late"][4]

    def send_early_grads(g):
        g32 = _pack_early_grads(g)
        sent["early"] = (g32,) + _reduce_start(g32.astype(BF16), place, name="early_grads_start")
        return sent["early"][4]

    late_bufs = (lax.empty((N_CHIPS, LATE_ROWS, 1024), F32), lax.empty((N_CHIPS, LATE_ROWS, 1024), BF16))
    grad_x, gs, g_conv_w, loss = _local_step(x[0], positions.reshape(T, 1), loss_target[0], W, late_weights,
                                             late_bufs, send_late_grads, send_early_grads, cw, vec)

    def finish(group, after):
        g32, sems_, g_thru, land_, _ = sent[group]
        landed = _reduce_wait(sems_, g_thru, land_, after, name=group + "_grads_wait")
        red = _reduce_sum(place, g32, landed, name=group + "_grads_sum")
        return _join_halves(red, name=group + "_grads_join")

    def as2d(a):
        return a.reshape(-1, a.shape[-1]) if a.ndim != 4 or a.shape[2] == 1 else a.reshape(a.shape[1], -1)

    grads, delta, new_m, new_v = {}, {}, {}, {}

    def update(n):
        shp = wts[n].shape
        outs = _adamw(as2d(wts[n]), as2d(grads[n]), as2d(mom[n]), as2d(var[n]), name="adamw_" + n)
        delta[n], new_m[n], new_v[n] = (o.reshape(shp) for o in outs)

    g_late = finish("late", [sent["early"][4]])
    for n, off, rows in LATE:
        shp = wts[n].shape
        outs = _adamw(as2d(wts[n]), g_late, as2d(mom[n]), as2d(var[n]), g_row=off, name="adamw_" + n)
        delta[n], new_m[n], new_v[n], grads[n] = (o.reshape(shp) for o in outs)

    svec = jnp.concatenate(
        [_row1024(gs[n]) for n in SMALL_VECS] + [_row1024(loss)]
        + [jnp.zeros((SMALL_CONVW_ROW - SMALL_LOSS_ROW - 1, 1024), F32), g_conv_w], axis=0)
    small_sent = _reduce_start(svec, place, name="small_grads_start", whole=True)

    g_early = finish("early", [grad_x, delta["w_ff2"]])
    grads["w_in"] = g_early[:1200].T[None]
    off = 1200
    for n, rows in EARLY[1:]:
        grads[n] = g_early[off:off + rows].reshape(wts[n].shape)
        off += rows
    for n, _ in EARLY:
        update(n)

    landed = _reduce_wait(small_sent[0], small_sent[1], small_sent[2], [delta["w_in"]],
                          name="small_grads_wait", whole=True)
    ssum = _sum_small((2 * place[0] + place[1])[None], svec, landed)
    for r, n in enumerate(SMALL_VECS):
        grads[n] = ssum[r:r + 1, :wts[n].shape[1]]
    cw_sum = ssum[SMALL_CONVW_ROW:SMALL_CONVW_ROW + CONV_W]
    grads["conv_w"] = lax.dynamic_slice_in_dim(cw_sum, place[0] * 256, 256, axis=1).reshape(conv_w.shape)
    loss_out = ssum[SMALL_LOSS_ROW, 0]
    update("conv_w")
    small = [jnp.concatenate([_row1024(t[n]) for n in SMALL_VECS], axis=0) for t in (wts, grads, mom, var)]
    outs = _adamw(*small, name="adamw_small")
    for r, n in enumerate(SMALL_VECS):
        delta[n], new_m[n], new_v[n] = (o[r:r + 1, :wts[n].shape[1]] for o in outs)

    return (loss_out, grad_x[None], *[grads[n] for n in ORDER], *[delta[n] for n in ORDER],
            *[new_m[n] for n in ORDER], *[new_v[n] for n in ORDER])
```

```python
import functools

import numpy as np
import jax
import jax.numpy as jnp
from jax import lax
from jax.experimental import pallas as pl
from jax.experimental.pallas import tpu as pltpu

F32 = jnp.float32
BF16 = jnp.bfloat16
MESH = pl.DeviceIdType.MESH

D_MODEL = 1024
N_HEADS = 8
NOPE = 128
ROPE = 64
QK_PAD = 256
Q_RANK = 384
KV_RANK = 256
CONV_W = 31
CONV_TAPS_PAD = 32
HALO = 32
D_FF = 4096
EPS = 1e-6
ROPE_THETA = 10000.0
SCALE = float((NOPE + ROPE) ** -0.5)
Q_SCALE = SCALE * float(np.log2(np.e))
K_UNSCALE = float(np.log(2.0))
Z_A, Z_B, Z_GA, Z_GC, Z_S = range(5)
Z_COLS = 5 * D_MODEL
N_SMALL = Q_RANK + KV_RANK + ROPE
N_CHIPS = 4
N_DEV = 8

ADAM_LR = 0.001
ADAM_B1 = 0.9
ADAM_B2 = 0.999
ADAM_EPS = 1e-08
ADAM_WD = 0.01
ADAM_STEP = 10

VMEM_LIMIT = 48 * 1024 * 1024
NEG = -1e30

EARLY = (("w_in", 1200), ("w_uq", 144), ("w_uk", 64), ("w_uv", 64))
EARLY_ROWS = 1536
LATE = (("w_ff1", 0, 1024), ("w_o_attn", 1024, 256), ("w_pw2", 1536, 256), ("w_out", 1792, 256),
        ("w_ff2", 2048, 1024))
LATE_OFF = {n: off for n, off, _ in LATE}
LATE_ROWS = 3072
SUM_TILE = 256

SMALL_VECS = ("norm_mix_pre", "conv_b", "conv_ln_g", "conv_ln_b", "b_pw2", "norm_mix_post",
              "norm_mlp_pre", "norm_mlp_post", "q_norm", "kv_norm")
SMALL_LOSS_ROW = 10
SMALL_CONVW_ROW = 16
SMALL_ROWS = 48


ANY = pl.BlockSpec(memory_space=pl.ANY)


def _params(*sem):
    return pltpu.CompilerParams(dimension_semantics=sem, vmem_limit_bytes=VMEM_LIMIT)


def _tile(n, cap):
    if n <= cap:
        return n
    t = (cap // 128) * 128
    while n % t:
        t -= 128
    return t


def _row_tile(n, cap):
    if n <= cap:
        return n
    t = (cap // 8) * 8
    while n % t:
        t -= 8
    return t


MM_VMEM_BUDGET = 40 * 1024 * 1024


def _mm_tiles(M, N, K, in_bytes, out_bytes):
    tn = _tile(N, 1024)
    tk = K
    while True:
        for cap in (1024, 512, 256):
            tm = _tile(M, cap)
            need = 2 * tk * (tm * in_bytes[0] + tn * in_bytes[1]) + 2 * tm * tn * out_bytes
            need += tm * tn * 4 if tk < K else 0
            if need <= MM_VMEM_BUDGET:
                return tm, tn, tk
        tk = _tile(K, tk - 128)


def _mm(a, b, *, name, ta=False, tb=False, out_dtype=F32, extras=(), epilogue=None, out_dtypes=None,
        after=(), shape=None, tiles=None, b_spec=None, into=None):
    if shape is None:
        M, K = (a.shape[1], a.shape[0]) if ta else a.shape
        N = b.shape[0] if tb else b.shape[1]
        assert K == (b.shape[1] if tb else b.shape[0]), (a.shape, b.shape, ta, tb)
    else:
        M, N, K = shape
    out_dtypes = tuple(out_dtypes or (out_dtype,))
    if into is not None:
        out_dtypes = tuple(buf.dtype for buf, _ in into)
    if tiles is None:
        out_bytes = sum(jnp.dtype(d).itemsize for d in out_dtypes) + sum(e.dtype.itemsize for e in extras)
        tiles = _mm_tiles(M, N, K, (a.dtype.itemsize, b.dtype.itemsize), out_bytes)
    tm, tn, tk = tiles
    nk = K // tk
    dims = (((0 if ta else 1,), (1 if tb else 0,)), ((), ()))
    n_ex, n_out = len(extras), len(out_dtypes)
    n_pass = len(after) + (n_out if into is not None else 0)

    def body(*refs):
        a_ref, b_ref = refs[:2]
        ex_refs = refs[2:2 + n_ex]
        o_refs = refs[2 + n_ex + n_pass:2 + n_ex + n_pass + n_out]
        k = pl.program_id(2)
        bv = b_ref[...]
        if bv.ndim == 3:
            bv = bv.reshape(-1, bv.shape[-1])
        part = lax.dot_general(a_ref[...].astype(BF16), bv.astype(BF16), dims, preferred_element_type=F32)

        def finish(acc):
            res = epilogue(acc, *[r[...] for r in ex_refs]) if epilogue else (acc,) * n_out
            for o_ref, r in zip(o_refs, res):
                o_ref[...] = r.astype(o_ref.dtype)

        if nk == 1:
            finish(part)
        else:
            acc_ref = refs[-1]

            @pl.when(k == 0)
            def _():
                acc_ref[...] = part

            @pl.when((k > 0) & (k < nk - 1))
            def _():
                acc_ref[...] += part

            @pl.when(k == nk - 1)
            def _():
                finish(acc_ref[...] + part)

    a_spec = (pl.BlockSpec((tk, tm), lambda i, j, k: (k, i)) if ta
              else pl.BlockSpec((tm, tk), lambda i, j, k: (i, k)))
    if b_spec is None:
        b_spec = (pl.BlockSpec((tn, tk), lambda i, j, k: (j, k)) if tb
                  else pl.BlockSpec((tk, tn), lambda i, j, k: (k, j)))
    o_spec = pl.BlockSpec((tm, tn), lambda i, j, k: (i, j))
    if into is None:
        out_shape = [jax.ShapeDtypeStruct((M, N), dt) for dt in out_dtypes]
        out_specs, passed, aliases = [o_spec] * n_out, list(after), {}
    else:
        out_shape = [jax.ShapeDtypeStruct(buf.shape, buf.dtype) for buf, _ in into]
        out_specs = [spec for _, spec in into]
        passed = list(after) + [buf for buf, _ in into]
        aliases = {2 + n_ex + len(after) + o: o for o in range(n_out)}
    res = pl.pallas_call(
        body, name=name, out_shape=out_shape,
        grid=(M // tm, N // tn, nk),
        in_specs=[a_spec, b_spec] + [o_spec] * n_ex + [ANY] * len(passed),
        out_specs=out_specs, input_output_aliases=aliases,
        scratch_shapes=[pltpu.VMEM((tm, tn), F32)] if nk > 1 else [],
        compiler_params=_params("parallel", "parallel", "arbitrary"),
    )(a, b, *extras, *passed)
    return res[0] if n_out == 1 else res


class _LateWeights:
    def __init__(self, pack):
        self.pack = pack

    def fwd(self, x, n, *, name, **kw):
        T, off = x.shape[0], LATE_OFF[n]
        tm = _tile(T, 512)
        if n == "w_ff1":
            return _mm(x, self.pack, name=name, shape=(T, 4096, 1024), tiles=(_tile(T, 1024), 1024, 1024),
                       b_spec=pl.BlockSpec((None, 1024, 1024), lambda i, j, k: (j, 0, 0)), **kw)
        if n == "w_ff2":
            view = self.pack.reshape(N_CHIPS, LATE_ROWS // 1024, 1024, 1024)
            return _mm(x, view, name=name, shape=(T, 1024, 4096), tiles=(tm, 1024, 4096),
                       b_spec=pl.BlockSpec((N_CHIPS, None, 1024, 1024), lambda i, j, k: (0, off // 1024, 0, 0)),
                       **kw)
        return _mm(x, self.pack.reshape(N_CHIPS, LATE_ROWS // 256, 256, 1024), name=name,
                   shape=(T, 1024, 1024), tiles=(_tile(T, 1024), 1024, 1024), b_spec=self._whole(off), **kw)

    @staticmethod
    def _whole(off):
        return pl.BlockSpec((N_CHIPS, None, 256, 1024), lambda i, j, k: (0, off // 256, 0, 0))

    def bwd(self, dy, n, *, name, **kw):
        T, off = dy.shape[0], LATE_OFF[n]
        tm = _tile(T, 1024)
        if n == "w_ff1":
            return _mm(dy, self.pack, tb=True, name=name, shape=(T, 1024, 4096), tiles=(tm, 1024, 1024),
                       b_spec=pl.BlockSpec((None, 1024, 1024), lambda i, j, k: (k, 0, 0)), **kw)
        if n == "w_ff2":
            return _mm(dy, self.pack, tb=True, name=name, shape=(T, 4096, 1024), tiles=(tm, 1024, 1024),
                       b_spec=pl.BlockSpec((None, 1024, 1024), lambda i, j, k: (j, off // 1024, 0)), **kw)
        return _mm(dy, self.pack.reshape(N_CHIPS, LATE_ROWS // 256, 256, 1024), tb=True, name=name,
                   shape=(T, 1024, 1024), tiles=(tm, 1024, 1024), b_spec=self._whole(off), **kw)


def _late_grad(x, dy, n, bufs, *, name):
    T, off = x.shape[0], LATE_OFF[n]
    if n == "w_ff1":
        spec = pl.BlockSpec((None, 512, 1024), lambda i, j, k: (j, i, 0))
        shape, tiles = (1024, 4096, T), (512, 1024, T)
    elif n == "w_ff2":
        spec = pl.BlockSpec((None, 512, 1024), lambda i, j, k: (i // 2, off // 512 + i % 2, 0))
        shape, tiles = (4096, 1024, T), (512, 1024, T)
    else:
        spec = pl.BlockSpec((None, 256, 1024), lambda i, j, k: (i, off // 256, 0))
        shape, tiles = (1024, 1024, T), (256, 1024, T)
    return _mm(x, dy, ta=True, name=name, shape=shape, tiles=tiles, into=[(b, spec) for b in bufs])


def _rowwise(fn, rows, consts, outs, reds=(), *, name, tm=256, after=()):
    T = rows[0][0].shape[0]
    tm = min(tm, T)
    n_row, n_const, n_out = len(rows), len(consts), len(outs)
    placed = [o for o in outs if len(o) == 3]
    n_pass = len(after) + len(placed)

    def body(*refs):
        i = pl.program_id(0)
        vals = [r[...] for r in refs[:n_row + n_const]]
        vals = [v.astype(F32) if v.dtype == BF16 else v for v in vals]
        res = fn(*vals)
        out_refs = refs[n_row + n_const + n_pass:]
        for k in range(n_out):
            out_refs[k][...] = res[k].astype(out_refs[k].dtype)
        for k in range(len(reds)):
            ref = out_refs[n_out + k]

            @pl.when(i == 0)
            def _(ref=ref):
                ref[...] = jnp.zeros_like(ref)

            ref[...] += res[n_out + k]

    in_specs = [pl.BlockSpec((tm, w), functools.partial(lambda i, cb: (i, cb), cb=cb))
                for (_, w, cb) in rows]
    in_specs += [pl.BlockSpec(c.shape, lambda i: (0, 0)) for c in consts]
    in_specs += [ANY] * n_pass
    out_specs, out_shape, aliases = [], [], {}
    for o, out in enumerate(outs):
        if len(out) == 3:
            w, buf, cb = out
            out_specs.append(pl.BlockSpec((tm, w), functools.partial(lambda i, cb: (i, cb), cb=cb)))
            out_shape.append(jax.ShapeDtypeStruct(buf.shape, buf.dtype))
            aliases[n_row + n_const + len(after) + len(aliases)] = o
        else:
            out_specs.append(pl.BlockSpec((tm, out[0]), lambda i: (i, 0)))
            out_shape.append(jax.ShapeDtypeStruct((T, out[0]), out[1]))
    out_specs += [pl.BlockSpec(s, lambda i: (0, 0)) for s in reds]
    out_shape += [jax.ShapeDtypeStruct(s, F32) for s in reds]
    return pl.pallas_call(
        body, name=name, out_shape=out_shape, grid=(T // tm,),
        in_specs=in_specs, out_specs=out_specs, input_output_aliases=aliases,
        compiler_params=_params("arbitrary"),
    )(*[r[0] for r in rows], *consts, *after, *[o[1] for o in placed])


def _rms(x, g):
    r = lax.rsqrt(jnp.mean(x * x, axis=-1, keepdims=True) + EPS)
    return x * r * g


def _rms_bwd(x, g, dy):
    r = lax.rsqrt(jnp.mean(x * x, axis=-1, keepdims=True) + EPS)
    n = x * r
    dyg = dy * g
    dx = r * (dyg - n * jnp.mean(dyg * n, axis=-1, keepdims=True))
    return dx, jnp.sum(dy * n, axis=0, keepdims=True)


def _sigmoid(x):
    return 1.0 / (1.0 + jnp.exp(-x))


def _rope(x, cs, sa, sb):
    return x * cs + pltpu.roll(x, 32, 1) * sa + pltpu.roll(x, 96, 1) * sb


def _rope_t(dy, cs, sa, sb):
    return dy * cs + pltpu.roll(dy * sa, 96, 1) + pltpu.roll(dy * sb, 32, 1)


def _causal_mask(t):
    row = lax.broadcasted_iota(jnp.int32, (t, t), 0)
    col = lax.broadcasted_iota(jnp.int32, (t, t), 1)
    return col <= row


FLASH_FWD_TQ = 4096
FLASH_FWD_TK = 1024
FLASH_FWD_HEADS = 1
FLASH_BWD_TQ = 1024
FLASH_BWD_TK = 512
NT_DIMS = (((1,), (1,)), ((), ()))
TN_DIMS = (((0,), (0,)), ((), ()))


def _flash_fwd(q, k, v, *, tq=FLASH_FWD_TQ, tk=FLASH_FWD_TK, hp=FLASH_FWD_HEADS):
    T = q.shape[0]
    tq, tk = min(tq, T), min(tk, T)
    n_diag = tq // tk
    assert n_diag >= 1 and N_HEADS % hp == 0
    heads = range(hp)

    def body(q_ref, k_ref, v_ref, o_ref, lse_ref):
        i = pl.program_id(1)

        def cols(ref, rows, h, width=QK_PAD):
            return ref[rows, h * width:(h + 1) * width]

        def logits(j, r0=0):
            sl = pl.ds(pl.multiple_of(j * tk, tk), tk)
            return tuple(lax.dot_general(cols(q_ref, slice(r0, tq), h), cols(k_ref, sl, h), NT_DIMS,
                                         preferred_element_type=F32) for h in heads)

        def update(j, s, m, acc, masked, r0=0):
            sl = pl.ds(pl.multiple_of(j * tk, tk), tk)
            m_out, acc_out = [], []
            for h in heads:
                sh = s[h]
                if masked:
                    row = i * tq + r0 + lax.broadcasted_iota(jnp.int32, (tq - r0, tk), 0)
                    col = j * tk + lax.broadcasted_iota(jnp.int32, (tq - r0, tk), 1)
                    sh = jnp.where(col <= row, sh, NEG)
                m_old, acc_old = m[h][r0:], acc[h][r0:]
                m_new = jnp.maximum(m_old, jnp.max(sh, axis=-1, keepdims=True))
                p = jnp.exp2(sh - m_new)
                acc_new = (jnp.exp2(m_old - m_new) * acc_old
                           + jnp.dot(p.astype(BF16), cols(v_ref, sl, h), preferred_element_type=F32))
                if r0:
                    m_new = jnp.concatenate([m[h][:r0], m_new], axis=0)
                    acc_new = jnp.concatenate([acc[h][:r0], acc_new], axis=0)
                m_out.append(m_new)
                acc_out.append(acc_new)
            return tuple(m_out), tuple(acc_out)

        def step(j, carry):
            s, m, acc = carry
            s_next = logits(j + 1)
            m, acc = update(j, s, m, acc, False)
            return s_next, m, acc

        first = i * n_diag
        init = (logits(0), tuple(jnp.full((tq, 1), NEG, F32) for _ in heads),
                tuple(jnp.zeros((tq, QK_PAD), F32) for _ in heads))
        s, m, acc = lax.fori_loop(0, first, step, init)
        for d in range(n_diag):
            m, acc = update(first + d, s, m, acc, True, r0=d * tk)
            if d + 1 < n_diag:
                s = logits(first + d + 1, r0=(d + 1) * tk)
        for h in heads:
            l = acc[h][:, NOPE:]
            o_ref[:, h * NOPE:(h + 1) * NOPE] = (acc[h][:, :NOPE] / l).astype(BF16)
            lse_ref[:, h * NOPE:(h + 1) * NOPE] = m[h] + jnp.log2(l)

    return pl.pallas_call(
        body, name="flash_fwd",
        out_shape=[jax.ShapeDtypeStruct((T, N_HEADS * NOPE), BF16),
                   jax.ShapeDtypeStruct((T, N_HEADS * NOPE), F32)],
        grid=(N_HEADS // hp, T // tq),
        in_specs=[pl.BlockSpec((tq, hp * QK_PAD), lambda h, i: (i, h)),
                  pl.BlockSpec((T, hp * QK_PAD), lambda h, i: (0, h)),
                  pl.BlockSpec((T, hp * QK_PAD), lambda h, i: (0, h))],
        out_specs=[pl.BlockSpec((tq, hp * NOPE), lambda h, i: (i, h)),
                   pl.BlockSpec((tq, hp * NOPE), lambda h, i: (i, h))],
        compiler_params=_params("parallel", "arbitrary"),
    )(q, k, v)


def _flash_bwd(q, k, v, do, lse, *, tq=FLASH_BWD_TQ, tk=FLASH_BWD_TK):
    T = q.shape[0]
    tq, tk = min(tq, T), min(tk, T)

    def body(q_ref, k_ref, v_ref, do_ref, lse_ref, dq_ref, dk_ref, dv_ref, dq_acc):
        dq_acc[...] = jnp.zeros_like(dq_acc)
        for j in range(T // tk):
            k0 = j * tk
            kj = k_ref[k0:k0 + tk, :]
            vj = v_ref[k0:k0 + tk, :]
            dk = jnp.zeros((tk, QK_PAD), F32)
            dv = jnp.zeros((tk, NOPE), F32)
            for i in range(k0 // tq, T // tq):
                r0 = max(i * tq, k0)
                r1 = (i + 1) * tq
                qi = q_ref[r0:r1, :]
                doi = do_ref[r0:r1, :]
                s = lax.dot_general(qi, kj, NT_DIMS, preferred_element_type=F32)
                p = jnp.exp2(s - lse_ref[r0:r1, :][:, :1])
                if r0 < k0 + tk:
                    row = r0 + lax.broadcasted_iota(jnp.int32, (r1 - r0, tk), 0)
                    col = k0 + lax.broadcasted_iota(jnp.int32, (r1 - r0, tk), 1)
                    p = jnp.where(col <= row, p, 0.0)
                dv = dv + lax.dot_general(p.astype(BF16), doi[:, :NOPE], TN_DIMS, preferred_element_type=F32)
                ds = (p * lax.dot_general(doi, vj, NT_DIMS, preferred_element_type=F32)).astype(BF16)
                dk = dk + lax.dot_general(ds, qi, TN_DIMS, preferred_element_type=F32)
                dq_acc[r0:r1, :] += jnp.dot(ds, kj, preferred_element_type=F32)
            dk_ref[k0:k0 + tk, :] = dk.astype(BF16)
            dv_ref[k0:k0 + tk, :] = dv.astype(BF16)
        dq_ref[...] = dq_acc[...].astype(BF16)

    return pl.pallas_call(
        body, name="flash_bwd",
        out_shape=[jax.ShapeDtypeStruct((T, N_HEADS * QK_PAD), BF16),
                   jax.ShapeDtypeStruct((T, N_HEADS * QK_PAD), BF16),
                   jax.ShapeDtypeStruct((T, N_HEADS * NOPE), BF16)],
        grid=(N_HEADS,),
        in_specs=[pl.BlockSpec((T, QK_PAD), lambda h: (0, h)),
                  pl.BlockSpec((T, QK_PAD), lambda h: (0, h)),
                  pl.BlockSpec((T, QK_PAD), lambda h: (0, h)),
                  pl.BlockSpec((T, QK_PAD), lambda h: (0, h)),
                  pl.BlockSpec((T, NOPE), lambda h: (0, h))],
        out_specs=[pl.BlockSpec((T, QK_PAD), lambda h: (0, h)),
                   pl.BlockSpec((T, QK_PAD), lambda h: (0, h)),
                   pl.BlockSpec((T, NOPE), lambda h: (0, h))],
        scratch_shapes=[pltpu.VMEM((T, QK_PAD), F32)],
        compiler_params=_params("arbitrary"),
    )(q, k, v, do, lse)


SUB = 128


def _windows(buf, t0, c0, first, last):
    span = SUB + 8 * ((last + 7) // 8)
    base = buf[t0:t0 + span, c0:c0 + SUB]
    for r in range(8):
        offs = [o for o in range(first, last + 1) if o % 8 == r]
        if offs:
            win = base if r == 0 else pltpu.roll(base, span - r, 0)
            for o in offs:
                yield o, win[o - r:o - r + SUB, :]


def _conv_fwd(u, w, bias, *, tm=256, ct=256):
    T, C = u.shape
    tm = min(tm, T)
    hb = tm // HALO
    lead = HALO - (CONV_W - 1)

    def body(cur_ref, halo_ref, w_ref, b_ref, o_ref, buf):
        t = pl.program_id(1)
        buf[0:HALO, :] = jnp.where(t > 0, halo_ref[...], 0.0)
        buf[HALO:, :] = cur_ref[...]
        for c0 in range(0, ct, SUB):
            for t0 in range(0, tm, SUB):
                acc = jnp.broadcast_to(b_ref[:, c0:c0 + SUB], (SUB, SUB))
                for o, win in _windows(buf, t0, c0, lead, lead + CONV_W - 1):
                    acc = acc + win * w_ref[o - lead:o - lead + 1, c0:c0 + SUB]
                o_ref[t0:t0 + SUB, c0:c0 + SUB] = acc

    return pl.pallas_call(
        body, name="conv_fwd",
        out_shape=jax.ShapeDtypeStruct((T, C), F32),
        grid=(C // ct, T // tm),
        in_specs=[pl.BlockSpec((tm, ct), lambda c, t: (t, c)),
                  pl.BlockSpec((HALO, ct), lambda c, t: (jnp.maximum(t * hb - 1, 0), c)),
                  pl.BlockSpec((CONV_TAPS_PAD, ct), lambda c, t: (0, c)),
                  pl.BlockSpec((1, ct), lambda c, t: (0, c))],
        out_specs=pl.BlockSpec((tm, ct), lambda c, t: (t, c)),
        scratch_shapes=[pltpu.VMEM((HALO + tm, ct), F32)],
        compiler_params=_params("parallel", "arbitrary"),
    )(u, u, w, bias)


def _conv_bwd(u, du1, w, *, tm=256, ct=256):
    T, C = u.shape
    tm = min(tm, T)
    hb = tm // HALO
    nt = T // tm
    last_halo = T // HALO - 1
    lead = HALO - (CONV_W - 1)

    def body(u_ref, uh_ref, d_ref, dh_ref, w_ref, du_ref, dw_ref, ubuf, dbuf, dw8):
        t = pl.program_id(1)
        ubuf[0:HALO, :] = jnp.where(t > 0, uh_ref[...], 0.0)
        ubuf[HALO:, :] = u_ref[...]
        dbuf[0:tm, :] = d_ref[...]
        dbuf[tm:, :] = jnp.where(t < nt - 1, dh_ref[...], 0.0)

        @pl.when(t == 0)
        def _():
            dw8[...] = jnp.zeros_like(dw8)

        for c0 in range(0, ct, SUB):
            for t0 in range(0, tm, SUB):
                acc = jnp.zeros((SUB, SUB), F32)
                for o, win in _windows(dbuf, t0, c0, 0, CONV_W - 1):
                    kk = CONV_W - 1 - o
                    acc = acc + win * w_ref[kk:kk + 1, c0:c0 + SUB]
                du_ref[t0:t0 + SUB, c0:c0 + SUB] = acc
                dsub = d_ref[t0:t0 + SUB, c0:c0 + SUB]
                for o, win in _windows(ubuf, t0, c0, lead, lead + CONV_W - 1):
                    kk = o - lead
                    dw8[8 * kk:8 * kk + 8, c0:c0 + SUB] += jnp.sum(
                        (dsub * win).reshape(SUB // 8, 8, SUB), axis=0)

        @pl.when(t == nt - 1)
        def _():
            dw_ref[...] = jnp.sum(dw8[...].reshape(CONV_TAPS_PAD, 8, ct), axis=1)

    return pl.pallas_call(
        body, name="conv_bwd",
        out_shape=[jax.ShapeDtypeStruct((T, C), F32), jax.ShapeDtypeStruct((CONV_TAPS_PAD, C), F32)],
        grid=(C // ct, nt),
        in_specs=[pl.BlockSpec((tm, ct), lambda c, t: (t, c)),
                  pl.BlockSpec((HALO, ct), lambda c, t: (jnp.maximum(t * hb - 1, 0), c)),
                  pl.BlockSpec((tm, ct), lambda c, t: (t, c)),
                  pl.BlockSpec((HALO, ct), lambda c, t: (jnp.minimum((t + 1) * hb, last_halo), c)),
                  pl.BlockSpec((CONV_TAPS_PAD, ct), lambda c, t: (0, c))],
        out_specs=[pl.BlockSpec((tm, ct), lambda c, t: (t, c)),
                   pl.BlockSpec((CONV_TAPS_PAD, ct), lambda c, t: (0, c))],
        scratch_shapes=[pltpu.VMEM((HALO + tm, ct), F32), pltpu.VMEM((tm + HALO, ct), F32),
                        pltpu.VMEM((CONV_TAPS_PAD * 8, ct), F32)],
        compiler_params=_params("parallel", "arbitrary"),
    )(u, u, du1, du1, w)


def _adamw(w, g, m, v, *, name, g_row=None):
    R, C = w.shape
    tr = _row_tile(R, max(8, (1 << 18) // C // 8 * 8))
    first = 0 if g_row is None else g_row // tr
    assert g_row is None or g_row % tr == 0

    def body(w_ref, g_ref, m_ref, v_ref, d_ref, nm_ref, nv_ref, *g_out):
        gv = g_ref[...]
        nm = ADAM_B1 * m_ref[...] + (1.0 - ADAM_B1) * gv
        nv = ADAM_B2 * v_ref[...] + (1.0 - ADAM_B2) * (gv * gv)
        m_hat = nm / (1.0 - ADAM_B1 ** ADAM_STEP)
        v_hat = nv / (1.0 - ADAM_B2 ** ADAM_STEP)
        d_ref[...] = -ADAM_LR * (m_hat / (jnp.sqrt(v_hat) + ADAM_EPS) + ADAM_WD * w_ref[...])
        nm_ref[...] = nm
        nv_ref[...] = nv
        for ref in g_out:
            ref[...] = gv

    spec = pl.BlockSpec((tr, C), lambda i: (i, 0))
    g_spec = pl.BlockSpec((tr, C), lambda i: (first + i, 0))
    n_out = 3 if g_row is None else 4
    return pl.pallas_call(
        body, name=name, out_shape=[jax.ShapeDtypeStruct((R, C), F32)] * n_out,
        grid=(R // tr,), in_specs=[spec, g_spec, spec, spec], out_specs=[spec] * n_out,
        compiler_params=_params("parallel"),
    )(w, g, m, v)


def _coords():
    return lax.axis_index("x"), lax.axis_index("y"), lax.axis_index("c")


def _remote(src, dst, ssem, rsem, to):
    return pltpu.make_async_remote_copy(src_ref=src, dst_ref=dst, send_sem=ssem, recv_sem=rsem,
                                        device_id=to, device_id_type=MESH)


def _half(c, rows):
    return pl.ds(pl.multiple_of(c * (rows // 2), 16), rows // 2)


def _gather_weights(wpack, cw):
    def body(w_ref, cw_ref, full_ref, cwf_ref, ssem, rsem):
        x, y, c = _coords()
        q = 2 * x + y
        sib = (x, y, 1 - c)
        chips = [(1 - x, y), (x, 1 - y), (1 - x, 1 - y)]
        mine, other = _half(c, w_ref.shape[0]), _half(1 - c, w_ref.shape[0])
        first = []
        for j, (px, py) in enumerate(chips):
            first.append(_remote(w_ref.at[mine], full_ref.at[q, mine], ssem.at[j], rsem.at[j], (px, py, c)))
            first.append(_remote(cw_ref, cwf_ref.at[q], ssem.at[6 + j], rsem.at[6 + j], (px, py, c)))
        for cp in first:
            cp.start()
        passed = []
        for j, (px, py) in enumerate(chips):
            pq = 2 * px + py
            _remote(w_ref.at[mine], full_ref.at[pq, mine], ssem.at[j], rsem.at[j], (px, py, c)).wait_recv()
            fw = _remote(full_ref.at[pq, mine], full_ref.at[pq, mine], ssem.at[3 + j], rsem.at[3 + j], sib)
            fw.start()
            passed.append(fw)
        for j, (px, py) in enumerate(chips):
            pq = 2 * px + py
            _remote(full_ref.at[pq, other], full_ref.at[pq, other], ssem.at[3 + j], rsem.at[3 + j], sib).wait_recv()
            _remote(cw_ref, cwf_ref.at[pq], ssem.at[6 + j], rsem.at[6 + j], (px, py, c)).wait_recv()
        for cp in first + passed:
            cp.wait_send()

    return pl.pallas_call(
        body, name="gather_weights",
        out_shape=[jax.ShapeDtypeStruct((N_CHIPS,) + wpack.shape, wpack.dtype),
                   jax.ShapeDtypeStruct((N_CHIPS,) + cw.shape, cw.dtype)],
        in_specs=[ANY, ANY], out_specs=[ANY, ANY],
        scratch_shapes=[pltpu.SemaphoreType.DMA((9,)), pltpu.SemaphoreType.DMA((9,))],
    )(wpack, cw)


SEM = pl.BlockSpec(memory_space=pltpu.SEMAPHORE)
HBM = pl.BlockSpec(memory_space=pltpu.HBM)
N_LATE = 6


def _late_peers(x, y, c):
    out = []
    for j, (px, py) in enumerate([(1 - x, y), (x, 1 - y), (1 - x, 1 - y)]):
        for t in (0, 1):
            out.append((2 * j + t, px, py, c if t == 0 else 1 - c))
    return out


def _late_gather_start(w, after):
    rows = w.shape[0]

    def body(w_ref, land_ref, after_ref, *outs):
        sems = outs[:2 * N_LATE]
        token = outs[2 * N_LATE + 2]
        x, y, c = _coords()
        q = 2 * x + y
        mine = _half(c, rows)
        for k, px, py, pc in _late_peers(x, y, c):
            _remote(w_ref.at[mine], land_ref.at[q, mine], sems[k], sems[N_LATE + k], (px, py, pc)).start()
        token[...] = jnp.zeros_like(token)

    res = pl.pallas_call(
        body, name="late_gather_start",
        out_shape=tuple([pltpu.SemaphoreType.DMA(())] * (2 * N_LATE)) + (
            pltpu.HBM(w.shape, w.dtype), pltpu.HBM((N_CHIPS,) + w.shape, w.dtype),
            jax.ShapeDtypeStruct((8, 128), F32)),
        in_specs=(HBM, HBM, ANY),
        out_specs=tuple([SEM] * (2 * N_LATE)) + (HBM, HBM, pl.BlockSpec(memory_space=pltpu.VMEM)),
        input_output_aliases={0: 2 * N_LATE, 1: 2 * N_LATE + 1},
        compiler_params=pltpu.CompilerParams(has_side_effects=pltpu.SideEffectType.DATAFLOW_SIDE_EFFECTING),
    )(pltpu.with_memory_space_constraint(w, pltpu.HBM),
      pltpu.with_memory_space_constraint(lax.empty((N_CHIPS,) + w.shape, w.dtype), pltpu.HBM), after)
    return res[:2 * N_LATE], res[2 * N_LATE], res[2 * N_LATE + 1], res[2 * N_LATE + 2]


def _late_gather_wait(sems, w_thru, land_thru, after):
    rows = w_thru.shape[0]

    def body(w_ref, land_ref, *rest):
        sems = rest[:2 * N_LATE]
        x, y, c = _coords()
        for k, px, py, pc in _late_peers(x, y, c):
            cp = _remote(w_ref.at[_half(pc, rows)], land_ref.at[2 * px + py, _half(pc, rows)],
                         sems[k], sems[N_LATE + k], (px, py, pc))
            cp.wait_send()
            cp.wait_recv()

    return pl.pallas_call(
        body, name="late_gather_wait",
        out_shape=(pltpu.HBM(w_thru.shape, w_thru.dtype), pltpu.HBM(land_thru.shape, land_thru.dtype)),
        in_specs=(HBM, HBM) + tuple([SEM] * (2 * N_LATE)) + (ANY,),
        out_specs=(HBM, HBM), input_output_aliases={0: 0, 1: 1},
        compiler_params=pltpu.CompilerParams(has_side_effects=pltpu.SideEffectType.DATAFLOW_SIDE_EFFECTING),
    )(w_thru, land_thru, *sems, after)[1]


N_PEERS = N_DEV - 1


def _peers(x, y, c):
    out = []
    for r in range(1, N_DEV):
        out.append((r - 1, 1 - x if r & 4 else x, 1 - y if r & 2 else y, 1 - c if r & 1 else c))
    return out


def _piece(g_ref, px, py, pc, whole):
    return g_ref if whole else g_ref.at[2 * px + py, _half(pc, g_ref.shape[1])]


def _reduce_start(g16, after, *, name, whole=False):
    def body(g_ref, land_ref, after_ref, *outs):
        sems = outs[:2 * N_PEERS]
        token = outs[2 * N_PEERS + 2]
        x, y, c = _coords()
        for k, px, py, pc in _peers(x, y, c):
            _remote(_piece(g_ref, px, py, pc, whole), land_ref.at[k], sems[k], sems[N_PEERS + k],
                    (px, py, pc)).start()
        token[...] = jnp.zeros_like(token)

    land_shape = (N_PEERS,) + (g16.shape if whole else (g16.shape[1] // 2, 1024))
    res = pl.pallas_call(
        body, name=name,
        out_shape=tuple([pltpu.SemaphoreType.DMA(())] * (2 * N_PEERS)) + (
            pltpu.HBM(g16.shape, g16.dtype), pltpu.HBM(land_shape, g16.dtype),
            jax.ShapeDtypeStruct((8, 128), F32)),
        in_specs=(HBM, HBM, ANY),
        out_specs=tuple([SEM] * (2 * N_PEERS)) + (HBM, HBM, pl.BlockSpec(memory_space=pltpu.VMEM)),
        input_output_aliases={0: 2 * N_PEERS, 1: 2 * N_PEERS + 1},
        compiler_params=pltpu.CompilerParams(has_side_effects=pltpu.SideEffectType.DATAFLOW_SIDE_EFFECTING),
    )(pltpu.with_memory_space_constraint(g16, pltpu.HBM),
      pltpu.with_memory_space_constraint(lax.empty(land_shape, g16.dtype), pltpu.HBM), after)
    return res[:2 * N_PEERS], res[2 * N_PEERS], res[2 * N_PEERS + 1], res[2 * N_PEERS + 2]


def _reduce_wait(sems, g_thru, land_thru, after, *, name, whole=False):
    def body(g_ref, land_ref, *rest):
        sems = rest[:2 * N_PEERS]
        x, y, c = _coords()
        for k, px, py, pc in _peers(x, y, c):
            cp = _remote(_piece(g_ref, px, py, pc, whole), land_ref.at[k], sems[k], sems[N_PEERS + k],
                         (px, py, pc))
            cp.wait_send()
            cp.wait_recv()

    return pl.pallas_call(
        body, name=name,
        out_shape=(pltpu.HBM(g_thru.shape, g_thru.dtype), pltpu.HBM(land_thru.shape, land_thru.dtype)),
        in_specs=(HBM, HBM) + tuple([SEM] * (2 * N_PEERS)) + tuple([ANY] * len(after)),
        out_specs=(HBM, HBM), input_output_aliases={0: 0, 1: 1},
        compiler_params=pltpu.CompilerParams(has_side_effects=pltpu.SideEffectType.DATAFLOW_SIDE_EFFECTING),
    )(g_thru, land_thru, *sems, *after)[1]


def _reduce_sum(place, g32, land, *, name):
    rows = g32.shape[1]
    nb = rows // 2 // SUM_TILE

    def body(place_ref, g_ref, l_ref, o_ref):
        acc = g_ref[...]
        for k in range(N_PEERS):
            acc = acc + l_ref[k].astype(F32)
        o_ref[...] = acc

    return pl.pallas_call(
        body, name=name,
        out_shape=jax.ShapeDtypeStruct((rows, 1024), F32),
        grid_spec=pltpu.PrefetchScalarGridSpec(
            num_scalar_prefetch=1, grid=(nb,),
            in_specs=[pl.BlockSpec((None, SUM_TILE, 1024), lambda i, s: (s[0], s[1] * nb + i, 0)),
                      pl.BlockSpec((N_PEERS, SUM_TILE, 1024), lambda i, s: (0, i, 0))],
            out_specs=pl.BlockSpec((SUM_TILE, 1024), lambda i, s: (s[1] * nb + i, 0))),
        compiler_params=_params("parallel"),
    )(place, g32, land)


def _join_halves(red, *, name):
    rows = red.shape[0]

    def body(r_ref, o_ref, ssem, rsem):
        x, y, c = _coords()
        cp = _remote(r_ref.at[_half(c, rows)], o_ref.at[_half(c, rows)], ssem, rsem, (x, y, 1 - c))
        cp.start()
        _remote(r_ref.at[_half(1 - c, rows)], o_ref.at[_half(1 - c, rows)], ssem, rsem,
                (x, y, 1 - c)).wait_recv()
        cp.wait_send()

    return pl.pallas_call(
        body, name=name,
        out_shape=jax.ShapeDtypeStruct(red.shape, F32),
        in_specs=[ANY], out_specs=ANY, input_output_aliases={0: 0},
        scratch_shapes=[pltpu.SemaphoreType.DMA, pltpu.SemaphoreType.DMA],
    )(red)


def _sum_small(me, svec, land):
    def body(me_ref, s_ref, l_ref, o_ref):
        mine = me_ref[0]
        acc = None
        for d in range(N_DEV):
            theirs = l_ref[jnp.maximum(jnp.bitwise_xor(mine, d) - 1, 0)]
            v = jnp.where(mine == d, s_ref[...], theirs)
            acc = v if acc is None else acc + v
        o_ref[...] = acc

    return pl.pallas_call(
        body, name="sum_small",
        out_shape=jax.ShapeDtypeStruct(svec.shape, F32),
        grid_spec=pltpu.PrefetchScalarGridSpec(
            num_scalar_prefetch=1, grid=(1,),
            in_specs=[pl.BlockSpec(svec.shape, lambda i, s: (0, 0)),
                      pl.BlockSpec(land.shape, lambda i, s: (0, 0, 0))],
            out_specs=pl.BlockSpec(svec.shape, lambda i, s: (0, 0))),
        compiler_params=_params("arbitrary"),
    )(me, svec, land)


def _pack_early_shards(wts):
    parts = [wts["w_in"][0].T.astype(BF16)] + [wts[n].astype(BF16).reshape(-1, 1024) for n, _ in EARLY[1:]]
    used = sum(r for _, r in EARLY)
    return jnp.concatenate(parts + [jnp.zeros((EARLY_ROWS - used, 1024), BF16)], axis=0)


def _pack_late_shards(wts):
    parts, at = [], 0
    for n, off, rows in LATE:
        if off > at:
            parts.append(jnp.zeros((off - at, 1024), BF16))
        parts.append(wts[n].astype(BF16).reshape(rows, 1024))
        at = off + rows
    return jnp.concatenate(parts, axis=0)


def _unpack_early(full):
    w = {}
    w["w_in_t"] = jnp.concatenate(
        [full[0, N_SMALL:1200]] + [full[p, :1200] for p in range(1, N_CHIPS)]
        + [full[0, :N_SMALL], jnp.zeros((D_MODEL - N_SMALL, 1024), full.dtype)], axis=0)
    w["w_uq"] = full[:, 1200:1344].reshape(4, 384, 384).transpose(1, 0, 2).reshape(384, 1536)
    w["w_uk"] = full[:, 1344:1408].reshape(256, 1024)
    w["w_uv"] = full[:, 1408:1472].reshape(256, 1024)
    return w


def _pack_early_grads(g):
    gt = g["w_in_t"]
    rest = 1200 - N_SMALL
    w_in = jnp.stack([jnp.concatenate([gt[4 * D_MODEL:4 * D_MODEL + N_SMALL], gt[:rest]], axis=0)]
                     + [gt[rest + 1200 * (p - 1):rest + 1200 * p] for p in range(1, N_CHIPS)])
    parts = [
        w_in,
        g["w_uq"].reshape(384, 4, 384).transpose(1, 0, 2).reshape(4, 144, 1024),
        g["w_uk"].reshape(4, 64, 1024),
        g["w_uv"].reshape(4, 64, 1024),
        jnp.zeros((4, EARLY_ROWS - sum(r for _, r in EARLY), 1024), F32),
    ]
    return jnp.concatenate(parts, axis=1)


def _pad_w_uq(w):
    w = w.reshape(Q_RANK, N_HEADS, NOPE + ROPE)
    return jnp.pad(w, ((0, 0), (0, 0), (0, QK_PAD - NOPE - ROPE))).reshape(Q_RANK, N_HEADS * QK_PAD)


def _local_step(xs, pos, tgt, W, late_weights, late_bufs, send_late_grads, send_early_grads, cw, vec):
    D = D_MODEL
    w_in_t = W["w_in_t"]
    w_uq = _pad_w_uq(W["w_uq"])
    w_ukv = jnp.concatenate([W["w_uk"], W["w_uv"]], axis=1)
    inv = ROPE_THETA ** (-jnp.arange(0, ROPE, 2, dtype=F32) / ROPE)
    inv_freq = jnp.concatenate([inv, inv, jnp.zeros((128 - ROPE,), F32)])[None, :]

    (h,) = _rowwise(lambda x, g: (_rms(x, g),), [(xs, D, 0)], [vec["norm_mix_pre"]], [(D, BF16)],
                    name="rms_pre")
    z = _mm(h, w_in_t, tb=True, out_dtype=BF16, name="mm_in")

    def qkv_prep(zs, p, qn, kvn, invf):
        ang = p.astype(F32) * invf
        cosv, sinv = jnp.cos(ang), jnp.sin(ang)
        ln = lax.broadcasted_iota(jnp.int32, ang.shape, 1)
        cs = jnp.where(ln < ROPE, cosv, 0.0)
        sa = jnp.where((ln >= ROPE // 2) & (ln < ROPE), sinv, 0.0)
        sb = jnp.where(ln < ROPE // 2, -sinv, 0.0)
        cqn = _rms(zs[:, :Q_RANK], qn)
        ckvn = _rms(zs[:, Q_RANK:Q_RANK + KV_RANK], kvn)
        kr = _rope(zs[:, 640:768], cs, sa, sb)
        return cqn, ckvn, kr, cs, sa, sb

    cqn, ckvn, krot, cs, sa, sb = _rowwise(
        qkv_prep, [(z, D, Z_S), (pos, 1, 0)], [vec["q_norm"], vec["kv_norm"], inv_freq],
        [(Q_RANK, BF16), (KV_RANK, BF16), (128, BF16), (128, F32), (128, F32), (128, F32)],
        name="qkv_prep")
    q_pre = _mm(cqn, w_uq, out_dtype=BF16, name="mm_uq")
    kv = _mm(ckvn, w_ukv, out_dtype=BF16, name="mm_ukv")

    def qk_rope(qp, kn, vv, kr, cs_, sa_, sb_):
        qs, ks, vs = [], [], []
        ones = jnp.ones((qp.shape[0], NOPE), F32)
        for hd in range(N_HEADS):
            qs.append(qp[:, hd * QK_PAD:hd * QK_PAD + NOPE])
            qs.append(_rope(qp[:, hd * QK_PAD + NOPE:(hd + 1) * QK_PAD], cs_, sa_, sb_))
            ks.append(kn[:, hd * NOPE:(hd + 1) * NOPE])
            ks.append(kr.astype(F32))
            vs.append(vv[:, hd * NOPE:(hd + 1) * NOPE])
            vs.append(ones)
        return (jnp.concatenate(qs, axis=1) * Q_SCALE, jnp.concatenate(ks, axis=1),
                jnp.concatenate(vs, axis=1))

    q, k, v = _rowwise(
        qk_rope, [(q_pre, 2048, 0), (kv, 1024, 0), (kv, 1024, 1), (krot, 128, 0), (cs, 128, 0),
                  (sa, 128, 0), (sb, 128, 0)], [],
        [(2048, BF16), (2048, BF16), (2048, BF16)], name="qk_rope")
    attn, lse = _flash_fwd(q, k, v)
    LW = late_weights(attn)
    y_attn = LW.fwd(attn, "w_o_attn", out_dtype=BF16, name="mm_o_attn")

    (u0,) = _rowwise(lambda a, b: (a * _sigmoid(b),), [(z, D, Z_A), (z, D, Z_B)], [], [(D, F32)], name="glu")
    u1 = _conv_fwd(u0, cw, vec["conv_b"])

    def ln_parts(u, g, b):
        mu = jnp.mean(u, axis=-1, keepdims=True)
        xc = u - mu
        rs = lax.rsqrt(jnp.mean(xc * xc, axis=-1, keepdims=True) + EPS)
        nh = xc * rs
        return nh, rs, nh * g + b

    def ln_silu(u, g, b):
        y = ln_parts(u, g, b)[2]
        return (y * _sigmoid(y),)

    (u3,) = _rowwise(ln_silu, [(u1, D, 0)], [vec["conv_ln_g"], vec["conv_ln_b"]], [(D, BF16)],
                     name="ln_silu")
    y_conv = LW.fwd(u3, "w_pw2", out_dtype=BF16, name="mm_pw2")

    def merge(ga, gc, ya, yc, b2):
        return (_sigmoid(ga) * ya + _sigmoid(gc) * (yc + b2),)

    (merged,) = _rowwise(merge, [(z, D, Z_GA), (z, D, Z_GC), (y_attn, D, 0), (y_conv, D, 0)], [vec["b_pw2"]],
                         [(D, BF16)], name="merge")
    mo = LW.fwd(merged, "w_out", name="mm_out")

    def post1(x, m, g1, g2):
        x1 = x + _rms(m, g1)
        return x1, _rms(x1, g2)

    x1, h2 = _rowwise(post1, [(xs, D, 0), (mo, D, 0)], [vec["norm_mix_post"], vec["norm_mlp_pre"]],
                      [(D, F32), (D, BF16)], name="post_mix")

    def sqrelu(acc):
        r = jnp.maximum(acc, 0.0)
        return r * r, r

    a2, r1 = LW.fwd(h2, "w_ff1", name="mm_ff1", epilogue=sqrelu, out_dtypes=(BF16, BF16))
    f = LW.fwd(a2, "w_ff2", name="mm_ff2")

    def post2(x1_, f_, t_, g):
        e = x1_ + _rms(f_, g) - t_
        dy = e * (1.0 / D)
        df, dg = _rms_bwd(f_, g, dy)
        loss = jnp.broadcast_to(jnp.sum(e * e, keepdims=True) * (0.5 / D), (1, 128))
        return dy, df, dg, loss

    dy, df, g_norm_mlp_post, loss = _rowwise(
        post2, [(x1, D, 0), (f, D, 0), (tgt, D, 0)], [vec["norm_mlp_post"]],
        [(D, F32), (D, BF16)], [(1, D), (1, 128)], name="post_mlp_loss")
    late_bufs = _late_grad(a2, df, "w_ff2", late_bufs, name="mm_dw_ff2")
    df1 = LW.bwd(df, "w_ff2", name="mm_df1", extras=[r1], out_dtypes=(BF16,),
              epilogue=lambda acc, r: (acc * (2.0 * r.astype(F32)),))
    late_bufs = _late_grad(h2, df1, "w_ff1", late_bufs, name="mm_dw_ff1")
    dh2 = LW.bwd(df1, "w_ff1", out_dtype=BF16, name="mm_dh2")

    def bwd_mid(dy_, dh2_, x1_, m, g2, g1):
        d1, dg2 = _rms_bwd(x1_, g2, dh2_)
        dx1_ = dy_ + d1
        dm, dg1 = _rms_bwd(m, g1, dx1_)
        return dx1_, dm, dg2, dg1

    dx1, dmo, g_norm_mlp_pre, g_norm_mix_post = _rowwise(
        bwd_mid, [(dy, D, 0), (dh2, D, 0), (x1, D, 0), (mo, D, 0)],
        [vec["norm_mlp_pre"], vec["norm_mix_post"]], [(D, F32), (D, BF16)], [(1, D), (1, D)],
        name="bwd_mid")
    late_bufs = _late_grad(merged, dmo, "w_out", late_bufs, name="mm_dw_out")
    dmerged = LW.bwd(dmo, "w_out", out_dtype=BF16, name="mm_dmerged")

    def dmerge(dm, ga, gc, ya, yc, b2):
        sga, sgc = _sigmoid(ga), _sigmoid(gc)
        dya = dm * sga
        dyc = dm * sgc
        dga = dm * ya * sga * (1.0 - sga)
        dgc = dm * (yc + b2) * sgc * (1.0 - sgc)
        return dya, dyc, jnp.concatenate([dga, dgc], axis=1), jnp.sum(dyc, axis=0, keepdims=True)

    dz = lax.empty((xs.shape[0], Z_COLS), BF16)
    dya, dyc, dz, g_b_pw2 = _rowwise(
        dmerge, [(dmerged, D, 0), (z, D, Z_GA), (z, D, Z_GC), (y_attn, D, 0), (y_conv, D, 0)],
        [vec["b_pw2"]], [(D, BF16), (D, BF16), (2 * D, dz, Z_GA // 2)], [(1, D)], name="dmerge")

    late_bufs = _late_grad(attn, dya, "w_o_attn", late_bufs, name="mm_dw_o_attn")
    late_bufs = _late_grad(u3, dyc, "w_pw2", late_bufs, name="mm_dw_pw2")
    late_token = send_late_grads(late_bufs)

    dattn = LW.bwd(dya, "w_o_attn", out_dtype=BF16, name="mm_dattn")

    def delta_fn(do_, o_):
        do32 = do_.astype(F32)
        pr = do32 * o_.astype(F32)
        ln = lax.broadcasted_iota(jnp.int32, (pr.shape[0], NOPE), 1)
        cols = []
        for hd in range(N_HEADS):
            dl = jnp.sum(pr[:, hd * NOPE:(hd + 1) * NOPE], axis=-1, keepdims=True)
            hi = dl.astype(BF16).astype(F32)
            cols.append(do32[:, hd * NOPE:(hd + 1) * NOPE])
            cols.append(jnp.where(ln == 0, -hi, jnp.where(ln == 1, hi - dl, 0.0)))
        return (jnp.concatenate(cols, axis=1),)

    (do_ext,) = _rowwise(delta_fn, [(dattn, D, 0), (attn, D, 0)], [], [(2048, BF16)], name="attn_delta",
                         after=[late_token])
    dq, dk, dv = _flash_bwd(q, k, v, do_ext, lse)

    def unrope(dq_, dk_, dv_, cs_, sa_, sb_):
        dq_, dk_ = dq_.astype(F32), dk_.astype(F32)
        qs, kn = [], []
        dkr = jnp.zeros_like(cs_)
        for hd in range(N_HEADS):
            qs.append(dq_[:, hd * QK_PAD:hd * QK_PAD + NOPE] * SCALE)
            qs.append(_rope_t(dq_[:, hd * QK_PAD + NOPE:(hd + 1) * QK_PAD] * SCALE, cs_, sa_, sb_))
            kn.append(dk_[:, hd * QK_PAD:hd * QK_PAD + NOPE] * K_UNSCALE)
            dkr = dkr + dk_[:, hd * QK_PAD + NOPE:(hd + 1) * QK_PAD]
        return (jnp.concatenate(qs, axis=1), jnp.concatenate(kn + [dv_], axis=1),
                _rope_t(dkr * K_UNSCALE, cs_, sa_, sb_))

    dq_pre, dkv, dkr = _rowwise(
        unrope, [(dq, 2048, 0), (dk, 2048, 0), (dv, 1024, 0), (cs, 128, 0), (sa, 128, 0), (sb, 128, 0)], [],
        [(2048, BF16), (2048, BF16), (128, F32)], name="unrope")
    g_w_uq = _mm(cqn, dq_pre, ta=True, name="mm_dw_uq")
    dcqn = _mm(dq_pre, w_uq, tb=True, out_dtype=BF16, name="mm_dcqn")
    g_w_ukv = _mm(ckvn, dkv, ta=True, name="mm_dw_ukv")
    dckvn = _mm(dkv, w_ukv, tb=True, out_dtype=BF16, name="mm_dckvn")

    def small_bwd(zs, dcq_, dckv_, dkr_, qn, kvn):
        dcq, dqn = _rms_bwd(zs[:, :Q_RANK], qn, dcq_)
        dckv, dkvn = _rms_bwd(zs[:, Q_RANK:Q_RANK + KV_RANK], kvn, dckv_)
        pad = jnp.zeros((zs.shape[0], D - 768), F32)
        return jnp.concatenate([dcq, dckv, dkr_, pad], axis=1), dqn, dkvn

    dz, g_q_norm, g_kv_norm = _rowwise(
        small_bwd, [(z, D, Z_S), (dcqn, Q_RANK, 0), (dckvn, KV_RANK, 0), (dkr, 128, 0)],
        [vec["q_norm"], vec["kv_norm"]], [(D, dz, Z_S)], [(1, Q_RANK), (1, KV_RANK)], name="small_bwd")

    du3 = LW.bwd(dyc, "w_pw2", out_dtype=BF16, name="mm_du3")

    def ln_silu_bwd(d3, u, g, b):
        nh, rs, y = ln_parts(u, g, b)
        sg = _sigmoid(y)
        dyv = d3 * (sg * (1.0 + y * (1.0 - sg)))
        dnh = dyv * g
        du = rs * (dnh - jnp.mean(dnh, axis=-1, keepdims=True)
                   - nh * jnp.mean(dnh * nh, axis=-1, keepdims=True))
        return (du, jnp.sum(dyv * nh, axis=0, keepdims=True), jnp.sum(dyv, axis=0, keepdims=True),
                jnp.sum(du, axis=0, keepdims=True))

    du1, g_ln_g, g_ln_b, g_conv_b = _rowwise(
        ln_silu_bwd, [(du3, D, 0), (u1, D, 0)], [vec["conv_ln_g"], vec["conv_ln_b"]], [(D, F32)],
        [(1, D), (1, D), (1, D)], name="ln_silu_bwd")
    du0, g_conv_w = _conv_bwd(u0, du1, cw)

    def glu_bwd(d0, a, b):
        sg = _sigmoid(b)
        return (jnp.concatenate([d0 * sg, d0 * a * sg * (1.0 - sg)], axis=1),)

    (dz,) = _rowwise(glu_bwd, [(du0, D, 0), (z, D, Z_A), (z, D, Z_B)], [], [(2 * D, dz, Z_A // 2)],
                     name="glu_bwd")
    g_w_in_t = _mm(dz, h, ta=True, name="mm_dw_in")
    early_token = send_early_grads({
        "w_in_t": g_w_in_t,
        "w_uq": g_w_uq.reshape(Q_RANK, N_HEADS, QK_PAD)[:, :, :NOPE + ROPE].reshape(Q_RANK, 1536),
        "w_uk": g_w_ukv[:, :1024], "w_uv": g_w_ukv[:, 1024:]})
    dh = _mm(dz, w_in_t, name="mm_dh", after=[early_token])

    def final(dx1_, dh_, x, g):
        d, dg = _rms_bwd(x, g, dh_)
        return dx1_ + d, dg

    grad_x, g_norm_mix_pre = _rowwise(final, [(dx1, D, 0), (dh, D, 0), (xs, D, 0)], [vec["norm_mix_pre"]],
                                      [(D, F32)], [(1, D)], name="bwd_final")

    gs = {
        "norm_mix_pre": g_norm_mix_pre, "conv_b": g_conv_b, "conv_ln_g": g_ln_g, "conv_ln_b": g_ln_b,
        "b_pw2": g_b_pw2, "norm_mix_post": g_norm_mix_post, "norm_mlp_pre": g_norm_mlp_pre,
        "norm_mlp_post": g_norm_mlp_post, "q_norm": g_q_norm, "kv_norm": g_kv_norm,
    }
    return grad_x, gs, g_conv_w, loss


def _row1024(a):
    return jnp.pad(a, ((0, 0), (0, 1024 - a.shape[1])))


BIG = tuple(n for n, _ in EARLY) + tuple(n for n, _, _ in LATE)
ORDER = ("norm_mix_pre", "w_in", "q_norm", "w_uq", "kv_norm", "w_uk", "w_uv", "w_o_attn", "conv_w",
         "conv_b", "conv_ln_g", "conv_ln_b", "w_pw2", "b_pw2", "w_out", "norm_mix_post", "norm_mlp_pre",
         "w_ff1", "w_ff2", "norm_mlp_post")


def kernel(x, positions, norm_mix_pre, w_in, q_norm, w_uq, kv_norm, w_uk, w_uv, w_o_attn, conv_w, conv_b, conv_ln_g, conv_ln_b, w_pw2, b_pw2, w_out, norm_mix_post, norm_mlp_pre, w_ff1, w_ff2, norm_mlp_post, loss_target, m_norm_mix_pre, m_w_in, m_q_norm, m_w_uq, m_kv_norm, m_w_uk, m_w_uv, m_w_o_attn, m_conv_w, m_conv_b, m_conv_ln_g, m_conv_ln_b, m_w_pw2, m_b_pw2, m_w_out, m_norm_mix_post, m_norm_mlp_pre, m_w_ff1, m_w_ff2, m_norm_mlp_post, v_norm_mix_pre, v_w_in, v_q_norm, v_w_uq, v_kv_norm, v_w_uk, v_w_uv, v_w_o_attn, v_conv_w, v_conv_b, v_conv_ln_g, v_conv_ln_b, v_w_pw2, v_b_pw2, v_w_out, v_norm_mix_post, v_norm_mlp_pre, v_w_ff1, v_w_ff2, v_norm_mlp_post):
    args = dict(locals())
    wts = {n: args[n] for n in ORDER}
    mom = {n: args["m_" + n] for n in ORDER}
    var = {n: args["v_" + n] for n in ORDER}
    T = x.shape[1]
    place = jnp.stack([2 * lax.axis_index("x") + lax.axis_index("y"), lax.axis_index("c")]).astype(jnp.int32)

    w_early = _pack_early_shards(wts)
    w_late = _pack_late_shards(wts)
    cw_own = jnp.pad(conv_w.reshape(CONV_W, -1), ((0, CONV_TAPS_PAD - CONV_W), (0, 0)))
    full, cw_all = _gather_weights(w_early, cw_own)
    full = lax.dynamic_update_slice(full, w_early[None], (place[0], 0, 0))
    cw_all = lax.dynamic_update_slice(cw_all, cw_own[None], (place[0], 0, 0))
    W = _unpack_early(full)
    cw = cw_all.transpose(1, 0, 2).reshape(CONV_TAPS_PAD, D_MODEL)
    sems, w_thru, land, token = _late_gather_start(w_late, full)
    vec = {n: wts[n] for n in SMALL_VECS}
    vec["norm_mix_pre"] = vec["norm_mix_pre"] + token[:1, :1]

    def late_weights(after):
        landed = _late_gather_wait(sems, w_thru, land, after)
        return _LateWeights(lax.dynamic_update_slice(landed, w_late[None], (place[0], 0, 0)))

    sent = {}

    def send_late_grads(bufs):
        sent["late"] = (bufs[0],) + _reduce_start(bufs[1], place, name="late_grads_start")
        return sent["late"][4]

    def send_early_grads(g):
        g32 = _pack_early_grads(g)
        sent["early"] = (g32,) + _reduce_start(g32.astype(BF16), place, name="early_grads_start")
        return sent["early"][4]

    late_bufs = (lax.empty((N_CHIPS, LATE_ROWS, 1024), F32), lax.empty((N_CHIPS, LATE_ROWS, 1024), BF16))
    grad_x, gs, g_conv_w, loss = _local_step(x[0], positions.reshape(T, 1), loss_target[0], W, late_weights,
                                             late_bufs, send_late_grads, send_early_grads, cw, vec)

    def finish(group, after):
        g32, sems_, g_thru, land_, _ = sent[group]
        landed = _reduce_wait(sems_, g_thru, land_, after, name=group + "_grads_wait")
        red = _reduce_sum(place, g32, landed, name=group + "_grads_sum")
        return _join_halves(red, name=group + "_grads_join")

    def as2d(a):
        return a.reshape(-1, a.shape[-1]) if a.ndim != 4 or a.shape[2] == 1 else a.reshape(a.shape[1], -1)

    grads, delta, new_m, new_v = {}, {}, {}, {}

    def update(n):
        shp = wts[n].shape
        outs = _adamw(as2d(wts[n]), as2d(grads[n]), as2d(mom[n]), as2d(var[n]), name="adamw_" + n)
        delta[n], new_m[n], new_v[n] = (o.reshape(shp) for o in outs)

    g_late = finish("late", [sent["early"][4]])
    for n, off, rows in LATE:
        shp = wts[n].shape
        outs = _adamw(as2d(wts[n]), g_late, as2d(mom[n]), as2d(var[n]), g_row=off, name="adamw_" + n)
        delta[n], new_m[n], new_v[n], grads[n] = (o.reshape(shp) for o in outs)

    svec = jnp.concatenate(
        [_row1024(gs[n]) for n in SMALL_VECS] + [_row1024(loss)]
        + [jnp.zeros((SMALL_CONVW_ROW - SMALL_LOSS_ROW - 1, 1024), F32), g_conv_w], axis=0)
    small_sent = _reduce_start(svec, place, name="small_grads_start", whole=True)

    g_early = finish("early", [grad_x, delta["w_ff2"]])
    grads["w_in"] = g_early[:1200].T[None]
    off = 1200
    for n, rows in EARLY[1:]:
        grads[n] = g_early[off:off + rows].reshape(wts[n].shape)
        off += rows
    for n, _ in EARLY:
        update(n)

    landed = _reduce_wait(small_sent[0], small_sent[1], small_sent[2], [delta["w_in"]],
                          name="small_grads_wait", whole=True)
    ssum = _sum_small((2 * place[0] + place[1])[None], svec, landed)
    for r, n in enumerate(SMALL_VECS):
        grads[n] = ssum[r:r + 1, :wts[n].shape[1]]
    cw_sum = ssum[SMALL_CONVW_ROW:SMALL_CONVW_ROW + CONV_W]
    grads["conv_w"] = lax.dynamic_slice_in_dim(cw_sum, place[0] * 256, 256, axis=1).reshape(conv_w.shape)
    loss_out = ssum[SMALL_LOSS_ROW, 0]
    update("conv_w")
    small = [jnp.concatenate([_row1024(t[n]) for n in SMALL_VECS], axis=0) for t in (wts, grads, mom, var)]
    outs = _adamw(*small, name="adamw_small")
    for r, n in enumerate(SMALL_VECS):
        delta[n], new_m[n], new_v[n] = (o[r:r + 1, :wts[n].shape[1]] for o in outs)

    return (loss_out, grad_x[None], *[grads[n] for n in ORDER], *[delta[n] for n in ORDER],
            *[new_m[n] for n in ORDER], *[new_v[n] for n in ORDER])
```

```python
import functools

import numpy as np
import jax
import jax.numpy as jnp
from jax import lax
from jax.experimental import pallas as pl
from jax.experimental.pallas import tpu as pltpu

F32 = jnp.float32
BF16 = jnp.bfloat16
MESH = pl.DeviceIdType.MESH

D_MODEL = 1024
N_HEADS = 8
NOPE = 128
ROPE = 64
QK_PAD = 256
Q_RANK = 384
KV_RANK = 256
CONV_W = 31
CONV_TAPS_PAD = 32
HALO = 32
D_FF = 4096
EPS = 1e-6
ROPE_THETA = 10000.0
SCALE = float((NOPE + ROPE) ** -0.5)
Q_SCALE = SCALE * float(np.log2(np.e))
K_UNSCALE = float(np.log(2.0))
Z_A, Z_B, Z_GA, Z_GC, Z_S = range(5)
Z_COLS = 5 * D_MODEL
N_SMALL = Q_RANK + KV_RANK + ROPE
N_CHIPS = 4
N_DEV = 8

ADAM_LR = 0.001
ADAM_B1 = 0.9
ADAM_B2 = 0.999
ADAM_EPS = 1e-08
ADAM_WD = 0.01
ADAM_STEP = 10

VMEM_LIMIT = 48 * 1024 * 1024
NEG = -1e30

EARLY = (("w_in", 1200), ("w_uq", 144), ("w_uk", 64), ("w_uv", 64))
EARLY_ROWS = 1536
LATE = (("w_ff1", 0, 1024), ("w_o_attn", 1024, 256), ("w_pw2", 1536, 256), ("w_out", 1792, 256),
        ("w_ff2", 2048, 1024))
LATE_OFF = {n: off for n, off, _ in LATE}
LATE_ROWS = 3072
SUM_TILE = 256

SMALL_VECS = ("norm_mix_pre", "conv_b", "conv_ln_g", "conv_ln_b", "b_pw2", "norm_mix_post",
              "norm_mlp_pre", "norm_mlp_post", "q_norm", "kv_norm")
SMALL_LOSS_ROW = 10
SMALL_CONVW_ROW = 16
SMALL_ROWS = 48


ANY = pl.BlockSpec(memory_space=pl.ANY)


def _params(*sem):
    return pltpu.CompilerParams(dimension_semantics=sem, vmem_limit_bytes=VMEM_LIMIT)


def _tile(n, cap):
    if n <= cap:
        return n
    t = (cap // 128) * 128
    while n % t:
        t -= 128
    return t


def _row_tile(n, cap):
    if n <= cap:
        return n
    t = (cap // 8) * 8
    while n % t:
        t -= 8
    return t


MM_VMEM_BUDGET = 40 * 1024 * 1024


def _mm_tiles(M, N, K, in_bytes, out_bytes):
    tn = _tile(N, 1024)
    tk = K
    while True:
        for cap in (1024, 512, 256):
            tm = _tile(M, cap)
            need = 2 * tk * (tm * in_bytes[0] + tn * in_bytes[1]) + 2 * tm * tn * out_bytes
            need += tm * tn * 4 if tk < K else 0
            if need <= MM_VMEM_BUDGET:
                return tm, tn, tk
        tk = _tile(K, tk - 128)


def _mm(a, b, *, name, ta=False, tb=False, out_dtype=F32, extras=(), epilogue=None, out_dtypes=None,
        after=(), shape=None, tiles=None, b_spec=None, into=None, reds=()):
    if shape is None:
        M, K = (a.shape[1], a.shape[0]) if ta else a.shape
        N = b.shape[0] if tb else b.shape[1]
        assert K == (b.shape[1] if tb else b.shape[0]), (a.shape, b.shape, ta, tb)
    else:
        M, N, K = shape
    out_dtypes = tuple(out_dtypes or (out_dtype,))
    if into is not None:
        out_dtypes = tuple(buf.dtype for buf, _ in into)
    if tiles is None:
        out_bytes = sum(jnp.dtype(d).itemsize for d in out_dtypes) + sum(e.dtype.itemsize for e in extras)
        tiles = _mm_tiles(M, N, K, (a.dtype.itemsize, b.dtype.itemsize), out_bytes)
    tm, tn, tk = tiles
    nk = K // tk
    dims = (((0 if ta else 1,), (1 if tb else 0,)), ((), ()))
    n_ex, n_out = len(extras), len(out_dtypes)
    n_pass = len(after) + (n_out if into is not None else 0)
    assert not reds or tn == N

    def body(*refs):
        a_ref, b_ref = refs[:2]
        ex_refs = refs[2:2 + n_ex]
        o_refs = refs[2 + n_ex + n_pass:2 + n_ex + n_pass + n_out]
        red_refs = refs[2 + n_ex + n_pass + n_out:2 + n_ex + n_pass + n_out + len(reds)]
        first_rows, k = pl.program_id(0) == 0, pl.program_id(2)
        bv = b_ref[...]
        if bv.ndim == 3:
            bv = bv.reshape(-1, bv.shape[-1])
        part = lax.dot_general(a_ref[...].astype(BF16), bv.astype(BF16), dims, preferred_element_type=F32)

        def finish(acc):
            ex = [r[...] for r in ex_refs]
            ex = [e.astype(F32) if e.dtype == BF16 else e for e in ex]
            res = epilogue(acc, *ex) if epilogue else (acc,) * n_out
            for o_ref, r in zip(o_refs, res):
                o_ref[...] = r.astype(o_ref.dtype)
            for red_ref, r in zip(red_refs, res[n_out:]):
                @pl.when(first_rows)
                def _(red_ref=red_ref):
                    red_ref[...] = jnp.zeros_like(red_ref)

                red_ref[...] += r

        if nk == 1:
            finish(part)
        else:
            acc_ref = refs[-1]

            @pl.when(k == 0)
            def _():
                acc_ref[...] = part

            @pl.when((k > 0) & (k < nk - 1))
            def _():
                acc_ref[...] += part

            @pl.when(k == nk - 1)
            def _():
                finish(acc_ref[...] + part)

    a_spec = (pl.BlockSpec((tk, tm), lambda i, j, k: (k, i)) if ta
              else pl.BlockSpec((tm, tk), lambda i, j, k: (i, k)))
    if b_spec is None:
        b_spec = (pl.BlockSpec((tn, tk), lambda i, j, k: (j, k)) if tb
                  else pl.BlockSpec((tk, tn), lambda i, j, k: (k, j)))
    o_spec = pl.BlockSpec((tm, tn), lambda i, j, k: (i, j))
    row_spec = pl.BlockSpec((1, tn), lambda i, j, k: (0, j))
    ex_specs = [row_spec if e.shape[0] == 1 else o_spec for e in extras]
    if into is None:
        out_shape = [jax.ShapeDtypeStruct((M, N), dt) for dt in out_dtypes]
        out_specs, passed, aliases = [o_spec] * n_out, list(after), {}
    else:
        out_shape = [jax.ShapeDtypeStruct(buf.shape, buf.dtype) for buf, _ in into]
        out_specs = [spec for _, spec in into]
        passed = list(after) + [buf for buf, _ in into]
        aliases = {2 + n_ex + len(after) + o: o for o in range(n_out)}
    out_shape = out_shape + [jax.ShapeDtypeStruct((1, w), F32) for w in reds]
    out_specs = out_specs + [pl.BlockSpec((1, w), lambda i, j, k: (0, 0)) for w in reds]
    res = pl.pallas_call(
        body, name=name, out_shape=out_shape,
        grid=(M // tm, N // tn, nk),
        in_specs=[a_spec, b_spec] + ex_specs + [ANY] * len(passed),
        out_specs=out_specs, input_output_aliases=aliases,
        scratch_shapes=[pltpu.VMEM((tm, tn), F32)] if nk > 1 else [],
        compiler_params=_params("arbitrary" if reds else "parallel", "parallel", "arbitrary"),
    )(a, b, *extras, *passed)
    return res[0] if len(res) == 1 else res


class _LateWeights:
    def __init__(self, pack):
        self.pack = pack

    def fwd(self, x, n, *, name, tm=1024, **kw):
        T, off = x.shape[0], LATE_OFF[n]
        if n == "w_ff1":
            return _mm(x, self.pack, name=name, shape=(T, 4096, 1024), tiles=(_tile(T, tm), 1024, 1024),
                       b_spec=pl.BlockSpec((None, 1024, 1024), lambda i, j, k: (j, 0, 0)), **kw)
        if n == "w_ff2":
            view = self.pack.reshape(N_CHIPS, LATE_ROWS // 1024, 1024, 1024)
            return _mm(x, view, name=name, shape=(T, 1024, 4096), tiles=(_tile(T, min(tm, 512)), 1024, 4096),
                       b_spec=pl.BlockSpec((N_CHIPS, None, 1024, 1024), lambda i, j, k: (0, off // 1024, 0, 0)),
                       **kw)
        return _mm(x, self.pack.reshape(N_CHIPS, LATE_ROWS // 256, 256, 1024), name=name,
                   shape=(T, 1024, 1024), tiles=(_tile(T, tm), 1024, 1024), b_spec=self._whole(off), **kw)

    @staticmethod
    def _whole(off):
        return pl.BlockSpec((N_CHIPS, None, 256, 1024), lambda i, j, k: (0, off // 256, 0, 0))

    def bwd(self, dy, n, *, name, tm=1024, **kw):
        T, off = dy.shape[0], LATE_OFF[n]
        tm = _tile(T, tm)
        if n == "w_ff1":
            return _mm(dy, self.pack, tb=True, name=name, shape=(T, 1024, 4096), tiles=(tm, 1024, 1024),
                       b_spec=pl.BlockSpec((None, 1024, 1024), lambda i, j, k: (k, 0, 0)), **kw)
        if n == "w_ff2":
            return _mm(dy, self.pack, tb=True, name=name, shape=(T, 4096, 1024), tiles=(tm, 1024, 1024),
                       b_spec=pl.BlockSpec((None, 1024, 1024), lambda i, j, k: (j, off // 1024, 0)), **kw)
        return _mm(dy, self.pack.reshape(N_CHIPS, LATE_ROWS // 256, 256, 1024), tb=True, name=name,
                   shape=(T, 1024, 1024), tiles=(tm, 1024, 1024), b_spec=self._whole(off), **kw)


def _late_grad(x, dy, n, bufs, *, name):
    T, off = x.shape[0], LATE_OFF[n]
    if n == "w_ff1":
        spec = pl.BlockSpec((None, 512, 1024), lambda i, j, k: (j, i, 0))
        shape, tiles = (1024, 4096, T), (512, 1024, T)
    elif n == "w_ff2":
        spec = pl.BlockSpec((None, 512, 1024), lambda i, j, k: (i // 2, off // 512 + i % 2, 0))
        shape, tiles = (4096, 1024, T), (512, 1024, T)
    else:
        spec = pl.BlockSpec((None, 256, 1024), lambda i, j, k: (i, off // 256, 0))
        shape, tiles = (1024, 1024, T), (256, 1024, T)
    return _mm(x, dy, ta=True, name=name, shape=shape, tiles=tiles, into=[(b, spec) for b in bufs])


def _rowwise(fn, rows, consts, outs, reds=(), *, name, tm=256, after=()):
    T = rows[0][0].shape[0]
    tm = min(tm, T)
    n_row, n_const, n_out = len(rows), len(consts), len(outs)
    placed = [o for o in outs if len(o) == 3]
    n_pass = len(after) + len(placed)

    def body(*refs):
        i = pl.program_id(0)
        vals = [r[...] for r in refs[:n_row + n_const]]
        vals = [v.astype(F32) if v.dtype == BF16 else v for v in vals]
        res = fn(*vals)
        out_refs = refs[n_row + n_const + n_pass:]
        for k in range(n_out):
            out_refs[k][...] = res[k].astype(out_refs[k].dtype)
        for k in range(len(reds)):
            ref = out_refs[n_out + k]

            @pl.when(i == 0)
            def _(ref=ref):
                ref[...] = jnp.zeros_like(ref)

            ref[...] += res[n_out + k]

    in_specs = [pl.BlockSpec((tm, w), functools.partial(lambda i, cb: (i, cb), cb=cb))
                for (_, w, cb) in rows]
    in_specs += [pl.BlockSpec(c.shape, lambda i: (0, 0)) for c in consts]
    in_specs += [ANY] * n_pass
    out_specs, out_shape, aliases = [], [], {}
    for o, out in enumerate(outs):
        if len(out) == 3:
            w, buf, cb = out
            out_specs.append(pl.BlockSpec((tm, w), functools.partial(lambda i, cb: (i, cb), cb=cb)))
            out_shape.append(jax.ShapeDtypeStruct(buf.shape, buf.dtype))
            aliases[n_row + n_const + len(after) + len(aliases)] = o
        else:
            out_specs.append(pl.BlockSpec((tm, out[0]), lambda i: (i, 0)))
            out_shape.append(jax.ShapeDtypeStruct((T, out[0]), out[1]))
    out_specs += [pl.BlockSpec(s, lambda i: (0, 0)) for s in reds]
    out_shape += [jax.ShapeDtypeStruct(s, F32) for s in reds]
    return pl.pallas_call(
        body, name=name, out_shape=out_shape, grid=(T // tm,),
        in_specs=in_specs, out_specs=out_specs, input_output_aliases=aliases,
        compiler_params=_params("arbitrary"),
    )(*[r[0] for r in rows], *consts, *after, *[o[1] for o in placed])


def _rms(x, g):
    r = lax.rsqrt(jnp.mean(x * x, axis=-1, keepdims=True) + EPS)
    return x * r * g


def _rms_bwd(x, g, dy):
    r = lax.rsqrt(jnp.mean(x * x, axis=-1, keepdims=True) + EPS)
    n = x * r
    dyg = dy * g
    dx = r * (dyg - n * jnp.mean(dyg * n, axis=-1, keepdims=True))
    return dx, jnp.sum(dy * n, axis=0, keepdims=True)


def _sigmoid(x):
    return 1.0 / (1.0 + jnp.exp(-x))


def _rope(x, cs, sa, sb):
    return x * cs + pltpu.roll(x, 32, 1) * sa + pltpu.roll(x, 96, 1) * sb


def _rope_t(dy, cs, sa, sb):
    return dy * cs + pltpu.roll(dy * sa, 96, 1) + pltpu.roll(dy * sb, 32, 1)


def _causal_mask(t):
    row = lax.broadcasted_iota(jnp.int32, (t, t), 0)
    col = lax.broadcasted_iota(jnp.int32, (t, t), 1)
    return col <= row


FLASH_FWD_TQ = 4096
FLASH_FWD_TK = 1024
FLASH_FWD_HEADS = 1
FLASH_BWD_TQ = 1024
FLASH_BWD_TK = 512
NT_DIMS = (((1,), (1,)), ((), ()))
TN_DIMS = (((0,), (0,)), ((), ()))


def _flash_fwd(q, k, v, *, tq=FLASH_FWD_TQ, tk=FLASH_FWD_TK, hp=FLASH_FWD_HEADS):
    T = q.shape[0]
    tq, tk = min(tq, T), min(tk, T)
    n_diag = tq // tk
    assert n_diag >= 1 and N_HEADS % hp == 0
    heads = range(hp)

    def body(q_ref, k_ref, v_ref, o_ref, lse_ref):
        i = pl.program_id(1)

        def cols(ref, rows, h, width=QK_PAD):
            return ref[rows, h * width:(h + 1) * width]

        def logits(j, r0=0):
            sl = pl.ds(pl.multiple_of(j * tk, tk), tk)
            return tuple(lax.dot_general(cols(q_ref, slice(r0, tq), h), cols(k_ref, sl, h), NT_DIMS,
                                         preferred_element_type=F32) for h in heads)

        def update(j, s, m, acc, masked, r0=0):
            sl = pl.ds(pl.multiple_of(j * tk, tk), tk)
            m_out, acc_out = [], []
            for h in heads:
                sh = s[h]
                if masked:
                    row = i * tq + r0 + lax.broadcasted_iota(jnp.int32, (tq - r0, tk), 0)
                    col = j * tk + lax.broadcasted_iota(jnp.int32, (tq - r0, tk), 1)
                    sh = jnp.where(col <= row, sh, NEG)
                m_old, acc_old = m[h][r0:], acc[h][r0:]
                m_new = jnp.maximum(m_old, jnp.max(sh, axis=-1, keepdims=True))
                p = jnp.exp2(sh - m_new)
                acc_new = (jnp.exp2(m_old - m_new) * acc_old
                           + jnp.dot(p.astype(BF16), cols(v_ref, sl, h), preferred_element_type=F32))
                if r0:
                    m_new = jnp.concatenate([m[h][:r0], m_new], axis=0)
                    acc_new = jnp.concatenate([acc[h][:r0], acc_new], axis=0)
                m_out.append(m_new)
                acc_out.append(acc_new)
            return tuple(m_out), tuple(acc_out)

        def step(j, carry):
            s, m, acc = carry
            s_next = logits(j + 1)
            m, acc = update(j, s, m, acc, False)
            return s_next, m, acc

        first = i * n_diag
        init = (logits(0), tuple(jnp.full((tq, 1), NEG, F32) for _ in heads),
                tuple(jnp.zeros((tq, QK_PAD), F32) for _ in heads))
        s, m, acc = lax.fori_loop(0, first, step, init)
        for d in range(n_diag):
            m, acc = update(first + d, s, m, acc, True, r0=d * tk)
            if d + 1 < n_diag:
                s = logits(first + d + 1, r0=(d + 1) * tk)
        for h in heads:
            l = acc[h][:, NOPE:]
            o_ref[:, h * NOPE:(h + 1) * NOPE] = (acc[h][:, :NOPE] / l).astype(BF16)
            lse_ref[:, h * NOPE:(h + 1) * NOPE] = m[h] + jnp.log2(l)

    return pl.pallas_call(
        body, name="flash_fwd",
        out_shape=[jax.ShapeDtypeStruct((T, N_HEADS * NOPE), BF16),
                   jax.ShapeDtypeStruct((T, N_HEADS * NOPE), F32)],
        grid=(N_HEADS // hp, T // tq),
        in_specs=[pl.BlockSpec((tq, hp * QK_PAD), lambda h, i: (i, h)),
                  pl.BlockSpec((T, hp * QK_PAD), lambda h, i: (0, h)),
                  pl.BlockSpec((T, hp * QK_PAD), lambda h, i: (0, h))],
        out_specs=[pl.BlockSpec((tq, hp * NOPE), lambda h, i: (i, h)),
                   pl.BlockSpec((tq, hp * NOPE), lambda h, i: (i, h))],
        compiler_params=_params("parallel", "arbitrary"),
    )(q, k, v)


def _flash_bwd(q, k, v, do, lse, *, tq=FLASH_BWD_TQ, tk=FLASH_BWD_TK):
    T = q.shape[0]
    tq, tk = min(tq, T), min(tk, T)

    def body(q_ref, k_ref, v_ref, do_ref, lse_ref, dq_ref, dk_ref, dv_ref, dq_acc):
        dq_acc[...] = jnp.zeros_like(dq_acc)
        for j in range(T // tk):
            k0 = j * tk
            kj = k_ref[k0:k0 + tk, :]
            vj = v_ref[k0:k0 + tk, :]
            dk = jnp.zeros((tk, QK_PAD), F32)
            dv = jnp.zeros((tk, NOPE), F32)
            for i in range(k0 // tq, T // tq):
                r0 = max(i * tq, k0)
                r1 = (i + 1) * tq
                qi = q_ref[r0:r1, :]
                doi = do_ref[r0:r1, :]
                s = lax.dot_general(qi, kj, NT_DIMS, preferred_element_type=F32)
                p = jnp.exp2(s - lse_ref[r0:r1, :][:, :1])
                if r0 < k0 + tk:
                    row = r0 + lax.broadcasted_iota(jnp.int32, (r1 - r0, tk), 0)
                    col = k0 + lax.broadcasted_iota(jnp.int32, (r1 - r0, tk), 1)
                    p = jnp.where(col <= row, p, 0.0)
                dv = dv + lax.dot_general(p.astype(BF16), doi[:, :NOPE], TN_DIMS, preferred_element_type=F32)
                ds = (p * lax.dot_general(doi, vj, NT_DIMS, preferred_element_type=F32)).astype(BF16)
                dk = dk + lax.dot_general(ds, qi, TN_DIMS, preferred_element_type=F32)
                dq_acc[r0:r1, :] += jnp.dot(ds, kj, preferred_element_type=F32)
            dk_ref[k0:k0 + tk, :] = dk.astype(BF16)
            dv_ref[k0:k0 + tk, :] = dv.astype(BF16)
        dq_ref[...] = dq_acc[...].astype(BF16)

    return pl.pallas_call(
        body, name="flash_bwd",
        out_shape=[jax.ShapeDtypeStruct((T, N_HEADS * QK_PAD), BF16),
                   jax.ShapeDtypeStruct((T, N_HEADS * QK_PAD), BF16),
                   jax.ShapeDtypeStruct((T, N_HEADS * NOPE), BF16)],
        grid=(N_HEADS,),
        in_specs=[pl.BlockSpec((T, QK_PAD), lambda h: (0, h)),
                  pl.BlockSpec((T, QK_PAD), lambda h: (0, h)),
                  pl.BlockSpec((T, QK_PAD), lambda h: (0, h)),
                  pl.BlockSpec((T, QK_PAD), lambda h: (0, h)),
                  pl.BlockSpec((T, NOPE), lambda h: (0, h))],
        out_specs=[pl.BlockSpec((T, QK_PAD), lambda h: (0, h)),
                   pl.BlockSpec((T, QK_PAD), lambda h: (0, h)),
                   pl.BlockSpec((T, NOPE), lambda h: (0, h))],
        scratch_shapes=[pltpu.VMEM((T, QK_PAD), F32)],
        compiler_params=_params("arbitrary"),
    )(q, k, v, do, lse)


SUB = 128


def _windows(buf, t0, c0, first, last):
    span = SUB + 8 * ((last + 7) // 8)
    base = buf[t0:t0 + span, c0:c0 + SUB]
    for r in range(8):
        offs = [o for o in range(first, last + 1) if o % 8 == r]
        if offs:
            win = base if r == 0 else pltpu.roll(base, span - r, 0)
            for o in offs:
                yield o, win[o - r:o - r + SUB, :]


def _conv_fwd(u, w, bias, *, tm=256, ct=256):
    T, C = u.shape
    tm = min(tm, T)
    hb = tm // HALO
    lead = HALO - (CONV_W - 1)

    def body(cur_ref, halo_ref, w_ref, b_ref, o_ref, buf):
        t = pl.program_id(1)
        buf[0:HALO, :] = jnp.where(t > 0, halo_ref[...], 0.0)
        buf[HALO:, :] = cur_ref[...]
        for c0 in range(0, ct, SUB):
            for t0 in range(0, tm, SUB):
                acc = jnp.broadcast_to(b_ref[:, c0:c0 + SUB], (SUB, SUB))
                for o, win in _windows(buf, t0, c0, lead, lead + CONV_W - 1):
                    acc = acc + win * w_ref[o - lead:o - lead + 1, c0:c0 + SUB]
                o_ref[t0:t0 + SUB, c0:c0 + SUB] = acc

    return pl.pallas_call(
        body, name="conv_fwd",
        out_shape=jax.ShapeDtypeStruct((T, C), F32),
        grid=(C // ct, T // tm),
        in_specs=[pl.BlockSpec((tm, ct), lambda c, t: (t, c)),
                  pl.BlockSpec((HALO, ct), lambda c, t: (jnp.maximum(t * hb - 1, 0), c)),
                  pl.BlockSpec((CONV_TAPS_PAD, ct), lambda c, t: (0, c)),
                  pl.BlockSpec((1, ct), lambda c, t: (0, c))],
        out_specs=pl.BlockSpec((tm, ct), lambda c, t: (t, c)),
        scratch_shapes=[pltpu.VMEM((HALO + tm, ct), F32)],
        compiler_params=_params("parallel", "arbitrary"),
    )(u, u, w, bias)


def _conv_bwd(u, du1, w, *, tm=256, ct=256):
    T, C = u.shape
    tm = min(tm, T)
    hb = tm // HALO
    nt = T // tm
    last_halo = T // HALO - 1
    lead = HALO - (CONV_W - 1)

    def body(u_ref, uh_ref, d_ref, dh_ref, w_ref, du_ref, dw_ref, ubuf, dbuf, dw8):
        t = pl.program_id(1)
        ubuf[0:HALO, :] = jnp.where(t > 0, uh_ref[...], 0.0)
        ubuf[HALO:, :] = u_ref[...]
        dbuf[0:tm, :] = d_ref[...]
        dbuf[tm:, :] = jnp.where(t < nt - 1, dh_ref[...], 0.0)

        @pl.when(t == 0)
        def _():
            dw8[...] = jnp.zeros_like(dw8)

        for c0 in range(0, ct, SUB):
            for t0 in range(0, tm, SUB):
                acc = jnp.zeros((SUB, SUB), F32)
                for o, win in _windows(dbuf, t0, c0, 0, CONV_W - 1):
                    kk = CONV_W - 1 - o
                    acc = acc + win * w_ref[kk:kk + 1, c0:c0 + SUB]
                du_ref[t0:t0 + SUB, c0:c0 + SUB] = acc
                dsub = d_ref[t0:t0 + SUB, c0:c0 + SUB]
                for o, win in _windows(ubuf, t0, c0, lead, lead + CONV_W - 1):
                    kk = o - lead
                    dw8[8 * kk:8 * kk + 8, c0:c0 + SUB] += jnp.sum(
                        (dsub * win).reshape(SUB // 8, 8, SUB), axis=0)

        @pl.when(t == nt - 1)
        def _():
            dw_ref[...] = jnp.sum(dw8[...].reshape(CONV_TAPS_PAD, 8, ct), axis=1)

    return pl.pallas_call(
        body, name="conv_bwd",
        out_shape=[jax.ShapeDtypeStruct((T, C), F32), jax.ShapeDtypeStruct((CONV_TAPS_PAD, C), F32)],
        grid=(C // ct, nt),
        in_specs=[pl.BlockSpec((tm, ct), lambda c, t: (t, c)),
                  pl.BlockSpec((HALO, ct), lambda c, t: (jnp.maximum(t * hb - 1, 0), c)),
                  pl.BlockSpec((tm, ct), lambda c, t: (t, c)),
                  pl.BlockSpec((HALO, ct), lambda c, t: (jnp.minimum((t + 1) * hb, last_halo), c)),
                  pl.BlockSpec((CONV_TAPS_PAD, ct), lambda c, t: (0, c))],
        out_specs=[pl.BlockSpec((tm, ct), lambda c, t: (t, c)),
                   pl.BlockSpec((CONV_TAPS_PAD, ct), lambda c, t: (0, c))],
        scratch_shapes=[pltpu.VMEM((HALO + tm, ct), F32), pltpu.VMEM((tm + HALO, ct), F32),
                        pltpu.VMEM((CONV_TAPS_PAD * 8, ct), F32)],
        compiler_params=_params("parallel", "arbitrary"),
    )(u, u, du1, du1, w)


def _adamw(w, g, m, v, *, name, g_row=None):
    R, C = w.shape
    tr = _row_tile(R, max(8, (1 << 18) // C // 8 * 8))
    first = 0 if g_row is None else g_row // tr
    assert g_row is None or g_row % tr == 0

    def body(w_ref, g_ref, m_ref, v_ref, d_ref, nm_ref, nv_ref, *g_out):
        gv = g_ref[...]
        nm = ADAM_B1 * m_ref[...] + (1.0 - ADAM_B1) * gv
        nv = ADAM_B2 * v_ref[...] + (1.0 - ADAM_B2) * (gv * gv)
        m_hat = nm / (1.0 - ADAM_B1 ** ADAM_STEP)
        v_hat = nv / (1.0 - ADAM_B2 ** ADAM_STEP)
        d_ref[...] = -ADAM_LR * (m_hat / (jnp.sqrt(v_hat) + ADAM_EPS) + ADAM_WD * w_ref[...])
        nm_ref[...] = nm
        nv_ref[...] = nv
        for ref in g_out:
            ref[...] = gv

    spec = pl.BlockSpec((tr, C), lambda i: (i, 0))
    g_spec = pl.BlockSpec((tr, C), lambda i: (first + i, 0))
    n_out = 3 if g_row is None else 4
    return pl.pallas_call(
        body, name=name, out_shape=[jax.ShapeDtypeStruct((R, C), F32)] * n_out,
        grid=(R // tr,), in_specs=[spec, g_spec, spec, spec], out_specs=[spec] * n_out,
        compiler_params=_params("parallel"),
    )(w, g, m, v)


def _coords():
    return lax.axis_index("x"), lax.axis_index("y"), lax.axis_index("c")


def _remote(src, dst, ssem, rsem, to):
    return pltpu.make_async_remote_copy(src_ref=src, dst_ref=dst, send_sem=ssem, recv_sem=rsem,
                                        device_id=to, device_id_type=MESH)


def _half(c, rows):
    return pl.ds(pl.multiple_of(c * (rows // 2), 16), rows // 2)


def _gather_weights(wpack, cw):
    def body(w_ref, cw_ref, full_ref, cwf_ref, ssem, rsem):
        x, y, c = _coords()
        q = 2 * x + y
        sib = (x, y, 1 - c)
        chips = [(1 - x, y), (x, 1 - y), (1 - x, 1 - y)]
        mine, other = _half(c, w_ref.shape[0]), _half(1 - c, w_ref.shape[0])
        first = []
        for j, (px, py) in enumerate(chips):
            first.append(_remote(w_ref.at[mine], full_ref.at[q, mine], ssem.at[j], rsem.at[j], (px, py, c)))
            first.append(_remote(cw_ref, cwf_ref.at[q], ssem.at[6 + j], rsem.at[6 + j], (px, py, c)))
        for cp in first:
            cp.start()
        passed = []
        for j, (px, py) in enumerate(chips):
            pq = 2 * px + py
            _remote(w_ref.at[mine], full_ref.at[pq, mine], ssem.at[j], rsem.at[j], (px, py, c)).wait_recv()
            fw = _remote(full_ref.at[pq, mine], full_ref.at[pq, mine], ssem.at[3 + j], rsem.at[3 + j], sib)
            fw.start()
            passed.append(fw)
        for j, (px, py) in enumerate(chips):
            pq = 2 * px + py
            _remote(full_ref.at[pq, other], full_ref.at[pq, other], ssem.at[3 + j], rsem.at[3 + j], sib).wait_recv()
            _remote(cw_ref, cwf_ref.at[pq], ssem.at[6 + j], rsem.at[6 + j], (px, py, c)).wait_recv()
        for cp in first + passed:
            cp.wait_send()

    return pl.pallas_call(
        body, name="gather_weights",
        out_shape=[jax.ShapeDtypeStruct((N_CHIPS,) + wpack.shape, wpack.dtype),
                   jax.ShapeDtypeStruct((N_CHIPS,) + cw.shape, cw.dtype)],
        in_specs=[ANY, ANY], out_specs=[ANY, ANY],
        scratch_shapes=[pltpu.SemaphoreType.DMA((9,)), pltpu.SemaphoreType.DMA((9,))],
    )(wpack, cw)


SEM = pl.BlockSpec(memory_space=pltpu.SEMAPHORE)
HBM = pl.BlockSpec(memory_space=pltpu.HBM)
N_LATE = 6


def _late_peers(x, y, c):
    out = []
    for j, (px, py) in enumerate([(1 - x, y), (x, 1 - y), (1 - x, 1 - y)]):
        for t in (0, 1):
            out.append((2 * j + t, px, py, c if t == 0 else 1 - c))
    return out


def _late_gather_start(w, after):
    rows = w.shape[0]

    def body(w_ref, land_ref, after_ref, *outs):
        sems = outs[:2 * N_LATE]
        token = outs[2 * N_LATE + 2]
        x, y, c = _coords()
        q = 2 * x + y
        mine = _half(c, rows)
        for k, px, py, pc in _late_peers(x, y, c):
            _remote(w_ref.at[mine], land_ref.at[q, mine], sems[k], sems[N_LATE + k], (px, py, pc)).start()
        token[...] = jnp.zeros_like(token)

    res = pl.pallas_call(
        body, name="late_gather_start",
        out_shape=tuple([pltpu.SemaphoreType.DMA(())] * (2 * N_LATE)) + (
            pltpu.HBM(w.shape, w.dtype), pltpu.HBM((N_CHIPS,) + w.shape, w.dtype),
            jax.ShapeDtypeStruct((8, 128), F32)),
        in_specs=(HBM, HBM, ANY),
        out_specs=tuple([SEM] * (2 * N_LATE)) + (HBM, HBM, pl.BlockSpec(memory_space=pltpu.VMEM)),
        input_output_aliases={0: 2 * N_LATE, 1: 2 * N_LATE + 1},
        compiler_params=pltpu.CompilerParams(has_side_effects=pltpu.SideEffectType.DATAFLOW_SIDE_EFFECTING),
    )(pltpu.with_memory_space_constraint(w, pltpu.HBM),
      pltpu.with_memory_space_constraint(lax.empty((N_CHIPS,) + w.shape, w.dtype), pltpu.HBM), after)
    return res[:2 * N_LATE], res[2 * N_LATE], res[2 * N_LATE + 1], res[2 * N_LATE + 2]


def _late_gather_wait(sems, w_thru, land_thru, after):
    rows = w_thru.shape[0]

    def body(w_ref, land_ref, *rest):
        sems = rest[:2 * N_LATE]
        x, y, c = _coords()
        for k, px, py, pc in _late_peers(x, y, c):
            cp = _remote(w_ref.at[_half(pc, rows)], land_ref.at[2 * px + py, _half(pc, rows)],
                         sems[k], sems[N_LATE + k], (px, py, pc))
            cp.wait_send()
            cp.wait_recv()

    return pl.pallas_call(
        body, name="late_gather_wait",
        out_shape=(pltpu.HBM(w_thru.shape, w_thru.dtype), pltpu.HBM(land_thru.shape, land_thru.dtype)),
        in_specs=(HBM, HBM) + tuple([SEM] * (2 * N_LATE)) + (ANY,),
        out_specs=(HBM, HBM), input_output_aliases={0: 0, 1: 1},
        compiler_params=pltpu.CompilerParams(has_side_effects=pltpu.SideEffectType.DATAFLOW_SIDE_EFFECTING),
    )(w_thru, land_thru, *sems, after)[1]


N_PEERS = N_DEV - 1


def _peers(x, y, c):
    out = []
    for r in range(1, N_DEV):
        out.append((r - 1, 1 - x if r & 4 else x, 1 - y if r & 2 else y, 1 - c if r & 1 else c))
    return out


def _piece(g_ref, px, py, pc, whole):
    return g_ref if whole else g_ref.at[2 * px + py, _half(pc, g_ref.shape[1])]


def _reduce_start(g16, after, *, name, whole=False):
    def body(g_ref, land_ref, after_ref, *outs):
        sems = outs[:2 * N_PEERS]
        token = outs[2 * N_PEERS + 2]
        x, y, c = _coords()
        for k, px, py, pc in _peers(x, y, c):
            _remote(_piece(g_ref, px, py, pc, whole), land_ref.at[k], sems[k], sems[N_PEERS + k],
                    (px, py, pc)).start()
        token[...] = jnp.zeros_like(token)

    land_shape = (N_PEERS,) + (g16.shape if whole else (g16.shape[1] // 2, 1024))
    res = pl.pallas_call(
        body, name=name,
        out_shape=tuple([pltpu.SemaphoreType.DMA(())] * (2 * N_PEERS)) + (
            pltpu.HBM(g16.shape, g16.dtype), pltpu.HBM(land_shape, g16.dtype),
            jax.ShapeDtypeStruct((8, 128), F32)),
        in_specs=(HBM, HBM, ANY),
        out_specs=tuple([SEM] * (2 * N_PEERS)) + (HBM, HBM, pl.BlockSpec(memory_space=pltpu.VMEM)),
        input_output_aliases={0: 2 * N_PEERS, 1: 2 * N_PEERS + 1},
        compiler_params=pltpu.CompilerParams(has_side_effects=pltpu.SideEffectType.DATAFLOW_SIDE_EFFECTING),
    )(pltpu.with_memory_space_constraint(g16, pltpu.HBM),
      pltpu.with_memory_space_constraint(lax.empty(land_shape, g16.dtype), pltpu.HBM), after)
    return res[:2 * N_PEERS], res[2 * N_PEERS], res[2 * N_PEERS + 1], res[2 * N_PEERS + 2]


def _reduce_wait(sems, g_thru, land_thru, after, *, name, whole=False):
    def body(g_ref, land_ref, *rest):
        sems = rest[:2 * N_PEERS]
        x, y, c = _coords()
        for k, px, py, pc in _peers(x, y, c):
            cp = _remote(_piece(g_ref, px, py, pc, whole), land_ref.at[k], sems[k], sems[N_PEERS + k],
                         (px, py, pc))
            cp.wait_send()
            cp.wait_recv()

    return pl.pallas_call(
        body, name=name,
        out_shape=(pltpu.HBM(g_thru.shape, g_thru.dtype), pltpu.HBM(land_thru.shape, land_thru.dtype)),
        in_specs=(HBM, HBM) + tuple([SEM] * (2 * N_PEERS)) + tuple([ANY] * len(after)),
        out_specs=(HBM, HBM), input_output_aliases={0: 0, 1: 1},
        compiler_params=pltpu.CompilerParams(has_side_effects=pltpu.SideEffectType.DATAFLOW_SIDE_EFFECTING),
    )(g_thru, land_thru, *sems, *after)[1]


def _reduce_sum(place, g32, land, *, name):
    rows = g32.shape[1]
    nb = rows // 2 // SUM_TILE

    def body(place_ref, g_ref, l_ref, o_ref):
        acc = g_ref[...]
        for k in range(N_PEERS):
            acc = acc + l_ref[k].astype(F32)
        o_ref[...] = acc

    return pl.pallas_call(
        body, name=name,
        out_shape=jax.ShapeDtypeStruct((rows, 1024), F32),
        grid_spec=pltpu.PrefetchScalarGridSpec(
            num_scalar_prefetch=1, grid=(nb,),
            in_specs=[pl.BlockSpec((None, SUM_TILE, 1024), lambda i, s: (s[0], s[1] * nb + i, 0)),
                      pl.BlockSpec((N_PEERS, SUM_TILE, 1024), lambda i, s: (0, i, 0))],
            out_specs=pl.BlockSpec((SUM_TILE, 1024), lambda i, s: (s[1] * nb + i, 0))),
        compiler_params=_params("parallel"),
    )(place, g32, land)


def _join_halves(red, *, name):
    rows = red.shape[0]

    def body(r_ref, o_ref, ssem, rsem):
        x, y, c = _coords()
        cp = _remote(r_ref.at[_half(c, rows)], o_ref.at[_half(c, rows)], ssem, rsem, (x, y, 1 - c))
        cp.start()
        _remote(r_ref.at[_half(1 - c, rows)], o_ref.at[_half(1 - c, rows)], ssem, rsem,
                (x, y, 1 - c)).wait_recv()
        cp.wait_send()

    return pl.pallas_call(
        body, name=name,
        out_shape=jax.ShapeDtypeStruct(red.shape, F32),
        in_specs=[ANY], out_specs=ANY, input_output_aliases={0: 0},
        scratch_shapes=[pltpu.SemaphoreType.DMA, pltpu.SemaphoreType.DMA],
    )(red)


def _sum_small(me, svec, land):
    def body(me_ref, s_ref, l_ref, o_ref):
        mine = me_ref[0]
        acc = None
        for d in range(N_DEV):
            theirs = l_ref[jnp.maximum(jnp.bitwise_xor(mine, d) - 1, 0)]
            v = jnp.where(mine == d, s_ref[...], theirs)
            acc = v if acc is None else acc + v
        o_ref[...] = acc

    return pl.pallas_call(
        body, name="sum_small",
        out_shape=jax.ShapeDtypeStruct(svec.shape, F32),
        grid_spec=pltpu.PrefetchScalarGridSpec(
            num_scalar_prefetch=1, grid=(1,),
            in_specs=[pl.BlockSpec(svec.shape, lambda i, s: (0, 0)),
                      pl.BlockSpec(land.shape, lambda i, s: (0, 0, 0))],
            out_specs=pl.BlockSpec(svec.shape, lambda i, s: (0, 0))),
        compiler_params=_params("arbitrary"),
    )(me, svec, land)


def _pack_early_shards(wts):
    parts = [wts["w_in"][0].T.astype(BF16)] + [wts[n].astype(BF16).reshape(-1, 1024) for n, _ in EARLY[1:]]
    used = sum(r for _, r in EARLY)
    return jnp.concatenate(parts + [jnp.zeros((EARLY_ROWS - used, 1024), BF16)], axis=0)


def _pack_late_shards(wts):
    parts, at = [], 0
    for n, off, rows in LATE:
        if off > at:
            parts.append(jnp.zeros((off - at, 1024), BF16))
        parts.append(wts[n].astype(BF16).reshape(rows, 1024))
        at = off + rows
    return jnp.concatenate(parts, axis=0)


def _unpack_early(full):
    w = {}
    w["w_in_t"] = jnp.concatenate(
        [full[0, N_SMALL:1200]] + [full[p, :1200] for p in range(1, N_CHIPS)]
        + [full[0, :N_SMALL], jnp.zeros((D_MODEL - N_SMALL, 1024), full.dtype)], axis=0)
    w["w_uq"] = full[:, 1200:1344].reshape(4, 384, 384).transpose(1, 0, 2).reshape(384, 1536)
    w["w_uk"] = full[:, 1344:1408].reshape(256, 1024)
    w["w_uv"] = full[:, 1408:1472].reshape(256, 1024)
    return w


def _pack_early_grads(g):
    gt = g["w_in_t"]
    rest = 1200 - N_SMALL
    w_in = jnp.stack([jnp.concatenate([gt[4 * D_MODEL:4 * D_MODEL + N_SMALL], gt[:rest]], axis=0)]
                     + [gt[rest + 1200 * (p - 1):rest + 1200 * p] for p in range(1, N_CHIPS)])
    parts = [
        w_in,
        g["w_uq"].reshape(384, 4, 384).transpose(1, 0, 2).reshape(4, 144, 1024),
        g["w_uk"].reshape(4, 64, 1024),
        g["w_uv"].reshape(4, 64, 1024),
        jnp.zeros((4, EARLY_ROWS - sum(r for _, r in EARLY), 1024), F32),
    ]
    return jnp.concatenate(parts, axis=1)


def _pad_w_uq(w):
    w = w.reshape(Q_RANK, N_HEADS, NOPE + ROPE)
    return jnp.pad(w, ((0, 0), (0, 0), (0, QK_PAD - NOPE - ROPE))).reshape(Q_RANK, N_HEADS * QK_PAD)


def _local_step(xs, pos, tgt, W, late_weights, late_bufs, send_late_grads, send_early_grads, cw, vec):
    D = D_MODEL
    w_in_t = W["w_in_t"]
    w_uq = _pad_w_uq(W["w_uq"])
    w_ukv = jnp.concatenate([W["w_uk"], W["w_uv"]], axis=1)
    inv = ROPE_THETA ** (-jnp.arange(0, ROPE, 2, dtype=F32) / ROPE)
    inv_freq = jnp.concatenate([inv, inv, jnp.zeros((128 - ROPE,), F32)])[None, :]

    (h,) = _rowwise(lambda x, g: (_rms(x, g),), [(xs, D, 0)], [vec["norm_mix_pre"]], [(D, BF16)],
                    name="rms_pre")
    z = _mm(h, w_in_t, tb=True, out_dtype=BF16, name="mm_in")

    def qkv_prep(zs, p, qn, kvn, invf):
        ang = p.astype(F32) * invf
        cosv, sinv = jnp.cos(ang), jnp.sin(ang)
        ln = lax.broadcasted_iota(jnp.int32, ang.shape, 1)
        cs = jnp.where(ln < ROPE, cosv, 0.0)
        sa = jnp.where((ln >= ROPE // 2) & (ln < ROPE), sinv, 0.0)
        sb = jnp.where(ln < ROPE // 2, -sinv, 0.0)
        cqn = _rms(zs[:, :Q_RANK], qn)
        ckvn = _rms(zs[:, Q_RANK:Q_RANK + KV_RANK], kvn)
        kr = _rope(zs[:, 640:768], cs, sa, sb)
        return cqn, ckvn, kr, cs, sa, sb

    cqn, ckvn, krot, cs, sa, sb = _rowwise(
        qkv_prep, [(z, D, Z_S), (pos, 1, 0)], [vec["q_norm"], vec["kv_norm"], inv_freq],
        [(Q_RANK, BF16), (KV_RANK, BF16), (128, BF16), (128, F32), (128, F32), (128, F32)],
        name="qkv_prep")
    q_pre = _mm(cqn, w_uq, out_dtype=BF16, name="mm_uq")
    kv = _mm(ckvn, w_ukv, out_dtype=BF16, name="mm_ukv")

    def qk_rope(qp, kn, vv, kr, cs_, sa_, sb_):
        qs, ks, vs = [], [], []
        ones = jnp.ones((qp.shape[0], NOPE), F32)
        for hd in range(N_HEADS):
            qs.append(qp[:, hd * QK_PAD:hd * QK_PAD + NOPE])
            qs.append(_rope(qp[:, hd * QK_PAD + NOPE:(hd + 1) * QK_PAD], cs_, sa_, sb_))
            ks.append(kn[:, hd * NOPE:(hd + 1) * NOPE])
            ks.append(kr.astype(F32))
            vs.append(vv[:, hd * NOPE:(hd + 1) * NOPE])
            vs.append(ones)
        return (jnp.concatenate(qs, axis=1) * Q_SCALE, jnp.concatenate(ks, axis=1),
                jnp.concatenate(vs, axis=1))

    q, k, v = _rowwise(
        qk_rope, [(q_pre, 2048, 0), (kv, 1024, 0), (kv, 1024, 1), (krot, 128, 0), (cs, 128, 0),
                  (sa, 128, 0), (sb, 128, 0)], [],
        [(2048, BF16), (2048, BF16), (2048, BF16)], name="qk_rope")
    attn, lse = _flash_fwd(q, k, v)
    LW = late_weights(attn)
    y_attn = LW.fwd(attn, "w_o_attn", out_dtype=BF16, name="mm_o_attn")

    (u0,) = _rowwise(lambda a, b: (a * _sigmoid(b),), [(z, D, Z_A), (z, D, Z_B)], [], [(D, F32)], name="glu")
    u1 = _conv_fwd(u0, cw, vec["conv_b"])

    def ln_parts(u, g, b):
        mu = jnp.mean(u, axis=-1, keepdims=True)
        xc = u - mu
        rs = lax.rsqrt(jnp.mean(xc * xc, axis=-1, keepdims=True) + EPS)
        nh = xc * rs
        return nh, rs, nh * g + b

    def ln_silu(u, g, b):
        y = ln_parts(u, g, b)[2]
        return (y * _sigmoid(y),)

    (u3,) = _rowwise(ln_silu, [(u1, D, 0)], [vec["conv_ln_g"], vec["conv_ln_b"]], [(D, BF16)],
                     name="ln_silu")
    y_conv = LW.fwd(u3, "w_pw2", out_dtype=BF16, name="mm_pw2")

    def merge(ga, gc, ya, yc, b2):
        return (_sigmoid(ga) * ya + _sigmoid(gc) * (yc + b2),)

    (merged,) = _rowwise(merge, [(z, D, Z_GA), (z, D, Z_GC), (y_attn, D, 0), (y_conv, D, 0)], [vec["b_pw2"]],
                         [(D, BF16)], name="merge")

    def post1(m, x, g1, g2):
        x1 = x + _rms(m, g1)
        return m, x1, _rms(x1, g2)

    mo, x1, h2 = LW.fwd(merged, "w_out", name="mm_out", tm=512, epilogue=post1, out_dtypes=(F32, F32, BF16),
                        extras=[xs, vec["norm_mix_post"], vec["norm_mlp_pre"]])

    def sqrelu(acc):
        r = jnp.maximum(acc, 0.0)
        return r * r, r

    a2, r1 = LW.fwd(h2, "w_ff1", name="mm_ff1", epilogue=sqrelu, out_dtypes=(BF16, BF16))

    def post2(f_, x1_, t_, g):
        e = x1_ + _rms(f_, g) - t_
        dy = e * (1.0 / D)
        df, dg = _rms_bwd(f_, g, dy)
        loss = jnp.broadcast_to(jnp.sum(e * e, keepdims=True) * (0.5 / D), (1, 128))
        return dy, df, dg, loss

    dy, df, g_norm_mlp_post, loss = LW.fwd(a2, "w_ff2", name="mm_ff2", epilogue=post2, out_dtypes=(F32, BF16),
                                           extras=[x1, tgt, vec["norm_mlp_post"]], reds=(D, 128))
    late_bufs = _late_grad(a2, df, "w_ff2", late_bufs, name="mm_dw_ff2")
    df1 = LW.bwd(df, "w_ff2", name="mm_df1", extras=[r1], out_dtypes=(BF16,),
              epilogue=lambda acc, r: (acc * (2.0 * r.astype(F32)),))
    late_bufs = _late_grad(h2, df1, "w_ff1", late_bufs, name="mm_dw_ff1")

    def bwd_mid(dh2_, dy_, x1_, m, g2, g1):
        d1, dg2 = _rms_bwd(x1_, g2, dh2_)
        dx1_ = dy_ + d1
        dm, dg1 = _rms_bwd(m, g1, dx1_)
        return dx1_, dm, dg2, dg1

    dx1, dmo, g_norm_mlp_pre, g_norm_mix_post = LW.bwd(
        df1, "w_ff1", name="mm_dh2", tm=512, epilogue=bwd_mid, out_dtypes=(F32, BF16), reds=(D, D),
        extras=[dy, x1, mo, vec["norm_mlp_pre"], vec["norm_mix_post"]])
    late_bufs = _late_grad(merged, dmo, "w_out", late_bufs, name="mm_dw_out")
    dmerged = LW.bwd(dmo, "w_out", out_dtype=BF16, name="mm_dmerged")

    def dmerge(dm, ga, gc, ya, yc, b2):
        sga, sgc = _sigmoid(ga), _sigmoid(gc)
        dya = dm * sga
        dyc = dm * sgc
        dga = dm * ya * sga * (1.0 - sga)
        dgc = dm * (yc + b2) * sgc * (1.0 - sgc)
        return dya, dyc, jnp.concatenate([dga, dgc], axis=1), jnp.sum(dyc, axis=0, keepdims=True)

    dz = lax.empty((xs.shape[0], Z_COLS), BF16)
    dya, dyc, dz, g_b_pw2 = _rowwise(
        dmerge, [(dmerged, D, 0), (z, D, Z_GA), (z, D, Z_GC), (y_attn, D, 0), (y_conv, D, 0)],
        [vec["b_pw2"]], [(D, BF16), (D, BF16), (2 * D, dz, Z_GA // 2)], [(1, D)], name="dmerge")

    late_bufs = _late_grad(attn, dya, "w_o_attn", late_bufs, name="mm_dw_o_attn")
    late_bufs = _late_grad(u3, dyc, "w_pw2", late_bufs, name="mm_dw_pw2")
    late_token = send_late_grads(late_bufs)

    dattn = LW.bwd(dya, "w_o_attn", out_dtype=BF16, name="mm_dattn")

    def delta_fn(do_, o_):
        do32 = do_.astype(F32)
        pr = do32 * o_.astype(F32)
        ln = lax.broadcasted_iota(jnp.int32, (pr.shape[0], NOPE), 1)
        cols = []
        for hd in range(N_HEADS):
            dl = jnp.sum(pr[:, hd * NOPE:(hd + 1) * NOPE], axis=-1, keepdims=True)
            hi = dl.astype(BF16).astype(F32)
            cols.append(do32[:, hd * NOPE:(hd + 1) * NOPE])
            cols.append(jnp.where(ln == 0, -hi, jnp.where(ln == 1, hi - dl, 0.0)))
        return (jnp.concatenate(cols, axis=1),)

    (do_ext,) = _rowwise(delta_fn, [(dattn, D, 0), (attn, D, 0)], [], [(2048, BF16)], name="attn_delta",
                         after=[late_token])
    dq, dk, dv = _flash_bwd(q, k, v, do_ext, lse)

    def unrope(dq_, dk_, dv_, cs_, sa_, sb_):
        dq_, dk_ = dq_.astype(F32), dk_.astype(F32)
        qs, kn = [], []
        dkr = jnp.zeros_like(cs_)
        for hd in range(N_HEADS):
            qs.append(dq_[:, hd * QK_PAD:hd * QK_PAD + NOPE] * SCALE)
            qs.append(_rope_t(dq_[:, hd * QK_PAD + NOPE:(hd + 1) * QK_PAD] * SCALE, cs_, sa_, sb_))
            kn.append(dk_[:, hd * QK_PAD:hd * QK_PAD + NOPE] * K_UNSCALE)
            dkr = dkr + dk_[:, hd * QK_PAD + NOPE:(hd + 1) * QK_PAD]
        return (jnp.concatenate(qs, axis=1), jnp.concatenate(kn + [dv_], axis=1),
                _rope_t(dkr * K_UNSCALE, cs_, sa_, sb_))

    dq_pre, dkv, dkr = _rowwise(
        unrope, [(dq, 2048, 0), (dk, 2048, 0), (dv, 1024, 0), (cs, 128, 0), (sa, 128, 0), (sb, 128, 0)], [],
        [(2048, BF16), (2048, BF16), (128, F32)], name="unrope")
    g_w_uq = _mm(cqn, dq_pre, ta=True, name="mm_dw_uq")
    dcqn = _mm(dq_pre, w_uq, tb=True, out_dtype=BF16, name="mm_dcqn")
    g_w_ukv = _mm(ckvn, dkv, ta=True, name="mm_dw_ukv")
    dckvn = _mm(dkv, w_ukv, tb=True, out_dtype=BF16, name="mm_dckvn")

    def small_bwd(zs, dcq_, dckv_, dkr_, qn, kvn):
        dcq, dqn = _rms_bwd(zs[:, :Q_RANK], qn, dcq_)
        dckv, dkvn = _rms_bwd(zs[:, Q_RANK:Q_RANK + KV_RANK], kvn, dckv_)
        pad = jnp.zeros((zs.shape[0], D - 768), F32)
        return jnp.concatenate([dcq, dckv, dkr_, pad], axis=1), dqn, dkvn

    dz, g_q_norm, g_kv_norm = _rowwise(
        small_bwd, [(z, D, Z_S), (dcqn, Q_RANK, 0), (dckvn, KV_RANK, 0), (dkr, 128, 0)],
        [vec["q_norm"], vec["kv_norm"]], [(D, dz, Z_S)], [(1, Q_RANK), (1, KV_RANK)], name="small_bwd")

    def ln_silu_bwd(d3, u, g, b):
        nh, rs, y = ln_parts(u, g, b)
        sg = _sigmoid(y)
        dyv = d3 * (sg * (1.0 + y * (1.0 - sg)))
        dnh = dyv * g
        du = rs * (dnh - jnp.mean(dnh, axis=-1, keepdims=True)
                   - nh * jnp.mean(dnh * nh, axis=-1, keepdims=True))
        return (du, jnp.sum(dyv * nh, axis=0, keepdims=True), jnp.sum(dyv, axis=0, keepdims=True),
                jnp.sum(du, axis=0, keepdims=True))

    du1, g_ln_g, g_ln_b, g_conv_b = LW.bwd(
        dyc, "w_pw2", name="mm_du3", tm=512, epilogue=ln_silu_bwd, out_dtypes=(F32,), reds=(D, D, D),
        extras=[u1, vec["conv_ln_g"], vec["conv_ln_b"]])
    du0, g_conv_w = _conv_bwd(u0, du1, cw)

    def glu_bwd(d0, a, b):
        sg = _sigmoid(b)
        return (jnp.concatenate([d0 * sg, d0 * a * sg * (1.0 - sg)], axis=1),)

    (dz,) = _rowwise(glu_bwd, [(du0, D, 0), (z, D, Z_A), (z, D, Z_B)], [], [(2 * D, dz, Z_A // 2)],
                     name="glu_bwd")
    g_w_in_t = _mm(dz, h, ta=True, name="mm_dw_in")
    early_token = send_early_grads({
        "w_in_t": g_w_in_t,
        "w_uq": g_w_uq.reshape(Q_RANK, N_HEADS, QK_PAD)[:, :, :NOPE + ROPE].reshape(Q_RANK, 1536),
        "w_uk": g_w_ukv[:, :1024], "w_uv": g_w_ukv[:, 1024:]})

    def final(dh_, dx1_, x, g):
        d, dg = _rms_bwd(x, g, dh_)
        return dx1_ + d, dg

    grad_x, g_norm_mix_pre = _mm(
        dz, w_in_t, name="mm_dh", after=[early_token], tiles=(_tile(xs.shape[0], 512), D, Z_COLS // 2),
        epilogue=final, out_dtypes=(F32,), reds=(D,), extras=[dx1, xs, vec["norm_mix_pre"]])

    gs = {
        "norm_mix_pre": g_norm_mix_pre, "conv_b": g_conv_b, "conv_ln_g": g_ln_g, "conv_ln_b": g_ln_b,
        "b_pw2": g_b_pw2, "norm_mix_post": g_norm_mix_post, "norm_mlp_pre": g_norm_mlp_pre,
        "norm_mlp_post": g_norm_mlp_post, "q_norm": g_q_norm, "kv_norm": g_kv_norm,
    }
    return grad_x, gs, g_conv_w, loss


def _row1024(a):
    return jnp.pad(a, ((0, 0), (0, 1024 - a.shape[1])))


BIG = tuple(n for n, _ in EARLY) + tuple(n for n, _, _ in LATE)
ORDER = ("norm_mix_pre", "w_in", "q_norm", "w_uq", "kv_norm", "w_uk", "w_uv", "w_o_attn", "conv_w",
         "conv_b", "conv_ln_g", "conv_ln_b", "w_pw2", "b_pw2", "w_out", "norm_mix_post", "norm_mlp_pre",
         "w_ff1", "w_ff2", "norm_mlp_post")


def kernel(x, positions, norm_mix_pre, w_in, q_norm, w_uq, kv_norm, w_uk, w_uv, w_o_attn, conv_w, conv_b, conv_ln_g, conv_ln_b, w_pw2, b_pw2, w_out, norm_mix_post, norm_mlp_pre, w_ff1, w_ff2, norm_mlp_post, loss_target, m_norm_mix_pre, m_w_in, m_q_norm, m_w_uq, m_kv_norm, m_w_uk, m_w_uv, m_w_o_attn, m_conv_w, m_conv_b, m_conv_ln_g, m_conv_ln_b, m_w_pw2, m_b_pw2, m_w_out, m_norm_mix_post, m_norm_mlp_pre, m_w_ff1, m_w_ff2, m_norm_mlp_post, v_norm_mix_pre, v_w_in, v_q_norm, v_w_uq, v_kv_norm, v_w_uk, v_w_uv, v_w_o_attn, v_conv_w, v_conv_b, v_conv_ln_g, v_conv_ln_b, v_w_pw2, v_b_pw2, v_w_out, v_norm_mix_post, v_norm_mlp_pre, v_w_ff1, v_w_ff2, v_norm_mlp_post):
    args = dict(locals())
    wts = {n: args[n] for n in ORDER}
    mom = {n: args["m_" + n] for n in ORDER}
    var = {n: args["v_" + n] for n in ORDER}
    T = x.shape[1]
    place = jnp.stack([2 * lax.axis_index("x") + lax.axis_index("y"), lax.axis_index("c")]).astype(jnp.int32)

    w_early = _pack_early_shards(wts)
    w_late = _pack_late_shards(wts)
    cw_own = jnp.pad(conv_w.reshape(CONV_W, -1), ((0, CONV_TAPS_PAD - CONV_W), (0, 0)))
    full, cw_all = _gather_weights(w_early, cw_own)
    full = lax.dynamic_update_slice(full, w_early[None], (place[0], 0, 0))
    cw_all = lax.dynamic_update_slice(cw_all, cw_own[None], (place[0], 0, 0))
    W = _unpack_early(full)
    cw = cw_all.transpose(1, 0, 2).reshape(CONV_TAPS_PAD, D_MODEL)
    sems, w_thru, land, token = _late_gather_start(w_late, full)
    vec = {n: wts[n] for n in SMALL_VECS}
    vec["norm_mix_pre"] = vec["norm_mix_pre"] + token[:1, :1]

    def late_weights(after):
        landed = _late_gather_wait(sems, w_thru, land, after)
        return _LateWeights(lax.dynamic_update_slice(landed, w_late[None], (place[0], 0, 0)))

    sent = {}

    def send_late_grads(bufs):
        sent["late"] = (bufs[0],) + _reduce_start(bufs[1], place, name="late_grads_start")
        return sent["late"][4]

    def send_early_grads(g):
        g32 = _pack_early_grads(g)
        sent["early"] = (g32,) + _reduce_start(g32.astype(BF16), place, name="early_grads_start")
        return sent["early"][4]

    late_bufs = (lax.empty((N_CHIPS, LATE_ROWS, 1024), F32), lax.empty((N_CHIPS, LATE_ROWS, 1024), BF16))
    grad_x, gs, g_conv_w, loss = _local_step(x[0], positions.reshape(T, 1), loss_target[0], W, late_weights,
                                             late_bufs, send_late_grads, send_early_grads, cw, vec)

    def finish(group, after):
        g32, sems_, g_thru, land_, _ = sent[group]
        landed = _reduce_wait(sems_, g_thru, land_, after, name=group + "_grads_wait")
        red = _reduce_sum(place, g32, landed, name=group + "_grads_sum")
        return _join_halves(red, name=group + "_grads_join")

    def as2d(a):
        return a.reshape(-1, a.shape[-1]) if a.ndim != 4 or a.shape[2] == 1 else a.reshape(a.shape[1], -1)

    grads, delta, new_m, new_v = {}, {}, {}, {}

    def update(n):
        shp = wts[n].shape
        outs = _adamw(as2d(wts[n]), as2d(grads[n]), as2d(mom[n]), as2d(var[n]), name="adamw_" + n)
        delta[n], new_m[n], new_v[n] = (o.reshape(shp) for o in outs)

    g_late = finish("late", [sent["early"][4]])
    for n, off, rows in LATE:
        shp = wts[n].shape
        outs = _adamw(as2d(wts[n]), g_late, as2d(mom[n]), as2d(var[n]), g_row=off, name="adamw_" + n)
        delta[n], new_m[n], new_v[n], grads[n] = (o.reshape(shp) for o in outs)

    svec = jnp.concatenate(
        [_row1024(gs[n]) for n in SMALL_VECS] + [_row1024(loss)]
        + [jnp.zeros((SMALL_CONVW_ROW - SMALL_LOSS_ROW - 1, 1024), F32), g_conv_w], axis=0)
    small_sent = _reduce_start(svec, place, name="small_grads_start", whole=True)

    g_early = finish("early", [grad_x, delta["w_ff2"]])
    grads["w_in"] = g_early[:1200].T[None]
    off = 1200
    for n, rows in EARLY[1:]:
        grads[n] = g_early[off:off + rows].reshape(wts[n].shape)
        off += rows
    for n, _ in EARLY:
        update(n)

    landed = _reduce_wait(small_sent[0], small_sent[1], small_sent[2], [delta["w_in"]],
                          name="small_grads_wait", whole=True)
    ssum = _sum_small((2 * place[0] + place[1])[None], svec, landed)
    for r, n in enumerate(SMALL_VECS):
        grads[n] = ssum[r:r + 1, :wts[n].shape[1]]
    cw_sum = ssum[SMALL_CONVW_ROW:SMALL_CONVW_ROW + CONV_W]
    grads["conv_w"] = lax.dynamic_slice_in_dim(cw_sum, place[0] * 256, 256, axis=1).reshape(conv_w.shape)
    loss_out = ssum[SMALL_LOSS_ROW, 0]
    update("conv_w")
    small = [jnp.concatenate([_row1024(t[n]) for n in SMALL_VECS], axis=0) for t in (wts, grads, mom, var)]
    outs = _adamw(*small, name="adamw_small")
    for r, n in enumerate(SMALL_VECS):
        delta[n], new_m[n], new_v[n] = (o[r:r + 1, :wts[n].shape[1]] for o in outs)

    return (loss_out, grad_x[None], *[grads[n] for n in ORDER], *[delta[n] for n in ORDER],
            *[new_m[n] for n in ORDER], *[new_v[n] for n in ORDER])
```

```python
import functools

import numpy as np
import jax
import jax.numpy as jnp
from jax import lax
from jax.experimental import pallas as pl
from jax.experimental.pallas import tpu as pltpu

F32 = jnp.float32
BF16 = jnp.bfloat16
MESH = pl.DeviceIdType.MESH

D_MODEL = 1024
N_HEADS = 8
NOPE = 128
ROPE = 64
QK_PAD = 256
Q_RANK = 384
KV_RANK = 256
CONV_W = 31
CONV_TAPS_PAD = 32
HALO = 32
D_FF = 4096
EPS = 1e-6
ROPE_THETA = 10000.0
SCALE = float((NOPE + ROPE) ** -0.5)
Q_SCALE = SCALE * float(np.log2(np.e))
K_UNSCALE = float(np.log(2.0))
Z_A, Z_B, Z_GA, Z_GC, Z_S = range(5)
Z_COLS = 5 * D_MODEL
N_SMALL = Q_RANK + KV_RANK + ROPE
N_CHIPS = 4
N_DEV = 8

ADAM_LR = 0.001
ADAM_B1 = 0.9
ADAM_B2 = 0.999
ADAM_EPS = 1e-08
ADAM_WD = 0.01
ADAM_STEP = 10

VMEM_LIMIT = 48 * 1024 * 1024
NEG = -1e30

EARLY = (("w_in", 1200), ("w_uq", 144), ("w_uk", 64), ("w_uv", 64))
EARLY_ROWS = 1536
LATE = (("w_ff1", 0, 1024), ("w_o_attn", 1024, 256), ("w_pw2", 1536, 256), ("w_out", 1792, 256),
        ("w_ff2", 2048, 1024))
LATE_OFF = {n: off for n, off, _ in LATE}
LATE_ROWS = 3072
SUM_TILE = 256

SMALL_VECS = ("norm_mix_pre", "conv_b", "conv_ln_g", "conv_ln_b", "b_pw2", "norm_mix_post",
              "norm_mlp_pre", "norm_mlp_post", "q_norm", "kv_norm")
SMALL_LOSS_ROW = 10
SMALL_CONVW_ROW = 16
SMALL_ROWS = 48


ANY = pl.BlockSpec(memory_space=pl.ANY)


def _params(*sem):
    return pltpu.CompilerParams(dimension_semantics=sem, vmem_limit_bytes=VMEM_LIMIT)


def _tile(n, cap):
    if n <= cap:
        return n
    t = (cap // 128) * 128
    while n % t:
        t -= 128
    return t


def _row_tile(n, cap):
    if n <= cap:
        return n
    t = (cap // 8) * 8
    while n % t:
        t -= 8
    return t


MM_VMEM_BUDGET = 40 * 1024 * 1024


def _mm_tiles(M, N, K, in_bytes, out_bytes):
    tn = _tile(N, 1024)
    tk = K
    while True:
        for cap in (1024, 512, 256):
            tm = _tile(M, cap)
            need = 2 * tk * (tm * in_bytes[0] + tn * in_bytes[1]) + 2 * tm * tn * out_bytes
            need += tm * tn * 4 if tk < K else 0
            if need <= MM_VMEM_BUDGET:
                return tm, tn, tk
        tk = _tile(K, tk - 128)


def _mm(a, b, *, name, ta=False, tb=False, out_dtype=F32, extras=(), epilogue=None, out_dtypes=None,
        after=(), shape=None, tiles=None, b_spec=None, reds=()):
    if shape is None:
        M, K = (a.shape[1], a.shape[0]) if ta else a.shape
        N = b.shape[0] if tb else b.shape[1]
        assert K == (b.shape[1] if tb else b.shape[0]), (a.shape, b.shape, ta, tb)
    else:
        M, N, K = shape
    outs = tuple(out_dtypes or (out_dtype,))
    placed = [o for o in outs if isinstance(o, tuple)]
    ex_arrays = [e[0] if isinstance(e, tuple) else e for e in extras]
    if tiles is None:
        out_bytes = sum(jnp.dtype(d).itemsize for d in outs) + sum(e.dtype.itemsize for e in ex_arrays)
        tiles = _mm_tiles(M, N, K, (a.dtype.itemsize, b.dtype.itemsize), out_bytes)
    tm, tn, tk = tiles
    nk = K // tk
    dims = (((0 if ta else 1,), (1 if tb else 0,)), ((), ()))
    n_ex, n_out = len(extras), len(outs)
    n_pass = len(after) + len(placed)
    assert not reds or tn == N

    def body(*refs):
        a_ref, b_ref = refs[:2]
        ex_refs = refs[2:2 + n_ex]
        o_refs = refs[2 + n_ex + n_pass:2 + n_ex + n_pass + n_out]
        red_refs = refs[2 + n_ex + n_pass + n_out:2 + n_ex + n_pass + n_out + len(reds)]
        first_rows, k = pl.program_id(0) == 0, pl.program_id(2)
        bv = b_ref[...]
        if bv.ndim == 3:
            bv = bv.reshape(-1, bv.shape[-1])
        bv = bv.astype(BF16)

        part = lax.dot_general(a_ref[...].astype(BF16), bv, dims, preferred_element_type=F32)

        def finish(acc):
            ex = [r[...] for r in ex_refs]
            ex = [e.astype(F32) if e.dtype == BF16 else e for e in ex]
            res = epilogue(acc, *ex) if epilogue else (acc,) * n_out
            for o_ref, r in zip(o_refs, res):
                o_ref[...] = r.astype(o_ref.dtype)
            for red_ref, r in zip(red_refs, res[n_out:]):
                red_ref[...] += r

        if reds:
            @pl.when(first_rows & (k == 0))
            def _():
                for red_ref in red_refs:
                    red_ref[...] = jnp.zeros_like(red_ref)

        if nk == 1:
            finish(part)
        else:
            acc_ref = refs[-1]

            @pl.when(k == 0)
            def _():
                acc_ref[...] = part

            @pl.when((k > 0) & (k < nk - 1))
            def _():
                acc_ref[...] += part

            @pl.when(k == nk - 1)
            def _():
                finish(acc_ref[...] + part)

    a_spec = (pl.BlockSpec((tk, tm), lambda i, j, k: (k, i)) if ta
              else pl.BlockSpec((tm, tk), lambda i, j, k: (i, k)))
    if b_spec is None:
        b_spec = (pl.BlockSpec((tn, tk), lambda i, j, k: (j, k)) if tb
                  else pl.BlockSpec((tk, tn), lambda i, j, k: (k, j)))
    o_spec = pl.BlockSpec((tm, tn), lambda i, j, k: (i, j))
    row_spec = pl.BlockSpec((1, tn), lambda i, j, k: (0, j))
    ex_specs = []
    for e in extras:
        if isinstance(e, tuple):
            ex_specs.append(pl.BlockSpec((tm, tn), functools.partial(lambda i, j, k, cb: (i, cb), cb=e[1])))
        else:
            ex_specs.append(row_spec if e.shape[0] == 1 else o_spec)
    out_shape, out_specs, aliases = [], [], {}
    for o, out in enumerate(outs):
        if isinstance(out, tuple):
            out_shape.append(jax.ShapeDtypeStruct(out[0].shape, out[0].dtype))
            out_specs.append(out[1])
            aliases[2 + n_ex + len(after) + len(aliases)] = o
        else:
            out_shape.append(jax.ShapeDtypeStruct((M, N), out))
            out_specs.append(o_spec)
    passed = list(after) + [buf for buf, _ in placed]
    out_shape = out_shape + [jax.ShapeDtypeStruct((1, w), F32) for w in reds]
    out_specs = out_specs + [pl.BlockSpec((1, w), lambda i, j, k: (0, 0)) for w in reds]
    res = pl.pallas_call(
        body, name=name, out_shape=out_shape,
        grid=(M // tm, N // tn, nk),
        in_specs=[a_spec, b_spec] + ex_specs + [ANY] * len(passed),
        out_specs=out_specs, input_output_aliases=aliases,
        scratch_shapes=[pltpu.VMEM((tm, tn), F32)] if nk > 1 else [],
        compiler_params=_params("arbitrary" if reds else "parallel", "parallel", "arbitrary"),
    )(a, b, *ex_arrays, *passed)
    return res[0] if len(res) == 1 else res


class _LateWeights:
    def __init__(self, pack):
        self.pack = pack

    def fwd(self, x, n, *, name, tm=1024, **kw):
        T, off = x.shape[0], LATE_OFF[n]
        if n == "w_ff1":
            return _mm(x, self.pack, name=name, shape=(T, 4096, 1024), tiles=(_tile(T, tm), 1024, 1024),
                       b_spec=pl.BlockSpec((None, 1024, 1024), lambda i, j, k: (j, 0, 0)), **kw)
        if n == "w_ff2":
            view = self.pack.reshape(N_CHIPS, LATE_ROWS // 1024, 1024, 1024)
            return _mm(x, view, name=name, shape=(T, 1024, 4096), tiles=(_tile(T, min(tm, 512)), 1024, 4096),
                       b_spec=pl.BlockSpec((N_CHIPS, None, 1024, 1024), lambda i, j, k: (0, off // 1024, 0, 0)),
                       **kw)
        return _mm(x, self.pack.reshape(N_CHIPS, LATE_ROWS // 256, 256, 1024), name=name,
                   shape=(T, 1024, 1024), tiles=(_tile(T, tm), 1024, 1024), b_spec=self._whole(off), **kw)

    @staticmethod
    def _whole(off):
        return pl.BlockSpec((N_CHIPS, None, 256, 1024), lambda i, j, k: (0, off // 256, 0, 0))

    def bwd(self, dy, n, *, name, tm=1024, **kw):
        T, off = dy.shape[0], LATE_OFF[n]
        tm = _tile(T, tm)
        if n == "w_ff1":
            return _mm(dy, self.pack, tb=True, name=name, shape=(T, 1024, 4096), tiles=(tm, 1024, 1024),
                       b_spec=pl.BlockSpec((None, 1024, 1024), lambda i, j, k: (k, 0, 0)), **kw)
        if n == "w_ff2":
            return _mm(dy, self.pack, tb=True, name=name, shape=(T, 4096, 1024), tiles=(tm, 1024, 1024),
                       b_spec=pl.BlockSpec((None, 1024, 1024), lambda i, j, k: (j, off // 1024, 0)), **kw)
        return _mm(dy, self.pack.reshape(N_CHIPS, LATE_ROWS // 256, 256, 1024), tb=True, name=name,
                   shape=(T, 1024, 1024), tiles=(tm, 1024, 1024), b_spec=self._whole(off), **kw)


def _late_grad(x, dy, n, bufs, *, name):
    T, off = x.shape[0], LATE_OFF[n]
    if n == "w_ff1":
        spec = pl.BlockSpec((None, 512, 1024), lambda i, j, k: (j, i, 0))
        shape, tiles = (1024, 4096, T), (512, 1024, T)
    elif n == "w_ff2":
        spec = pl.BlockSpec((None, 512, 1024), lambda i, j, k: (i // 2, off // 512 + i % 2, 0))
        shape, tiles = (4096, 1024, T), (512, 1024, T)
    else:
        spec = pl.BlockSpec((None, 256, 1024), lambda i, j, k: (i, off // 256, 0))
        shape, tiles = (1024, 1024, T), (256, 1024, T)
    return _mm(x, dy, ta=True, name=name, shape=shape, tiles=tiles, out_dtypes=[(b, spec) for b in bufs])


def _rowwise(fn, rows, consts, outs, reds=(), *, name, tm=256, after=()):
    T = rows[0][0].shape[0]
    tm = min(tm, T)
    n_row, n_const, n_out = len(rows), len(consts), len(outs)
    placed = [o for o in outs if len(o) == 3]
    n_pass = len(after) + len(placed)

    def body(*refs):
        i = pl.program_id(0)
        vals = [r[...] for r in refs[:n_row + n_const]]
        vals = [v.astype(F32) if v.dtype == BF16 else v for v in vals]
        res = fn(*vals)
        out_refs = refs[n_row + n_const + n_pass:]
        for k in range(n_out):
            out_refs[k][...] = res[k].astype(out_refs[k].dtype)
        for k in range(len(reds)):
            ref = out_refs[n_out + k]

            @pl.when(i == 0)
            def _(ref=ref):
                ref[...] = jnp.zeros_like(ref)

            ref[...] += res[n_out + k]

    in_specs = [pl.BlockSpec((tm, w), functools.partial(lambda i, cb: (i, cb), cb=cb))
                for (_, w, cb) in rows]
    in_specs += [pl.BlockSpec(c.shape, lambda i: (0, 0)) for c in consts]
    in_specs += [ANY] * n_pass
    out_specs, out_shape, aliases = [], [], {}
    for o, out in enumerate(outs):
        if len(out) == 3:
            w, buf, cb = out
            out_specs.append(pl.BlockSpec((tm, w), functools.partial(lambda i, cb: (i, cb), cb=cb)))
            out_shape.append(jax.ShapeDtypeStruct(buf.shape, buf.dtype))
            aliases[n_row + n_const + len(after) + len(aliases)] = o
        else:
            out_specs.append(pl.BlockSpec((tm, out[0]), lambda i: (i, 0)))
            out_shape.append(jax.ShapeDtypeStruct((T, out[0]), out[1]))
    out_specs += [pl.BlockSpec(s, lambda i: (0, 0)) for s in reds]
    out_shape += [jax.ShapeDtypeStruct(s, F32) for s in reds]
    return pl.pallas_call(
        body, name=name, out_shape=out_shape, grid=(T // tm,),
        in_specs=in_specs, out_specs=out_specs, input_output_aliases=aliases,
        compiler_params=_params("arbitrary"),
    )(*[r[0] for r in rows], *consts, *after, *[o[1] for o in placed])


def _rms(x, g):
    r = lax.rsqrt(jnp.mean(x * x, axis=-1, keepdims=True) + EPS)
    return x * r * g


def _rms_bwd(x, g, dy):
    r = lax.rsqrt(jnp.mean(x * x, axis=-1, keepdims=True) + EPS)
    n = x * r
    dyg = dy * g
    dx = r * (dyg - n * jnp.mean(dyg * n, axis=-1, keepdims=True))
    return dx, jnp.sum(dy * n, axis=0, keepdims=True)


def _sigmoid(x):
    return 1.0 / (1.0 + jnp.exp(-x))


def _rope(x, cs, sa, sb):
    return x * cs + pltpu.roll(x, 32, 1) * sa + pltpu.roll(x, 96, 1) * sb


def _rope_t(dy, cs, sa, sb):
    return dy * cs + pltpu.roll(dy * sa, 96, 1) + pltpu.roll(dy * sb, 32, 1)


def _causal_mask(t):
    row = lax.broadcasted_iota(jnp.int32, (t, t), 0)
    col = lax.broadcasted_iota(jnp.int32, (t, t), 1)
    return col <= row


FLASH_FWD_TQ = 4096
FLASH_FWD_TK = 1024
FLASH_FWD_HEADS = 1
FLASH_BWD_TQ = 1024
FLASH_BWD_TK = 512
NT_DIMS = (((1,), (1,)), ((), ()))
TN_DIMS = (((0,), (0,)), ((), ()))


def _flash_fwd(q, k, v, *, tq=FLASH_FWD_TQ, tk=FLASH_FWD_TK, hp=FLASH_FWD_HEADS):
    T = q.shape[0]
    tq, tk = min(tq, T), min(tk, T)
    n_diag = tq // tk
    assert n_diag >= 1 and N_HEADS % hp == 0
    heads = range(hp)

    def body(q_ref, k_ref, v_ref, o_ref, lse_ref):
        i = pl.program_id(1)

        def cols(ref, rows, h, width=QK_PAD):
            return ref[rows, h * width:(h + 1) * width]

        def logits(j, r0=0):
            sl = pl.ds(pl.multiple_of(j * tk, tk), tk)
            return tuple(lax.dot_general(cols(q_ref, slice(r0, tq), h), cols(k_ref, sl, h), NT_DIMS,
                                         preferred_element_type=F32) for h in heads)

        def update(j, s, m, acc, masked, r0=0):
            sl = pl.ds(pl.multiple_of(j * tk, tk), tk)
            m_out, acc_out = [], []
            for h in heads:
                sh = s[h]
                if masked:
                    row = i * tq + r0 + lax.broadcasted_iota(jnp.int32, (tq - r0, tk), 0)
                    col = j * tk + lax.broadcasted_iota(jnp.int32, (tq - r0, tk), 1)
                    sh = jnp.where(col <= row, sh, NEG)
                m_old, acc_old = m[h][r0:], acc[h][r0:]
                m_new = jnp.maximum(m_old, jnp.max(sh, axis=-1, keepdims=True))
                p = jnp.exp2(sh - m_new)
                acc_new = (jnp.exp2(m_old - m_new) * acc_old
                           + jnp.dot(p.astype(BF16), cols(v_ref, sl, h), preferred_element_type=F32))
                if r0:
                    m_new = jnp.concatenate([m[h][:r0], m_new], axis=0)
                    acc_new = jnp.concatenate([acc[h][:r0], acc_new], axis=0)
                m_out.append(m_new)
                acc_out.append(acc_new)
            return tuple(m_out), tuple(acc_out)

        def step(j, carry):
            s, m, acc = carry
            s_next = logits(j + 1)
            m, acc = update(j, s, m, acc, False)
            return s_next, m, acc

        first = i * n_diag
        init = (logits(0), tuple(jnp.full((tq, 1), NEG, F32) for _ in heads),
                tuple(jnp.zeros((tq, QK_PAD), F32) for _ in heads))
        s, m, acc = lax.fori_loop(0, first, step, init)
        for d in range(n_diag):
            m, acc = update(first + d, s, m, acc, True, r0=d * tk)
            if d + 1 < n_diag:
                s = logits(first + d + 1, r0=(d + 1) * tk)
        for h in heads:
            l = acc[h][:, NOPE:]
            o_ref[:, h * NOPE:(h + 1) * NOPE] = (acc[h][:, :NOPE] / l).astype(BF16)
            lse_ref[:, h * NOPE:(h + 1) * NOPE] = m[h] + jnp.log2(l)

    return pl.pallas_call(
        body, name="flash_fwd",
        out_shape=[jax.ShapeDtypeStruct((T, N_HEADS * NOPE), BF16),
                   jax.ShapeDtypeStruct((T, N_HEADS * NOPE), F32)],
        grid=(N_HEADS // hp, T // tq),
        in_specs=[pl.BlockSpec((tq, hp * QK_PAD), lambda h, i: (i, h)),
                  pl.BlockSpec((T, hp * QK_PAD), lambda h, i: (0, h)),
                  pl.BlockSpec((T, hp * QK_PAD), lambda h, i: (0, h))],
        out_specs=[pl.BlockSpec((tq, hp * NOPE), lambda h, i: (i, h)),
                   pl.BlockSpec((tq, hp * NOPE), lambda h, i: (i, h))],
        compiler_params=_params("parallel", "arbitrary"),
    )(q, k, v)


def _flash_bwd(q, k, v, do, lse, *, tq=FLASH_BWD_TQ, tk=FLASH_BWD_TK):
    T = q.shape[0]
    tq, tk = min(tq, T), min(tk, T)

    def body(q_ref, k_ref, v_ref, do_ref, lse_ref, dq_ref, dk_ref, dv_ref, dq_acc):
        dq_acc[...] = jnp.zeros_like(dq_acc)
        for j in range(T // tk):
            k0 = j * tk
            kj = k_ref[k0:k0 + tk, :]
            vj = v_ref[k0:k0 + tk, :]
            dk = jnp.zeros((tk, QK_PAD), F32)
            dv = jnp.zeros((tk, NOPE), F32)
            for i in range(k0 // tq, T // tq):
                r0 = max(i * tq, k0)
                r1 = (i + 1) * tq
                qi = q_ref[r0:r1, :]
                doi = do_ref[r0:r1, :]
                s = lax.dot_general(qi, kj, NT_DIMS, preferred_element_type=F32)
                p = jnp.exp2(s - lse_ref[r0:r1, :][:, :1])
                if r0 < k0 + tk:
                    row = r0 + lax.broadcasted_iota(jnp.int32, (r1 - r0, tk), 0)
                    col = k0 + lax.broadcasted_iota(jnp.int32, (r1 - r0, tk), 1)
                    p = jnp.where(col <= row, p, 0.0)
                dv = dv + lax.dot_general(p.astype(BF16), doi[:, :NOPE], TN_DIMS, preferred_element_type=F32)
                ds = (p * lax.dot_general(doi, vj, NT_DIMS, preferred_element_type=F32)).astype(BF16)
                dk = dk + lax.dot_general(ds, qi, TN_DIMS, preferred_element_type=F32)
                dq_acc[r0:r1, :] += jnp.dot(ds, kj, preferred_element_type=F32)
            dk_ref[k0:k0 + tk, :] = dk.astype(BF16)
            dv_ref[k0:k0 + tk, :] = dv.astype(BF16)
        dq_ref[...] = dq_acc[...].astype(BF16)

    return pl.pallas_call(
        body, name="flash_bwd",
        out_shape=[jax.ShapeDtypeStruct((T, N_HEADS * QK_PAD), BF16),
                   jax.ShapeDtypeStruct((T, N_HEADS * QK_PAD), BF16),
                   jax.ShapeDtypeStruct((T, N_HEADS * NOPE), BF16)],
        grid=(N_HEADS,),
        in_specs=[pl.BlockSpec((T, QK_PAD), lambda h: (0, h)),
                  pl.BlockSpec((T, QK_PAD), lambda h: (0, h)),
                  pl.BlockSpec((T, QK_PAD), lambda h: (0, h)),
                  pl.BlockSpec((T, QK_PAD), lambda h: (0, h)),
                  pl.BlockSpec((T, NOPE), lambda h: (0, h))],
        out_specs=[pl.BlockSpec((T, QK_PAD), lambda h: (0, h)),
                   pl.BlockSpec((T, QK_PAD), lambda h: (0, h)),
                   pl.BlockSpec((T, NOPE), lambda h: (0, h))],
        scratch_shapes=[pltpu.VMEM((T, QK_PAD), F32)],
        compiler_params=_params("arbitrary"),
    )(q, k, v, do, lse)


SUB = 128


def _windows(buf, t0, c0, first, last):
    span = SUB + 8 * ((last + 7) // 8)
    base = buf[t0:t0 + span, c0:c0 + SUB]
    for r in range(8):
        offs = [o for o in range(first, last + 1) if o % 8 == r]
        if offs:
            win = base if r == 0 else pltpu.roll(base, span - r, 0)
            for o in offs:
                yield o, win[o - r:o - r + SUB, :]


def _conv_fwd(u, w, bias, *, tm=256, ct=256):
    T, C = u.shape
    tm = min(tm, T)
    hb = tm // HALO
    lead = HALO - (CONV_W - 1)

    def body(cur_ref, halo_ref, w_ref, b_ref, o_ref, buf):
        t = pl.program_id(1)
        buf[0:HALO, :] = jnp.where(t > 0, halo_ref[...], 0.0)
        buf[HALO:, :] = cur_ref[...]
        for c0 in range(0, ct, SUB):
            for t0 in range(0, tm, SUB):
                acc = jnp.broadcast_to(b_ref[:, c0:c0 + SUB], (SUB, SUB))
                for o, win in _windows(buf, t0, c0, lead, lead + CONV_W - 1):
                    acc = acc + win * w_ref[o - lead:o - lead + 1, c0:c0 + SUB]
                o_ref[t0:t0 + SUB, c0:c0 + SUB] = acc

    return pl.pallas_call(
        body, name="conv_fwd",
        out_shape=jax.ShapeDtypeStruct((T, C), F32),
        grid=(C // ct, T // tm),
        in_specs=[pl.BlockSpec((tm, ct), lambda c, t: (t, c)),
                  pl.BlockSpec((HALO, ct), lambda c, t: (jnp.maximum(t * hb - 1, 0), c)),
                  pl.BlockSpec((CONV_TAPS_PAD, ct), lambda c, t: (0, c)),
                  pl.BlockSpec((1, ct), lambda c, t: (0, c))],
        out_specs=pl.BlockSpec((tm, ct), lambda c, t: (t, c)),
        scratch_shapes=[pltpu.VMEM((HALO + tm, ct), F32)],
        compiler_params=_params("parallel", "arbitrary"),
    )(u, u, w, bias)


def _conv_bwd(u, du1, w, *, tm=256, ct=256):
    T, C = u.shape
    tm = min(tm, T)
    hb = tm // HALO
    nt = T // tm
    last_halo = T // HALO - 1
    lead = HALO - (CONV_W - 1)

    def body(u_ref, uh_ref, d_ref, dh_ref, w_ref, du_ref, dw_ref, ubuf, dbuf, dw8):
        t = pl.program_id(1)
        ubuf[0:HALO, :] = jnp.where(t > 0, uh_ref[...], 0.0)
        ubuf[HALO:, :] = u_ref[...]
        dbuf[0:tm, :] = d_ref[...]
        dbuf[tm:, :] = jnp.where(t < nt - 1, dh_ref[...], 0.0)

        @pl.when(t == 0)
        def _():
            dw8[...] = jnp.zeros_like(dw8)

        for c0 in range(0, ct, SUB):
            for t0 in range(0, tm, SUB):
                acc = jnp.zeros((SUB, SUB), F32)
                for o, win in _windows(dbuf, t0, c0, 0, CONV_W - 1):
                    kk = CONV_W - 1 - o
                    acc = acc + win * w_ref[kk:kk + 1, c0:c0 + SUB]
                du_ref[t0:t0 + SUB, c0:c0 + SUB] = acc
                dsub = d_ref[t0:t0 + SUB, c0:c0 + SUB]
                for o, win in _windows(ubuf, t0, c0, lead, lead + CONV_W - 1):
                    kk = o - lead
                    dw8[8 * kk:8 * kk + 8, c0:c0 + SUB] += jnp.sum(
                        (dsub * win).reshape(SUB // 8, 8, SUB), axis=0)

        @pl.when(t == nt - 1)
        def _():
            dw_ref[...] = jnp.sum(dw8[...].reshape(CONV_TAPS_PAD, 8, ct), axis=1)

    return pl.pallas_call(
        body, name="conv_bwd",
        out_shape=[jax.ShapeDtypeStruct((T, C), F32), jax.ShapeDtypeStruct((CONV_TAPS_PAD, C), F32)],
        grid=(C // ct, nt),
        in_specs=[pl.BlockSpec((tm, ct), lambda c, t: (t, c)),
                  pl.BlockSpec((HALO, ct), lambda c, t: (jnp.maximum(t * hb - 1, 0), c)),
                  pl.BlockSpec((tm, ct), lambda c, t: (t, c)),
                  pl.BlockSpec((HALO, ct), lambda c, t: (jnp.minimum((t + 1) * hb, last_halo), c)),
                  pl.BlockSpec((CONV_TAPS_PAD, ct), lambda c, t: (0, c))],
        out_specs=[pl.BlockSpec((tm, ct), lambda c, t: (t, c)),
                   pl.BlockSpec((CONV_TAPS_PAD, ct), lambda c, t: (0, c))],
        scratch_shapes=[pltpu.VMEM((HALO + tm, ct), F32), pltpu.VMEM((tm + HALO, ct), F32),
                        pltpu.VMEM((CONV_TAPS_PAD * 8, ct), F32)],
        compiler_params=_params("parallel", "arbitrary"),
    )(u, u, du1, du1, w)


def _adamw(w, g, m, v, *, name, g_row=None):
    R, C = w.shape
    tr = _row_tile(R, max(8, (1 << 18) // C // 8 * 8))
    first = 0 if g_row is None else g_row // tr
    assert g_row is None or g_row % tr == 0

    def body(w_ref, g_ref, m_ref, v_ref, d_ref, nm_ref, nv_ref, *g_out):
        gv = g_ref[...]
        nm = ADAM_B1 * m_ref[...] + (1.0 - ADAM_B1) * gv
        nv = ADAM_B2 * v_ref[...] + (1.0 - ADAM_B2) * (gv * gv)
        m_hat = nm / (1.0 - ADAM_B1 ** ADAM_STEP)
        v_hat = nv / (1.0 - ADAM_B2 ** ADAM_STEP)
        d_ref[...] = -ADAM_LR * (m_hat / (jnp.sqrt(v_hat) + ADAM_EPS) + ADAM_WD * w_ref[...])
        nm_ref[...] = nm
        nv_ref[...] = nv
        for ref in g_out:
            ref[...] = gv

    spec = pl.BlockSpec((tr, C), lambda i: (i, 0))
    g_spec = pl.BlockSpec((tr, C), lambda i: (first + i, 0))
    n_out = 3 if g_row is None else 4
    return pl.pallas_call(
        body, name=name, out_shape=[jax.ShapeDtypeStruct((R, C), F32)] * n_out,
        grid=(R // tr,), in_specs=[spec, g_spec, spec, spec], out_specs=[spec] * n_out,
        compiler_params=_params("parallel"),
    )(w, g, m, v)


def _coords():
    return lax.axis_index("x"), lax.axis_index("y"), lax.axis_index("c")


def _remote(src, dst, ssem, rsem, to):
    return pltpu.make_async_remote_copy(src_ref=src, dst_ref=dst, send_sem=ssem, recv_sem=rsem,
                                        device_id=to, device_id_type=MESH)


def _half(c, rows):
    return pl.ds(pl.multiple_of(c * (rows // 2), 16), rows // 2)


def _gather_weights(wpack, cw):
    def body(w_ref, cw_ref, full_ref, cwf_ref, ssem, rsem):
        x, y, c = _coords()
        q = 2 * x + y
        sib = (x, y, 1 - c)
        chips = [(1 - x, y), (x, 1 - y), (1 - x, 1 - y)]
        mine, other = _half(c, w_ref.shape[0]), _half(1 - c, w_ref.shape[0])
        first = []
        for j, (px, py) in enumerate(chips):
            first.append(_remote(w_ref.at[mine], full_ref.at[q, mine], ssem.at[j], rsem.at[j], (px, py, c)))
            first.append(_remote(cw_ref, cwf_ref.at[q], ssem.at[6 + j], rsem.at[6 + j], (px, py, c)))
        for cp in first:
            cp.start()
        passed = []
        for j, (px, py) in enumerate(chips):
            pq = 2 * px + py
            _remote(w_ref.at[mine], full_ref.at[pq, mine], ssem.at[j], rsem.at[j], (px, py, c)).wait_recv()
            fw = _remote(full_ref.at[pq, mine], full_ref.at[pq, mine], ssem.at[3 + j], rsem.at[3 + j], sib)
            fw.start()
            passed.append(fw)
        for j, (px, py) in enumerate(chips):
            pq = 2 * px + py
            _remote(full_ref.at[pq, other], full_ref.at[pq, other], ssem.at[3 + j], rsem.at[3 + j], sib).wait_recv()
            _remote(cw_ref, cwf_ref.at[pq], ssem.at[6 + j], rsem.at[6 + j], (px, py, c)).wait_recv()
        for cp in first + passed:
            cp.wait_send()

    return pl.pallas_call(
        body, name="gather_weights",
        out_shape=[jax.ShapeDtypeStruct((N_CHIPS,) + wpack.shape, wpack.dtype),
                   jax.ShapeDtypeStruct((N_CHIPS,) + cw.shape, cw.dtype)],
        in_specs=[ANY, ANY], out_specs=[ANY, ANY],
        scratch_shapes=[pltpu.SemaphoreType.DMA((9,)), pltpu.SemaphoreType.DMA((9,))],
    )(wpack, cw)


SEM = pl.BlockSpec(memory_space=pltpu.SEMAPHORE)
HBM = pl.BlockSpec(memory_space=pltpu.HBM)
N_LATE = 6


def _late_peers(x, y, c):
    out = []
    for j, (px, py) in enumerate([(1 - x, y), (x, 1 - y), (1 - x, 1 - y)]):
        for t in (0, 1):
            out.append((2 * j + t, px, py, c if t == 0 else 1 - c))
    return out


def _late_gather_start(w, after):
    rows = w.shape[0]

    def body(w_ref, land_ref, after_ref, *outs):
        sems = outs[:2 * N_LATE]
        token = outs[2 * N_LATE + 2]
        x, y, c = _coords()
        q = 2 * x + y
        mine = _half(c, rows)
        for k, px, py, pc in _late_peers(x, y, c):
            _remote(w_ref.at[mine], land_ref.at[q, mine], sems[k], sems[N_LATE + k], (px, py, pc)).start()
        token[...] = jnp.zeros_like(token)

    res = pl.pallas_call(
        body, name="late_gather_start",
        out_shape=tuple([pltpu.SemaphoreType.DMA(())] * (2 * N_LATE)) + (
            pltpu.HBM(w.shape, w.dtype), pltpu.HBM((N_CHIPS,) + w.shape, w.dtype),
            jax.ShapeDtypeStruct((8, 128), F32)),
        in_specs=(HBM, HBM, ANY),
        out_specs=tuple([SEM] * (2 * N_LATE)) + (HBM, HBM, pl.BlockSpec(memory_space=pltpu.VMEM)),
        input_output_aliases={0: 2 * N_LATE, 1: 2 * N_LATE + 1},
        compiler_params=pltpu.CompilerParams(has_side_effects=pltpu.SideEffectType.DATAFLOW_SIDE_EFFECTING),
    )(pltpu.with_memory_space_constraint(w, pltpu.HBM),
      pltpu.with_memory_space_constraint(lax.empty((N_CHIPS,) + w.shape, w.dtype), pltpu.HBM), after)
    return res[:2 * N_LATE], res[2 * N_LATE], res[2 * N_LATE + 1], res[2 * N_LATE + 2]


def _late_gather_wait(sems, w_thru, land_thru, after):
    rows = w_thru.shape[0]

    def body(w_ref, land_ref, *rest):
        sems = rest[:2 * N_LATE]
        x, y, c = _coords()
        for k, px, py, pc in _late_peers(x, y, c):
            cp = _remote(w_ref.at[_half(pc, rows)], land_ref.at[2 * px + py, _half(pc, rows)],
                         sems[k], sems[N_LATE + k], (px, py, pc))
            cp.wait_send()
            cp.wait_recv()

    return pl.pallas_call(
        body, name="late_gather_wait",
        out_shape=(pltpu.HBM(w_thru.shape, w_thru.dtype), pltpu.HBM(land_thru.shape, land_thru.dtype)),
        in_specs=(HBM, HBM) + tuple([SEM] * (2 * N_LATE)) + (ANY,),
        out_specs=(HBM, HBM), input_output_aliases={0: 0, 1: 1},
        compiler_params=pltpu.CompilerParams(has_side_effects=pltpu.SideEffectType.DATAFLOW_SIDE_EFFECTING),
    )(w_thru, land_thru, *sems, after)[1]


N_PEERS = N_DEV - 1


def _peers(x, y, c):
    out = []
    for r in range(1, N_DEV):
        out.append((r - 1, 1 - x if r & 4 else x, 1 - y if r & 2 else y, 1 - c if r & 1 else c))
    return out


def _piece(g_ref, px, py, pc, whole):
    return g_ref if whole else g_ref.at[2 * px + py, _half(pc, g_ref.shape[1])]


def _reduce_start(g16, after, *, name, whole=False):
    def body(g_ref, land_ref, after_ref, *outs):
        sems = outs[:2 * N_PEERS]
        token = outs[2 * N_PEERS + 2]
        x, y, c = _coords()
        for k, px, py, pc in _peers(x, y, c):
            _remote(_piece(g_ref, px, py, pc, whole), land_ref.at[k], sems[k], sems[N_PEERS + k],
                    (px, py, pc)).start()
        token[...] = jnp.zeros_like(token)

    land_shape = (N_PEERS,) + (g16.shape if whole else (g16.shape[1] // 2, 1024))
    res = pl.pallas_call(
        body, name=name,
        out_shape=tuple([pltpu.SemaphoreType.DMA(())] * (2 * N_PEERS)) + (
            pltpu.HBM(g16.shape, g16.dtype), pltpu.HBM(land_shape, g16.dtype),
            jax.ShapeDtypeStruct((8, 128), F32)),
        in_specs=(HBM, HBM, ANY),
        out_specs=tuple([SEM] * (2 * N_PEERS)) + (HBM, HBM, pl.BlockSpec(memory_space=pltpu.VMEM)),
        input_output_aliases={0: 2 * N_PEERS, 1: 2 * N_PEERS + 1},
        compiler_params=pltpu.CompilerParams(has_side_effects=pltpu.SideEffectType.DATAFLOW_SIDE_EFFECTING),
    )(pltpu.with_memory_space_constraint(g16, pltpu.HBM),
      pltpu.with_memory_space_constraint(lax.empty(land_shape, g16.dtype), pltpu.HBM), after)
    return res[:2 * N_PEERS], res[2 * N_PEERS], res[2 * N_PEERS + 1], res[2 * N_PEERS + 2]


def _reduce_wait(sems, g_thru, land_thru, after, *, name, whole=False):
    def body(g_ref, land_ref, *rest):
        sems = rest[:2 * N_PEERS]
        x, y, c = _coords()
        for k, px, py, pc in _peers(x, y, c):
            cp = _remote(_piece(g_ref, px, py, pc, whole), land_ref.at[k], sems[k], sems[N_PEERS + k],
                         (px, py, pc))
            cp.wait_send()
            cp.wait_recv()

    return pl.pallas_call(
        body, name=name,
        out_shape=(pltpu.HBM(g_thru.shape, g_thru.dtype), pltpu.HBM(land_thru.shape, land_thru.dtype)),
        in_specs=(HBM, HBM) + tuple([SEM] * (2 * N_PEERS)) + tuple([ANY] * len(after)),
        out_specs=(HBM, HBM), input_output_aliases={0: 0, 1: 1},
        compiler_params=pltpu.CompilerParams(has_side_effects=pltpu.SideEffectType.DATAFLOW_SIDE_EFFECTING),
    )(g_thru, land_thru, *sems, *after)[1]


def _reduce_sum(place, g32, land, *, name):
    rows = g32.shape[1]
    nb = rows // 2 // SUM_TILE

    def body(place_ref, g_ref, l_ref, o_ref):
        acc = g_ref[...]
        for k in range(N_PEERS):
            acc = acc + l_ref[k].astype(F32)
        o_ref[...] = acc

    return pl.pallas_call(
        body, name=name,
        out_shape=jax.ShapeDtypeStruct((rows, 1024), F32),
        grid_spec=pltpu.PrefetchScalarGridSpec(
            num_scalar_prefetch=1, grid=(nb,),
            in_specs=[pl.BlockSpec((None, SUM_TILE, 1024), lambda i, s: (s[0], s[1] * nb + i, 0)),
                      pl.BlockSpec((N_PEERS, SUM_TILE, 1024), lambda i, s: (0, i, 0))],
            out_specs=pl.BlockSpec((SUM_TILE, 1024), lambda i, s: (s[1] * nb + i, 0))),
        compiler_params=_params("parallel"),
    )(place, g32, land)


def _join_halves(red, *, name):
    rows = red.shape[0]

    def body(r_ref, o_ref, ssem, rsem):
        x, y, c = _coords()
        cp = _remote(r_ref.at[_half(c, rows)], o_ref.at[_half(c, rows)], ssem, rsem, (x, y, 1 - c))
        cp.start()
        _remote(r_ref.at[_half(1 - c, rows)], o_ref.at[_half(1 - c, rows)], ssem, rsem,
                (x, y, 1 - c)).wait_recv()
        cp.wait_send()

    return pl.pallas_call(
        body, name=name,
        out_shape=jax.ShapeDtypeStruct(red.shape, F32),
        in_specs=[ANY], out_specs=ANY, input_output_aliases={0: 0},
        scratch_shapes=[pltpu.SemaphoreType.DMA, pltpu.SemaphoreType.DMA],
    )(red)


def _sum_small(me, svec, land):
    def body(me_ref, s_ref, l_ref, o_ref):
        mine = me_ref[0]
        acc = None
        for d in range(N_DEV):
            theirs = l_ref[jnp.maximum(jnp.bitwise_xor(mine, d) - 1, 0)]
            v = jnp.where(mine == d, s_ref[...], theirs)
            acc = v if acc is None else acc + v
        o_ref[...] = acc

    return pl.pallas_call(
        body, name="sum_small",
        out_shape=jax.ShapeDtypeStruct(svec.shape, F32),
        grid_spec=pltpu.PrefetchScalarGridSpec(
            num_scalar_prefetch=1, grid=(1,),
            in_specs=[pl.BlockSpec(svec.shape, lambda i, s: (0, 0)),
                      pl.BlockSpec(land.shape, lambda i, s: (0, 0, 0))],
            out_specs=pl.BlockSpec(svec.shape, lambda i, s: (0, 0))),
        compiler_params=_params("arbitrary"),
    )(me, svec, land)


def _pack_early_shards(wts):
    parts = [wts["w_in"][0].T.astype(BF16)] + [wts[n].astype(BF16).reshape(-1, 1024) for n, _ in EARLY[1:]]
    used = sum(r for _, r in EARLY)
    return jnp.concatenate(parts + [jnp.zeros((EARLY_ROWS - used, 1024), BF16)], axis=0)


def _pack_late_shards(wts):
    parts, at = [], 0
    for n, off, rows in LATE:
        if off > at:
            parts.append(jnp.zeros((off - at, 1024), BF16))
        parts.append(wts[n].astype(BF16).reshape(rows, 1024))
        at = off + rows
    return jnp.concatenate(parts, axis=0)


def _unpack_early(full):
    w = {}
    w["w_in_t"] = jnp.concatenate(
        [full[0, N_SMALL:1200]] + [full[p, :1200] for p in range(1, N_CHIPS)]
        + [full[0, :N_SMALL], jnp.zeros((D_MODEL - N_SMALL, 1024), full.dtype)], axis=0)
    w["w_uq"] = full[:, 1200:1344].reshape(4, 384, 384).transpose(1, 0, 2).reshape(384, 1536)
    w["w_uk"] = full[:, 1344:1408].reshape(256, 1024)
    w["w_uv"] = full[:, 1408:1472].reshape(256, 1024)
    return w


def _pack_early_grads(g):
    gt = g["w_in_t"]
    rest = 1200 - N_SMALL
    w_in = jnp.stack([jnp.concatenate([gt[4 * D_MODEL:4 * D_MODEL + N_SMALL], gt[:rest]], axis=0)]
                     + [gt[rest + 1200 * (p - 1):rest + 1200 * p] for p in range(1, N_CHIPS)])
    parts = [
        w_in,
        g["w_uq"].reshape(384, 4, 384).transpose(1, 0, 2).reshape(4, 144, 1024),
        g["w_uk"].reshape(4, 64, 1024),
        g["w_uv"].reshape(4, 64, 1024),
        jnp.zeros((4, EARLY_ROWS - sum(r for _, r in EARLY), 1024), F32),
    ]
    return jnp.concatenate(parts, axis=1)


def _pad_w_uq(w):
    w = w.reshape(Q_RANK, N_HEADS, NOPE + ROPE)
    return jnp.pad(w, ((0, 0), (0, 0), (0, QK_PAD - NOPE - ROPE))).reshape(Q_RANK, N_HEADS * QK_PAD)


def _local_step(xs, pos, tgt, W, late_weights, late_bufs, send_late_grads, send_early_grads, cw, vec):
    D = D_MODEL
    w_in_t = W["w_in_t"]
    w_uq = _pad_w_uq(W["w_uq"])
    w_ukv = jnp.concatenate([W["w_uk"], W["w_uv"]], axis=1)
    inv = ROPE_THETA ** (-jnp.arange(0, ROPE, 2, dtype=F32) / ROPE)
    inv_freq = jnp.concatenate([inv, inv, jnp.zeros((128 - ROPE,), F32)])[None, :]

    (h,) = _rowwise(lambda x, g: (_rms(x, g),), [(xs, D, 0)], [vec["norm_mix_pre"]], [(D, BF16)],
                    name="rms_pre")
    z = _mm(h, w_in_t, tb=True, out_dtype=BF16, name="mm_in")

    def qkv_prep(zs, p, qn, kvn, invf):
        ang = p.astype(F32) * invf
        cosv, sinv = jnp.cos(ang), jnp.sin(ang)
        ln = lax.broadcasted_iota(jnp.int32, ang.shape, 1)
        cs = jnp.where(ln < ROPE, cosv, 0.0)
        sa = jnp.where((ln >= ROPE // 2) & (ln < ROPE), sinv, 0.0)
        sb = jnp.where(ln < ROPE // 2, -sinv, 0.0)
        cqn = _rms(zs[:, :Q_RANK], qn)
        ckvn = _rms(zs[:, Q_RANK:Q_RANK + KV_RANK], kvn)
        kr = _rope(zs[:, 640:768], cs, sa, sb)
        return cqn, ckvn, kr, cs, sa, sb

    cqn, ckvn, krot, cs, sa, sb = _rowwise(
        qkv_prep, [(z, D, Z_S), (pos, 1, 0)], [vec["q_norm"], vec["kv_norm"], inv_freq],
        [(Q_RANK, BF16), (KV_RANK, BF16), (128, BF16), (128, F32), (128, F32), (128, F32)],
        name="qkv_prep")
    q_pre = _mm(cqn, w_uq, out_dtype=BF16, name="mm_uq")
    kv = _mm(ckvn, w_ukv, out_dtype=BF16, name="mm_ukv")

    def qk_rope(qp, kn, vv, kr, cs_, sa_, sb_):
        qs, ks, vs = [], [], []
        ones = jnp.ones((qp.shape[0], NOPE), F32)
        for hd in range(N_HEADS):
            qs.append(qp[:, hd * QK_PAD:hd * QK_PAD + NOPE])
            qs.append(_rope(qp[:, hd * QK_PAD + NOPE:(hd + 1) * QK_PAD], cs_, sa_, sb_))
            ks.append(kn[:, hd * NOPE:(hd + 1) * NOPE])
            ks.append(kr.astype(F32))
            vs.append(vv[:, hd * NOPE:(hd + 1) * NOPE])
            vs.append(ones)
        return (jnp.concatenate(qs, axis=1) * Q_SCALE, jnp.concatenate(ks, axis=1),
                jnp.concatenate(vs, axis=1))

    q, k, v = _rowwise(
        qk_rope, [(q_pre, 2048, 0), (kv, 1024, 0), (kv, 1024, 1), (krot, 128, 0), (cs, 128, 0),
                  (sa, 128, 0), (sb, 128, 0)], [],
        [(2048, BF16), (2048, BF16), (2048, BF16)], name="qk_rope")
    attn, lse = _flash_fwd(q, k, v)
    LW = late_weights(attn)
    y_attn = LW.fwd(attn, "w_o_attn", out_dtype=BF16, name="mm_o_attn")

    (u0,) = _rowwise(lambda a, b: (a * _sigmoid(b),), [(z, D, Z_A), (z, D, Z_B)], [], [(D, F32)], name="glu")
    u1 = _conv_fwd(u0, cw, vec["conv_b"])

    def ln_parts(u, g, b):
        mu = jnp.mean(u, axis=-1, keepdims=True)
        xc = u - mu
        rs = lax.rsqrt(jnp.mean(xc * xc, axis=-1, keepdims=True) + EPS)
        nh = xc * rs
        return nh, rs, nh * g + b

    def ln_silu(u, g, b):
        y = ln_parts(u, g, b)[2]
        return (y * _sigmoid(y),)

    (u3,) = _rowwise(ln_silu, [(u1, D, 0)], [vec["conv_ln_g"], vec["conv_ln_b"]], [(D, BF16)],
                     name="ln_silu")

    def merge(yc, ga, gc, ya, b2):
        return yc, _sigmoid(ga) * ya + _sigmoid(gc) * (yc + b2)

    y_conv, merged = LW.fwd(u3, "w_pw2", name="mm_pw2", tm=512, epilogue=merge, out_dtypes=(BF16, BF16),
                            extras=[(z, Z_GA), (z, Z_GC), y_attn, vec["b_pw2"]])

    def post1(m, x, g1, g2):
        x1 = x + _rms(m, g1)
        return m, x1, _rms(x1, g2)

    mo, x1, h2 = LW.fwd(merged, "w_out", name="mm_out", tm=512, epilogue=post1, out_dtypes=(F32, F32, BF16),
                        extras=[xs, vec["norm_mix_post"], vec["norm_mlp_pre"]])

    def sqrelu(acc):
        r = jnp.maximum(acc, 0.0)
        return r * r, r

    a2, r1 = LW.fwd(h2, "w_ff1", name="mm_ff1", epilogue=sqrelu, out_dtypes=(BF16, BF16))

    def post2(f_, x1_, t_, g):
        e = x1_ + _rms(f_, g) - t_
        dy = e * (1.0 / D)
        df, dg = _rms_bwd(f_, g, dy)
        loss = jnp.broadcast_to(jnp.sum(e * e, keepdims=True) * (0.5 / D), (1, 128))
        return dy, df, dg, loss

    dy, df, g_norm_mlp_post, loss = LW.fwd(a2, "w_ff2", name="mm_ff2", epilogue=post2, out_dtypes=(F32, BF16),
                                           extras=[x1, tgt, vec["norm_mlp_post"]], reds=(D, 128))
    late_bufs = _late_grad(a2, df, "w_ff2", late_bufs, name="mm_dw_ff2")
    df1 = LW.bwd(df, "w_ff2", name="mm_df1", extras=[r1], out_dtypes=(BF16,),
              epilogue=lambda acc, r: (acc * (2.0 * r.astype(F32)),))
    late_bufs = _late_grad(h2, df1, "w_ff1", late_bufs, name="mm_dw_ff1")

    def bwd_mid(dh2_, dy_, x1_, m, g2, g1):
        d1, dg2 = _rms_bwd(x1_, g2, dh2_)
        dx1_ = dy_ + d1
        dm, dg1 = _rms_bwd(m, g1, dx1_)
        return dx1_, dm, dg2, dg1

    dx1, dmo, g_norm_mlp_pre, g_norm_mix_post = LW.bwd(
        df1, "w_ff1", name="mm_dh2", tm=512, epilogue=bwd_mid, out_dtypes=(F32, BF16), reds=(D, D),
        extras=[dy, x1, mo, vec["norm_mlp_pre"], vec["norm_mix_post"]])
    late_bufs = _late_grad(merged, dmo, "w_out", late_bufs, name="mm_dw_out")
    def dmerge(dm, ga, gc, ya, yc, b2):
        sga, sgc = _sigmoid(ga), _sigmoid(gc)
        dya = dm * sga
        dyc = dm * sgc
        dga = dm * ya * sga * (1.0 - sga)
        dgc = dm * (yc + b2) * sgc * (1.0 - sgc)
        return dya, dyc, jnp.concatenate([dga, dgc], axis=1), jnp.sum(dyc, axis=0, keepdims=True)

    dz = lax.empty((xs.shape[0], Z_COLS), BF16)
    tm_dm = _tile(xs.shape[0], 512)
    gates = pl.BlockSpec((tm_dm, 2 * D), lambda i, j, k: (i, Z_GA // 2))
    dya, dyc, dz, g_b_pw2 = LW.bwd(
        dmo, "w_out", name="mm_dmerged", tm=tm_dm, epilogue=dmerge, out_dtypes=(BF16, BF16, (dz, gates)),
        reds=(D,), extras=[(z, Z_GA), (z, Z_GC), y_attn, y_conv, vec["b_pw2"]])

    late_bufs = _late_grad(attn, dya, "w_o_attn", late_bufs, name="mm_dw_o_attn")
    late_bufs = _late_grad(u3, dyc, "w_pw2", late_bufs, name="mm_dw_pw2")
    late_token = send_late_grads(late_bufs)

    dattn = LW.bwd(dya, "w_o_attn", out_dtype=BF16, name="mm_dattn")

    def delta_fn(do_, o_):
        do32 = do_.astype(F32)
        pr = do32 * o_.astype(F32)
        ln = lax.broadcasted_iota(jnp.int32, (pr.shape[0], NOPE), 1)
        cols = []
        for hd in range(N_HEADS):
            dl = jnp.sum(pr[:, hd * NOPE:(hd + 1) * NOPE], axis=-1, keepdims=True)
            hi = dl.astype(BF16).astype(F32)
            cols.append(do32[:, hd * NOPE:(hd + 1) * NOPE])
            cols.append(jnp.where(ln == 0, -hi, jnp.where(ln == 1, hi - dl, 0.0)))
        return (jnp.concatenate(cols, axis=1),)

    (do_ext,) = _rowwise(delta_fn, [(dattn, D, 0), (attn, D, 0)], [], [(2048, BF16)], name="attn_delta",
                         after=[late_token])
    dq, dk, dv = _flash_bwd(q, k, v, do_ext, lse)

    def unrope(dq_, dk_, dv_, cs_, sa_, sb_):
        dq_, dk_ = dq_.astype(F32), dk_.astype(F32)
        qs, kn = [], []
        dkr = jnp.zeros_like(cs_)
        for hd in range(N_HEADS):
            qs.append(dq_[:, hd * QK_PAD:hd * QK_PAD + NOPE] * SCALE)
            qs.append(_rope_t(dq_[:, hd * QK_PAD + NOPE:(hd + 1) * QK_PAD] * SCALE, cs_, sa_, sb_))
            kn.append(dk_[:, hd * QK_PAD:hd * QK_PAD + NOPE] * K_UNSCALE)
            dkr = dkr + dk_[:, hd * QK_PAD + NOPE:(hd + 1) * QK_PAD]
        return (jnp.concatenate(qs, axis=1), jnp.concatenate(kn + [dv_], axis=1),
                _rope_t(dkr * K_UNSCALE, cs_, sa_, sb_))

    dq_pre, dkv, dkr = _rowwise(
        unrope, [(dq, 2048, 0), (dk, 2048, 0), (dv, 1024, 0), (cs, 128, 0), (sa, 128, 0), (sb, 128, 0)], [],
        [(2048, BF16), (2048, BF16), (128, F32)], name="unrope")
    g_w_uq = _mm(cqn, dq_pre, ta=True, name="mm_dw_uq")
    dcqn = _mm(dq_pre, w_uq, tb=True, out_dtype=BF16, name="mm_dcqn")
    g_w_ukv = _mm(ckvn, dkv, ta=True, name="mm_dw_ukv")
    dckvn = _mm(dkv, w_ukv, tb=True, out_dtype=BF16, name="mm_dckvn")

    def small_bwd(zs, dcq_, dckv_, dkr_, qn, kvn):
        dcq, dqn = _rms_bwd(zs[:, :Q_RANK], qn, dcq_)
        dckv, dkvn = _rms_bwd(zs[:, Q_RANK:Q_RANK + KV_RANK], kvn, dckv_)
        pad = jnp.zeros((zs.shape[0], D - 768), F32)
        return jnp.concatenate([dcq, dckv, dkr_, pad], axis=1), dqn, dkvn

    dz, g_q_norm, g_kv_norm = _rowwise(
        small_bwd, [(z, D, Z_S), (dcqn, Q_RANK, 0), (dckvn, KV_RANK, 0), (dkr, 128, 0)],
        [vec["q_norm"], vec["kv_norm"]], [(D, dz, Z_S)], [(1, Q_RANK), (1, KV_RANK)], name="small_bwd")

    def ln_silu_bwd(d3, u, g, b):
        nh, rs, y = ln_parts(u, g, b)
        sg = _sigmoid(y)
        dyv = d3 * (sg * (1.0 + y * (1.0 - sg)))
        dnh = dyv * g
        du = rs * (dnh - jnp.mean(dnh, axis=-1, keepdims=True)
                   - nh * jnp.mean(dnh * nh, axis=-1, keepdims=True))
        return (du, jnp.sum(dyv * nh, axis=0, keepdims=True), jnp.sum(dyv, axis=0, keepdims=True),
                jnp.sum(du, axis=0, keepdims=True))

    du1, g_ln_g, g_ln_b, g_conv_b = LW.bwd(
        dyc, "w_pw2", name="mm_du3", tm=512, epilogue=ln_silu_bwd, out_dtypes=(F32,), reds=(D, D, D),
        extras=[u1, vec["conv_ln_g"], vec["conv_ln_b"]])
    du0, g_conv_w = _conv_bwd(u0, du1, cw)

    def glu_bwd(d0, a, b):
        sg = _sigmoid(b)
        return (jnp.concatenate([d0 * sg, d0 * a * sg * (1.0 - sg)], axis=1),)

    (dz,) = _rowwise(glu_bwd, [(du0, D, 0), (z, D, Z_A), (z, D, Z_B)], [], [(2 * D, dz, Z_A // 2)],
                     name="glu_bwd")
    g_w_in_t = _mm(dz, h, ta=True, name="mm_dw_in")
    early_token = send_early_grads({
        "w_in_t": g_w_in_t,
        "w_uq": g_w_uq.reshape(Q_RANK, N_HEADS, QK_PAD)[:, :, :NOPE + ROPE].reshape(Q_RANK, 1536),
        "w_uk": g_w_ukv[:, :1024], "w_uv": g_w_ukv[:, 1024:]})

    def final(dh_, dx1_, x, g):
        d, dg = _rms_bwd(x, g, dh_)
        return dx1_ + d, dg

    grad_x, g_norm_mix_pre = _mm(
        dz, w_in_t, name="mm_dh", after=[early_token], tiles=(_tile(xs.shape[0], 512), D, Z_COLS // 2),
        epilogue=final, out_dtypes=(F32,), reds=(D,), extras=[dx1, xs, vec["norm_mix_pre"]])

    gs = {
        "norm_mix_pre": g_norm_mix_pre, "conv_b": g_conv_b, "conv_ln_g": g_ln_g, "conv_ln_b": g_ln_b,
        "b_pw2": g_b_pw2, "norm_mix_post": g_norm_mix_post, "norm_mlp_pre": g_norm_mlp_pre,
        "norm_mlp_post": g_norm_mlp_post, "q_norm": g_q_norm, "kv_norm": g_kv_norm,
    }
    return grad_x, gs, g_conv_w, loss


def _row1024(a):
    return jnp.pad(a, ((0, 0), (0, 1024 - a.shape[1])))


BIG = tuple(n for n, _ in EARLY) + tuple(n for n, _, _ in LATE)
ORDER = ("norm_mix_pre", "w_in", "q_norm", "w_uq", "kv_norm", "w_uk", "w_uv", "w_o_attn", "conv_w",
         "conv_b", "conv_ln_g", "conv_ln_b", "w_pw2", "b_pw2", "w_out", "norm_mix_post", "norm_mlp_pre",
         "w_ff1", "w_ff2", "norm_mlp_post")


def kernel(x, positions, norm_mix_pre, w_in, q_norm, w_uq, kv_norm, w_uk, w_uv, w_o_attn, conv_w, conv_b, conv_ln_g, conv_ln_b, w_pw2, b_pw2, w_out, norm_mix_post, norm_mlp_pre, w_ff1, w_ff2, norm_mlp_post, loss_target, m_norm_mix_pre, m_w_in, m_q_norm, m_w_uq, m_kv_norm, m_w_uk, m_w_uv, m_w_o_attn, m_conv_w, m_conv_b, m_conv_ln_g, m_conv_ln_b, m_w_pw2, m_b_pw2, m_w_out, m_norm_mix_post, m_norm_mlp_pre, m_w_ff1, m_w_ff2, m_norm_mlp_post, v_norm_mix_pre, v_w_in, v_q_norm, v_w_uq, v_kv_norm, v_w_uk, v_w_uv, v_w_o_attn, v_conv_w, v_conv_b, v_conv_ln_g, v_conv_ln_b, v_w_pw2, v_b_pw2, v_w_out, v_norm_mix_post, v_norm_mlp_pre, v_w_ff1, v_w_ff2, v_norm_mlp_post):
    args = dict(locals())
    wts = {n: args[n] for n in ORDER}
    mom = {n: args["m_" + n] for n in ORDER}
    var = {n: args["v_" + n] for n in ORDER}
    T = x.shape[1]
    place = jnp.stack([2 * lax.axis_index("x") + lax.axis_index("y"), lax.axis_index("c")]).astype(jnp.int32)

    w_early = _pack_early_shards(wts)
    w_late = _pack_late_shards(wts)
    cw_own = jnp.pad(conv_w.reshape(CONV_W, -1), ((0, CONV_TAPS_PAD - CONV_W), (0, 0)))
    full, cw_all = _gather_weights(w_early, cw_own)
    full = lax.dynamic_update_slice(full, w_early[None], (place[0], 0, 0))
    cw_all = lax.dynamic_update_slice(cw_all, cw_own[None], (place[0], 0, 0))
    W = _unpack_early(full)
    cw = cw_all.transpose(1, 0, 2).reshape(CONV_TAPS_PAD, D_MODEL)
    sems, w_thru, land, token = _late_gather_start(w_late, full)
    vec = {n: wts[n] for n in SMALL_VECS}
    vec["norm_mix_pre"] = vec["norm_mix_pre"] + token[:1, :1]

    def late_weights(after):
        landed = _late_gather_wait(sems, w_thru, land, after)
        return _LateWeights(lax.dynamic_update_slice(landed, w_late[None], (place[0], 0, 0)))

    sent = {}

    def send_late_grads(bufs):
        sent["late"] = (bufs[0],) + _reduce_start(bufs[1], place, name="late_grads_start")
        return sent["late"][4]

    def send_early_grads(g):
        g32 = _pack_early_grads(g)
        sent["early"] = (g32,) + _reduce_start(g32.astype(BF16), place, name="early_grads_start")
        return sent["early"][4]

    late_bufs = (lax.empty((N_CHIPS, LATE_ROWS, 1024), F32), lax.empty((N_CHIPS, LATE_ROWS, 1024), BF16))
    grad_x, gs, g_conv_w, loss = _local_step(x[0], positions.reshape(T, 1), loss_target[0], W, late_weights,
                                             late_bufs, send_late_grads, send_early_grads, cw, vec)

    def finish(group, after):
        g32, sems_, g_thru, land_, _ = sent[group]
        landed = _reduce_wait(sems_, g_thru, land_, after, name=group + "_grads_wait")
        red = _reduce_sum(place, g32, landed, name=group + "_grads_sum")
        return _join_halves(red, name=group + "_grads_join")

    def as2d(a):
        return a.reshape(-1, a.shape[-1]) if a.ndim != 4 or a.shape[2] == 1 else a.reshape(a.shape[1], -1)

    grads, delta, new_m, new_v = {}, {}, {}, {}

    def update(n):
        shp = wts[n].shape
        outs = _adamw(as2d(wts[n]), as2d(grads[n]), as2d(mom[n]), as2d(var[n]), name="adamw_" + n)
        delta[n], new_m[n], new_v[n] = (o.reshape(shp) for o in outs)

    g_late = finish("late", [sent["early"][4]])
    for n, off, rows in LATE:
        shp = wts[n].shape
        outs = _adamw(as2d(wts[n]), g_late, as2d(mom[n]), as2d(var[n]), g_row=off, name="adamw_" + n)
        delta[n], new_m[n], new_v[n], grads[n] = (o.reshape(shp) for o in outs)

    svec = jnp.concatenate(
        [_row1024(gs[n]) for n in SMALL_VECS] + [_row1024(loss)]
        + [jnp.zeros((SMALL_CONVW_ROW - SMALL_LOSS_ROW - 1, 1024), F32), g_conv_w], axis=0)
    small_sent = _reduce_start(svec, place, name="small_grads_start", whole=True)

    g_early = finish("early", [grad_x, delta["w_ff2"]])
    outs = _adamw(wts["w_in"][0].T, g_early[:1200], mom["w_in"][0].T, var["w_in"][0].T, name="adamw_w_in")
    delta["w_in"], new_m["w_in"], new_v["w_in"] = (o.T[None] for o in outs)
    grads["w_in"] = g_early[:1200].T[None]
    off = 1200
    for n, rows in EARLY[1:]:
        grads[n] = g_early[off:off + rows].reshape(wts[n].shape)
        off += rows
        update(n)

    landed = _reduce_wait(small_sent[0], small_sent[1], small_sent[2], [delta["w_in"]],
                          name="small_grads_wait", whole=True)
    ssum = _sum_small((2 * place[0] + place[1])[None], svec, landed)
    for r, n in enumerate(SMALL_VECS):
        grads[n] = ssum[r:r + 1, :wts[n].shape[1]]
    cw_sum = ssum[SMALL_CONVW_ROW:SMALL_CONVW_ROW + CONV_W]
    grads["conv_w"] = lax.dynamic_slice_in_dim(cw_sum, place[0] * 256, 256, axis=1).reshape(conv_w.shape)
    loss_out = ssum[SMALL_LOSS_ROW, 0]
    update("conv_w")
    small = [jnp.concatenate([_row1024(t[n]) for n in SMALL_VECS], axis=0) for t in (wts, grads, mom, var)]
    outs = _adamw(*small, name="adamw_small")
    for r, n in enumerate(SMALL_VECS):
        delta[n], new_m[n], new_v[n] = (o[r:r + 1, :wts[n].shape[1]] for o in outs)

    return (loss_out, grad_x[None], *[grads[n] for n in ORDER], *[delta[n] for n in ORDER],
            *[new_m[n] for n in ORDER], *[new_v[n] for n in ORDER])
```

```python
import functools

import numpy as np
import jax
import jax.numpy as jnp
from jax import lax
from jax.experimental import pallas as pl
from jax.experimental.pallas import tpu as pltpu

F32 = jnp.float32
BF16 = jnp.bfloat16
MESH = pl.DeviceIdType.MESH

D_MODEL = 1024
N_HEADS = 8
NOPE = 128
ROPE = 64
QK_PAD = 256
Q_RANK = 384
KV_RANK = 256
CONV_W = 31
CONV_TAPS_PAD = 32
HALO = 32
D_FF = 4096
EPS = 1e-6
ROPE_THETA = 10000.0
SCALE = float((NOPE + ROPE) ** -0.5)
Q_SCALE = SCALE * float(np.log2(np.e))
K_UNSCALE = float(np.log(2.0))
Z_A, Z_B, Z_GA, Z_GC, Z_S = range(5)
Z_COLS = 5 * D_MODEL
N_SMALL = Q_RANK + KV_RANK + ROPE
N_CHIPS = 4
N_DEV = 8

ADAM_LR = 0.001
ADAM_B1 = 0.9
ADAM_B2 = 0.999
ADAM_EPS = 1e-08
ADAM_WD = 0.01
ADAM_STEP = 10

VMEM_LIMIT = 48 * 1024 * 1024
NEG = -1e30

EARLY = (("w_in", 1200), ("w_uq", 144), ("w_uk", 64), ("w_uv", 64))
EARLY_ROWS = 1536
LATE = (("w_ff1", 0, 1024), ("w_o_attn", 1024, 256), ("w_pw2", 1536, 256), ("w_out", 1792, 256),
        ("w_ff2", 2048, 1024))
LATE_OFF = {n: off for n, off, _ in LATE}
LATE_ROWS = 3072
SUM_TILE = 256

SMALL_VECS = ("norm_mix_pre", "conv_b", "conv_ln_g", "conv_ln_b", "b_pw2", "norm_mix_post",
              "norm_mlp_pre", "norm_mlp_post", "q_norm", "kv_norm")
SMALL_LOSS_ROW = 10
SMALL_CONVW_ROW = 16
SMALL_ROWS = 48


ANY = pl.BlockSpec(memory_space=pl.ANY)


def _params(*sem):
    return pltpu.CompilerParams(dimension_semantics=sem, vmem_limit_bytes=VMEM_LIMIT)


def _tile(n, cap):
    if n <= cap:
        return n
    t = (cap // 128) * 128
    while n % t:
        t -= 128
    return t


def _row_tile(n, cap):
    if n <= cap:
        return n
    t = (cap // 8) * 8
    while n % t:
        t -= 8
    return t


MM_VMEM_BUDGET = 40 * 1024 * 1024


def _mm_tiles(M, N, K, in_bytes, out_bytes):
    tn = _tile(N, 1024)
    tk = K
    while True:
        for cap in (1024, 512, 256):
            tm = _tile(M, cap)
            need = 2 * tk * (tm * in_bytes[0] + tn * in_bytes[1]) + 2 * tm * tn * out_bytes
            need += tm * tn * 4 if tk < K else 0
            if need <= MM_VMEM_BUDGET:
                return tm, tn, tk
        tk = _tile(K, tk - 128)


def _mm(a, b, *, name, ta=False, tb=False, out_dtype=F32, extras=(), epilogue=None, out_dtypes=None,
        after=(), shape=None, tiles=None, b_spec=None, reds=()):
    if shape is None:
        M, K = (a.shape[1], a.shape[0]) if ta else a.shape
        N = b.shape[0] if tb else b.shape[1]
        assert K == (b.shape[1] if tb else b.shape[0]), (a.shape, b.shape, ta, tb)
    else:
        M, N, K = shape
    outs = tuple(out_dtypes or (out_dtype,))
    placed = [o for o in outs if isinstance(o, tuple)]
    ex_arrays = [e[0] if isinstance(e, tuple) else e for e in extras]
    if tiles is None:
        out_bytes = sum(jnp.dtype(d).itemsize for d in outs) + sum(e.dtype.itemsize for e in ex_arrays)
        tiles = _mm_tiles(M, N, K, (a.dtype.itemsize, b.dtype.itemsize), out_bytes)
    tm, tn, tk = tiles
    nk = K // tk
    dims = (((0 if ta else 1,), (1 if tb else 0,)), ((), ()))
    n_ex, n_out = len(extras), len(outs)
    n_pass = len(after) + len(placed)
    assert not reds or tn == N

    def body(*refs):
        a_ref, b_ref = refs[:2]
        ex_refs = refs[2:2 + n_ex]
        o_refs = refs[2 + n_ex + n_pass:2 + n_ex + n_pass + n_out]
        red_refs = refs[2 + n_ex + n_pass + n_out:2 + n_ex + n_pass + n_out + len(reds)]
        first_rows, k = pl.program_id(0) == 0, pl.program_id(2)
        bv = b_ref[...]
        if bv.ndim == 3:
            bv = bv.reshape(-1, bv.shape[-1])
        bv = bv.astype(BF16)

        part = lax.dot_general(a_ref[...].astype(BF16), bv, dims, preferred_element_type=F32)

        def finish(acc):
            ex = [r[...] for r in ex_refs]
            ex = [e.astype(F32) if e.dtype == BF16 else e for e in ex]
            res = epilogue(acc, *ex) if epilogue else (acc,) * n_out
            for o_ref, r in zip(o_refs, res):
                o_ref[...] = r.astype(o_ref.dtype)
            for red_ref, r in zip(red_refs, res[n_out:]):
                red_ref[...] += r

        if reds:
            @pl.when(first_rows & (k == 0))
            def _():
                for red_ref in red_refs:
                    red_ref[...] = jnp.zeros_like(red_ref)

        if nk == 1:
            finish(part)
        else:
            acc_ref = refs[-1]

            @pl.when(k == 0)
            def _():
                acc_ref[...] = part

            @pl.when((k > 0) & (k < nk - 1))
            def _():
                acc_ref[...] += part

            @pl.when(k == nk - 1)
            def _():
                finish(acc_ref[...] + part)

    a_spec = (pl.BlockSpec((tk, tm), lambda i, j, k: (k, i)) if ta
              else pl.BlockSpec((tm, tk), lambda i, j, k: (i, k)))
    if b_spec is None:
        b_spec = (pl.BlockSpec((tn, tk), lambda i, j, k: (j, k)) if tb
                  else pl.BlockSpec((tk, tn), lambda i, j, k: (k, j)))
    o_spec = pl.BlockSpec((tm, tn), lambda i, j, k: (i, j))
    row_spec = pl.BlockSpec((1, tn), lambda i, j, k: (0, j))
    ex_specs = []
    for e in extras:
        if isinstance(e, tuple):
            ex_specs.append(pl.BlockSpec((tm, tn), functools.partial(lambda i, j, k, cb: (i, cb), cb=e[1])))
        else:
            ex_specs.append(row_spec if e.shape[0] == 1 else o_spec)
    out_shape, out_specs, aliases = [], [], {}
    for o, out in enumerate(outs):
        if isinstance(out, tuple):
            out_shape.append(jax.ShapeDtypeStruct(out[0].shape, out[0].dtype))
            out_specs.append(out[1])
            aliases[2 + n_ex + len(after) + len(aliases)] = o
        else:
            out_shape.append(jax.ShapeDtypeStruct((M, N), out))
            out_specs.append(o_spec)
    passed = list(after) + [buf for buf, _ in placed]
    out_shape = out_shape + [jax.ShapeDtypeStruct((1, w), F32) for w in reds]
    out_specs = out_specs + [pl.BlockSpec((1, w), lambda i, j, k: (0, 0)) for w in reds]
    res = pl.pallas_call(
        body, name=name, out_shape=out_shape,
        grid=(M // tm, N // tn, nk),
        in_specs=[a_spec, b_spec] + ex_specs + [ANY] * len(passed),
        out_specs=out_specs, input_output_aliases=aliases,
        scratch_shapes=[pltpu.VMEM((tm, tn), F32)] if nk > 1 else [],
        compiler_params=_params("arbitrary" if reds else "parallel", "parallel", "arbitrary"),
    )(a, b, *ex_arrays, *passed)
    return res[0] if len(res) == 1 else res


class _LateWeights:
    def __init__(self, pack):
        self.pack = pack

    def fwd(self, x, n, *, name, tm=1024, **kw):
        T, off = x.shape[0], LATE_OFF[n]
        if n == "w_ff1":
            return _mm(x, self.pack, name=name, shape=(T, 4096, 1024), tiles=(_tile(T, tm), 1024, 1024),
                       b_spec=pl.BlockSpec((None, 1024, 1024), lambda i, j, k: (j, 0, 0)), **kw)
        if n == "w_ff2":
            view = self.pack.reshape(N_CHIPS, LATE_ROWS // 1024, 1024, 1024)
            return _mm(x, view, name=name, shape=(T, 1024, 4096), tiles=(_tile(T, min(tm, 512)), 1024, 4096),
                       b_spec=pl.BlockSpec((N_CHIPS, None, 1024, 1024), lambda i, j, k: (0, off // 1024, 0, 0)),
                       **kw)
        return _mm(x, self.pack.reshape(N_CHIPS, LATE_ROWS // 256, 256, 1024), name=name,
                   shape=(T, 1024, 1024), tiles=(_tile(T, tm), 1024, 1024), b_spec=self._whole(off), **kw)

    @staticmethod
    def _whole(off):
        return pl.BlockSpec((N_CHIPS, None, 256, 1024), lambda i, j, k: (0, off // 256, 0, 0))

    def bwd(self, dy, n, *, name, tm=1024, **kw):
        T, off = dy.shape[0], LATE_OFF[n]
        tm = _tile(T, tm)
        if n == "w_ff1":
            return _mm(dy, self.pack, tb=True, name=name, shape=(T, 1024, 4096), tiles=(tm, 1024, 1024),
                       b_spec=pl.BlockSpec((None, 1024, 1024), lambda i, j, k: (k, 0, 0)), **kw)
        if n == "w_ff2":
            return _mm(dy, self.pack, tb=True, name=name, shape=(T, 4096, 1024), tiles=(tm, 1024, 1024),
                       b_spec=pl.BlockSpec((None, 1024, 1024), lambda i, j, k: (j, off // 1024, 0)), **kw)
        return _mm(dy, self.pack.reshape(N_CHIPS, LATE_ROWS // 256, 256, 1024), tb=True, name=name,
                   shape=(T, 1024, 1024), tiles=(tm, 1024, 1024), b_spec=self._whole(off), **kw)


def _late_grad(x, dy, n, bufs, *, name):
    T, off = x.shape[0], LATE_OFF[n]
    if n == "w_ff1":
        spec = pl.BlockSpec((None, 512, 1024), lambda i, j, k: (j, i, 0))
        shape, tiles = (1024, 4096, T), (512, 1024, T)
    elif n == "w_ff2":
        spec = pl.BlockSpec((None, 512, 1024), lambda i, j, k: (i // 2, off // 512 + i % 2, 0))
        shape, tiles = (4096, 1024, T), (512, 1024, T)
    else:
        spec = pl.BlockSpec((None, 256, 1024), lambda i, j, k: (i, off // 256, 0))
        shape, tiles = (1024, 1024, T), (256, 1024, T)
    return _mm(x, dy, ta=True, name=name, shape=shape, tiles=tiles, out_dtypes=[(b, spec) for b in bufs])


def _rowwise(fn, rows, consts, outs, reds=(), *, name, tm=256, after=()):
    T = rows[0][0].shape[0]
    tm = min(tm, T)
    n_row, n_const, n_out = len(rows), len(consts), len(outs)
    placed = [o for o in outs if len(o) == 3]
    n_pass = len(after) + len(placed)

    def body(*refs):
        i = pl.program_id(0)
        vals = [r[...] for r in refs[:n_row + n_const]]
        vals = [v.astype(F32) if v.dtype == BF16 else v for v in vals]
        res = fn(*vals)
        out_refs = refs[n_row + n_const + n_pass:]
        for k in range(n_out):
            out_refs[k][...] = res[k].astype(out_refs[k].dtype)
        for k in range(len(reds)):
            ref = out_refs[n_out + k]

            @pl.when(i == 0)
            def _(ref=ref):
                ref[...] = jnp.zeros_like(ref)

            ref[...] += res[n_out + k]

    in_specs = [pl.BlockSpec((tm, w), functools.partial(lambda i, cb: (i, cb), cb=cb))
                for (_, w, cb) in rows]
    in_specs += [pl.BlockSpec(c.shape, lambda i: (0, 0)) for c in consts]
    in_specs += [ANY] * n_pass
    out_specs, out_shape, aliases = [], [], {}
    for o, out in enumerate(outs):
        if len(out) == 3:
            w, buf, cb = out
            out_specs.append(pl.BlockSpec((tm, w), functools.partial(lambda i, cb: (i, cb), cb=cb)))
            out_shape.append(jax.ShapeDtypeStruct(buf.shape, buf.dtype))
            aliases[n_row + n_const + len(after) + len(aliases)] = o
        else:
            out_specs.append(pl.BlockSpec((tm, out[0]), lambda i: (i, 0)))
            out_shape.append(jax.ShapeDtypeStruct((T, out[0]), out[1]))
    out_specs += [pl.BlockSpec(s, lambda i: (0, 0)) for s in reds]
    out_shape += [jax.ShapeDtypeStruct(s, F32) for s in reds]
    return pl.pallas_call(
        body, name=name, out_shape=out_shape, grid=(T // tm,),
        in_specs=in_specs, out_specs=out_specs, input_output_aliases=aliases,
        compiler_params=_params("arbitrary"),
    )(*[r[0] for r in rows], *consts, *after, *[o[1] for o in placed])


def _rms(x, g):
    r = lax.rsqrt(jnp.mean(x * x, axis=-1, keepdims=True) + EPS)
    return x * r * g


def _rms_bwd(x, g, dy):
    r = lax.rsqrt(jnp.mean(x * x, axis=-1, keepdims=True) + EPS)
    n = x * r
    dyg = dy * g
    dx = r * (dyg - n * jnp.mean(dyg * n, axis=-1, keepdims=True))
    return dx, jnp.sum(dy * n, axis=0, keepdims=True)


def _sigmoid(x):
    return 1.0 / (1.0 + jnp.exp(-x))


def _rope(x, cs, sa, sb):
    return x * cs + pltpu.roll(x, 32, 1) * sa + pltpu.roll(x, 96, 1) * sb


def _rope_t(dy, cs, sa, sb):
    return dy * cs + pltpu.roll(dy * sa, 96, 1) + pltpu.roll(dy * sb, 32, 1)


def _causal_mask(t):
    row = lax.broadcasted_iota(jnp.int32, (t, t), 0)
    col = lax.broadcasted_iota(jnp.int32, (t, t), 1)
    return col <= row


FLASH_FWD_TQ = 4096
FLASH_FWD_TK = 1024
FLASH_FWD_HEADS = 1
FLASH_BWD_TQ = 1024
FLASH_BWD_TK = 512
NT_DIMS = (((1,), (1,)), ((), ()))
TN_DIMS = (((0,), (0,)), ((), ()))


def _flash_fwd(q, k, v, *, tq=FLASH_FWD_TQ, tk=FLASH_FWD_TK, hp=FLASH_FWD_HEADS):
    T = q.shape[0]
    tq, tk = min(tq, T), min(tk, T)
    n_diag = tq // tk
    assert n_diag >= 1 and N_HEADS % hp == 0
    heads = range(hp)

    def body(q_ref, k_ref, v_ref, o_ref, lse_ref):
        i = pl.program_id(1)

        def cols(ref, rows, h, width=QK_PAD):
            return ref[rows, h * width:(h + 1) * width]

        def logits(j, r0=0):
            sl = pl.ds(pl.multiple_of(j * tk, tk), tk)
            return tuple(lax.dot_general(cols(q_ref, slice(r0, tq), h), cols(k_ref, sl, h), NT_DIMS,
                                         preferred_element_type=F32) for h in heads)

        def update(j, s, m, acc, masked, r0=0):
            sl = pl.ds(pl.multiple_of(j * tk, tk), tk)
            m_out, acc_out = [], []
            for h in heads:
                sh = s[h]
                if masked:
                    row = i * tq + r0 + lax.broadcasted_iota(jnp.int32, (tq - r0, tk), 0)
                    col = j * tk + lax.broadcasted_iota(jnp.int32, (tq - r0, tk), 1)
                    sh = jnp.where(col <= row, sh, NEG)
                m_old, acc_old = m[h][r0:], acc[h][r0:]
                m_new = jnp.maximum(m_old, jnp.max(sh, axis=-1, keepdims=True))
                p = jnp.exp2(sh - m_new)
                acc_new = (jnp.exp2(m_old - m_new) * acc_old
                           + jnp.dot(p.astype(BF16), cols(v_ref, sl, h), preferred_element_type=F32))
                if r0:
                    m_new = jnp.concatenate([m[h][:r0], m_new], axis=0)
                    acc_new = jnp.concatenate([acc[h][:r0], acc_new], axis=0)
                m_out.append(m_new)
                acc_out.append(acc_new)
            return tuple(m_out), tuple(acc_out)

        def step(j, carry):
            s, m, acc = carry
            s_next = logits(j + 1)
            m, acc = update(j, s, m, acc, False)
            return s_next, m, acc

        first = i * n_diag
        init = (logits(0), tuple(jnp.full((tq, 1), NEG, F32) for _ in heads),
                tuple(jnp.zeros((tq, QK_PAD), F32) for _ in heads))
        s, m, acc = lax.fori_loop(0, first, step, init)
        for d in range(n_diag):
            m, acc = update(first + d, s, m, acc, True, r0=d * tk)
            if d + 1 < n_diag:
                s = logits(first + d + 1, r0=(d + 1) * tk)
        for h in heads:
            l = acc[h][:, NOPE:]
            o_ref[:, h * NOPE:(h + 1) * NOPE] = (acc[h][:, :NOPE] / l).astype(BF16)
            lse_ref[:, h * NOPE:(h + 1) * NOPE] = m[h] + jnp.log2(l)

    return pl.pallas_call(
        body, name="flash_fwd",
        out_shape=[jax.ShapeDtypeStruct((T, N_HEADS * NOPE), BF16),
                   jax.ShapeDtypeStruct((T, N_HEADS * NOPE), F32)],
        grid=(N_HEADS // hp, T // tq),
        in_specs=[pl.BlockSpec((tq, hp * QK_PAD), lambda h, i: (i, h)),
                  pl.BlockSpec((T, hp * QK_PAD), lambda h, i: (0, h)),
                  pl.BlockSpec((T, hp * QK_PAD), lambda h, i: (0, h))],
        out_specs=[pl.BlockSpec((tq, hp * NOPE), lambda h, i: (i, h)),
                   pl.BlockSpec((tq, hp * NOPE), lambda h, i: (i, h))],
        compiler_params=_params("parallel", "arbitrary"),
    )(q, k, v)


def _flash_bwd(q, k, v, do, lse, *, tq=FLASH_BWD_TQ, tk=FLASH_BWD_TK):
    T = q.shape[0]
    tq, tk = min(tq, T), min(tk, T)

    def body(q_ref, k_ref, v_ref, do_ref, lse_ref, dq_ref, dk_ref, dv_ref, dq_acc):
        dq_acc[...] = jnp.zeros_like(dq_acc)
        for j in range(T // tk):
            k0 = j * tk
            kj = k_ref[k0:k0 + tk, :]
            vj = v_ref[k0:k0 + tk, :]
            dk = jnp.zeros((tk, QK_PAD), F32)
            dv = jnp.zeros((tk, NOPE), F32)
            for i in range(k0 // tq, T // tq):
                r0 = max(i * tq, k0)
                r1 = (i + 1) * tq
                qi = q_ref[r0:r1, :]
                doi = do_ref[r0:r1, :]
                s = lax.dot_general(qi, kj, NT_DIMS, preferred_element_type=F32)
                p = jnp.exp2(s - lse_ref[r0:r1, :][:, :1])
                if r0 < k0 + tk:
                    row = r0 + lax.broadcasted_iota(jnp.int32, (r1 - r0, tk), 0)
                    col = k0 + lax.broadcasted_iota(jnp.int32, (r1 - r0, tk), 1)
                    p = jnp.where(col <= row, p, 0.0)
                dv = dv + lax.dot_general(p.astype(BF16), doi[:, :NOPE], TN_DIMS, preferred_element_type=F32)
                ds = (p * lax.dot_general(doi, vj, NT_DIMS, preferred_element_type=F32)).astype(BF16)
                dk = dk + lax.dot_general(ds, qi, TN_DIMS, preferred_element_type=F32)
                dq_acc[r0:r1, :] += jnp.dot(ds, kj, preferred_element_type=F32)
            dk_ref[k0:k0 + tk, :] = dk.astype(BF16)
            dv_ref[k0:k0 + tk, :] = dv.astype(BF16)
        dq_ref[...] = dq_acc[...].astype(BF16)

    return pl.pallas_call(
        body, name="flash_bwd",
        out_shape=[jax.ShapeDtypeStruct((T, N_HEADS * QK_PAD), BF16),
                   jax.ShapeDtypeStruct((T, N_HEADS * QK_PAD), BF16),
                   jax.ShapeDtypeStruct((T, N_HEADS * NOPE), BF16)],
        grid=(N_HEADS,),
        in_specs=[pl.BlockSpec((T, QK_PAD), lambda h: (0, h)),
                  pl.BlockSpec((T, QK_PAD), lambda h: (0, h)),
                  pl.BlockSpec((T, QK_PAD), lambda h: (0, h)),
                  pl.BlockSpec((T, QK_PAD), lambda h: (0, h)),
                  pl.BlockSpec((T, NOPE), lambda h: (0, h))],
        out_specs=[pl.BlockSpec((T, QK_PAD), lambda h: (0, h)),
                   pl.BlockSpec((T, QK_PAD), lambda h: (0, h)),
                   pl.BlockSpec((T, NOPE), lambda h: (0, h))],
        scratch_shapes=[pltpu.VMEM((T, QK_PAD), F32)],
        compiler_params=_params("arbitrary"),
    )(q, k, v, do, lse)


SUB = 128


def _windows(buf, t0, c0, first, last):
    span = SUB + 8 * ((last + 7) // 8)
    base = buf[t0:t0 + span, c0:c0 + SUB]
    for r in range(8):
        offs = [o for o in range(first, last + 1) if o % 8 == r]
        if offs:
            win = base if r == 0 else pltpu.roll(base, span - r, 0)
            for o in offs:
                yield o, win[o - r:o - r + SUB, :]


def _glu(a, b):
    return a.astype(F32) * _sigmoid(b.astype(F32))


def _conv_fwd(z, w, bias, *, tm=256):
    T, C = z.shape[0], D_MODEL
    tm = min(tm, T)
    hb = tm // HALO
    lead = HALO - (CONV_W - 1)

    def body(a_ref, b_ref, ah_ref, bh_ref, w_ref, bias_ref, o_ref, buf):
        t = pl.program_id(0)
        buf[0:HALO, :] = jnp.where(t > 0, _glu(ah_ref[...], bh_ref[...]), 0.0)
        buf[HALO:, :] = _glu(a_ref[...], b_ref[...])
        for c0 in range(0, C, SUB):
            for t0 in range(0, tm, SUB):
                acc = jnp.broadcast_to(bias_ref[:, c0:c0 + SUB], (SUB, SUB))
                for o, win in _windows(buf, t0, c0, lead, lead + CONV_W - 1):
                    acc = acc + win * w_ref[o - lead:o - lead + 1, c0:c0 + SUB]
                o_ref[t0:t0 + SUB, c0:c0 + SUB] = acc

    def halo(cb):
        return pl.BlockSpec((HALO, C), lambda t: (jnp.maximum(t * hb - 1, 0), cb))

    return pl.pallas_call(
        body, name="conv_fwd",
        out_shape=jax.ShapeDtypeStruct((T, C), F32),
        grid=(T // tm,),
        in_specs=[pl.BlockSpec((tm, C), lambda t: (t, Z_A)), pl.BlockSpec((tm, C), lambda t: (t, Z_B)),
                  halo(Z_A), halo(Z_B),
                  pl.BlockSpec((CONV_TAPS_PAD, C), lambda t: (0, 0)), pl.BlockSpec((1, C), lambda t: (0, 0))],
        out_specs=pl.BlockSpec((tm, C), lambda t: (t, 0)),
        scratch_shapes=[pltpu.VMEM((HALO + tm, C), F32)],
        compiler_params=_params("parallel"),
    )(z, z, z, z, w, bias)


def _conv_bwd(z, du1, w, dz, *, tm=256):
    T, C = z.shape[0], D_MODEL
    tm = min(tm, T)
    hb = tm // HALO
    nt = T // tm
    last_halo = T // HALO - 1
    lead = HALO - (CONV_W - 1)
    assert (Z_A, Z_B) == (0, 1)

    def body(a_ref, b_ref, ah_ref, bh_ref, d_ref, dh_ref, w_ref, dz_in, dz_ref, dw_ref, ubuf, dbuf, dw8, gate):
        t = pl.program_id(0)
        ubuf[0:HALO, :] = jnp.where(t > 0, _glu(ah_ref[...], bh_ref[...]), 0.0)
        gate[...] = _sigmoid(b_ref[...].astype(F32))
        ubuf[HALO:, :] = a_ref[...].astype(F32) * gate[...]
        dbuf[0:tm, :] = d_ref[...]
        dbuf[tm:, :] = jnp.where(t < nt - 1, dh_ref[...], 0.0)

        @pl.when(t == 0)
        def _():
            dw8[...] = jnp.zeros_like(dw8)

        for c0 in range(0, C, SUB):
            for t0 in range(0, tm, SUB):
                acc = jnp.zeros((SUB, SUB), F32)
                for o, win in _windows(dbuf, t0, c0, 0, CONV_W - 1):
                    kk = CONV_W - 1 - o
                    acc = acc + win * w_ref[kk:kk + 1, c0:c0 + SUB]
                av = a_ref[t0:t0 + SUB, c0:c0 + SUB].astype(F32)
                sg = gate[t0:t0 + SUB, c0:c0 + SUB]
                dz_ref[t0:t0 + SUB, c0:c0 + SUB] = (acc * sg).astype(dz_ref.dtype)
                dz_ref[t0:t0 + SUB, C + c0:C + c0 + SUB] = (acc * av * sg * (1.0 - sg)).astype(dz_ref.dtype)
                dsub = d_ref[t0:t0 + SUB, c0:c0 + SUB]
                for o, win in _windows(ubuf, t0, c0, lead, lead + CONV_W - 1):
                    kk = o - lead
                    dw8[8 * kk:8 * kk + 8, c0:c0 + SUB] += jnp.sum(
                        (dsub * win).reshape(SUB // 8, 8, SUB), axis=0)

        @pl.when(t == nt - 1)
        def _():
            dw_ref[...] = jnp.sum(dw8[...].reshape(CONV_TAPS_PAD, 8, C), axis=1)

    def halo(cb):
        return pl.BlockSpec((HALO, C), lambda t: (jnp.maximum(t * hb - 1, 0), cb))

    return pl.pallas_call(
        body, name="conv_bwd",
        out_shape=[jax.ShapeDtypeStruct(dz.shape, dz.dtype), jax.ShapeDtypeStruct((CONV_TAPS_PAD, C), F32)],
        grid=(nt,),
        in_specs=[pl.BlockSpec((tm, C), lambda t: (t, Z_A)), pl.BlockSpec((tm, C), lambda t: (t, Z_B)),
                  halo(Z_A), halo(Z_B),
                  pl.BlockSpec((tm, C), lambda t: (t, 0)),
                  pl.BlockSpec((HALO, C), lambda t: (jnp.minimum((t + 1) * hb, last_halo), 0)),
                  pl.BlockSpec((CONV_TAPS_PAD, C), lambda t: (0, 0)), ANY],
        out_specs=[pl.BlockSpec((tm, 2 * C), lambda t: (t, 0)),
                   pl.BlockSpec((CONV_TAPS_PAD, C), lambda t: (0, 0))],
        input_output_aliases={7: 0},
        scratch_shapes=[pltpu.VMEM((HALO + tm, C), F32), pltpu.VMEM((tm + HALO, C), F32),
                        pltpu.VMEM((CONV_TAPS_PAD * 8, C), F32), pltpu.VMEM((tm, C), F32)],
        compiler_params=_params("arbitrary"),
    )(z, z, z, z, du1, du1, w, dz)


def _adamw(w, g, m, v, *, name, g_row=None):
    R, C = w.shape
    tr = _row_tile(R, max(8, (1 << 18) // C // 8 * 8))
    first = 0 if g_row is None else g_row // tr
    assert g_row is None or g_row % tr == 0

    def body(w_ref, g_ref, m_ref, v_ref, d_ref, nm_ref, nv_ref, *g_out):
        gv = g_ref[...]
        nm = ADAM_B1 * m_ref[...] + (1.0 - ADAM_B1) * gv
        nv = ADAM_B2 * v_ref[...] + (1.0 - ADAM_B2) * (gv * gv)
        m_hat = nm / (1.0 - ADAM_B1 ** ADAM_STEP)
        v_hat = nv / (1.0 - ADAM_B2 ** ADAM_STEP)
        d_ref[...] = -ADAM_LR * (m_hat / (jnp.sqrt(v_hat) + ADAM_EPS) + ADAM_WD * w_ref[...])
        nm_ref[...] = nm
        nv_ref[...] = nv
        for ref in g_out:
            ref[...] = gv

    spec = pl.BlockSpec((tr, C), lambda i: (i, 0))
    g_spec = pl.BlockSpec((tr, C), lambda i: (first + i, 0))
    n_out = 3 if g_row is None else 4
    return pl.pallas_call(
        body, name=name, out_shape=[jax.ShapeDtypeStruct((R, C), F32)] * n_out,
        grid=(R // tr,), in_specs=[spec, g_spec, spec, spec], out_specs=[spec] * n_out,
        compiler_params=_params("parallel"),
    )(w, g, m, v)


def _coords():
    return lax.axis_index("x"), lax.axis_index("y"), lax.axis_index("c")


def _remote(src, dst, ssem, rsem, to):
    return pltpu.make_async_remote_copy(src_ref=src, dst_ref=dst, send_sem=ssem, recv_sem=rsem,
                                        device_id=to, device_id_type=MESH)


def _half(c, rows):
    return pl.ds(pl.multiple_of(c * (rows // 2), 16), rows // 2)


def _gather_weights(wpack, cw):
    def body(w_ref, cw_ref, full_ref, cwf_ref, ssem, rsem):
        x, y, c = _coords()
        q = 2 * x + y
        sib = (x, y, 1 - c)
        chips = [(1 - x, y), (x, 1 - y), (1 - x, 1 - y)]
        mine, other = _half(c, w_ref.shape[0]), _half(1 - c, w_ref.shape[0])
        first = []
        for j, (px, py) in enumerate(chips):
            first.append(_remote(w_ref.at[mine], full_ref.at[q, mine], ssem.at[j], rsem.at[j], (px, py, c)))
            first.append(_remote(cw_ref, cwf_ref.at[q], ssem.at[6 + j], rsem.at[6 + j], (px, py, c)))
        for cp in first:
            cp.start()
        passed = []
        for j, (px, py) in enumerate(chips):
            pq = 2 * px + py
            _remote(w_ref.at[mine], full_ref.at[pq, mine], ssem.at[j], rsem.at[j], (px, py, c)).wait_recv()
            fw = _remote(full_ref.at[pq, mine], full_ref.at[pq, mine], ssem.at[3 + j], rsem.at[3 + j], sib)
            fw.start()
            passed.append(fw)
        for j, (px, py) in enumerate(chips):
            pq = 2 * px + py
            _remote(full_ref.at[pq, other], full_ref.at[pq, other], ssem.at[3 + j], rsem.at[3 + j], sib).wait_recv()
            _remote(cw_ref, cwf_ref.at[pq], ssem.at[6 + j], rsem.at[6 + j], (px, py, c)).wait_recv()
        for cp in first + passed:
            cp.wait_send()

    return pl.pallas_call(
        body, name="gather_weights",
        out_shape=[jax.ShapeDtypeStruct((N_CHIPS,) + wpack.shape, wpack.dtype),
                   jax.ShapeDtypeStruct((N_CHIPS,) + cw.shape, cw.dtype)],
        in_specs=[ANY, ANY], out_specs=[ANY, ANY],
        scratch_shapes=[pltpu.SemaphoreType.DMA((9,)), pltpu.SemaphoreType.DMA((9,))],
    )(wpack, cw)


SEM = pl.BlockSpec(memory_space=pltpu.SEMAPHORE)
HBM = pl.BlockSpec(memory_space=pltpu.HBM)
N_LATE = 6


def _late_peers(x, y, c):
    out = []
    for j, (px, py) in enumerate([(1 - x, y), (x, 1 - y), (1 - x, 1 - y)]):
        for t in (0, 1):
            out.append((2 * j + t, px, py, c if t == 0 else 1 - c))
    return out


def _late_gather_start(w, after):
    rows = w.shape[0]

    def body(w_ref, land_ref, after_ref, *outs):
        sems = outs[:2 * N_LATE]
        token = outs[2 * N_LATE + 2]
        x, y, c = _coords()
        q = 2 * x + y
        mine = _half(c, rows)
        for k, px, py, pc in _late_peers(x, y, c):
            _remote(w_ref.at[mine], land_ref.at[q, mine], sems[k], sems[N_LATE + k], (px, py, pc)).start()
        token[...] = jnp.zeros_like(token)

    res = pl.pallas_call(
        body, name="late_gather_start",
        out_shape=tuple([pltpu.SemaphoreType.DMA(())] * (2 * N_LATE)) + (
            pltpu.HBM(w.shape, w.dtype), pltpu.HBM((N_CHIPS,) + w.shape, w.dtype),
            jax.ShapeDtypeStruct((8, 128), F32)),
        in_specs=(HBM, HBM, ANY),
        out_specs=tuple([SEM] * (2 * N_LATE)) + (HBM, HBM, pl.BlockSpec(memory_space=pltpu.VMEM)),
        input_output_aliases={0: 2 * N_LATE, 1: 2 * N_LATE + 1},
        compiler_params=pltpu.CompilerParams(has_side_effects=pltpu.SideEffectType.DATAFLOW_SIDE_EFFECTING),
    )(pltpu.with_memory_space_constraint(w, pltpu.HBM),
      pltpu.with_memory_space_constraint(lax.empty((N_CHIPS,) + w.shape, w.dtype), pltpu.HBM), after)
    return res[:2 * N_LATE], res[2 * N_LATE], res[2 * N_LATE + 1], res[2 * N_LATE + 2]


def _late_gather_wait(sems, w_thru, land_thru, after):
    rows = w_thru.shape[0]

    def body(w_ref, land_ref, *rest):
        sems = rest[:2 * N_LATE]
        x, y, c = _coords()
        for k, px, py, pc in _late_peers(x, y, c):
            cp = _remote(w_ref.at[_half(pc, rows)], land_ref.at[2 * px + py, _half(pc, rows)],
                         sems[k], sems[N_LATE + k], (px, py, pc))
            cp.wait_send()
            cp.wait_recv()

    return pl.pallas_call(
        body, name="late_gather_wait",
        out_shape=(pltpu.HBM(w_thru.shape, w_thru.dtype), pltpu.HBM(land_thru.shape, land_thru.dtype)),
        in_specs=(HBM, HBM) + tuple([SEM] * (2 * N_LATE)) + (ANY,),
        out_specs=(HBM, HBM), input_output_aliases={0: 0, 1: 1},
        compiler_params=pltpu.CompilerParams(has_side_effects=pltpu.SideEffectType.DATAFLOW_SIDE_EFFECTING),
    )(w_thru, land_thru, *sems, after)[1]


N_PEERS = N_DEV - 1


def _peers(x, y, c):
    out = []
    for r in range(1, N_DEV):
        out.append((r - 1, 1 - x if r & 4 else x, 1 - y if r & 2 else y, 1 - c if r & 1 else c))
    return out


def _piece(g_ref, px, py, pc, whole):
    return g_ref if whole else g_ref.at[2 * px + py, _half(pc, g_ref.shape[1])]


def _reduce_start(g16, after, *, name, whole=False):
    def body(g_ref, land_ref, after_ref, *outs):
        sems = outs[:2 * N_PEERS]
        token = outs[2 * N_PEERS + 2]
        x, y, c = _coords()
        for k, px, py, pc in _peers(x, y, c):
            _remote(_piece(g_ref, px, py, pc, whole), land_ref.at[k], sems[k], sems[N_PEERS + k],
                    (px, py, pc)).start()
        token[...] = jnp.zeros_like(token)

    land_shape = (N_PEERS,) + (g16.shape if whole else (g16.shape[1] // 2, 1024))
    res = pl.pallas_call(
        body, name=name,
        out_shape=tuple([pltpu.SemaphoreType.DMA(())] * (2 * N_PEERS)) + (
            pltpu.HBM(g16.shape, g16.dtype), pltpu.HBM(land_shape, g16.dtype),
            jax.ShapeDtypeStruct((8, 128), F32)),
        in_specs=(HBM, HBM, ANY),
        out_specs=tuple([SEM] * (2 * N_PEERS)) + (HBM, HBM, pl.BlockSpec(memory_space=pltpu.VMEM)),
        input_output_aliases={0: 2 * N_PEERS, 1: 2 * N_PEERS + 1},
        compiler_params=pltpu.CompilerParams(has_side_effects=pltpu.SideEffectType.DATAFLOW_SIDE_EFFECTING),
    )(pltpu.with_memory_space_constraint(g16, pltpu.HBM),
      pltpu.with_memory_space_constraint(lax.empty(land_shape, g16.dtype), pltpu.HBM), after)
    return res[:2 * N_PEERS], res[2 * N_PEERS], res[2 * N_PEERS + 1], res[2 * N_PEERS + 2]


def _reduce_wait(sems, g_thru, land_thru, after, *, name, whole=False):
    def body(g_ref, land_ref, *rest):
        sems = rest[:2 * N_PEERS]
        x, y, c = _coords()
        for k, px, py, pc in _peers(x, y, c):
            cp = _remote(_piece(g_ref, px, py, pc, whole), land_ref.at[k], sems[k], sems[N_PEERS + k],
                         (px, py, pc))
            cp.wait_send()
            cp.wait_recv()

    return pl.pallas_call(
        body, name=name,
        out_shape=(pltpu.HBM(g_thru.shape, g_thru.dtype), pltpu.HBM(land_thru.shape, land_thru.dtype)),
        in_specs=(HBM, HBM) + tuple([SEM] * (2 * N_PEERS)) + tuple([ANY] * len(after)),
        out_specs=(HBM, HBM), input_output_aliases={0: 0, 1: 1},
        compiler_params=pltpu.CompilerParams(has_side_effects=pltpu.SideEffectType.DATAFLOW_SIDE_EFFECTING),
    )(g_thru, land_thru, *sems, *after)[1]


def _reduce_sum(place, g32, land, *, name):
    rows = g32.shape[1]
    nb = rows // 2 // SUM_TILE

    def body(place_ref, g_ref, l_ref, o_ref):
        acc = g_ref[...]
        for k in range(N_PEERS):
            acc = acc + l_ref[k].astype(F32)
        o_ref[...] = acc

    return pl.pallas_call(
        body, name=name,
        out_shape=jax.ShapeDtypeStruct((rows, 1024), F32),
        grid_spec=pltpu.PrefetchScalarGridSpec(
            num_scalar_prefetch=1, grid=(nb,),
            in_specs=[pl.BlockSpec((None, SUM_TILE, 1024), lambda i, s: (s[0], s[1] * nb + i, 0)),
                      pl.BlockSpec((N_PEERS, SUM_TILE, 1024), lambda i, s: (0, i, 0))],
            out_specs=pl.BlockSpec((SUM_TILE, 1024), lambda i, s: (s[1] * nb + i, 0))),
        compiler_params=_params("parallel"),
    )(place, g32, land)


def _join_halves(red, *, name):
    rows = red.shape[0]

    def body(r_ref, o_ref, ssem, rsem):
        x, y, c = _coords()
        cp = _remote(r_ref.at[_half(c, rows)], o_ref.at[_half(c, rows)], ssem, rsem, (x, y, 1 - c))
        cp.start()
        _remote(r_ref.at[_half(1 - c, rows)], o_ref.at[_half(1 - c, rows)], ssem, rsem,
                (x, y, 1 - c)).wait_recv()
        cp.wait_send()

    return pl.pallas_call(
        body, name=name,
        out_shape=jax.ShapeDtypeStruct(red.shape, F32),
        in_specs=[ANY], out_specs=ANY, input_output_aliases={0: 0},
        scratch_shapes=[pltpu.SemaphoreType.DMA, pltpu.SemaphoreType.DMA],
    )(red)


def _sum_small(me, svec, land):
    def body(me_ref, s_ref, l_ref, o_ref):
        mine = me_ref[0]
        acc = None
        for d in range(N_DEV):
            theirs = l_ref[jnp.maximum(jnp.bitwise_xor(mine, d) - 1, 0)]
            v = jnp.where(mine == d, s_ref[...], theirs)
            acc = v if acc is None else acc + v
        o_ref[...] = acc

    return pl.pallas_call(
        body, name="sum_small",
        out_shape=jax.ShapeDtypeStruct(svec.shape, F32),
        grid_spec=pltpu.PrefetchScalarGridSpec(
            num_scalar_prefetch=1, grid=(1,),
            in_specs=[pl.BlockSpec(svec.shape, lambda i, s: (0, 0)),
                      pl.BlockSpec(land.shape, lambda i, s: (0, 0, 0))],
            out_specs=pl.BlockSpec(svec.shape, lambda i, s: (0, 0))),
        compiler_params=_params("arbitrary"),
    )(me, svec, land)


def _pack_early_shards(wts):
    parts = [wts["w_in"][0].T.astype(BF16)] + [wts[n].astype(BF16).reshape(-1, 1024) for n, _ in EARLY[1:]]
    used = sum(r for _, r in EARLY)
    return jnp.concatenate(parts + [jnp.zeros((EARLY_ROWS - used, 1024), BF16)], axis=0)


def _pack_late_shards(wts):
    parts, at = [], 0
    for n, off, rows in LATE:
        if off > at:
            parts.append(jnp.zeros((off - at, 1024), BF16))
        parts.append(wts[n].astype(BF16).reshape(rows, 1024))
        at = off + rows
    return jnp.concatenate(parts, axis=0)


def _unpack_early(full):
    w = {}
    w["w_in_t"] = jnp.concatenate(
        [full[0, N_SMALL:1200]] + [full[p, :1200] for p in range(1, N_CHIPS)]
        + [full[0, :N_SMALL], jnp.zeros((D_MODEL - N_SMALL, 1024), full.dtype)], axis=0)
    w["w_uq"] = full[:, 1200:1344].reshape(4, 384, 384).transpose(1, 0, 2).reshape(384, 1536)
    w["w_uk"] = full[:, 1344:1408].reshape(256, 1024)
    w["w_uv"] = full[:, 1408:1472].reshape(256, 1024)
    return w


def _pack_early_grads(g):
    gt = g["w_in_t"]
    rest = 1200 - N_SMALL
    w_in = jnp.stack([jnp.concatenate([gt[4 * D_MODEL:4 * D_MODEL + N_SMALL], gt[:rest]], axis=0)]
                     + [gt[rest + 1200 * (p - 1):rest + 1200 * p] for p in range(1, N_CHIPS)])
    parts = [
        w_in,
        g["w_uq"].reshape(384, 4, 384).transpose(1, 0, 2).reshape(4, 144, 1024),
        g["w_uk"].reshape(4, 64, 1024),
        g["w_uv"].reshape(4, 64, 1024),
        jnp.zeros((4, EARLY_ROWS - sum(r for _, r in EARLY), 1024), F32),
    ]
    return jnp.concatenate(parts, axis=1)


def _pad_w_uq(w):
    w = w.reshape(Q_RANK, N_HEADS, NOPE + ROPE)
    return jnp.pad(w, ((0, 0), (0, 0), (0, QK_PAD - NOPE - ROPE))).reshape(Q_RANK, N_HEADS * QK_PAD)


def _local_step(xs, pos, tgt, W, late_weights, late_bufs, send_late_grads, send_early_grads, cw, vec):
    D = D_MODEL
    w_in_t = W["w_in_t"]
    w_uq = _pad_w_uq(W["w_uq"])
    w_ukv = jnp.concatenate([W["w_uk"], W["w_uv"]], axis=1)
    inv = ROPE_THETA ** (-jnp.arange(0, ROPE, 2, dtype=F32) / ROPE)
    inv_freq = jnp.concatenate([inv, inv, jnp.zeros((128 - ROPE,), F32)])[None, :]

    (h,) = _rowwise(lambda x, g: (_rms(x, g),), [(xs, D, 0)], [vec["norm_mix_pre"]], [(D, BF16)],
                    name="rms_pre")
    z = _mm(h, w_in_t, tb=True, out_dtype=BF16, name="mm_in")

    def qkv_prep(zs, p, qn, kvn, invf):
        ang = p.astype(F32) * invf
        cosv, sinv = jnp.cos(ang), jnp.sin(ang)
        ln = lax.broadcasted_iota(jnp.int32, ang.shape, 1)
        cs = jnp.where(ln < ROPE, cosv, 0.0)
        sa = jnp.where((ln >= ROPE // 2) & (ln < ROPE), sinv, 0.0)
        sb = jnp.where(ln < ROPE // 2, -sinv, 0.0)
        cqn = _rms(zs[:, :Q_RANK], qn)
        ckvn = _rms(zs[:, Q_RANK:Q_RANK + KV_RANK], kvn)
        kr = _rope(zs[:, 640:768], cs, sa, sb)
        return cqn, ckvn, kr, cs, sa, sb

    cqn, ckvn, krot, cs, sa, sb = _rowwise(
        qkv_prep, [(z, D, Z_S), (pos, 1, 0)], [vec["q_norm"], vec["kv_norm"], inv_freq],
        [(Q_RANK, BF16), (KV_RANK, BF16), (128, BF16), (128, F32), (128, F32), (128, F32)],
        name="qkv_prep")
    q_pre = _mm(cqn, w_uq, out_dtype=BF16, name="mm_uq")
    kv = _mm(ckvn, w_ukv, out_dtype=BF16, name="mm_ukv")

    def qk_rope(qp, kn, vv, kr, cs_, sa_, sb_):
        qs, ks, vs = [], [], []
        ones = jnp.ones((qp.shape[0], NOPE), F32)
        for hd in range(N_HEADS):
            qs.append(qp[:, hd * QK_PAD:hd * QK_PAD + NOPE])
            qs.append(_rope(qp[:, hd * QK_PAD + NOPE:(hd + 1) * QK_PAD], cs_, sa_, sb_))
            ks.append(kn[:, hd * NOPE:(hd + 1) * NOPE])
            ks.append(kr.astype(F32))
            vs.append(vv[:, hd * NOPE:(hd + 1) * NOPE])
            vs.append(ones)
        return (jnp.concatenate(qs, axis=1) * Q_SCALE, jnp.concatenate(ks, axis=1),
                jnp.concatenate(vs, axis=1))

    q, k, v = _rowwise(
        qk_rope, [(q_pre, 2048, 0), (kv, 1024, 0), (kv, 1024, 1), (krot, 128, 0), (cs, 128, 0),
                  (sa, 128, 0), (sb, 128, 0)], [],
        [(2048, BF16), (2048, BF16), (2048, BF16)], name="qk_rope")
    attn, lse = _flash_fwd(q, k, v)
    LW = late_weights(attn)
    y_attn = LW.fwd(attn, "w_o_attn", out_dtype=BF16, name="mm_o_attn")

    u1 = _conv_fwd(z, cw, vec["conv_b"])

    def ln_parts(u, g, b):
        mu = jnp.mean(u, axis=-1, keepdims=True)
        xc = u - mu
        rs = lax.rsqrt(jnp.mean(xc * xc, axis=-1, keepdims=True) + EPS)
        nh = xc * rs
        return nh, rs, nh * g + b

    def ln_silu(u, g, b):
        y = ln_parts(u, g, b)[2]
        return (y * _sigmoid(y),)

    (u3,) = _rowwise(ln_silu, [(u1, D, 0)], [vec["conv_ln_g"], vec["conv_ln_b"]], [(D, BF16)],
                     name="ln_silu")

    def merge(yc, ga, gc, ya, b2):
        return yc, _sigmoid(ga) * ya + _sigmoid(gc) * (yc + b2)

    y_conv, merged = LW.fwd(u3, "w_pw2", name="mm_pw2", tm=512, epilogue=merge, out_dtypes=(BF16, BF16),
                            extras=[(z, Z_GA), (z, Z_GC), y_attn, vec["b_pw2"]])

    def post1(m, x, g1, g2):
        x1 = x + _rms(m, g1)
        return m, x1, _rms(x1, g2)

    mo, x1, h2 = LW.fwd(merged, "w_out", name="mm_out", tm=512, epilogue=post1, out_dtypes=(F32, F32, BF16),
                        extras=[xs, vec["norm_mix_post"], vec["norm_mlp_pre"]])

    def sqrelu(acc):
        r = jnp.maximum(acc, 0.0)
        return r * r, r

    a2, r1 = LW.fwd(h2, "w_ff1", name="mm_ff1", epilogue=sqrelu, out_dtypes=(BF16, BF16))

    def post2(f_, x1_, t_, g):
        e = x1_ + _rms(f_, g) - t_
        dy = e * (1.0 / D)
        df, dg = _rms_bwd(f_, g, dy)
        loss = jnp.broadcast_to(jnp.sum(e * e, keepdims=True) * (0.5 / D), (1, 128))
        return dy, df, dg, loss

    dy, df, g_norm_mlp_post, loss = LW.fwd(a2, "w_ff2", name="mm_ff2", epilogue=post2, out_dtypes=(F32, BF16),
                                           extras=[x1, tgt, vec["norm_mlp_post"]], reds=(D, 128))
    late_bufs = _late_grad(a2, df, "w_ff2", late_bufs, name="mm_dw_ff2")
    df1 = LW.bwd(df, "w_ff2", name="mm_df1", extras=[r1], out_dtypes=(BF16,),
              epilogue=lambda acc, r: (acc * (2.0 * r.astype(F32)),))
    late_bufs = _late_grad(h2, df1, "w_ff1", late_bufs, name="mm_dw_ff1")

    def bwd_mid(dh2_, dy_, x1_, m, g2, g1):
        d1, dg2 = _rms_bwd(x1_, g2, dh2_)
        dx1_ = dy_ + d1
        dm, dg1 = _rms_bwd(m, g1, dx1_)
        return dx1_, dm, dg2, dg1

    dx1, dmo, g_norm_mlp_pre, g_norm_mix_post = LW.bwd(
        df1, "w_ff1", name="mm_dh2", tm=512, epilogue=bwd_mid, out_dtypes=(F32, BF16), reds=(D, D),
        extras=[dy, x1, mo, vec["norm_mlp_pre"], vec["norm_mix_post"]])
    late_bufs = _late_grad(merged, dmo, "w_out", late_bufs, name="mm_dw_out")
    def dmerge(dm, ga, gc, ya, yc, b2):
        sga, sgc = _sigmoid(ga), _sigmoid(gc)
        dya = dm * sga
        dyc = dm * sgc
        dga = dm * ya * sga * (1.0 - sga)
        dgc = dm * (yc + b2) * sgc * (1.0 - sgc)
        return dya, dyc, jnp.concatenate([dga, dgc], axis=1), jnp.sum(dyc, axis=0, keepdims=True)

    dz = lax.empty((xs.shape[0], Z_COLS), BF16)
    tm_dm = _tile(xs.shape[0], 512)
    gates = pl.BlockSpec((tm_dm, 2 * D), lambda i, j, k: (i, Z_GA // 2))
    dya, dyc, dz, g_b_pw2 = LW.bwd(
        dmo, "w_out", name="mm_dmerged", tm=tm_dm, epilogue=dmerge, out_dtypes=(BF16, BF16, (dz, gates)),
        reds=(D,), extras=[(z, Z_GA), (z, Z_GC), y_attn, y_conv, vec["b_pw2"]])

    late_bufs = _late_grad(attn, dya, "w_o_attn", late_bufs, name="mm_dw_o_attn")
    late_bufs = _late_grad(u3, dyc, "w_pw2", late_bufs, name="mm_dw_pw2")
    late_token = send_late_grads(late_bufs)

    def delta_fn(acc, o_):
        do32 = acc.astype(BF16).astype(F32)
        pr = do32 * o_
        ln = lax.broadcasted_iota(jnp.int32, (pr.shape[0], NOPE), 1)
        cols = []
        for hd in range(N_HEADS):
            dl = jnp.sum(pr[:, hd * NOPE:(hd + 1) * NOPE], axis=-1, keepdims=True)
            hi = dl.astype(BF16).astype(F32)
            cols.append(do32[:, hd * NOPE:(hd + 1) * NOPE])
            cols.append(jnp.where(ln == 0, -hi, jnp.where(ln == 1, hi - dl, 0.0)))
        return (jnp.concatenate(cols, axis=1),)

    tm_do = _tile(xs.shape[0], 512)
    do_ext = LW.bwd(dya, "w_o_attn", name="mm_dattn", tm=tm_do, epilogue=delta_fn, extras=[attn],
                    after=[late_token], out_dtypes=(
                        (lax.empty((xs.shape[0], N_HEADS * QK_PAD), BF16),
                         pl.BlockSpec((tm_do, N_HEADS * QK_PAD), lambda i, j, k: (i, 0))),))
    dq, dk, dv = _flash_bwd(q, k, v, do_ext, lse)

    def unrope(dq_, dk_, dv_, cs_, sa_, sb_):
        dq_, dk_ = dq_.astype(F32), dk_.astype(F32)
        qs, kn = [], []
        dkr = jnp.zeros_like(cs_)
        for hd in range(N_HEADS):
            qs.append(dq_[:, hd * QK_PAD:hd * QK_PAD + NOPE] * SCALE)
            qs.append(_rope_t(dq_[:, hd * QK_PAD + NOPE:(hd + 1) * QK_PAD] * SCALE, cs_, sa_, sb_))
            kn.append(dk_[:, hd * QK_PAD:hd * QK_PAD + NOPE] * K_UNSCALE)
            dkr = dkr + dk_[:, hd * QK_PAD + NOPE:(hd + 1) * QK_PAD]
        return (jnp.concatenate(qs, axis=1), jnp.concatenate(kn + [dv_], axis=1),
                _rope_t(dkr * K_UNSCALE, cs_, sa_, sb_))

    dq_pre, dkv, dkr = _rowwise(
        unrope, [(dq, 2048, 0), (dk, 2048, 0), (dv, 1024, 0), (cs, 128, 0), (sa, 128, 0), (sb, 128, 0)], [],
        [(2048, BF16), (2048, BF16), (128, F32)], name="unrope")
    g_w_uq = _mm(cqn, dq_pre, ta=True, name="mm_dw_uq")
    dcqn = _mm(dq_pre, w_uq, tb=True, out_dtype=BF16, name="mm_dcqn")
    g_w_ukv = _mm(ckvn, dkv, ta=True, name="mm_dw_ukv")
    dckvn = _mm(dkv, w_ukv, tb=True, out_dtype=BF16, name="mm_dckvn")

    def small_bwd(zs, dcq_, dckv_, dkr_, qn, kvn):
        dcq, dqn = _rms_bwd(zs[:, :Q_RANK], qn, dcq_)
        dckv, dkvn = _rms_bwd(zs[:, Q_RANK:Q_RANK + KV_RANK], kvn, dckv_)
        pad = jnp.zeros((zs.shape[0], D - 768), F32)
        return jnp.concatenate([dcq, dckv, dkr_, pad], axis=1), dqn, dkvn

    dz, g_q_norm, g_kv_norm = _rowwise(
        small_bwd, [(z, D, Z_S), (dcqn, Q_RANK, 0), (dckvn, KV_RANK, 0), (dkr, 128, 0)],
        [vec["q_norm"], vec["kv_norm"]], [(D, dz, Z_S)], [(1, Q_RANK), (1, KV_RANK)], name="small_bwd")

    def ln_silu_bwd(d3, u, g, b):
        nh, rs, y = ln_parts(u, g, b)
        sg = _sigmoid(y)
        dyv = d3 * (sg * (1.0 + y * (1.0 - sg)))
        dnh = dyv * g
        du = rs * (dnh - jnp.mean(dnh, axis=-1, keepdims=True)
                   - nh * jnp.mean(dnh * nh, axis=-1, keepdims=True))
        return (du, jnp.sum(dyv * nh, axis=0, keepdims=True), jnp.sum(dyv, axis=0, keepdims=True),
                jnp.sum(du, axis=0, keepdims=True))

    du1, g_ln_g, g_ln_b, g_conv_b = LW.bwd(
        dyc, "w_pw2", name="mm_du3", tm=512, epilogue=ln_silu_bwd, out_dtypes=(F32,), reds=(D, D, D),
        extras=[u1, vec["conv_ln_g"], vec["conv_ln_b"]])
    dz, g_conv_w = _conv_bwd(z, du1, cw, dz)
    g_w_in_t = _mm(dz, h, ta=True, name="mm_dw_in")
    early_token = send_early_grads({
        "w_in_t": g_w_in_t,
        "w_uq": g_w_uq.reshape(Q_RANK, N_HEADS, QK_PAD)[:, :, :NOPE + ROPE].reshape(Q_RANK, 1536),
        "w_uk": g_w_ukv[:, :1024], "w_uv": g_w_ukv[:, 1024:]})

    def final(dh_, dx1_, x, g):
        d, dg = _rms_bwd(x, g, dh_)
        return dx1_ + d, dg

    grad_x, g_norm_mix_pre = _mm(
        dz, w_in_t, name="mm_dh", after=[early_token], tiles=(_tile(xs.shape[0], 512), D, Z_COLS // 2),
        epilogue=final, out_dtypes=(F32,), reds=(D,), extras=[dx1, xs, vec["norm_mix_pre"]])

    gs = {
        "norm_mix_pre": g_norm_mix_pre, "conv_b": g_conv_b, "conv_ln_g": g_ln_g, "conv_ln_b": g_ln_b,
        "b_pw2": g_b_pw2, "norm_mix_post": g_norm_mix_post, "norm_mlp_pre": g_norm_mlp_pre,
        "norm_mlp_post": g_norm_mlp_post, "q_norm": g_q_norm, "kv_norm": g_kv_norm,
    }
    return grad_x, gs, g_conv_w, loss


def _row1024(a):
    return jnp.pad(a, ((0, 0), (0, 1024 - a.shape[1])))


BIG = tuple(n for n, _ in EARLY) + tuple(n for n, _, _ in LATE)
ORDER = ("norm_mix_pre", "w_in", "q_norm", "w_uq", "kv_norm", "w_uk", "w_uv", "w_o_attn", "conv_w",
         "conv_b", "conv_ln_g", "conv_ln_b", "w_pw2", "b_pw2", "w_out", "norm_mix_post", "norm_mlp_pre",
         "w_ff1", "w_ff2", "norm_mlp_post")


def kernel(x, positions, norm_mix_pre, w_in, q_norm, w_uq, kv_norm, w_uk, w_uv, w_o_attn, conv_w, conv_b, conv_ln_g, conv_ln_b, w_pw2, b_pw2, w_out, norm_mix_post, norm_mlp_pre, w_ff1, w_ff2, norm_mlp_post, loss_target, m_norm_mix_pre, m_w_in, m_q_norm, m_w_uq, m_kv_norm, m_w_uk, m_w_uv, m_w_o_attn, m_conv_w, m_conv_b, m_conv_ln_g, m_conv_ln_b, m_w_pw2, m_b_pw2, m_w_out, m_norm_mix_post, m_norm_mlp_pre, m_w_ff1, m_w_ff2, m_norm_mlp_post, v_norm_mix_pre, v_w_in, v_q_norm, v_w_uq, v_kv_norm, v_w_uk, v_w_uv, v_w_o_attn, v_conv_w, v_conv_b, v_conv_ln_g, v_conv_ln_b, v_w_pw2, v_b_pw2, v_w_out, v_norm_mix_post, v_norm_mlp_pre, v_w_ff1, v_w_ff2, v_norm_mlp_post):
    args = dict(locals())
    wts = {n: args[n] for n in ORDER}
    mom = {n: args["m_" + n] for n in ORDER}
    var = {n: args["v_" + n] for n in ORDER}
    T = x.shape[1]
    place = jnp.stack([2 * lax.axis_index("x") + lax.axis_index("y"), lax.axis_index("c")]).astype(jnp.int32)

    w_early = _pack_early_shards(wts)
    w_late = _pack_late_shards(wts)
    cw_own = jnp.pad(conv_w.reshape(CONV_W, -1), ((0, CONV_TAPS_PAD - CONV_W), (0, 0)))
    full, cw_all = _gather_weights(w_early, cw_own)
    full = lax.dynamic_update_slice(full, w_early[None], (place[0], 0, 0))
    cw_all = lax.dynamic_update_slice(cw_all, cw_own[None], (place[0], 0, 0))
    W = _unpack_early(full)
    cw = cw_all.transpose(1, 0, 2).reshape(CONV_TAPS_PAD, D_MODEL)
    sems, w_thru, land, token = _late_gather_start(w_late, full)
    vec = {n: wts[n] for n in SMALL_VECS}
    vec["norm_mix_pre"] = vec["norm_mix_pre"] + token[:1, :1]

    def late_weights(after):
        landed = _late_gather_wait(sems, w_thru, land, after)
        return _LateWeights(lax.dynamic_update_slice(landed, w_late[None], (place[0], 0, 0)))

    sent = {}

    def send_late_grads(bufs):
        sent["late"] = (bufs[0],) + _reduce_start(bufs[1], place, name="late_grads_start")
        return sent["late"][4]

    def send_early_grads(g):
        g32 = _pack_early_grads(g)
        sent["early"] = (g32,) + _reduce_start(g32.astype(BF16), place, name="early_grads_start")
        return sent["early"][4]

    late_bufs = (lax.empty((N_CHIPS, LATE_ROWS, 1024), F32), lax.empty((N_CHIPS, LATE_ROWS, 1024), BF16))
    grad_x, gs, g_conv_w, loss = _local_step(x[0], positions.reshape(T, 1), loss_target[0], W, late_weights,
                                             late_bufs, send_late_grads, send_early_grads, cw, vec)

    def finish(group, after):
        g32, sems_, g_thru, land_, _ = sent[group]
        landed = _reduce_wait(sems_, g_thru, land_, after, name=group + "_grads_wait")
        red = _reduce_sum(place, g32, landed, name=group + "_grads_sum")
        return _join_halves(red, name=group + "_grads_join")

    def as2d(a):
        return a.reshape(-1, a.shape[-1]) if a.ndim != 4 or a.shape[2] == 1 else a.reshape(a.shape[1], -1)

    grads, delta, new_m, new_v = {}, {}, {}, {}

    def update(n):
        shp = wts[n].shape
        outs = _adamw(as2d(wts[n]), as2d(grads[n]), as2d(mom[n]), as2d(var[n]), name="adamw_" + n)
        delta[n], new_m[n], new_v[n] = (o.reshape(shp) for o in outs)

    g_late = finish("late", [sent["early"][4]])
    for n, off, rows in LATE:
        shp = wts[n].shape
        outs = _adamw(as2d(wts[n]), g_late, as2d(mom[n]), as2d(var[n]), g_row=off, name="adamw_" + n)
        delta[n], new_m[n], new_v[n], grads[n] = (o.reshape(shp) for o in outs)

    svec = jnp.concatenate(
        [_row1024(gs[n]) for n in SMALL_VECS] + [_row1024(loss)]
        + [jnp.zeros((SMALL_CONVW_ROW - SMALL_LOSS_ROW - 1, 1024), F32), g_conv_w], axis=0)
    small_sent = _reduce_start(svec, place, name="small_grads_start", whole=True)

    g_early = finish("early", [grad_x, delta["w_ff2"]])
    outs = _adamw(wts["w_in"][0].T, g_early[:1200], mom["w_in"][0].T, var["w_in"][0].T, name="adamw_w_in")
    delta["w_in"], new_m["w_in"], new_v["w_in"] = (o.T[None] for o in outs)
    grads["w_in"] = g_early[:1200].T[None]
    off = 1200
    for n, rows in EARLY[1:]:
        grads[n] = g_early[off:off + rows].reshape(wts[n].shape)
        off += rows
        update(n)

    landed = _reduce_wait(small_sent[0], small_sent[1], small_sent[2], [delta["w_in"]],
                          name="small_grads_wait", whole=True)
    ssum = _sum_small((2 * place[0] + place[1])[None], svec, landed)
    for r, n in enumerate(SMALL_VECS):
        grads[n] = ssum[r:r + 1, :wts[n].shape[1]]
    cw_sum = ssum[SMALL_CONVW_ROW:SMALL_CONVW_ROW + CONV_W]
    grads["conv_w"] = lax.dynamic_slice_in_dim(cw_sum, place[0] * 256, 256, axis=1).reshape(conv_w.shape)
    loss_out = ssum[SMALL_LOSS_ROW, 0]
    update("conv_w")
    small = [jnp.concatenate([_row1024(t[n]) for n in SMALL_VECS], axis=0) for t in (wts, grads, mom, var)]
    outs = _adamw(*small, name="adamw_small")
    for r, n in enumerate(SMALL_VECS):
        delta[n], new_m[n], new_v[n] = (o[r:r + 1, :wts[n].shape[1]] for o in outs)

    return (loss_out, grad_x[None], *[grads[n] for n in ORDER], *[delta[n] for n in ORDER],
            *[new_m[n] for n in ORDER], *[new_v[n] for n in ORDER])
```

```python
import functools

import numpy as np
import jax
import jax.numpy as jnp
from jax import lax
from jax.experimental import pallas as pl
from jax.experimental.pallas import tpu as pltpu

F32 = jnp.float32
BF16 = jnp.bfloat16
MESH = pl.DeviceIdType.MESH

D_MODEL = 1024
N_HEADS = 8
NOPE = 128
ROPE = 64
QK_PAD = 256
Q_RANK = 384
KV_RANK = 256
CONV_W = 31
CONV_TAPS_PAD = 32
HALO = 32
D_FF = 4096
EPS = 1e-6
ROPE_THETA = 10000.0
SCALE = float((NOPE + ROPE) ** -0.5)
Q_SCALE = SCALE * float(np.log2(np.e))
K_UNSCALE = float(np.log(2.0))
Z_A, Z_B, Z_GA, Z_GC, Z_S = range(5)
Z_COLS = 5 * D_MODEL
N_SMALL = Q_RANK + KV_RANK + ROPE
N_CHIPS = 4
N_DEV = 8

ADAM_LR = 0.001
ADAM_B1 = 0.9
ADAM_B2 = 0.999
ADAM_EPS = 1e-08
ADAM_WD = 0.01
ADAM_STEP = 10

VMEM_LIMIT = 48 * 1024 * 1024
NEG = -1e30

EARLY = (("w_in", 1200), ("w_uq", 144), ("w_uk", 64), ("w_uv", 64))
EARLY_ROWS = 1536
LATE = (("w_ff1", 0, 1024), ("w_o_attn", 1024, 256), ("w_pw2", 1536, 256), ("w_out", 1792, 256),
        ("w_ff2", 2048, 1024))
LATE_OFF = {n: off for n, off, _ in LATE}
LATE_ROWS = 3072
SUM_TILE = 256

SMALL_VECS = ("norm_mix_pre", "conv_b", "conv_ln_g", "conv_ln_b", "b_pw2", "norm_mix_post",
              "norm_mlp_pre", "norm_mlp_post", "q_norm", "kv_norm")
SMALL_LOSS_ROW = 10
SMALL_CONVW_ROW = 16
SMALL_ROWS = 48


ANY = pl.BlockSpec(memory_space=pl.ANY)


def _params(*sem):
    return pltpu.CompilerParams(dimension_semantics=sem, vmem_limit_bytes=VMEM_LIMIT)


def _tile(n, cap):
    if n <= cap:
        return n
    t = (cap // 128) * 128
    while n % t:
        t -= 128
    return t


def _row_tile(n, cap):
    if n <= cap:
        return n
    t = (cap // 8) * 8
    while n % t:
        t -= 8
    return t


MM_VMEM_BUDGET = 40 * 1024 * 1024


def _mm_tiles(M, N, K, in_bytes, out_bytes):
    tn = _tile(N, 1024)
    tk = K
    while True:
        for cap in (1024, 512, 256):
            tm = _tile(M, cap)
            need = 2 * tk * (tm * in_bytes[0] + tn * in_bytes[1]) + 2 * tm * tn * out_bytes
            need += tm * tn * 4 if tk < K else 0
            if need <= MM_VMEM_BUDGET:
                return tm, tn, tk
        tk = _tile(K, tk - 128)


def _mm(a, b, *, name, ta=False, tb=False, out_dtype=F32, extras=(), epilogue=None, out_dtypes=None,
        after=(), shape=None, tiles=None, b_spec=None, reds=(), k_slabs=False):
    if shape is None:
        M, K = (a.shape[1], a.shape[0]) if ta else a.shape
        N = b.shape[0] if tb else b.shape[1]
        assert K == (b.shape[1] if tb else b.shape[0]), (a.shape, b.shape, ta, tb)
    else:
        M, N, K = shape
    outs = tuple(out_dtypes or (out_dtype,))
    placed = [o for o in outs if isinstance(o, tuple)]
    ex_arrays = [e[0] if isinstance(e, tuple) else e for e in extras]
    if tiles is None:
        out_bytes = sum(jnp.dtype(d).itemsize for d in outs) + sum(e.dtype.itemsize for e in ex_arrays)
        tiles = _mm_tiles(M, N, K, (a.dtype.itemsize, b.dtype.itemsize), out_bytes)
    tm, tn, tk = tiles
    nk = K // tk
    dims = (((0 if ta else 1,), (1 if tb else 0,)), ((), ()))
    n_ex, n_out = len(extras), len(outs)
    n_pass = len(after) + len(placed)
    assert not reds or tn == N

    def body(*refs):
        a_ref, b_ref = refs[:2]
        ex_refs = refs[2:2 + n_ex]
        o_refs = refs[2 + n_ex + n_pass:2 + n_ex + n_pass + n_out]
        red_refs = refs[2 + n_ex + n_pass + n_out:2 + n_ex + n_pass + n_out + len(reds)]
        first_rows, k = pl.program_id(0) == 0, pl.program_id(2)
        bv = b_ref[...].astype(BF16)
        if k_slabs:
            kc = bv.shape[-1]
            part = sum(lax.dot_general(a_ref[:, p * kc:(p + 1) * kc].astype(BF16), bv[p], dims,
                                       preferred_element_type=F32) for p in range(bv.shape[0]))
        else:
            if bv.ndim == 3:
                bv = bv.reshape(-1, bv.shape[-1])
            part = lax.dot_general(a_ref[...].astype(BF16), bv, dims, preferred_element_type=F32)

        def finish(acc):
            ex = [r[...] for r in ex_refs]
            ex = [e.astype(F32) if e.dtype == BF16 else e for e in ex]
            res = epilogue(acc, *ex) if epilogue else (acc,) * n_out
            for o_ref, r in zip(o_refs, res):
                o_ref[...] = r.astype(o_ref.dtype)
            for red_ref, r in zip(red_refs, res[n_out:]):
                red_ref[...] += r

        if reds:
            @pl.when(first_rows & (k == 0))
            def _():
                for red_ref in red_refs:
                    red_ref[...] = jnp.zeros_like(red_ref)

        if nk == 1:
            finish(part)
        else:
            acc_ref = refs[-1]

            @pl.when(k == 0)
            def _():
                acc_ref[...] = part

            @pl.when((k > 0) & (k < nk - 1))
            def _():
                acc_ref[...] += part

            @pl.when(k == nk - 1)
            def _():
                finish(acc_ref[...] + part)

    a_spec = (pl.BlockSpec((tk, tm), lambda i, j, k: (k, i)) if ta
              else pl.BlockSpec((tm, tk), lambda i, j, k: (i, k)))
    if b_spec is None:
        b_spec = (pl.BlockSpec((tn, tk), lambda i, j, k: (j, k)) if tb
                  else pl.BlockSpec((tk, tn), lambda i, j, k: (k, j)))
    o_spec = pl.BlockSpec((tm, tn), lambda i, j, k: (i, j))
    row_spec = pl.BlockSpec((1, tn), lambda i, j, k: (0, j))
    ex_specs = []
    for e in extras:
        if isinstance(e, tuple):
            ex_specs.append(pl.BlockSpec((tm, tn), functools.partial(lambda i, j, k, cb: (i, cb), cb=e[1])))
        else:
            ex_specs.append(row_spec if e.shape[0] == 1 else o_spec)
    out_shape, out_specs, aliases = [], [], {}
    for o, out in enumerate(outs):
        if isinstance(out, tuple):
            out_shape.append(jax.ShapeDtypeStruct(out[0].shape, out[0].dtype))
            out_specs.append(out[1])
            aliases[2 + n_ex + len(after) + len(aliases)] = o
        else:
            out_shape.append(jax.ShapeDtypeStruct((M, N), out))
            out_specs.append(o_spec)
    passed = list(after) + [buf for buf, _ in placed]
    out_shape = out_shape + [jax.ShapeDtypeStruct((1, w), F32) for w in reds]
    out_specs = out_specs + [pl.BlockSpec((1, w), lambda i, j, k: (0, 0)) for w in reds]
    res = pl.pallas_call(
        body, name=name, out_shape=out_shape,
        grid=(M // tm, N // tn, nk),
        in_specs=[a_spec, b_spec] + ex_specs + [ANY] * len(passed),
        out_specs=out_specs, input_output_aliases=aliases,
        scratch_shapes=[pltpu.VMEM((tm, tn), F32)] if nk > 1 else [],
        compiler_params=_params(*(("arbitrary",) * 3 if reds else ("parallel", "parallel", "arbitrary"))),
    )(a, b, *ex_arrays, *passed)
    return res[0] if len(res) == 1 else res


class _LateWeights:
    def __init__(self, pack):
        self.pack = pack

    def fwd(self, x, n, *, name, tm=1024, **kw):
        T, off = x.shape[0], LATE_OFF[n]
        if n == "w_ff1":
            return _mm(x, self.pack, name=name, shape=(T, 4096, 1024), tiles=(_tile(T, tm), 1024, 1024),
                       b_spec=pl.BlockSpec((None, 1024, 1024), lambda i, j, k: (j, 0, 0)), **kw)
        if n == "w_ff2":
            view = self.pack.reshape(N_CHIPS, LATE_ROWS // 1024, 1024, 1024)
            return _mm(x, view, name=name, shape=(T, 1024, 4096), tiles=(_tile(T, min(tm, 512)), 1024, 4096),
                       b_spec=pl.BlockSpec((N_CHIPS, None, 1024, 1024), lambda i, j, k: (0, off // 1024, 0, 0)),
                       **kw)
        return _mm(x, self.pack.reshape(N_CHIPS, LATE_ROWS // 256, 256, 1024), name=name,
                   shape=(T, 1024, 1024), tiles=(_tile(T, tm), 1024, 1024), b_spec=self._whole(off), **kw)

    @staticmethod
    def _whole(off):
        return pl.BlockSpec((N_CHIPS, None, 256, 1024), lambda i, j, k: (0, off // 256, 0, 0))

    def bwd(self, dy, n, *, name, tm=1024, **kw):
        T, off = dy.shape[0], LATE_OFF[n]
        tm = _tile(T, tm)
        if n == "w_ff1":
            view = self.pack.reshape(N_CHIPS, LATE_ROWS // 1024, 1024, 1024)
            return _mm(dy, view, tb=True, name=name, shape=(T, 1024, 4096), tiles=(tm, 1024, 4096),
                       b_spec=pl.BlockSpec((N_CHIPS, None, 1024, 1024), lambda i, j, k: (0, off // 1024, 0, 0),
                                           pipeline_mode=pl.Buffered(1)), k_slabs=True, **kw)
        if n == "w_ff2":
            return _mm(dy, self.pack, tb=True, name=name, shape=(T, 4096, 1024), tiles=(tm, 1024, 1024),
                       b_spec=pl.BlockSpec((None, 1024, 1024), lambda i, j, k: (j, off // 1024, 0)), **kw)
        return _mm(dy, self.pack.reshape(N_CHIPS, LATE_ROWS // 256, 256, 1024), tb=True, name=name,
                   shape=(T, 1024, 1024), tiles=(tm, 1024, 1024), b_spec=self._whole(off), **kw)


def _late_grad(x, dy, n, bufs, *, name):
    T, off = x.shape[0], LATE_OFF[n]
    if n == "w_ff1":
        spec = pl.BlockSpec((None, 512, 1024), lambda i, j, k: (j, i, 0))
        shape, tiles = (1024, 4096, T), (512, 1024, T)
    elif n == "w_ff2":
        spec = pl.BlockSpec((None, 512, 1024), lambda i, j, k: (i // 2, off // 512 + i % 2, 0))
        shape, tiles = (4096, 1024, T), (512, 1024, T)
    else:
        spec = pl.BlockSpec((None, 256, 1024), lambda i, j, k: (i, off // 256, 0))
        shape, tiles = (1024, 1024, T), (256, 1024, T)
    return _mm(x, dy, ta=True, name=name, shape=shape, tiles=tiles, out_dtypes=[(b, spec) for b in bufs])


def _rowwise(fn, rows, consts, outs, reds=(), *, name, tm=256, after=()):
    T = rows[0][0].shape[0]
    tm = min(tm, T)
    n_row, n_const, n_out = len(rows), len(consts), len(outs)
    placed = [o for o in outs if len(o) == 3]
    n_pass = len(after) + len(placed)

    def body(*refs):
        i = pl.program_id(0)
        vals = [r[...] for r in refs[:n_row + n_const]]
        vals = [v.astype(F32) if v.dtype == BF16 else v for v in vals]
        res = fn(*vals)
        out_refs = refs[n_row + n_const + n_pass:]
        for k in range(n_out):
            out_refs[k][...] = res[k].astype(out_refs[k].dtype)
        for k in range(len(reds)):
            ref = out_refs[n_out + k]

            @pl.when(i == 0)
            def _(ref=ref):
                ref[...] = jnp.zeros_like(ref)

            ref[...] += res[n_out + k]

    in_specs = [pl.BlockSpec((tm, w), functools.partial(lambda i, cb: (i, cb), cb=cb))
                for (_, w, cb) in rows]
    in_specs += [pl.BlockSpec(c.shape, lambda i: (0, 0)) for c in consts]
    in_specs += [ANY] * n_pass
    out_specs, out_shape, aliases = [], [], {}
    for o, out in enumerate(outs):
        if len(out) == 3:
            w, buf, cb = out
            out_specs.append(pl.BlockSpec((tm, w), functools.partial(lambda i, cb: (i, cb), cb=cb)))
            out_shape.append(jax.ShapeDtypeStruct(buf.shape, buf.dtype))
            aliases[n_row + n_const + len(after) + len(aliases)] = o
        else:
            out_specs.append(pl.BlockSpec((tm, out[0]), lambda i: (i, 0)))
            out_shape.append(jax.ShapeDtypeStruct((T, out[0]), out[1]))
    out_specs += [pl.BlockSpec(s, lambda i: (0, 0)) for s in reds]
    out_shape += [jax.ShapeDtypeStruct(s, F32) for s in reds]
    return pl.pallas_call(
        body, name=name, out_shape=out_shape, grid=(T // tm,),
        in_specs=in_specs, out_specs=out_specs, input_output_aliases=aliases,
        compiler_params=_params("arbitrary"),
    )(*[r[0] for r in rows], *consts, *after, *[o[1] for o in placed])


def _rms(x, g):
    r = lax.rsqrt(jnp.mean(x * x, axis=-1, keepdims=True) + EPS)
    return x * r * g


def _rms_bwd(x, g, dy):
    r = lax.rsqrt(jnp.mean(x * x, axis=-1, keepdims=True) + EPS)
    n = x * r
    dyg = dy * g
    dx = r * (dyg - n * jnp.mean(dyg * n, axis=-1, keepdims=True))
    return dx, jnp.sum(dy * n, axis=0, keepdims=True)


def _sigmoid(x):
    return 1.0 / (1.0 + jnp.exp(-x))


def _rope(x, cs, sa, sb):
    return x * cs + pltpu.roll(x, 32, 1) * sa + pltpu.roll(x, 96, 1) * sb


def _rope_t(dy, cs, sa, sb):
    return dy * cs + pltpu.roll(dy * sa, 96, 1) + pltpu.roll(dy * sb, 32, 1)


def _causal_mask(t):
    row = lax.broadcasted_iota(jnp.int32, (t, t), 0)
    col = lax.broadcasted_iota(jnp.int32, (t, t), 1)
    return col <= row


FLASH_FWD_TQ = 4096
FLASH_FWD_TK = 1024
FLASH_FWD_HEADS = 1
FLASH_BWD_TQ = 1024
FLASH_BWD_TK = 512
NT_DIMS = (((1,), (1,)), ((), ()))
TN_DIMS = (((0,), (0,)), ((), ()))


def _flash_fwd(q, k, v, *, tq=FLASH_FWD_TQ, tk=FLASH_FWD_TK, hp=FLASH_FWD_HEADS):
    T = q.shape[0]
    tq, tk = min(tq, T), min(tk, T)
    n_diag = tq // tk
    assert n_diag >= 1 and N_HEADS % hp == 0
    heads = range(hp)

    def body(q_ref, k_ref, v_ref, o_ref, lse_ref):
        i = pl.program_id(1)

        def cols(ref, rows, h, width=QK_PAD):
            return ref[rows, h * width:(h + 1) * width]

        def logits(j, r0=0):
            sl = pl.ds(pl.multiple_of(j * tk, tk), tk)
            return tuple(lax.dot_general(cols(q_ref, slice(r0, tq), h), cols(k_ref, sl, h), NT_DIMS,
                                         preferred_element_type=F32) for h in heads)

        def update(j, s, m, acc, masked, r0=0):
            sl = pl.ds(pl.multiple_of(j * tk, tk), tk)
            m_out, acc_out = [], []
            for h in heads:
                sh = s[h]
                if masked:
                    row = i * tq + r0 + lax.broadcasted_iota(jnp.int32, (tq - r0, tk), 0)
                    col = j * tk + lax.broadcasted_iota(jnp.int32, (tq - r0, tk), 1)
                    sh = jnp.where(col <= row, sh, NEG)
                m_old, acc_old = m[h][r0:], acc[h][r0:]
                m_new = jnp.maximum(m_old, jnp.max(sh, axis=-1, keepdims=True))
                p = jnp.exp2(sh - m_new)
                acc_new = (jnp.exp2(m_old - m_new) * acc_old
                           + jnp.dot(p.astype(BF16), cols(v_ref, sl, h), preferred_element_type=F32))
                if r0:
                    m_new = jnp.concatenate([m[h][:r0], m_new], axis=0)
                    acc_new = jnp.concatenate([acc[h][:r0], acc_new], axis=0)
                m_out.append(m_new)
                acc_out.append(acc_new)
            return tuple(m_out), tuple(acc_out)

        def step(j, carry):
            s, m, acc = carry
            s_next = logits(j + 1)
            m, acc = update(j, s, m, acc, False)
            return s_next, m, acc

        first = i * n_diag
        init = (logits(0), tuple(jnp.full((tq, 1), NEG, F32) for _ in heads),
                tuple(jnp.zeros((tq, QK_PAD), F32) for _ in heads))
        s, m, acc = lax.fori_loop(0, first, step, init)
        for d in range(n_diag):
            m, acc = update(first + d, s, m, acc, True, r0=d * tk)
            if d + 1 < n_diag:
                s = logits(first + d + 1, r0=(d + 1) * tk)
        for h in heads:
            l = acc[h][:, NOPE:]
            o_ref[:, h * NOPE:(h + 1) * NOPE] = (acc[h][:, :NOPE] / l).astype(BF16)
            lse_ref[:, h * NOPE:(h + 1) * NOPE] = m[h] + jnp.log2(l)

    return pl.pallas_call(
        body, name="flash_fwd",
        out_shape=[jax.ShapeDtypeStruct((T, N_HEADS * NOPE), BF16),
                   jax.ShapeDtypeStruct((T, N_HEADS * NOPE), F32)],
        grid=(N_HEADS // hp, T // tq),
        in_specs=[pl.BlockSpec((tq, hp * QK_PAD), lambda h, i: (i, h)),
                  pl.BlockSpec((T, hp * QK_PAD), lambda h, i: (0, h)),
                  pl.BlockSpec((T, hp * QK_PAD), lambda h, i: (0, h))],
        out_specs=[pl.BlockSpec((tq, hp * NOPE), lambda h, i: (i, h)),
                   pl.BlockSpec((tq, hp * NOPE), lambda h, i: (i, h))],
        compiler_params=_params("parallel", "arbitrary"),
    )(q, k, v)


def _flash_bwd(q, k, v, do, lse, *, tq=FLASH_BWD_TQ, tk=FLASH_BWD_TK):
    T = q.shape[0]
    tq, tk = min(tq, T), min(tk, T)

    def body(q_ref, k_ref, v_ref, do_ref, lse_ref, dq_ref, dk_ref, dv_ref, dq_acc):
        dq_acc[...] = jnp.zeros_like(dq_acc)
        for j in range(T // tk):
            k0 = j * tk
            kj = k_ref[k0:k0 + tk, :]
            vj = v_ref[k0:k0 + tk, :]
            dk = jnp.zeros((tk, QK_PAD), F32)
            dv = jnp.zeros((tk, NOPE), F32)
            for i in range(k0 // tq, T // tq):
                r0 = max(i * tq, k0)
                r1 = (i + 1) * tq
                qi = q_ref[r0:r1, :]
                doi = do_ref[r0:r1, :]
                s = lax.dot_general(qi, kj, NT_DIMS, preferred_element_type=F32)
                p = jnp.exp2(s - lse_ref[r0:r1, :][:, :1])
                if r0 < k0 + tk:
                    row = r0 + lax.broadcasted_iota(jnp.int32, (r1 - r0, tk), 0)
                    col = k0 + lax.broadcasted_iota(jnp.int32, (r1 - r0, tk), 1)
                    p = jnp.where(col <= row, p, 0.0)
                dv = dv + lax.dot_general(p.astype(BF16), doi[:, :NOPE], TN_DIMS, preferred_element_type=F32)
                ds = (p * lax.dot_general(doi, vj, NT_DIMS, preferred_element_type=F32)).astype(BF16)
                dk = dk + lax.dot_general(ds, qi, TN_DIMS, preferred_element_type=F32)
                dq_acc[r0:r1, :] += jnp.dot(ds, kj, preferred_element_type=F32)
            dk_ref[k0:k0 + tk, :] = dk.astype(BF16)
            dv_ref[k0:k0 + tk, :] = dv.astype(BF16)
        dq_ref[...] = dq_acc[...].astype(BF16)

    return pl.pallas_call(
        body, name="flash_bwd",
        out_shape=[jax.ShapeDtypeStruct((T, N_HEADS * QK_PAD), BF16),
                   jax.ShapeDtypeStruct((T, N_HEADS * QK_PAD), BF16),
                   jax.ShapeDtypeStruct((T, N_HEADS * NOPE), BF16)],
        grid=(N_HEADS,),
        in_specs=[pl.BlockSpec((T, QK_PAD), lambda h: (0, h)),
                  pl.BlockSpec((T, QK_PAD), lambda h: (0, h)),
                  pl.BlockSpec((T, QK_PAD), lambda h: (0, h)),
                  pl.BlockSpec((T, QK_PAD), lambda h: (0, h)),
                  pl.BlockSpec((T, NOPE), lambda h: (0, h))],
        out_specs=[pl.BlockSpec((T, QK_PAD), lambda h: (0, h)),
                   pl.BlockSpec((T, QK_PAD), lambda h: (0, h)),
                   pl.BlockSpec((T, NOPE), lambda h: (0, h))],
        scratch_shapes=[pltpu.VMEM((T, QK_PAD), F32)],
        compiler_params=_params("arbitrary"),
    )(q, k, v, do, lse)


SUB = 128


def _windows(buf, t0, c0, first, last):
    span = SUB + 8 * ((last + 7) // 8)
    base = buf[t0:t0 + span, c0:c0 + SUB]
    for r in range(8):
        offs = [o for o in range(first, last + 1) if o % 8 == r]
        if offs:
            win = base if r == 0 else pltpu.roll(base, span - r, 0)
            for o in offs:
                yield o, win[o - r:o - r + SUB, :]


def _glu(a, b):
    return a.astype(F32) * _sigmoid(b.astype(F32))


def _conv_fwd(z, w, bias, *, tm=256):
    T, C = z.shape[0], D_MODEL
    tm = min(tm, T)
    hb = tm // HALO
    lead = HALO - (CONV_W - 1)

    def body(a_ref, b_ref, ah_ref, bh_ref, w_ref, bias_ref, o_ref, buf):
        t = pl.program_id(0)
        buf[0:HALO, :] = jnp.where(t > 0, _glu(ah_ref[...], bh_ref[...]), 0.0)
        buf[HALO:, :] = _glu(a_ref[...], b_ref[...])
        for c0 in range(0, C, SUB):
            for t0 in range(0, tm, SUB):
                acc = jnp.broadcast_to(bias_ref[:, c0:c0 + SUB], (SUB, SUB))
                for o, win in _windows(buf, t0, c0, lead, lead + CONV_W - 1):
                    acc = acc + win * w_ref[o - lead:o - lead + 1, c0:c0 + SUB]
                o_ref[t0:t0 + SUB, c0:c0 + SUB] = acc

    def halo(cb):
        return pl.BlockSpec((HALO, C), lambda t: (jnp.maximum(t * hb - 1, 0), cb))

    return pl.pallas_call(
        body, name="conv_fwd",
        out_shape=jax.ShapeDtypeStruct((T, C), F32),
        grid=(T // tm,),
        in_specs=[pl.BlockSpec((tm, C), lambda t: (t, Z_A)), pl.BlockSpec((tm, C), lambda t: (t, Z_B)),
                  halo(Z_A), halo(Z_B),
                  pl.BlockSpec((CONV_TAPS_PAD, C), lambda t: (0, 0)), pl.BlockSpec((1, C), lambda t: (0, 0))],
        out_specs=pl.BlockSpec((tm, C), lambda t: (t, 0)),
        scratch_shapes=[pltpu.VMEM((HALO + tm, C), F32)],
        compiler_params=_params("parallel"),
    )(z, z, z, z, w, bias)


def _conv_bwd(z, du1, w, dz, *, tm=256):
    T, C = z.shape[0], D_MODEL
    tm = min(tm, T)
    hb = tm // HALO
    nt = T // tm
    last_halo = T // HALO - 1
    lead = HALO - (CONV_W - 1)
    assert (Z_A, Z_B) == (0, 1)

    def body(a_ref, b_ref, ah_ref, bh_ref, d_ref, dh_ref, w_ref, dz_in, dz_ref, dw_ref, ubuf, dbuf, dw8, gate):
        t = pl.program_id(0)
        ubuf[0:HALO, :] = jnp.where(t > 0, _glu(ah_ref[...], bh_ref[...]), 0.0)
        gate[...] = _sigmoid(b_ref[...].astype(F32))
        ubuf[HALO:, :] = a_ref[...].astype(F32) * gate[...]
        dbuf[0:tm, :] = d_ref[...]
        dbuf[tm:, :] = jnp.where(t < nt - 1, dh_ref[...], 0.0)

        @pl.when(t == 0)
        def _():
            dw8[...] = jnp.zeros_like(dw8)

        for c0 in range(0, C, SUB):
            for t0 in range(0, tm, SUB):
                acc = jnp.zeros((SUB, SUB), F32)
                for o, win in _windows(dbuf, t0, c0, 0, CONV_W - 1):
                    kk = CONV_W - 1 - o
                    acc = acc + win * w_ref[kk:kk + 1, c0:c0 + SUB]
                av = a_ref[t0:t0 + SUB, c0:c0 + SUB].astype(F32)
                sg = gate[t0:t0 + SUB, c0:c0 + SUB]
                dz_ref[t0:t0 + SUB, c0:c0 + SUB] = (acc * sg).astype(dz_ref.dtype)
                dz_ref[t0:t0 + SUB, C + c0:C + c0 + SUB] = (acc * av * sg * (1.0 - sg)).astype(dz_ref.dtype)
                dsub = d_ref[t0:t0 + SUB, c0:c0 + SUB]
                for o, win in _windows(ubuf, t0, c0, lead, lead + CONV_W - 1):
                    kk = o - lead
                    dw8[8 * kk:8 * kk + 8, c0:c0 + SUB] += jnp.sum(
                        (dsub * win).reshape(SUB // 8, 8, SUB), axis=0)

        @pl.when(t == nt - 1)
        def _():
            dw_ref[...] = jnp.sum(dw8[...].reshape(CONV_TAPS_PAD, 8, C), axis=1)

    def halo(cb):
        return pl.BlockSpec((HALO, C), lambda t: (jnp.maximum(t * hb - 1, 0), cb))

    return pl.pallas_call(
        body, name="conv_bwd",
        out_shape=[jax.ShapeDtypeStruct(dz.shape, dz.dtype), jax.ShapeDtypeStruct((CONV_TAPS_PAD, C), F32)],
        grid=(nt,),
        in_specs=[pl.BlockSpec((tm, C), lambda t: (t, Z_A)), pl.BlockSpec((tm, C), lambda t: (t, Z_B)),
                  halo(Z_A), halo(Z_B),
                  pl.BlockSpec((tm, C), lambda t: (t, 0)),
                  pl.BlockSpec((HALO, C), lambda t: (jnp.minimum((t + 1) * hb, last_halo), 0)),
                  pl.BlockSpec((CONV_TAPS_PAD, C), lambda t: (0, 0)), ANY],
        out_specs=[pl.BlockSpec((tm, 2 * C), lambda t: (t, 0)),
                   pl.BlockSpec((CONV_TAPS_PAD, C), lambda t: (0, 0))],
        input_output_aliases={7: 0},
        scratch_shapes=[pltpu.VMEM((HALO + tm, C), F32), pltpu.VMEM((tm + HALO, C), F32),
                        pltpu.VMEM((CONV_TAPS_PAD * 8, C), F32), pltpu.VMEM((tm, C), F32)],
        compiler_params=_params("arbitrary"),
    )(z, z, z, z, du1, du1, w, dz)


def _adamw(w, g, m, v, *, name, g_row=None):
    R, C = w.shape
    tr = _row_tile(R, max(8, (1 << 18) // C // 8 * 8))
    first = 0 if g_row is None else g_row // tr
    assert g_row is None or g_row % tr == 0

    def body(w_ref, g_ref, m_ref, v_ref, d_ref, nm_ref, nv_ref, *g_out):
        gv = g_ref[...]
        nm = ADAM_B1 * m_ref[...] + (1.0 - ADAM_B1) * gv
        nv = ADAM_B2 * v_ref[...] + (1.0 - ADAM_B2) * (gv * gv)
        m_hat = nm / (1.0 - ADAM_B1 ** ADAM_STEP)
        v_hat = nv / (1.0 - ADAM_B2 ** ADAM_STEP)
        d_ref[...] = -ADAM_LR * (m_hat / (jnp.sqrt(v_hat) + ADAM_EPS) + ADAM_WD * w_ref[...])
        nm_ref[...] = nm
        nv_ref[...] = nv
        for ref in g_out:
            ref[...] = gv

    spec = pl.BlockSpec((tr, C), lambda i: (i, 0))
    g_spec = pl.BlockSpec((tr, C), lambda i: (first + i, 0))
    n_out = 3 if g_row is None else 4
    return pl.pallas_call(
        body, name=name, out_shape=[jax.ShapeDtypeStruct((R, C), F32)] * n_out,
        grid=(R // tr,), in_specs=[spec, g_spec, spec, spec], out_specs=[spec] * n_out,
        compiler_params=_params("parallel"),
    )(w, g, m, v)


def _coords():
    return lax.axis_index("x"), lax.axis_index("y"), lax.axis_index("c")


def _remote(src, dst, ssem, rsem, to):
    return pltpu.make_async_remote_copy(src_ref=src, dst_ref=dst, send_sem=ssem, recv_sem=rsem,
                                        device_id=to, device_id_type=MESH)


def _half(c, rows):
    return pl.ds(pl.multiple_of(c * (rows // 2), 16), rows // 2)


def _gather_weights(wpack, cw):
    def body(w_ref, cw_ref, full_ref, cwf_ref, ssem, rsem):
        x, y, c = _coords()
        q = 2 * x + y
        sib = (x, y, 1 - c)
        chips = [(1 - x, y), (x, 1 - y), (1 - x, 1 - y)]
        mine, other = _half(c, w_ref.shape[0]), _half(1 - c, w_ref.shape[0])
        first = []
        for j, (px, py) in enumerate(chips):
            first.append(_remote(w_ref.at[mine], full_ref.at[q, mine], ssem.at[j], rsem.at[j], (px, py, c)))
            first.append(_remote(cw_ref, cwf_ref.at[q], ssem.at[6 + j], rsem.at[6 + j], (px, py, c)))
        for cp in first:
            cp.start()
        passed = []
        for j, (px, py) in enumerate(chips):
            pq = 2 * px + py
            _remote(w_ref.at[mine], full_ref.at[pq, mine], ssem.at[j], rsem.at[j], (px, py, c)).wait_recv()
            fw = _remote(full_ref.at[pq, mine], full_ref.at[pq, mine], ssem.at[3 + j], rsem.at[3 + j], sib)
            fw.start()
            passed.append(fw)
        for j, (px, py) in enumerate(chips):
            pq = 2 * px + py
            _remote(full_ref.at[pq, other], full_ref.at[pq, other], ssem.at[3 + j], rsem.at[3 + j], sib).wait_recv()
            _remote(cw_ref, cwf_ref.at[pq], ssem.at[6 + j], rsem.at[6 + j], (px, py, c)).wait_recv()
        for cp in first + passed:
            cp.wait_send()

    return pl.pallas_call(
        body, name="gather_weights",
        out_shape=[jax.ShapeDtypeStruct((N_CHIPS,) + wpack.shape, wpack.dtype),
                   jax.ShapeDtypeStruct((N_CHIPS,) + cw.shape, cw.dtype)],
        in_specs=[ANY, ANY], out_specs=[ANY, ANY],
        scratch_shapes=[pltpu.SemaphoreType.DMA((9,)), pltpu.SemaphoreType.DMA((9,))],
    )(wpack, cw)


SEM = pl.BlockSpec(memory_space=pltpu.SEMAPHORE)
HBM = pl.BlockSpec(memory_space=pltpu.HBM)
N_LATE = 6


def _late_peers(x, y, c):
    out = []
    for j, (px, py) in enumerate([(1 - x, y), (x, 1 - y), (1 - x, 1 - y)]):
        for t in (0, 1):
            out.append((2 * j + t, px, py, c if t == 0 else 1 - c))
    return out


def _late_gather_start(w, after):
    rows = w.shape[0]

    def body(w_ref, land_ref, after_ref, *outs):
        sems = outs[:2 * N_LATE]
        token = outs[2 * N_LATE + 2]
        x, y, c = _coords()
        q = 2 * x + y
        mine = _half(c, rows)
        for k, px, py, pc in _late_peers(x, y, c):
            _remote(w_ref.at[mine], land_ref.at[q, mine], sems[k], sems[N_LATE + k], (px, py, pc)).start()
        token[...] = jnp.zeros_like(token)

    res = pl.pallas_call(
        body, name="late_gather_start",
        out_shape=tuple([pltpu.SemaphoreType.DMA(())] * (2 * N_LATE)) + (
            pltpu.HBM(w.shape, w.dtype), pltpu.HBM((N_CHIPS,) + w.shape, w.dtype),
            jax.ShapeDtypeStruct((8, 128), F32)),
        in_specs=(HBM, HBM, ANY),
        out_specs=tuple([SEM] * (2 * N_LATE)) + (HBM, HBM, pl.BlockSpec(memory_space=pltpu.VMEM)),
        input_output_aliases={0: 2 * N_LATE, 1: 2 * N_LATE + 1},
        compiler_params=pltpu.CompilerParams(has_side_effects=pltpu.SideEffectType.DATAFLOW_SIDE_EFFECTING),
    )(pltpu.with_memory_space_constraint(w, pltpu.HBM),
      pltpu.with_memory_space_constraint(lax.empty((N_CHIPS,) + w.shape, w.dtype), pltpu.HBM), after)
    return res[:2 * N_LATE], res[2 * N_LATE], res[2 * N_LATE + 1], res[2 * N_LATE + 2]


def _late_gather_wait(sems, w_thru, land_thru, after):
    rows = w_thru.shape[0]

    def body(w_ref, land_ref, *rest):
        sems = rest[:2 * N_LATE]
        x, y, c = _coords()
        for k, px, py, pc in _late_peers(x, y, c):
            cp = _remote(w_ref.at[_half(pc, rows)], land_ref.at[2 * px + py, _half(pc, rows)],
                         sems[k], sems[N_LATE + k], (px, py, pc))
            cp.wait_send()
            cp.wait_recv()

    return pl.pallas_call(
        body, name="late_gather_wait",
        out_shape=(pltpu.HBM(w_thru.shape, w_thru.dtype), pltpu.HBM(land_thru.shape, land_thru.dtype)),
        in_specs=(HBM, HBM) + tuple([SEM] * (2 * N_LATE)) + (ANY,),
        out_specs=(HBM, HBM), input_output_aliases={0: 0, 1: 1},
        compiler_params=pltpu.CompilerParams(has_side_effects=pltpu.SideEffectType.DATAFLOW_SIDE_EFFECTING),
    )(w_thru, land_thru, *sems, after)[1]


N_PEERS = N_DEV - 1


def _peers(x, y, c):
    out = []
    for r in range(1, N_DEV):
        out.append((r - 1, 1 - x if r & 4 else x, 1 - y if r & 2 else y, 1 - c if r & 1 else c))
    return out


def _piece(g_ref, px, py, pc, whole):
    return g_ref if whole else g_ref.at[2 * px + py, _half(pc, g_ref.shape[1])]


def _reduce_start(g16, after, *, name, whole=False):
    def body(g_ref, land_ref, after_ref, *outs):
        sems = outs[:2 * N_PEERS]
        token = outs[2 * N_PEERS + 2]
        x, y, c = _coords()
        for k, px, py, pc in _peers(x, y, c):
            _remote(_piece(g_ref, px, py, pc, whole), land_ref.at[k], sems[k], sems[N_PEERS + k],
                    (px, py, pc)).start()
        token[...] = jnp.zeros_like(token)

    land_shape = (N_PEERS,) + (g16.shape if whole else (g16.shape[1] // 2, 1024))
    res = pl.pallas_call(
        body, name=name,
        out_shape=tuple([pltpu.SemaphoreType.DMA(())] * (2 * N_PEERS)) + (
            pltpu.HBM(g16.shape, g16.dtype), pltpu.HBM(land_shape, g16.dtype),
            jax.ShapeDtypeStruct((8, 128), F32)),
        in_specs=(HBM, HBM, ANY),
        out_specs=tuple([SEM] * (2 * N_PEERS)) + (HBM, HBM, pl.BlockSpec(memory_space=pltpu.VMEM)),
        input_output_aliases={0: 2 * N_PEERS, 1: 2 * N_PEERS + 1},
        compiler_params=pltpu.CompilerParams(has_side_effects=pltpu.SideEffectType.DATAFLOW_SIDE_EFFECTING),
    )(pltpu.with_memory_space_constraint(g16, pltpu.HBM),
      pltpu.with_memory_space_constraint(lax.empty(land_shape, g16.dtype), pltpu.HBM), after)
    return res[:2 * N_PEERS], res[2 * N_PEERS], res[2 * N_PEERS + 1], res[2 * N_PEERS + 2]


def _reduce_wait(sems, g_thru, land_thru, after, *, name, whole=False):
    def body(g_ref, land_ref, *rest):
        sems = rest[:2 * N_PEERS]
        x, y, c = _coords()
        for k, px, py, pc in _peers(x, y, c):
            cp = _remote(_piece(g_ref, px, py, pc, whole), land_ref.at[k], sems[k], sems[N_PEERS + k],
                         (px, py, pc))
            cp.wait_send()
            cp.wait_recv()

    return pl.pallas_call(
        body, name=name,
        out_shape=(pltpu.HBM(g_thru.shape, g_thru.dtype), pltpu.HBM(land_thru.shape, land_thru.dtype)),
        in_specs=(HBM, HBM) + tuple([SEM] * (2 * N_PEERS)) + tuple([ANY] * len(after)),
        out_specs=(HBM, HBM), input_output_aliases={0: 0, 1: 1},
        compiler_params=pltpu.CompilerParams(has_side_effects=pltpu.SideEffectType.DATAFLOW_SIDE_EFFECTING),
    )(g_thru, land_thru, *sems, *after)[1]


def _reduce_sum(place, g32, land, *, name):
    rows = g32.shape[1]
    nb = rows // 2 // SUM_TILE

    def body(place_ref, g_ref, l_ref, o_ref):
        acc = g_ref[...]
        for k in range(N_PEERS):
            acc = acc + l_ref[k].astype(F32)
        o_ref[...] = acc

    return pl.pallas_call(
        body, name=name,
        out_shape=jax.ShapeDtypeStruct((rows, 1024), F32),
        grid_spec=pltpu.PrefetchScalarGridSpec(
            num_scalar_prefetch=1, grid=(nb,),
            in_specs=[pl.BlockSpec((None, SUM_TILE, 1024), lambda i, s: (s[0], s[1] * nb + i, 0)),
                      pl.BlockSpec((N_PEERS, SUM_TILE, 1024), lambda i, s: (0, i, 0))],
            out_specs=pl.BlockSpec((SUM_TILE, 1024), lambda i, s: (s[1] * nb + i, 0))),
        compiler_params=_params("parallel"),
    )(place, g32, land)


def _join_halves(red, *, name):
    rows = red.shape[0]

    def body(r_ref, o_ref, ssem, rsem):
        x, y, c = _coords()
        cp = _remote(r_ref.at[_half(c, rows)], o_ref.at[_half(c, rows)], ssem, rsem, (x, y, 1 - c))
        cp.start()
        _remote(r_ref.at[_half(1 - c, rows)], o_ref.at[_half(1 - c, rows)], ssem, rsem,
                (x, y, 1 - c)).wait_recv()
        cp.wait_send()

    return pl.pallas_call(
        body, name=name,
        out_shape=jax.ShapeDtypeStruct(red.shape, F32),
        in_specs=[ANY], out_specs=ANY, input_output_aliases={0: 0},
        scratch_shapes=[pltpu.SemaphoreType.DMA, pltpu.SemaphoreType.DMA],
    )(red)


def _sum_small(me, svec, land, *, name):
    def body(me_ref, s_ref, l_ref, o_ref):
        mine = me_ref[0]
        acc = None
        for d in range(N_DEV):
            theirs = l_ref[jnp.maximum(jnp.bitwise_xor(mine, d) - 1, 0)]
            v = jnp.where(mine == d, s_ref[...], theirs)
            acc = v if acc is None else acc + v
        o_ref[...] = acc

    return pl.pallas_call(
        body, name=name,
        out_shape=jax.ShapeDtypeStruct(svec.shape, F32),
        grid_spec=pltpu.PrefetchScalarGridSpec(
            num_scalar_prefetch=1, grid=(1,),
            in_specs=[pl.BlockSpec(svec.shape, lambda i, s: (0, 0)),
                      pl.BlockSpec(land.shape, lambda i, s: (0, 0, 0))],
            out_specs=pl.BlockSpec(svec.shape, lambda i, s: (0, 0))),
        compiler_params=_params("arbitrary"),
    )(me, svec, land)


def _pack_early_shards(wts):
    parts = [wts["w_in"][0].T.astype(BF16)] + [wts[n].astype(BF16).reshape(-1, 1024) for n, _ in EARLY[1:]]
    used = sum(r for _, r in EARLY)
    return jnp.concatenate(parts + [jnp.zeros((EARLY_ROWS - used, 1024), BF16)], axis=0)


def _pack_late_shards(wts):
    parts, at = [], 0
    for n, off, rows in LATE:
        if off > at:
            parts.append(jnp.zeros((off - at, 1024), BF16))
        parts.append(wts[n].astype(BF16).reshape(rows, 1024))
        at = off + rows
    return jnp.concatenate(parts, axis=0)


def _unpack_early(full):
    w = {}
    w["w_in_t"] = jnp.concatenate(
        [full[0, N_SMALL:1200]] + [full[p, :1200] for p in range(1, N_CHIPS)]
        + [full[0, :N_SMALL], jnp.zeros((D_MODEL - N_SMALL, 1024), full.dtype)], axis=0)
    w["w_uq"] = full[:, 1200:1344].reshape(4, 384, 384).transpose(1, 0, 2).reshape(384, 1536)
    w["w_uk"] = full[:, 1344:1408].reshape(256, 1024)
    w["w_uv"] = full[:, 1408:1472].reshape(256, 1024)
    return w


def _pack_early_grads(g):
    gt = g["w_in_t"]
    rest = 1200 - N_SMALL
    w_in = jnp.stack([jnp.concatenate([gt[4 * D_MODEL:4 * D_MODEL + N_SMALL], gt[:rest]], axis=0)]
                     + [gt[rest + 1200 * (p - 1):rest + 1200 * p] for p in range(1, N_CHIPS)])
    parts = [
        w_in,
        g["w_uq"].reshape(384, 4, 384).transpose(1, 0, 2).reshape(4, 144, 1024),
        g["w_uk"].reshape(4, 64, 1024),
        g["w_uv"].reshape(4, 64, 1024),
        jnp.zeros((4, EARLY_ROWS - sum(r for _, r in EARLY), 1024), F32),
    ]
    return jnp.concatenate(parts, axis=1)


def _pad_w_uq(w):
    w = w.reshape(Q_RANK, N_HEADS, NOPE + ROPE)
    return jnp.pad(w, ((0, 0), (0, 0), (0, QK_PAD - NOPE - ROPE))).reshape(Q_RANK, N_HEADS * QK_PAD)


def _local_step(xs, pos, tgt, W, late_weights, late_bufs, send_late_grads, send_small_grads, send_early_grads,
                cw, vec):
    D = D_MODEL
    w_in_t = W["w_in_t"]
    w_uq = _pad_w_uq(W["w_uq"])
    w_ukv = jnp.concatenate([W["w_uk"], W["w_uv"]], axis=1)
    inv = ROPE_THETA ** (-jnp.arange(0, ROPE, 2, dtype=F32) / ROPE)
    inv_freq = jnp.concatenate([inv, inv, jnp.zeros((128 - ROPE,), F32)])[None, :]

    (h,) = _rowwise(lambda x, g: (_rms(x, g),), [(xs, D, 0)], [vec["norm_mix_pre"]], [(D, BF16)],
                    name="rms_pre")
    z = _mm(h, w_in_t, tb=True, out_dtype=BF16, name="mm_in")

    def qkv_prep(zs, p, qn, kvn, invf):
        ang = p.astype(F32) * invf
        cosv, sinv = jnp.cos(ang), jnp.sin(ang)
        ln = lax.broadcasted_iota(jnp.int32, ang.shape, 1)
        cs = jnp.where(ln < ROPE, cosv, 0.0)
        sa = jnp.where((ln >= ROPE // 2) & (ln < ROPE), sinv, 0.0)
        sb = jnp.where(ln < ROPE // 2, -sinv, 0.0)
        cqn = _rms(zs[:, :Q_RANK], qn)
        ckvn = _rms(zs[:, Q_RANK:Q_RANK + KV_RANK], kvn)
        kr = _rope(zs[:, 640:768], cs, sa, sb)
        return cqn, ckvn, kr, cs, sa, sb

    cqn, ckvn, krot, cs, sa, sb = _rowwise(
        qkv_prep, [(z, D, Z_S), (pos, 1, 0)], [vec["q_norm"], vec["kv_norm"], inv_freq],
        [(Q_RANK, BF16), (KV_RANK, BF16), (128, BF16), (128, F32), (128, F32), (128, F32)],
        name="qkv_prep")
    q_pre = _mm(cqn, w_uq, out_dtype=BF16, name="mm_uq")
    kv = _mm(ckvn, w_ukv, out_dtype=BF16, name="mm_ukv")

    def qk_rope(qp, kn, vv, kr, cs_, sa_, sb_):
        qs, ks, vs = [], [], []
        ones = jnp.ones((qp.shape[0], NOPE), F32)
        for hd in range(N_HEADS):
            qs.append(qp[:, hd * QK_PAD:hd * QK_PAD + NOPE])
            qs.append(_rope(qp[:, hd * QK_PAD + NOPE:(hd + 1) * QK_PAD], cs_, sa_, sb_))
            ks.append(kn[:, hd * NOPE:(hd + 1) * NOPE])
            ks.append(kr.astype(F32))
            vs.append(vv[:, hd * NOPE:(hd + 1) * NOPE])
            vs.append(ones)
        return (jnp.concatenate(qs, axis=1) * Q_SCALE, jnp.concatenate(ks, axis=1),
                jnp.concatenate(vs, axis=1))

    q, k, v = _rowwise(
        qk_rope, [(q_pre, 2048, 0), (kv, 1024, 0), (kv, 1024, 1), (krot, 128, 0), (cs, 128, 0),
                  (sa, 128, 0), (sb, 128, 0)], [],
        [(2048, BF16), (2048, BF16), (2048, BF16)], name="qk_rope")
    attn, lse = _flash_fwd(q, k, v)
    LW = late_weights(attn)
    y_attn = LW.fwd(attn, "w_o_attn", out_dtype=BF16, name="mm_o_attn")

    u1 = _conv_fwd(z, cw, vec["conv_b"])

    def ln_parts(u, g, b):
        mu = jnp.mean(u, axis=-1, keepdims=True)
        xc = u - mu
        rs = lax.rsqrt(jnp.mean(xc * xc, axis=-1, keepdims=True) + EPS)
        nh = xc * rs
        return nh, rs, nh * g + b

    def ln_silu(u, g, b):
        y = ln_parts(u, g, b)[2]
        return (y * _sigmoid(y),)

    (u3,) = _rowwise(ln_silu, [(u1, D, 0)], [vec["conv_ln_g"], vec["conv_ln_b"]], [(D, BF16)],
                     name="ln_silu")

    def merge(yc, ga, gc, ya, b2):
        return yc, _sigmoid(ga) * ya + _sigmoid(gc) * (yc + b2)

    y_conv, merged = LW.fwd(u3, "w_pw2", name="mm_pw2", tm=512, epilogue=merge, out_dtypes=(BF16, BF16),
                            extras=[(z, Z_GA), (z, Z_GC), y_attn, vec["b_pw2"]])

    def post1(m, x, g1, g2):
        x1 = x + _rms(m, g1)
        return m, x1, _rms(x1, g2)

    mo, x1, h2 = LW.fwd(merged, "w_out", name="mm_out", tm=512, epilogue=post1, out_dtypes=(F32, F32, BF16),
                        extras=[xs, vec["norm_mix_post"], vec["norm_mlp_pre"]])

    def sqrelu(acc):
        r = jnp.maximum(acc, 0.0)
        return r * r, r

    a2, r1 = LW.fwd(h2, "w_ff1", name="mm_ff1", epilogue=sqrelu, out_dtypes=(BF16, BF16))

    def post2(f_, x1_, t_, g):
        e = x1_ + _rms(f_, g) - t_
        dy = e * (1.0 / D)
        df, dg = _rms_bwd(f_, g, dy)
        loss = jnp.broadcast_to(jnp.sum(e * e, keepdims=True) * (0.5 / D), (1, 128))
        return dy, df, dg, loss

    dy, df, g_norm_mlp_post, loss = LW.fwd(a2, "w_ff2", name="mm_ff2", epilogue=post2, out_dtypes=(F32, BF16),
                                           extras=[x1, tgt, vec["norm_mlp_post"]], reds=(D, 128))
    late_bufs = _late_grad(a2, df, "w_ff2", late_bufs, name="mm_dw_ff2")
    df1 = LW.bwd(df, "w_ff2", name="mm_df1", extras=[r1], out_dtypes=(BF16,),
              epilogue=lambda acc, r: (acc * (2.0 * r.astype(F32)),))
    late_bufs = _late_grad(h2, df1, "w_ff1", late_bufs, name="mm_dw_ff1")

    def bwd_mid(dh2_, dy_, x1_, m, g2, g1):
        d1, dg2 = _rms_bwd(x1_, g2, dh2_)
        dx1_ = dy_ + d1
        dm, dg1 = _rms_bwd(m, g1, dx1_)
        return dx1_, dm, dg2, dg1

    dx1, dmo, g_norm_mlp_pre, g_norm_mix_post = LW.bwd(
        df1, "w_ff1", name="mm_dh2", tm=512, epilogue=bwd_mid, out_dtypes=(F32, BF16), reds=(D, D),
        extras=[dy, x1, mo, vec["norm_mlp_pre"], vec["norm_mix_post"]])
    late_bufs = _late_grad(merged, dmo, "w_out", late_bufs, name="mm_dw_out")
    def dmerge(dm, ga, gc, ya, yc, b2):
        sga, sgc = _sigmoid(ga), _sigmoid(gc)
        dya = dm * sga
        dyc = dm * sgc
        dga = dm * ya * sga * (1.0 - sga)
        dgc = dm * (yc + b2) * sgc * (1.0 - sgc)
        return dya, dyc, jnp.concatenate([dga, dgc], axis=1), jnp.sum(dyc, axis=0, keepdims=True)

    dz = lax.empty((xs.shape[0], Z_COLS), BF16)
    tm_dm = _tile(xs.shape[0], 512)
    gates = pl.BlockSpec((tm_dm, 2 * D), lambda i, j, k: (i, Z_GA // 2))
    dya, dyc, dz, g_b_pw2 = LW.bwd(
        dmo, "w_out", name="mm_dmerged", tm=tm_dm, epilogue=dmerge, out_dtypes=(BF16, BF16, (dz, gates)),
        reds=(D,), extras=[(z, Z_GA), (z, Z_GC), y_attn, y_conv, vec["b_pw2"]])

    late_bufs = _late_grad(attn, dya, "w_o_attn", late_bufs, name="mm_dw_o_attn")
    late_bufs = _late_grad(u3, dyc, "w_pw2", late_bufs, name="mm_dw_pw2")
    late_token = send_late_grads(late_bufs)

    def delta_fn(acc, o_):
        do32 = acc.astype(BF16).astype(F32)
        pr = do32 * o_
        ln = lax.broadcasted_iota(jnp.int32, (pr.shape[0], NOPE), 1)
        cols = []
        for hd in range(N_HEADS):
            dl = jnp.sum(pr[:, hd * NOPE:(hd + 1) * NOPE], axis=-1, keepdims=True)
            hi = dl.astype(BF16).astype(F32)
            cols.append(do32[:, hd * NOPE:(hd + 1) * NOPE])
            cols.append(jnp.where(ln == 0, -hi, jnp.where(ln == 1, hi - dl, 0.0)))
        return (jnp.concatenate(cols, axis=1),)

    tm_do = _tile(xs.shape[0], 512)
    do_ext = LW.bwd(dya, "w_o_attn", name="mm_dattn", tm=tm_do, epilogue=delta_fn, extras=[attn],
                    after=[late_token], out_dtypes=(
                        (lax.empty((xs.shape[0], N_HEADS * QK_PAD), BF16),
                         pl.BlockSpec((tm_do, N_HEADS * QK_PAD), lambda i, j, k: (i, 0))),))
    dq, dk, dv = _flash_bwd(q, k, v, do_ext, lse)

    def unrope(dq_, dk_, dv_, cs_, sa_, sb_):
        dq_, dk_ = dq_.astype(F32), dk_.astype(F32)
        qs, kn = [], []
        dkr = jnp.zeros_like(cs_)
        for hd in range(N_HEADS):
            qs.append(dq_[:, hd * QK_PAD:hd * QK_PAD + NOPE] * SCALE)
            qs.append(_rope_t(dq_[:, hd * QK_PAD + NOPE:(hd + 1) * QK_PAD] * SCALE, cs_, sa_, sb_))
            kn.append(dk_[:, hd * QK_PAD:hd * QK_PAD + NOPE] * K_UNSCALE)
            dkr = dkr + dk_[:, hd * QK_PAD + NOPE:(hd + 1) * QK_PAD]
        return (jnp.concatenate(qs, axis=1), jnp.concatenate(kn + [dv_], axis=1),
                _rope_t(dkr * K_UNSCALE, cs_, sa_, sb_))

    dq_pre, dkv, dkr = _rowwise(
        unrope, [(dq, 2048, 0), (dk, 2048, 0), (dv, 1024, 0), (cs, 128, 0), (sa, 128, 0), (sb, 128, 0)], [],
        [(2048, BF16), (2048, BF16), (128, F32)], name="unrope")
    g_w_uq = _mm(cqn, dq_pre, ta=True, name="mm_dw_uq")
    dcqn = _mm(dq_pre, w_uq, tb=True, out_dtype=BF16, name="mm_dcqn")
    g_w_ukv = _mm(ckvn, dkv, ta=True, name="mm_dw_ukv")
    dckvn = _mm(dkv, w_ukv, tb=True, out_dtype=BF16, name="mm_dckvn")

    def small_bwd(zs, dcq_, dckv_, dkr_, qn, kvn):
        dcq, dqn = _rms_bwd(zs[:, :Q_RANK], qn, dcq_)
        dckv, dkvn = _rms_bwd(zs[:, Q_RANK:Q_RANK + KV_RANK], kvn, dckv_)
        pad = jnp.zeros((zs.shape[0], D - 768), F32)
        return jnp.concatenate([dcq, dckv, dkr_, pad], axis=1), dqn, dkvn

    dz, g_q_norm, g_kv_norm = _rowwise(
        small_bwd, [(z, D, Z_S), (dcqn, Q_RANK, 0), (dckvn, KV_RANK, 0), (dkr, 128, 0)],
        [vec["q_norm"], vec["kv_norm"]], [(D, dz, Z_S)], [(1, Q_RANK), (1, KV_RANK)], name="small_bwd")

    def ln_silu_bwd(d3, u, g, b):
        nh, rs, y = ln_parts(u, g, b)
        sg = _sigmoid(y)
        dyv = d3 * (sg * (1.0 + y * (1.0 - sg)))
        dnh = dyv * g
        du = rs * (dnh - jnp.mean(dnh, axis=-1, keepdims=True)
                   - nh * jnp.mean(dnh * nh, axis=-1, keepdims=True))
        return (du, jnp.sum(dyv * nh, axis=0, keepdims=True), jnp.sum(dyv, axis=0, keepdims=True),
                jnp.sum(du, axis=0, keepdims=True))

    du1, g_ln_g, g_ln_b, g_conv_b = LW.bwd(
        dyc, "w_pw2", name="mm_du3", tm=512, epilogue=ln_silu_bwd, out_dtypes=(F32,), reds=(D, D, D),
        extras=[u1, vec["conv_ln_g"], vec["conv_ln_b"]])
    dz, g_conv_w = _conv_bwd(z, du1, cw, dz)
    small_token = send_small_grads({
        "conv_b": g_conv_b, "conv_ln_g": g_ln_g, "conv_ln_b": g_ln_b, "b_pw2": g_b_pw2,
        "norm_mix_post": g_norm_mix_post, "norm_mlp_pre": g_norm_mlp_pre, "norm_mlp_post": g_norm_mlp_post,
        "q_norm": g_q_norm, "kv_norm": g_kv_norm}, g_conv_w, loss)
    g_w_in_t = _mm(dz, h, ta=True, name="mm_dw_in", after=[small_token])
    early_token = send_early_grads({
        "w_in_t": g_w_in_t,
        "w_uq": g_w_uq.reshape(Q_RANK, N_HEADS, QK_PAD)[:, :, :NOPE + ROPE].reshape(Q_RANK, 1536),
        "w_uk": g_w_ukv[:, :1024], "w_uv": g_w_ukv[:, 1024:]})

    def final(dh_, dx1_, x, g):
        d, dg = _rms_bwd(x, g, dh_)
        return dx1_ + d, dg

    grad_x, g_norm_mix_pre = _mm(
        dz, w_in_t, name="mm_dh", after=[early_token], tiles=(_tile(xs.shape[0], 512), D, Z_COLS),
        b_spec=pl.BlockSpec((Z_COLS, D), lambda i, j, k: (0, 0), pipeline_mode=pl.Buffered(1)),
        epilogue=final, out_dtypes=(F32,), reds=(D,), extras=[dx1, xs, vec["norm_mix_pre"]])

    return grad_x, g_norm_mix_pre


def _row1024(a):
    return jnp.pad(a, ((0, 0), (0, 1024 - a.shape[1])))


BIG = tuple(n for n, _ in EARLY) + tuple(n for n, _, _ in LATE)
ORDER = ("norm_mix_pre", "w_in", "q_norm", "w_uq", "kv_norm", "w_uk", "w_uv", "w_o_attn", "conv_w",
         "conv_b", "conv_ln_g", "conv_ln_b", "w_pw2", "b_pw2", "w_out", "norm_mix_post", "norm_mlp_pre",
         "w_ff1", "w_ff2", "norm_mlp_post")


def kernel(x, positions, norm_mix_pre, w_in, q_norm, w_uq, kv_norm, w_uk, w_uv, w_o_attn, conv_w, conv_b, conv_ln_g, conv_ln_b, w_pw2, b_pw2, w_out, norm_mix_post, norm_mlp_pre, w_ff1, w_ff2, norm_mlp_post, loss_target, m_norm_mix_pre, m_w_in, m_q_norm, m_w_uq, m_kv_norm, m_w_uk, m_w_uv, m_w_o_attn, m_conv_w, m_conv_b, m_conv_ln_g, m_conv_ln_b, m_w_pw2, m_b_pw2, m_w_out, m_norm_mix_post, m_norm_mlp_pre, m_w_ff1, m_w_ff2, m_norm_mlp_post, v_norm_mix_pre, v_w_in, v_q_norm, v_w_uq, v_kv_norm, v_w_uk, v_w_uv, v_w_o_attn, v_conv_w, v_conv_b, v_conv_ln_g, v_conv_ln_b, v_w_pw2, v_b_pw2, v_w_out, v_norm_mix_post, v_norm_mlp_pre, v_w_ff1, v_w_ff2, v_norm_mlp_post):
    args = dict(locals())
    wts = {n: args[n] for n in ORDER}
    mom = {n: args["m_" + n] for n in ORDER}
    var = {n: args["v_" + n] for n in ORDER}
    T = x.shape[1]
    place = jnp.stack([2 * lax.axis_index("x") + lax.axis_index("y"), lax.axis_index("c")]).astype(jnp.int32)

    w_early = _pack_early_shards(wts)
    w_late = _pack_late_shards(wts)
    cw_own = jnp.pad(conv_w.reshape(CONV_W, -1), ((0, CONV_TAPS_PAD - CONV_W), (0, 0)))
    full, cw_all = _gather_weights(w_early, cw_own)
    full = lax.dynamic_update_slice(full, w_early[None], (place[0], 0, 0))
    cw_all = lax.dynamic_update_slice(cw_all, cw_own[None], (place[0], 0, 0))
    W = _unpack_early(full)
    cw = cw_all.transpose(1, 0, 2).reshape(CONV_TAPS_PAD, D_MODEL)
    sems, w_thru, land, token = _late_gather_start(w_late, full)
    vec = {n: wts[n] for n in SMALL_VECS}
    vec["norm_mix_pre"] = vec["norm_mix_pre"] + token[:1, :1]

    def late_weights(after):
        landed = _late_gather_wait(sems, w_thru, land, after)
        return _LateWeights(lax.dynamic_update_slice(landed, w_late[None], (place[0], 0, 0)))

    sent = {}

    def send_late_grads(bufs):
        sent["late"] = (bufs[0],) + _reduce_start(bufs[1], place, name="late_grads_start")
        return sent["late"][4]

    def send_small_grads(gs, g_conv_w, loss):
        rows = [_row1024(gs[n]) if n in gs else jnp.zeros((1, 1024), F32) for n in SMALL_VECS]
        svec = jnp.concatenate(rows + [_row1024(loss), jnp.zeros((SMALL_CONVW_ROW - SMALL_LOSS_ROW - 1, 1024), F32),
                                       g_conv_w], axis=0)
        sent["small"] = (svec,) + _reduce_start(svec, place, name="small_grads_start", whole=True)
        return sent["small"][4]

    def send_early_grads(g):
        g32 = _pack_early_grads(g)
        sent["early"] = (g32,) + _reduce_start(g32.astype(BF16), place, name="early_grads_start")
        return sent["early"][4]

    late_bufs = (lax.empty((N_CHIPS, LATE_ROWS, 1024), F32), lax.empty((N_CHIPS, LATE_ROWS, 1024), BF16))
    grad_x, g_norm_mix_pre = _local_step(x[0], positions.reshape(T, 1), loss_target[0], W, late_weights,
                                         late_bufs, send_late_grads, send_small_grads, send_early_grads, cw, vec)
    last = jnp.pad(g_norm_mix_pre, ((0, 7), (0, 0)))
    sent["last"] = (last,) + _reduce_start(last, place, name="last_grads_start", whole=True)

    def finish(group, after):
        g32, sems_, g_thru, land_, _ = sent[group]
        landed = _reduce_wait(sems_, g_thru, land_, after, name=group + "_grads_wait")
        red = _reduce_sum(place, g32, landed, name=group + "_grads_sum")
        return _join_halves(red, name=group + "_grads_join")

    def as2d(a):
        return a.reshape(-1, a.shape[-1]) if a.ndim != 4 or a.shape[2] == 1 else a.reshape(a.shape[1], -1)

    grads, delta, new_m, new_v = {}, {}, {}, {}

    def update(n):
        shp = wts[n].shape
        outs = _adamw(as2d(wts[n]), as2d(grads[n]), as2d(mom[n]), as2d(var[n]), name="adamw_" + n)
        delta[n], new_m[n], new_v[n] = (o.reshape(shp) for o in outs)

    g_late = finish("late", [sent["early"][4]])
    for n, off, rows in LATE:
        shp = wts[n].shape
        outs = _adamw(as2d(wts[n]), g_late, as2d(mom[n]), as2d(var[n]), g_row=off, name="adamw_" + n)
        delta[n], new_m[n], new_v[n], grads[n] = (o.reshape(shp) for o in outs)

    g_early = finish("early", [grad_x, delta["w_ff2"]])
    outs = _adamw(wts["w_in"][0].T, g_early[:1200], mom["w_in"][0].T, var["w_in"][0].T, name="adamw_w_in")
    delta["w_in"], new_m["w_in"], new_v["w_in"] = (o.T[None] for o in outs)
    grads["w_in"] = g_early[:1200].T[None]
    off = 1200
    for n, rows in EARLY[1:]:
        grads[n] = g_early[off:off + rows].reshape(wts[n].shape)
        off += rows
        update(n)

    def finish_small(group, after):
        svec, sems_, s_thru, land_, _ = sent[group]
        landed = _reduce_wait(sems_, s_thru, land_, after, name=group + "_grads_wait", whole=True)
        return _sum_small((2 * place[0] + place[1])[None], svec, landed, name=group + "_grads_sum")

    ssum = finish_small("small", [delta["w_in"]])
    for r, n in enumerate(SMALL_VECS):
        grads[n] = ssum[r:r + 1, :wts[n].shape[1]]
    grads["norm_mix_pre"] = finish_small("last", [ssum])[:1]
    cw_sum = ssum[SMALL_CONVW_ROW:SMALL_CONVW_ROW + CONV_W]
    grads["conv_w"] = lax.dynamic_slice_in_dim(cw_sum, place[0] * 256, 256, axis=1).reshape(conv_w.shape)
    loss_out = ssum[SMALL_LOSS_ROW, 0]
    update("conv_w")
    small = [jnp.concatenate([_row1024(t[n]) for n in SMALL_VECS], axis=0) for t in (wts, grads, mom, var)]
    outs = _adamw(*small, name="adamw_small")
    for r, n in enumerate(SMALL_VECS):
        delta[n], new_m[n], new_v[n] = (o[r:r + 1, :wts[n].shape[1]] for o in outs)

    return (loss_out, grad_x[None], *[grads[n] for n in ORDER], *[delta[n] for n in ORDER],
            *[new_m[n] for n in ORDER], *[new_v[n] for n in ORDER])
```

```python
import functools

import numpy as np
import jax
import jax.numpy as jnp
from jax import lax
from jax.experimental import pallas as pl
from jax.experimental.pallas import tpu as pltpu

F32 = jnp.float32
BF16 = jnp.bfloat16
MESH = pl.DeviceIdType.MESH

D_MODEL = 1024
N_HEADS = 8
NOPE = 128
ROPE = 64
QK_PAD = 256
Q_RANK = 384
KV_RANK = 256
CONV_W = 31
CONV_TAPS_PAD = 32
HALO = 32
D_FF = 4096
EPS = 1e-6
ROPE_THETA = 10000.0
SCALE = float((NOPE + ROPE) ** -0.5)
Q_SCALE = SCALE * float(np.log2(np.e))
K_UNSCALE = float(np.log(2.0))
Z_A, Z_B, Z_GA, Z_GC, Z_S = range(5)
Z_COLS = 5 * D_MODEL
N_SMALL = Q_RANK + KV_RANK + ROPE
N_CHIPS = 4
N_DEV = 8

ADAM_LR = 0.001
ADAM_B1 = 0.9
ADAM_B2 = 0.999
ADAM_EPS = 1e-08
ADAM_WD = 0.01
ADAM_STEP = 10

VMEM_LIMIT = 48 * 1024 * 1024
NEG = -1e30

EARLY_ROWS = 1280
MID = (("w_uq", 144), ("w_uk", 64), ("w_uv", 64))
MID_ROWS = 512
LATE = (("w_ff1", 0, 1024), ("w_o_attn", 1024, 256), ("w_pw2", 1536, 256), ("w_out", 1792, 256),
        ("w_ff2", 2048, 1024))
LATE_OFF = {n: off for n, off, _ in LATE}
LATE_ROWS = 3072
SUM_TILE = 128

SMALL_VECS = ("norm_mix_pre", "conv_b", "conv_ln_g", "conv_ln_b", "b_pw2", "norm_mix_post",
              "norm_mlp_pre", "norm_mlp_post", "q_norm", "kv_norm")
SMALL_LOSS_ROW = 10
SMALL_CONVW_ROW = 16
SMALL_ROWS = 48


ANY = pl.BlockSpec(memory_space=pl.ANY)


def _params(*sem):
    return pltpu.CompilerParams(dimension_semantics=sem, vmem_limit_bytes=VMEM_LIMIT)


def _tile(n, cap):
    if n <= cap:
        return n
    t = (cap // 128) * 128
    while n % t:
        t -= 128
    return t


def _row_tile(n, cap):
    if n <= cap:
        return n
    t = (cap // 8) * 8
    while n % t:
        t -= 8
    return t


MM_VMEM_BUDGET = 40 * 1024 * 1024


def _mm_tiles(M, N, K, in_bytes, out_bytes):
    tn = _tile(N, 1024)
    tk = K
    while True:
        for cap in (1024, 512, 256):
            tm = _tile(M, cap)
            need = 2 * tk * (tm * in_bytes[0] + tn * in_bytes[1]) + 2 * tm * tn * out_bytes
            need += tm * tn * 4 if tk < K else 0
            if need <= MM_VMEM_BUDGET:
                return tm, tn, tk
        tk = _tile(K, tk - 128)


def _mm(a, b, *, name, ta=False, tb=False, out_dtype=F32, extras=(), epilogue=None, out_dtypes=None,
        after=(), shape=None, tiles=None, b_spec=None, reds=(), k_slabs=False):
    if shape is None:
        M, K = (a.shape[1], a.shape[0]) if ta else a.shape
        N = b.shape[0] if tb else b.shape[1]
        assert K == (b.shape[1] if tb else b.shape[0]), (a.shape, b.shape, ta, tb)
    else:
        M, N, K = shape
    outs = tuple(out_dtypes or (out_dtype,))
    placed = [o for o in outs if isinstance(o, tuple)]
    ex_arrays = [e[0] if isinstance(e, tuple) else e for e in extras]
    if tiles is None:
        out_bytes = sum(jnp.dtype(d).itemsize for d in outs) + sum(e.dtype.itemsize for e in ex_arrays)
        tiles = _mm_tiles(M, N, K, (a.dtype.itemsize, b.dtype.itemsize), out_bytes)
    tm, tn, tk = tiles
    nk = K // tk
    dims = (((0 if ta else 1,), (1 if tb else 0,)), ((), ()))
    n_ex, n_out = len(extras), len(outs)
    n_pass = len(after) + len(placed)
    assert not reds or tn == N

    def body(*refs):
        a_ref, b_ref = refs[:2]
        ex_refs = refs[2:2 + n_ex]
        o_refs = refs[2 + n_ex + n_pass:2 + n_ex + n_pass + n_out]
        red_refs = refs[2 + n_ex + n_pass + n_out:2 + n_ex + n_pass + n_out + len(reds)]
        first_rows, k = pl.program_id(0) == 0, pl.program_id(2)
        bv = b_ref[...].astype(BF16)
        if k_slabs:
            kc = bv.shape[-1]
            part = sum(lax.dot_general(a_ref[:, p * kc:(p + 1) * kc].astype(BF16), bv[p], dims,
                                       preferred_element_type=F32) for p in range(bv.shape[0]))
        else:
            if bv.ndim == 3:
                bv = bv.reshape(-1, bv.shape[-1])
            part = lax.dot_general(a_ref[...].astype(BF16), bv, dims, preferred_element_type=F32)

        def finish(acc):
            ex = [r[...] for r in ex_refs]
            ex = [e.astype(F32) if e.dtype == BF16 else e for e in ex]
            res = epilogue(acc, *ex) if epilogue else (acc,) * n_out
            for o_ref, r in zip(o_refs, res):
                o_ref[...] = r.astype(o_ref.dtype)
            for red_ref, r in zip(red_refs, res[n_out:]):
                red_ref[...] += r

        if reds:
            @pl.when(first_rows & (k == 0))
            def _():
                for red_ref in red_refs:
                    red_ref[...] = jnp.zeros_like(red_ref)

        if nk == 1:
            finish(part)
        else:
            acc_ref = refs[-1]

            @pl.when(k == 0)
            def _():
                acc_ref[...] = part

            @pl.when((k > 0) & (k < nk - 1))
            def _():
                acc_ref[...] += part

            @pl.when(k == nk - 1)
            def _():
                finish(acc_ref[...] + part)

    a_spec = (pl.BlockSpec((tk, tm), lambda i, j, k: (k, i)) if ta
              else pl.BlockSpec((tm, tk), lambda i, j, k: (i, k)))
    if b_spec is None:
        b_spec = (pl.BlockSpec((tn, tk), lambda i, j, k: (j, k)) if tb
                  else pl.BlockSpec((tk, tn), lambda i, j, k: (k, j)))
    o_spec = pl.BlockSpec((tm, tn), lambda i, j, k: (i, j))
    row_spec = pl.BlockSpec((1, tn), lambda i, j, k: (0, j))
    ex_specs = []
    for e in extras:
        if isinstance(e, tuple):
            ex_specs.append(pl.BlockSpec((tm, tn), functools.partial(lambda i, j, k, cb: (i, cb), cb=e[1])))
        else:
            ex_specs.append(row_spec if e.shape[0] == 1 else o_spec)
    out_shape, out_specs, aliases = [], [], {}
    for o, out in enumerate(outs):
        if isinstance(out, tuple):
            out_shape.append(jax.ShapeDtypeStruct(out[0].shape, out[0].dtype))
            out_specs.append(out[1])
            aliases[2 + n_ex + len(after) + len(aliases)] = o
        else:
            out_shape.append(jax.ShapeDtypeStruct((M, N), out))
            out_specs.append(o_spec)
    passed = list(after) + [buf for buf, _ in placed]
    out_shape = out_shape + [jax.ShapeDtypeStruct((1, w), F32) for w in reds]
    out_specs = out_specs + [pl.BlockSpec((1, w), lambda i, j, k: (0, 0)) for w in reds]
    res = pl.pallas_call(
        body, name=name, out_shape=out_shape,
        grid=(M // tm, N // tn, nk),
        in_specs=[a_spec, b_spec] + ex_specs + [ANY] * len(passed),
        out_specs=out_specs, input_output_aliases=aliases,
        scratch_shapes=[pltpu.VMEM((tm, tn), F32)] if nk > 1 else [],
        compiler_params=_params(*(("arbitrary",) * 3 if reds else ("parallel", "parallel", "arbitrary"))),
    )(a, b, *ex_arrays, *passed)
    return res[0] if len(res) == 1 else res


class _LateWeights:
    def __init__(self, pack):
        self.pack = pack

    def fwd(self, x, n, *, name, tm=1024, **kw):
        T, off = x.shape[0], LATE_OFF[n]
        if n == "w_ff1":
            return _mm(x, self.pack, name=name, shape=(T, 4096, 1024), tiles=(_tile(T, tm), 1024, 1024),
                       b_spec=pl.BlockSpec((None, 1024, 1024), lambda i, j, k: (j, 0, 0)), **kw)
        if n == "w_ff2":
            view = self.pack.reshape(N_CHIPS, LATE_ROWS // 1024, 1024, 1024)
            return _mm(x, view, name=name, shape=(T, 1024, 4096), tiles=(_tile(T, min(tm, 512)), 1024, 4096),
                       b_spec=pl.BlockSpec((N_CHIPS, None, 1024, 1024), lambda i, j, k: (0, off // 1024, 0, 0)),
                       **kw)
        return _mm(x, self.pack.reshape(N_CHIPS, LATE_ROWS // 256, 256, 1024), name=name,
                   shape=(T, 1024, 1024), tiles=(_tile(T, tm), 1024, 1024), b_spec=self._whole(off), **kw)

    @staticmethod
    def _whole(off):
        return pl.BlockSpec((N_CHIPS, None, 256, 1024), lambda i, j, k: (0, off // 256, 0, 0))

    def bwd(self, dy, n, *, name, tm=1024, **kw):
        T, off = dy.shape[0], LATE_OFF[n]
        tm = _tile(T, tm)
        if n == "w_ff1":
            view = self.pack.reshape(N_CHIPS, LATE_ROWS // 1024, 1024, 1024)
            return _mm(dy, view, tb=True, name=name, shape=(T, 1024, 4096), tiles=(tm, 1024, 4096),
                       b_spec=pl.BlockSpec((N_CHIPS, None, 1024, 1024), lambda i, j, k: (0, off // 1024, 0, 0),
                                           pipeline_mode=pl.Buffered(1)), k_slabs=True, **kw)
        if n == "w_ff2":
            return _mm(dy, self.pack, tb=True, name=name, shape=(T, 4096, 1024), tiles=(tm, 1024, 1024),
                       b_spec=pl.BlockSpec((None, 1024, 1024), lambda i, j, k: (j, off // 1024, 0)), **kw)
        return _mm(dy, self.pack.reshape(N_CHIPS, LATE_ROWS // 256, 256, 1024), tb=True, name=name,
                   shape=(T, 1024, 1024), tiles=(tm, 1024, 1024), b_spec=self._whole(off), **kw)


def _late_grad(x, dy, n, bufs, *, name):
    T, off = x.shape[0], LATE_OFF[n]
    if n == "w_ff1":
        spec = pl.BlockSpec((None, 512, 1024), lambda i, j, k: (j, i, 0))
        shape, tiles = (1024, 4096, T), (512, 1024, T)
    elif n == "w_ff2":
        spec = pl.BlockSpec((None, 512, 1024), lambda i, j, k: (i // 2, off // 512 + i % 2, 0))
        shape, tiles = (4096, 1024, T), (512, 1024, T)
    else:
        spec = pl.BlockSpec((None, 256, 1024), lambda i, j, k: (i, off // 256, 0))
        shape, tiles = (1024, 1024, T), (256, 1024, T)
    return _mm(x, dy, ta=True, name=name, shape=shape, tiles=tiles, out_dtypes=[(b, spec) for b in bufs])


def _rowwise(fn, rows, consts, outs, reds=(), *, name, tm=256, after=()):
    T = rows[0][0].shape[0]
    tm = min(tm, T)
    n_row, n_const, n_out = len(rows), len(consts), len(outs)
    placed = [o for o in outs if len(o) == 3]
    n_pass = len(after) + len(placed)

    def body(*refs):
        i = pl.program_id(0)
        vals = [r[...] for r in refs[:n_row + n_const]]
        vals = [v.astype(F32) if v.dtype == BF16 else v for v in vals]
        res = fn(*vals)
        out_refs = refs[n_row + n_const + n_pass:]
        for k in range(n_out):
            out_refs[k][...] = res[k].astype(out_refs[k].dtype)
        for k in range(len(reds)):
            ref = out_refs[n_out + k]

            @pl.when(i == 0)
            def _(ref=ref):
                ref[...] = jnp.zeros_like(ref)

            ref[...] += res[n_out + k]

    in_specs = [pl.BlockSpec((tm, w), functools.partial(lambda i, cb: (i, cb), cb=cb))
                for (_, w, cb) in rows]
    in_specs += [pl.BlockSpec(c.shape, lambda i: (0, 0)) for c in consts]
    in_specs += [ANY] * n_pass
    out_specs, out_shape, aliases = [], [], {}
    for o, out in enumerate(outs):
        if len(out) == 3:
            w, buf, cb = out
            out_specs.append(pl.BlockSpec((tm, w), functools.partial(lambda i, cb: (i, cb), cb=cb)))
            out_shape.append(jax.ShapeDtypeStruct(buf.shape, buf.dtype))
            aliases[n_row + n_const + len(after) + len(aliases)] = o
        else:
            out_specs.append(pl.BlockSpec((tm, out[0]), lambda i: (i, 0)))
            out_shape.append(jax.ShapeDtypeStruct((T, out[0]), out[1]))
    out_specs += [pl.BlockSpec(s, lambda i: (0, 0)) for s in reds]
    out_shape += [jax.ShapeDtypeStruct(s, F32) for s in reds]
    return pl.pallas_call(
        body, name=name, out_shape=out_shape, grid=(T // tm,),
        in_specs=in_specs, out_specs=out_specs, input_output_aliases=aliases,
        compiler_params=_params("arbitrary"),
    )(*[r[0] for r in rows], *consts, *after, *[o[1] for o in placed])


def _rms(x, g):
    r = lax.rsqrt(jnp.mean(x * x, axis=-1, keepdims=True) + EPS)
    return x * r * g


def _rms_bwd(x, g, dy):
    r = lax.rsqrt(jnp.mean(x * x, axis=-1, keepdims=True) + EPS)
    n = x * r
    dyg = dy * g
    dx = r * (dyg - n * jnp.mean(dyg * n, axis=-1, keepdims=True))
    return dx, jnp.sum(dy * n, axis=0, keepdims=True)


def _sigmoid(x):
    return 1.0 / (1.0 + jnp.exp(-x))


def _rope(x, cs, sa, sb):
    return x * cs + pltpu.roll(x, 32, 1) * sa + pltpu.roll(x, 96, 1) * sb


def _rope_t(dy, cs, sa, sb):
    return dy * cs + pltpu.roll(dy * sa, 96, 1) + pltpu.roll(dy * sb, 32, 1)


def _causal_mask(t):
    row = lax.broadcasted_iota(jnp.int32, (t, t), 0)
    col = lax.broadcasted_iota(jnp.int32, (t, t), 1)
    return col <= row


FLASH_FWD_TQ = 4096
FLASH_FWD_TK = 1024
FLASH_FWD_HEADS = 1
FLASH_BWD_TQ = 1024
FLASH_BWD_TK = 512
NT_DIMS = (((1,), (1,)), ((), ()))
TN_DIMS = (((0,), (0,)), ((), ()))


def _flash_fwd(q, k, v, *, tq=FLASH_FWD_TQ, tk=FLASH_FWD_TK, hp=FLASH_FWD_HEADS):
    T = q.shape[0]
    tq, tk = min(tq, T), min(tk, T)
    n_diag = tq // tk
    assert n_diag >= 1 and N_HEADS % hp == 0
    heads = range(hp)

    def body(q_ref, k_ref, v_ref, o_ref, lse_ref):
        i = pl.program_id(1)

        def cols(ref, rows, h, width=QK_PAD):
            return ref[rows, h * width:(h + 1) * width]

        def logits(j, r0=0):
            sl = pl.ds(pl.multiple_of(j * tk, tk), tk)
            return tuple(lax.dot_general(cols(q_ref, slice(r0, tq), h), cols(k_ref, sl, h), NT_DIMS,
                                         preferred_element_type=F32) for h in heads)

        def update(j, s, m, acc, masked, r0=0):
            sl = pl.ds(pl.multiple_of(j * tk, tk), tk)
            m_out, acc_out = [], []
            for h in heads:
                sh = s[h]
                if masked:
                    row = i * tq + r0 + lax.broadcasted_iota(jnp.int32, (tq - r0, tk), 0)
                    col = j * tk + lax.broadcasted_iota(jnp.int32, (tq - r0, tk), 1)
                    sh = jnp.where(col <= row, sh, NEG)
                m_old, acc_old = m[h][r0:], acc[h][r0:]
                m_new = jnp.maximum(m_old, jnp.max(sh, axis=-1, keepdims=True))
                p = jnp.exp2(sh - m_new)
                acc_new = (jnp.exp2(m_old - m_new) * acc_old
                           + jnp.dot(p.astype(BF16), cols(v_ref, sl, h), preferred_element_type=F32))
                if r0:
                    m_new = jnp.concatenate([m[h][:r0], m_new], axis=0)
                    acc_new = jnp.concatenate([acc[h][:r0], acc_new], axis=0)
                m_out.append(m_new)
                acc_out.append(acc_new)
            return tuple(m_out), tuple(acc_out)

        def step(j, carry):
            s, m, acc = carry
            s_next = logits(j + 1)
            m, acc = update(j, s, m, acc, False)
            return s_next, m, acc

        first = i * n_diag
        init = (logits(0), tuple(jnp.full((tq, 1), NEG, F32) for _ in heads),
                tuple(jnp.zeros((tq, QK_PAD), F32) for _ in heads))
        s, m, acc = lax.fori_loop(0, first, step, init)
        for d in range(n_diag):
            m, acc = update(first + d, s, m, acc, True, r0=d * tk)
            if d + 1 < n_diag:
                s = logits(first + d + 1, r0=(d + 1) * tk)
        for h in heads:
            l = acc[h][:, NOPE:]
            o_ref[:, h * NOPE:(h + 1) * NOPE] = (acc[h][:, :NOPE] / l).astype(BF16)
            lse_ref[:, h * NOPE:(h + 1) * NOPE] = m[h] + jnp.log2(l)

    return pl.pallas_call(
        body, name="flash_fwd",
        out_shape=[jax.ShapeDtypeStruct((T, N_HEADS * NOPE), BF16),
                   jax.ShapeDtypeStruct((T, N_HEADS * NOPE), F32)],
        grid=(N_HEADS // hp, T // tq),
        in_specs=[pl.BlockSpec((tq, hp * QK_PAD), lambda h, i: (i, h)),
                  pl.BlockSpec((T, hp * QK_PAD), lambda h, i: (0, h)),
                  pl.BlockSpec((T, hp * QK_PAD), lambda h, i: (0, h))],
        out_specs=[pl.BlockSpec((tq, hp * NOPE), lambda h, i: (i, h)),
                   pl.BlockSpec((tq, hp * NOPE), lambda h, i: (i, h))],
        compiler_params=_params("parallel", "arbitrary"),
    )(q, k, v)


def _flash_bwd(q, k, v, do, lse, *, tq=FLASH_BWD_TQ, tk=FLASH_BWD_TK):
    T = q.shape[0]
    tq, tk = min(tq, T), min(tk, T)

    def body(q_ref, k_ref, v_ref, do_ref, lse_ref, dq_ref, dk_ref, dv_ref, dq_acc):
        dq_acc[...] = jnp.zeros_like(dq_acc)
        for j in range(T // tk):
            k0 = j * tk
            kj = k_ref[k0:k0 + tk, :]
            vj = v_ref[k0:k0 + tk, :]
            dk = jnp.zeros((tk, QK_PAD), F32)
            dv = jnp.zeros((tk, NOPE), F32)
            for i in range(k0 // tq, T // tq):
                r0 = max(i * tq, k0)
                r1 = (i + 1) * tq
                qi = q_ref[r0:r1, :]
                doi = do_ref[r0:r1, :]
                s = lax.dot_general(qi, kj, NT_DIMS, preferred_element_type=F32)
                p = jnp.exp2(s - lse_ref[r0:r1, :][:, :1])
                if r0 < k0 + tk:
                    row = r0 + lax.broadcasted_iota(jnp.int32, (r1 - r0, tk), 0)
                    col = k0 + lax.broadcasted_iota(jnp.int32, (r1 - r0, tk), 1)
                    p = jnp.where(col <= row, p, 0.0)
                dv = dv + lax.dot_general(p.astype(BF16), doi[:, :NOPE], TN_DIMS, preferred_element_type=F32)
                ds = (p * lax.dot_general(doi, vj, NT_DIMS, preferred_element_type=F32)).astype(BF16)
                dk = dk + lax.dot_general(ds, qi, TN_DIMS, preferred_element_type=F32)
                dq_acc[r0:r1, :] += jnp.dot(ds, kj, preferred_element_type=F32)
            dk_ref[k0:k0 + tk, :] = dk.astype(BF16)
            dv_ref[k0:k0 + tk, :] = dv.astype(BF16)
        dq_ref[...] = dq_acc[...].astype(BF16)

    return pl.pallas_call(
        body, name="flash_bwd",
        out_shape=[jax.ShapeDtypeStruct((T, N_HEADS * QK_PAD), BF16),
                   jax.ShapeDtypeStruct((T, N_HEADS * QK_PAD), BF16),
                   jax.ShapeDtypeStruct((T, N_HEADS * NOPE), BF16)],
        grid=(N_HEADS,),
        in_specs=[pl.BlockSpec((T, QK_PAD), lambda h: (0, h)),
                  pl.BlockSpec((T, QK_PAD), lambda h: (0, h)),
                  pl.BlockSpec((T, QK_PAD), lambda h: (0, h)),
                  pl.BlockSpec((T, QK_PAD), lambda h: (0, h)),
                  pl.BlockSpec((T, NOPE), lambda h: (0, h))],
        out_specs=[pl.BlockSpec((T, QK_PAD), lambda h: (0, h)),
                   pl.BlockSpec((T, QK_PAD), lambda h: (0, h)),
                   pl.BlockSpec((T, NOPE), lambda h: (0, h))],
        scratch_shapes=[pltpu.VMEM((T, QK_PAD), F32)],
        compiler_params=_params("arbitrary"),
    )(q, k, v, do, lse)


SUB = 128


def _windows(buf, t0, c0, first, last):
    span = SUB + 8 * ((last + 7) // 8)
    base = buf[t0:t0 + span, c0:c0 + SUB]
    for r in range(8):
        offs = [o for o in range(first, last + 1) if o % 8 == r]
        if offs:
            win = base if r == 0 else pltpu.roll(base, span - r, 0)
            for o in offs:
                yield o, win[o - r:o - r + SUB, :]


def _glu(a, b):
    return a.astype(F32) * _sigmoid(b.astype(F32))


def _conv_fwd(z, w, bias, *, tm=256):
    T, C = z.shape[0], D_MODEL
    tm = min(tm, T)
    hb = tm // HALO
    lead = HALO - (CONV_W - 1)

    def body(a_ref, b_ref, ah_ref, bh_ref, w_ref, bias_ref, o_ref, buf):
        t = pl.program_id(0)
        buf[0:HALO, :] = jnp.where(t > 0, _glu(ah_ref[...], bh_ref[...]), 0.0)
        buf[HALO:, :] = _glu(a_ref[...], b_ref[...])
        for c0 in range(0, C, SUB):
            for t0 in range(0, tm, SUB):
                acc = jnp.broadcast_to(bias_ref[:, c0:c0 + SUB], (SUB, SUB))
                for o, win in _windows(buf, t0, c0, lead, lead + CONV_W - 1):
                    acc = acc + win * w_ref[o - lead:o - lead + 1, c0:c0 + SUB]
                o_ref[t0:t0 + SUB, c0:c0 + SUB] = acc

    def halo(cb):
        return pl.BlockSpec((HALO, C), lambda t: (jnp.maximum(t * hb - 1, 0), cb))

    return pl.pallas_call(
        body, name="conv_fwd",
        out_shape=jax.ShapeDtypeStruct((T, C), F32),
        grid=(T // tm,),
        in_specs=[pl.BlockSpec((tm, C), lambda t: (t, Z_A)), pl.BlockSpec((tm, C), lambda t: (t, Z_B)),
                  halo(Z_A), halo(Z_B),
                  pl.BlockSpec((CONV_TAPS_PAD, C), lambda t: (0, 0)), pl.BlockSpec((1, C), lambda t: (0, 0))],
        out_specs=pl.BlockSpec((tm, C), lambda t: (t, 0)),
        scratch_shapes=[pltpu.VMEM((HALO + tm, C), F32)],
        compiler_params=_params("parallel"),
    )(z, z, z, z, w, bias)


def _conv_bwd(z, du1, w, dz, *, tm=256):
    T, C = z.shape[0], D_MODEL
    tm = min(tm, T)
    hb = tm // HALO
    nt = T // tm
    last_halo = T // HALO - 1
    lead = HALO - (CONV_W - 1)
    assert (Z_A, Z_B) == (0, 1)

    def body(a_ref, b_ref, ah_ref, bh_ref, d_ref, dh_ref, w_ref, dz_in, dz_ref, dw_ref, ubuf, dbuf, dw8, gate):
        t = pl.program_id(0)
        ubuf[0:HALO, :] = jnp.where(t > 0, _glu(ah_ref[...], bh_ref[...]), 0.0)
        gate[...] = _sigmoid(b_ref[...].astype(F32))
        ubuf[HALO:, :] = a_ref[...].astype(F32) * gate[...]
        dbuf[0:tm, :] = d_ref[...]
        dbuf[tm:, :] = jnp.where(t < nt - 1, dh_ref[...], 0.0)

        @pl.when(t == 0)
        def _():
            dw8[...] = jnp.zeros_like(dw8)

        for c0 in range(0, C, SUB):
            for t0 in range(0, tm, SUB):
                acc = jnp.zeros((SUB, SUB), F32)
                for o, win in _windows(dbuf, t0, c0, 0, CONV_W - 1):
                    kk = CONV_W - 1 - o
                    acc = acc + win * w_ref[kk:kk + 1, c0:c0 + SUB]
                av = a_ref[t0:t0 + SUB, c0:c0 + SUB].astype(F32)
                sg = gate[t0:t0 + SUB, c0:c0 + SUB]
                dz_ref[t0:t0 + SUB, c0:c0 + SUB] = (acc * sg).astype(dz_ref.dtype)
                dz_ref[t0:t0 + SUB, C + c0:C + c0 + SUB] = (acc * av * sg * (1.0 - sg)).astype(dz_ref.dtype)
                dsub = d_ref[t0:t0 + SUB, c0:c0 + SUB]
                for o, win in _windows(ubuf, t0, c0, lead, lead + CONV_W - 1):
                    kk = o - lead
                    dw8[8 * kk:8 * kk + 8, c0:c0 + SUB] += jnp.sum(
                        (dsub * win).reshape(SUB // 8, 8, SUB), axis=0)

        @pl.when(t == nt - 1)
        def _():
            dw_ref[...] = jnp.sum(dw8[...].reshape(CONV_TAPS_PAD, 8, C), axis=1)

    def halo(cb):
        return pl.BlockSpec((HALO, C), lambda t: (jnp.maximum(t * hb - 1, 0), cb))

    return pl.pallas_call(
        body, name="conv_bwd",
        out_shape=[jax.ShapeDtypeStruct(dz.shape, dz.dtype), jax.ShapeDtypeStruct((CONV_TAPS_PAD, C), F32)],
        grid=(nt,),
        in_specs=[pl.BlockSpec((tm, C), lambda t: (t, Z_A)), pl.BlockSpec((tm, C), lambda t: (t, Z_B)),
                  halo(Z_A), halo(Z_B),
                  pl.BlockSpec((tm, C), lambda t: (t, 0)),
                  pl.BlockSpec((HALO, C), lambda t: (jnp.minimum((t + 1) * hb, last_halo), 0)),
                  pl.BlockSpec((CONV_TAPS_PAD, C), lambda t: (0, 0)), ANY],
        out_specs=[pl.BlockSpec((tm, 2 * C), lambda t: (t, 0)),
                   pl.BlockSpec((CONV_TAPS_PAD, C), lambda t: (0, 0))],
        input_output_aliases={7: 0},
        scratch_shapes=[pltpu.VMEM((HALO + tm, C), F32), pltpu.VMEM((tm + HALO, C), F32),
                        pltpu.VMEM((CONV_TAPS_PAD * 8, C), F32), pltpu.VMEM((tm, C), F32)],
        compiler_params=_params("arbitrary"),
    )(z, z, z, z, du1, du1, w, dz)


def _adamw(w, g, m, v, *, name, g_row=None):
    R, C = w.shape
    tr = _row_tile(R, max(8, (1 << 18) // C // 8 * 8))
    first = 0 if g_row is None else g_row // tr
    assert g_row is None or g_row % tr == 0

    def body(w_ref, g_ref, m_ref, v_ref, d_ref, nm_ref, nv_ref, *g_out):
        gv = g_ref[...]
        nm = ADAM_B1 * m_ref[...] + (1.0 - ADAM_B1) * gv
        nv = ADAM_B2 * v_ref[...] + (1.0 - ADAM_B2) * (gv * gv)
        m_hat = nm / (1.0 - ADAM_B1 ** ADAM_STEP)
        v_hat = nv / (1.0 - ADAM_B2 ** ADAM_STEP)
        d_ref[...] = -ADAM_LR * (m_hat / (jnp.sqrt(v_hat) + ADAM_EPS) + ADAM_WD * w_ref[...])
        nm_ref[...] = nm
        nv_ref[...] = nv
        for ref in g_out:
            ref[...] = gv

    spec = pl.BlockSpec((tr, C), lambda i: (i, 0))
    g_spec = pl.BlockSpec((tr, C), lambda i: (first + i, 0))
    n_out = 3 if g_row is None else 4
    return pl.pallas_call(
        body, name=name, out_shape=[jax.ShapeDtypeStruct((R, C), F32)] * n_out,
        grid=(R // tr,), in_specs=[spec, g_spec, spec, spec], out_specs=[spec] * n_out,
        compiler_params=_params("parallel"),
    )(w, g, m, v)


def _coords():
    return lax.axis_index("x"), lax.axis_index("y"), lax.axis_index("c")


def _remote(src, dst, ssem, rsem, to):
    return pltpu.make_async_remote_copy(src_ref=src, dst_ref=dst, send_sem=ssem, recv_sem=rsem,
                                        device_id=to, device_id_type=MESH)


def _half(c, rows):
    return pl.ds(pl.multiple_of(c * (rows // 2), 16), rows // 2)


def _gather_weights(wpack, cw):
    def body(w_ref, cw_ref, full_ref, cwf_ref, ssem, rsem):
        x, y, c = _coords()
        q = 2 * x + y
        sib = (x, y, 1 - c)
        chips = [(1 - x, y), (x, 1 - y), (1 - x, 1 - y)]
        mine, other = _half(c, w_ref.shape[0]), _half(1 - c, w_ref.shape[0])
        first = []
        for j, (px, py) in enumerate(chips):
            first.append(_remote(w_ref.at[mine], full_ref.at[q, mine], ssem.at[j], rsem.at[j], (px, py, c)))
            first.append(_remote(cw_ref, cwf_ref.at[q], ssem.at[6 + j], rsem.at[6 + j], (px, py, c)))
        for cp in first:
            cp.start()
        passed = []
        for j, (px, py) in enumerate(chips):
            pq = 2 * px + py
            _remote(w_ref.at[mine], full_ref.at[pq, mine], ssem.at[j], rsem.at[j], (px, py, c)).wait_recv()
            fw = _remote(full_ref.at[pq, mine], full_ref.at[pq, mine], ssem.at[3 + j], rsem.at[3 + j], sib)
            fw.start()
            passed.append(fw)
        for j, (px, py) in enumerate(chips):
            pq = 2 * px + py
            _remote(full_ref.at[pq, other], full_ref.at[pq, other], ssem.at[3 + j], rsem.at[3 + j], sib).wait_recv()
            _remote(cw_ref, cwf_ref.at[pq], ssem.at[6 + j], rsem.at[6 + j], (px, py, c)).wait_recv()
        for cp in first + passed:
            cp.wait_send()

    return pl.pallas_call(
        body, name="gather_weights",
        out_shape=[jax.ShapeDtypeStruct((N_CHIPS,) + wpack.shape, wpack.dtype),
                   jax.ShapeDtypeStruct((N_CHIPS,) + cw.shape, cw.dtype)],
        in_specs=[ANY, ANY], out_specs=[ANY, ANY],
        scratch_shapes=[pltpu.SemaphoreType.DMA((9,)), pltpu.SemaphoreType.DMA((9,))],
    )(wpack, cw)


SEM = pl.BlockSpec(memory_space=pltpu.SEMAPHORE)
HBM = pl.BlockSpec(memory_space=pltpu.HBM)
N_LATE = 6


def _late_peers(x, y, c):
    out = []
    for j, (px, py) in enumerate([(1 - x, y), (x, 1 - y), (1 - x, 1 - y)]):
        for t in (0, 1):
            out.append((2 * j + t, px, py, c if t == 0 else 1 - c))
    return out


def _late_gather_start(w, after, *, name):
    rows = w.shape[0]

    def body(w_ref, land_ref, after_ref, *outs):
        sems = outs[:2 * N_LATE]
        token = outs[2 * N_LATE + 2]
        x, y, c = _coords()
        q = 2 * x + y
        mine = _half(c, rows)
        for k, px, py, pc in _late_peers(x, y, c):
            _remote(w_ref.at[mine], land_ref.at[q, mine], sems[k], sems[N_LATE + k], (px, py, pc)).start()
        token[...] = jnp.zeros_like(token)

    res = pl.pallas_call(
        body, name=name,
        out_shape=tuple([pltpu.SemaphoreType.DMA(())] * (2 * N_LATE)) + (
            pltpu.HBM(w.shape, w.dtype), pltpu.HBM((N_CHIPS,) + w.shape, w.dtype),
            jax.ShapeDtypeStruct((8, 128), F32)),
        in_specs=(HBM, HBM, ANY),
        out_specs=tuple([SEM] * (2 * N_LATE)) + (HBM, HBM, pl.BlockSpec(memory_space=pltpu.VMEM)),
        input_output_aliases={0: 2 * N_LATE, 1: 2 * N_LATE + 1},
        compiler_params=pltpu.CompilerParams(has_side_effects=pltpu.SideEffectType.DATAFLOW_SIDE_EFFECTING),
    )(pltpu.with_memory_space_constraint(w, pltpu.HBM),
      pltpu.with_memory_space_constraint(lax.empty((N_CHIPS,) + w.shape, w.dtype), pltpu.HBM), after)
    return res[:2 * N_LATE], res[2 * N_LATE], res[2 * N_LATE + 1], res[2 * N_LATE + 2]


def _late_gather_wait(sems, w_thru, land_thru, after, *, name):
    rows = w_thru.shape[0]

    def body(w_ref, land_ref, *rest):
        sems = rest[:2 * N_LATE]
        x, y, c = _coords()
        for k, px, py, pc in _late_peers(x, y, c):
            cp = _remote(w_ref.at[_half(pc, rows)], land_ref.at[2 * px + py, _half(pc, rows)],
                         sems[k], sems[N_LATE + k], (px, py, pc))
            cp.wait_send()
            cp.wait_recv()

    return pl.pallas_call(
        body, name=name,
        out_shape=(pltpu.HBM(w_thru.shape, w_thru.dtype), pltpu.HBM(land_thru.shape, land_thru.dtype)),
        in_specs=(HBM, HBM) + tuple([SEM] * (2 * N_LATE)) + (ANY,),
        out_specs=(HBM, HBM), input_output_aliases={0: 0, 1: 1},
        compiler_params=pltpu.CompilerParams(has_side_effects=pltpu.SideEffectType.DATAFLOW_SIDE_EFFECTING),
    )(w_thru, land_thru, *sems, after)[1]


N_PEERS = N_DEV - 1


def _peers(x, y, c):
    out = []
    for r in range(1, N_DEV):
        out.append((r - 1, 1 - x if r & 4 else x, 1 - y if r & 2 else y, 1 - c if r & 1 else c))
    return out


def _piece(g_ref, px, py, pc, whole):
    return g_ref if whole else g_ref.at[2 * px + py, _half(pc, g_ref.shape[1])]


def _reduce_start(g16, after, *, name, whole=False):
    def body(g_ref, land_ref, after_ref, *outs):
        sems = outs[:2 * N_PEERS]
        token = outs[2 * N_PEERS + 2]
        x, y, c = _coords()
        for k, px, py, pc in _peers(x, y, c):
            _remote(_piece(g_ref, px, py, pc, whole), land_ref.at[k], sems[k], sems[N_PEERS + k],
                    (px, py, pc)).start()
        token[...] = jnp.zeros_like(token)

    land_shape = (N_PEERS,) + (g16.shape if whole else (g16.shape[1] // 2, 1024))
    res = pl.pallas_call(
        body, name=name,
        out_shape=tuple([pltpu.SemaphoreType.DMA(())] * (2 * N_PEERS)) + (
            pltpu.HBM(g16.shape, g16.dtype), pltpu.HBM(land_shape, g16.dtype),
            jax.ShapeDtypeStruct((8, 128), F32)),
        in_specs=(HBM, HBM, ANY),
        out_specs=tuple([SEM] * (2 * N_PEERS)) + (HBM, HBM, pl.BlockSpec(memory_space=pltpu.VMEM)),
        input_output_aliases={0: 2 * N_PEERS, 1: 2 * N_PEERS + 1},
        compiler_params=pltpu.CompilerParams(has_side_effects=pltpu.SideEffectType.DATAFLOW_SIDE_EFFECTING),
    )(pltpu.with_memory_space_constraint(g16, pltpu.HBM),
      pltpu.with_memory_space_constraint(lax.empty(land_shape, g16.dtype), pltpu.HBM), after)
    return res[:2 * N_PEERS], res[2 * N_PEERS], res[2 * N_PEERS + 1], res[2 * N_PEERS + 2]


def _reduce_wait(sems, g_thru, land_thru, after, *, name, whole=False):
    def body(g_ref, land_ref, *rest):
        sems = rest[:2 * N_PEERS]
        x, y, c = _coords()
        for k, px, py, pc in _peers(x, y, c):
            cp = _remote(_piece(g_ref, px, py, pc, whole), land_ref.at[k], sems[k], sems[N_PEERS + k],
                         (px, py, pc))
            cp.wait_send()
            cp.wait_recv()

    return pl.pallas_call(
        body, name=name,
        out_shape=(pltpu.HBM(g_thru.shape, g_thru.dtype), pltpu.HBM(land_thru.shape, land_thru.dtype)),
        in_specs=(HBM, HBM) + tuple([SEM] * (2 * N_PEERS)) + tuple([ANY] * len(after)),
        out_specs=(HBM, HBM), input_output_aliases={0: 0, 1: 1},
        compiler_params=pltpu.CompilerParams(has_side_effects=pltpu.SideEffectType.DATAFLOW_SIDE_EFFECTING),
    )(g_thru, land_thru, *sems, *after)[1]


def _reduce_sum(place, g32, land, *, name):
    rows = g32.shape[1]
    nb = rows // 2 // SUM_TILE

    def body(place_ref, g_ref, l_ref, o_ref):
        acc = g_ref[...]
        for k in range(N_PEERS):
            acc = acc + l_ref[k].astype(F32)
        o_ref[...] = acc

    return pl.pallas_call(
        body, name=name,
        out_shape=jax.ShapeDtypeStruct((rows, 1024), F32),
        grid_spec=pltpu.PrefetchScalarGridSpec(
            num_scalar_prefetch=1, grid=(nb,),
            in_specs=[pl.BlockSpec((None, SUM_TILE, 1024), lambda i, s: (s[0], s[1] * nb + i, 0)),
                      pl.BlockSpec((N_PEERS, SUM_TILE, 1024), lambda i, s: (0, i, 0))],
            out_specs=pl.BlockSpec((SUM_TILE, 1024), lambda i, s: (s[1] * nb + i, 0))),
        compiler_params=_params("parallel"),
    )(place, g32, land)


def _join_halves(red, *, name):
    rows = red.shape[0]

    def body(r_ref, o_ref, ssem, rsem):
        x, y, c = _coords()
        cp = _remote(r_ref.at[_half(c, rows)], o_ref.at[_half(c, rows)], ssem, rsem, (x, y, 1 - c))
        cp.start()
        _remote(r_ref.at[_half(1 - c, rows)], o_ref.at[_half(1 - c, rows)], ssem, rsem,
                (x, y, 1 - c)).wait_recv()
        cp.wait_send()

    return pl.pallas_call(
        body, name=name,
        out_shape=jax.ShapeDtypeStruct(red.shape, F32),
        in_specs=[ANY], out_specs=ANY, input_output_aliases={0: 0},
        scratch_shapes=[pltpu.SemaphoreType.DMA, pltpu.SemaphoreType.DMA],
    )(red)


def _sum_small(me, svec, land, *, name):
    def body(me_ref, s_ref, l_ref, o_ref):
        mine = me_ref[0]
        acc = None
        for d in range(N_DEV):
            theirs = l_ref[jnp.maximum(jnp.bitwise_xor(mine, d) - 1, 0)]
            v = jnp.where(mine == d, s_ref[...], theirs)
            acc = v if acc is None else acc + v
        o_ref[...] = acc

    return pl.pallas_call(
        body, name=name,
        out_shape=jax.ShapeDtypeStruct(svec.shape, F32),
        grid_spec=pltpu.PrefetchScalarGridSpec(
            num_scalar_prefetch=1, grid=(1,),
            in_specs=[pl.BlockSpec(svec.shape, lambda i, s: (0, 0)),
                      pl.BlockSpec(land.shape, lambda i, s: (0, 0, 0))],
            out_specs=pl.BlockSpec(svec.shape, lambda i, s: (0, 0))),
        compiler_params=_params("arbitrary"),
    )(me, svec, land)


def _pack_early_shards(wts):
    return jnp.concatenate([wts["w_in"][0].T.astype(BF16), jnp.zeros((EARLY_ROWS - 1200, 1024), BF16)], axis=0)


def _pack_mid_shards(wts):
    parts = [wts[n].astype(BF16).reshape(-1, 1024) for n, _ in MID]
    return jnp.concatenate(parts + [jnp.zeros((MID_ROWS - sum(r for _, r in MID), 1024), BF16)], axis=0)


def _pack_late_shards(wts):
    parts, at = [], 0
    for n, off, rows in LATE:
        if off > at:
            parts.append(jnp.zeros((off - at, 1024), BF16))
        parts.append(wts[n].astype(BF16).reshape(rows, 1024))
        at = off + rows
    return jnp.concatenate(parts, axis=0)


def _unpack_early(full):
    return jnp.concatenate(
        [full[0, N_SMALL:1200]] + [full[p, :1200] for p in range(1, N_CHIPS)]
        + [full[0, :N_SMALL], jnp.zeros((D_MODEL - N_SMALL, 1024), full.dtype)], axis=0)


def _unpack_mid(full):
    return {"w_uq": full[:, :144].reshape(4, 384, 384).transpose(1, 0, 2).reshape(384, 1536),
            "w_uk": full[:, 144:208].reshape(256, 1024),
            "w_uv": full[:, 208:272].reshape(256, 1024)}


def _pack_early_grads(gt):
    rest = 1200 - N_SMALL
    w_in = jnp.stack([jnp.concatenate([gt[4 * D_MODEL:4 * D_MODEL + N_SMALL], gt[:rest]], axis=0)]
                     + [gt[rest + 1200 * (p - 1):rest + 1200 * p] for p in range(1, N_CHIPS)])
    return jnp.concatenate([w_in, jnp.zeros((4, EARLY_ROWS - 1200, 1024), F32)], axis=1)


def _pack_mid_grads(g):
    parts = [
        g["w_uq"].reshape(384, 4, 384).transpose(1, 0, 2).reshape(4, 144, 1024),
        g["w_uk"].reshape(4, 64, 1024),
        g["w_uv"].reshape(4, 64, 1024),
        jnp.zeros((4, MID_ROWS - sum(r for _, r in MID), 1024), F32),
    ]
    return jnp.concatenate(parts, axis=1)


def _pad_w_uq(w):
    w = w.reshape(Q_RANK, N_HEADS, NOPE + ROPE)
    return jnp.pad(w, ((0, 0), (0, 0), (0, QK_PAD - NOPE - ROPE))).reshape(Q_RANK, N_HEADS * QK_PAD)


def _local_step(xs, pos, tgt, w_in_t, mid_weights, late_weights, late_bufs, send_mid_grads, send_late_grads,
                send_small_grads, send_early_grads, cw, vec):
    D = D_MODEL
    inv = ROPE_THETA ** (-jnp.arange(0, ROPE, 2, dtype=F32) / ROPE)
    inv_freq = jnp.concatenate([inv, inv, jnp.zeros((128 - ROPE,), F32)])[None, :]

    (h,) = _rowwise(lambda x, g: (_rms(x, g),), [(xs, D, 0)], [vec["norm_mix_pre"]], [(D, BF16)],
                    name="rms_pre")
    z = _mm(h, w_in_t, tb=True, out_dtype=BF16, name="mm_in")

    def qkv_prep(zs, p, qn, kvn, invf):
        ang = p.astype(F32) * invf
        cosv, sinv = jnp.cos(ang), jnp.sin(ang)
        ln = lax.broadcasted_iota(jnp.int32, ang.shape, 1)
        cs = jnp.where(ln < ROPE, cosv, 0.0)
        sa = jnp.where((ln >= ROPE // 2) & (ln < ROPE), sinv, 0.0)
        sb = jnp.where(ln < ROPE // 2, -sinv, 0.0)
        cqn = _rms(zs[:, :Q_RANK], qn)
        ckvn = _rms(zs[:, Q_RANK:Q_RANK + KV_RANK], kvn)
        kr = _rope(zs[:, 640:768], cs, sa, sb)
        return cqn, ckvn, kr, cs, sa, sb

    cqn, ckvn, krot, cs, sa, sb = _rowwise(
        qkv_prep, [(z, D, Z_S), (pos, 1, 0)], [vec["q_norm"], vec["kv_norm"], inv_freq],
        [(Q_RANK, BF16), (KV_RANK, BF16), (128, BF16), (128, F32), (128, F32), (128, F32)],
        name="qkv_prep")
    W = mid_weights(z)
    w_uq = _pad_w_uq(W["w_uq"])
    w_ukv = jnp.concatenate([W["w_uk"], W["w_uv"]], axis=1)
    q_pre = _mm(cqn, w_uq, out_dtype=BF16, name="mm_uq")
    kv = _mm(ckvn, w_ukv, out_dtype=BF16, name="mm_ukv")

    def qk_rope(qp, kn, vv, kr, cs_, sa_, sb_):
        qs, ks, vs = [], [], []
        ones = jnp.ones((qp.shape[0], NOPE), F32)
        for hd in range(N_HEADS):
            qs.append(qp[:, hd * QK_PAD:hd * QK_PAD + NOPE])
            qs.append(_rope(qp[:, hd * QK_PAD + NOPE:(hd + 1) * QK_PAD], cs_, sa_, sb_))
            ks.append(kn[:, hd * NOPE:(hd + 1) * NOPE])
            ks.append(kr.astype(F32))
            vs.append(vv[:, hd * NOPE:(hd + 1) * NOPE])
            vs.append(ones)
        return (jnp.concatenate(qs, axis=1) * Q_SCALE, jnp.concatenate(ks, axis=1),
                jnp.concatenate(vs, axis=1))

    q, k, v = _rowwise(
        qk_rope, [(q_pre, 2048, 0), (kv, 1024, 0), (kv, 1024, 1), (krot, 128, 0), (cs, 128, 0),
                  (sa, 128, 0), (sb, 128, 0)], [],
        [(2048, BF16), (2048, BF16), (2048, BF16)], name="qk_rope")
    attn, lse = _flash_fwd(q, k, v)
    LW = late_weights(attn)
    y_attn = LW.fwd(attn, "w_o_attn", out_dtype=BF16, name="mm_o_attn")

    u1 = _conv_fwd(z, cw, vec["conv_b"])

    def ln_parts(u, g, b):
        mu = jnp.mean(u, axis=-1, keepdims=True)
        xc = u - mu
        rs = lax.rsqrt(jnp.mean(xc * xc, axis=-1, keepdims=True) + EPS)
        nh = xc * rs
        return nh, rs, nh * g + b

    def ln_silu(u, g, b):
        y = ln_parts(u, g, b)[2]
        return (y * _sigmoid(y),)

    (u3,) = _rowwise(ln_silu, [(u1, D, 0)], [vec["conv_ln_g"], vec["conv_ln_b"]], [(D, BF16)],
                     name="ln_silu")

    def merge(yc, ga, gc, ya, b2):
        return yc, _sigmoid(ga) * ya + _sigmoid(gc) * (yc + b2)

    y_conv, merged = LW.fwd(u3, "w_pw2", name="mm_pw2", tm=512, epilogue=merge, out_dtypes=(BF16, BF16),
                            extras=[(z, Z_GA), (z, Z_GC), y_attn, vec["b_pw2"]])

    def post1(m, x, g1, g2):
        x1 = x + _rms(m, g1)
        return m, x1, _rms(x1, g2)

    mo, x1, h2 = LW.fwd(merged, "w_out", name="mm_out", tm=512, epilogue=post1, out_dtypes=(F32, F32, BF16),
                        extras=[xs, vec["norm_mix_post"], vec["norm_mlp_pre"]])

    def sqrelu(acc):
        r = jnp.maximum(acc, 0.0)
        return r * r, r

    a2, r1 = LW.fwd(h2, "w_ff1", name="mm_ff1", epilogue=sqrelu, out_dtypes=(BF16, BF16))

    def post2(f_, x1_, t_, g):
        e = x1_ + _rms(f_, g) - t_
        dy = e * (1.0 / D)
        df, dg = _rms_bwd(f_, g, dy)
        loss = jnp.broadcast_to(jnp.sum(e * e, keepdims=True) * (0.5 / D), (1, 128))
        return dy, df, dg, loss

    dy, df, g_norm_mlp_post, loss = LW.fwd(a2, "w_ff2", name="mm_ff2", epilogue=post2, out_dtypes=(F32, BF16),
                                           extras=[x1, tgt, vec["norm_mlp_post"]], reds=(D, 128))
    late_bufs = _late_grad(a2, df, "w_ff2", late_bufs, name="mm_dw_ff2")
    df1 = LW.bwd(df, "w_ff2", name="mm_df1", extras=[r1], out_dtypes=(BF16,),
              epilogue=lambda acc, r: (acc * (2.0 * r.astype(F32)),))
    late_bufs = _late_grad(h2, df1, "w_ff1", late_bufs, name="mm_dw_ff1")

    def bwd_mid(dh2_, dy_, x1_, m, g2, g1):
        d1, dg2 = _rms_bwd(x1_, g2, dh2_)
        dx1_ = dy_ + d1
        dm, dg1 = _rms_bwd(m, g1, dx1_)
        return dx1_, dm, dg2, dg1

    dx1, dmo, g_norm_mlp_pre, g_norm_mix_post = LW.bwd(
        df1, "w_ff1", name="mm_dh2", tm=512, epilogue=bwd_mid, out_dtypes=(F32, BF16), reds=(D, D),
        extras=[dy, x1, mo, vec["norm_mlp_pre"], vec["norm_mix_post"]])
    late_bufs = _late_grad(merged, dmo, "w_out", late_bufs, name="mm_dw_out")
    def dmerge(dm, ga, gc, ya, yc, b2):
        sga, sgc = _sigmoid(ga), _sigmoid(gc)
        dya = dm * sga
        dyc = dm * sgc
        dga = dm * ya * sga * (1.0 - sga)
        dgc = dm * (yc + b2) * sgc * (1.0 - sgc)
        return dya, dyc, jnp.concatenate([dga, dgc], axis=1), jnp.sum(dyc, axis=0, keepdims=True)

    dz = lax.empty((xs.shape[0], Z_COLS), BF16)
    tm_dm = _tile(xs.shape[0], 512)
    gates = pl.BlockSpec((tm_dm, 2 * D), lambda i, j, k: (i, Z_GA // 2))
    dya, dyc, dz, g_b_pw2 = LW.bwd(
        dmo, "w_out", name="mm_dmerged", tm=tm_dm, epilogue=dmerge, out_dtypes=(BF16, BF16, (dz, gates)),
        reds=(D,), extras=[(z, Z_GA), (z, Z_GC), y_attn, y_conv, vec["b_pw2"]])

    late_bufs = _late_grad(attn, dya, "w_o_attn", late_bufs, name="mm_dw_o_attn")
    late_bufs = _late_grad(u3, dyc, "w_pw2", late_bufs, name="mm_dw_pw2")
    late_token = send_late_grads(late_bufs)

    def delta_fn(acc, o_):
        do32 = acc.astype(BF16).astype(F32)
        pr = do32 * o_
        ln = lax.broadcasted_iota(jnp.int32, (pr.shape[0], NOPE), 1)
        cols = []
        for hd in range(N_HEADS):
            dl = jnp.sum(pr[:, hd * NOPE:(hd + 1) * NOPE], axis=-1, keepdims=True)
            hi = dl.astype(BF16).astype(F32)
            cols.append(do32[:, hd * NOPE:(hd + 1) * NOPE])
            cols.append(jnp.where(ln == 0, -hi, jnp.where(ln == 1, hi - dl, 0.0)))
        return (jnp.concatenate(cols, axis=1),)

    tm_do = _tile(xs.shape[0], 512)
    do_ext = LW.bwd(dya, "w_o_attn", name="mm_dattn", tm=tm_do, epilogue=delta_fn, extras=[attn],
                    after=[late_token], out_dtypes=(
                        (lax.empty((xs.shape[0], N_HEADS * QK_PAD), BF16),
                         pl.BlockSpec((tm_do, N_HEADS * QK_PAD), lambda i, j, k: (i, 0))),))
    dq, dk, dv = _flash_bwd(q, k, v, do_ext, lse)

    def unrope(dq_, dk_, dv_, cs_, sa_, sb_):
        dq_, dk_ = dq_.astype(F32), dk_.astype(F32)
        qs, kn = [], []
        dkr = jnp.zeros_like(cs_)
        for hd in range(N_HEADS):
            qs.append(dq_[:, hd * QK_PAD:hd * QK_PAD + NOPE] * SCALE)
            qs.append(_rope_t(dq_[:, hd * QK_PAD + NOPE:(hd + 1) * QK_PAD] * SCALE, cs_, sa_, sb_))
            kn.append(dk_[:, hd * QK_PAD:hd * QK_PAD + NOPE] * K_UNSCALE)
            dkr = dkr + dk_[:, hd * QK_PAD + NOPE:(hd + 1) * QK_PAD]
        return (jnp.concatenate(qs, axis=1), jnp.concatenate(kn + [dv_], axis=1),
                _rope_t(dkr * K_UNSCALE, cs_, sa_, sb_))

    dq_pre, dkv, dkr = _rowwise(
        unrope, [(dq, 2048, 0), (dk, 2048, 0), (dv, 1024, 0), (cs, 128, 0), (sa, 128, 0), (sb, 128, 0)], [],
        [(2048, BF16), (2048, BF16), (128, F32)], name="unrope")
    g_w_uq = _mm(cqn, dq_pre, ta=True, name="mm_dw_uq")
    dcqn = _mm(dq_pre, w_uq, tb=True, out_dtype=BF16, name="mm_dcqn")
    g_w_ukv = _mm(ckvn, dkv, ta=True, name="mm_dw_ukv")
    mid_token = send_mid_grads({
        "w_uq": g_w_uq.reshape(Q_RANK, N_HEADS, QK_PAD)[:, :, :NOPE + ROPE].reshape(Q_RANK, 1536),
        "w_uk": g_w_ukv[:, :1024], "w_uv": g_w_ukv[:, 1024:]})
    dckvn = _mm(dkv, w_ukv, tb=True, out_dtype=BF16, name="mm_dckvn", after=[mid_token])

    def small_bwd(zs, dcq_, dckv_, dkr_, qn, kvn):
        dcq, dqn = _rms_bwd(zs[:, :Q_RANK], qn, dcq_)
        dckv, dkvn = _rms_bwd(zs[:, Q_RANK:Q_RANK + KV_RANK], kvn, dckv_)
        pad = jnp.zeros((zs.shape[0], D - 768), F32)
        return jnp.concatenate([dcq, dckv, dkr_, pad], axis=1), dqn, dkvn

    dz, g_q_norm, g_kv_norm = _rowwise(
        small_bwd, [(z, D, Z_S), (dcqn, Q_RANK, 0), (dckvn, KV_RANK, 0), (dkr, 128, 0)],
        [vec["q_norm"], vec["kv_norm"]], [(D, dz, Z_S)], [(1, Q_RANK), (1, KV_RANK)], name="small_bwd")

    def ln_silu_bwd(d3, u, g, b):
        nh, rs, y = ln_parts(u, g, b)
        sg = _sigmoid(y)
        dyv = d3 * (sg * (1.0 + y * (1.0 - sg)))
        dnh = dyv * g
        du = rs * (dnh - jnp.mean(dnh, axis=-1, keepdims=True)
                   - nh * jnp.mean(dnh * nh, axis=-1, keepdims=True))
        return (du, jnp.sum(dyv * nh, axis=0, keepdims=True), jnp.sum(dyv, axis=0, keepdims=True),
                jnp.sum(du, axis=0, keepdims=True))

    du1, g_ln_g, g_ln_b, g_conv_b = LW.bwd(
        dyc, "w_pw2", name="mm_du3", tm=512, epilogue=ln_silu_bwd, out_dtypes=(F32,), reds=(D, D, D),
        extras=[u1, vec["conv_ln_g"], vec["conv_ln_b"]])
    dz, g_conv_w = _conv_bwd(z, du1, cw, dz)
    small_token = send_small_grads({
        "conv_b": g_conv_b, "conv_ln_g": g_ln_g, "conv_ln_b": g_ln_b, "b_pw2": g_b_pw2,
        "norm_mix_post": g_norm_mix_post, "norm_mlp_pre": g_norm_mlp_pre, "norm_mlp_post": g_norm_mlp_post,
        "q_norm": g_q_norm, "kv_norm": g_kv_norm}, g_conv_w, loss)
    g_w_in_t = _mm(dz, h, ta=True, name="mm_dw_in", after=[small_token])
    early_token = send_early_grads(g_w_in_t)

    def final(dh_, dx1_, x, g):
        d, dg = _rms_bwd(x, g, dh_)
        return dx1_ + d, dg

    grad_x, g_norm_mix_pre = _mm(
        dz, w_in_t, name="mm_dh", after=[early_token], tiles=(_tile(xs.shape[0], 512), D, Z_COLS),
        b_spec=pl.BlockSpec((Z_COLS, D), lambda i, j, k: (0, 0), pipeline_mode=pl.Buffered(1)),
        epilogue=final, out_dtypes=(F32,), reds=(D,), extras=[dx1, xs, vec["norm_mix_pre"]])

    return grad_x, g_norm_mix_pre


def _row1024(a):
    return jnp.pad(a, ((0, 0), (0, 1024 - a.shape[1])))


BIG = ("w_in",) + tuple(n for n, _ in MID) + tuple(n for n, _, _ in LATE)
ORDER = ("norm_mix_pre", "w_in", "q_norm", "w_uq", "kv_norm", "w_uk", "w_uv", "w_o_attn", "conv_w",
         "conv_b", "conv_ln_g", "conv_ln_b", "w_pw2", "b_pw2", "w_out", "norm_mix_post", "norm_mlp_pre",
         "w_ff1", "w_ff2", "norm_mlp_post")


def kernel(x, positions, norm_mix_pre, w_in, q_norm, w_uq, kv_norm, w_uk, w_uv, w_o_attn, conv_w, conv_b, conv_ln_g, conv_ln_b, w_pw2, b_pw2, w_out, norm_mix_post, norm_mlp_pre, w_ff1, w_ff2, norm_mlp_post, loss_target, m_norm_mix_pre, m_w_in, m_q_norm, m_w_uq, m_kv_norm, m_w_uk, m_w_uv, m_w_o_attn, m_conv_w, m_conv_b, m_conv_ln_g, m_conv_ln_b, m_w_pw2, m_b_pw2, m_w_out, m_norm_mix_post, m_norm_mlp_pre, m_w_ff1, m_w_ff2, m_norm_mlp_post, v_norm_mix_pre, v_w_in, v_q_norm, v_w_uq, v_kv_norm, v_w_uk, v_w_uv, v_w_o_attn, v_conv_w, v_conv_b, v_conv_ln_g, v_conv_ln_b, v_w_pw2, v_b_pw2, v_w_out, v_norm_mix_post, v_norm_mlp_pre, v_w_ff1, v_w_ff2, v_norm_mlp_post):
    args = dict(locals())
    wts = {n: args[n] for n in ORDER}
    mom = {n: args["m_" + n] for n in ORDER}
    var = {n: args["v_" + n] for n in ORDER}
    T = x.shape[1]
    place = jnp.stack([2 * lax.axis_index("x") + lax.axis_index("y"), lax.axis_index("c")]).astype(jnp.int32)

    w_early = _pack_early_shards(wts)
    w_mid = _pack_mid_shards(wts)
    w_late = _pack_late_shards(wts)
    cw_own = jnp.pad(conv_w.reshape(CONV_W, -1), ((0, CONV_TAPS_PAD - CONV_W), (0, 0)))
    full, cw_all = _gather_weights(w_early, cw_own)
    full = lax.dynamic_update_slice(full, w_early[None], (place[0], 0, 0))
    cw_all = lax.dynamic_update_slice(cw_all, cw_own[None], (place[0], 0, 0))
    w_in_t = _unpack_early(full)
    cw = cw_all.transpose(1, 0, 2).reshape(CONV_TAPS_PAD, D_MODEL)
    mid_sent = _late_gather_start(w_mid, full, name="mid_gather_start")
    late_sent = _late_gather_start(w_late, mid_sent[3], name="late_gather_start")
    vec = {n: wts[n] for n in SMALL_VECS}
    vec["norm_mix_pre"] = vec["norm_mix_pre"] + late_sent[3][:1, :1]

    def landed_weights(started, own, after, name):
        sems, w_thru, land, _ = started
        landed = _late_gather_wait(sems, w_thru, land, after, name=name)
        return lax.dynamic_update_slice(landed, own[None], (place[0], 0, 0))

    def mid_weights(after):
        return _unpack_mid(landed_weights(mid_sent, w_mid, after, "mid_gather_wait"))

    def late_weights(after):
        return _LateWeights(landed_weights(late_sent, w_late, after, "late_gather_wait"))

    sent = {}

    def send_mid_grads(g):
        g32 = _pack_mid_grads(g)
        sent["mid"] = (g32,) + _reduce_start(g32.astype(BF16), place, name="mid_grads_start")
        return sent["mid"][4]

    def send_late_grads(bufs):
        sent["late"] = (bufs[0],) + _reduce_start(bufs[1], place, name="late_grads_start")
        return sent["late"][4]

    def send_small_grads(gs, g_conv_w, loss):
        rows = [_row1024(gs[n]) if n in gs else jnp.zeros((1, 1024), F32) for n in SMALL_VECS]
        svec = jnp.concatenate(rows + [_row1024(loss), jnp.zeros((SMALL_CONVW_ROW - SMALL_LOSS_ROW - 1, 1024), F32),
                                       g_conv_w], axis=0)
        sent["small"] = (svec,) + _reduce_start(svec, place, name="small_grads_start", whole=True)
        return sent["small"][4]

    def send_early_grads(g):
        g32 = _pack_early_grads(g)
        sent["early"] = (g32,) + _reduce_start(g32.astype(BF16), place, name="early_grads_start")
        return sent["early"][4]

    late_bufs = (lax.empty((N_CHIPS, LATE_ROWS, 1024), F32), lax.empty((N_CHIPS, LATE_ROWS, 1024), BF16))
    grad_x, g_norm_mix_pre = _local_step(x[0], positions.reshape(T, 1), loss_target[0], w_in_t, mid_weights,
                                         late_weights, late_bufs, send_mid_grads, send_late_grads,
                                         send_small_grads, send_early_grads, cw, vec)
    last = jnp.pad(g_norm_mix_pre, ((0, 7), (0, 0)))
    sent["last"] = (last,) + _reduce_start(last, place, name="last_grads_start", whole=True)

    def finish(group, after):
        g32, sems_, g_thru, land_, _ = sent[group]
        landed = _reduce_wait(sems_, g_thru, land_, after, name=group + "_grads_wait")
        red = _reduce_sum(place, g32, landed, name=group + "_grads_sum")
        return _join_halves(red, name=group + "_grads_join")

    def as2d(a):
        return a.reshape(-1, a.shape[-1]) if a.ndim != 4 or a.shape[2] == 1 else a.reshape(a.shape[1], -1)

    grads, delta, new_m, new_v = {}, {}, {}, {}

    def update(n):
        shp = wts[n].shape
        outs = _adamw(as2d(wts[n]), as2d(grads[n]), as2d(mom[n]), as2d(var[n]), name="adamw_" + n)
        delta[n], new_m[n], new_v[n] = (o.reshape(shp) for o in outs)

    g_mid, off = finish("mid", [sent["early"][4]]), 0
    for n, rows in MID:
        grads[n] = g_mid[off:off + rows].reshape(wts[n].shape)
        off += rows
        update(n)
    g_late = finish("late", [delta["w_uv"]])
    for n, off, rows in LATE:
        shp = wts[n].shape
        outs = _adamw(as2d(wts[n]), g_late, as2d(mom[n]), as2d(var[n]), g_row=off, name="adamw_" + n)
        delta[n], new_m[n], new_v[n], grads[n] = (o.reshape(shp) for o in outs)

    g_early = finish("early", [grad_x, delta["w_ff2"]])
    outs = _adamw(wts["w_in"][0].T, g_early[:1200], mom["w_in"][0].T, var["w_in"][0].T, name="adamw_w_in")
    delta["w_in"], new_m["w_in"], new_v["w_in"] = (o.T[None] for o in outs)
    grads["w_in"] = g_early[:1200].T[None]

    def finish_small(group, after):
        svec, sems_, s_thru, land_, _ = sent[group]
        landed = _reduce_wait(sems_, s_thru, land_, after, name=group + "_grads_wait", whole=True)
        return _sum_small((2 * place[0] + place[1])[None], svec, landed, name=group + "_grads_sum")

    ssum = finish_small("small", [delta["w_in"]])
    for r, n in enumerate(SMALL_VECS):
        grads[n] = ssum[r:r + 1, :wts[n].shape[1]]
    grads["norm_mix_pre"] = finish_small("last", [ssum])[:1]
    cw_sum = ssum[SMALL_CONVW_ROW:SMALL_CONVW_ROW + CONV_W]
    grads["conv_w"] = lax.dynamic_slice_in_dim(cw_sum, place[0] * 256, 256, axis=1).reshape(conv_w.shape)
    loss_out = ssum[SMALL_LOSS_ROW, 0]
    update("conv_w")
    small = [jnp.concatenate([_row1024(t[n]) for n in SMALL_VECS], axis=0) for t in (wts, grads, mom, var)]
    outs = _adamw(*small, name="adamw_small")
    for r, n in enumerate(SMALL_VECS):
        delta[n], new_m[n], new_v[n] = (o[r:r + 1, :wts[n].shape[1]] for o in outs)

    return (loss_out, grad_x[None], *[grads[n] for n in ORDER], *[delta[n] for n in ORDER],
            *[new_m[n] for n in ORDER], *[new_v[n] for n in ORDER])
```

```python
import functools

import numpy as np
import jax
import jax.numpy as jnp
from jax import lax
from jax.experimental import pallas as pl
from jax.experimental.pallas import tpu as pltpu

F32 = jnp.float32
BF16 = jnp.bfloat16
MESH = pl.DeviceIdType.MESH

D_MODEL = 1024
N_HEADS = 8
NOPE = 128
ROPE = 64
QK_PAD = 256
Q_RANK = 384
KV_RANK = 256
CONV_W = 31
CONV_TAPS_PAD = 32
HALO = 32
D_FF = 4096
EPS = 1e-6
ROPE_THETA = 10000.0
SCALE = float((NOPE + ROPE) ** -0.5)
Q_SCALE = SCALE * float(np.log2(np.e))
K_UNSCALE = float(np.log(2.0))
Z_A, Z_B, Z_GA, Z_GC, Z_S = range(5)
Z_COLS = 5 * D_MODEL
N_SMALL = Q_RANK + KV_RANK + ROPE
N_CHIPS = 4
N_DEV = 8

ADAM_LR = 0.001
ADAM_B1 = 0.9
ADAM_B2 = 0.999
ADAM_EPS = 1e-08
ADAM_WD = 0.01
ADAM_STEP = 10

VMEM_LIMIT = 48 * 1024 * 1024
NEG = -1e30

EARLY_ROWS = 1280
MID = (("w_uq", 144), ("w_uk", 64), ("w_uv", 64))
MID_ROWS = 512
LATE = (("w_ff1", 0, 1024), ("w_o_attn", 1024, 256), ("w_pw2", 1536, 256), ("w_out", 1792, 256),
        ("w_ff2", 2048, 1024))
LATE_OFF = {n: off for n, off, _ in LATE}
LATE_ROWS = 3072
SUM_TILE = 128

SMALL_VECS = ("norm_mix_pre", "conv_b", "conv_ln_g", "conv_ln_b", "b_pw2", "norm_mix_post",
              "norm_mlp_pre", "norm_mlp_post", "q_norm", "kv_norm")
SMALL_LOSS_ROW = 10
SMALL_CONVW_ROW = 16
SMALL_ROWS = 48


ANY = pl.BlockSpec(memory_space=pl.ANY)


def _params(*sem):
    return pltpu.CompilerParams(dimension_semantics=sem, vmem_limit_bytes=VMEM_LIMIT)


def _tile(n, cap):
    if n <= cap:
        return n
    t = (cap // 128) * 128
    while n % t:
        t -= 128
    return t


def _row_tile(n, cap):
    if n <= cap:
        return n
    t = (cap // 8) * 8
    while n % t:
        t -= 8
    return t


MM_VMEM_BUDGET = 40 * 1024 * 1024


def _mm_tiles(M, N, K, in_bytes, out_bytes):
    tn = _tile(N, 1024)
    tk = K
    while True:
        for cap in (1024, 512, 256):
            tm = _tile(M, cap)
            need = 2 * tk * (tm * in_bytes[0] + tn * in_bytes[1]) + 2 * tm * tn * out_bytes
            need += tm * tn * 4 if tk < K else 0
            if need <= MM_VMEM_BUDGET:
                return tm, tn, tk
        tk = _tile(K, tk - 128)


def _mm(a, b, *, name, ta=False, tb=False, out_dtype=F32, extras=(), epilogue=None, out_dtypes=None,
        after=(), shape=None, tiles=None, b_spec=None, reds=(), k_slabs=False):
    if shape is None:
        M, K = (a.shape[1], a.shape[0]) if ta else a.shape
        N = b.shape[0] if tb else b.shape[1]
        assert K == (b.shape[1] if tb else b.shape[0]), (a.shape, b.shape, ta, tb)
    else:
        M, N, K = shape
    outs = tuple(out_dtypes or (out_dtype,))
    placed = [o for o in outs if isinstance(o, tuple)]
    ex_arrays = [e[0] if isinstance(e, tuple) else e for e in extras]
    if tiles is None:
        out_bytes = sum(jnp.dtype(d).itemsize for d in outs) + sum(e.dtype.itemsize for e in ex_arrays)
        tiles = _mm_tiles(M, N, K, (a.dtype.itemsize, b.dtype.itemsize), out_bytes)
    tm, tn, tk = tiles
    nk = K // tk
    dims = (((0 if ta else 1,), (1 if tb else 0,)), ((), ()))
    n_ex, n_out = len(extras), len(outs)
    n_pass = len(after) + len(placed)
    assert not reds or tn == N

    def body(*refs):
        a_ref, b_ref = refs[:2]
        ex_refs = refs[2:2 + n_ex]
        o_refs = refs[2 + n_ex + n_pass:2 + n_ex + n_pass + n_out]
        red_refs = refs[2 + n_ex + n_pass + n_out:2 + n_ex + n_pass + n_out + len(reds)]
        first_rows, k = pl.program_id(0) == 0, pl.program_id(2)
        bv = b_ref[...].astype(BF16)
        if k_slabs:
            kc = bv.shape[-1]
            part = sum(lax.dot_general(a_ref[:, p * kc:(p + 1) * kc].astype(BF16), bv[p], dims,
                                       preferred_element_type=F32) for p in range(bv.shape[0]))
        else:
            if bv.ndim == 3:
                bv = bv.reshape(-1, bv.shape[-1])
            part = lax.dot_general(a_ref[...].astype(BF16), bv, dims, preferred_element_type=F32)

        def finish(acc):
            ex = [r[...] for r in ex_refs]
            ex = [e.astype(F32) if e.dtype == BF16 else e for e in ex]
            res = epilogue(acc, *ex) if epilogue else (acc,) * n_out
            for o_ref, r in zip(o_refs, res):
                o_ref[...] = r.astype(o_ref.dtype)
            for red_ref, r in zip(red_refs, res[n_out:]):
                red_ref[...] += r

        if reds:
            @pl.when(first_rows & (k == 0))
            def _():
                for red_ref in red_refs:
                    red_ref[...] = jnp.zeros_like(red_ref)

        if nk == 1:
            finish(part)
        else:
            acc_ref = refs[-1]

            @pl.when(k == 0)
            def _():
                acc_ref[...] = part

            @pl.when((k > 0) & (k < nk - 1))
            def _():
                acc_ref[...] += part

            @pl.when(k == nk - 1)
            def _():
                finish(acc_ref[...] + part)

    a_spec = (pl.BlockSpec((tk, tm), lambda i, j, k: (k, i)) if ta
              else pl.BlockSpec((tm, tk), lambda i, j, k: (i, k)))
    if b_spec is None:
        b_spec = (pl.BlockSpec((tn, tk), lambda i, j, k: (j, k)) if tb
                  else pl.BlockSpec((tk, tn), lambda i, j, k: (k, j)))
    o_spec = pl.BlockSpec((tm, tn), lambda i, j, k: (i, j))
    row_spec = pl.BlockSpec((1, tn), lambda i, j, k: (0, j))
    ex_specs = []
    for e in extras:
        if isinstance(e, tuple):
            ex_specs.append(pl.BlockSpec((tm, tn), functools.partial(lambda i, j, k, cb: (i, cb), cb=e[1])))
        else:
            ex_specs.append(row_spec if e.shape[0] == 1 else o_spec)
    out_shape, out_specs, aliases = [], [], {}
    for o, out in enumerate(outs):
        if isinstance(out, tuple):
            out_shape.append(jax.ShapeDtypeStruct(out[0].shape, out[0].dtype))
            out_specs.append(out[1])
            aliases[2 + n_ex + len(after) + len(aliases)] = o
        else:
            out_shape.append(jax.ShapeDtypeStruct((M, N), out))
            out_specs.append(o_spec)
    passed = list(after) + [buf for buf, _ in placed]
    out_shape = out_shape + [jax.ShapeDtypeStruct((1, w), F32) for w in reds]
    out_specs = out_specs + [pl.BlockSpec((1, w), lambda i, j, k: (0, 0)) for w in reds]
    res = pl.pallas_call(
        body, name=name, out_shape=out_shape,
        grid=(M // tm, N // tn, nk),
        in_specs=[a_spec, b_spec] + ex_specs + [ANY] * len(passed),
        out_specs=out_specs, input_output_aliases=aliases,
        scratch_shapes=[pltpu.VMEM((tm, tn), F32)] if nk > 1 else [],
        compiler_params=_params(*(("arbitrary",) * 3 if reds else ("parallel", "parallel", "arbitrary"))),
    )(a, b, *ex_arrays, *passed)
    return res[0] if len(res) == 1 else res


class _LateWeights:
    def __init__(self, pack_small, mlp_pack):
        self._small, self._mlp_pack, self._mlp = pack_small, mlp_pack, None

    def _pack_of(self, n, x):
        if n not in ("w_ff1", "w_ff2"):
            return self._small
        if self._mlp is None:
            self._mlp = self._mlp_pack(x)
        return self._mlp

    def fwd(self, x, n, *, name, tm=1024, **kw):
        T, off = x.shape[0], LATE_OFF[n]
        self.pack = self._pack_of(n, x)
        if n == "w_ff1":
            return _mm(x, self.pack, name=name, shape=(T, 4096, 1024), tiles=(_tile(T, tm), 1024, 1024),
                       b_spec=pl.BlockSpec((None, 1024, 1024), lambda i, j, k: (j, 0, 0)), **kw)
        if n == "w_ff2":
            view = self.pack.reshape(N_CHIPS, LATE_ROWS // 1024, 1024, 1024)
            return _mm(x, view, name=name, shape=(T, 1024, 4096), tiles=(_tile(T, min(tm, 512)), 1024, 4096),
                       b_spec=pl.BlockSpec((N_CHIPS, None, 1024, 1024), lambda i, j, k: (0, off // 1024, 0, 0)),
                       **kw)
        return _mm(x, self.pack.reshape(N_CHIPS, LATE_ROWS // 256, 256, 1024), name=name,
                   shape=(T, 1024, 1024), tiles=(_tile(T, tm), 1024, 1024), b_spec=self._whole(off), **kw)

    @staticmethod
    def _whole(off):
        return pl.BlockSpec((N_CHIPS, None, 256, 1024), lambda i, j, k: (0, off // 256, 0, 0))

    def bwd(self, dy, n, *, name, tm=1024, **kw):
        T, off = dy.shape[0], LATE_OFF[n]
        tm = _tile(T, tm)
        self.pack = self._pack_of(n, dy)
        if n == "w_ff1":
            view = self.pack.reshape(N_CHIPS, LATE_ROWS // 1024, 1024, 1024)
            return _mm(dy, view, tb=True, name=name, shape=(T, 1024, 4096), tiles=(tm, 1024, 4096),
                       b_spec=pl.BlockSpec((N_CHIPS, None, 1024, 1024), lambda i, j, k: (0, off // 1024, 0, 0),
                                           pipeline_mode=pl.Buffered(1)), k_slabs=True, **kw)
        if n == "w_ff2":
            return _mm(dy, self.pack, tb=True, name=name, shape=(T, 4096, 1024), tiles=(tm, 1024, 1024),
                       b_spec=pl.BlockSpec((None, 1024, 1024), lambda i, j, k: (j, off // 1024, 0)), **kw)
        return _mm(dy, self.pack.reshape(N_CHIPS, LATE_ROWS // 256, 256, 1024), tb=True, name=name,
                   shape=(T, 1024, 1024), tiles=(tm, 1024, 1024), b_spec=self._whole(off), **kw)


def _late_grad(x, dy, n, bufs, *, name):
    T, off = x.shape[0], LATE_OFF[n]
    if n == "w_ff1":
        spec = pl.BlockSpec((None, 512, 1024), lambda i, j, k: (j, i, 0))
        shape, tiles = (1024, 4096, T), (512, 1024, T)
    elif n == "w_ff2":
        spec = pl.BlockSpec((None, 512, 1024), lambda i, j, k: (i // 2, off // 512 + i % 2, 0))
        shape, tiles = (4096, 1024, T), (512, 1024, T)
    else:
        spec = pl.BlockSpec((None, 256, 1024), lambda i, j, k: (i, off // 256, 0))
        shape, tiles = (1024, 1024, T), (256, 1024, T)
    return _mm(x, dy, ta=True, name=name, shape=shape, tiles=tiles, out_dtypes=[(b, spec) for b in bufs])


def _rowwise(fn, rows, consts, outs, reds=(), *, name, tm=256, after=()):
    T = rows[0][0].shape[0]
    tm = min(tm, T)
    n_row, n_const, n_out = len(rows), len(consts), len(outs)
    placed = [o for o in outs if len(o) == 3]
    n_pass = len(after) + len(placed)

    def body(*refs):
        i = pl.program_id(0)
        vals = [r[...] for r in refs[:n_row + n_const]]
        vals = [v.astype(F32) if v.dtype == BF16 else v for v in vals]
        res = fn(*vals)
        out_refs = refs[n_row + n_const + n_pass:]
        for k in range(n_out):
            out_refs[k][...] = res[k].astype(out_refs[k].dtype)
        for k in range(len(reds)):
            ref = out_refs[n_out + k]

            @pl.when(i == 0)
            def _(ref=ref):
                ref[...] = jnp.zeros_like(ref)

            ref[...] += res[n_out + k]

    in_specs = [pl.BlockSpec((tm, w), functools.partial(lambda i, cb: (i, cb), cb=cb))
                for (_, w, cb) in rows]
    in_specs += [pl.BlockSpec(c.shape, lambda i: (0, 0)) for c in consts]
    in_specs += [ANY] * n_pass
    out_specs, out_shape, aliases = [], [], {}
    for o, out in enumerate(outs):
        if len(out) == 3:
            w, buf, cb = out
            out_specs.append(pl.BlockSpec((tm, w), functools.partial(lambda i, cb: (i, cb), cb=cb)))
            out_shape.append(jax.ShapeDtypeStruct(buf.shape, buf.dtype))
            aliases[n_row + n_const + len(after) + len(aliases)] = o
        else:
            out_specs.append(pl.BlockSpec((tm, out[0]), lambda i: (i, 0)))
            out_shape.append(jax.ShapeDtypeStruct((T, out[0]), out[1]))
    out_specs += [pl.BlockSpec(s, lambda i: (0, 0)) for s in reds]
    out_shape += [jax.ShapeDtypeStruct(s, F32) for s in reds]
    return pl.pallas_call(
        body, name=name, out_shape=out_shape, grid=(T // tm,),
        in_specs=in_specs, out_specs=out_specs, input_output_aliases=aliases,
        compiler_params=_params("arbitrary"),
    )(*[r[0] for r in rows], *consts, *after, *[o[1] for o in placed])


def _rms(x, g):
    r = lax.rsqrt(jnp.mean(x * x, axis=-1, keepdims=True) + EPS)
    return x * r * g


def _rms_bwd(x, g, dy):
    r = lax.rsqrt(jnp.mean(x * x, axis=-1, keepdims=True) + EPS)
    n = x * r
    dyg = dy * g
    dx = r * (dyg - n * jnp.mean(dyg * n, axis=-1, keepdims=True))
    return dx, jnp.sum(dy * n, axis=0, keepdims=True)


def _sigmoid(x):
    return 1.0 / (1.0 + jnp.exp(-x))


def _rope(x, cs, sa, sb):
    return x * cs + pltpu.roll(x, 32, 1) * sa + pltpu.roll(x, 96, 1) * sb


def _rope_t(dy, cs, sa, sb):
    return dy * cs + pltpu.roll(dy * sa, 96, 1) + pltpu.roll(dy * sb, 32, 1)


def _causal_mask(t):
    row = lax.broadcasted_iota(jnp.int32, (t, t), 0)
    col = lax.broadcasted_iota(jnp.int32, (t, t), 1)
    return col <= row


FLASH_FWD_TQ = 4096
FLASH_FWD_TK = 1024
FLASH_FWD_HEADS = 1
FLASH_BWD_TQ = 1024
FLASH_BWD_TK = 512
NT_DIMS = (((1,), (1,)), ((), ()))
TN_DIMS = (((0,), (0,)), ((), ()))


def _flash_fwd(q, k, v, *, tq=FLASH_FWD_TQ, tk=FLASH_FWD_TK, hp=FLASH_FWD_HEADS):
    T = q.shape[0]
    tq, tk = min(tq, T), min(tk, T)
    n_diag = tq // tk
    assert n_diag >= 1 and N_HEADS % hp == 0
    heads = range(hp)

    def body(q_ref, k_ref, v_ref, o_ref, lse_ref):
        i = pl.program_id(1)

        def cols(ref, rows, h, width=QK_PAD):
            return ref[rows, h * width:(h + 1) * width]

        def logits(j, r0=0):
            sl = pl.ds(pl.multiple_of(j * tk, tk), tk)
            return tuple(lax.dot_general(cols(q_ref, slice(r0, tq), h), cols(k_ref, sl, h), NT_DIMS,
                                         preferred_element_type=F32) for h in heads)

        def update(j, s, m, acc, masked, r0=0):
            sl = pl.ds(pl.multiple_of(j * tk, tk), tk)
            m_out, acc_out = [], []
            for h in heads:
                sh = s[h]
                if masked:
                    row = i * tq + r0 + lax.broadcasted_iota(jnp.int32, (tq - r0, tk), 0)
                    col = j * tk + lax.broadcasted_iota(jnp.int32, (tq - r0, tk), 1)
                    sh = jnp.where(col <= row, sh, NEG)
                m_old, acc_old = m[h][r0:], acc[h][r0:]
                m_new = jnp.maximum(m_old, jnp.max(sh, axis=-1, keepdims=True))
                p = jnp.exp2(sh - m_new)
                acc_new = (jnp.exp2(m_old - m_new) * acc_old
                           + jnp.dot(p.astype(BF16), cols(v_ref, sl, h), preferred_element_type=F32))
                if r0:
                    m_new = jnp.concatenate([m[h][:r0], m_new], axis=0)
                    acc_new = jnp.concatenate([acc[h][:r0], acc_new], axis=0)
                m_out.append(m_new)
                acc_out.append(acc_new)
            return tuple(m_out), tuple(acc_out)

        def step(j, carry):
            s, m, acc = carry
            s_next = logits(j + 1)
            m, acc = update(j, s, m, acc, False)
            return s_next, m, acc

        first = i * n_diag
        init = (logits(0), tuple(jnp.full((tq, 1), NEG, F32) for _ in heads),
                tuple(jnp.zeros((tq, QK_PAD), F32) for _ in heads))
        s, m, acc = lax.fori_loop(0, first, step, init)
        for d in range(n_diag):
            m, acc = update(first + d, s, m, acc, True, r0=d * tk)
            if d + 1 < n_diag:
                s = logits(first + d + 1, r0=(d + 1) * tk)
        for h in heads:
            l = acc[h][:, NOPE:]
            o_ref[:, h * NOPE:(h + 1) * NOPE] = (acc[h][:, :NOPE] / l).astype(BF16)
            lse_ref[:, h * NOPE:(h + 1) * NOPE] = m[h] + jnp.log2(l)

    return pl.pallas_call(
        body, name="flash_fwd",
        out_shape=[jax.ShapeDtypeStruct((T, N_HEADS * NOPE), BF16),
                   jax.ShapeDtypeStruct((T, N_HEADS * NOPE), F32)],
        grid=(N_HEADS // hp, T // tq),
        in_specs=[pl.BlockSpec((tq, hp * QK_PAD), lambda h, i: (i, h)),
                  pl.BlockSpec((T, hp * QK_PAD), lambda h, i: (0, h)),
                  pl.BlockSpec((T, hp * QK_PAD), lambda h, i: (0, h))],
        out_specs=[pl.BlockSpec((tq, hp * NOPE), lambda h, i: (i, h)),
                   pl.BlockSpec((tq, hp * NOPE), lambda h, i: (i, h))],
        compiler_params=_params("parallel", "arbitrary"),
    )(q, k, v)


def _flash_bwd(q, k, v, do, lse, *, tq=FLASH_BWD_TQ, tk=FLASH_BWD_TK):
    T = q.shape[0]
    tq, tk = min(tq, T), min(tk, T)

    def body(q_ref, k_ref, v_ref, do_ref, lse_ref, dq_ref, dk_ref, dv_ref, dq_acc):
        dq_acc[...] = jnp.zeros_like(dq_acc)
        for j in range(T // tk):
            k0 = j * tk
            kj = k_ref[k0:k0 + tk, :]
            vj = v_ref[k0:k0 + tk, :]
            dk = jnp.zeros((tk, QK_PAD), F32)
            dv = jnp.zeros((tk, NOPE), F32)
            for i in range(k0 // tq, T // tq):
                r0 = max(i * tq, k0)
                r1 = (i + 1) * tq
                qi = q_ref[r0:r1, :]
                doi = do_ref[r0:r1, :]
                s = lax.dot_general(qi, kj, NT_DIMS, preferred_element_type=F32)
                p = jnp.exp2(s - lse_ref[r0:r1, :][:, :1])
                if r0 < k0 + tk:
                    row = r0 + lax.broadcasted_iota(jnp.int32, (r1 - r0, tk), 0)
                    col = k0 + lax.broadcasted_iota(jnp.int32, (r1 - r0, tk), 1)
                    p = jnp.where(col <= row, p, 0.0)
                dv = dv + lax.dot_general(p.astype(BF16), doi[:, :NOPE], TN_DIMS, preferred_element_type=F32)
                ds = (p * lax.dot_general(doi, vj, NT_DIMS, preferred_element_type=F32)).astype(BF16)
                dk = dk + lax.dot_general(ds, qi, TN_DIMS, preferred_element_type=F32)
                dq_acc[r0:r1, :] += jnp.dot(ds, kj, preferred_element_type=F32)
            dk_ref[k0:k0 + tk, :] = dk.astype(BF16)
            dv_ref[k0:k0 + tk, :] = dv.astype(BF16)
        dq_ref[...] = dq_acc[...].astype(BF16)

    return pl.pallas_call(
        body, name="flash_bwd",
        out_shape=[jax.ShapeDtypeStruct((T, N_HEADS * QK_PAD), BF16),
                   jax.ShapeDtypeStruct((T, N_HEADS * QK_PAD), BF16),
                   jax.ShapeDtypeStruct((T, N_HEADS * NOPE), BF16)],
        grid=(N_HEADS,),
        in_specs=[pl.BlockSpec((T, QK_PAD), lambda h: (0, h)),
                  pl.BlockSpec((T, QK_PAD), lambda h: (0, h)),
                  pl.BlockSpec((T, QK_PAD), lambda h: (0, h)),
                  pl.BlockSpec((T, QK_PAD), lambda h: (0, h)),
                  pl.BlockSpec((T, NOPE), lambda h: (0, h))],
        out_specs=[pl.BlockSpec((T, QK_PAD), lambda h: (0, h)),
                   pl.BlockSpec((T, QK_PAD), lambda h: (0, h)),
                   pl.BlockSpec((T, NOPE), lambda h: (0, h))],
        scratch_shapes=[pltpu.VMEM((T, QK_PAD), F32)],
        compiler_params=_params("arbitrary"),
    )(q, k, v, do, lse)


SUB = 128


def _windows(buf, t0, c0, first, last):
    span = SUB + 8 * ((last + 7) // 8)
    base = buf[t0:t0 + span, c0:c0 + SUB]
    for r in range(8):
        offs = [o for o in range(first, last + 1) if o % 8 == r]
        if offs:
            win = base if r == 0 else pltpu.roll(base, span - r, 0)
            for o in offs:
                yield o, win[o - r:o - r + SUB, :]


def _glu(a, b):
    return a.astype(F32) * _sigmoid(b.astype(F32))


def _conv_fwd(z, w, bias, *, tm=256):
    T, C = z.shape[0], D_MODEL
    tm = min(tm, T)
    hb = tm // HALO
    lead = HALO - (CONV_W - 1)

    def body(a_ref, b_ref, ah_ref, bh_ref, w_ref, bias_ref, o_ref, buf):
        t = pl.program_id(0)
        buf[0:HALO, :] = jnp.where(t > 0, _glu(ah_ref[...], bh_ref[...]), 0.0)
        buf[HALO:, :] = _glu(a_ref[...], b_ref[...])
        for c0 in range(0, C, SUB):
            for t0 in range(0, tm, SUB):
                acc = jnp.broadcast_to(bias_ref[:, c0:c0 + SUB], (SUB, SUB))
                for o, win in _windows(buf, t0, c0, lead, lead + CONV_W - 1):
                    acc = acc + win * w_ref[o - lead:o - lead + 1, c0:c0 + SUB]
                o_ref[t0:t0 + SUB, c0:c0 + SUB] = acc

    def halo(cb):
        return pl.BlockSpec((HALO, C), lambda t: (jnp.maximum(t * hb - 1, 0), cb))

    return pl.pallas_call(
        body, name="conv_fwd",
        out_shape=jax.ShapeDtypeStruct((T, C), F32),
        grid=(T // tm,),
        in_specs=[pl.BlockSpec((tm, C), lambda t: (t, Z_A)), pl.BlockSpec((tm, C), lambda t: (t, Z_B)),
                  halo(Z_A), halo(Z_B),
                  pl.BlockSpec((CONV_TAPS_PAD, C), lambda t: (0, 0)), pl.BlockSpec((1, C), lambda t: (0, 0))],
        out_specs=pl.BlockSpec((tm, C), lambda t: (t, 0)),
        scratch_shapes=[pltpu.VMEM((HALO + tm, C), F32)],
        compiler_params=_params("parallel"),
    )(z, z, z, z, w, bias)


def _conv_bwd(z, du1, w, dz, *, tm=256):
    T, C = z.shape[0], D_MODEL
    tm = min(tm, T)
    hb = tm // HALO
    nt = T // tm
    last_halo = T // HALO - 1
    lead = HALO - (CONV_W - 1)
    assert (Z_A, Z_B) == (0, 1)

    def body(a_ref, b_ref, ah_ref, bh_ref, d_ref, dh_ref, w_ref, dz_in, dz_ref, dw_ref, ubuf, dbuf, dw8, gate):
        t = pl.program_id(0)
        ubuf[0:HALO, :] = jnp.where(t > 0, _glu(ah_ref[...], bh_ref[...]), 0.0)
        gate[...] = _sigmoid(b_ref[...].astype(F32))
        ubuf[HALO:, :] = a_ref[...].astype(F32) * gate[...]
        dbuf[0:tm, :] = d_ref[...]
        dbuf[tm:, :] = jnp.where(t < nt - 1, dh_ref[...], 0.0)

        @pl.when(t == 0)
        def _():
            dw8[...] = jnp.zeros_like(dw8)

        for c0 in range(0, C, SUB):
            for t0 in range(0, tm, SUB):
                acc = jnp.zeros((SUB, SUB), F32)
                for o, win in _windows(dbuf, t0, c0, 0, CONV_W - 1):
                    kk = CONV_W - 1 - o
                    acc = acc + win * w_ref[kk:kk + 1, c0:c0 + SUB]
                av = a_ref[t0:t0 + SUB, c0:c0 + SUB].astype(F32)
                sg = gate[t0:t0 + SUB, c0:c0 + SUB]
                dz_ref[t0:t0 + SUB, c0:c0 + SUB] = (acc * sg).astype(dz_ref.dtype)
                dz_ref[t0:t0 + SUB, C + c0:C + c0 + SUB] = (acc * av * sg * (1.0 - sg)).astype(dz_ref.dtype)
                dsub = d_ref[t0:t0 + SUB, c0:c0 + SUB]
                for o, win in _windows(ubuf, t0, c0, lead, lead + CONV_W - 1):
                    kk = o - lead
                    dw8[8 * kk:8 * kk + 8, c0:c0 + SUB] += jnp.sum(
                        (dsub * win).reshape(SUB // 8, 8, SUB), axis=0)

        @pl.when(t == nt - 1)
        def _():
            dw_ref[...] = jnp.sum(dw8[...].reshape(CONV_TAPS_PAD, 8, C), axis=1)

    def halo(cb):
        return pl.BlockSpec((HALO, C), lambda t: (jnp.maximum(t * hb - 1, 0), cb))

    return pl.pallas_call(
        body, name="conv_bwd",
        out_shape=[jax.ShapeDtypeStruct(dz.shape, dz.dtype), jax.ShapeDtypeStruct((CONV_TAPS_PAD, C), F32)],
        grid=(nt,),
        in_specs=[pl.BlockSpec((tm, C), lambda t: (t, Z_A)), pl.BlockSpec((tm, C), lambda t: (t, Z_B)),
                  halo(Z_A), halo(Z_B),
                  pl.BlockSpec((tm, C), lambda t: (t, 0)),
                  pl.BlockSpec((HALO, C), lambda t: (jnp.minimum((t + 1) * hb, last_halo), 0)),
                  pl.BlockSpec((CONV_TAPS_PAD, C), lambda t: (0, 0)), ANY],
        out_specs=[pl.BlockSpec((tm, 2 * C), lambda t: (t, 0)),
                   pl.BlockSpec((CONV_TAPS_PAD, C), lambda t: (0, 0))],
        input_output_aliases={7: 0},
        scratch_shapes=[pltpu.VMEM((HALO + tm, C), F32), pltpu.VMEM((tm + HALO, C), F32),
                        pltpu.VMEM((CONV_TAPS_PAD * 8, C), F32), pltpu.VMEM((tm, C), F32)],
        compiler_params=_params("arbitrary"),
    )(z, z, z, z, du1, du1, w, dz)


def _adamw(w, g, m, v, *, name, g_row=None):
    R, C = w.shape
    tr = _row_tile(R, max(8, (1 << 18) // C // 8 * 8))
    first = 0 if g_row is None else g_row // tr
    assert g_row is None or g_row % tr == 0

    def body(w_ref, g_ref, m_ref, v_ref, d_ref, nm_ref, nv_ref, *g_out):
        gv = g_ref[...]
        nm = ADAM_B1 * m_ref[...] + (1.0 - ADAM_B1) * gv
        nv = ADAM_B2 * v_ref[...] + (1.0 - ADAM_B2) * (gv * gv)
        m_hat = nm / (1.0 - ADAM_B1 ** ADAM_STEP)
        v_hat = nv / (1.0 - ADAM_B2 ** ADAM_STEP)
        d_ref[...] = -ADAM_LR * (m_hat / (jnp.sqrt(v_hat) + ADAM_EPS) + ADAM_WD * w_ref[...])
        nm_ref[...] = nm
        nv_ref[...] = nv
        for ref in g_out:
            ref[...] = gv

    spec = pl.BlockSpec((tr, C), lambda i: (i, 0))
    g_spec = pl.BlockSpec((tr, C), lambda i: (first + i, 0))
    n_out = 3 if g_row is None else 4
    return pl.pallas_call(
        body, name=name, out_shape=[jax.ShapeDtypeStruct((R, C), F32)] * n_out,
        grid=(R // tr,), in_specs=[spec, g_spec, spec, spec], out_specs=[spec] * n_out,
        compiler_params=_params("parallel"),
    )(w, g, m, v)


def _coords():
    return lax.axis_index("x"), lax.axis_index("y"), lax.axis_index("c")


def _remote(src, dst, ssem, rsem, to):
    return pltpu.make_async_remote_copy(src_ref=src, dst_ref=dst, send_sem=ssem, recv_sem=rsem,
                                        device_id=to, device_id_type=MESH)


def _half(c, rows):
    return pl.ds(pl.multiple_of(c * (rows // 2), 16), rows // 2)


def _gather_weights(wpack, cw):
    def body(w_ref, cw_ref, full_ref, cwf_ref, ssem, rsem):
        x, y, c = _coords()
        q = 2 * x + y
        sib = (x, y, 1 - c)
        chips = [(1 - x, y), (x, 1 - y), (1 - x, 1 - y)]
        mine, other = _half(c, w_ref.shape[0]), _half(1 - c, w_ref.shape[0])
        first = []
        for j, (px, py) in enumerate(chips):
            first.append(_remote(w_ref.at[mine], full_ref.at[q, mine], ssem.at[j], rsem.at[j], (px, py, c)))
            first.append(_remote(cw_ref, cwf_ref.at[q], ssem.at[6 + j], rsem.at[6 + j], (px, py, c)))
        for cp in first:
            cp.start()
        passed = []
        for j, (px, py) in enumerate(chips):
            pq = 2 * px + py
            _remote(w_ref.at[mine], full_ref.at[pq, mine], ssem.at[j], rsem.at[j], (px, py, c)).wait_recv()
            fw = _remote(full_ref.at[pq, mine], full_ref.at[pq, mine], ssem.at[3 + j], rsem.at[3 + j], sib)
            fw.start()
            passed.append(fw)
        for j, (px, py) in enumerate(chips):
            pq = 2 * px + py
            _remote(full_ref.at[pq, other], full_ref.at[pq, other], ssem.at[3 + j], rsem.at[3 + j], sib).wait_recv()
            _remote(cw_ref, cwf_ref.at[pq], ssem.at[6 + j], rsem.at[6 + j], (px, py, c)).wait_recv()
        for cp in first + passed:
            cp.wait_send()

    return pl.pallas_call(
        body, name="gather_weights",
        out_shape=[jax.ShapeDtypeStruct((N_CHIPS,) + wpack.shape, wpack.dtype),
                   jax.ShapeDtypeStruct((N_CHIPS,) + cw.shape, cw.dtype)],
        in_specs=[ANY, ANY], out_specs=[ANY, ANY],
        scratch_shapes=[pltpu.SemaphoreType.DMA((9,)), pltpu.SemaphoreType.DMA((9,))],
    )(wpack, cw)


SEM = pl.BlockSpec(memory_space=pltpu.SEMAPHORE)
HBM = pl.BlockSpec(memory_space=pltpu.HBM)
N_LATE = 6


def _late_peers(x, y, c):
    out = []
    for j, (px, py) in enumerate([(1 - x, y), (x, 1 - y), (1 - x, 1 - y)]):
        for t in (0, 1):
            out.append((2 * j + t, px, py, c if t == 0 else 1 - c))
    return out


def _late_gather_start(w, after, *, name, part=None):
    rows = w.shape[0]
    part = part or (lambda c: _half(c, rows))

    def body(w_ref, land_ref, after_ref, *outs):
        sems = outs[:2 * N_LATE]
        token = outs[2 * N_LATE + 2]
        x, y, c = _coords()
        q = 2 * x + y
        mine = part(c)
        for k, px, py, pc in _late_peers(x, y, c):
            _remote(w_ref.at[mine], land_ref.at[q, mine], sems[k], sems[N_LATE + k], (px, py, pc)).start()
        token[...] = jnp.zeros_like(token)

    res = pl.pallas_call(
        body, name=name,
        out_shape=tuple([pltpu.SemaphoreType.DMA(())] * (2 * N_LATE)) + (
            pltpu.HBM(w.shape, w.dtype), pltpu.HBM((N_CHIPS,) + w.shape, w.dtype),
            jax.ShapeDtypeStruct((8, 128), F32)),
        in_specs=(HBM, HBM, ANY),
        out_specs=tuple([SEM] * (2 * N_LATE)) + (HBM, HBM, pl.BlockSpec(memory_space=pltpu.VMEM)),
        input_output_aliases={0: 2 * N_LATE, 1: 2 * N_LATE + 1},
        compiler_params=pltpu.CompilerParams(has_side_effects=pltpu.SideEffectType.DATAFLOW_SIDE_EFFECTING),
    )(pltpu.with_memory_space_constraint(w, pltpu.HBM),
      pltpu.with_memory_space_constraint(lax.empty((N_CHIPS,) + w.shape, w.dtype), pltpu.HBM), after)
    return res[:2 * N_LATE], res[2 * N_LATE], res[2 * N_LATE + 1], res[2 * N_LATE + 2]


def _late_gather_wait(sems, w_thru, land_thru, after, *, name, part=None):
    rows = w_thru.shape[0]
    part = part or (lambda c: _half(c, rows))

    def body(w_ref, land_ref, *rest):
        sems = rest[:2 * N_LATE]
        x, y, c = _coords()
        for k, px, py, pc in _late_peers(x, y, c):
            cp = _remote(w_ref.at[part(pc)], land_ref.at[2 * px + py, part(pc)],
                         sems[k], sems[N_LATE + k], (px, py, pc))
            cp.wait_send()
            cp.wait_recv()

    return pl.pallas_call(
        body, name=name,
        out_shape=(pltpu.HBM(w_thru.shape, w_thru.dtype), pltpu.HBM(land_thru.shape, land_thru.dtype)),
        in_specs=(HBM, HBM) + tuple([SEM] * (2 * N_LATE)) + (ANY,),
        out_specs=(HBM, HBM), input_output_aliases={0: 0, 1: 1},
        compiler_params=pltpu.CompilerParams(has_side_effects=pltpu.SideEffectType.DATAFLOW_SIDE_EFFECTING),
    )(w_thru, land_thru, *sems, after)[1]


N_PEERS = N_DEV - 1


def _peers(x, y, c):
    out = []
    for r in range(1, N_DEV):
        out.append((r - 1, 1 - x if r & 4 else x, 1 - y if r & 2 else y, 1 - c if r & 1 else c))
    return out


def _piece(g_ref, px, py, pc, whole):
    return g_ref if whole else g_ref.at[2 * px + py, _half(pc, g_ref.shape[1])]


def _reduce_start(g16, after, *, name, whole=False):
    def body(g_ref, land_ref, after_ref, *outs):
        sems = outs[:2 * N_PEERS]
        token = outs[2 * N_PEERS + 2]
        x, y, c = _coords()
        for k, px, py, pc in _peers(x, y, c):
            _remote(_piece(g_ref, px, py, pc, whole), land_ref.at[k], sems[k], sems[N_PEERS + k],
                    (px, py, pc)).start()
        token[...] = jnp.zeros_like(token)

    land_shape = (N_PEERS,) + (g16.shape if whole else (g16.shape[1] // 2, 1024))
    res = pl.pallas_call(
        body, name=name,
        out_shape=tuple([pltpu.SemaphoreType.DMA(())] * (2 * N_PEERS)) + (
            pltpu.HBM(g16.shape, g16.dtype), pltpu.HBM(land_shape, g16.dtype),
            jax.ShapeDtypeStruct((8, 128), F32)),
        in_specs=(HBM, HBM, ANY),
        out_specs=tuple([SEM] * (2 * N_PEERS)) + (HBM, HBM, pl.BlockSpec(memory_space=pltpu.VMEM)),
        input_output_aliases={0: 2 * N_PEERS, 1: 2 * N_PEERS + 1},
        compiler_params=pltpu.CompilerParams(has_side_effects=pltpu.SideEffectType.DATAFLOW_SIDE_EFFECTING),
    )(pltpu.with_memory_space_constraint(g16, pltpu.HBM),
      pltpu.with_memory_space_constraint(lax.empty(land_shape, g16.dtype), pltpu.HBM), after)
    return res[:2 * N_PEERS], res[2 * N_PEERS], res[2 * N_PEERS + 1], res[2 * N_PEERS + 2]


def _reduce_wait(sems, g_thru, land_thru, after, *, name, whole=False):
    def body(g_ref, land_ref, *rest):
        sems = rest[:2 * N_PEERS]
        x, y, c = _coords()
        for k, px, py, pc in _peers(x, y, c):
            cp = _remote(_piece(g_ref, px, py, pc, whole), land_ref.at[k], sems[k], sems[N_PEERS + k],
                         (px, py, pc))
            cp.wait_send()
            cp.wait_recv()

    return pl.pallas_call(
        body, name=name,
        out_shape=(pltpu.HBM(g_thru.shape, g_thru.dtype), pltpu.HBM(land_thru.shape, land_thru.dtype)),
        in_specs=(HBM, HBM) + tuple([SEM] * (2 * N_PEERS)) + tuple([ANY] * len(after)),
        out_specs=(HBM, HBM), input_output_aliases={0: 0, 1: 1},
        compiler_params=pltpu.CompilerParams(has_side_effects=pltpu.SideEffectType.DATAFLOW_SIDE_EFFECTING),
    )(g_thru, land_thru, *sems, *after)[1]


def _reduce_sum(place, g32, land, *, name):
    rows = g32.shape[1]
    nb = rows // 2 // SUM_TILE

    def body(place_ref, g_ref, l_ref, o_ref):
        acc = g_ref[...]
        for k in range(N_PEERS):
            acc = acc + l_ref[k].astype(F32)
        o_ref[...] = acc

    return pl.pallas_call(
        body, name=name,
        out_shape=jax.ShapeDtypeStruct((rows, 1024), F32),
        grid_spec=pltpu.PrefetchScalarGridSpec(
            num_scalar_prefetch=1, grid=(nb,),
            in_specs=[pl.BlockSpec((None, SUM_TILE, 1024), lambda i, s: (s[0], s[1] * nb + i, 0)),
                      pl.BlockSpec((N_PEERS, SUM_TILE, 1024), lambda i, s: (0, i, 0))],
            out_specs=pl.BlockSpec((SUM_TILE, 1024), lambda i, s: (s[1] * nb + i, 0))),
        compiler_params=_params("parallel"),
    )(place, g32, land)


def _join_halves(red, *, name):
    rows = red.shape[0]

    def body(r_ref, o_ref, ssem, rsem):
        x, y, c = _coords()
        cp = _remote(r_ref.at[_half(c, rows)], o_ref.at[_half(c, rows)], ssem, rsem, (x, y, 1 - c))
        cp.start()
        _remote(r_ref.at[_half(1 - c, rows)], o_ref.at[_half(1 - c, rows)], ssem, rsem,
                (x, y, 1 - c)).wait_recv()
        cp.wait_send()

    return pl.pallas_call(
        body, name=name,
        out_shape=jax.ShapeDtypeStruct(red.shape, F32),
        in_specs=[ANY], out_specs=ANY, input_output_aliases={0: 0},
        scratch_shapes=[pltpu.SemaphoreType.DMA, pltpu.SemaphoreType.DMA],
    )(red)


def _sum_small(me, svec, land, *, name):
    def body(me_ref, s_ref, l_ref, o_ref):
        mine = me_ref[0]
        acc = None
        for d in range(N_DEV):
            theirs = l_ref[jnp.maximum(jnp.bitwise_xor(mine, d) - 1, 0)]
            v = jnp.where(mine == d, s_ref[...], theirs)
            acc = v if acc is None else acc + v
        o_ref[...] = acc

    return pl.pallas_call(
        body, name=name,
        out_shape=jax.ShapeDtypeStruct(svec.shape, F32),
        grid_spec=pltpu.PrefetchScalarGridSpec(
            num_scalar_prefetch=1, grid=(1,),
            in_specs=[pl.BlockSpec(svec.shape, lambda i, s: (0, 0)),
                      pl.BlockSpec(land.shape, lambda i, s: (0, 0, 0))],
            out_specs=pl.BlockSpec(svec.shape, lambda i, s: (0, 0))),
        compiler_params=_params("arbitrary"),
    )(me, svec, land)


def _pack_early_shards(wts):
    return jnp.concatenate([wts["w_in"][0].T.astype(BF16), jnp.zeros((EARLY_ROWS - 1200, 1024), BF16)], axis=0)


def _pack_mid_shards(wts):
    parts = [wts[n].astype(BF16).reshape(-1, 1024) for n, _ in MID]
    return jnp.concatenate(parts + [jnp.zeros((MID_ROWS - sum(r for _, r in MID), 1024), BF16)], axis=0)


def _pack_late_shards(wts):
    parts, at = [], 0
    for n, off, rows in LATE:
        if off > at:
            parts.append(jnp.zeros((off - at, 1024), BF16))
        parts.append(wts[n].astype(BF16).reshape(rows, 1024))
        at = off + rows
    return jnp.concatenate(parts, axis=0)


def _unpack_early(full):
    return jnp.concatenate(
        [full[0, N_SMALL:1200]] + [full[p, :1200] for p in range(1, N_CHIPS)]
        + [full[0, :N_SMALL], jnp.zeros((D_MODEL - N_SMALL, 1024), full.dtype)], axis=0)


def _unpack_mid(full):
    return {"w_uq": full[:, :144].reshape(4, 384, 384).transpose(1, 0, 2).reshape(384, 1536),
            "w_uk": full[:, 144:208].reshape(256, 1024),
            "w_uv": full[:, 208:272].reshape(256, 1024)}


def _pack_early_grads(gt):
    rest = 1200 - N_SMALL
    w_in = jnp.stack([jnp.concatenate([gt[4 * D_MODEL:4 * D_MODEL + N_SMALL], gt[:rest]], axis=0)]
                     + [gt[rest + 1200 * (p - 1):rest + 1200 * p] for p in range(1, N_CHIPS)])
    return jnp.concatenate([w_in, jnp.zeros((4, EARLY_ROWS - 1200, 1024), F32)], axis=1)


def _pack_mid_grads(g):
    parts = [
        g["w_uq"].reshape(384, 4, 384).transpose(1, 0, 2).reshape(4, 144, 1024),
        g["w_uk"].reshape(4, 64, 1024),
        g["w_uv"].reshape(4, 64, 1024),
        jnp.zeros((4, MID_ROWS - sum(r for _, r in MID), 1024), F32),
    ]
    return jnp.concatenate(parts, axis=1)


def _pad_w_uq(w):
    w = w.reshape(Q_RANK, N_HEADS, NOPE + ROPE)
    return jnp.pad(w, ((0, 0), (0, 0), (0, QK_PAD - NOPE - ROPE))).reshape(Q_RANK, N_HEADS * QK_PAD)


def _local_step(xs, pos, tgt, w_in_t, mid_weights, late_weights, late_bufs, send_mid_grads, send_late_grads,
                send_small_grads, send_early_grads, cw, vec):
    D = D_MODEL
    inv = ROPE_THETA ** (-jnp.arange(0, ROPE, 2, dtype=F32) / ROPE)
    inv_freq = jnp.concatenate([inv, inv, jnp.zeros((128 - ROPE,), F32)])[None, :]

    (h,) = _rowwise(lambda x, g: (_rms(x, g),), [(xs, D, 0)], [vec["norm_mix_pre"]], [(D, BF16)],
                    name="rms_pre")
    z = _mm(h, w_in_t, tb=True, out_dtype=BF16, name="mm_in")

    def qkv_prep(zs, p, qn, kvn, invf):
        ang = p.astype(F32) * invf
        cosv, sinv = jnp.cos(ang), jnp.sin(ang)
        ln = lax.broadcasted_iota(jnp.int32, ang.shape, 1)
        cs = jnp.where(ln < ROPE, cosv, 0.0)
        sa = jnp.where((ln >= ROPE // 2) & (ln < ROPE), sinv, 0.0)
        sb = jnp.where(ln < ROPE // 2, -sinv, 0.0)
        cqn = _rms(zs[:, :Q_RANK], qn)
        ckvn = _rms(zs[:, Q_RANK:Q_RANK + KV_RANK], kvn)
        kr = _rope(zs[:, 640:768], cs, sa, sb)
        return cqn, ckvn, kr, cs, sa, sb

    cqn, ckvn, krot, cs, sa, sb = _rowwise(
        qkv_prep, [(z, D, Z_S), (pos, 1, 0)], [vec["q_norm"], vec["kv_norm"], inv_freq],
        [(Q_RANK, BF16), (KV_RANK, BF16), (128, BF16), (128, F32), (128, F32), (128, F32)],
        name="qkv_prep")
    W = mid_weights(z)
    w_uq = _pad_w_uq(W["w_uq"])
    w_ukv = jnp.concatenate([W["w_uk"], W["w_uv"]], axis=1)
    q_pre = _mm(cqn, w_uq, out_dtype=BF16, name="mm_uq")
    kv = _mm(ckvn, w_ukv, out_dtype=BF16, name="mm_ukv")

    def qk_rope(qp, kn, vv, kr, cs_, sa_, sb_):
        qs, ks, vs = [], [], []
        ones = jnp.ones((qp.shape[0], NOPE), F32)
        for hd in range(N_HEADS):
            qs.append(qp[:, hd * QK_PAD:hd * QK_PAD + NOPE])
            qs.append(_rope(qp[:, hd * QK_PAD + NOPE:(hd + 1) * QK_PAD], cs_, sa_, sb_))
            ks.append(kn[:, hd * NOPE:(hd + 1) * NOPE])
            ks.append(kr.astype(F32))
            vs.append(vv[:, hd * NOPE:(hd + 1) * NOPE])
            vs.append(ones)
        return (jnp.concatenate(qs, axis=1) * Q_SCALE, jnp.concatenate(ks, axis=1),
                jnp.concatenate(vs, axis=1))

    q, k, v = _rowwise(
        qk_rope, [(q_pre, 2048, 0), (kv, 1024, 0), (kv, 1024, 1), (krot, 128, 0), (cs, 128, 0),
                  (sa, 128, 0), (sb, 128, 0)], [],
        [(2048, BF16), (2048, BF16), (2048, BF16)], name="qk_rope")
    attn, lse = _flash_fwd(q, k, v)
    LW = late_weights(attn)
    y_attn = LW.fwd(attn, "w_o_attn", out_dtype=BF16, name="mm_o_attn")

    u1 = _conv_fwd(z, cw, vec["conv_b"])

    def ln_parts(u, g, b):
        mu = jnp.mean(u, axis=-1, keepdims=True)
        xc = u - mu
        rs = lax.rsqrt(jnp.mean(xc * xc, axis=-1, keepdims=True) + EPS)
        nh = xc * rs
        return nh, rs, nh * g + b

    def ln_silu(u, g, b):
        y = ln_parts(u, g, b)[2]
        return (y * _sigmoid(y),)

    (u3,) = _rowwise(ln_silu, [(u1, D, 0)], [vec["conv_ln_g"], vec["conv_ln_b"]], [(D, BF16)],
                     name="ln_silu")

    def merge(yc, ga, gc, ya, b2):
        return yc, _sigmoid(ga) * ya + _sigmoid(gc) * (yc + b2)

    y_conv, merged = LW.fwd(u3, "w_pw2", name="mm_pw2", tm=512, epilogue=merge, out_dtypes=(BF16, BF16),
                            extras=[(z, Z_GA), (z, Z_GC), y_attn, vec["b_pw2"]])

    def post1(m, x, g1, g2):
        x1 = x + _rms(m, g1)
        return m, x1, _rms(x1, g2)

    mo, x1, h2 = LW.fwd(merged, "w_out", name="mm_out", tm=512, epilogue=post1, out_dtypes=(F32, F32, BF16),
                        extras=[xs, vec["norm_mix_post"], vec["norm_mlp_pre"]])

    def sqrelu(acc):
        r = jnp.maximum(acc, 0.0)
        return r * r, r

    a2, r1 = LW.fwd(h2, "w_ff1", name="mm_ff1", epilogue=sqrelu, out_dtypes=(BF16, BF16))

    def post2(f_, x1_, t_, g):
        e = x1_ + _rms(f_, g) - t_
        dy = e * (1.0 / D)
        df, dg = _rms_bwd(f_, g, dy)
        loss = jnp.broadcast_to(jnp.sum(e * e, keepdims=True) * (0.5 / D), (1, 128))
        return dy, df, dg, loss

    dy, df, g_norm_mlp_post, loss = LW.fwd(a2, "w_ff2", name="mm_ff2", epilogue=post2, out_dtypes=(F32, BF16),
                                           extras=[x1, tgt, vec["norm_mlp_post"]], reds=(D, 128))
    late_bufs = _late_grad(a2, df, "w_ff2", late_bufs, name="mm_dw_ff2")
    df1 = LW.bwd(df, "w_ff2", name="mm_df1", extras=[r1], out_dtypes=(BF16,),
              epilogue=lambda acc, r: (acc * (2.0 * r.astype(F32)),))
    late_bufs = _late_grad(h2, df1, "w_ff1", late_bufs, name="mm_dw_ff1")

    def bwd_mid(dh2_, dy_, x1_, m, g2, g1):
        d1, dg2 = _rms_bwd(x1_, g2, dh2_)
        dx1_ = dy_ + d1
        dm, dg1 = _rms_bwd(m, g1, dx1_)
        return dx1_, dm, dg2, dg1

    dx1, dmo, g_norm_mlp_pre, g_norm_mix_post = LW.bwd(
        df1, "w_ff1", name="mm_dh2", tm=512, epilogue=bwd_mid, out_dtypes=(F32, BF16), reds=(D, D),
        extras=[dy, x1, mo, vec["norm_mlp_pre"], vec["norm_mix_post"]])
    late_bufs = _late_grad(merged, dmo, "w_out", late_bufs, name="mm_dw_out")
    def dmerge(dm, ga, gc, ya, yc, b2):
        sga, sgc = _sigmoid(ga), _sigmoid(gc)
        dya = dm * sga
        dyc = dm * sgc
        dga = dm * ya * sga * (1.0 - sga)
        dgc = dm * (yc + b2) * sgc * (1.0 - sgc)
        return dya, dyc, jnp.concatenate([dga, dgc], axis=1), jnp.sum(dyc, axis=0, keepdims=True)

    dz = lax.empty((xs.shape[0], Z_COLS), BF16)
    tm_dm = _tile(xs.shape[0], 512)
    gates = pl.BlockSpec((tm_dm, 2 * D), lambda i, j, k: (i, Z_GA // 2))
    dya, dyc, dz, g_b_pw2 = LW.bwd(
        dmo, "w_out", name="mm_dmerged", tm=tm_dm, epilogue=dmerge, out_dtypes=(BF16, BF16, (dz, gates)),
        reds=(D,), extras=[(z, Z_GA), (z, Z_GC), y_attn, y_conv, vec["b_pw2"]])

    late_bufs = _late_grad(attn, dya, "w_o_attn", late_bufs, name="mm_dw_o_attn")
    late_bufs = _late_grad(u3, dyc, "w_pw2", late_bufs, name="mm_dw_pw2")
    late_token = send_late_grads(late_bufs)

    def delta_fn(acc, o_):
        do32 = acc.astype(BF16).astype(F32)
        pr = do32 * o_
        ln = lax.broadcasted_iota(jnp.int32, (pr.shape[0], NOPE), 1)
        cols = []
        for hd in range(N_HEADS):
            dl = jnp.sum(pr[:, hd * NOPE:(hd + 1) * NOPE], axis=-1, keepdims=True)
            hi = dl.astype(BF16).astype(F32)
            cols.append(do32[:, hd * NOPE:(hd + 1) * NOPE])
            cols.append(jnp.where(ln == 0, -hi, jnp.where(ln == 1, hi - dl, 0.0)))
        return (jnp.concatenate(cols, axis=1),)

    tm_do = _tile(xs.shape[0], 512)
    do_ext = LW.bwd(dya, "w_o_attn", name="mm_dattn", tm=tm_do, epilogue=delta_fn, extras=[attn],
                    after=[late_token], out_dtypes=(
                        (lax.empty((xs.shape[0], N_HEADS * QK_PAD), BF16),
                         pl.BlockSpec((tm_do, N_HEADS * QK_PAD), lambda i, j, k: (i, 0))),))
    dq, dk, dv = _flash_bwd(q, k, v, do_ext, lse)

    def unrope(dq_, dk_, dv_, cs_, sa_, sb_):
        dq_, dk_ = dq_.astype(F32), dk_.astype(F32)
        qs, kn = [], []
        dkr = jnp.zeros_like(cs_)
        for hd in range(N_HEADS):
            qs.append(dq_[:, hd * QK_PAD:hd * QK_PAD + NOPE] * SCALE)
            qs.append(_rope_t(dq_[:, hd * QK_PAD + NOPE:(hd + 1) * QK_PAD] * SCALE, cs_, sa_, sb_))
            kn.append(dk_[:, hd * QK_PAD:hd * QK_PAD + NOPE] * K_UNSCALE)
            dkr = dkr + dk_[:, hd * QK_PAD + NOPE:(hd + 1) * QK_PAD]
        return (jnp.concatenate(qs, axis=1), jnp.concatenate(kn + [dv_], axis=1),
                _rope_t(dkr * K_UNSCALE, cs_, sa_, sb_))

    dq_pre, dkv, dkr = _rowwise(
        unrope, [(dq, 2048, 0), (dk, 2048, 0), (dv, 1024, 0), (cs, 128, 0), (sa, 128, 0), (sb, 128, 0)], [],
        [(2048, BF16), (2048, BF16), (128, F32)], name="unrope")
    g_w_uq = _mm(cqn, dq_pre, ta=True, name="mm_dw_uq")
    dcqn = _mm(dq_pre, w_uq, tb=True, out_dtype=BF16, name="mm_dcqn")
    g_w_ukv = _mm(ckvn, dkv, ta=True, name="mm_dw_ukv")
    mid_token = send_mid_grads({
        "w_uq": g_w_uq.reshape(Q_RANK, N_HEADS, QK_PAD)[:, :, :NOPE + ROPE].reshape(Q_RANK, 1536),
        "w_uk": g_w_ukv[:, :1024], "w_uv": g_w_ukv[:, 1024:]})
    dckvn = _mm(dkv, w_ukv, tb=True, out_dtype=BF16, name="mm_dckvn", after=[mid_token])

    def small_bwd(zs, dcq_, dckv_, dkr_, qn, kvn):
        dcq, dqn = _rms_bwd(zs[:, :Q_RANK], qn, dcq_)
        dckv, dkvn = _rms_bwd(zs[:, Q_RANK:Q_RANK + KV_RANK], kvn, dckv_)
        pad = jnp.zeros((zs.shape[0], D - 768), F32)
        return jnp.concatenate([dcq, dckv, dkr_, pad], axis=1), dqn, dkvn

    dz, g_q_norm, g_kv_norm = _rowwise(
        small_bwd, [(z, D, Z_S), (dcqn, Q_RANK, 0), (dckvn, KV_RANK, 0), (dkr, 128, 0)],
        [vec["q_norm"], vec["kv_norm"]], [(D, dz, Z_S)], [(1, Q_RANK), (1, KV_RANK)], name="small_bwd")

    def ln_silu_bwd(d3, u, g, b):
        nh, rs, y = ln_parts(u, g, b)
        sg = _sigmoid(y)
        dyv = d3 * (sg * (1.0 + y * (1.0 - sg)))
        dnh = dyv * g
        du = rs * (dnh - jnp.mean(dnh, axis=-1, keepdims=True)
                   - nh * jnp.mean(dnh * nh, axis=-1, keepdims=True))
        return (du, jnp.sum(dyv * nh, axis=0, keepdims=True), jnp.sum(dyv, axis=0, keepdims=True),
                jnp.sum(du, axis=0, keepdims=True))

    du1, g_ln_g, g_ln_b, g_conv_b = LW.bwd(
        dyc, "w_pw2", name="mm_du3", tm=512, epilogue=ln_silu_bwd, out_dtypes=(F32,), reds=(D, D, D),
        extras=[u1, vec["conv_ln_g"], vec["conv_ln_b"]])
    dz, g_conv_w = _conv_bwd(z, du1, cw, dz)
    small_token = send_small_grads({
        "conv_b": g_conv_b, "conv_ln_g": g_ln_g, "conv_ln_b": g_ln_b, "b_pw2": g_b_pw2,
        "norm_mix_post": g_norm_mix_post, "norm_mlp_pre": g_norm_mlp_pre, "norm_mlp_post": g_norm_mlp_post,
        "q_norm": g_q_norm, "kv_norm": g_kv_norm}, g_conv_w, loss)
    g_w_in_t = _mm(dz, h, ta=True, name="mm_dw_in", after=[small_token])
    early_token = send_early_grads(g_w_in_t)

    def final(dh_, dx1_, x, g):
        d, dg = _rms_bwd(x, g, dh_)
        return dx1_ + d, dg

    grad_x, g_norm_mix_pre = _mm(
        dz, w_in_t, name="mm_dh", after=[early_token], tiles=(_tile(xs.shape[0], 512), D, Z_COLS),
        b_spec=pl.BlockSpec((Z_COLS, D), lambda i, j, k: (0, 0), pipeline_mode=pl.Buffered(1)),
        epilogue=final, out_dtypes=(F32,), reds=(D,), extras=[dx1, xs, vec["norm_mix_pre"]])

    return grad_x, g_norm_mix_pre


def _row1024(a):
    return jnp.pad(a, ((0, 0), (0, 1024 - a.shape[1])))


BIG = ("w_in",) + tuple(n for n, _ in MID) + tuple(n for n, _, _ in LATE)
ORDER = ("norm_mix_pre", "w_in", "q_norm", "w_uq", "kv_norm", "w_uk", "w_uv", "w_o_attn", "conv_w",
         "conv_b", "conv_ln_g", "conv_ln_b", "w_pw2", "b_pw2", "w_out", "norm_mix_post", "norm_mlp_pre",
         "w_ff1", "w_ff2", "norm_mlp_post")


def kernel(x, positions, norm_mix_pre, w_in, q_norm, w_uq, kv_norm, w_uk, w_uv, w_o_attn, conv_w, conv_b, conv_ln_g, conv_ln_b, w_pw2, b_pw2, w_out, norm_mix_post, norm_mlp_pre, w_ff1, w_ff2, norm_mlp_post, loss_target, m_norm_mix_pre, m_w_in, m_q_norm, m_w_uq, m_kv_norm, m_w_uk, m_w_uv, m_w_o_attn, m_conv_w, m_conv_b, m_conv_ln_g, m_conv_ln_b, m_w_pw2, m_b_pw2, m_w_out, m_norm_mix_post, m_norm_mlp_pre, m_w_ff1, m_w_ff2, m_norm_mlp_post, v_norm_mix_pre, v_w_in, v_q_norm, v_w_uq, v_kv_norm, v_w_uk, v_w_uv, v_w_o_attn, v_conv_w, v_conv_b, v_conv_ln_g, v_conv_ln_b, v_w_pw2, v_b_pw2, v_w_out, v_norm_mix_post, v_norm_mlp_pre, v_w_ff1, v_w_ff2, v_norm_mlp_post):
    args = dict(locals())
    wts = {n: args[n] for n in ORDER}
    mom = {n: args["m_" + n] for n in ORDER}
    var = {n: args["v_" + n] for n in ORDER}
    T = x.shape[1]
    place = jnp.stack([2 * lax.axis_index("x") + lax.axis_index("y"), lax.axis_index("c")]).astype(jnp.int32)

    w_early = _pack_early_shards(wts)
    w_mid = _pack_mid_shards(wts)
    w_late = _pack_late_shards(wts)
    cw_own = jnp.pad(conv_w.reshape(CONV_W, -1), ((0, CONV_TAPS_PAD - CONV_W), (0, 0)))
    full, cw_all = _gather_weights(w_early, cw_own)
    full = lax.dynamic_update_slice(full, w_early[None], (place[0], 0, 0))
    cw_all = lax.dynamic_update_slice(cw_all, cw_own[None], (place[0], 0, 0))
    w_in_t = _unpack_early(full)
    cw = cw_all.transpose(1, 0, 2).reshape(CONV_TAPS_PAD, D_MODEL)
    def small_part(c):
        return pl.ds(pl.multiple_of(LATE_OFF["w_o_attn"] + (LATE_ROWS // 2 - LATE_OFF["w_o_attn"]) * c, 16),
                     LATE_ROWS // 2 - LATE_OFF["w_o_attn"])

    def mlp_part(c):
        return pl.ds(pl.multiple_of(LATE_OFF["w_ff2"] * c, 16), 1024)

    mid_sent = _late_gather_start(w_mid, full, name="mid_gather_start")
    late_sent = _late_gather_start(w_late, mid_sent[3], name="late_gather_start", part=small_part)
    mlp_sent = _late_gather_start(w_late, late_sent[3], name="mlp_gather_start", part=mlp_part)
    vec = {n: wts[n] for n in SMALL_VECS}
    vec["norm_mix_pre"] = vec["norm_mix_pre"] + mlp_sent[3][:1, :1]

    def landed_weights(started, own, after, name, part=None):
        sems, w_thru, land, _ = started
        landed = _late_gather_wait(sems, w_thru, land, after, name=name, part=part)
        return lax.dynamic_update_slice(landed, own[None], (place[0], 0, 0))

    def mid_weights(after):
        return _unpack_mid(landed_weights(mid_sent, w_mid, after, "mid_gather_wait"))

    def late_weights(after):
        return _LateWeights(
            landed_weights(late_sent, w_late, after, "late_gather_wait", small_part),
            lambda after_: landed_weights(mlp_sent, w_late, after_, "mlp_gather_wait", mlp_part))

    sent = {}

    def send_mid_grads(g):
        g32 = _pack_mid_grads(g)
        sent["mid"] = (g32,) + _reduce_start(g32.astype(BF16), place, name="mid_grads_start")
        return sent["mid"][4]

    def send_late_grads(bufs):
        sent["late"] = (bufs[0],) + _reduce_start(bufs[1], place, name="late_grads_start")
        return sent["late"][4]

    def send_small_grads(gs, g_conv_w, loss):
        rows = [_row1024(gs[n]) if n in gs else jnp.zeros((1, 1024), F32) for n in SMALL_VECS]
        svec = jnp.concatenate(rows + [_row1024(loss), jnp.zeros((SMALL_CONVW_ROW - SMALL_LOSS_ROW - 1, 1024), F32),
                                       g_conv_w], axis=0)
        sent["small"] = (svec,) + _reduce_start(svec, place, name="small_grads_start", whole=True)
        return sent["small"][4]

    def send_early_grads(g):
        g32 = _pack_early_grads(g)
        sent["early"] = (g32,) + _reduce_start(g32.astype(BF16), place, name="early_grads_start")
        return sent["early"][4]

    late_bufs = (lax.empty((N_CHIPS, LATE_ROWS, 1024), F32), lax.empty((N_CHIPS, LATE_ROWS, 1024), BF16))
    grad_x, g_norm_mix_pre = _local_step(x[0], positions.reshape(T, 1), loss_target[0], w_in_t, mid_weights,
                                         late_weights, late_bufs, send_mid_grads, send_late_grads,
                                         send_small_grads, send_early_grads, cw, vec)
    last = jnp.pad(g_norm_mix_pre, ((0, 7), (0, 0)))
    sent["last"] = (last,) + _reduce_start(last, place, name="last_grads_start", whole=True)

    def finish(group, after):
        g32, sems_, g_thru, land_, _ = sent[group]
        landed = _reduce_wait(sems_, g_thru, land_, after, name=group + "_grads_wait")
        red = _reduce_sum(place, g32, landed, name=group + "_grads_sum")
        return _join_halves(red, name=group + "_grads_join")

    def as2d(a):
        return a.reshape(-1, a.shape[-1]) if a.ndim != 4 or a.shape[2] == 1 else a.reshape(a.shape[1], -1)

    grads, delta, new_m, new_v = {}, {}, {}, {}

    def update(n):
        shp = wts[n].shape
        outs = _adamw(as2d(wts[n]), as2d(grads[n]), as2d(mom[n]), as2d(var[n]), name="adamw_" + n)
        delta[n], new_m[n], new_v[n] = (o.reshape(shp) for o in outs)

    g_mid, off = finish("mid", [sent["early"][4]]), 0
    for n, rows in MID:
        grads[n] = g_mid[off:off + rows].reshape(wts[n].shape)
        off += rows
        update(n)
    g_late = finish("late", [delta["w_uv"]])
    for n, off, rows in LATE:
        shp = wts[n].shape
        outs = _adamw(as2d(wts[n]), g_late, as2d(mom[n]), as2d(var[n]), g_row=off, name="adamw_" + n)
        delta[n], new_m[n], new_v[n], grads[n] = (o.reshape(shp) for o in outs)

    g_early = finish("early", [grad_x, delta["w_ff2"]])
    outs = _adamw(wts["w_in"][0].T, g_early[:1200], mom["w_in"][0].T, var["w_in"][0].T, name="adamw_w_in")
    delta["w_in"], new_m["w_in"], new_v["w_in"] = (o.T[None] for o in outs)
    grads["w_in"] = g_early[:1200].T[None]

    def finish_small(group, after):
        svec, sems_, s_thru, land_, _ = sent[group]
        landed = _reduce_wait(sems_, s_thru, land_, after, name=group + "_grads_wait", whole=True)
        return _sum_small((2 * place[0] + place[1])[None], svec, landed, name=group + "_grads_sum")

    ssum = finish_small("small", [delta["w_in"]])
    for r, n in enumerate(SMALL_VECS):
        grads[n] = ssum[r:r + 1, :wts[n].shape[1]]
    grads["norm_mix_pre"] = finish_small("last", [ssum])[:1]
    cw_sum = ssum[SMALL_CONVW_ROW:SMALL_CONVW_ROW + CONV_W]
    grads["conv_w"] = lax.dynamic_slice_in_dim(cw_sum, place[0] * 256, 256, axis=1).reshape(conv_w.shape)
    loss_out = ssum[SMALL_LOSS_ROW, 0]
    update("conv_w")
    small = [jnp.concatenate([_row1024(t[n]) for n in SMALL_VECS], axis=0) for t in (wts, grads, mom, var)]
    outs = _adamw(*small, name="adamw_small")
    for r, n in enumerate(SMALL_VECS):
        delta[n], new_m[n], new_v[n] = (o[r:r + 1, :wts[n].shape[1]] for o in outs)

    return (loss_out, grad_x[None], *[grads[n] for n in ORDER], *[delta[n] for n in ORDER],
            *[new_m[n] for n in ORDER], *[new_v[n] for n in ORDER])
```

```python
import functools

import numpy as np
import jax
import jax.numpy as jnp
from jax import lax
from jax.experimental import pallas as pl
from jax.experimental.pallas import tpu as pltpu

F32 = jnp.float32
BF16 = jnp.bfloat16
MESH = pl.DeviceIdType.MESH

D_MODEL = 1024
N_HEADS = 8
NOPE = 128
ROPE = 64
QK_PAD = 256
Q_RANK = 384
KV_RANK = 256
CONV_W = 31
CONV_TAPS_PAD = 32
HALO = 32
EPS = 1e-6
ROPE_THETA = 10000.0
SCALE = float((NOPE + ROPE) ** -0.5)
Q_SCALE = SCALE * float(np.log2(np.e))
K_UNSCALE = float(np.log(2.0))
Z_A, Z_B, Z_GA, Z_GC, Z_S = range(5)
Z_COLS = 5 * D_MODEL
N_SMALL = Q_RANK + KV_RANK + ROPE
N_CHIPS = 4
N_DEV = 8

ADAM_LR = 0.001
ADAM_B1 = 0.9
ADAM_B2 = 0.999
ADAM_EPS = 1e-08
ADAM_WD = 0.01
ADAM_STEP = 10

VMEM_LIMIT = 48 * 1024 * 1024
NEG = -1e30

EARLY = (("w_in", 1200), ("w_uq", 144), ("w_uk", 64), ("w_uv", 64))
EARLY_ROWS = 1536
LATE = (("w_ff1", 0, 1024), ("w_o_attn", 1024, 256), ("w_pw2", 1536, 256), ("w_out", 1792, 256),
        ("w_ff2", 2048, 1024))
LATE_OFF = {n: off for n, off, _ in LATE}
LATE_ROWS = 3072
SUM_TILE = 256

SMALL_VECS = ("norm_mix_pre", "conv_b", "conv_ln_g", "conv_ln_b", "b_pw2", "norm_mix_post",
              "norm_mlp_pre", "norm_mlp_post", "q_norm", "kv_norm")
SMALL_LOSS_ROW = 10
SMALL_CONVW_ROW = 16


ANY = pl.BlockSpec(memory_space=pl.ANY)


def _params(*sem):
    return pltpu.CompilerParams(dimension_semantics=sem, vmem_limit_bytes=VMEM_LIMIT)


def _tile(n, cap):
    if n <= cap:
        return n
    t = (cap // 128) * 128
    while n % t:
        t -= 128
    return t


def _row_tile(n, cap):
    if n <= cap:
        return n
    t = (cap // 8) * 8
    while n % t:
        t -= 8
    return t


MM_VMEM_BUDGET = 40 * 1024 * 1024


def _mm_tiles(M, N, K, in_bytes, out_bytes):
    tk = K
    while True:
        best = None
        for tn in sorted({_tile(N, cap) for cap in (2560, 2048, 1024)}, reverse=True):
            for tm in sorted({_tile(M, cap) for cap in (1024, 512, 256)}, reverse=True):
                need = 2 * tk * (tm * in_bytes[0] + tn * in_bytes[1]) + 2 * tm * tn * out_bytes
                need += tm * tn * 4 if tk < K else 0
                steps = (M // tm) * (N // tn)
                if need <= MM_VMEM_BUDGET and (best is None or steps < best[0]):
                    best = (steps, tm, tn)
        if best is not None:
            return best[1], best[2], tk
        tk = _tile(K, tk - 128)


def _mm(a, b, *, name, ta=False, tb=False, out_dtype=F32, extras=(), epilogue=None, out_dtypes=None,
        after=(), shape=None, tiles=None, b_spec=None, reds=(), k_slabs=False):
    if shape is None:
        M, K = (a.shape[1], a.shape[0]) if ta else a.shape
        N = b.shape[0] if tb else b.shape[1]
        assert K == (b.shape[1] if tb else b.shape[0]), (a.shape, b.shape, ta, tb)
    else:
        M, N, K = shape
    outs = tuple(out_dtypes or (out_dtype,))
    placed = [o for o in outs if isinstance(o, tuple)]
    ex_arrays = [e[0] if isinstance(e, tuple) else e for e in extras]
    if tiles is None:
        out_bytes = sum(jnp.dtype(d).itemsize for d in outs) + sum(e.dtype.itemsize for e in ex_arrays)
        tiles = _mm_tiles(M, N, K, (a.dtype.itemsize, b.dtype.itemsize), out_bytes)
    tm, tn, tk = tiles
    nk = K // tk
    dims = (((0 if ta else 1,), (1 if tb else 0,)), ((), ()))
    n_ex, n_out = len(extras), len(outs)
    n_pass = len(after) + len(placed)
    assert not reds or tn == N

    def body(*refs):
        a_ref, b_ref = refs[:2]
        ex_refs = refs[2:2 + n_ex]
        o_refs = refs[2 + n_ex + n_pass:2 + n_ex + n_pass + n_out]
        red_refs = refs[2 + n_ex + n_pass + n_out:2 + n_ex + n_pass + n_out + len(reds)]
        first_rows, k = pl.program_id(0) == 0, pl.program_id(2)
        bv = b_ref[...].astype(BF16)
        if k_slabs:
            kc = bv.shape[-1]
            part = sum(lax.dot_general(a_ref[:, p * kc:(p + 1) * kc].astype(BF16), bv[p], dims,
                                       preferred_element_type=F32) for p in range(bv.shape[0]))
        else:
            if bv.ndim == 3:
                bv = bv.reshape(-1, bv.shape[-1])
            part = lax.dot_general(a_ref[...].astype(BF16), bv, dims, preferred_element_type=F32)

        def finish(acc):
            ex = [r[...] for r in ex_refs]
            ex = [e.astype(F32) if e.dtype == BF16 else e for e in ex]
            res = epilogue(acc, *ex) if epilogue else (acc,) * n_out
            for o_ref, r in zip(o_refs, res):
                o_ref[...] = r.astype(o_ref.dtype)
            for red_ref, r in zip(red_refs, res[n_out:]):
                red_ref[...] += r

        if reds:
            @pl.when(first_rows & (k == 0))
            def _():
                for red_ref in red_refs:
                    red_ref[...] = jnp.zeros_like(red_ref)

        if nk == 1:
            finish(part)
        else:
            acc_ref = refs[-1]

            @pl.when(k == 0)
            def _():
                acc_ref[...] = part

            @pl.when((k > 0) & (k < nk - 1))
            def _():
                acc_ref[...] += part

            @pl.when(k == nk - 1)
            def _():
                finish(acc_ref[...] + part)

    a_spec = (pl.BlockSpec((tk, tm), lambda i, j, k: (k, i)) if ta
              else pl.BlockSpec((tm, tk), lambda i, j, k: (i, k)))
    if b_spec is None:
        b_spec = (pl.BlockSpec((tn, tk), lambda i, j, k: (j, k)) if tb
                  else pl.BlockSpec((tk, tn), lambda i, j, k: (k, j)))
    o_spec = pl.BlockSpec((tm, tn), lambda i, j, k: (i, j))
    row_spec = pl.BlockSpec((1, tn), lambda i, j, k: (0, j))
    ex_specs = []
    for e in extras:
        if isinstance(e, tuple):
            ex_specs.append(pl.BlockSpec((tm, tn), functools.partial(lambda i, j, k, cb: (i, cb), cb=e[1])))
        else:
            ex_specs.append(row_spec if e.shape[0] == 1 else o_spec)
    out_shape, out_specs, aliases = [], [], {}
    for o, out in enumerate(outs):
        if isinstance(out, tuple):
            out_shape.append(jax.ShapeDtypeStruct(out[0].shape, out[0].dtype))
            out_specs.append(out[1])
            aliases[2 + n_ex + len(after) + len(aliases)] = o
        else:
            out_shape.append(jax.ShapeDtypeStruct((M, N), out))
            out_specs.append(o_spec)
    passed = list(after) + [buf for buf, _ in placed]
    out_shape = out_shape + [jax.ShapeDtypeStruct((1, w), F32) for w in reds]
    out_specs = out_specs + [pl.BlockSpec((1, w), lambda i, j, k: (0, 0)) for w in reds]
    res = pl.pallas_call(
        body, name=name, out_shape=out_shape,
        grid=(M // tm, N // tn, nk),
        in_specs=[a_spec, b_spec] + ex_specs + [ANY] * len(passed),
        out_specs=out_specs, input_output_aliases=aliases,
        scratch_shapes=[pltpu.VMEM((tm, tn), F32)] if nk > 1 else [],
        compiler_params=_params(*(("arbitrary",) * 3 if reds else ("parallel", "parallel", "arbitrary"))),
    )(a, b, *ex_arrays, *passed)
    return res[0] if len(res) == 1 else res


class _LateWeights:
    def __init__(self, pack):
        self.pack = pack

    def fwd(self, x, n, *, name, tm=1024, **kw):
        T, off = x.shape[0], LATE_OFF[n]
        if n == "w_ff1":
            return _mm(x, self.pack, name=name, shape=(T, 4096, 1024), tiles=(_tile(T, tm), 1024, 1024),
                       b_spec=pl.BlockSpec((None, 1024, 1024), lambda i, j, k: (j, 0, 0)), **kw)
        if n == "w_ff2":
            view = self.pack.reshape(N_CHIPS, LATE_ROWS // 1024, 1024, 1024)
            return _mm(x, view, name=name, shape=(T, 1024, 4096), tiles=(_tile(T, min(tm, 512)), 1024, 4096),
                       b_spec=pl.BlockSpec((N_CHIPS, None, 1024, 1024), lambda i, j, k: (0, off // 1024, 0, 0)),
                       **kw)
        return _mm(x, self.pack.reshape(N_CHIPS, LATE_ROWS // 256, 256, 1024), name=name,
                   shape=(T, 1024, 1024), tiles=(_tile(T, tm), 1024, 1024), b_spec=self._whole(off), **kw)

    @staticmethod
    def _whole(off):
        return pl.BlockSpec((N_CHIPS, None, 256, 1024), lambda i, j, k: (0, off // 256, 0, 0))

    def bwd(self, dy, n, *, name, tm=1024, **kw):
        T, off = dy.shape[0], LATE_OFF[n]
        tm = _tile(T, tm)
        if n == "w_ff1":
            view = self.pack.reshape(N_CHIPS, LATE_ROWS // 1024, 1024, 1024)
            return _mm(dy, view, tb=True, name=name, shape=(T, 1024, 4096), tiles=(tm, 1024, 4096),
                       b_spec=pl.BlockSpec((N_CHIPS, None, 1024, 1024), lambda i, j, k: (0, off // 1024, 0, 0),
                                           pipeline_mode=pl.Buffered(1)), k_slabs=True, **kw)
        if n == "w_ff2":
            return _mm(dy, self.pack, tb=True, name=name, shape=(T, 4096, 1024), tiles=(tm, 1024, 1024),
                       b_spec=pl.BlockSpec((None, 1024, 1024), lambda i, j, k: (j, off // 1024, 0)), **kw)
        return _mm(dy, self.pack.reshape(N_CHIPS, LATE_ROWS // 256, 256, 1024), tb=True, name=name,
                   shape=(T, 1024, 1024), tiles=(tm, 1024, 1024), b_spec=self._whole(off), **kw)


def _late_grad(x, dy, n, bufs, *, name):
    T, off = x.shape[0], LATE_OFF[n]
    if n == "w_ff1":
        spec = pl.BlockSpec((None, 512, 1024), lambda i, j, k: (j, i, 0))
        shape, tiles = (1024, 4096, T), (512, 1024, T)
    elif n == "w_ff2":
        spec = pl.BlockSpec((None, 512, 1024), lambda i, j, k: (i // 2, off // 512 + i % 2, 0))
        shape, tiles = (4096, 1024, T), (512, 1024, T)
    else:
        spec = pl.BlockSpec((None, 256, 1024), lambda i, j, k: (i, off // 256, 0))
        shape, tiles = (1024, 1024, T), (256, 1024, T)
    return _mm(x, dy, ta=True, name=name, shape=shape, tiles=tiles, out_dtypes=[(b, spec) for b in bufs])


def _rowwise(fn, rows, consts, outs, reds=(), *, name, tm=256, after=()):
    T = rows[0][0].shape[0]
    tm = min(tm, T)
    n_row, n_const, n_out = len(rows), len(consts), len(outs)
    placed = [o for o in outs if len(o) == 3]
    n_pass = len(after) + len(placed)

    def body(*refs):
        i = pl.program_id(0)
        vals = [r[...] for r in refs[:n_row + n_const]]
        vals = [v.astype(F32) if v.dtype == BF16 else v for v in vals]
        res = fn(*vals)
        out_refs = refs[n_row + n_const + n_pass:]
        for k in range(n_out):
            out_refs[k][...] = res[k].astype(out_refs[k].dtype)
        for k in range(len(reds)):
            ref = out_refs[n_out + k]

            @pl.when(i == 0)
            def _(ref=ref):
                ref[...] = jnp.zeros_like(ref)

            ref[...] += res[n_out + k]

    in_specs = [pl.BlockSpec((tm, w), functools.partial(lambda i, cb: (i, cb), cb=cb))
                for (_, w, cb) in rows]
    in_specs += [pl.BlockSpec(c.shape, lambda i: (0, 0)) for c in consts]
    in_specs += [ANY] * n_pass
    out_specs, out_shape, aliases = [], [], {}
    for o, out in enumerate(outs):
        if len(out) == 3:
            w, buf, cb = out
            out_specs.append(pl.BlockSpec((tm, w), functools.partial(lambda i, cb: (i, cb), cb=cb)))
            out_shape.append(jax.ShapeDtypeStruct(buf.shape, buf.dtype))
            aliases[n_row + n_const + len(after) + len(aliases)] = o
        else:
            out_specs.append(pl.BlockSpec((tm, out[0]), lambda i: (i, 0)))
            out_shape.append(jax.ShapeDtypeStruct((T, out[0]), out[1]))
    out_specs += [pl.BlockSpec(s, lambda i: (0, 0)) for s in reds]
    out_shape += [jax.ShapeDtypeStruct(s, F32) for s in reds]
    return pl.pallas_call(
        body, name=name, out_shape=out_shape, grid=(T // tm,),
        in_specs=in_specs, out_specs=out_specs, input_output_aliases=aliases,
        compiler_params=_params("arbitrary"),
    )(*[r[0] for r in rows], *consts, *after, *[o[1] for o in placed])


def _rms(x, g):
    r = lax.rsqrt(jnp.mean(x * x, axis=-1, keepdims=True) + EPS)
    return x * r * g


def _rms_bwd(x, g, dy):
    r = lax.rsqrt(jnp.mean(x * x, axis=-1, keepdims=True) + EPS)
    n = x * r
    dyg = dy * g
    dx = r * (dyg - n * jnp.mean(dyg * n, axis=-1, keepdims=True))
    return dx, jnp.sum(dy * n, axis=0, keepdims=True)


def _sigmoid(x):
    return 1.0 / (1.0 + jnp.exp(-x))


def _rope(x, cs, sa, sb):
    return x * cs + pltpu.roll(x, 32, 1) * sa + pltpu.roll(x, 96, 1) * sb


def _rope_t(dy, cs, sa, sb):
    return dy * cs + pltpu.roll(dy * sa, 96, 1) + pltpu.roll(dy * sb, 32, 1)


FLASH_FWD_TQ = 4096
FLASH_FWD_TK = 1024
FLASH_BWD_TQ = 1024
FLASH_BWD_TK = 512
NT_DIMS = (((1,), (1,)), ((), ()))
TN_DIMS = (((0,), (0,)), ((), ()))


def _flash_fwd(q, k, v, *, tq=FLASH_FWD_TQ, tk=FLASH_FWD_TK):
    T = q.shape[0]
    tq, tk = min(tq, T), min(tk, T)
    n_diag = tq // tk
    assert n_diag >= 1

    def body(q_ref, k_ref, v_ref, o_ref, lse_ref):
        i = pl.program_id(1)

        def logits(j, r0=0):
            sl = pl.ds(pl.multiple_of(j * tk, tk), tk)
            return lax.dot_general(q_ref[r0:tq, :], k_ref[sl, :], NT_DIMS, preferred_element_type=F32)

        def update(j, s, m, acc, masked, r0=0):
            sl = pl.ds(pl.multiple_of(j * tk, tk), tk)
            if masked:
                row = i * tq + r0 + lax.broadcasted_iota(jnp.int32, (tq - r0, tk), 0)
                col = j * tk + lax.broadcasted_iota(jnp.int32, (tq - r0, tk), 1)
                s = jnp.where(col <= row, s, NEG)
            m_old, acc_old = m[r0:], acc[r0:]
            m_new = jnp.maximum(m_old, jnp.max(s, axis=-1, keepdims=True))
            p = jnp.exp2(s - m_new)
            acc_new = (jnp.exp2(m_old - m_new) * acc_old
                       + jnp.dot(p.astype(BF16), v_ref[sl, :], preferred_element_type=F32))
            if r0:
                m_new = jnp.concatenate([m[:r0], m_new], axis=0)
                acc_new = jnp.concatenate([acc[:r0], acc_new], axis=0)
            return m_new, acc_new

        def step(j, carry):
            s, m, acc = carry
            s_next = logits(j + 1)
            m, acc = update(j, s, m, acc, False)
            return s_next, m, acc

        first = i * n_diag
        init = (logits(0), jnp.full((tq, 1), NEG, F32), jnp.zeros((tq, QK_PAD), F32))
        s, m, acc = lax.fori_loop(0, first, step, init)
        for d in range(n_diag):
            m, acc = update(first + d, s, m, acc, True, r0=d * tk)
            if d + 1 < n_diag:
                s = logits(first + d + 1, r0=(d + 1) * tk)
        l = acc[:, NOPE:]
        o_ref[...] = (acc[:, :NOPE] / l).astype(BF16)
        lse_ref[...] = m + jnp.log2(l)

    return pl.pallas_call(
        body, name="flash_fwd",
        out_shape=[jax.ShapeDtypeStruct((T, N_HEADS * NOPE), BF16),
                   jax.ShapeDtypeStruct((T, N_HEADS * NOPE), F32)],
        grid=(N_HEADS, T // tq),
        in_specs=[pl.BlockSpec((tq, QK_PAD), lambda h, i: (i, h)),
                  pl.BlockSpec((T, QK_PAD), lambda h, i: (0, h)),
                  pl.BlockSpec((T, QK_PAD), lambda h, i: (0, h))],
        out_specs=[pl.BlockSpec((tq, NOPE), lambda h, i: (i, h)),
                   pl.BlockSpec((tq, NOPE), lambda h, i: (i, h))],
        compiler_params=_params("parallel", "arbitrary"),
    )(q, k, v)


def _flash_bwd(q, k, v, do, lse, *, tq=FLASH_BWD_TQ, tk=FLASH_BWD_TK):
    T = q.shape[0]
    tq, tk = min(tq, T), min(tk, T)

    def body(q_ref, k_ref, v_ref, do_ref, lse_ref, dq_ref, dk_ref, dv_ref, dq_acc):
        dq_acc[...] = jnp.zeros_like(dq_acc)
        for j in range(T // tk):
            k0 = j * tk
            kj = k_ref[k0:k0 + tk, :]
            vj = v_ref[k0:k0 + tk, :]
            dk = jnp.zeros((tk, QK_PAD), F32)
            dv = jnp.zeros((tk, NOPE), F32)
            for i in range(k0 // tq, T // tq):
                r0 = max(i * tq, k0)
                r1 = (i + 1) * tq
                qi = q_ref[r0:r1, :]
                doi = do_ref[r0:r1, :]
                s = lax.dot_general(qi, kj, NT_DIMS, preferred_element_type=F32)
                p = jnp.exp2(s - lse_ref[r0:r1, :][:, :1])
                if r0 < k0 + tk:
                    row = r0 + lax.broadcasted_iota(jnp.int32, (r1 - r0, tk), 0)
                    col = k0 + lax.broadcasted_iota(jnp.int32, (r1 - r0, tk), 1)
                    p = jnp.where(col <= row, p, 0.0)
                dv = dv + lax.dot_general(p.astype(BF16), doi[:, :NOPE], TN_DIMS, preferred_element_type=F32)
                ds = (p * lax.dot_general(doi, vj, NT_DIMS, preferred_element_type=F32)).astype(BF16)
                dk = dk + lax.dot_general(ds, qi, TN_DIMS, preferred_element_type=F32)
                dq_acc[r0:r1, :] += jnp.dot(ds, kj, preferred_element_type=F32)
            dk_ref[k0:k0 + tk, :] = dk.astype(BF16)
            dv_ref[k0:k0 + tk, :] = dv.astype(BF16)
        dq_ref[...] = dq_acc[...].astype(BF16)

    return pl.pallas_call(
        body, name="flash_bwd",
        out_shape=[jax.ShapeDtypeStruct((T, N_HEADS * QK_PAD), BF16),
                   jax.ShapeDtypeStruct((T, N_HEADS * QK_PAD), BF16),
                   jax.ShapeDtypeStruct((T, N_HEADS * NOPE), BF16)],
        grid=(N_HEADS,),
        in_specs=[pl.BlockSpec((T, QK_PAD), lambda h: (0, h)),
                  pl.BlockSpec((T, QK_PAD), lambda h: (0, h)),
                  pl.BlockSpec((T, QK_PAD), lambda h: (0, h)),
                  pl.BlockSpec((T, QK_PAD), lambda h: (0, h)),
                  pl.BlockSpec((T, NOPE), lambda h: (0, h))],
        out_specs=[pl.BlockSpec((T, QK_PAD), lambda h: (0, h)),
                   pl.BlockSpec((T, QK_PAD), lambda h: (0, h)),
                   pl.BlockSpec((T, NOPE), lambda h: (0, h))],
        scratch_shapes=[pltpu.VMEM((T, QK_PAD), F32)],
        compiler_params=_params("arbitrary"),
    )(q, k, v, do, lse)


SUB = 128


def _windows(buf, t0, c0, first, last):
    span = SUB + 8 * ((last + 7) // 8)
    base = buf[t0:t0 + span, c0:c0 + SUB]
    for r in range(8):
        offs = [o for o in range(first, last + 1) if o % 8 == r]
        if offs:
            win = base if r == 0 else pltpu.roll(base, span - r, 0)
            for o in offs:
                yield o, win[o - r:o - r + SUB, :]


def _glu(a, b):
    return a.astype(F32) * _sigmoid(b.astype(F32))


def _conv_fwd(z, w, bias, *, tm=256):
    T, C = z.shape[0], D_MODEL
    tm = min(tm, T)
    hb = tm // HALO
    lead = HALO - (CONV_W - 1)

    def body(a_ref, b_ref, ah_ref, bh_ref, w_ref, bias_ref, o_ref, buf):
        t = pl.program_id(0)
        buf[0:HALO, :] = jnp.where(t > 0, _glu(ah_ref[...], bh_ref[...]), 0.0)
        buf[HALO:, :] = _glu(a_ref[...], b_ref[...])
        for c0 in range(0, C, SUB):
            for t0 in range(0, tm, SUB):
                acc = jnp.broadcast_to(bias_ref[:, c0:c0 + SUB], (SUB, SUB))
                for o, win in _windows(buf, t0, c0, lead, lead + CONV_W - 1):
                    acc = acc + win * w_ref[o - lead:o - lead + 1, c0:c0 + SUB]
                o_ref[t0:t0 + SUB, c0:c0 + SUB] = acc

    def halo(cb):
        return pl.BlockSpec((HALO, C), lambda t: (jnp.maximum(t * hb - 1, 0), cb))

    return pl.pallas_call(
        body, name="conv_fwd",
        out_shape=jax.ShapeDtypeStruct((T, C), F32),
        grid=(T // tm,),
        in_specs=[pl.BlockSpec((tm, C), lambda t: (t, Z_A)), pl.BlockSpec((tm, C), lambda t: (t, Z_B)),
                  halo(Z_A), halo(Z_B),
                  pl.BlockSpec((CONV_TAPS_PAD, C), lambda t: (0, 0)), pl.BlockSpec((1, C), lambda t: (0, 0))],
        out_specs=pl.BlockSpec((tm, C), lambda t: (t, 0)),
        scratch_shapes=[pltpu.VMEM((HALO + tm, C), F32)],
        compiler_params=_params("parallel"),
    )(z, z, z, z, w, bias)


def _conv_bwd(z, du1, w, dz, *, tm=256):
    T, C = z.shape[0], D_MODEL
    tm = min(tm, T)
    hb = tm // HALO
    nt = T // tm
    last_halo = T // HALO - 1
    lead = HALO - (CONV_W - 1)
    assert (Z_A, Z_B) == (0, 1)

    def body(a_ref, b_ref, ah_ref, bh_ref, d_ref, dh_ref, w_ref, dz_in, dz_ref, dw_ref, ubuf, dbuf, dw8, gate):
        t = pl.program_id(0)
        ubuf[0:HALO, :] = jnp.where(t > 0, _glu(ah_ref[...], bh_ref[...]), 0.0)
        gate[...] = _sigmoid(b_ref[...].astype(F32))
        ubuf[HALO:, :] = a_ref[...].astype(F32) * gate[...]
        dbuf[0:tm, :] = d_ref[...]
        dbuf[tm:, :] = jnp.where(t < nt - 1, dh_ref[...], 0.0)

        @pl.when(t == 0)
        def _():
            dw8[...] = jnp.zeros_like(dw8)

        for c0 in range(0, C, SUB):
            for t0 in range(0, tm, SUB):
                acc = jnp.zeros((SUB, SUB), F32)
                for o, win in _windows(dbuf, t0, c0, 0, CONV_W - 1):
                    kk = CONV_W - 1 - o
                    acc = acc + win * w_ref[kk:kk + 1, c0:c0 + SUB]
                av = a_ref[t0:t0 + SUB, c0:c0 + SUB].astype(F32)
                sg = gate[t0:t0 + SUB, c0:c0 + SUB]
                dz_ref[t0:t0 + SUB, c0:c0 + SUB] = (acc * sg).astype(dz_ref.dtype)
                dz_ref[t0:t0 + SUB, C + c0:C + c0 + SUB] = (acc * av * sg * (1.0 - sg)).astype(dz_ref.dtype)
                dsub = d_ref[t0:t0 + SUB, c0:c0 + SUB]
                for o, win in _windows(ubuf, t0, c0, lead, lead + CONV_W - 1):
                    kk = o - lead
                    dw8[8 * kk:8 * kk + 8, c0:c0 + SUB] += jnp.sum(
                        (dsub * win).reshape(SUB // 8, 8, SUB), axis=0)

        @pl.when(t == nt - 1)
        def _():
            dw_ref[...] = jnp.sum(dw8[...].reshape(CONV_TAPS_PAD, 8, C), axis=1)

    def halo(cb):
        return pl.BlockSpec((HALO, C), lambda t: (jnp.maximum(t * hb - 1, 0), cb))

    return pl.pallas_call(
        body, name="conv_bwd",
        out_shape=[jax.ShapeDtypeStruct(dz.shape, dz.dtype), jax.ShapeDtypeStruct((CONV_TAPS_PAD, C), F32)],
        grid=(nt,),
        in_specs=[pl.BlockSpec((tm, C), lambda t: (t, Z_A)), pl.BlockSpec((tm, C), lambda t: (t, Z_B)),
                  halo(Z_A), halo(Z_B),
                  pl.BlockSpec((tm, C), lambda t: (t, 0)),
                  pl.BlockSpec((HALO, C), lambda t: (jnp.minimum((t + 1) * hb, last_halo), 0)),
                  pl.BlockSpec((CONV_TAPS_PAD, C), lambda t: (0, 0)), ANY],
        out_specs=[pl.BlockSpec((tm, 2 * C), lambda t: (t, 0)),
                   pl.BlockSpec((CONV_TAPS_PAD, C), lambda t: (0, 0))],
        input_output_aliases={7: 0},
        scratch_shapes=[pltpu.VMEM((HALO + tm, C), F32), pltpu.VMEM((tm + HALO, C), F32),
                        pltpu.VMEM((CONV_TAPS_PAD * 8, C), F32), pltpu.VMEM((tm, C), F32)],
        compiler_params=_params("arbitrary"),
    )(z, z, z, z, du1, du1, w, dz)


def _adamw(w, g, m, v, *, name, g_row=None):
    R, C = w.shape
    tr = _row_tile(R, max(8, (1 << 18) // C // 8 * 8))
    first = 0 if g_row is None else g_row // tr
    assert g_row is None or g_row % tr == 0

    def body(w_ref, g_ref, m_ref, v_ref, d_ref, nm_ref, nv_ref, *g_out):
        gv = g_ref[...]
        nm = ADAM_B1 * m_ref[...] + (1.0 - ADAM_B1) * gv
        nv = ADAM_B2 * v_ref[...] + (1.0 - ADAM_B2) * (gv * gv)
        m_hat = nm / (1.0 - ADAM_B1 ** ADAM_STEP)
        v_hat = nv / (1.0 - ADAM_B2 ** ADAM_STEP)
        d_ref[...] = -ADAM_LR * (m_hat / (jnp.sqrt(v_hat) + ADAM_EPS) + ADAM_WD * w_ref[...])
        nm_ref[...] = nm
        nv_ref[...] = nv
        for ref in g_out:
            ref[...] = gv

    spec = pl.BlockSpec((tr, C), lambda i: (i, 0))
    g_spec = pl.BlockSpec((tr, C), lambda i: (first + i, 0))
    n_out = 3 if g_row is None else 4
    return pl.pallas_call(
        body, name=name, out_shape=[jax.ShapeDtypeStruct((R, C), F32)] * n_out,
        grid=(R // tr,), in_specs=[spec, g_spec, spec, spec], out_specs=[spec] * n_out,
        compiler_params=_params("parallel"),
    )(w, g, m, v)


def _coords():
    return lax.axis_index("x"), lax.axis_index("y"), lax.axis_index("c")


def _remote(src, dst, ssem, rsem, to):
    return pltpu.make_async_remote_copy(src_ref=src, dst_ref=dst, send_sem=ssem, recv_sem=rsem,
                                        device_id=to, device_id_type=MESH)


def _half(c, rows):
    return pl.ds(pl.multiple_of(c * (rows // 2), 16), rows // 2)


def _gather_weights(wpack, cw):
    def body(w_ref, cw_ref, full_ref, cwf_ref, ssem, rsem):
        x, y, c = _coords()
        q = 2 * x + y
        sib = (x, y, 1 - c)
        chips = [(1 - x, y), (x, 1 - y), (1 - x, 1 - y)]
        mine, other = _half(c, w_ref.shape[0]), _half(1 - c, w_ref.shape[0])
        first = []
        for j, (px, py) in enumerate(chips):
            first.append(_remote(w_ref.at[mine], full_ref.at[q, mine], ssem.at[j], rsem.at[j], (px, py, c)))
            first.append(_remote(cw_ref, cwf_ref.at[q], ssem.at[6 + j], rsem.at[6 + j], (px, py, c)))
        for cp in first:
            cp.start()
        passed = []
        for j, (px, py) in enumerate(chips):
            pq = 2 * px + py
            _remote(w_ref.at[mine], full_ref.at[pq, mine], ssem.at[j], rsem.at[j], (px, py, c)).wait_recv()
            fw = _remote(full_ref.at[pq, mine], full_ref.at[pq, mine], ssem.at[3 + j], rsem.at[3 + j], sib)
            fw.start()
            passed.append(fw)
        for j, (px, py) in enumerate(chips):
            pq = 2 * px + py
            _remote(full_ref.at[pq, other], full_ref.at[pq, other], ssem.at[3 + j], rsem.at[3 + j], sib).wait_recv()
            _remote(cw_ref, cwf_ref.at[pq], ssem.at[6 + j], rsem.at[6 + j], (px, py, c)).wait_recv()
        for cp in first + passed:
            cp.wait_send()

    return pl.pallas_call(
        body, name="gather_weights",
        out_shape=[jax.ShapeDtypeStruct((N_CHIPS,) + wpack.shape, wpack.dtype),
                   jax.ShapeDtypeStruct((N_CHIPS,) + cw.shape, cw.dtype)],
        in_specs=[ANY, ANY], out_specs=[ANY, ANY],
        scratch_shapes=[pltpu.SemaphoreType.DMA((9,)), pltpu.SemaphoreType.DMA((9,))],
    )(wpack, cw)


SEM = pl.BlockSpec(memory_space=pltpu.SEMAPHORE)
HBM = pl.BlockSpec(memory_space=pltpu.HBM)
N_LATE = 6


def _late_peers(x, y, c):
    out = []
    for j, (px, py) in enumerate([(1 - x, y), (x, 1 - y), (1 - x, 1 - y)]):
        for t in (0, 1):
            out.append((2 * j + t, px, py, c if t == 0 else 1 - c))
    return out


def _late_gather_start(w, after):
    rows = w.shape[0]

    def body(w_ref, land_ref, after_ref, *outs):
        sems = outs[:2 * N_LATE]
        token = outs[2 * N_LATE + 2]
        x, y, c = _coords()
        q = 2 * x + y
        mine = _half(c, rows)
        for k, px, py, pc in _late_peers(x, y, c):
            _remote(w_ref.at[mine], land_ref.at[q, mine], sems[k], sems[N_LATE + k], (px, py, pc)).start()
        token[...] = jnp.zeros_like(token)

    res = pl.pallas_call(
        body, name="late_gather_start",
        out_shape=tuple([pltpu.SemaphoreType.DMA(())] * (2 * N_LATE)) + (
            pltpu.HBM(w.shape, w.dtype), pltpu.HBM((N_CHIPS,) + w.shape, w.dtype),
            jax.ShapeDtypeStruct((8, 128), F32)),
        in_specs=(HBM, HBM, ANY),
        out_specs=tuple([SEM] * (2 * N_LATE)) + (HBM, HBM, pl.BlockSpec(memory_space=pltpu.VMEM)),
        input_output_aliases={0: 2 * N_LATE, 1: 2 * N_LATE + 1},
        compiler_params=pltpu.CompilerParams(has_side_effects=pltpu.SideEffectType.DATAFLOW_SIDE_EFFECTING),
    )(pltpu.with_memory_space_constraint(w, pltpu.HBM),
      pltpu.with_memory_space_constraint(lax.empty((N_CHIPS,) + w.shape, w.dtype), pltpu.HBM), after)
    return res[:2 * N_LATE], res[2 * N_LATE], res[2 * N_LATE + 1], res[2 * N_LATE + 2]


def _late_gather_wait(sems, w_thru, land_thru, after):
    rows = w_thru.shape[0]

    def body(w_ref, land_ref, *rest):
        sems = rest[:2 * N_LATE]
        x, y, c = _coords()
        for k, px, py, pc in _late_peers(x, y, c):
            cp = _remote(w_ref.at[_half(pc, rows)], land_ref.at[2 * px + py, _half(pc, rows)],
                         sems[k], sems[N_LATE + k], (px, py, pc))
            cp.wait_send()
            cp.wait_recv()

    return pl.pallas_call(
        body, name="late_gather_wait",
        out_shape=(pltpu.HBM(w_thru.shape, w_thru.dtype), pltpu.HBM(land_thru.shape, land_thru.dtype)),
        in_specs=(HBM, HBM) + tuple([SEM] * (2 * N_LATE)) + (ANY,),
        out_specs=(HBM, HBM), input_output_aliases={0: 0, 1: 1},
        compiler_params=pltpu.CompilerParams(has_side_effects=pltpu.SideEffectType.DATAFLOW_SIDE_EFFECTING),
    )(w_thru, land_thru, *sems, after)[1]


N_PEERS = N_DEV - 1


def _peers(x, y, c):
    out = []
    for r in range(1, N_DEV):
        out.append((r - 1, 1 - x if r & 4 else x, 1 - y if r & 2 else y, 1 - c if r & 1 else c))
    return out


def _piece(g_ref, px, py, pc, whole):
    return g_ref if whole else g_ref.at[2 * px + py, _half(pc, g_ref.shape[1])]


def _reduce_start(g16, after, *, name, whole=False):
    def body(g_ref, land_ref, after_ref, *outs):
        sems = outs[:2 * N_PEERS]
        token = outs[2 * N_PEERS + 2]
        x, y, c = _coords()
        for k, px, py, pc in _peers(x, y, c):
            _remote(_piece(g_ref, px, py, pc, whole), land_ref.at[k], sems[k], sems[N_PEERS + k],
                    (px, py, pc)).start()
        token[...] = jnp.zeros_like(token)

    land_shape = (N_PEERS,) + (g16.shape if whole else (g16.shape[1] // 2, 1024))
    res = pl.pallas_call(
        body, name=name,
        out_shape=tuple([pltpu.SemaphoreType.DMA(())] * (2 * N_PEERS)) + (
            pltpu.HBM(g16.shape, g16.dtype), pltpu.HBM(land_shape, g16.dtype),
            jax.ShapeDtypeStruct((8, 128), F32)),
        in_specs=(HBM, HBM, ANY),
        out_specs=tuple([SEM] * (2 * N_PEERS)) + (HBM, HBM, pl.BlockSpec(memory_space=pltpu.VMEM)),
        input_output_aliases={0: 2 * N_PEERS, 1: 2 * N_PEERS + 1},
        compiler_params=pltpu.CompilerParams(has_side_effects=pltpu.SideEffectType.DATAFLOW_SIDE_EFFECTING),
    )(pltpu.with_memory_space_constraint(g16, pltpu.HBM),
      pltpu.with_memory_space_constraint(lax.empty(land_shape, g16.dtype), pltpu.HBM), after)
    return res[:2 * N_PEERS], res[2 * N_PEERS], res[2 * N_PEERS + 1], res[2 * N_PEERS + 2]


def _reduce_wait(sems, g_thru, land_thru, after, *, name, whole=False):
    def body(g_ref, land_ref, *rest):
        sems = rest[:2 * N_PEERS]
        x, y, c = _coords()
        for k, px, py, pc in _peers(x, y, c):
            cp = _remote(_piece(g_ref, px, py, pc, whole), land_ref.at[k], sems[k], sems[N_PEERS + k],
                         (px, py, pc))
            cp.wait_send()
            cp.wait_recv()

    return pl.pallas_call(
        body, name=name,
        out_shape=(pltpu.HBM(g_thru.shape, g_thru.dtype), pltpu.HBM(land_thru.shape, land_thru.dtype)),
        in_specs=(HBM, HBM) + tuple([SEM] * (2 * N_PEERS)) + tuple([ANY] * len(after)),
        out_specs=(HBM, HBM), input_output_aliases={0: 0, 1: 1},
        compiler_params=pltpu.CompilerParams(has_side_effects=pltpu.SideEffectType.DATAFLOW_SIDE_EFFECTING),
    )(g_thru, land_thru, *sems, *after)[1]


def _reduce_sum(place, g32, land, *, name):
    rows = g32.shape[1]
    nb = rows // 2 // SUM_TILE

    def body(place_ref, g_ref, l_ref, o_ref):
        acc = g_ref[...]
        for k in range(N_PEERS):
            acc = acc + l_ref[k].astype(F32)
        o_ref[...] = acc

    return pl.pallas_call(
        body, name=name,
        out_shape=jax.ShapeDtypeStruct((rows, 1024), F32),
        grid_spec=pltpu.PrefetchScalarGridSpec(
            num_scalar_prefetch=1, grid=(nb,),
            in_specs=[pl.BlockSpec((None, SUM_TILE, 1024), lambda i, s: (s[0], s[1] * nb + i, 0)),
                      pl.BlockSpec((N_PEERS, SUM_TILE, 1024), lambda i, s: (0, i, 0))],
            out_specs=pl.BlockSpec((SUM_TILE, 1024), lambda i, s: (s[1] * nb + i, 0))),
        compiler_params=_params("parallel"),
    )(place, g32, land)


def _join_halves(red, *, name):
    rows = red.shape[0]

    def body(r_ref, o_ref, ssem, rsem):
        x, y, c = _coords()
        cp = _remote(r_ref.at[_half(c, rows)], o_ref.at[_half(c, rows)], ssem, rsem, (x, y, 1 - c))
        cp.start()
        _remote(r_ref.at[_half(1 - c, rows)], o_ref.at[_half(1 - c, rows)], ssem, rsem,
                (x, y, 1 - c)).wait_recv()
        cp.wait_send()

    return pl.pallas_call(
        body, name=name,
        out_shape=jax.ShapeDtypeStruct(red.shape, F32),
        in_specs=[ANY], out_specs=ANY, input_output_aliases={0: 0},
        scratch_shapes=[pltpu.SemaphoreType.DMA, pltpu.SemaphoreType.DMA],
    )(red)


def _sum_small(me, svec, land, *, name):
    def body(me_ref, s_ref, l_ref, o_ref):
        mine = me_ref[0]
        acc = None
        for d in range(N_DEV):
            theirs = l_ref[jnp.maximum(jnp.bitwise_xor(mine, d) - 1, 0)]
            v = jnp.where(mine == d, s_ref[...], theirs)
            acc = v if acc is None else acc + v
        o_ref[...] = acc

    return pl.pallas_call(
        body, name=name,
        out_shape=jax.ShapeDtypeStruct(svec.shape, F32),
        grid_spec=pltpu.PrefetchScalarGridSpec(
            num_scalar_prefetch=1, grid=(1,),
            in_specs=[pl.BlockSpec(svec.shape, lambda i, s: (0, 0)),
                      pl.BlockSpec(land.shape, lambda i, s: (0, 0, 0))],
            out_specs=pl.BlockSpec(svec.shape, lambda i, s: (0, 0))),
        compiler_params=_params("arbitrary"),
    )(me, svec, land)


def _pack_early_shards(wts):
    parts = [wts["w_in"][0].T.astype(BF16)] + [wts[n].astype(BF16).reshape(-1, 1024) for n, _ in EARLY[1:]]
    used = sum(r for _, r in EARLY)
    return jnp.concatenate(parts + [jnp.zeros((EARLY_ROWS - used, 1024), BF16)], axis=0)


def _pack_late_shards(wts):
    parts, at = [], 0
    for n, off, rows in LATE:
        if off > at:
            parts.append(jnp.zeros((off - at, 1024), BF16))
        parts.append(wts[n].astype(BF16).reshape(rows, 1024))
        at = off + rows
    return jnp.concatenate(parts, axis=0)


def _unpack_early(full):
    w = {}
    w["w_in_t"] = jnp.concatenate(
        [full[0, N_SMALL:1200]] + [full[p, :1200] for p in range(1, N_CHIPS)]
        + [full[0, :N_SMALL], jnp.zeros((D_MODEL - N_SMALL, 1024), full.dtype)], axis=0)
    w["w_uq"] = full[:, 1200:1344].reshape(4, 384, 384).transpose(1, 0, 2).reshape(384, 1536)
    w["w_uk"] = full[:, 1344:1408].reshape(256, 1024)
    w["w_uv"] = full[:, 1408:1472].reshape(256, 1024)
    return w


def _pack_early_grads(g):
    gt = g["w_in_t"]
    rest = 1200 - N_SMALL
    w_in = jnp.stack([jnp.concatenate([gt[4 * D_MODEL:4 * D_MODEL + N_SMALL], gt[:rest]], axis=0)]
                     + [gt[rest + 1200 * (p - 1):rest + 1200 * p] for p in range(1, N_CHIPS)])
    parts = [
        w_in,
        g["w_uq"].reshape(384, 4, 384).transpose(1, 0, 2).reshape(4, 144, 1024),
        g["w_uk"].reshape(4, 64, 1024),
        g["w_uv"].reshape(4, 64, 1024),
        jnp.zeros((4, EARLY_ROWS - sum(r for _, r in EARLY), 1024), F32),
    ]
    return jnp.concatenate(parts, axis=1)


def _pad_w_uq(w):
    w = w.reshape(Q_RANK, N_HEADS, NOPE + ROPE)
    return jnp.pad(w, ((0, 0), (0, 0), (0, QK_PAD - NOPE - ROPE))).reshape(Q_RANK, N_HEADS * QK_PAD)


def _local_step(xs, pos, tgt, W, late_weights, late_bufs, send_late_grads, send_small_grads, send_early_grads,
                cw, vec):
    D = D_MODEL
    w_in_t = W["w_in_t"]
    w_uq = _pad_w_uq(W["w_uq"])
    w_ukv = jnp.concatenate([W["w_uk"], W["w_uv"]], axis=1)
    inv = ROPE_THETA ** (-jnp.arange(0, ROPE, 2, dtype=F32) / ROPE)
    inv_freq = jnp.concatenate([inv, inv, jnp.zeros((128 - ROPE,), F32)])[None, :]

    (h,) = _rowwise(lambda x, g: (_rms(x, g),), [(xs, D, 0)], [vec["norm_mix_pre"]], [(D, BF16)],
                    name="rms_pre")
    z = _mm(h, w_in_t, tb=True, out_dtype=BF16, name="mm_in")

    def qkv_prep(zs, p, qn, kvn, invf):
        ang = p.astype(F32) * invf
        cosv, sinv = jnp.cos(ang), jnp.sin(ang)
        ln = lax.broadcasted_iota(jnp.int32, ang.shape, 1)
        cs = jnp.where(ln < ROPE, cosv, 0.0)
        sa = jnp.where((ln >= ROPE // 2) & (ln < ROPE), sinv, 0.0)
        sb = jnp.where(ln < ROPE // 2, -sinv, 0.0)
        cqn = _rms(zs[:, :Q_RANK], qn)
        ckvn = _rms(zs[:, Q_RANK:Q_RANK + KV_RANK], kvn)
        kr = _rope(zs[:, 640:768], cs, sa, sb)
        return cqn, ckvn, kr, cs, sa, sb

    cqn, ckvn, krot, cs, sa, sb = _rowwise(
        qkv_prep, [(z, D, Z_S), (pos, 1, 0)], [vec["q_norm"], vec["kv_norm"], inv_freq],
        [(Q_RANK, BF16), (KV_RANK, BF16), (128, BF16), (128, F32), (128, F32), (128, F32)],
        name="qkv_prep")
    q_pre = _mm(cqn, w_uq, out_dtype=BF16, name="mm_uq")
    kv = _mm(ckvn, w_ukv, out_dtype=BF16, name="mm_ukv")

    def qk_rope(qp, kn, vv, kr, cs_, sa_, sb_):
        qs, ks, vs = [], [], []
        ones = jnp.ones((qp.shape[0], NOPE), F32)
        for hd in range(N_HEADS):
            qs.append(qp[:, hd * QK_PAD:hd * QK_PAD + NOPE])
            qs.append(_rope(qp[:, hd * QK_PAD + NOPE:(hd + 1) * QK_PAD], cs_, sa_, sb_))
            ks.append(kn[:, hd * NOPE:(hd + 1) * NOPE])
            ks.append(kr.astype(F32))
            vs.append(vv[:, hd * NOPE:(hd + 1) * NOPE])
            vs.append(ones)
        return (jnp.concatenate(qs, axis=1) * Q_SCALE, jnp.concatenate(ks, axis=1),
                jnp.concatenate(vs, axis=1))

    q, k, v = _rowwise(
        qk_rope, [(q_pre, 2048, 0), (kv, 1024, 0), (kv, 1024, 1), (krot, 128, 0), (cs, 128, 0),
                  (sa, 128, 0), (sb, 128, 0)], [],
        [(2048, BF16), (2048, BF16), (2048, BF16)], name="qk_rope")
    attn, lse = _flash_fwd(q, k, v)
    LW = late_weights(attn)
    y_attn = LW.fwd(attn, "w_o_attn", out_dtype=BF16, name="mm_o_attn")

    u1 = _conv_fwd(z, cw, vec["conv_b"])

    def ln_parts(u, g, b):
        mu = jnp.mean(u, axis=-1, keepdims=True)
        xc = u - mu
        rs = lax.rsqrt(jnp.mean(xc * xc, axis=-1, keepdims=True) + EPS)
        nh = xc * rs
        return nh, rs, nh * g + b

    def ln_silu(u, g, b):
        y = ln_parts(u, g, b)[2]
        return (y * _sigmoid(y),)

    (u3,) = _rowwise(ln_silu, [(u1, D, 0)], [vec["conv_ln_g"], vec["conv_ln_b"]], [(D, BF16)],
                     name="ln_silu")

    def merge(yc, ga, gc, ya, b2):
        return yc, _sigmoid(ga) * ya + _sigmoid(gc) * (yc + b2)

    y_conv, merged = LW.fwd(u3, "w_pw2", name="mm_pw2", tm=512, epilogue=merge, out_dtypes=(BF16, BF16),
                            extras=[(z, Z_GA), (z, Z_GC), y_attn, vec["b_pw2"]])

    def post1(m, x, g1, g2):
        x1 = x + _rms(m, g1)
        return m, x1, _rms(x1, g2)

    mo, x1, h2 = LW.fwd(merged, "w_out", name="mm_out", tm=512, epilogue=post1, out_dtypes=(F32, F32, BF16),
                        extras=[xs, vec["norm_mix_post"], vec["norm_mlp_pre"]])

    def sqrelu(acc):
        r = jnp.maximum(acc, 0.0)
        return r * r, r

    a2, r1 = LW.fwd(h2, "w_ff1", name="mm_ff1", epilogue=sqrelu, out_dtypes=(BF16, BF16))

    def post2(f_, x1_, t_, g):
        e = x1_ + _rms(f_, g) - t_
        dy = e * (1.0 / D)
        df, dg = _rms_bwd(f_, g, dy)
        loss = jnp.broadcast_to(jnp.sum(e * e, keepdims=True) * (0.5 / D), (1, 128))
        return dy, df, dg, loss

    dy, df, g_norm_mlp_post, loss = LW.fwd(a2, "w_ff2", name="mm_ff2", epilogue=post2, out_dtypes=(F32, BF16),
                                           extras=[x1, tgt, vec["norm_mlp_post"]], reds=(D, 128))
    late_bufs = _late_grad(a2, df, "w_ff2", late_bufs, name="mm_dw_ff2")
    df1 = LW.bwd(df, "w_ff2", name="mm_df1", extras=[r1], out_dtypes=(BF16,),
              epilogue=lambda acc, r: (acc * (2.0 * r.astype(F32)),))
    late_bufs = _late_grad(h2, df1, "w_ff1", late_bufs, name="mm_dw_ff1")

    def bwd_mid(dh2_, dy_, x1_, m, g2, g1):
        d1, dg2 = _rms_bwd(x1_, g2, dh2_)
        dx1_ = dy_ + d1
        dm, dg1 = _rms_bwd(m, g1, dx1_)
        return dx1_, dm, dg2, dg1

    dx1, dmo, g_norm_mlp_pre, g_norm_mix_post = LW.bwd(
        df1, "w_ff1", name="mm_dh2", tm=512, epilogue=bwd_mid, out_dtypes=(F32, BF16), reds=(D, D),
        extras=[dy, x1, mo, vec["norm_mlp_pre"], vec["norm_mix_post"]])
    late_bufs = _late_grad(merged, dmo, "w_out", late_bufs, name="mm_dw_out")
    def dmerge(dm, ga, gc, ya, yc, b2):
        sga, sgc = _sigmoid(ga), _sigmoid(gc)
        dya = dm * sga
        dyc = dm * sgc
        dga = dm * ya * sga * (1.0 - sga)
        dgc = dm * (yc + b2) * sgc * (1.0 - sgc)
        return dya, dyc, jnp.concatenate([dga, dgc], axis=1), jnp.sum(dyc, axis=0, keepdims=True)

    dz = lax.empty((xs.shape[0], Z_COLS), BF16)
    tm_dm = _tile(xs.shape[0], 512)
    gates = pl.BlockSpec((tm_dm, 2 * D), lambda i, j, k: (i, Z_GA // 2))
    dya, dyc, dz, g_b_pw2 = LW.bwd(
        dmo, "w_out", name="mm_dmerged", tm=tm_dm, epilogue=dmerge, out_dtypes=(BF16, BF16, (dz, gates)),
        reds=(D,), extras=[(z, Z_GA), (z, Z_GC), y_attn, y_conv, vec["b_pw2"]])

    late_bufs = _late_grad(attn, dya, "w_o_attn", late_bufs, name="mm_dw_o_attn")
    late_bufs = _late_grad(u3, dyc, "w_pw2", late_bufs, name="mm_dw_pw2")
    late_token = send_late_grads(late_bufs)

    def delta_fn(acc, o_):
        do32 = acc.astype(BF16).astype(F32)
        pr = do32 * o_
        ln = lax.broadcasted_iota(jnp.int32, (pr.shape[0], NOPE), 1)
        cols = []
        for hd in range(N_HEADS):
            dl = jnp.sum(pr[:, hd * NOPE:(hd + 1) * NOPE], axis=-1, keepdims=True)
            hi = dl.astype(BF16).astype(F32)
            cols.append(do32[:, hd * NOPE:(hd + 1) * NOPE])
            cols.append(jnp.where(ln == 0, -hi, jnp.where(ln == 1, hi - dl, 0.0)))
        return (jnp.concatenate(cols, axis=1),)

    tm_do = _tile(xs.shape[0], 512)
    do_ext = LW.bwd(dya, "w_o_attn", name="mm_dattn", tm=tm_do, epilogue=delta_fn, extras=[attn],
                    after=[late_token], out_dtypes=(
                        (lax.empty((xs.shape[0], N_HEADS * QK_PAD), BF16),
                         pl.BlockSpec((tm_do, N_HEADS * QK_PAD), lambda i, j, k: (i, 0))),))
    dq, dk, dv = _flash_bwd(q, k, v, do_ext, lse)

    def unrope(dq_, dk_, dv_, cs_, sa_, sb_):
        dq_, dk_ = dq_.astype(F32), dk_.astype(F32)
        qs, kn = [], []
        dkr = jnp.zeros_like(cs_)
        for hd in range(N_HEADS):
            qs.append(dq_[:, hd * QK_PAD:hd * QK_PAD + NOPE] * SCALE)
            qs.append(_rope_t(dq_[:, hd * QK_PAD + NOPE:(hd + 1) * QK_PAD] * SCALE, cs_, sa_, sb_))
            kn.append(dk_[:, hd * QK_PAD:hd * QK_PAD + NOPE] * K_UNSCALE)
            dkr = dkr + dk_[:, hd * QK_PAD + NOPE:(hd + 1) * QK_PAD]
        return (jnp.concatenate(qs, axis=1), jnp.concatenate(kn + [dv_], axis=1),
                _rope_t(dkr * K_UNSCALE, cs_, sa_, sb_))

    dq_pre, dkv, dkr = _rowwise(
        unrope, [(dq, 2048, 0), (dk, 2048, 0), (dv, 1024, 0), (cs, 128, 0), (sa, 128, 0), (sb, 128, 0)], [],
        [(2048, BF16), (2048, BF16), (128, F32)], name="unrope")
    g_w_uq = _mm(cqn, dq_pre, ta=True, name="mm_dw_uq")
    dcqn = _mm(dq_pre, w_uq, tb=True, out_dtype=BF16, name="mm_dcqn")
    g_w_ukv = _mm(ckvn, dkv, ta=True, name="mm_dw_ukv")
    dckvn = _mm(dkv, w_ukv, tb=True, out_dtype=BF16, name="mm_dckvn")

    def small_bwd(zs, dcq_, dckv_, dkr_, qn, kvn):
        dcq, dqn = _rms_bwd(zs[:, :Q_RANK], qn, dcq_)
        dckv, dkvn = _rms_bwd(zs[:, Q_RANK:Q_RANK + KV_RANK], kvn, dckv_)
        pad = jnp.zeros((zs.shape[0], D - 768), F32)
        return jnp.concatenate([dcq, dckv, dkr_, pad], axis=1), dqn, dkvn

    dz, g_q_norm, g_kv_norm = _rowwise(
        small_bwd, [(z, D, Z_S), (dcqn, Q_RANK, 0), (dckvn, KV_RANK, 0), (dkr, 128, 0)],
        [vec["q_norm"], vec["kv_norm"]], [(D, dz, Z_S)], [(1, Q_RANK), (1, KV_RANK)], name="small_bwd")

    def ln_silu_bwd(d3, u, g, b):
        nh, rs, y = ln_parts(u, g, b)
        sg = _sigmoid(y)
        dyv = d3 * (sg * (1.0 + y * (1.0 - sg)))
        dnh = dyv * g
        du = rs * (dnh - jnp.mean(dnh, axis=-1, keepdims=True)
                   - nh * jnp.mean(dnh * nh, axis=-1, keepdims=True))
        return (du, jnp.sum(dyv * nh, axis=0, keepdims=True), jnp.sum(dyv, axis=0, keepdims=True),
                jnp.sum(du, axis=0, keepdims=True))

    du1, g_ln_g, g_ln_b, g_conv_b = LW.bwd(
        dyc, "w_pw2", name="mm_du3", tm=512, epilogue=ln_silu_bwd, out_dtypes=(F32,), reds=(D, D, D),
        extras=[u1, vec["conv_ln_g"], vec["conv_ln_b"]])
    dz, g_conv_w = _conv_bwd(z, du1, cw, dz)
    small_token = send_small_grads({
        "conv_b": g_conv_b, "conv_ln_g": g_ln_g, "conv_ln_b": g_ln_b, "b_pw2": g_b_pw2,
        "norm_mix_post": g_norm_mix_post, "norm_mlp_pre": g_norm_mlp_pre, "norm_mlp_post": g_norm_mlp_post,
        "q_norm": g_q_norm, "kv_norm": g_kv_norm}, g_conv_w, loss)
    g_w_in_t = _mm(dz, h, ta=True, name="mm_dw_in", after=[small_token])
    early_token = send_early_grads({
        "w_in_t": g_w_in_t,
        "w_uq": g_w_uq.reshape(Q_RANK, N_HEADS, QK_PAD)[:, :, :NOPE + ROPE].reshape(Q_RANK, 1536),
        "w_uk": g_w_ukv[:, :1024], "w_uv": g_w_ukv[:, 1024:]})

    def final(dh_, dx1_, x, g):
        d, dg = _rms_bwd(x, g, dh_)
        return dx1_ + d, dg

    grad_x, g_norm_mix_pre = _mm(
        dz, w_in_t, name="mm_dh", after=[early_token], tiles=(_tile(xs.shape[0], 512), D, Z_COLS),
        b_spec=pl.BlockSpec((Z_COLS, D), lambda i, j, k: (0, 0), pipeline_mode=pl.Buffered(1)),
        epilogue=final, out_dtypes=(F32,), reds=(D,), extras=[dx1, xs, vec["norm_mix_pre"]])

    return grad_x, g_norm_mix_pre


def _row1024(a):
    return jnp.pad(a, ((0, 0), (0, 1024 - a.shape[1])))


ORDER = ("norm_mix_pre", "w_in", "q_norm", "w_uq", "kv_norm", "w_uk", "w_uv", "w_o_attn", "conv_w",
         "conv_b", "conv_ln_g", "conv_ln_b", "w_pw2", "b_pw2", "w_out", "norm_mix_post", "norm_mlp_pre",
         "w_ff1", "w_ff2", "norm_mlp_post")


def kernel(x, positions, norm_mix_pre, w_in, q_norm, w_uq, kv_norm, w_uk, w_uv, w_o_attn, conv_w, conv_b, conv_ln_g, conv_ln_b, w_pw2, b_pw2, w_out, norm_mix_post, norm_mlp_pre, w_ff1, w_ff2, norm_mlp_post, loss_target, m_norm_mix_pre, m_w_in, m_q_norm, m_w_uq, m_kv_norm, m_w_uk, m_w_uv, m_w_o_attn, m_conv_w, m_conv_b, m_conv_ln_g, m_conv_ln_b, m_w_pw2, m_b_pw2, m_w_out, m_norm_mix_post, m_norm_mlp_pre, m_w_ff1, m_w_ff2, m_norm_mlp_post, v_norm_mix_pre, v_w_in, v_q_norm, v_w_uq, v_kv_norm, v_w_uk, v_w_uv, v_w_o_attn, v_conv_w, v_conv_b, v_conv_ln_g, v_conv_ln_b, v_w_pw2, v_b_pw2, v_w_out, v_norm_mix_post, v_norm_mlp_pre, v_w_ff1, v_w_ff2, v_norm_mlp_post):
    args = dict(locals())
    wts = {n: args[n] for n in ORDER}
    mom = {n: args["m_" + n] for n in ORDER}
    var = {n: args["v_" + n] for n in ORDER}
    T = x.shape[1]
    place = jnp.stack([2 * lax.axis_index("x") + lax.axis_index("y"), lax.axis_index("c")]).astype(jnp.int32)

    w_early = _pack_early_shards(wts)
    w_late = _pack_late_shards(wts)
    cw_own = jnp.pad(conv_w.reshape(CONV_W, -1), ((0, CONV_TAPS_PAD - CONV_W), (0, 0)))
    full, cw_all = _gather_weights(w_early, cw_own)
    full = lax.dynamic_update_slice(full, w_early[None], (place[0], 0, 0))
    cw_all = lax.dynamic_update_slice(cw_all, cw_own[None], (place[0], 0, 0))
    W = _unpack_early(full)
    cw = cw_all.transpose(1, 0, 2).reshape(CONV_TAPS_PAD, D_MODEL)
    sems, w_thru, land, token = _late_gather_start(w_late, full)
    vec = {n: wts[n] for n in SMALL_VECS}
    vec["norm_mix_pre"] = vec["norm_mix_pre"] + token[:1, :1]

    def late_weights(after):
        landed = _late_gather_wait(sems, w_thru, land, after)
        return _LateWeights(lax.dynamic_update_slice(landed, w_late[None], (place[0], 0, 0)))

    sent = {}

    def send_late_grads(bufs):
        sent["late"] = (bufs[0],) + _reduce_start(bufs[1], place, name="late_grads_start")
        return sent["late"][4]

    def send_small_grads(gs, g_conv_w, loss):
        rows = [_row1024(gs[n]) if n in gs else jnp.zeros((1, 1024), F32) for n in SMALL_VECS]
        svec = jnp.concatenate(rows + [_row1024(loss), jnp.zeros((SMALL_CONVW_ROW - SMALL_LOSS_ROW - 1, 1024), F32),
                                       g_conv_w], axis=0)
        sent["small"] = (svec,) + _reduce_start(svec, place, name="small_grads_start", whole=True)
        return sent["small"][4]

    def send_early_grads(g):
        g32 = _pack_early_grads(g)
        sent["early"] = (g32,) + _reduce_start(g32.astype(BF16), place, name="early_grads_start")
        return sent["early"][4]

    late_bufs = (lax.empty((N_CHIPS, LATE_ROWS, 1024), F32), lax.empty((N_CHIPS, LATE_ROWS, 1024), BF16))
    grad_x, g_norm_mix_pre = _local_step(x[0], positions.reshape(T, 1), loss_target[0], W, late_weights,
                                         late_bufs, send_late_grads, send_small_grads, send_early_grads, cw, vec)
    last = jnp.pad(g_norm_mix_pre, ((0, 7), (0, 0)))
    sent["last"] = (last,) + _reduce_start(last, place, name="last_grads_start", whole=True)

    def finish(group, after):
        g32, sems_, g_thru, land_, _ = sent[group]
        landed = _reduce_wait(sems_, g_thru, land_, after, name=group + "_grads_wait")
        red = _reduce_sum(place, g32, landed, name=group + "_grads_sum")
        return _join_halves(red, name=group + "_grads_join")

    def as2d(a):
        return a.reshape(-1, a.shape[-1]) if a.ndim != 4 or a.shape[2] == 1 else a.reshape(a.shape[1], -1)

    grads, delta, new_m, new_v = {}, {}, {}, {}

    def update(n):
        shp = wts[n].shape
        outs = _adamw(as2d(wts[n]), as2d(grads[n]), as2d(mom[n]), as2d(var[n]), name="adamw_" + n)
        delta[n], new_m[n], new_v[n] = (o.reshape(shp) for o in outs)

    g_late = finish("late", [sent["early"][4]])
    for n, off, rows in LATE:
        shp = wts[n].shape
        outs = _adamw(as2d(wts[n]), g_late, as2d(mom[n]), as2d(var[n]), g_row=off, name="adamw_" + n)
        delta[n], new_m[n], new_v[n], grads[n] = (o.reshape(shp) for o in outs)

    g_early = finish("early", [grad_x, delta["w_ff2"]])
    outs = _adamw(wts["w_in"][0].T, g_early[:1200], mom["w_in"][0].T, var["w_in"][0].T, name="adamw_w_in")
    delta["w_in"], new_m["w_in"], new_v["w_in"] = (o.T[None] for o in outs)
    grads["w_in"] = g_early[:1200].T[None]
    off = 1200
    for n, rows in EARLY[1:]:
        grads[n] = g_early[off:off + rows].reshape(wts[n].shape)
        off += rows
        update(n)

    def finish_small(group, after):
        svec, sems_, s_thru, land_, _ = sent[group]
        landed = _reduce_wait(sems_, s_thru, land_, after, name=group + "_grads_wait", whole=True)
        return _sum_small((2 * place[0] + place[1])[None], svec, landed, name=group + "_grads_sum")

    ssum = finish_small("small", [delta["w_in"]])
    for r, n in enumerate(SMALL_VECS):
        grads[n] = ssum[r:r + 1, :wts[n].shape[1]]
    grads["norm_mix_pre"] = finish_small("last", [ssum])[:1]
    cw_sum = ssum[SMALL_CONVW_ROW:SMALL_CONVW_ROW + CONV_W]
    grads["conv_w"] = lax.dynamic_slice_in_dim(cw_sum, place[0] * 256, 256, axis=1).reshape(conv_w.shape)
    loss_out = ssum[SMALL_LOSS_ROW, 0]
    update("conv_w")
    small = [jnp.concatenate([_row1024(t[n]) for n in SMALL_VECS], axis=0) for t in (wts, grads, mom, var)]
    outs = _adamw(*small, name="adamw_small")
    for r, n in enumerate(SMALL_VECS):
        delta[n], new_m[n], new_v[n] = (o[r:r + 1, :wts[n].shape[1]] for o in outs)

    return (loss_out, grad_x[None], *[grads[n] for n in ORDER], *[delta[n] for n in ORDER],
            *[new_m[n] for n in ORDER], *[new_v[n] for n in ORDER])
```

```python
import functools

import numpy as np
import jax
import jax.numpy as jnp
from jax import lax
from jax.experimental import pallas as pl
from jax.experimental.pallas import tpu as pltpu

F32 = jnp.float32
BF16 = jnp.bfloat16
MESH = pl.DeviceIdType.MESH

D_MODEL = 1024
N_HEADS = 8
NOPE = 128
ROPE = 64
QK_PAD = 256
Q_RANK = 384
KV_RANK = 256
CONV_W = 31
CONV_TAPS_PAD = 32
HALO = 32
EPS = 1e-6
ROPE_THETA = 10000.0
SCALE = float((NOPE + ROPE) ** -0.5)
Q_SCALE = SCALE * float(np.log2(np.e))
K_UNSCALE = float(np.log(2.0))
Z_A, Z_B, Z_GA, Z_GC, Z_S = range(5)
Z_COLS = 5 * D_MODEL
N_SMALL = Q_RANK + KV_RANK + ROPE
N_CHIPS = 4
N_DEV = 8

ADAM_LR = 0.001
ADAM_B1 = 0.9
ADAM_B2 = 0.999
ADAM_EPS = 1e-08
ADAM_WD = 0.01
ADAM_STEP = 10

VMEM_LIMIT = 48 * 1024 * 1024
NEG = -1e30

EARLY = (("w_in", 1200), ("w_uq", 144), ("w_uk", 64), ("w_uv", 64))
EARLY_ROWS = 1536
LATE = (("w_ff1", 0, 1024), ("w_o_attn", 1024, 256), ("w_pw2", 1536, 256), ("w_out", 1792, 256),
        ("w_ff2", 2048, 1024))
LATE_OFF = {n: off for n, off, _ in LATE}
LATE_ROWS = 3072
SUM_TILE = 256

SMALL_VECS = ("norm_mix_pre", "conv_b", "conv_ln_g", "conv_ln_b", "b_pw2", "norm_mix_post",
              "norm_mlp_pre", "norm_mlp_post", "q_norm", "kv_norm")
SMALL_LOSS_ROW = 10
SMALL_CONVW_ROW = 16


ANY = pl.BlockSpec(memory_space=pl.ANY)


def _params(*sem):
    return pltpu.CompilerParams(dimension_semantics=sem, vmem_limit_bytes=VMEM_LIMIT)


def _tile(n, cap):
    if n <= cap:
        return n
    t = (cap // 128) * 128
    while n % t:
        t -= 128
    return t


def _row_tile(n, cap):
    if n <= cap:
        return n
    t = (cap // 8) * 8
    while n % t:
        t -= 8
    return t


MM_VMEM_BUDGET = 40 * 1024 * 1024


def _mm_tiles(M, N, K, in_bytes, out_bytes):
    tk = K
    while True:
        best = None
        for tn in sorted({_tile(N, cap) for cap in (2560, 2048, 1024)}, reverse=True):
            for tm in sorted({_tile(M, cap) for cap in (1024, 512, 256)}, reverse=True):
                need = 2 * tk * (tm * in_bytes[0] + tn * in_bytes[1]) + 2 * tm * tn * out_bytes
                need += tm * tn * 4 if tk < K else 0
                steps = (M // tm) * (N // tn)
                if need <= MM_VMEM_BUDGET and (best is None or steps < best[0]):
                    best = (steps, tm, tn)
        if best is not None:
            return best[1], best[2], tk
        tk = _tile(K, tk - 128)


def _mm(a, b, *, name, ta=False, tb=False, out_dtype=F32, extras=(), epilogue=None, out_dtypes=None,
        after=(), shape=None, tiles=None, b_spec=None, reds=(), k_slabs=False):
    if shape is None:
        M, K = (a.shape[1], a.shape[0]) if ta else a.shape
        N = b.shape[0] if tb else b.shape[1]
        assert K == (b.shape[1] if tb else b.shape[0]), (a.shape, b.shape, ta, tb)
    else:
        M, N, K = shape
    outs = tuple(out_dtypes or (out_dtype,))
    placed = [o for o in outs if isinstance(o, tuple)]
    ex_arrays = [e[0] if isinstance(e, tuple) else e for e in extras]
    if tiles is None:
        out_bytes = sum(jnp.dtype(d).itemsize for d in outs) + sum(e.dtype.itemsize for e in ex_arrays)
        tiles = _mm_tiles(M, N, K, (a.dtype.itemsize, b.dtype.itemsize), out_bytes)
    tm, tn, tk = tiles
    nk = K // tk
    dims = (((0 if ta else 1,), (1 if tb else 0,)), ((), ()))
    n_ex, n_out = len(extras), len(outs)
    n_pass = len(after) + len(placed)
    assert not reds or tn == N

    def body(*refs):
        a_ref, b_ref = refs[:2]
        ex_refs = refs[2:2 + n_ex]
        o_refs = refs[2 + n_ex + n_pass:2 + n_ex + n_pass + n_out]
        red_refs = refs[2 + n_ex + n_pass + n_out:2 + n_ex + n_pass + n_out + len(reds)]
        first_rows, k = pl.program_id(0) == 0, pl.program_id(2)
        bv = b_ref[...].astype(BF16)
        if k_slabs:
            kc = bv.shape[-1]
            part = sum(lax.dot_general(a_ref[:, p * kc:(p + 1) * kc].astype(BF16), bv[p], dims,
                                       preferred_element_type=F32) for p in range(bv.shape[0]))
        else:
            if bv.ndim == 3:
                bv = bv.reshape(-1, bv.shape[-1])
            part = lax.dot_general(a_ref[...].astype(BF16), bv, dims, preferred_element_type=F32)

        def finish(acc):
            ex = [r[...] for r in ex_refs]
            ex = [e.astype(F32) if e.dtype == BF16 else e for e in ex]
            res = epilogue(acc, *ex) if epilogue else (acc,) * n_out
            for o_ref, r in zip(o_refs, res):
                o_ref[...] = r.astype(o_ref.dtype)
            for red_ref, r in zip(red_refs, res[n_out:]):
                red_ref[...] += r

        if reds:
            @pl.when(first_rows & (k == 0))
            def _():
                for red_ref in red_refs:
                    red_ref[...] = jnp.zeros_like(red_ref)

        if nk == 1:
            finish(part)
        else:
            acc_ref = refs[-1]

            @pl.when(k == 0)
            def _():
                acc_ref[...] = part

            @pl.when((k > 0) & (k < nk - 1))
            def _():
                acc_ref[...] += part

            @pl.when(k == nk - 1)
            def _():
                finish(acc_ref[...] + part)

    a_spec = (pl.BlockSpec((tk, tm), lambda i, j, k: (k, i)) if ta
              else pl.BlockSpec((tm, tk), lambda i, j, k: (i, k)))
    if b_spec is None:
        b_spec = (pl.BlockSpec((tn, tk), lambda i, j, k: (j, k)) if tb
                  else pl.BlockSpec((tk, tn), lambda i, j, k: (k, j)))
    o_spec = pl.BlockSpec((tm, tn), lambda i, j, k: (i, j))
    row_spec = pl.BlockSpec((1, tn), lambda i, j, k: (0, j))
    ex_specs = []
    for e in extras:
        if isinstance(e, tuple):
            ex_specs.append(pl.BlockSpec((tm, tn), functools.partial(lambda i, j, k, cb: (i, cb), cb=e[1])))
        else:
            ex_specs.append(row_spec if e.shape[0] == 1 else o_spec)
    out_shape, out_specs, aliases = [], [], {}
    for o, out in enumerate(outs):
        if isinstance(out, tuple):
            out_shape.append(jax.ShapeDtypeStruct(out[0].shape, out[0].dtype))
            out_specs.append(out[1])
            aliases[2 + n_ex + len(after) + len(aliases)] = o
        else:
            out_shape.append(jax.ShapeDtypeStruct((M, N), out))
            out_specs.append(o_spec)
    passed = list(after) + [buf for buf, _ in placed]
    out_shape = out_shape + [jax.ShapeDtypeStruct((1, w), F32) for w in reds]
    out_specs = out_specs + [pl.BlockSpec((1, w), lambda i, j, k: (0, 0)) for w in reds]
    res = pl.pallas_call(
        body, name=name, out_shape=out_shape,
        grid=(M // tm, N // tn, nk),
        in_specs=[a_spec, b_spec] + ex_specs + [ANY] * len(passed),
        out_specs=out_specs, input_output_aliases=aliases,
        scratch_shapes=[pltpu.VMEM((tm, tn), F32)] if nk > 1 else [],
        compiler_params=_params(*(("arbitrary",) * 3 if reds else ("parallel", "parallel", "arbitrary"))),
    )(a, b, *ex_arrays, *passed)
    return res[0] if len(res) == 1 else res


class _LateWeights:
    def __init__(self, pack):
        self.pack = pack

    def fwd(self, x, n, *, name, tm=1024, **kw):
        T, off = x.shape[0], LATE_OFF[n]
        if n == "w_ff1":
            return _mm(x, self.pack, name=name, shape=(T, 4096, 1024), tiles=(_tile(T, tm), 1024, 1024),
                       b_spec=pl.BlockSpec((None, 1024, 1024), lambda i, j, k: (j, 0, 0)), **kw)
        if n == "w_ff2":
            view = self.pack.reshape(N_CHIPS, LATE_ROWS // 1024, 1024, 1024)
            return _mm(x, view, name=name, shape=(T, 1024, 4096), tiles=(_tile(T, min(tm, 512)), 1024, 4096),
                       b_spec=pl.BlockSpec((N_CHIPS, None, 1024, 1024), lambda i, j, k: (0, off // 1024, 0, 0)),
                       **kw)
        return _mm(x, self.pack.reshape(N_CHIPS, LATE_ROWS // 256, 256, 1024), name=name,
                   shape=(T, 1024, 1024), tiles=(_tile(T, tm), 1024, 1024), b_spec=self._whole(off), **kw)

    @staticmethod
    def _whole(off):
        return pl.BlockSpec((N_CHIPS, None, 256, 1024), lambda i, j, k: (0, off // 256, 0, 0))

    def bwd(self, dy, n, *, name, tm=1024, **kw):
        T, off = dy.shape[0], LATE_OFF[n]
        tm = _tile(T, tm)
        if n == "w_ff1":
            view = self.pack.reshape(N_CHIPS, LATE_ROWS // 1024, 1024, 1024)
            return _mm(dy, view, tb=True, name=name, shape=(T, 1024, 4096), tiles=(tm, 1024, 4096),
                       b_spec=pl.BlockSpec((N_CHIPS, None, 1024, 1024), lambda i, j, k: (0, off // 1024, 0, 0),
                                           pipeline_mode=pl.Buffered(1)), k_slabs=True, **kw)
        if n == "w_ff2":
            return _mm(dy, self.pack, tb=True, name=name, shape=(T, 4096, 1024), tiles=(tm, 1024, 1024),
                       b_spec=pl.BlockSpec((None, 1024, 1024), lambda i, j, k: (j, off // 1024, 0)), **kw)
        return _mm(dy, self.pack.reshape(N_CHIPS, LATE_ROWS // 256, 256, 1024), tb=True, name=name,
                   shape=(T, 1024, 1024), tiles=(tm, 1024, 1024), b_spec=self._whole(off), **kw)


def _late_grad(x, dy, n, bufs, *, name):
    T, off = x.shape[0], LATE_OFF[n]
    if n == "w_ff1":
        spec = pl.BlockSpec((None, 512, 1024), lambda i, j, k: (j, i, 0))
        shape, tiles = (1024, 4096, T), (512, 1024, T)
    elif n == "w_ff2":
        spec = pl.BlockSpec((None, 512, 1024), lambda i, j, k: (i // 2, off // 512 + i % 2, 0))
        shape, tiles = (4096, 1024, T), (512, 1024, T)
    else:
        spec = pl.BlockSpec((None, 256, 1024), lambda i, j, k: (i, off // 256, 0))
        shape, tiles = (1024, 1024, T), (256, 1024, T)
    return _mm(x, dy, ta=True, name=name, shape=shape, tiles=tiles, out_dtypes=[(b, spec) for b in bufs])


def _rowwise(fn, rows, consts, outs, reds=(), *, name, tm=256, after=()):
    T = rows[0][0].shape[0]
    tm = min(tm, T)
    n_row, n_const, n_out = len(rows), len(consts), len(outs)
    placed = [o for o in outs if len(o) == 3]
    n_pass = len(after) + len(placed)

    def body(*refs):
        i = pl.program_id(0)
        vals = [r[...] for r in refs[:n_row + n_const]]
        vals = [v.astype(F32) if v.dtype == BF16 else v for v in vals]
        res = fn(*vals)
        out_refs = refs[n_row + n_const + n_pass:]
        for k in range(n_out):
            out_refs[k][...] = res[k].astype(out_refs[k].dtype)
        for k in range(len(reds)):
            ref = out_refs[n_out + k]

            @pl.when(i == 0)
            def _(ref=ref):
                ref[...] = jnp.zeros_like(ref)

            ref[...] += res[n_out + k]

    in_specs = [pl.BlockSpec((tm, w), functools.partial(lambda i, cb: (i, cb), cb=cb))
                for (_, w, cb) in rows]
    in_specs += [pl.BlockSpec(c.shape, lambda i: (0, 0)) for c in consts]
    in_specs += [ANY] * n_pass
    out_specs, out_shape, aliases = [], [], {}
    for o, out in enumerate(outs):
        if len(out) == 3:
            w, buf, cb = out
            out_specs.append(pl.BlockSpec((tm, w), functools.partial(lambda i, cb: (i, cb), cb=cb)))
            out_shape.append(jax.ShapeDtypeStruct(buf.shape, buf.dtype))
            aliases[n_row + n_const + len(after) + len(aliases)] = o
        else:
            out_specs.append(pl.BlockSpec((tm, out[0]), lambda i: (i, 0)))
            out_shape.append(jax.ShapeDtypeStruct((T, out[0]), out[1]))
    out_specs += [pl.BlockSpec(s, lambda i: (0, 0)) for s in reds]
    out_shape += [jax.ShapeDtypeStruct(s, F32) for s in reds]
    return pl.pallas_call(
        body, name=name, out_shape=out_shape, grid=(T // tm,),
        in_specs=in_specs, out_specs=out_specs, input_output_aliases=aliases,
        compiler_params=_params("arbitrary"),
    )(*[r[0] for r in rows], *consts, *after, *[o[1] for o in placed])


def _rms(x, g):
    r = lax.rsqrt(jnp.mean(x * x, axis=-1, keepdims=True) + EPS)
    return x * r * g


def _rms_bwd(x, g, dy):
    r = lax.rsqrt(jnp.mean(x * x, axis=-1, keepdims=True) + EPS)
    n = x * r
    dyg = dy * g
    dx = r * (dyg - n * jnp.mean(dyg * n, axis=-1, keepdims=True))
    return dx, jnp.sum(dy * n, axis=0, keepdims=True)


def _sigmoid(x):
    return 1.0 / (1.0 + jnp.exp(-x))


def _rope(x, cs, sa, sb):
    return x * cs + pltpu.roll(x, 32, 1) * sa + pltpu.roll(x, 96, 1) * sb


def _rope_t(dy, cs, sa, sb):
    return dy * cs + pltpu.roll(dy * sa, 96, 1) + pltpu.roll(dy * sb, 32, 1)


FLASH_FWD_TQ = 4096
FLASH_FWD_TK = 1024
FLASH_BWD_TQ = 1024
FLASH_BWD_TK = 512
NT_DIMS = (((1,), (1,)), ((), ()))
TN_DIMS = (((0,), (0,)), ((), ()))


def _flash_fwd(q, k, v, *, tq=FLASH_FWD_TQ, tk=FLASH_FWD_TK):
    T = q.shape[0]
    tq, tk = min(tq, T), min(tk, T)
    n_diag = tq // tk
    assert n_diag >= 1

    def body(q_ref, k_ref, v_ref, o_ref, lse_ref):
        i = pl.program_id(1)

        def logits(j, r0=0):
            sl = pl.ds(pl.multiple_of(j * tk, tk), tk)
            return lax.dot_general(q_ref[r0:tq, :], k_ref[sl, :], NT_DIMS, preferred_element_type=F32)

        def update(j, s, m, acc, masked, r0=0):
            sl = pl.ds(pl.multiple_of(j * tk, tk), tk)
            if masked:
                row = i * tq + r0 + lax.broadcasted_iota(jnp.int32, (tq - r0, tk), 0)
                col = j * tk + lax.broadcasted_iota(jnp.int32, (tq - r0, tk), 1)
                s = jnp.where(col <= row, s, NEG)
            m_old, acc_old = m[r0:], acc[r0:]
            m_new = jnp.maximum(m_old, jnp.max(s, axis=-1, keepdims=True))
            p = jnp.exp2(s - m_new)
            acc_new = (jnp.exp2(m_old - m_new) * acc_old
                       + jnp.dot(p.astype(BF16), v_ref[sl, :], preferred_element_type=F32))
            if r0:
                m_new = jnp.concatenate([m[:r0], m_new], axis=0)
                acc_new = jnp.concatenate([acc[:r0], acc_new], axis=0)
            return m_new, acc_new

        def step(j, carry):
            s, m, acc = carry
            s_next = logits(j + 1)
            m, acc = update(j, s, m, acc, False)
            return s_next, m, acc

        first = i * n_diag
        init = (logits(0), jnp.full((tq, 1), NEG, F32), jnp.zeros((tq, QK_PAD), F32))
        s, m, acc = lax.fori_loop(0, first, step, init)
        for d in range(n_diag):
            m, acc = update(first + d, s, m, acc, True, r0=d * tk)
            if d + 1 < n_diag:
                s = logits(first + d + 1, r0=(d + 1) * tk)
        l = acc[:, NOPE:]
        o_ref[...] = (acc[:, :NOPE] / l).astype(BF16)
        lse_ref[...] = m + jnp.log2(l)

    return pl.pallas_call(
        body, name="flash_fwd",
        out_shape=[jax.ShapeDtypeStruct((T, N_HEADS * NOPE), BF16),
                   jax.ShapeDtypeStruct((T, N_HEADS * NOPE), F32)],
        grid=(N_HEADS, T // tq),
        in_specs=[pl.BlockSpec((tq, QK_PAD), lambda h, i: (i, h)),
                  pl.BlockSpec((T, QK_PAD), lambda h, i: (0, h)),
                  pl.BlockSpec((T, QK_PAD), lambda h, i: (0, h))],
        out_specs=[pl.BlockSpec((tq, NOPE), lambda h, i: (i, h)),
                   pl.BlockSpec((tq, NOPE), lambda h, i: (i, h))],
        compiler_params=_params("parallel", "arbitrary"),
    )(q, k, v)


def _flash_bwd(q, k, v, do, lse, *, tq=FLASH_BWD_TQ, tk=FLASH_BWD_TK):
    T = q.shape[0]
    tq, tk = min(tq, T), min(tk, T)

    def body(q_ref, k_ref, v_ref, do_ref, lse_ref, dq_ref, dk_ref, dv_ref, dq_acc):
        dq_acc[...] = jnp.zeros_like(dq_acc)
        for j in range(T // tk):
            k0 = j * tk
            kj = k_ref[k0:k0 + tk, :]
            vj = v_ref[k0:k0 + tk, :]
            dk = jnp.zeros((tk, QK_PAD), F32)
            dv = jnp.zeros((tk, NOPE), F32)
            for i in range(k0 // tq, T // tq):
                r0 = max(i * tq, k0)
                r1 = (i + 1) * tq
                qi = q_ref[r0:r1, :]
                doi = do_ref[r0:r1, :]
                s = lax.dot_general(qi, kj, NT_DIMS, preferred_element_type=F32)
                p = jnp.exp2(s - lse_ref[r0:r1, :][:, :1])
                if r0 < k0 + tk:
                    row = r0 + lax.broadcasted_iota(jnp.int32, (r1 - r0, tk), 0)
                    col = k0 + lax.broadcasted_iota(jnp.int32, (r1 - r0, tk), 1)
                    p = jnp.where(col <= row, p, 0.0)
                dv = dv + lax.dot_general(p.astype(BF16), doi[:, :NOPE], TN_DIMS, preferred_element_type=F32)
                ds = (p * lax.dot_general(doi, vj, NT_DIMS, preferred_element_type=F32)).astype(BF16)
                dk = dk + lax.dot_general(ds, qi, TN_DIMS, preferred_element_type=F32)
                dq_acc[r0:r1, :] += jnp.dot(ds, kj, preferred_element_type=F32)
            dk_ref[k0:k0 + tk, :] = dk.astype(BF16)
            dv_ref[k0:k0 + tk, :] = dv.astype(BF16)
        dq_ref[...] = dq_acc[...].astype(BF16)

    return pl.pallas_call(
        body, name="flash_bwd",
        out_shape=[jax.ShapeDtypeStruct((T, N_HEADS * QK_PAD), BF16),
                   jax.ShapeDtypeStruct((T, N_HEADS * QK_PAD), BF16),
                   jax.ShapeDtypeStruct((T, N_HEADS * NOPE), BF16)],
        grid=(N_HEADS,),
        in_specs=[pl.BlockSpec((T, QK_PAD), lambda h: (0, h)),
                  pl.BlockSpec((T, QK_PAD), lambda h: (0, h)),
                  pl.BlockSpec((T, QK_PAD), lambda h: (0, h)),
                  pl.BlockSpec((T, QK_PAD), lambda h: (0, h)),
                  pl.BlockSpec((T, NOPE), lambda h: (0, h))],
        out_specs=[pl.BlockSpec((T, QK_PAD), lambda h: (0, h)),
                   pl.BlockSpec((T, QK_PAD), lambda h: (0, h)),
                   pl.BlockSpec((T, NOPE), lambda h: (0, h))],
        scratch_shapes=[pltpu.VMEM((T, QK_PAD), F32)],
        compiler_params=_params("arbitrary"),
    )(q, k, v, do, lse)


SUB = 128


def _windows(buf, t0, c0, first, last):
    span = SUB + 8 * ((last + 7) // 8)
    base = buf[t0:t0 + span, c0:c0 + SUB]
    for r in range(8):
        offs = [o for o in range(first, last + 1) if o % 8 == r]
        if offs:
            win = base if r == 0 else pltpu.roll(base, span - r, 0)
            for o in offs:
                yield o, win[o - r:o - r + SUB, :]


def _glu(a, b):
    return a.astype(F32) * _sigmoid(b.astype(F32))


def _conv_fwd(z, w, bias, *, tm=256):
    T, C = z.shape[0], D_MODEL
    tm = min(tm, T)
    hb = tm // HALO
    lead = HALO - (CONV_W - 1)

    def body(a_ref, b_ref, ah_ref, bh_ref, w_ref, bias_ref, o_ref, buf):
        t = pl.program_id(0)
        buf[0:HALO, :] = jnp.where(t > 0, _glu(ah_ref[...], bh_ref[...]), 0.0)
        buf[HALO:, :] = _glu(a_ref[...], b_ref[...])
        for c0 in range(0, C, SUB):
            for t0 in range(0, tm, SUB):
                acc = jnp.broadcast_to(bias_ref[:, c0:c0 + SUB], (SUB, SUB))
                for o, win in _windows(buf, t0, c0, lead, lead + CONV_W - 1):
                    acc = acc + win * w_ref[o - lead:o - lead + 1, c0:c0 + SUB]
                o_ref[t0:t0 + SUB, c0:c0 + SUB] = acc

    def halo(cb):
        return pl.BlockSpec((HALO, C), lambda t: (jnp.maximum(t * hb - 1, 0), cb))

    return pl.pallas_call(
        body, name="conv_fwd",
        out_shape=jax.ShapeDtypeStruct((T, C), F32),
        grid=(T // tm,),
        in_specs=[pl.BlockSpec((tm, C), lambda t: (t, Z_A)), pl.BlockSpec((tm, C), lambda t: (t, Z_B)),
                  halo(Z_A), halo(Z_B),
                  pl.BlockSpec((CONV_TAPS_PAD, C), lambda t: (0, 0)), pl.BlockSpec((1, C), lambda t: (0, 0))],
        out_specs=pl.BlockSpec((tm, C), lambda t: (t, 0)),
        scratch_shapes=[pltpu.VMEM((HALO + tm, C), F32)],
        compiler_params=_params("parallel"),
    )(z, z, z, z, w, bias)


def _conv_bwd(z, du1, w, dz, *, tm=256):
    T, C = z.shape[0], D_MODEL
    tm = min(tm, T)
    hb = tm // HALO
    nt = T // tm
    last_halo = T // HALO - 1
    lead = HALO - (CONV_W - 1)
    assert (Z_A, Z_B) == (0, 1)

    def body(a_ref, b_ref, ah_ref, bh_ref, d_ref, dh_ref, w_ref, dz_in, dz_ref, dw_ref, ubuf, dbuf, dw8, gate):
        t = pl.program_id(0)
        ubuf[0:HALO, :] = jnp.where(t > 0, _glu(ah_ref[...], bh_ref[...]), 0.0)
        gate[...] = _sigmoid(b_ref[...].astype(F32))
        ubuf[HALO:, :] = a_ref[...].astype(F32) * gate[...]
        dbuf[0:tm, :] = d_ref[...]
        dbuf[tm:, :] = jnp.where(t < nt - 1, dh_ref[...], 0.0)

        @pl.when(t == 0)
        def _():
            dw8[...] = jnp.zeros_like(dw8)

        for c0 in range(0, C, SUB):
            for t0 in range(0, tm, SUB):
                acc = jnp.zeros((SUB, SUB), F32)
                for o, win in _windows(dbuf, t0, c0, 0, CONV_W - 1):
                    kk = CONV_W - 1 - o
                    acc = acc + win * w_ref[kk:kk + 1, c0:c0 + SUB]
                av = a_ref[t0:t0 + SUB, c0:c0 + SUB].astype(F32)
                sg = gate[t0:t0 + SUB, c0:c0 + SUB]
                dz_ref[t0:t0 + SUB, c0:c0 + SUB] = (acc * sg).astype(dz_ref.dtype)
                dz_ref[t0:t0 + SUB, C + c0:C + c0 + SUB] = (acc * av * sg * (1.0 - sg)).astype(dz_ref.dtype)
                dsub = d_ref[t0:t0 + SUB, c0:c0 + SUB]
                for o, win in _windows(ubuf, t0, c0, lead, lead + CONV_W - 1):
                    kk = o - lead
                    dw8[8 * kk:8 * kk + 8, c0:c0 + SUB] += jnp.sum(
                        (dsub * win).reshape(SUB // 8, 8, SUB), axis=0)

        @pl.when(t == nt - 1)
        def _():
            dw_ref[...] = jnp.sum(dw8[...].reshape(CONV_TAPS_PAD, 8, C), axis=1)

    def halo(cb):
        return pl.BlockSpec((HALO, C), lambda t: (jnp.maximum(t * hb - 1, 0), cb))

    return pl.pallas_call(
        body, name="conv_bwd",
        out_shape=[jax.ShapeDtypeStruct(dz.shape, dz.dtype), jax.ShapeDtypeStruct((CONV_TAPS_PAD, C), F32)],
        grid=(nt,),
        in_specs=[pl.BlockSpec((tm, C), lambda t: (t, Z_A)), pl.BlockSpec((tm, C), lambda t: (t, Z_B)),
                  halo(Z_A), halo(Z_B),
                  pl.BlockSpec((tm, C), lambda t: (t, 0)),
                  pl.BlockSpec((HALO, C), lambda t: (jnp.minimum((t + 1) * hb, last_halo), 0)),
                  pl.BlockSpec((CONV_TAPS_PAD, C), lambda t: (0, 0)), ANY],
        out_specs=[pl.BlockSpec((tm, 2 * C), lambda t: (t, 0)),
                   pl.BlockSpec((CONV_TAPS_PAD, C), lambda t: (0, 0))],
        input_output_aliases={7: 0},
        scratch_shapes=[pltpu.VMEM((HALO + tm, C), F32), pltpu.VMEM((tm + HALO, C), F32),
                        pltpu.VMEM((CONV_TAPS_PAD * 8, C), F32), pltpu.VMEM((tm, C), F32)],
        compiler_params=_params("arbitrary"),
    )(z, z, z, z, du1, du1, w, dz)


def _adamw(w, g, m, v, *, name, g_row=None):
    R, C = w.shape
    tr = _row_tile(R, max(8, (1 << 18) // C // 8 * 8))
    first = 0 if g_row is None else g_row // tr
    assert g_row is None or g_row % tr == 0

    def body(w_ref, g_ref, m_ref, v_ref, d_ref, nm_ref, nv_ref, *g_out):
        gv = g_ref[...]
        nm = ADAM_B1 * m_ref[...] + (1.0 - ADAM_B1) * gv
        nv = ADAM_B2 * v_ref[...] + (1.0 - ADAM_B2) * (gv * gv)
        m_hat = nm / (1.0 - ADAM_B1 ** ADAM_STEP)
        v_hat = nv / (1.0 - ADAM_B2 ** ADAM_STEP)
        d_ref[...] = -ADAM_LR * (m_hat / (jnp.sqrt(v_hat) + ADAM_EPS) + ADAM_WD * w_ref[...])
        nm_ref[...] = nm
        nv_ref[...] = nv
        for ref in g_out:
            ref[...] = gv

    spec = pl.BlockSpec((tr, C), lambda i: (i, 0))
    g_spec = pl.BlockSpec((tr, C), lambda i: (first + i, 0))
    n_out = 3 if g_row is None else 4
    return pl.pallas_call(
        body, name=name, out_shape=[jax.ShapeDtypeStruct((R, C), F32)] * n_out,
        grid=(R // tr,), in_specs=[spec, g_spec, spec, spec], out_specs=[spec] * n_out,
        compiler_params=_params("parallel"),
    )(w, g, m, v)


def _coords():
    return lax.axis_index("x"), lax.axis_index("y"), lax.axis_index("c")


def _remote(src, dst, ssem, rsem, to):
    return pltpu.make_async_remote_copy(src_ref=src, dst_ref=dst, send_sem=ssem, recv_sem=rsem,
                                        device_id=to, device_id_type=MESH)


def _half(c, rows):
    return pl.ds(pl.multiple_of(c * (rows // 2), 16), rows // 2)


def _gather_weights(wpack, cw):
    def body(w_ref, cw_ref, full_ref, cwf_ref, ssem, rsem):
        x, y, c = _coords()
        q = 2 * x + y
        sib = (x, y, 1 - c)
        chips = [(1 - x, y), (x, 1 - y), (1 - x, 1 - y)]
        mine, other = _half(c, w_ref.shape[0]), _half(1 - c, w_ref.shape[0])
        first = []
        for j, (px, py) in enumerate(chips):
            first.append(_remote(w_ref.at[mine], full_ref.at[q, mine], ssem.at[j], rsem.at[j], (px, py, c)))
            first.append(_remote(cw_ref, cwf_ref.at[q], ssem.at[6 + j], rsem.at[6 + j], (px, py, c)))
        for cp in first:
            cp.start()
        passed = []
        for j, (px, py) in enumerate(chips):
            pq = 2 * px + py
            _remote(w_ref.at[mine], full_ref.at[pq, mine], ssem.at[j], rsem.at[j], (px, py, c)).wait_recv()
            fw = _remote(full_ref.at[pq, mine], full_ref.at[pq, mine], ssem.at[3 + j], rsem.at[3 + j], sib)
            fw.start()
            passed.append(fw)
        for j, (px, py) in enumerate(chips):
            pq = 2 * px + py
            _remote(full_ref.at[pq, other], full_ref.at[pq, other], ssem.at[3 + j], rsem.at[3 + j], sib).wait_recv()
            _remote(cw_ref, cwf_ref.at[pq], ssem.at[6 + j], rsem.at[6 + j], (px, py, c)).wait_recv()
        for cp in first + passed:
            cp.wait_send()

    return pl.pallas_call(
        body, name="gather_weights",
        out_shape=[jax.ShapeDtypeStruct((N_CHIPS,) + wpack.shape, wpack.dtype),
                   jax.ShapeDtypeStruct((N_CHIPS,) + cw.shape, cw.dtype)],
        in_specs=[ANY, ANY], out_specs=[ANY, ANY],
        scratch_shapes=[pltpu.SemaphoreType.DMA((9,)), pltpu.SemaphoreType.DMA((9,))],
    )(wpack, cw)


SEM = pl.BlockSpec(memory_space=pltpu.SEMAPHORE)
HBM = pl.BlockSpec(memory_space=pltpu.HBM)
N_LATE = 6


def _late_peers(x, y, c):
    out = []
    for j, (px, py) in enumerate([(1 - x, y), (x, 1 - y), (1 - x, 1 - y)]):
        for t in (0, 1):
            out.append((2 * j + t, px, py, c if t == 0 else 1 - c))
    return out


def _late_gather_start(w, after):
    rows = w.shape[0]

    def body(w_ref, land_ref, after_ref, *outs):
        sems = outs[:2 * N_LATE]
        token = outs[2 * N_LATE + 2]
        x, y, c = _coords()
        q = 2 * x + y
        mine = _half(c, rows)
        for k, px, py, pc in _late_peers(x, y, c):
            _remote(w_ref.at[mine], land_ref.at[q, mine], sems[k], sems[N_LATE + k], (px, py, pc)).start()
        token[...] = jnp.zeros_like(token)

    res = pl.pallas_call(
        body, name="late_gather_start",
        out_shape=tuple([pltpu.SemaphoreType.DMA(())] * (2 * N_LATE)) + (
            pltpu.HBM(w.shape, w.dtype), pltpu.HBM((N_CHIPS,) + w.shape, w.dtype),
            jax.ShapeDtypeStruct((8, 128), F32)),
        in_specs=(HBM, HBM, ANY),
        out_specs=tuple([SEM] * (2 * N_LATE)) + (HBM, HBM, pl.BlockSpec(memory_space=pltpu.VMEM)),
        input_output_aliases={0: 2 * N_LATE, 1: 2 * N_LATE + 1},
        compiler_params=pltpu.CompilerParams(has_side_effects=pltpu.SideEffectType.DATAFLOW_SIDE_EFFECTING),
    )(pltpu.with_memory_space_constraint(w, pltpu.HBM),
      pltpu.with_memory_space_constraint(lax.empty((N_CHIPS,) + w.shape, w.dtype), pltpu.HBM), after)
    return res[:2 * N_LATE], res[2 * N_LATE], res[2 * N_LATE + 1], res[2 * N_LATE + 2]


def _late_gather_wait(sems, w_thru, land_thru, after):
    rows = w_thru.shape[0]

    def body(w_ref, land_ref, *rest):
        sems = rest[:2 * N_LATE]
        x, y, c = _coords()
        for k, px, py, pc in _late_peers(x, y, c):
            cp = _remote(w_ref.at[_half(pc, rows)], land_ref.at[2 * px + py, _half(pc, rows)],
                         sems[k], sems[N_LATE + k], (px, py, pc))
            cp.wait_send()
            cp.wait_recv()

    return pl.pallas_call(
        body, name="late_gather_wait",
        out_shape=(pltpu.HBM(w_thru.shape, w_thru.dtype), pltpu.HBM(land_thru.shape, land_thru.dtype)),
        in_specs=(HBM, HBM) + tuple([SEM] * (2 * N_LATE)) + (ANY,),
        out_specs=(HBM, HBM), input_output_aliases={0: 0, 1: 1},
        compiler_params=pltpu.CompilerParams(has_side_effects=pltpu.SideEffectType.DATAFLOW_SIDE_EFFECTING),
    )(w_thru, land_thru, *sems, after)[1]


N_PEERS = N_DEV - 1


def _peers(x, y, c):
    out = []
    for r in range(1, N_DEV):
        out.append((r - 1, 1 - x if r & 4 else x, 1 - y if r & 2 else y, 1 - c if r & 1 else c))
    return out


def _piece(g_ref, px, py, pc, whole):
    return g_ref if whole else g_ref.at[2 * px + py, _half(pc, g_ref.shape[1])]


def _reduce_start(g16, after, *, name, whole=False):
    def body(g_ref, land_ref, after_ref, *outs):
        sems = outs[:2 * N_PEERS]
        token = outs[2 * N_PEERS + 2]
        x, y, c = _coords()
        for k, px, py, pc in _peers(x, y, c):
            _remote(_piece(g_ref, px, py, pc, whole), land_ref.at[k], sems[k], sems[N_PEERS + k],
                    (px, py, pc)).start()
        token[...] = jnp.zeros_like(token)

    land_shape = (N_PEERS,) + (g16.shape if whole else (g16.shape[1] // 2, 1024))
    res = pl.pallas_call(
        body, name=name,
        out_shape=tuple([pltpu.SemaphoreType.DMA(())] * (2 * N_PEERS)) + (
            pltpu.HBM(g16.shape, g16.dtype), pltpu.HBM(land_shape, g16.dtype),
            jax.ShapeDtypeStruct((8, 128), F32)),
        in_specs=(HBM, HBM, ANY),
        out_specs=tuple([SEM] * (2 * N_PEERS)) + (HBM, HBM, pl.BlockSpec(memory_space=pltpu.VMEM)),
        input_output_aliases={0: 2 * N_PEERS, 1: 2 * N_PEERS + 1},
        compiler_params=pltpu.CompilerParams(has_side_effects=pltpu.SideEffectType.DATAFLOW_SIDE_EFFECTING),
    )(pltpu.with_memory_space_constraint(g16, pltpu.HBM),
      pltpu.with_memory_space_constraint(lax.empty(land_shape, g16.dtype), pltpu.HBM), after)
    return res[:2 * N_PEERS], res[2 * N_PEERS], res[2 * N_PEERS + 1], res[2 * N_PEERS + 2]


def _reduce_wait(sems, g_thru, land_thru, after, *, name, whole=False):
    def body(g_ref, land_ref, *rest):
        sems = rest[:2 * N_PEERS]
        x, y, c = _coords()
        for k, px, py, pc in _peers(x, y, c):
            cp = _remote(_piece(g_ref, px, py, pc, whole), land_ref.at[k], sems[k], sems[N_PEERS + k],
                         (px, py, pc))
            cp.wait_send()
            cp.wait_recv()

    return pl.pallas_call(
        body, name=name,
        out_shape=(pltpu.HBM(g_thru.shape, g_thru.dtype), pltpu.HBM(land_thru.shape, land_thru.dtype)),
        in_specs=(HBM, HBM) + tuple([SEM] * (2 * N_PEERS)) + tuple([ANY] * len(after)),
        out_specs=(HBM, HBM), input_output_aliases={0: 0, 1: 1},
        compiler_params=pltpu.CompilerParams(has_side_effects=pltpu.SideEffectType.DATAFLOW_SIDE_EFFECTING),
    )(g_thru, land_thru, *sems, *after)[1]


def _reduce_sum(place, g32, land, *, name):
    rows = g32.shape[1]
    nb = rows // 2 // SUM_TILE

    def body(place_ref, g_ref, l_ref, o_ref):
        acc = g_ref[...]
        for k in range(N_PEERS):
            acc = acc + l_ref[k].astype(F32)
        o_ref[...] = acc

    return pl.pallas_call(
        body, name=name,
        out_shape=jax.ShapeDtypeStruct((rows, 1024), F32),
        grid_spec=pltpu.PrefetchScalarGridSpec(
            num_scalar_prefetch=1, grid=(nb,),
            in_specs=[pl.BlockSpec((None, SUM_TILE, 1024), lambda i, s: (s[0], s[1] * nb + i, 0)),
                      pl.BlockSpec((N_PEERS, SUM_TILE, 1024), lambda i, s: (0, i, 0))],
            out_specs=pl.BlockSpec((SUM_TILE, 1024), lambda i, s: (s[1] * nb + i, 0))),
        compiler_params=_params("parallel"),
    )(place, g32, land)


def _join_halves(red, *, name):
    rows = red.shape[0]

    def body(r_ref, o_ref, ssem, rsem):
        x, y, c = _coords()
        cp = _remote(r_ref.at[_half(c, rows)], o_ref.at[_half(c, rows)], ssem, rsem, (x, y, 1 - c))
        cp.start()
        _remote(r_ref.at[_half(1 - c, rows)], o_ref.at[_half(1 - c, rows)], ssem, rsem,
                (x, y, 1 - c)).wait_recv()
        cp.wait_send()

    return pl.pallas_call(
        body, name=name,
        out_shape=jax.ShapeDtypeStruct(red.shape, F32),
        in_specs=[ANY], out_specs=ANY, input_output_aliases={0: 0},
        scratch_shapes=[pltpu.SemaphoreType.DMA, pltpu.SemaphoreType.DMA],
    )(red)


def _sum_small(me, svec, land, *, name):
    def body(me_ref, s_ref, l_ref, o_ref):
        mine = me_ref[0]
        acc = None
        for d in range(N_DEV):
            theirs = l_ref[jnp.maximum(jnp.bitwise_xor(mine, d) - 1, 0)]
            v = jnp.where(mine == d, s_ref[...], theirs)
            acc = v if acc is None else acc + v
        o_ref[...] = acc

    return pl.pallas_call(
        body, name=name,
        out_shape=jax.ShapeDtypeStruct(svec.shape, F32),
        grid_spec=pltpu.PrefetchScalarGridSpec(
            num_scalar_prefetch=1, grid=(1,),
            in_specs=[pl.BlockSpec(svec.shape, lambda i, s: (0, 0)),
                      pl.BlockSpec(land.shape, lambda i, s: (0, 0, 0))],
            out_specs=pl.BlockSpec(svec.shape, lambda i, s: (0, 0))),
        compiler_params=_params("arbitrary"),
    )(me, svec, land)


def _pack_early_shards(wts):
    parts = [wts["w_in"][0].T.astype(BF16)] + [wts[n].astype(BF16).reshape(-1, 1024) for n, _ in EARLY[1:]]
    used = sum(r for _, r in EARLY)
    return jnp.concatenate(parts + [jnp.zeros((EARLY_ROWS - used, 1024), BF16)], axis=0)


def _pack_late_shards(wts):
    parts, at = [], 0
    for n, off, rows in LATE:
        if off > at:
            parts.append(jnp.zeros((off - at, 1024), BF16))
        parts.append(wts[n].astype(BF16).reshape(rows, 1024))
        at = off + rows
    return jnp.concatenate(parts, axis=0)


def _unpack_early(full):
    w = {}
    w["w_in_t"] = jnp.concatenate(
        [full[0, N_SMALL:1200]] + [full[p, :1200] for p in range(1, N_CHIPS)]
        + [full[0, :N_SMALL], jnp.zeros((D_MODEL - N_SMALL, 1024), full.dtype)], axis=0)
    w["w_uq"] = full[:, 1200:1344].reshape(4, 384, 384).transpose(1, 0, 2).reshape(384, 1536)
    w["w_uk"] = full[:, 1344:1408].reshape(256, 1024)
    w["w_uv"] = full[:, 1408:1472].reshape(256, 1024)
    return w


def _pack_early_grads(g):
    gt = g["w_in_t"]
    rest = 1200 - N_SMALL
    w_in = jnp.stack([jnp.concatenate([gt[4 * D_MODEL:4 * D_MODEL + N_SMALL], gt[:rest]], axis=0)]
                     + [gt[rest + 1200 * (p - 1):rest + 1200 * p] for p in range(1, N_CHIPS)])
    parts = [
        w_in,
        g["w_uq"].reshape(384, 4, 384).transpose(1, 0, 2).reshape(4, 144, 1024),
        g["w_uk"].reshape(4, 64, 1024),
        g["w_uv"].reshape(4, 64, 1024),
        jnp.zeros((4, EARLY_ROWS - sum(r for _, r in EARLY), 1024), F32),
    ]
    return jnp.concatenate(parts, axis=1)


def _pad_w_uq(w):
    w = w.reshape(Q_RANK, N_HEADS, NOPE + ROPE)
    return jnp.pad(w, ((0, 0), (0, 0), (0, QK_PAD - NOPE - ROPE))).reshape(Q_RANK, N_HEADS * QK_PAD)


def _local_step(xs, pos, tgt, W, late_weights, late_bufs, send_late_grads, send_small_grads, send_early_grads,
                cw, vec):
    D = D_MODEL
    w_in_t = W["w_in_t"]
    w_uq = _pad_w_uq(W["w_uq"])
    w_ukv = jnp.concatenate([W["w_uk"], W["w_uv"]], axis=1)
    inv = ROPE_THETA ** (-jnp.arange(0, ROPE, 2, dtype=F32) / ROPE)
    inv_freq = jnp.concatenate([inv, inv, jnp.zeros((128 - ROPE,), F32)])[None, :]

    (h,) = _rowwise(lambda x, g: (_rms(x, g),), [(xs, D, 0)], [vec["norm_mix_pre"]], [(D, BF16)],
                    name="rms_pre")
    z = _mm(h, w_in_t, tb=True, out_dtype=BF16, name="mm_in")

    def qkv_prep(zs, p, qn, kvn, invf):
        ang = p.astype(F32) * invf
        cosv, sinv = jnp.cos(ang), jnp.sin(ang)
        ln = lax.broadcasted_iota(jnp.int32, ang.shape, 1)
        cs = jnp.where(ln < ROPE, cosv, 0.0)
        sa = jnp.where((ln >= ROPE // 2) & (ln < ROPE), sinv, 0.0)
        sb = jnp.where(ln < ROPE // 2, -sinv, 0.0)
        cqn = _rms(zs[:, :Q_RANK], qn)
        ckvn = _rms(zs[:, Q_RANK:Q_RANK + KV_RANK], kvn)
        kr = _rope(zs[:, 640:768], cs, sa, sb)
        return cqn, ckvn, kr, cs, sa, sb

    cqn, ckvn, krot, cs, sa, sb = _rowwise(
        qkv_prep, [(z, D, Z_S), (pos, 1, 0)], [vec["q_norm"], vec["kv_norm"], inv_freq],
        [(Q_RANK, BF16), (KV_RANK, BF16), (128, BF16), (128, F32), (128, F32), (128, F32)],
        name="qkv_prep")
    q_pre = _mm(cqn, w_uq, out_dtype=BF16, name="mm_uq")
    kv = _mm(ckvn, w_ukv, out_dtype=BF16, name="mm_ukv")

    def qk_rope(qp, kn, vv, kr, cs_, sa_, sb_):
        qs, ks, vs = [], [], []
        ones = jnp.ones((qp.shape[0], NOPE), F32)
        for hd in range(N_HEADS):
            qs.append(qp[:, hd * QK_PAD:hd * QK_PAD + NOPE])
            qs.append(_rope(qp[:, hd * QK_PAD + NOPE:(hd + 1) * QK_PAD], cs_, sa_, sb_))
            ks.append(kn[:, hd * NOPE:(hd + 1) * NOPE])
            ks.append(kr.astype(F32))
            vs.append(vv[:, hd * NOPE:(hd + 1) * NOPE])
            vs.append(ones)
        return (jnp.concatenate(qs, axis=1) * Q_SCALE, jnp.concatenate(ks, axis=1),
                jnp.concatenate(vs, axis=1))

    q, k, v = _rowwise(
        qk_rope, [(q_pre, 2048, 0), (kv, 1024, 0), (kv, 1024, 1), (krot, 128, 0), (cs, 128, 0),
                  (sa, 128, 0), (sb, 128, 0)], [],
        [(2048, BF16), (2048, BF16), (2048, BF16)], name="qk_rope")
    attn, lse = _flash_fwd(q, k, v)
    LW = late_weights(attn)
    y_attn = LW.fwd(attn, "w_o_attn", out_dtype=BF16, name="mm_o_attn")

    u1 = _conv_fwd(z, cw, vec["conv_b"])

    def ln_parts(u, g, b):
        mu = jnp.mean(u, axis=-1, keepdims=True)
        xc = u - mu
        rs = lax.rsqrt(jnp.mean(xc * xc, axis=-1, keepdims=True) + EPS)
        nh = xc * rs
        return nh, rs, nh * g + b

    def ln_silu(u, g, b):
        y = ln_parts(u, g, b)[2]
        return (y * _sigmoid(y),)

    (u3,) = _rowwise(ln_silu, [(u1, D, 0)], [vec["conv_ln_g"], vec["conv_ln_b"]], [(D, BF16)],
                     name="ln_silu")

    def merge(yc, ga, gc, ya, b2):
        return yc, _sigmoid(ga) * ya + _sigmoid(gc) * (yc + b2)

    y_conv, merged = LW.fwd(u3, "w_pw2", name="mm_pw2", tm=512, epilogue=merge, out_dtypes=(BF16, BF16),
                            extras=[(z, Z_GA), (z, Z_GC), y_attn, vec["b_pw2"]])

    def post1(m, x, g1, g2):
        x1 = x + _rms(m, g1)
        return m, x1, _rms(x1, g2)

    mo, x1, h2 = LW.fwd(merged, "w_out", name="mm_out", tm=512, epilogue=post1, out_dtypes=(F32, F32, BF16),
                        extras=[xs, vec["norm_mix_post"], vec["norm_mlp_pre"]])

    def sqrelu(acc):
        r = jnp.maximum(acc, 0.0)
        return r * r, r

    a2, r1 = LW.fwd(h2, "w_ff1", name="mm_ff1", tm=2048, epilogue=sqrelu, out_dtypes=(BF16, BF16))

    def post2(f_, x1_, t_, g):
        e = x1_ + _rms(f_, g) - t_
        dy = e * (1.0 / D)
        df, dg = _rms_bwd(f_, g, dy)
        loss = jnp.broadcast_to(jnp.sum(e * e, keepdims=True) * (0.5 / D), (1, 128))
        return dy, df, dg, loss

    dy, df, g_norm_mlp_post, loss = LW.fwd(a2, "w_ff2", name="mm_ff2", epilogue=post2, out_dtypes=(F32, BF16),
                                           extras=[x1, tgt, vec["norm_mlp_post"]], reds=(D, 128))
    late_bufs = _late_grad(a2, df, "w_ff2", late_bufs, name="mm_dw_ff2")
    df1 = LW.bwd(df, "w_ff2", name="mm_df1", tm=2048, extras=[r1], out_dtypes=(BF16,),
              epilogue=lambda acc, r: (acc * (2.0 * r.astype(F32)),))
    late_bufs = _late_grad(h2, df1, "w_ff1", late_bufs, name="mm_dw_ff1")

    def bwd_mid(dh2_, dy_, x1_, m, g2, g1):
        d1, dg2 = _rms_bwd(x1_, g2, dh2_)
        dx1_ = dy_ + d1
        dm, dg1 = _rms_bwd(m, g1, dx1_)
        return dx1_, dm, dg2, dg1

    dx1, dmo, g_norm_mlp_pre, g_norm_mix_post = LW.bwd(
        df1, "w_ff1", name="mm_dh2", tm=512, epilogue=bwd_mid, out_dtypes=(F32, BF16), reds=(D, D),
        extras=[dy, x1, mo, vec["norm_mlp_pre"], vec["norm_mix_post"]])
    late_bufs = _late_grad(merged, dmo, "w_out", late_bufs, name="mm_dw_out")
    def dmerge(dm, ga, gc, ya, yc, b2):
        sga, sgc = _sigmoid(ga), _sigmoid(gc)
        dya = dm * sga
        dyc = dm * sgc
        dga = dm * ya * sga * (1.0 - sga)
        dgc = dm * (yc + b2) * sgc * (1.0 - sgc)
        return dya, dyc, jnp.concatenate([dga, dgc], axis=1), jnp.sum(dyc, axis=0, keepdims=True)

    dz = lax.empty((xs.shape[0], Z_COLS), BF16)
    tm_dm = _tile(xs.shape[0], 512)
    gates = pl.BlockSpec((tm_dm, 2 * D), lambda i, j, k: (i, Z_GA // 2))
    dya, dyc, dz, g_b_pw2 = LW.bwd(
        dmo, "w_out", name="mm_dmerged", tm=tm_dm, epilogue=dmerge, out_dtypes=(BF16, BF16, (dz, gates)),
        reds=(D,), extras=[(z, Z_GA), (z, Z_GC), y_attn, y_conv, vec["b_pw2"]])

    late_bufs = _late_grad(attn, dya, "w_o_attn", late_bufs, name="mm_dw_o_attn")
    late_bufs = _late_grad(u3, dyc, "w_pw2", late_bufs, name="mm_dw_pw2")
    late_token = send_late_grads(late_bufs)

    def delta_fn(acc, o_):
        do32 = acc.astype(BF16).astype(F32)
        pr = do32 * o_
        ln = lax.broadcasted_iota(jnp.int32, (pr.shape[0], NOPE), 1)
        cols = []
        for hd in range(N_HEADS):
            dl = jnp.sum(pr[:, hd * NOPE:(hd + 1) * NOPE], axis=-1, keepdims=True)
            hi = dl.astype(BF16).astype(F32)
            cols.append(do32[:, hd * NOPE:(hd + 1) * NOPE])
            cols.append(jnp.where(ln == 0, -hi, jnp.where(ln == 1, hi - dl, 0.0)))
        return (jnp.concatenate(cols, axis=1),)

    tm_do = _tile(xs.shape[0], 512)
    do_ext = LW.bwd(dya, "w_o_attn", name="mm_dattn", tm=tm_do, epilogue=delta_fn, extras=[attn],
                    after=[late_token], out_dtypes=(
                        (lax.empty((xs.shape[0], N_HEADS * QK_PAD), BF16),
                         pl.BlockSpec((tm_do, N_HEADS * QK_PAD), lambda i, j, k: (i, 0))),))
    dq, dk, dv = _flash_bwd(q, k, v, do_ext, lse)

    def unrope(dq_, dk_, dv_, cs_, sa_, sb_):
        dq_, dk_ = dq_.astype(F32), dk_.astype(F32)
        qs, kn = [], []
        dkr = jnp.zeros_like(cs_)
        for hd in range(N_HEADS):
            qs.append(dq_[:, hd * QK_PAD:hd * QK_PAD + NOPE] * SCALE)
            qs.append(_rope_t(dq_[:, hd * QK_PAD + NOPE:(hd + 1) * QK_PAD] * SCALE, cs_, sa_, sb_))
            kn.append(dk_[:, hd * QK_PAD:hd * QK_PAD + NOPE] * K_UNSCALE)
            dkr = dkr + dk_[:, hd * QK_PAD + NOPE:(hd + 1) * QK_PAD]
        return (jnp.concatenate(qs, axis=1), jnp.concatenate(kn + [dv_], axis=1),
                _rope_t(dkr * K_UNSCALE, cs_, sa_, sb_))

    dq_pre, dkv, dkr = _rowwise(
        unrope, [(dq, 2048, 0), (dk, 2048, 0), (dv, 1024, 0), (cs, 128, 0), (sa, 128, 0), (sb, 128, 0)], [],
        [(2048, BF16), (2048, BF16), (128, F32)], name="unrope")
    g_w_uq = _mm(cqn, dq_pre, ta=True, name="mm_dw_uq")
    dcqn = _mm(dq_pre, w_uq, tb=True, out_dtype=BF16, name="mm_dcqn")
    g_w_ukv = _mm(ckvn, dkv, ta=True, name="mm_dw_ukv")
    dckvn = _mm(dkv, w_ukv, tb=True, out_dtype=BF16, name="mm_dckvn")

    def small_bwd(zs, dcq_, dckv_, dkr_, qn, kvn):
        dcq, dqn = _rms_bwd(zs[:, :Q_RANK], qn, dcq_)
        dckv, dkvn = _rms_bwd(zs[:, Q_RANK:Q_RANK + KV_RANK], kvn, dckv_)
        pad = jnp.zeros((zs.shape[0], D - 768), F32)
        return jnp.concatenate([dcq, dckv, dkr_, pad], axis=1), dqn, dkvn

    dz, g_q_norm, g_kv_norm = _rowwise(
        small_bwd, [(z, D, Z_S), (dcqn, Q_RANK, 0), (dckvn, KV_RANK, 0), (dkr, 128, 0)],
        [vec["q_norm"], vec["kv_norm"]], [(D, dz, Z_S)], [(1, Q_RANK), (1, KV_RANK)], name="small_bwd")

    def ln_silu_bwd(d3, u, g, b):
        nh, rs, y = ln_parts(u, g, b)
        sg = _sigmoid(y)
        dyv = d3 * (sg * (1.0 + y * (1.0 - sg)))
        dnh = dyv * g
        du = rs * (dnh - jnp.mean(dnh, axis=-1, keepdims=True)
                   - nh * jnp.mean(dnh * nh, axis=-1, keepdims=True))
        return (du, jnp.sum(dyv * nh, axis=0, keepdims=True), jnp.sum(dyv, axis=0, keepdims=True),
                jnp.sum(du, axis=0, keepdims=True))

    du1, g_ln_g, g_ln_b, g_conv_b = LW.bwd(
        dyc, "w_pw2", name="mm_du3", tm=512, epilogue=ln_silu_bwd, out_dtypes=(F32,), reds=(D, D, D),
        extras=[u1, vec["conv_ln_g"], vec["conv_ln_b"]])
    dz, g_conv_w = _conv_bwd(z, du1, cw, dz)
    small_token = send_small_grads({
        "conv_b": g_conv_b, "conv_ln_g": g_ln_g, "conv_ln_b": g_ln_b, "b_pw2": g_b_pw2,
        "norm_mix_post": g_norm_mix_post, "norm_mlp_pre": g_norm_mlp_pre, "norm_mlp_post": g_norm_mlp_post,
        "q_norm": g_q_norm, "kv_norm": g_kv_norm}, g_conv_w, loss)
    g_w_in_t = _mm(dz, h, ta=True, name="mm_dw_in", after=[small_token])
    early_token = send_early_grads({
        "w_in_t": g_w_in_t,
        "w_uq": g_w_uq.reshape(Q_RANK, N_HEADS, QK_PAD)[:, :, :NOPE + ROPE].reshape(Q_RANK, 1536),
        "w_uk": g_w_ukv[:, :1024], "w_uv": g_w_ukv[:, 1024:]})

    def final(dh_, dx1_, x, g):
        d, dg = _rms_bwd(x, g, dh_)
        return dx1_ + d, dg

    grad_x, g_norm_mix_pre = _mm(
        dz, w_in_t, name="mm_dh", after=[early_token], tiles=(_tile(xs.shape[0], 512), D, Z_COLS),
        b_spec=pl.BlockSpec((Z_COLS, D), lambda i, j, k: (0, 0), pipeline_mode=pl.Buffered(1)),
        epilogue=final, out_dtypes=(F32,), reds=(D,), extras=[dx1, xs, vec["norm_mix_pre"]])

    return grad_x, g_norm_mix_pre


def _row1024(a):
    return jnp.pad(a, ((0, 0), (0, 1024 - a.shape[1])))


ORDER = ("norm_mix_pre", "w_in", "q_norm", "w_uq", "kv_norm", "w_uk", "w_uv", "w_o_attn", "conv_w",
         "conv_b", "conv_ln_g", "conv_ln_b", "w_pw2", "b_pw2", "w_out", "norm_mix_post", "norm_mlp_pre",
         "w_ff1", "w_ff2", "norm_mlp_post")


def kernel(x, positions, norm_mix_pre, w_in, q_norm, w_uq, kv_norm, w_uk, w_uv, w_o_attn, conv_w, conv_b, conv_ln_g, conv_ln_b, w_pw2, b_pw2, w_out, norm_mix_post, norm_mlp_pre, w_ff1, w_ff2, norm_mlp_post, loss_target, m_norm_mix_pre, m_w_in, m_q_norm, m_w_uq, m_kv_norm, m_w_uk, m_w_uv, m_w_o_attn, m_conv_w, m_conv_b, m_conv_ln_g, m_conv_ln_b, m_w_pw2, m_b_pw2, m_w_out, m_norm_mix_post, m_norm_mlp_pre, m_w_ff1, m_w_ff2, m_norm_mlp_post, v_norm_mix_pre, v_w_in, v_q_norm, v_w_uq, v_kv_norm, v_w_uk, v_w_uv, v_w_o_attn, v_conv_w, v_conv_b, v_conv_ln_g, v_conv_ln_b, v_w_pw2, v_b_pw2, v_w_out, v_norm_mix_post, v_norm_mlp_pre, v_w_ff1, v_w_ff2, v_norm_mlp_post):
    args = dict(locals())
    wts = {n: args[n] for n in ORDER}
    mom = {n: args["m_" + n] for n in ORDER}
    var = {n: args["v_" + n] for n in ORDER}
    T = x.shape[1]
    place = jnp.stack([2 * lax.axis_index("x") + lax.axis_index("y"), lax.axis_index("c")]).astype(jnp.int32)

    w_early = _pack_early_shards(wts)
    w_late = _pack_late_shards(wts)
    cw_own = jnp.pad(conv_w.reshape(CONV_W, -1), ((0, CONV_TAPS_PAD - CONV_W), (0, 0)))
    full, cw_all = _gather_weights(w_early, cw_own)
    full = lax.dynamic_update_slice(full, w_early[None], (place[0], 0, 0))
    cw_all = lax.dynamic_update_slice(cw_all, cw_own[None], (place[0], 0, 0))
    W = _unpack_early(full)
    cw = cw_all.transpose(1, 0, 2).reshape(CONV_TAPS_PAD, D_MODEL)
    sems, w_thru, land, token = _late_gather_start(w_late, full)
    vec = {n: wts[n] for n in SMALL_VECS}
    vec["norm_mix_pre"] = vec["norm_mix_pre"] + token[:1, :1]

    def late_weights(after):
        landed = _late_gather_wait(sems, w_thru, land, after)
        return _LateWeights(lax.dynamic_update_slice(landed, w_late[None], (place[0], 0, 0)))

    sent = {}

    def send_late_grads(bufs):
        sent["late"] = (bufs[0],) + _reduce_start(bufs[1], place, name="late_grads_start")
        return sent["late"][4]

    def send_small_grads(gs, g_conv_w, loss):
        rows = [_row1024(gs[n]) if n in gs else jnp.zeros((1, 1024), F32) for n in SMALL_VECS]
        svec = jnp.concatenate(rows + [_row1024(loss), jnp.zeros((SMALL_CONVW_ROW - SMALL_LOSS_ROW - 1, 1024), F32),
                                       g_conv_w], axis=0)
        sent["small"] = (svec,) + _reduce_start(svec, place, name="small_grads_start", whole=True)
        return sent["small"][4]

    def send_early_grads(g):
        g32 = _pack_early_grads(g)
        sent["early"] = (g32,) + _reduce_start(g32.astype(BF16), place, name="early_grads_start")
        return sent["early"][4]

    late_bufs = (lax.empty((N_CHIPS, LATE_ROWS, 1024), F32), lax.empty((N_CHIPS, LATE_ROWS, 1024), BF16))
    grad_x, g_norm_mix_pre = _local_step(x[0], positions.reshape(T, 1), loss_target[0], W, late_weights,
                                         late_bufs, send_late_grads, send_small_grads, send_early_grads, cw, vec)
    last = jnp.pad(g_norm_mix_pre, ((0, 7), (0, 0)))
    sent["last"] = (last,) + _reduce_start(last, place, name="last_grads_start", whole=True)

    def finish(group, after):
        g32, sems_, g_thru, land_, _ = sent[group]
        landed = _reduce_wait(sems_, g_thru, land_, after, name=group + "_grads_wait")
        red = _reduce_sum(place, g32, landed, name=group + "_grads_sum")
        return _join_halves(red, name=group + "_grads_join")

    def as2d(a):
        return a.reshape(-1, a.shape[-1]) if a.ndim != 4 or a.shape[2] == 1 else a.reshape(a.shape[1], -1)

    grads, delta, new_m, new_v = {}, {}, {}, {}

    def update(n):
        shp = wts[n].shape
        outs = _adamw(as2d(wts[n]), as2d(grads[n]), as2d(mom[n]), as2d(var[n]), name="adamw_" + n)
        delta[n], new_m[n], new_v[n] = (o.reshape(shp) for o in outs)

    g_late = finish("late", [sent["early"][4]])
    for n, off, rows in LATE:
        shp = wts[n].shape
        outs = _adamw(as2d(wts[n]), g_late, as2d(mom[n]), as2d(var[n]), g_row=off, name="adamw_" + n)
        delta[n], new_m[n], new_v[n], grads[n] = (o.reshape(shp) for o in outs)

    g_early = finish("early", [grad_x, delta["w_ff2"]])
    outs = _adamw(wts["w_in"][0].T, g_early[:1200], mom["w_in"][0].T, var["w_in"][0].T, name="adamw_w_in")
    delta["w_in"], new_m["w_in"], new_v["w_in"] = (o.T[None] for o in outs)
    grads["w_in"] = g_early[:1200].T[None]
    off = 1200
    for n, rows in EARLY[1:]:
        grads[n] = g_early[off:off + rows].reshape(wts[n].shape)
        off += rows
        update(n)

    def finish_small(group, after):
        svec, sems_, s_thru, land_, _ = sent[group]
        landed = _reduce_wait(sems_, s_thru, land_, after, name=group + "_grads_wait", whole=True)
        return _sum_small((2 * place[0] + place[1])[None], svec, landed, name=group + "_grads_sum")

    ssum = finish_small("small", [delta["w_in"]])
    for r, n in enumerate(SMALL_VECS):
        grads[n] = ssum[r:r + 1, :wts[n].shape[1]]
    grads["norm_mix_pre"] = finish_small("last", [ssum])[:1]
    cw_sum = ssum[SMALL_CONVW_ROW:SMALL_CONVW_ROW + CONV_W]
    grads["conv_w"] = lax.dynamic_slice_in_dim(cw_sum, place[0] * 256, 256, axis=1).reshape(conv_w.shape)
    loss_out = ssum[SMALL_LOSS_ROW, 0]
    update("conv_w")
    small = [jnp.concatenate([_row1024(t[n]) for n in SMALL_VECS], axis=0) for t in (wts, grads, mom, var)]
    outs = _adamw(*small, name="adamw_small")
    for r, n in enumerate(SMALL_VECS):
        delta[n], new_m[n], new_v[n] = (o[r:r + 1, :wts[n].shape[1]] for o in outs)

    return (loss_out, grad_x[None], *[grads[n] for n in ORDER], *[delta[n] for n in ORDER],
            *[new_m[n] for n in ORDER], *[new_v[n] for n in ORDER])
```

```python
import functools

import numpy as np
import jax
import jax.numpy as jnp
from jax import lax
from jax.experimental import pallas as pl
from jax.experimental.pallas import tpu as pltpu

F32 = jnp.float32
BF16 = jnp.bfloat16
MESH = pl.DeviceIdType.MESH

D_MODEL = 1024
N_HEADS = 8
NOPE = 128
ROPE = 64
QK_PAD = 256
Q_RANK = 384
KV_RANK = 256
CONV_W = 31
CONV_TAPS_PAD = 32
HALO = 32
EPS = 1e-6
ROPE_THETA = 10000.0
SCALE = float((NOPE + ROPE) ** -0.5)
Q_SCALE = SCALE * float(np.log2(np.e))
K_UNSCALE = float(np.log(2.0))
Z_A, Z_B, Z_GA, Z_GC, Z_S = range(5)
Z_COLS = 5 * D_MODEL
N_SMALL = Q_RANK + KV_RANK + ROPE
N_CHIPS = 4
N_DEV = 8

ADAM_LR = 0.001
ADAM_B1 = 0.9
ADAM_B2 = 0.999
ADAM_EPS = 1e-08
ADAM_WD = 0.01
ADAM_STEP = 10

VMEM_LIMIT = 48 * 1024 * 1024
NEG = -1e30

EARLY = (("w_in", 1200), ("w_uq", 144), ("w_uk", 64), ("w_uv", 64))
EARLY_ROWS = 1536
LATE = (("w_ff1", 0, 1024), ("w_o_attn", 1024, 256), ("w_pw2", 1536, 256), ("w_out", 1792, 256),
        ("w_ff2", 2048, 1024))
LATE_OFF = {n: off for n, off, _ in LATE}
LATE_ROWS = 3072
SUM_TILE = 256

SMALL_VECS = ("norm_mix_pre", "conv_b", "conv_ln_g", "conv_ln_b", "b_pw2", "norm_mix_post",
              "norm_mlp_pre", "norm_mlp_post", "q_norm", "kv_norm")
SMALL_LOSS_ROW = 10
SMALL_CONVW_ROW = 16


ANY = pl.BlockSpec(memory_space=pl.ANY)


def _params(*sem):
    return pltpu.CompilerParams(dimension_semantics=sem, vmem_limit_bytes=VMEM_LIMIT)


def _tile(n, cap):
    if n <= cap:
        return n
    t = (cap // 128) * 128
    while n % t:
        t -= 128
    return t


def _row_tile(n, cap):
    if n <= cap:
        return n
    t = (cap // 8) * 8
    while n % t:
        t -= 8
    return t


MM_VMEM_BUDGET = 40 * 1024 * 1024


def _mm_tiles(M, N, K, in_bytes, out_bytes):
    tk = K
    while True:
        best = None
        for tn in sorted({_tile(N, cap) for cap in (2560, 2048, 1024)}, reverse=True):
            for tm in sorted({_tile(M, cap) for cap in (1024, 512, 256)}, reverse=True):
                need = 2 * tk * (tm * in_bytes[0] + tn * in_bytes[1]) + 2 * tm * tn * out_bytes
                need += tm * tn * 4 if tk < K else 0
                steps = (M // tm) * (N // tn)
                if need <= MM_VMEM_BUDGET and (best is None or steps < best[0]):
                    best = (steps, tm, tn)
        if best is not None:
            return best[1], best[2], tk
        tk = _tile(K, tk - 128)


def _mm(a, b, *, name, ta=False, tb=False, out_dtype=F32, extras=(), epilogue=None, out_dtypes=None,
        after=(), shape=None, tiles=None, b_spec=None, reds=(), k_slabs=False):
    if shape is None:
        M, K = (a.shape[1], a.shape[0]) if ta else a.shape
        N = b.shape[0] if tb else b.shape[1]
        assert K == (b.shape[1] if tb else b.shape[0]), (a.shape, b.shape, ta, tb)
    else:
        M, N, K = shape
    outs = tuple(out_dtypes or (out_dtype,))
    placed = [o for o in outs if isinstance(o, tuple)]
    ex_arrays = [e[0] if isinstance(e, tuple) else e for e in extras]
    if tiles is None:
        out_bytes = sum(jnp.dtype(d).itemsize for d in outs) + sum(e.dtype.itemsize for e in ex_arrays)
        tiles = _mm_tiles(M, N, K, (a.dtype.itemsize, b.dtype.itemsize), out_bytes)
    tm, tn, tk = tiles
    nk = K // tk
    dims = (((0 if ta else 1,), (1 if tb else 0,)), ((), ()))
    n_ex, n_out = len(extras), len(outs)
    n_pass = len(after) + len(placed)
    assert not reds or tn == N

    def body(*refs):
        a_ref, b_ref = refs[:2]
        ex_refs = refs[2:2 + n_ex]
        o_refs = refs[2 + n_ex + n_pass:2 + n_ex + n_pass + n_out]
        red_refs = refs[2 + n_ex + n_pass + n_out:2 + n_ex + n_pass + n_out + len(reds)]
        first_rows, k = pl.program_id(0) == 0, pl.program_id(2)
        bv = b_ref[...].astype(BF16)
        if k_slabs:
            kc = bv.shape[-1]
            part = sum(lax.dot_general(a_ref[:, p * kc:(p + 1) * kc].astype(BF16), bv[p], dims,
                                       preferred_element_type=F32) for p in range(bv.shape[0]))
        else:
            if bv.ndim == 3:
                bv = bv.reshape(-1, bv.shape[-1])
            part = lax.dot_general(a_ref[...].astype(BF16), bv, dims, preferred_element_type=F32)

        def finish(acc):
            ex = [r[...] for r in ex_refs]
            ex = [e.astype(F32) if e.dtype == BF16 else e for e in ex]
            res = epilogue(acc, *ex) if epilogue else (acc,) * n_out
            for o_ref, r in zip(o_refs, res):
                o_ref[...] = r.astype(o_ref.dtype)
            for red_ref, r in zip(red_refs, res[n_out:]):
                red_ref[...] += r

        if reds:
            @pl.when(first_rows & (k == 0))
            def _():
                for red_ref in red_refs:
                    red_ref[...] = jnp.zeros_like(red_ref)

        if nk == 1:
            finish(part)
        else:
            acc_ref = refs[-1]

            @pl.when(k == 0)
            def _():
                acc_ref[...] = part

            @pl.when((k > 0) & (k < nk - 1))
            def _():
                acc_ref[...] += part

            @pl.when(k == nk - 1)
            def _():
                finish(acc_ref[...] + part)

    a_spec = (pl.BlockSpec((tk, tm), lambda i, j, k: (k, i)) if ta
              else pl.BlockSpec((tm, tk), lambda i, j, k: (i, k)))
    if b_spec is None:
        b_spec = (pl.BlockSpec((tn, tk), lambda i, j, k: (j, k)) if tb
                  else pl.BlockSpec((tk, tn), lambda i, j, k: (k, j)))
    o_spec = pl.BlockSpec((tm, tn), lambda i, j, k: (i, j))
    row_spec = pl.BlockSpec((1, tn), lambda i, j, k: (0, j))
    ex_specs = []
    for e in extras:
        if isinstance(e, tuple):
            ex_specs.append(pl.BlockSpec((tm, tn), functools.partial(lambda i, j, k, cb: (i, cb), cb=e[1])))
        else:
            ex_specs.append(row_spec if e.shape[0] == 1 else o_spec)
    out_shape, out_specs, aliases = [], [], {}
    for o, out in enumerate(outs):
        if isinstance(out, tuple):
            out_shape.append(jax.ShapeDtypeStruct(out[0].shape, out[0].dtype))
            out_specs.append(out[1])
            aliases[2 + n_ex + len(after) + len(aliases)] = o
        else:
            out_shape.append(jax.ShapeDtypeStruct((M, N), out))
            out_specs.append(o_spec)
    passed = list(after) + [buf for buf, _ in placed]
    out_shape = out_shape + [jax.ShapeDtypeStruct((1, w), F32) for w in reds]
    out_specs = out_specs + [pl.BlockSpec((1, w), lambda i, j, k: (0, 0)) for w in reds]
    res = pl.pallas_call(
        body, name=name, out_shape=out_shape,
        grid=(M // tm, N // tn, nk),
        in_specs=[a_spec, b_spec] + ex_specs + [ANY] * len(passed),
        out_specs=out_specs, input_output_aliases=aliases,
        scratch_shapes=[pltpu.VMEM((tm, tn), F32)] if nk > 1 else [],
        compiler_params=_params(*(("arbitrary",) * 3 if reds else ("parallel", "parallel", "arbitrary"))),
    )(a, b, *ex_arrays, *passed)
    return res[0] if len(res) == 1 else res


class _LateWeights:
    def __init__(self, pack):
        self.pack = pack

    def fwd(self, x, n, *, name, tm=1024, **kw):
        T, off = x.shape[0], LATE_OFF[n]
        if n == "w_ff1":
            return _mm(x, self.pack, name=name, shape=(T, 4096, 1024), tiles=(_tile(T, tm), 1024, 1024),
                       b_spec=pl.BlockSpec((None, 1024, 1024), lambda i, j, k: (j, 0, 0)), **kw)
        if n == "w_ff2":
            view = self.pack.reshape(N_CHIPS, LATE_ROWS // 1024, 1024, 1024)
            return _mm(x, view, name=name, shape=(T, 1024, 4096), tiles=(_tile(T, min(tm, 512)), 1024, 4096),
                       b_spec=pl.BlockSpec((N_CHIPS, None, 1024, 1024), lambda i, j, k: (0, off // 1024, 0, 0)),
                       **kw)
        return _mm(x, self.pack.reshape(N_CHIPS, LATE_ROWS // 256, 256, 1024), name=name,
                   shape=(T, 1024, 1024), tiles=(_tile(T, tm), 1024, 1024), b_spec=self._whole(off), **kw)

    @staticmethod
    def _whole(off):
        return pl.BlockSpec((N_CHIPS, None, 256, 1024), lambda i, j, k: (0, off // 256, 0, 0))

    def bwd(self, dy, n, *, name, tm=1024, **kw):
        T, off = dy.shape[0], LATE_OFF[n]
        tm = _tile(T, tm)
        if n == "w_ff1":
            view = self.pack.reshape(N_CHIPS, LATE_ROWS // 1024, 1024, 1024)
            return _mm(dy, view, tb=True, name=name, shape=(T, 1024, 4096), tiles=(tm, 1024, 4096),
                       b_spec=pl.BlockSpec((N_CHIPS, None, 1024, 1024), lambda i, j, k: (0, off // 1024, 0, 0),
                                           pipeline_mode=pl.Buffered(1)), k_slabs=True, **kw)
        if n == "w_ff2":
            return _mm(dy, self.pack, tb=True, name=name, shape=(T, 4096, 1024), tiles=(tm, 1024, 1024),
                       b_spec=pl.BlockSpec((None, 1024, 1024), lambda i, j, k: (j, off // 1024, 0)), **kw)
        return _mm(dy, self.pack.reshape(N_CHIPS, LATE_ROWS // 256, 256, 1024), tb=True, name=name,
                   shape=(T, 1024, 1024), tiles=(tm, 1024, 1024), b_spec=self._whole(off), **kw)


def _late_grad(x, dy, n, bufs, *, name):
    T, off = x.shape[0], LATE_OFF[n]
    if n == "w_ff1":
        spec = pl.BlockSpec((None, 512, 1024), lambda i, j, k: (j, i, 0))
        shape, tiles = (1024, 4096, T), (512, 1024, T)
    elif n == "w_ff2":
        spec = pl.BlockSpec((None, 512, 1024), lambda i, j, k: (i // 2, off // 512 + i % 2, 0))
        shape, tiles = (4096, 1024, T), (512, 1024, T)
    else:
        spec = pl.BlockSpec((None, 256, 1024), lambda i, j, k: (i, off // 256, 0))
        shape, tiles = (1024, 1024, T), (256, 1024, T)
    return _mm(x, dy, ta=True, name=name, shape=shape, tiles=tiles, out_dtypes=[(b, spec) for b in bufs])


def _rowwise(fn, rows, consts, outs, reds=(), *, name, tm=256, after=()):
    T = rows[0][0].shape[0]
    tm = min(tm, T)
    n_row, n_const, n_out = len(rows), len(consts), len(outs)
    placed = [o for o in outs if len(o) == 3]
    n_pass = len(after) + len(placed)

    def body(*refs):
        i = pl.program_id(0)
        vals = [r[...] for r in refs[:n_row + n_const]]
        vals = [v.astype(F32) if v.dtype == BF16 else v for v in vals]
        res = fn(*vals)
        out_refs = refs[n_row + n_const + n_pass:]
        for k in range(n_out):
            out_refs[k][...] = res[k].astype(out_refs[k].dtype)
        for k in range(len(reds)):
            ref = out_refs[n_out + k]

            @pl.when(i == 0)
            def _(ref=ref):
                ref[...] = jnp.zeros_like(ref)

            ref[...] += res[n_out + k]

    in_specs = [pl.BlockSpec((tm, w), functools.partial(lambda i, cb: (i, cb), cb=cb))
                for (_, w, cb) in rows]
    in_specs += [pl.BlockSpec(c.shape, lambda i: (0, 0)) for c in consts]
    in_specs += [ANY] * n_pass
    out_specs, out_shape, aliases = [], [], {}
    for o, out in enumerate(outs):
        if len(out) == 3:
            w, buf, cb = out
            out_specs.append(pl.BlockSpec((tm, w), functools.partial(lambda i, cb: (i, cb), cb=cb)))
            out_shape.append(jax.ShapeDtypeStruct(buf.shape, buf.dtype))
            aliases[n_row + n_const + len(after) + len(aliases)] = o
        else:
            out_specs.append(pl.BlockSpec((tm, out[0]), lambda i: (i, 0)))
            out_shape.append(jax.ShapeDtypeStruct((T, out[0]), out[1]))
    out_specs += [pl.BlockSpec(s, lambda i: (0, 0)) for s in reds]
    out_shape += [jax.ShapeDtypeStruct(s, F32) for s in reds]
    return pl.pallas_call(
        body, name=name, out_shape=out_shape, grid=(T // tm,),
        in_specs=in_specs, out_specs=out_specs, input_output_aliases=aliases,
        compiler_params=_params("arbitrary"),
    )(*[r[0] for r in rows], *consts, *after, *[o[1] for o in placed])


def _rms(x, g):
    r = lax.rsqrt(jnp.mean(x * x, axis=-1, keepdims=True) + EPS)
    return x * r * g


def _rms_bwd(x, g, dy):
    r = lax.rsqrt(jnp.mean(x * x, axis=-1, keepdims=True) + EPS)
    n = x * r
    dyg = dy * g
    dx = r * (dyg - n * jnp.mean(dyg * n, axis=-1, keepdims=True))
    return dx, jnp.sum(dy * n, axis=0, keepdims=True)


def _sigmoid(x):
    return 1.0 / (1.0 + jnp.exp(-x))


def _rope(x, cs, sa, sb):
    return x * cs + pltpu.roll(x, 32, 1) * sa + pltpu.roll(x, 96, 1) * sb


def _rope_t(dy, cs, sa, sb):
    return dy * cs + pltpu.roll(dy * sa, 96, 1) + pltpu.roll(dy * sb, 32, 1)


FLASH_FWD_TQ = 4096
FLASH_FWD_TK = 1024
FLASH_BWD_TQ = 1024
FLASH_BWD_TK = 512
NT_DIMS = (((1,), (1,)), ((), ()))
TN_DIMS = (((0,), (0,)), ((), ()))


def _flash_fwd(q, k, v, *, tq=FLASH_FWD_TQ, tk=FLASH_FWD_TK):
    T = q.shape[0]
    tq, tk = min(tq, T), min(tk, T)
    n_diag = tq // tk
    assert n_diag >= 1

    def body(q_ref, k_ref, v_ref, o_ref, lse_ref):
        i = pl.program_id(1)

        def logits(j, r0=0):
            sl = pl.ds(pl.multiple_of(j * tk, tk), tk)
            return lax.dot_general(q_ref[r0:tq, :], k_ref[sl, :], NT_DIMS, preferred_element_type=F32)

        def update(j, s, m, acc, masked, r0=0):
            sl = pl.ds(pl.multiple_of(j * tk, tk), tk)
            if masked:
                row = i * tq + r0 + lax.broadcasted_iota(jnp.int32, (tq - r0, tk), 0)
                col = j * tk + lax.broadcasted_iota(jnp.int32, (tq - r0, tk), 1)
                s = jnp.where(col <= row, s, NEG)
            m_old, acc_old = m[r0:], acc[r0:]
            m_new = jnp.maximum(m_old, jnp.max(s, axis=-1, keepdims=True))
            p = jnp.exp2(s - m_new)
            acc_new = (jnp.exp2(m_old - m_new) * acc_old
                       + jnp.dot(p.astype(BF16), v_ref[sl, :], preferred_element_type=F32))
            if r0:
                m_new = jnp.concatenate([m[:r0], m_new], axis=0)
                acc_new = jnp.concatenate([acc[:r0], acc_new], axis=0)
            return m_new, acc_new

        def step(j, carry):
            s, m, acc = carry
            s_next = logits(j + 1)
            m, acc = update(j, s, m, acc, False)
            return s_next, m, acc

        first = i * n_diag
        init = (logits(0), jnp.full((tq, 1), NEG, F32), jnp.zeros((tq, QK_PAD), F32))
        s, m, acc = lax.fori_loop(0, first, step, init)
        for d in range(n_diag):
            m, acc = update(first + d, s, m, acc, True, r0=d * tk)
            if d + 1 < n_diag:
                s = logits(first + d + 1, r0=(d + 1) * tk)
        l = acc[:, NOPE:]
        o_ref[...] = (acc[:, :NOPE] / l).astype(BF16)
        lse_ref[...] = m + jnp.log2(l)

    return pl.pallas_call(
        body, name="flash_fwd",
        out_shape=[jax.ShapeDtypeStruct((T, N_HEADS * NOPE), BF16),
                   jax.ShapeDtypeStruct((T, N_HEADS * NOPE), F32)],
        grid=(N_HEADS, T // tq),
        in_specs=[pl.BlockSpec((tq, QK_PAD), lambda h, i: (i, h)),
                  pl.BlockSpec((T, QK_PAD), lambda h, i: (0, h)),
                  pl.BlockSpec((T, QK_PAD), lambda h, i: (0, h))],
        out_specs=[pl.BlockSpec((tq, NOPE), lambda h, i: (i, h)),
                   pl.BlockSpec((tq, NOPE), lambda h, i: (i, h))],
        compiler_params=_params("parallel", "arbitrary"),
    )(q, k, v)


def _flash_bwd(q, k, v, do, lse, *, tq=FLASH_BWD_TQ, tk=FLASH_BWD_TK):
    T = q.shape[0]
    tq, tk = min(tq, T), min(tk, T)

    def body(q_ref, k_ref, v_ref, do_ref, lse_ref, dq_ref, dk_ref, dv_ref, dq_acc):
        dq_acc[...] = jnp.zeros_like(dq_acc)
        for j in range(T // tk):
            k0 = j * tk
            kj = k_ref[k0:k0 + tk, :]
            vj = v_ref[k0:k0 + tk, :]
            dk = jnp.zeros((tk, QK_PAD), F32)
            dv = jnp.zeros((tk, NOPE), F32)
            for i in range(k0 // tq, T // tq):
                r0 = max(i * tq, k0)
                r1 = (i + 1) * tq
                qi = q_ref[r0:r1, :]
                doi = do_ref[r0:r1, :]
                s = lax.dot_general(qi, kj, NT_DIMS, preferred_element_type=F32)
                p = jnp.exp2(s - lse_ref[r0:r1, :][:, :1])
                if r0 < k0 + tk:
                    row = r0 + lax.broadcasted_iota(jnp.int32, (r1 - r0, tk), 0)
                    col = k0 + lax.broadcasted_iota(jnp.int32, (r1 - r0, tk), 1)
                    p = jnp.where(col <= row, p, 0.0)
                dv = dv + lax.dot_general(p.astype(BF16), doi[:, :NOPE], TN_DIMS, preferred_element_type=F32)
                ds = (p * lax.dot_general(doi, vj, NT_DIMS, preferred_element_type=F32)).astype(BF16)
                dk = dk + lax.dot_general(ds, qi, TN_DIMS, preferred_element_type=F32)
                dq_acc[r0:r1, :] += jnp.dot(ds, kj, preferred_element_type=F32)
            dk_ref[k0:k0 + tk, :] = dk.astype(BF16)
            dv_ref[k0:k0 + tk, :] = dv.astype(BF16)
        dq_ref[...] = dq_acc[...].astype(BF16)

    return pl.pallas_call(
        body, name="flash_bwd",
        out_shape=[jax.ShapeDtypeStruct((T, N_HEADS * QK_PAD), BF16),
                   jax.ShapeDtypeStruct((T, N_HEADS * QK_PAD), BF16),
                   jax.ShapeDtypeStruct((T, N_HEADS * NOPE), BF16)],
        grid=(N_HEADS,),
        in_specs=[pl.BlockSpec((T, QK_PAD), lambda h: (0, h)),
                  pl.BlockSpec((T, QK_PAD), lambda h: (0, h)),
                  pl.BlockSpec((T, QK_PAD), lambda h: (0, h)),
                  pl.BlockSpec((T, QK_PAD), lambda h: (0, h)),
                  pl.BlockSpec((T, NOPE), lambda h: (0, h))],
        out_specs=[pl.BlockSpec((T, QK_PAD), lambda h: (0, h)),
                   pl.BlockSpec((T, QK_PAD), lambda h: (0, h)),
                   pl.BlockSpec((T, NOPE), lambda h: (0, h))],
        scratch_shapes=[pltpu.VMEM((T, QK_PAD), F32)],
        compiler_params=_params("arbitrary"),
    )(q, k, v, do, lse)


SUB = 128


def _windows(buf, t0, c0, first, last):
    span = SUB + 8 * ((last + 7) // 8)
    base = buf[t0:t0 + span, c0:c0 + SUB]
    for r in range(8):
        offs = [o for o in range(first, last + 1) if o % 8 == r]
        if offs:
            win = base if r == 0 else pltpu.roll(base, span - r, 0)
            for o in offs:
                yield o, win[o - r:o - r + SUB, :]


def _glu(a, b):
    return a.astype(F32) * _sigmoid(b.astype(F32))


def _conv_fwd(z, w, bias, *, tm=256):
    T, C = z.shape[0], D_MODEL
    tm = min(tm, T)
    hb = tm // HALO
    lead = HALO - (CONV_W - 1)

    def body(a_ref, b_ref, ah_ref, bh_ref, w_ref, bias_ref, o_ref, buf):
        t = pl.program_id(0)
        buf[0:HALO, :] = jnp.where(t > 0, _glu(ah_ref[...], bh_ref[...]), 0.0)
        buf[HALO:, :] = _glu(a_ref[...], b_ref[...])
        for c0 in range(0, C, SUB):
            for t0 in range(0, tm, SUB):
                acc = jnp.broadcast_to(bias_ref[:, c0:c0 + SUB], (SUB, SUB))
                for o, win in _windows(buf, t0, c0, lead, lead + CONV_W - 1):
                    acc = acc + win * w_ref[o - lead:o - lead + 1, c0:c0 + SUB]
                o_ref[t0:t0 + SUB, c0:c0 + SUB] = acc

    def halo(cb):
        return pl.BlockSpec((HALO, C), lambda t: (jnp.maximum(t * hb - 1, 0), cb))

    return pl.pallas_call(
        body, name="conv_fwd",
        out_shape=jax.ShapeDtypeStruct((T, C), F32),
        grid=(T // tm,),
        in_specs=[pl.BlockSpec((tm, C), lambda t: (t, Z_A)), pl.BlockSpec((tm, C), lambda t: (t, Z_B)),
                  halo(Z_A), halo(Z_B),
                  pl.BlockSpec((CONV_TAPS_PAD, C), lambda t: (0, 0)), pl.BlockSpec((1, C), lambda t: (0, 0))],
        out_specs=pl.BlockSpec((tm, C), lambda t: (t, 0)),
        scratch_shapes=[pltpu.VMEM((HALO + tm, C), F32)],
        compiler_params=_params("parallel"),
    )(z, z, z, z, w, bias)


def _conv_bwd(z, du1, w, dz, *, tm=256):
    T, C = z.shape[0], D_MODEL
    tm = min(tm, T)
    hb = tm // HALO
    nt = T // tm
    last_halo = T // HALO - 1
    lead = HALO - (CONV_W - 1)
    assert (Z_A, Z_B) == (0, 1)

    def body(a_ref, b_ref, ah_ref, bh_ref, d_ref, dh_ref, w_ref, dz_in, dz_ref, dw_ref, ubuf, dbuf, dw8, gate):
        t = pl.program_id(0)
        ubuf[0:HALO, :] = jnp.where(t > 0, _glu(ah_ref[...], bh_ref[...]), 0.0)
        gate[...] = _sigmoid(b_ref[...].astype(F32))
        ubuf[HALO:, :] = a_ref[...].astype(F32) * gate[...]
        dbuf[0:tm, :] = d_ref[...]
        dbuf[tm:, :] = jnp.where(t < nt - 1, dh_ref[...], 0.0)

        @pl.when(t == 0)
        def _():
            dw8[...] = jnp.zeros_like(dw8)

        for c0 in range(0, C, SUB):
            for t0 in range(0, tm, SUB):
                acc = jnp.zeros((SUB, SUB), F32)
                for o, win in _windows(dbuf, t0, c0, 0, CONV_W - 1):
                    kk = CONV_W - 1 - o
                    acc = acc + win * w_ref[kk:kk + 1, c0:c0 + SUB]
                av = a_ref[t0:t0 + SUB, c0:c0 + SUB].astype(F32)
                sg = gate[t0:t0 + SUB, c0:c0 + SUB]
                dz_ref[t0:t0 + SUB, c0:c0 + SUB] = (acc * sg).astype(dz_ref.dtype)
                dz_ref[t0:t0 + SUB, C + c0:C + c0 + SUB] = (acc * av * sg * (1.0 - sg)).astype(dz_ref.dtype)
                dsub = d_ref[t0:t0 + SUB, c0:c0 + SUB]
                for o, win in _windows(ubuf, t0, c0, lead, lead + CONV_W - 1):
                    kk = o - lead
                    dw8[8 * kk:8 * kk + 8, c0:c0 + SUB] += jnp.sum(
                        (dsub * win).reshape(SUB // 8, 8, SUB), axis=0)

        @pl.when(t == nt - 1)
        def _():
            dw_ref[...] = jnp.sum(dw8[...].reshape(CONV_TAPS_PAD, 8, C), axis=1)

    def halo(cb):
        return pl.BlockSpec((HALO, C), lambda t: (jnp.maximum(t * hb - 1, 0), cb))

    return pl.pallas_call(
        body, name="conv_bwd",
        out_shape=[jax.ShapeDtypeStruct(dz.shape, dz.dtype), jax.ShapeDtypeStruct((CONV_TAPS_PAD, C), F32)],
        grid=(nt,),
        in_specs=[pl.BlockSpec((tm, C), lambda t: (t, Z_A)), pl.BlockSpec((tm, C), lambda t: (t, Z_B)),
                  halo(Z_A), halo(Z_B),
                  pl.BlockSpec((tm, C), lambda t: (t, 0)),
                  pl.BlockSpec((HALO, C), lambda t: (jnp.minimum((t + 1) * hb, last_halo), 0)),
                  pl.BlockSpec((CONV_TAPS_PAD, C), lambda t: (0, 0)), ANY],
        out_specs=[pl.BlockSpec((tm, 2 * C), lambda t: (t, 0)),
                   pl.BlockSpec((CONV_TAPS_PAD, C), lambda t: (0, 0))],
        input_output_aliases={7: 0},
        scratch_shapes=[pltpu.VMEM((HALO + tm, C), F32), pltpu.VMEM((tm + HALO, C), F32),
                        pltpu.VMEM((CONV_TAPS_PAD * 8, C), F32), pltpu.VMEM((tm, C), F32)],
        compiler_params=_params("arbitrary"),
    )(z, z, z, z, du1, du1, w, dz)


def _adamw(w, g, m, v, *, name, g_row=None):
    R, C = w.shape
    tr = _row_tile(R, max(8, (1 << 18) // C // 8 * 8))
    first = 0 if g_row is None else g_row // tr
    assert g_row is None or g_row % tr == 0

    def body(w_ref, g_ref, m_ref, v_ref, d_ref, nm_ref, nv_ref, *g_out):
        gv = g_ref[...]
        nm = ADAM_B1 * m_ref[...] + (1.0 - ADAM_B1) * gv
        nv = ADAM_B2 * v_ref[...] + (1.0 - ADAM_B2) * (gv * gv)
        m_hat = nm / (1.0 - ADAM_B1 ** ADAM_STEP)
        v_hat = nv / (1.0 - ADAM_B2 ** ADAM_STEP)
        d_ref[...] = -ADAM_LR * (m_hat / (jnp.sqrt(v_hat) + ADAM_EPS) + ADAM_WD * w_ref[...])
        nm_ref[...] = nm
        nv_ref[...] = nv
        for ref in g_out:
            ref[...] = gv

    spec = pl.BlockSpec((tr, C), lambda i: (i, 0))
    g_spec = pl.BlockSpec((tr, C), lambda i: (first + i, 0))
    n_out = 3 if g_row is None else 4
    return pl.pallas_call(
        body, name=name, out_shape=[jax.ShapeDtypeStruct((R, C), F32)] * n_out,
        grid=(R // tr,), in_specs=[spec, g_spec, spec, spec], out_specs=[spec] * n_out,
        compiler_params=_params("parallel"),
    )(w, g, m, v)


def _coords():
    return lax.axis_index("x"), lax.axis_index("y"), lax.axis_index("c")


def _remote(src, dst, ssem, rsem, to):
    return pltpu.make_async_remote_copy(src_ref=src, dst_ref=dst, send_sem=ssem, recv_sem=rsem,
                                        device_id=to, device_id_type=MESH)


def _half(c, rows):
    return pl.ds(pl.multiple_of(c * (rows // 2), 16), rows // 2)


def _gather_weights(wpack, cw):
    def body(w_ref, cw_ref, full_ref, cwf_ref, ssem, rsem):
        x, y, c = _coords()
        q = 2 * x + y
        sib = (x, y, 1 - c)
        chips = [(1 - x, y), (x, 1 - y), (1 - x, 1 - y)]
        mine, other = _half(c, w_ref.shape[0]), _half(1 - c, w_ref.shape[0])
        first = []
        for j, (px, py) in enumerate(chips):
            first.append(_remote(w_ref.at[mine], full_ref.at[q, mine], ssem.at[j], rsem.at[j], (px, py, c)))
            first.append(_remote(cw_ref, cwf_ref.at[q], ssem.at[6 + j], rsem.at[6 + j], (px, py, c)))
        for cp in first:
            cp.start()
        passed = []
        for j, (px, py) in enumerate(chips):
            pq = 2 * px + py
            _remote(w_ref.at[mine], full_ref.at[pq, mine], ssem.at[j], rsem.at[j], (px, py, c)).wait_recv()
            fw = _remote(full_ref.at[pq, mine], full_ref.at[pq, mine], ssem.at[3 + j], rsem.at[3 + j], sib)
            fw.start()
            passed.append(fw)
        for j, (px, py) in enumerate(chips):
            pq = 2 * px + py
            _remote(full_ref.at[pq, other], full_ref.at[pq, other], ssem.at[3 + j], rsem.at[3 + j], sib).wait_recv()
            _remote(cw_ref, cwf_ref.at[pq], ssem.at[6 + j], rsem.at[6 + j], (px, py, c)).wait_recv()
        for cp in first + passed:
            cp.wait_send()

    return pl.pallas_call(
        body, name="gather_weights",
        out_shape=[jax.ShapeDtypeStruct((N_CHIPS,) + wpack.shape, wpack.dtype),
                   jax.ShapeDtypeStruct((N_CHIPS,) + cw.shape, cw.dtype)],
        in_specs=[ANY, ANY], out_specs=[ANY, ANY],
        scratch_shapes=[pltpu.SemaphoreType.DMA((9,)), pltpu.SemaphoreType.DMA((9,))],
    )(wpack, cw)


SEM = pl.BlockSpec(memory_space=pltpu.SEMAPHORE)
HBM = pl.BlockSpec(memory_space=pltpu.HBM)
N_LATE = 6


def _late_peers(x, y, c):
    out = []
    for j, (px, py) in enumerate([(1 - x, y), (x, 1 - y), (1 - x, 1 - y)]):
        for t in (0, 1):
            out.append((2 * j + t, px, py, c if t == 0 else 1 - c))
    return out


def _late_gather_start(w, after):
    rows = w.shape[0]

    def body(w_ref, land_ref, after_ref, *outs):
        sems = outs[:2 * N_LATE]
        token = outs[2 * N_LATE + 2]
        x, y, c = _coords()
        q = 2 * x + y
        mine = _half(c, rows)
        for k, px, py, pc in _late_peers(x, y, c):
            _remote(w_ref.at[mine], land_ref.at[q, mine], sems[k], sems[N_LATE + k], (px, py, pc)).start()
        token[...] = jnp.zeros_like(token)

    res = pl.pallas_call(
        body, name="late_gather_start",
        out_shape=tuple([pltpu.SemaphoreType.DMA(())] * (2 * N_LATE)) + (
            pltpu.HBM(w.shape, w.dtype), pltpu.HBM((N_CHIPS,) + w.shape, w.dtype),
            jax.ShapeDtypeStruct((8, 128), F32)),
        in_specs=(HBM, HBM, ANY),
        out_specs=tuple([SEM] * (2 * N_LATE)) + (HBM, HBM, pl.BlockSpec(memory_space=pltpu.VMEM)),
        input_output_aliases={0: 2 * N_LATE, 1: 2 * N_LATE + 1},
        compiler_params=pltpu.CompilerParams(has_side_effects=pltpu.SideEffectType.DATAFLOW_SIDE_EFFECTING),
    )(pltpu.with_memory_space_constraint(w, pltpu.HBM),
      pltpu.with_memory_space_constraint(lax.empty((N_CHIPS,) + w.shape, w.dtype), pltpu.HBM), after)
    return res[:2 * N_LATE], res[2 * N_LATE], res[2 * N_LATE + 1], res[2 * N_LATE + 2]


def _late_gather_wait(sems, w_thru, land_thru, after):
    rows = w_thru.shape[0]

    def body(w_ref, land_ref, *rest):
        sems = rest[:2 * N_LATE]
        x, y, c = _coords()
        for k, px, py, pc in _late_peers(x, y, c):
            cp = _remote(w_ref.at[_half(pc, rows)], land_ref.at[2 * px + py, _half(pc, rows)],
                         sems[k], sems[N_LATE + k], (px, py, pc))
            cp.wait_send()
            cp.wait_recv()

    return pl.pallas_call(
        body, name="late_gather_wait",
        out_shape=(pltpu.HBM(w_thru.shape, w_thru.dtype), pltpu.HBM(land_thru.shape, land_thru.dtype)),
        in_specs=(HBM, HBM) + tuple([SEM] * (2 * N_LATE)) + (ANY,),
        out_specs=(HBM, HBM), input_output_aliases={0: 0, 1: 1},
        compiler_params=pltpu.CompilerParams(has_side_effects=pltpu.SideEffectType.DATAFLOW_SIDE_EFFECTING),
    )(w_thru, land_thru, *sems, after)[1]


N_PEERS = N_DEV - 1


def _peers(x, y, c):
    out = []
    for r in range(1, N_DEV):
        out.append((r - 1, 1 - x if r & 4 else x, 1 - y if r & 2 else y, 1 - c if r & 1 else c))
    return out


def _piece(g_ref, px, py, pc, whole):
    return g_ref if whole else g_ref.at[2 * px + py, _half(pc, g_ref.shape[1])]


def _reduce_start(g16, after, *, name, whole=False):
    def body(g_ref, land_ref, after_ref, *outs):
        sems = outs[:2 * N_PEERS]
        token = outs[2 * N_PEERS + 2]
        x, y, c = _coords()
        for k, px, py, pc in _peers(x, y, c):
            _remote(_piece(g_ref, px, py, pc, whole), land_ref.at[k], sems[k], sems[N_PEERS + k],
                    (px, py, pc)).start()
        token[...] = jnp.zeros_like(token)

    land_shape = (N_PEERS,) + (g16.shape if whole else (g16.shape[1] // 2, 1024))
    res = pl.pallas_call(
        body, name=name,
        out_shape=tuple([pltpu.SemaphoreType.DMA(())] * (2 * N_PEERS)) + (
            pltpu.HBM(g16.shape, g16.dtype), pltpu.HBM(land_shape, g16.dtype),
            jax.ShapeDtypeStruct((8, 128), F32)),
        in_specs=(HBM, HBM, ANY),
        out_specs=tuple([SEM] * (2 * N_PEERS)) + (HBM, HBM, pl.BlockSpec(memory_space=pltpu.VMEM)),
        input_output_aliases={0: 2 * N_PEERS, 1: 2 * N_PEERS + 1},
        compiler_params=pltpu.CompilerParams(has_side_effects=pltpu.SideEffectType.DATAFLOW_SIDE_EFFECTING),
    )(pltpu.with_memory_space_constraint(g16, pltpu.HBM),
      pltpu.with_memory_space_constraint(lax.empty(land_shape, g16.dtype), pltpu.HBM), after)
    return res[:2 * N_PEERS], res[2 * N_PEERS], res[2 * N_PEERS + 1], res[2 * N_PEERS + 2]


def _reduce_wait(sems, g_thru, land_thru, after, *, name, whole=False):
    def body(g_ref, land_ref, *rest):
        sems = rest[:2 * N_PEERS]
        x, y, c = _coords()
        for k, px, py, pc in _peers(x, y, c):
            cp = _remote(_piece(g_ref, px, py, pc, whole), land_ref.at[k], sems[k], sems[N_PEERS + k],
                         (px, py, pc))
            cp.wait_send()
            cp.wait_recv()

    return pl.pallas_call(
        body, name=name,
        out_shape=(pltpu.HBM(g_thru.shape, g_thru.dtype), pltpu.HBM(land_thru.shape, land_thru.dtype)),
        in_specs=(HBM, HBM) + tuple([SEM] * (2 * N_PEERS)) + tuple([ANY] * len(after)),
        out_specs=(HBM, HBM), input_output_aliases={0: 0, 1: 1},
        compiler_params=pltpu.CompilerParams(has_side_effects=pltpu.SideEffectType.DATAFLOW_SIDE_EFFECTING),
    )(g_thru, land_thru, *sems, *after)[1]


def _reduce_sum(place, g32, land, *, name):
    rows = g32.shape[1]
    nb = rows // 2 // SUM_TILE

    def body(place_ref, g_ref, l_ref, o_ref):
        acc = g_ref[...]
        for k in range(N_PEERS):
            acc = acc + l_ref[k].astype(F32)
        o_ref[...] = acc

    return pl.pallas_call(
        body, name=name,
        out_shape=jax.ShapeDtypeStruct((rows, 1024), F32),
        grid_spec=pltpu.PrefetchScalarGridSpec(
            num_scalar_prefetch=1, grid=(nb,),
            in_specs=[pl.BlockSpec((None, SUM_TILE, 1024), lambda i, s: (s[0], s[1] * nb + i, 0)),
                      pl.BlockSpec((N_PEERS, SUM_TILE, 1024), lambda i, s: (0, i, 0))],
            out_specs=pl.BlockSpec((SUM_TILE, 1024), lambda i, s: (s[1] * nb + i, 0))),
        compiler_params=_params("parallel"),
    )(place, g32, land)


def _join_halves(red, *, name):
    rows = red.shape[0]

    def body(r_ref, o_ref, ssem, rsem):
        x, y, c = _coords()
        cp = _remote(r_ref.at[_half(c, rows)], o_ref.at[_half(c, rows)], ssem, rsem, (x, y, 1 - c))
        cp.start()
        _remote(r_ref.at[_half(1 - c, rows)], o_ref.at[_half(1 - c, rows)], ssem, rsem,
                (x, y, 1 - c)).wait_recv()
        cp.wait_send()

    return pl.pallas_call(
        body, name=name,
        out_shape=jax.ShapeDtypeStruct(red.shape, F32),
        in_specs=[ANY], out_specs=ANY, input_output_aliases={0: 0},
        scratch_shapes=[pltpu.SemaphoreType.DMA, pltpu.SemaphoreType.DMA],
    )(red)


def _sum_small(me, svec, land, *, name):
    def body(me_ref, s_ref, l_ref, o_ref):
        mine = me_ref[0]
        acc = None
        for d in range(N_DEV):
            theirs = l_ref[jnp.maximum(jnp.bitwise_xor(mine, d) - 1, 0)]
            v = jnp.where(mine == d, s_ref[...], theirs)
            acc = v if acc is None else acc + v
        o_ref[...] = acc

    return pl.pallas_call(
        body, name=name,
        out_shape=jax.ShapeDtypeStruct(svec.shape, F32),
        grid_spec=pltpu.PrefetchScalarGridSpec(
            num_scalar_prefetch=1, grid=(1,),
            in_specs=[pl.BlockSpec(svec.shape, lambda i, s: (0, 0)),
                      pl.BlockSpec(land.shape, lambda i, s: (0, 0, 0))],
            out_specs=pl.BlockSpec(svec.shape, lambda i, s: (0, 0))),
        compiler_params=_params("arbitrary"),
    )(me, svec, land)


def _pack_early_shards(wts):
    parts = [wts["w_in"][0].T.astype(BF16)] + [wts[n].astype(BF16).reshape(-1, 1024) for n, _ in EARLY[1:]]
    used = sum(r for _, r in EARLY)
    return jnp.concatenate(parts + [jnp.zeros((EARLY_ROWS - used, 1024), BF16)], axis=0)


def _pack_late_shards(wts):
    parts, at = [], 0
    for n, off, rows in LATE:
        if off > at:
            parts.append(jnp.zeros((off - at, 1024), BF16))
        parts.append(wts[n].astype(BF16).reshape(rows, 1024))
        at = off + rows
    return jnp.concatenate(parts, axis=0)


def _unpack_early(full):
    w = {}
    w["w_in_t"] = jnp.concatenate(
        [full[0, N_SMALL:1200]] + [full[p, :1200] for p in range(1, N_CHIPS)]
        + [full[0, :N_SMALL], jnp.zeros((D_MODEL - N_SMALL, 1024), full.dtype)], axis=0)
    w["w_uq"] = full[:, 1200:1344].reshape(4, 384, 384).transpose(1, 0, 2).reshape(384, 1536)
    w["w_uk"] = full[:, 1344:1408].reshape(256, 1024)
    w["w_uv"] = full[:, 1408:1472].reshape(256, 1024)
    return w


def _pack_early_grads(g):
    gt = g["w_in_t"]
    rest = 1200 - N_SMALL
    w_in = jnp.stack([jnp.concatenate([gt[4 * D_MODEL:4 * D_MODEL + N_SMALL], gt[:rest]], axis=0)]
                     + [gt[rest + 1200 * (p - 1):rest + 1200 * p] for p in range(1, N_CHIPS)])
    parts = [
        w_in,
        g["w_uq"].reshape(384, 4, 384).transpose(1, 0, 2).reshape(4, 144, 1024),
        g["w_uk"].reshape(4, 64, 1024),
        g["w_uv"].reshape(4, 64, 1024),
        jnp.zeros((4, EARLY_ROWS - sum(r for _, r in EARLY), 1024), F32),
    ]
    return jnp.concatenate(parts, axis=1)


def _pad_w_uq(w):
    w = w.reshape(Q_RANK, N_HEADS, NOPE + ROPE)
    return jnp.pad(w, ((0, 0), (0, 0), (0, QK_PAD - NOPE - ROPE))).reshape(Q_RANK, N_HEADS * QK_PAD)


def _local_step(xs, pos, tgt, W, late_weights, late_bufs, send_late_grads, send_small_grads, send_early_grads,
                cw, vec):
    D = D_MODEL
    w_in_t = W["w_in_t"]
    w_uq = _pad_w_uq(W["w_uq"])
    w_ukv = jnp.concatenate([W["w_uk"], W["w_uv"]], axis=1)
    inv = ROPE_THETA ** (-jnp.arange(0, ROPE, 2, dtype=F32) / ROPE)
    inv_freq = jnp.concatenate([inv, inv, jnp.zeros((128 - ROPE,), F32)])[None, :]

    (h,) = _rowwise(lambda x, g: (_rms(x, g),), [(xs, D, 0)], [vec["norm_mix_pre"]], [(D, BF16)],
                    name="rms_pre")
    z = _mm(h, w_in_t, tb=True, out_dtype=BF16, name="mm_in")

    def qkv_prep(zs, p, qn, kvn, invf):
        ang = p.astype(F32) * invf
        cosv, sinv = jnp.cos(ang), jnp.sin(ang)
        ln = lax.broadcasted_iota(jnp.int32, ang.shape, 1)
        cs = jnp.where(ln < ROPE, cosv, 0.0)
        sa = jnp.where((ln >= ROPE // 2) & (ln < ROPE), sinv, 0.0)
        sb = jnp.where(ln < ROPE // 2, -sinv, 0.0)
        cqn = _rms(zs[:, :Q_RANK], qn)
        ckvn = _rms(zs[:, Q_RANK:Q_RANK + KV_RANK], kvn)
        kr = _rope(zs[:, 640:768], cs, sa, sb)
        return cqn, ckvn, kr, cs, sa, sb

    cqn, ckvn, krot, cs, sa, sb = _rowwise(
        qkv_prep, [(z, D, Z_S), (pos, 1, 0)], [vec["q_norm"], vec["kv_norm"], inv_freq],
        [(Q_RANK, BF16), (KV_RANK, BF16), (128, BF16), (128, F32), (128, F32), (128, F32)],
        name="qkv_prep")
    q_pre = _mm(cqn, w_uq, out_dtype=BF16, name="mm_uq")
    kv = _mm(ckvn, w_ukv, out_dtype=BF16, name="mm_ukv")

    def qk_rope(qp, kn, vv, kr, cs_, sa_, sb_):
        qs, ks, vs = [], [], []
        ones = jnp.ones((qp.shape[0], NOPE), F32)
        for hd in range(N_HEADS):
            qs.append(qp[:, hd * QK_PAD:hd * QK_PAD + NOPE])
            qs.append(_rope(qp[:, hd * QK_PAD + NOPE:(hd + 1) * QK_PAD], cs_, sa_, sb_))
            ks.append(kn[:, hd * NOPE:(hd + 1) * NOPE])
            ks.append(kr.astype(F32))
            vs.append(vv[:, hd * NOPE:(hd + 1) * NOPE])
            vs.append(ones)
        return (jnp.concatenate(qs, axis=1) * Q_SCALE, jnp.concatenate(ks, axis=1),
                jnp.concatenate(vs, axis=1))

    q, k, v = _rowwise(
        qk_rope, [(q_pre, 2048, 0), (kv, 1024, 0), (kv, 1024, 1), (krot, 128, 0), (cs, 128, 0),
                  (sa, 128, 0), (sb, 128, 0)], [],
        [(2048, BF16), (2048, BF16), (2048, BF16)], name="qk_rope")
    attn, lse = _flash_fwd(q, k, v)
    LW = late_weights(attn)
    y_attn = LW.fwd(attn, "w_o_attn", out_dtype=BF16, name="mm_o_attn")

    u1 = _conv_fwd(z, cw, vec["conv_b"])

    def ln_parts(u, g, b):
        mu = jnp.mean(u, axis=-1, keepdims=True)
        xc = u - mu
        rs = lax.rsqrt(jnp.mean(xc * xc, axis=-1, keepdims=True) + EPS)
        nh = xc * rs
        return nh, rs, nh * g + b

    def ln_silu(u, g, b):
        y = ln_parts(u, g, b)[2]
        return (y * _sigmoid(y),)

    (u3,) = _rowwise(ln_silu, [(u1, D, 0)], [vec["conv_ln_g"], vec["conv_ln_b"]], [(D, BF16)],
                     name="ln_silu")

    def merge(yc, ga, gc, ya, b2):
        return yc, _sigmoid(ga) * ya + _sigmoid(gc) * (yc + b2)

    y_conv, merged = LW.fwd(u3, "w_pw2", name="mm_pw2", tm=1024, epilogue=merge, out_dtypes=(BF16, BF16),
                            extras=[(z, Z_GA), (z, Z_GC), y_attn, vec["b_pw2"]])

    def post1(m, x, g1, g2):
        x1 = x + _rms(m, g1)
        return m, x1, _rms(x1, g2)

    mo, x1, h2 = LW.fwd(merged, "w_out", name="mm_out", tm=512, epilogue=post1, out_dtypes=(F32, F32, BF16),
                        extras=[xs, vec["norm_mix_post"], vec["norm_mlp_pre"]])

    def sqrelu(acc):
        r = jnp.maximum(acc, 0.0)
        return r * r, r

    a2, r1 = LW.fwd(h2, "w_ff1", name="mm_ff1", tm=2048, epilogue=sqrelu, out_dtypes=(BF16, BF16))

    def post2(f_, x1_, t_, g):
        e = x1_ + _rms(f_, g) - t_
        dy = e * (1.0 / D)
        df, dg = _rms_bwd(f_, g, dy)
        loss = jnp.broadcast_to(jnp.sum(e * e, keepdims=True) * (0.5 / D), (1, 128))
        return dy, df, dg, loss

    dy, df, g_norm_mlp_post, loss = LW.fwd(a2, "w_ff2", name="mm_ff2", epilogue=post2, out_dtypes=(F32, BF16),
                                           extras=[x1, tgt, vec["norm_mlp_post"]], reds=(D, 128))
    late_bufs = _late_grad(a2, df, "w_ff2", late_bufs, name="mm_dw_ff2")
    df1 = LW.bwd(df, "w_ff2", name="mm_df1", tm=2048, extras=[r1], out_dtypes=(BF16,),
              epilogue=lambda acc, r: (acc * (2.0 * r.astype(F32)),))
    late_bufs = _late_grad(h2, df1, "w_ff1", late_bufs, name="mm_dw_ff1")

    def bwd_mid(dh2_, dy_, x1_, m, g2, g1):
        d1, dg2 = _rms_bwd(x1_, g2, dh2_)
        dx1_ = dy_ + d1
        dm, dg1 = _rms_bwd(m, g1, dx1_)
        return dx1_, dm, dg2, dg1

    dx1, dmo, g_norm_mlp_pre, g_norm_mix_post = LW.bwd(
        df1, "w_ff1", name="mm_dh2", tm=512, epilogue=bwd_mid, out_dtypes=(F32, BF16), reds=(D, D),
        extras=[dy, x1, mo, vec["norm_mlp_pre"], vec["norm_mix_post"]])
    late_bufs = _late_grad(merged, dmo, "w_out", late_bufs, name="mm_dw_out")
    def dmerge(dm, ga, gc, ya, yc, b2):
        sga, sgc = _sigmoid(ga), _sigmoid(gc)
        dya = dm * sga
        dyc = dm * sgc
        dga = dm * ya * sga * (1.0 - sga)
        dgc = dm * (yc + b2) * sgc * (1.0 - sgc)
        return dya, dyc, jnp.concatenate([dga, dgc], axis=1), jnp.sum(dyc, axis=0, keepdims=True)

    dz = lax.empty((xs.shape[0], Z_COLS), BF16)
    tm_dm = _tile(xs.shape[0], 512)
    gates = pl.BlockSpec((tm_dm, 2 * D), lambda i, j, k: (i, Z_GA // 2))
    dya, dyc, dz, g_b_pw2 = LW.bwd(
        dmo, "w_out", name="mm_dmerged", tm=tm_dm, epilogue=dmerge, out_dtypes=(BF16, BF16, (dz, gates)),
        reds=(D,), extras=[(z, Z_GA), (z, Z_GC), y_attn, y_conv, vec["b_pw2"]])

    late_bufs = _late_grad(attn, dya, "w_o_attn", late_bufs, name="mm_dw_o_attn")
    late_bufs = _late_grad(u3, dyc, "w_pw2", late_bufs, name="mm_dw_pw2")
    late_token = send_late_grads(late_bufs)

    def delta_fn(acc, o_):
        do32 = acc.astype(BF16).astype(F32)
        pr = do32 * o_
        ln = lax.broadcasted_iota(jnp.int32, (pr.shape[0], NOPE), 1)
        cols = []
        for hd in range(N_HEADS):
            dl = jnp.sum(pr[:, hd * NOPE:(hd + 1) * NOPE], axis=-1, keepdims=True)
            hi = dl.astype(BF16).astype(F32)
            cols.append(do32[:, hd * NOPE:(hd + 1) * NOPE])
            cols.append(jnp.where(ln == 0, -hi, jnp.where(ln == 1, hi - dl, 0.0)))
        return (jnp.concatenate(cols, axis=1),)

    tm_do = _tile(xs.shape[0], 1024)
    do_ext = LW.bwd(dya, "w_o_attn", name="mm_dattn", tm=tm_do, epilogue=delta_fn, extras=[attn],
                    after=[late_token], out_dtypes=(
                        (lax.empty((xs.shape[0], N_HEADS * QK_PAD), BF16),
                         pl.BlockSpec((tm_do, N_HEADS * QK_PAD), lambda i, j, k: (i, 0))),))
    dq, dk, dv = _flash_bwd(q, k, v, do_ext, lse)

    def unrope(dq_, dk_, dv_, cs_, sa_, sb_):
        dq_, dk_ = dq_.astype(F32), dk_.astype(F32)
        qs, kn = [], []
        dkr = jnp.zeros_like(cs_)
        for hd in range(N_HEADS):
            qs.append(dq_[:, hd * QK_PAD:hd * QK_PAD + NOPE] * SCALE)
            qs.append(_rope_t(dq_[:, hd * QK_PAD + NOPE:(hd + 1) * QK_PAD] * SCALE, cs_, sa_, sb_))
            kn.append(dk_[:, hd * QK_PAD:hd * QK_PAD + NOPE] * K_UNSCALE)
            dkr = dkr + dk_[:, hd * QK_PAD + NOPE:(hd + 1) * QK_PAD]
        return (jnp.concatenate(qs, axis=1), jnp.concatenate(kn + [dv_], axis=1),
                _rope_t(dkr * K_UNSCALE, cs_, sa_, sb_))

    dq_pre, dkv, dkr = _rowwise(
        unrope, [(dq, 2048, 0), (dk, 2048, 0), (dv, 1024, 0), (cs, 128, 0), (sa, 128, 0), (sb, 128, 0)], [],
        [(2048, BF16), (2048, BF16), (128, F32)], name="unrope")
    g_w_uq = _mm(cqn, dq_pre, ta=True, name="mm_dw_uq")
    dcqn = _mm(dq_pre, w_uq, tb=True, out_dtype=BF16, name="mm_dcqn")
    g_w_ukv = _mm(ckvn, dkv, ta=True, name="mm_dw_ukv")
    dckvn = _mm(dkv, w_ukv, tb=True, out_dtype=BF16, name="mm_dckvn")

    def small_bwd(zs, dcq_, dckv_, dkr_, qn, kvn):
        dcq, dqn = _rms_bwd(zs[:, :Q_RANK], qn, dcq_)
        dckv, dkvn = _rms_bwd(zs[:, Q_RANK:Q_RANK + KV_RANK], kvn, dckv_)
        pad = jnp.zeros((zs.shape[0], D - 768), F32)
        return jnp.concatenate([dcq, dckv, dkr_, pad], axis=1), dqn, dkvn

    dz, g_q_norm, g_kv_norm = _rowwise(
        small_bwd, [(z, D, Z_S), (dcqn, Q_RANK, 0), (dckvn, KV_RANK, 0), (dkr, 128, 0)],
        [vec["q_norm"], vec["kv_norm"]], [(D, dz, Z_S)], [(1, Q_RANK), (1, KV_RANK)], name="small_bwd")

    def ln_silu_bwd(d3, u, g, b):
        nh, rs, y = ln_parts(u, g, b)
        sg = _sigmoid(y)
        dyv = d3 * (sg * (1.0 + y * (1.0 - sg)))
        dnh = dyv * g
        du = rs * (dnh - jnp.mean(dnh, axis=-1, keepdims=True)
                   - nh * jnp.mean(dnh * nh, axis=-1, keepdims=True))
        return (du, jnp.sum(dyv * nh, axis=0, keepdims=True), jnp.sum(dyv, axis=0, keepdims=True),
                jnp.sum(du, axis=0, keepdims=True))

    du1, g_ln_g, g_ln_b, g_conv_b = LW.bwd(
        dyc, "w_pw2", name="mm_du3", tm=1024, epilogue=ln_silu_bwd, out_dtypes=(F32,), reds=(D, D, D),
        extras=[u1, vec["conv_ln_g"], vec["conv_ln_b"]])
    dz, g_conv_w = _conv_bwd(z, du1, cw, dz)
    small_token = send_small_grads({
        "conv_b": g_conv_b, "conv_ln_g": g_ln_g, "conv_ln_b": g_ln_b, "b_pw2": g_b_pw2,
        "norm_mix_post": g_norm_mix_post, "norm_mlp_pre": g_norm_mlp_pre, "norm_mlp_post": g_norm_mlp_post,
        "q_norm": g_q_norm, "kv_norm": g_kv_norm}, g_conv_w, loss)
    g_w_in_t = _mm(dz, h, ta=True, name="mm_dw_in", after=[small_token])
    early_token = send_early_grads({
        "w_in_t": g_w_in_t,
        "w_uq": g_w_uq.reshape(Q_RANK, N_HEADS, QK_PAD)[:, :, :NOPE + ROPE].reshape(Q_RANK, 1536),
        "w_uk": g_w_ukv[:, :1024], "w_uv": g_w_ukv[:, 1024:]})

    def final(dh_, dx1_, x, g):
        d, dg = _rms_bwd(x, g, dh_)
        return dx1_ + d, dg

    grad_x, g_norm_mix_pre = _mm(
        dz, w_in_t, name="mm_dh", after=[early_token], tiles=(_tile(xs.shape[0], 512), D, Z_COLS),
        b_spec=pl.BlockSpec((Z_COLS, D), lambda i, j, k: (0, 0), pipeline_mode=pl.Buffered(1)),
        epilogue=final, out_dtypes=(F32,), reds=(D,), extras=[dx1, xs, vec["norm_mix_pre"]])

    return grad_x, g_norm_mix_pre


def _row1024(a):
    return jnp.pad(a, ((0, 0), (0, 1024 - a.shape[1])))


ORDER = ("norm_mix_pre", "w_in", "q_norm", "w_uq", "kv_norm", "w_uk", "w_uv", "w_o_attn", "conv_w",
         "conv_b", "conv_ln_g", "conv_ln_b", "w_pw2", "b_pw2", "w_out", "norm_mix_post", "norm_mlp_pre",
         "w_ff1", "w_ff2", "norm_mlp_post")


def kernel(x, positions, norm_mix_pre, w_in, q_norm, w_uq, kv_norm, w_uk, w_uv, w_o_attn, conv_w, conv_b, conv_ln_g, conv_ln_b, w_pw2, b_pw2, w_out, norm_mix_post, norm_mlp_pre, w_ff1, w_ff2, norm_mlp_post, loss_target, m_norm_mix_pre, m_w_in, m_q_norm, m_w_uq, m_kv_norm, m_w_uk, m_w_uv, m_w_o_attn, m_conv_w, m_conv_b, m_conv_ln_g, m_conv_ln_b, m_w_pw2, m_b_pw2, m_w_out, m_norm_mix_post, m_norm_mlp_pre, m_w_ff1, m_w_ff2, m_norm_mlp_post, v_norm_mix_pre, v_w_in, v_q_norm, v_w_uq, v_kv_norm, v_w_uk, v_w_uv, v_w_o_attn, v_conv_w, v_conv_b, v_conv_ln_g, v_conv_ln_b, v_w_pw2, v_b_pw2, v_w_out, v_norm_mix_post, v_norm_mlp_pre, v_w_ff1, v_w_ff2, v_norm_mlp_post):
    args = dict(locals())
    wts = {n: args[n] for n in ORDER}
    mom = {n: args["m_" + n] for n in ORDER}
    var = {n: args["v_" + n] for n in ORDER}
    T = x.shape[1]
    place = jnp.stack([2 * lax.axis_index("x") + lax.axis_index("y"), lax.axis_index("c")]).astype(jnp.int32)

    w_early = _pack_early_shards(wts)
    w_late = _pack_late_shards(wts)
    cw_own = jnp.pad(conv_w.reshape(CONV_W, -1), ((0, CONV_TAPS_PAD - CONV_W), (0, 0)))
    full, cw_all = _gather_weights(w_early, cw_own)
    full = lax.dynamic_update_slice(full, w_early[None], (place[0], 0, 0))
    cw_all = lax.dynamic_update_slice(cw_all, cw_own[None], (place[0], 0, 0))
    W = _unpack_early(full)
    cw = cw_all.transpose(1, 0, 2).reshape(CONV_TAPS_PAD, D_MODEL)
    sems, w_thru, land, token = _late_gather_start(w_late, full)
    vec = {n: wts[n] for n in SMALL_VECS}
    vec["norm_mix_pre"] = vec["norm_mix_pre"] + token[:1, :1]

    def late_weights(after):
        landed = _late_gather_wait(sems, w_thru, land, after)
        return _LateWeights(lax.dynamic_update_slice(landed, w_late[None], (place[0], 0, 0)))

    sent = {}

    def send_late_grads(bufs):
        sent["late"] = (bufs[0],) + _reduce_start(bufs[1], place, name="late_grads_start")
        return sent["late"][4]

    def send_small_grads(gs, g_conv_w, loss):
        rows = [_row1024(gs[n]) if n in gs else jnp.zeros((1, 1024), F32) for n in SMALL_VECS]
        svec = jnp.concatenate(rows + [_row1024(loss), jnp.zeros((SMALL_CONVW_ROW - SMALL_LOSS_ROW - 1, 1024), F32),
                                       g_conv_w], axis=0)
        sent["small"] = (svec,) + _reduce_start(svec, place, name="small_grads_start", whole=True)
        return sent["small"][4]

    def send_early_grads(g):
        g32 = _pack_early_grads(g)
        sent["early"] = (g32,) + _reduce_start(g32.astype(BF16), place, name="early_grads_start")
        return sent["early"][4]

    late_bufs = (lax.empty((N_CHIPS, LATE_ROWS, 1024), F32), lax.empty((N_CHIPS, LATE_ROWS, 1024), BF16))
    grad_x, g_norm_mix_pre = _local_step(x[0], positions.reshape(T, 1), loss_target[0], W, late_weights,
                                         late_bufs, send_late_grads, send_small_grads, send_early_grads, cw, vec)
    last = jnp.pad(g_norm_mix_pre, ((0, 7), (0, 0)))
    sent["last"] = (last,) + _reduce_start(last, place, name="last_grads_start", whole=True)

    def finish(group, after):
        g32, sems_, g_thru, land_, _ = sent[group]
        landed = _reduce_wait(sems_, g_thru, land_, after, name=group + "_grads_wait")
        red = _reduce_sum(place, g32, landed, name=group + "_grads_sum")
        return _join_halves(red, name=group + "_grads_join")

    def as2d(a):
        return a.reshape(-1, a.shape[-1]) if a.ndim != 4 or a.shape[2] == 1 else a.reshape(a.shape[1], -1)

    grads, delta, new_m, new_v = {}, {}, {}, {}

    def update(n):
        shp = wts[n].shape
        outs = _adamw(as2d(wts[n]), as2d(grads[n]), as2d(mom[n]), as2d(var[n]), name="adamw_" + n)
        delta[n], new_m[n], new_v[n] = (o.reshape(shp) for o in outs)

    g_late = finish("late", [sent["early"][4]])
    for n, off, rows in LATE:
        shp = wts[n].shape
        outs = _adamw(as2d(wts[n]), g_late, as2d(mom[n]), as2d(var[n]), g_row=off, name="adamw_" + n)
        delta[n], new_m[n], new_v[n], grads[n] = (o.reshape(shp) for o in outs)

    g_early = finish("early", [grad_x, delta["w_ff2"]])
    outs = _adamw(wts["w_in"][0].T, g_early[:1200], mom["w_in"][0].T, var["w_in"][0].T, name="adamw_w_in")
    delta["w_in"], new_m["w_in"], new_v["w_in"] = (o.T[None] for o in outs)
    grads["w_in"] = g_early[:1200].T[None]
    off = 1200
    for n, rows in EARLY[1:]:
        grads[n] = g_early[off:off + rows].reshape(wts[n].shape)
        off += rows
        update(n)

    def finish_small(group, after):
        svec, sems_, s_thru, land_, _ = sent[group]
        landed = _reduce_wait(sems_, s_thru, land_, after, name=group + "_grads_wait", whole=True)
        return _sum_small((2 * place[0] + place[1])[None], svec, landed, name=group + "_grads_sum")

    ssum = finish_small("small", [delta["w_in"]])
    for r, n in enumerate(SMALL_VECS):
        grads[n] = ssum[r:r + 1, :wts[n].shape[1]]
    grads["norm_mix_pre"] = finish_small("last", [ssum])[:1]
    cw_sum = ssum[SMALL_CONVW_ROW:SMALL_CONVW_ROW + CONV_W]
    grads["conv_w"] = lax.dynamic_slice_in_dim(cw_sum, place[0] * 256, 256, axis=1).reshape(conv_w.shape)
    loss_out = ssum[SMALL_LOSS_ROW, 0]
    update("conv_w")
    small = [jnp.concatenate([_row1024(t[n]) for n in SMALL_VECS], axis=0) for t in (wts, grads, mom, var)]
    outs = _adamw(*small, name="adamw_small")
    for r, n in enumerate(SMALL_VECS):
        delta[n], new_m[n], new_v[n] = (o[r:r + 1, :wts[n].shape[1]] for o in outs)

    return (loss_out, grad_x[None], *[grads[n] for n in ORDER], *[delta[n] for n in ORDER],
            *[new_m[n] for n in ORDER], *[new_v[n] for n in ORDER])
```

```python
import functools

import numpy as np
import jax
import jax.numpy as jnp
from jax import lax
from jax.experimental import pallas as pl
from jax.experimental.pallas import tpu as pltpu

F32 = jnp.float32
BF16 = jnp.bfloat16
MESH = pl.DeviceIdType.MESH

D_MODEL = 1024
N_HEADS = 8
NOPE = 128
ROPE = 64
QK_PAD = 256
Q_RANK = 384
KV_RANK = 256
CONV_W = 31
CONV_TAPS_PAD = 32
HALO = 32
EPS = 1e-6
ROPE_THETA = 10000.0
SCALE = float((NOPE + ROPE) ** -0.5)
Q_SCALE = SCALE * float(np.log2(np.e))
K_UNSCALE = float(np.log(2.0))
Z_A, Z_B, Z_GA, Z_GC, Z_S = range(5)
Z_COLS = 5 * D_MODEL
N_SMALL = Q_RANK + KV_RANK + ROPE
N_CHIPS = 4
N_DEV = 8

ADAM_LR = 0.001
ADAM_B1 = 0.9
ADAM_B2 = 0.999
ADAM_EPS = 1e-08
ADAM_WD = 0.01
ADAM_STEP = 10

VMEM_LIMIT = 48 * 1024 * 1024
NEG = -1e30

EARLY = (("w_in", 1200), ("w_uq", 144), ("w_uk", 64), ("w_uv", 64))
EARLY_ROWS = 1536
LATE = (("w_ff1", 0, 1024), ("w_o_attn", 1024, 256), ("w_pw2", 1536, 256), ("w_out", 1792, 256),
        ("w_ff2", 2048, 1024))
LATE_OFF = {n: off for n, off, _ in LATE}
LATE_ROWS = 3072
SUM_TILE = 256

SMALL_VECS = ("norm_mix_pre", "conv_b", "conv_ln_g", "conv_ln_b", "b_pw2", "norm_mix_post",
              "norm_mlp_pre", "norm_mlp_post", "q_norm", "kv_norm")
SMALL_LOSS_ROW = 10
SMALL_CONVW_ROW = 16


ANY = pl.BlockSpec(memory_space=pl.ANY)


def _params(*sem):
    return pltpu.CompilerParams(dimension_semantics=sem, vmem_limit_bytes=VMEM_LIMIT)


def _tile(n, cap):
    if n <= cap:
        return n
    t = (cap // 128) * 128
    while n % t:
        t -= 128
    return t


def _row_tile(n, cap):
    if n <= cap:
        return n
    t = (cap // 8) * 8
    while n % t:
        t -= 8
    return t


MM_VMEM_BUDGET = 40 * 1024 * 1024


def _mm_tiles(M, N, K, in_bytes, out_bytes):
    tk = K
    while True:
        best = None
        for tn in sorted({_tile(N, cap) for cap in (2560, 2048, 1024)}, reverse=True):
            for tm in sorted({_tile(M, cap) for cap in (1024, 512, 256)}, reverse=True):
                need = 2 * tk * (tm * in_bytes[0] + tn * in_bytes[1]) + 2 * tm * tn * out_bytes
                need += tm * tn * 4 if tk < K else 0
                steps = (M // tm) * (N // tn)
                if need <= MM_VMEM_BUDGET and (best is None or steps < best[0]):
                    best = (steps, tm, tn)
        if best is not None:
            return best[1], best[2], tk
        tk = _tile(K, tk - 128)


def _mm(a, b, *, name, ta=False, tb=False, out_dtype=F32, extras=(), epilogue=None, out_dtypes=None,
        after=(), shape=None, tiles=None, b_spec=None, reds=(), k_slabs=False):
    if shape is None:
        M, K = (a.shape[1], a.shape[0]) if ta else a.shape
        N = b.shape[0] if tb else b.shape[1]
        assert K == (b.shape[1] if tb else b.shape[0]), (a.shape, b.shape, ta, tb)
    else:
        M, N, K = shape
    outs = tuple(out_dtypes or (out_dtype,))
    placed = [o for o in outs if isinstance(o, tuple)]
    ex_arrays = [e[0] if isinstance(e, tuple) else e for e in extras]
    if tiles is None:
        out_bytes = sum(jnp.dtype(d).itemsize for d in outs) + sum(e.dtype.itemsize for e in ex_arrays)
        tiles = _mm_tiles(M, N, K, (a.dtype.itemsize, b.dtype.itemsize), out_bytes)
    tm, tn, tk = tiles
    nk = K // tk
    dims = (((0 if ta else 1,), (1 if tb else 0,)), ((), ()))
    n_ex, n_out = len(extras), len(outs)
    n_pass = len(after) + len(placed)
    assert not reds or tn == N

    def body(*refs):
        a_ref, b_ref = refs[:2]
        ex_refs = refs[2:2 + n_ex]
        o_refs = refs[2 + n_ex + n_pass:2 + n_ex + n_pass + n_out]
        red_refs = refs[2 + n_ex + n_pass + n_out:2 + n_ex + n_pass + n_out + len(reds)]
        first_rows, k = pl.program_id(0) == 0, pl.program_id(2)
        bv = b_ref[...].astype(BF16)
        if k_slabs:
            kc = bv.shape[-1]
            part = sum(lax.dot_general(a_ref[:, p * kc:(p + 1) * kc].astype(BF16), bv[p], dims,
                                       preferred_element_type=F32) for p in range(bv.shape[0]))
        else:
            if bv.ndim == 3:
                bv = bv.reshape(-1, bv.shape[-1])
            part = lax.dot_general(a_ref[...].astype(BF16), bv, dims, preferred_element_type=F32)

        def finish(acc):
            ex = [r[...] for r in ex_refs]
            ex = [e.astype(F32) if e.dtype == BF16 else e for e in ex]
            res = epilogue(acc, *ex) if epilogue else (acc,) * n_out
            for o_ref, r in zip(o_refs, res):
                o_ref[...] = r.astype(o_ref.dtype)
            for red_ref, r in zip(red_refs, res[n_out:]):
                red_ref[...] += r

        if reds:
            @pl.when(first_rows & (k == 0))
            def _():
                for red_ref in red_refs:
                    red_ref[...] = jnp.zeros_like(red_ref)

        if nk == 1:
            finish(part)
        else:
            acc_ref = refs[-1]

            @pl.when(k == 0)
            def _():
                acc_ref[...] = part

            @pl.when((k > 0) & (k < nk - 1))
            def _():
                acc_ref[...] += part

            @pl.when(k == nk - 1)
            def _():
                finish(acc_ref[...] + part)

    a_spec = (pl.BlockSpec((tk, tm), lambda i, j, k: (k, i)) if ta
              else pl.BlockSpec((tm, tk), lambda i, j, k: (i, k)))
    if b_spec is None:
        b_spec = (pl.BlockSpec((tn, tk), lambda i, j, k: (j, k)) if tb
                  else pl.BlockSpec((tk, tn), lambda i, j, k: (k, j)))
    o_spec = pl.BlockSpec((tm, tn), lambda i, j, k: (i, j))
    row_spec = pl.BlockSpec((1, tn), lambda i, j, k: (0, j))
    ex_specs = []
    for e in extras:
        if isinstance(e, tuple):
            ex_specs.append(pl.BlockSpec((tm, tn), functools.partial(lambda i, j, k, cb: (i, cb), cb=e[1])))
        else:
            ex_specs.append(row_spec if e.shape[0] == 1 else o_spec)
    out_shape, out_specs, aliases = [], [], {}
    for o, out in enumerate(outs):
        if isinstance(out, tuple):
            out_shape.append(jax.ShapeDtypeStruct(out[0].shape, out[0].dtype))
            out_specs.append(out[1])
            aliases[2 + n_ex + len(after) + len(aliases)] = o
        else:
            out_shape.append(jax.ShapeDtypeStruct((M, N), out))
            out_specs.append(o_spec)
    passed = list(after) + [buf for buf, _ in placed]
    out_shape = out_shape + [jax.ShapeDtypeStruct((1, w), F32) for w in reds]
    out_specs = out_specs + [pl.BlockSpec((1, w), lambda i, j, k: (0, 0)) for w in reds]
    res = pl.pallas_call(
        body, name=name, out_shape=out_shape,
        grid=(M // tm, N // tn, nk),
        in_specs=[a_spec, b_spec] + ex_specs + [ANY] * len(passed),
        out_specs=out_specs, input_output_aliases=aliases,
        scratch_shapes=[pltpu.VMEM((tm, tn), F32)] if nk > 1 else [],
        compiler_params=_params(*(("arbitrary",) * 3 if reds else ("parallel", "parallel", "arbitrary"))),
    )(a, b, *ex_arrays, *passed)
    return res[0] if len(res) == 1 else res


class _LateWeights:
    def __init__(self, pack):
        self.pack = pack

    def fwd(self, x, n, *, name, tm=1024, **kw):
        T, off = x.shape[0], LATE_OFF[n]
        if n == "w_ff1":
            return _mm(x, self.pack, name=name, shape=(T, 4096, 1024), tiles=(_tile(T, tm), 1024, 1024),
                       b_spec=pl.BlockSpec((None, 1024, 1024), lambda i, j, k: (j, 0, 0)), **kw)
        if n == "w_ff2":
            view = self.pack.reshape(N_CHIPS, LATE_ROWS // 1024, 1024, 1024)
            return _mm(x, view, name=name, shape=(T, 1024, 4096), tiles=(_tile(T, min(tm, 512)), 1024, 4096),
                       b_spec=pl.BlockSpec((N_CHIPS, None, 1024, 1024), lambda i, j, k: (0, off // 1024, 0, 0)),
                       **kw)
        return _mm(x, self.pack.reshape(N_CHIPS, LATE_ROWS // 256, 256, 1024), name=name,
                   shape=(T, 1024, 1024), tiles=(_tile(T, tm), 1024, 1024), b_spec=self._whole(off), **kw)

    @staticmethod
    def _whole(off):
        return pl.BlockSpec((N_CHIPS, None, 256, 1024), lambda i, j, k: (0, off // 256, 0, 0))

    def bwd(self, dy, n, *, name, tm=1024, **kw):
        T, off = dy.shape[0], LATE_OFF[n]
        tm = _tile(T, tm)
        if n == "w_ff1":
            view = self.pack.reshape(N_CHIPS, LATE_ROWS // 1024, 1024, 1024)
            return _mm(dy, view, tb=True, name=name, shape=(T, 1024, 4096), tiles=(tm, 1024, 4096),
                       b_spec=pl.BlockSpec((N_CHIPS, None, 1024, 1024), lambda i, j, k: (0, off // 1024, 0, 0),
                                           pipeline_mode=pl.Buffered(1)), k_slabs=True, **kw)
        if n == "w_ff2":
            return _mm(dy, self.pack, tb=True, name=name, shape=(T, 4096, 1024), tiles=(tm, 1024, 1024),
                       b_spec=pl.BlockSpec((None, 1024, 1024), lambda i, j, k: (j, off // 1024, 0)), **kw)
        return _mm(dy, self.pack.reshape(N_CHIPS, LATE_ROWS // 256, 256, 1024), tb=True, name=name,
                   shape=(T, 1024, 1024), tiles=(tm, 1024, 1024), b_spec=self._whole(off), **kw)


def _late_grad(x, dy, n, bufs, *, name):
    T, off = x.shape[0], LATE_OFF[n]
    if n == "w_ff1":
        spec = pl.BlockSpec((None, 512, 1024), lambda i, j, k: (j, i, 0))
        shape, tiles = (1024, 4096, T), (512, 1024, T)
    elif n == "w_ff2":
        spec = pl.BlockSpec((None, 512, 1024), lambda i, j, k: (i // 2, off // 512 + i % 2, 0))
        shape, tiles = (4096, 1024, T), (512, 1024, T)
    else:
        spec = pl.BlockSpec((None, 256, 1024), lambda i, j, k: (i, off // 256, 0))
        shape, tiles = (1024, 1024, T), (256, 1024, T)
    return _mm(x, dy, ta=True, name=name, shape=shape, tiles=tiles, out_dtypes=[(b, spec) for b in bufs])


def _rowwise(fn, rows, consts, outs, reds=(), *, name, tm=256, after=()):
    T = rows[0][0].shape[0]
    tm = min(tm, T)
    n_row, n_const, n_out = len(rows), len(consts), len(outs)
    placed = [o for o in outs if len(o) == 3]
    n_pass = len(after) + len(placed)

    def body(*refs):
        i = pl.program_id(0)
        vals = [r[...] for r in refs[:n_row + n_const]]
        vals = [v.astype(F32) if v.dtype == BF16 else v for v in vals]
        res = fn(*vals)
        out_refs = refs[n_row + n_const + n_pass:]
        for k in range(n_out):
            out_refs[k][...] = res[k].astype(out_refs[k].dtype)
        for k in range(len(reds)):
            ref = out_refs[n_out + k]

            @pl.when(i == 0)
            def _(ref=ref):
                ref[...] = jnp.zeros_like(ref)

            ref[...] += res[n_out + k]

    in_specs = [pl.BlockSpec((tm, w), functools.partial(lambda i, cb: (i, cb), cb=cb))
                for (_, w, cb) in rows]
    in_specs += [pl.BlockSpec(c.shape, lambda i: (0, 0)) for c in consts]
    in_specs += [ANY] * n_pass
    out_specs, out_shape, aliases = [], [], {}
    for o, out in enumerate(outs):
        if len(out) == 3:
            w, buf, cb = out
            out_specs.append(pl.BlockSpec((tm, w), functools.partial(lambda i, cb: (i, cb), cb=cb)))
            out_shape.append(jax.ShapeDtypeStruct(buf.shape, buf.dtype))
            aliases[n_row + n_const + len(after) + len(aliases)] = o
        else:
            out_specs.append(pl.BlockSpec((tm, out[0]), lambda i: (i, 0)))
            out_shape.append(jax.ShapeDtypeStruct((T, out[0]), out[1]))
    out_specs += [pl.BlockSpec(s, lambda i: (0, 0)) for s in reds]
    out_shape += [jax.ShapeDtypeStruct(s, F32) for s in reds]
    return pl.pallas_call(
        body, name=name, out_shape=out_shape, grid=(T // tm,),
        in_specs=in_specs, out_specs=out_specs, input_output_aliases=aliases,
        compiler_params=_params("arbitrary"),
    )(*[r[0] for r in rows], *consts, *after, *[o[1] for o in placed])


def _rms(x, g):
    r = lax.rsqrt(jnp.mean(x * x, axis=-1, keepdims=True) + EPS)
    return x * r * g


def _rms_bwd(x, g, dy):
    r = lax.rsqrt(jnp.mean(x * x, axis=-1, keepdims=True) + EPS)
    n = x * r
    dyg = dy * g
    dx = r * (dyg - n * jnp.mean(dyg * n, axis=-1, keepdims=True))
    return dx, jnp.sum(dy * n, axis=0, keepdims=True)


def _sigmoid(x):
    return 1.0 / (1.0 + jnp.exp(-x))


def _rope(x, cs, sa, sb):
    return x * cs + pltpu.roll(x, 32, 1) * sa + pltpu.roll(x, 96, 1) * sb


def _rope_t(dy, cs, sa, sb):
    return dy * cs + pltpu.roll(dy * sa, 96, 1) + pltpu.roll(dy * sb, 32, 1)


FLASH_FWD_TQ = 4096
FLASH_FWD_TK = 1024
FLASH_BWD_TQ = 1024
FLASH_BWD_TK = 512
NT_DIMS = (((1,), (1,)), ((), ()))
TN_DIMS = (((0,), (0,)), ((), ()))


def _flash_fwd(q, k, v, *, tq=FLASH_FWD_TQ, tk=FLASH_FWD_TK):
    T = q.shape[0]
    tq, tk = min(tq, T), min(tk, T)
    n_diag = tq // tk
    assert n_diag >= 1

    def body(q_ref, k_ref, v_ref, o_ref, lse_ref):
        i = pl.program_id(1)

        def logits(j, r0=0):
            sl = pl.ds(pl.multiple_of(j * tk, tk), tk)
            return lax.dot_general(q_ref[r0:tq, :], k_ref[sl, :], NT_DIMS, preferred_element_type=F32)

        def update(j, s, m, acc, masked, r0=0):
            sl = pl.ds(pl.multiple_of(j * tk, tk), tk)
            if masked:
                row = i * tq + r0 + lax.broadcasted_iota(jnp.int32, (tq - r0, tk), 0)
                col = j * tk + lax.broadcasted_iota(jnp.int32, (tq - r0, tk), 1)
                s = jnp.where(col <= row, s, NEG)
            m_old, acc_old = m[r0:], acc[r0:]
            m_new = jnp.maximum(m_old, jnp.max(s, axis=-1, keepdims=True))
            p = jnp.exp2(s - m_new)
            acc_new = (jnp.exp2(m_old - m_new) * acc_old
                       + jnp.dot(p.astype(BF16), v_ref[sl, :], preferred_element_type=F32))
            if r0:
                m_new = jnp.concatenate([m[:r0], m_new], axis=0)
                acc_new = jnp.concatenate([acc[:r0], acc_new], axis=0)
            return m_new, acc_new

        def step(j, carry):
            s, m, acc = carry
            s_next = logits(j + 1)
            m, acc = update(j, s, m, acc, False)
            return s_next, m, acc

        first = i * n_diag
        init = (logits(0), jnp.full((tq, 1), NEG, F32), jnp.zeros((tq, QK_PAD), F32))
        s, m, acc = lax.fori_loop(0, first, step, init)
        for d in range(n_diag):
            m, acc = update(first + d, s, m, acc, True, r0=d * tk)
            if d + 1 < n_diag:
                s = logits(first + d + 1, r0=(d + 1) * tk)
        l = acc[:, NOPE:]
        o_ref[...] = (acc[:, :NOPE] / l).astype(BF16)
        lse_ref[...] = m + jnp.log2(l)

    return pl.pallas_call(
        body, name="flash_fwd",
        out_shape=[jax.ShapeDtypeStruct((T, N_HEADS * NOPE), BF16),
                   jax.ShapeDtypeStruct((T, N_HEADS * NOPE), F32)],
        grid=(N_HEADS, T // tq),
        in_specs=[pl.BlockSpec((tq, QK_PAD), lambda h, i: (i, h)),
                  pl.BlockSpec((T, QK_PAD), lambda h, i: (0, h)),
                  pl.BlockSpec((T, QK_PAD), lambda h, i: (0, h))],
        out_specs=[pl.BlockSpec((tq, NOPE), lambda h, i: (i, h)),
                   pl.BlockSpec((tq, NOPE), lambda h, i: (i, h))],
        compiler_params=_params("parallel", "arbitrary"),
    )(q, k, v)


def _flash_bwd(q, k, v, do, lse, *, tq=FLASH_BWD_TQ, tk=FLASH_BWD_TK):
    T = q.shape[0]
    tq, tk = min(tq, T), min(tk, T)

    def body(q_ref, k_ref, v_ref, do_ref, lse_ref, dq_ref, dk_ref, dv_ref, dq_acc):
        dq_acc[...] = jnp.zeros_like(dq_acc)
        for j in range(T // tk):
            k0 = j * tk
            kj = k_ref[k0:k0 + tk, :]
            vj = v_ref[k0:k0 + tk, :]
            dk = jnp.zeros((tk, QK_PAD), F32)
            dv = jnp.zeros((tk, NOPE), F32)
            for i in range(k0 // tq, T // tq):
                r0 = max(i * tq, k0)
                r1 = (i + 1) * tq
                qi = q_ref[r0:r1, :]
                doi = do_ref[r0:r1, :]
                s = lax.dot_general(qi, kj, NT_DIMS, preferred_element_type=F32)
                p = jnp.exp2(s - lse_ref[r0:r1, :][:, :1])
                if r0 < k0 + tk:
                    row = r0 + lax.broadcasted_iota(jnp.int32, (r1 - r0, tk), 0)
                    col = k0 + lax.broadcasted_iota(jnp.int32, (r1 - r0, tk), 1)
                    p = jnp.where(col <= row, p, 0.0)
                dv = dv + lax.dot_general(p.astype(BF16), doi[:, :NOPE], TN_DIMS, preferred_element_type=F32)
                ds = (p * lax.dot_general(doi, vj, NT_DIMS, preferred_element_type=F32)).astype(BF16)
                dk = dk + lax.dot_general(ds, qi, TN_DIMS, preferred_element_type=F32)
                dq_acc[r0:r1, :] += jnp.dot(ds, kj, preferred_element_type=F32)
            dk_ref[k0:k0 + tk, :] = dk.astype(BF16)
            dv_ref[k0:k0 + tk, :] = dv.astype(BF16)
        dq_ref[...] = dq_acc[...].astype(BF16)

    return pl.pallas_call(
        body, name="flash_bwd",
        out_shape=[jax.ShapeDtypeStruct((T, N_HEADS * QK_PAD), BF16),
                   jax.ShapeDtypeStruct((T, N_HEADS * QK_PAD), BF16),
                   jax.ShapeDtypeStruct((T, N_HEADS * NOPE), BF16)],
        grid=(N_HEADS,),
        in_specs=[pl.BlockSpec((T, QK_PAD), lambda h: (0, h)),
                  pl.BlockSpec((T, QK_PAD), lambda h: (0, h)),
                  pl.BlockSpec((T, QK_PAD), lambda h: (0, h)),
                  pl.BlockSpec((T, QK_PAD), lambda h: (0, h)),
                  pl.BlockSpec((T, NOPE), lambda h: (0, h))],
        out_specs=[pl.BlockSpec((T, QK_PAD), lambda h: (0, h)),
                   pl.BlockSpec((T, QK_PAD), lambda h: (0, h)),
                   pl.BlockSpec((T, NOPE), lambda h: (0, h))],
        scratch_shapes=[pltpu.VMEM((T, QK_PAD), F32)],
        compiler_params=_params("arbitrary"),
    )(q, k, v, do, lse)


SUB = 128


def _windows(buf, t0, c0, first, last):
    span = SUB + 8 * ((last + 7) // 8)
    base = buf[t0:t0 + span, c0:c0 + SUB]
    for r in range(8):
        offs = [o for o in range(first, last + 1) if o % 8 == r]
        if offs:
            win = base if r == 0 else pltpu.roll(base, span - r, 0)
            for o in offs:
                yield o, win[o - r:o - r + SUB, :]


def _glu(a, b):
    return a.astype(F32) * _sigmoid(b.astype(F32))


def _conv_fwd(z, w, bias, *, tm=256):
    T, C = z.shape[0], D_MODEL
    tm = min(tm, T)
    hb = tm // HALO
    lead = HALO - (CONV_W - 1)

    def body(a_ref, b_ref, ah_ref, bh_ref, w_ref, bias_ref, o_ref, gate_ref, buf):
        t = pl.program_id(0)
        buf[0:HALO, :] = jnp.where(t > 0, _glu(ah_ref[...], bh_ref[...]), 0.0)
        gate_ref[...] = _sigmoid(b_ref[...].astype(F32))
        buf[HALO:, :] = a_ref[...].astype(F32) * gate_ref[...]
        for c0 in range(0, C, SUB):
            for t0 in range(0, tm, SUB):
                acc = jnp.broadcast_to(bias_ref[:, c0:c0 + SUB], (SUB, SUB))
                for o, win in _windows(buf, t0, c0, lead, lead + CONV_W - 1):
                    acc = acc + win * w_ref[o - lead:o - lead + 1, c0:c0 + SUB]
                o_ref[t0:t0 + SUB, c0:c0 + SUB] = acc

    def halo(cb):
        return pl.BlockSpec((HALO, C), lambda t: (jnp.maximum(t * hb - 1, 0), cb))

    return pl.pallas_call(
        body, name="conv_fwd",
        out_shape=[jax.ShapeDtypeStruct((T, C), F32), jax.ShapeDtypeStruct((T, C), F32)],
        grid=(T // tm,),
        in_specs=[pl.BlockSpec((tm, C), lambda t: (t, Z_A)), pl.BlockSpec((tm, C), lambda t: (t, Z_B)),
                  halo(Z_A), halo(Z_B),
                  pl.BlockSpec((CONV_TAPS_PAD, C), lambda t: (0, 0)), pl.BlockSpec((1, C), lambda t: (0, 0))],
        out_specs=[pl.BlockSpec((tm, C), lambda t: (t, 0)), pl.BlockSpec((tm, C), lambda t: (t, 0))],
        scratch_shapes=[pltpu.VMEM((HALO + tm, C), F32)],
        compiler_params=_params("parallel"),
    )(z, z, z, z, w, bias)


def _conv_bwd(z, gate, du1, w, dz, *, tm=256):
    T, C = z.shape[0], D_MODEL
    tm = min(tm, T)
    hb = tm // HALO
    nt = T // tm
    last_halo = T // HALO - 1
    lead = HALO - (CONV_W - 1)
    assert (Z_A, Z_B) == (0, 1)

    def body(a_ref, gate, ah_ref, gh_ref, d_ref, dh_ref, w_ref, dz_in, dz_ref, dw_ref, ubuf, dbuf, dw8):
        t = pl.program_id(0)
        ubuf[0:HALO, :] = jnp.where(t > 0, ah_ref[...].astype(F32) * gh_ref[...], 0.0)
        ubuf[HALO:, :] = a_ref[...].astype(F32) * gate[...]
        dbuf[0:tm, :] = d_ref[...]
        dbuf[tm:, :] = jnp.where(t < nt - 1, dh_ref[...], 0.0)

        @pl.when(t == 0)
        def _():
            dw8[...] = jnp.zeros_like(dw8)

        for c0 in range(0, C, SUB):
            for t0 in range(0, tm, SUB):
                acc = jnp.zeros((SUB, SUB), F32)
                for o, win in _windows(dbuf, t0, c0, 0, CONV_W - 1):
                    kk = CONV_W - 1 - o
                    acc = acc + win * w_ref[kk:kk + 1, c0:c0 + SUB]
                av = a_ref[t0:t0 + SUB, c0:c0 + SUB].astype(F32)
                sg = gate[t0:t0 + SUB, c0:c0 + SUB]
                dz_ref[t0:t0 + SUB, c0:c0 + SUB] = (acc * sg).astype(dz_ref.dtype)
                dz_ref[t0:t0 + SUB, C + c0:C + c0 + SUB] = (acc * av * sg * (1.0 - sg)).astype(dz_ref.dtype)
                dsub = d_ref[t0:t0 + SUB, c0:c0 + SUB]
                for o, win in _windows(ubuf, t0, c0, lead, lead + CONV_W - 1):
                    kk = o - lead
                    dw8[8 * kk:8 * kk + 8, c0:c0 + SUB] += jnp.sum(
                        (dsub * win).reshape(SUB // 8, 8, SUB), axis=0)

        @pl.when(t == nt - 1)
        def _():
            dw_ref[...] = jnp.sum(dw8[...].reshape(CONV_TAPS_PAD, 8, C), axis=1)

    def halo(cb):
        return pl.BlockSpec((HALO, C), lambda t: (jnp.maximum(t * hb - 1, 0), cb))

    return pl.pallas_call(
        body, name="conv_bwd",
        out_shape=[jax.ShapeDtypeStruct(dz.shape, dz.dtype), jax.ShapeDtypeStruct((CONV_TAPS_PAD, C), F32)],
        grid=(nt,),
        in_specs=[pl.BlockSpec((tm, C), lambda t: (t, Z_A)), pl.BlockSpec((tm, C), lambda t: (t, 0)),
                  halo(Z_A), halo(0),
                  pl.BlockSpec((tm, C), lambda t: (t, 0)),
                  pl.BlockSpec((HALO, C), lambda t: (jnp.minimum((t + 1) * hb, last_halo), 0)),
                  pl.BlockSpec((CONV_TAPS_PAD, C), lambda t: (0, 0)), ANY],
        out_specs=[pl.BlockSpec((tm, 2 * C), lambda t: (t, 0)),
                   pl.BlockSpec((CONV_TAPS_PAD, C), lambda t: (0, 0))],
        input_output_aliases={7: 0},
        scratch_shapes=[pltpu.VMEM((HALO + tm, C), F32), pltpu.VMEM((tm + HALO, C), F32),
                        pltpu.VMEM((CONV_TAPS_PAD * 8, C), F32)],
        compiler_params=_params("arbitrary"),
    )(z, gate, z, gate, du1, du1, w, dz)


def _adamw(w, g, m, v, *, name, g_row=None):
    R, C = w.shape
    tr = _row_tile(R, max(8, (1 << 18) // C // 8 * 8))
    first = 0 if g_row is None else g_row // tr
    assert g_row is None or g_row % tr == 0

    def body(w_ref, g_ref, m_ref, v_ref, d_ref, nm_ref, nv_ref, *g_out):
        gv = g_ref[...]
        nm = ADAM_B1 * m_ref[...] + (1.0 - ADAM_B1) * gv
        nv = ADAM_B2 * v_ref[...] + (1.0 - ADAM_B2) * (gv * gv)
        m_hat = nm / (1.0 - ADAM_B1 ** ADAM_STEP)
        v_hat = nv / (1.0 - ADAM_B2 ** ADAM_STEP)
        d_ref[...] = -ADAM_LR * (m_hat / (jnp.sqrt(v_hat) + ADAM_EPS) + ADAM_WD * w_ref[...])
        nm_ref[...] = nm
        nv_ref[...] = nv
        for ref in g_out:
            ref[...] = gv

    spec = pl.BlockSpec((tr, C), lambda i: (i, 0))
    g_spec = pl.BlockSpec((tr, C), lambda i: (first + i, 0))
    n_out = 3 if g_row is None else 4
    return pl.pallas_call(
        body, name=name, out_shape=[jax.ShapeDtypeStruct((R, C), F32)] * n_out,
        grid=(R // tr,), in_specs=[spec, g_spec, spec, spec], out_specs=[spec] * n_out,
        compiler_params=_params("parallel"),
    )(w, g, m, v)


def _coords():
    return lax.axis_index("x"), lax.axis_index("y"), lax.axis_index("c")


def _remote(src, dst, ssem, rsem, to):
    return pltpu.make_async_remote_copy(src_ref=src, dst_ref=dst, send_sem=ssem, recv_sem=rsem,
                                        device_id=to, device_id_type=MESH)


def _half(c, rows):
    return pl.ds(pl.multiple_of(c * (rows // 2), 16), rows // 2)


def _gather_weights(wpack, cw):
    def body(w_ref, cw_ref, full_ref, cwf_ref, ssem, rsem):
        x, y, c = _coords()
        q = 2 * x + y
        sib = (x, y, 1 - c)
        chips = [(1 - x, y), (x, 1 - y), (1 - x, 1 - y)]
        mine, other = _half(c, w_ref.shape[0]), _half(1 - c, w_ref.shape[0])
        first = []
        for j, (px, py) in enumerate(chips):
            first.append(_remote(w_ref.at[mine], full_ref.at[q, mine], ssem.at[j], rsem.at[j], (px, py, c)))
            first.append(_remote(cw_ref, cwf_ref.at[q], ssem.at[6 + j], rsem.at[6 + j], (px, py, c)))
        for cp in first:
            cp.start()
        passed = []
        for j, (px, py) in enumerate(chips):
            pq = 2 * px + py
            _remote(w_ref.at[mine], full_ref.at[pq, mine], ssem.at[j], rsem.at[j], (px, py, c)).wait_recv()
            fw = _remote(full_ref.at[pq, mine], full_ref.at[pq, mine], ssem.at[3 + j], rsem.at[3 + j], sib)
            fw.start()
            passed.append(fw)
        for j, (px, py) in enumerate(chips):
            pq = 2 * px + py
            _remote(full_ref.at[pq, other], full_ref.at[pq, other], ssem.at[3 + j], rsem.at[3 + j], sib).wait_recv()
            _remote(cw_ref, cwf_ref.at[pq], ssem.at[6 + j], rsem.at[6 + j], (px, py, c)).wait_recv()
        for cp in first + passed:
            cp.wait_send()

    return pl.pallas_call(
        body, name="gather_weights",
        out_shape=[jax.ShapeDtypeStruct((N_CHIPS,) + wpack.shape, wpack.dtype),
                   jax.ShapeDtypeStruct((N_CHIPS,) + cw.shape, cw.dtype)],
        in_specs=[ANY, ANY], out_specs=[ANY, ANY],
        scratch_shapes=[pltpu.SemaphoreType.DMA((9,)), pltpu.SemaphoreType.DMA((9,))],
    )(wpack, cw)


SEM = pl.BlockSpec(memory_space=pltpu.SEMAPHORE)
HBM = pl.BlockSpec(memory_space=pltpu.HBM)
N_LATE = 6


def _late_peers(x, y, c):
    out = []
    for j, (px, py) in enumerate([(1 - x, y), (x, 1 - y), (1 - x, 1 - y)]):
        for t in (0, 1):
            out.append((2 * j + t, px, py, c if t == 0 else 1 - c))
    return out


def _late_gather_start(w, after):
    rows = w.shape[0]

    def body(w_ref, land_ref, after_ref, *outs):
        sems = outs[:2 * N_LATE]
        token = outs[2 * N_LATE + 2]
        x, y, c = _coords()
        q = 2 * x + y
        mine = _half(c, rows)
        for k, px, py, pc in _late_peers(x, y, c):
            _remote(w_ref.at[mine], land_ref.at[q, mine], sems[k], sems[N_LATE + k], (px, py, pc)).start()
        token[...] = jnp.zeros_like(token)

    res = pl.pallas_call(
        body, name="late_gather_start",
        out_shape=tuple([pltpu.SemaphoreType.DMA(())] * (2 * N_LATE)) + (
            pltpu.HBM(w.shape, w.dtype), pltpu.HBM((N_CHIPS,) + w.shape, w.dtype),
            jax.ShapeDtypeStruct((8, 128), F32)),
        in_specs=(HBM, HBM, ANY),
        out_specs=tuple([SEM] * (2 * N_LATE)) + (HBM, HBM, pl.BlockSpec(memory_space=pltpu.VMEM)),
        input_output_aliases={0: 2 * N_LATE, 1: 2 * N_LATE + 1},
        compiler_params=pltpu.CompilerParams(has_side_effects=pltpu.SideEffectType.DATAFLOW_SIDE_EFFECTING),
    )(pltpu.with_memory_space_constraint(w, pltpu.HBM),
      pltpu.with_memory_space_constraint(lax.empty((N_CHIPS,) + w.shape, w.dtype), pltpu.HBM), after)
    return res[:2 * N_LATE], res[2 * N_LATE], res[2 * N_LATE + 1], res[2 * N_LATE + 2]


def _late_gather_wait(sems, w_thru, land_thru, after):
    rows = w_thru.shape[0]

    def body(w_ref, land_ref, *rest):
        sems = rest[:2 * N_LATE]
        x, y, c = _coords()
        for k, px, py, pc in _late_peers(x, y, c):
            cp = _remote(w_ref.at[_half(pc, rows)], land_ref.at[2 * px + py, _half(pc, rows)],
                         sems[k], sems[N_LATE + k], (px, py, pc))
            cp.wait_send()
            cp.wait_recv()

    return pl.pallas_call(
        body, name="late_gather_wait",
        out_shape=(pltpu.HBM(w_thru.shape, w_thru.dtype), pltpu.HBM(land_thru.shape, land_thru.dtype)),
        in_specs=(HBM, HBM) + tuple([SEM] * (2 * N_LATE)) + (ANY,),
        out_specs=(HBM, HBM), input_output_aliases={0: 0, 1: 1},
        compiler_params=pltpu.CompilerParams(has_side_effects=pltpu.SideEffectType.DATAFLOW_SIDE_EFFECTING),
    )(w_thru, land_thru, *sems, after)[1]


N_PEERS = N_DEV - 1


def _peers(x, y, c):
    out = []
    for r in range(1, N_DEV):
        out.append((r - 1, 1 - x if r & 4 else x, 1 - y if r & 2 else y, 1 - c if r & 1 else c))
    return out


def _piece(g_ref, px, py, pc, whole):
    return g_ref if whole else g_ref.at[2 * px + py, _half(pc, g_ref.shape[1])]


def _reduce_start(g16, after, *, name, whole=False):
    def body(g_ref, land_ref, after_ref, *outs):
        sems = outs[:2 * N_PEERS]
        token = outs[2 * N_PEERS + 2]
        x, y, c = _coords()
        for k, px, py, pc in _peers(x, y, c):
            _remote(_piece(g_ref, px, py, pc, whole), land_ref.at[k], sems[k], sems[N_PEERS + k],
                    (px, py, pc)).start()
        token[...] = jnp.zeros_like(token)

    land_shape = (N_PEERS,) + (g16.shape if whole else (g16.shape[1] // 2, 1024))
    res = pl.pallas_call(
        body, name=name,
        out_shape=tuple([pltpu.SemaphoreType.DMA(())] * (2 * N_PEERS)) + (
            pltpu.HBM(g16.shape, g16.dtype), pltpu.HBM(land_shape, g16.dtype),
            jax.ShapeDtypeStruct((8, 128), F32)),
        in_specs=(HBM, HBM, ANY),
        out_specs=tuple([SEM] * (2 * N_PEERS)) + (HBM, HBM, pl.BlockSpec(memory_space=pltpu.VMEM)),
        input_output_aliases={0: 2 * N_PEERS, 1: 2 * N_PEERS + 1},
        compiler_params=pltpu.CompilerParams(has_side_effects=pltpu.SideEffectType.DATAFLOW_SIDE_EFFECTING),
    )(pltpu.with_memory_space_constraint(g16, pltpu.HBM),
      pltpu.with_memory_space_constraint(lax.empty(land_shape, g16.dtype), pltpu.HBM), after)
    return res[:2 * N_PEERS], res[2 * N_PEERS], res[2 * N_PEERS + 1], res[2 * N_PEERS + 2]


def _reduce_wait(sems, g_thru, land_thru, after, *, name, whole=False):
    def body(g_ref, land_ref, *rest):
        sems = rest[:2 * N_PEERS]
        x, y, c = _coords()
        for k, px, py, pc in _peers(x, y, c):
            cp = _remote(_piece(g_ref, px, py, pc, whole), land_ref.at[k], sems[k], sems[N_PEERS + k],
                         (px, py, pc))
            cp.wait_send()
            cp.wait_recv()

    return pl.pallas_call(
        body, name=name,
        out_shape=(pltpu.HBM(g_thru.shape, g_thru.dtype), pltpu.HBM(land_thru.shape, land_thru.dtype)),
        in_specs=(HBM, HBM) + tuple([SEM] * (2 * N_PEERS)) + tuple([ANY] * len(after)),
        out_specs=(HBM, HBM), input_output_aliases={0: 0, 1: 1},
        compiler_params=pltpu.CompilerParams(has_side_effects=pltpu.SideEffectType.DATAFLOW_SIDE_EFFECTING),
    )(g_thru, land_thru, *sems, *after)[1]


def _reduce_sum(place, g32, land, *, name):
    rows = g32.shape[1]
    nb = rows // 2 // SUM_TILE

    def body(place_ref, g_ref, l_ref, o_ref):
        acc = g_ref[...]
        for k in range(N_PEERS):
            acc = acc + l_ref[k].astype(F32)
        o_ref[...] = acc

    return pl.pallas_call(
        body, name=name,
        out_shape=jax.ShapeDtypeStruct((rows, 1024), F32),
        grid_spec=pltpu.PrefetchScalarGridSpec(
            num_scalar_prefetch=1, grid=(nb,),
            in_specs=[pl.BlockSpec((None, SUM_TILE, 1024), lambda i, s: (s[0], s[1] * nb + i, 0)),
                      pl.BlockSpec((N_PEERS, SUM_TILE, 1024), lambda i, s: (0, i, 0))],
            out_specs=pl.BlockSpec((SUM_TILE, 1024), lambda i, s: (s[1] * nb + i, 0))),
        compiler_params=_params("parallel"),
    )(place, g32, land)


def _join_halves(red, *, name):
    rows = red.shape[0]

    def body(r_ref, o_ref, ssem, rsem):
        x, y, c = _coords()
        cp = _remote(r_ref.at[_half(c, rows)], o_ref.at[_half(c, rows)], ssem, rsem, (x, y, 1 - c))
        cp.start()
        _remote(r_ref.at[_half(1 - c, rows)], o_ref.at[_half(1 - c, rows)], ssem, rsem,
                (x, y, 1 - c)).wait_recv()
        cp.wait_send()

    return pl.pallas_call(
        body, name=name,
        out_shape=jax.ShapeDtypeStruct(red.shape, F32),
        in_specs=[ANY], out_specs=ANY, input_output_aliases={0: 0},
        scratch_shapes=[pltpu.SemaphoreType.DMA, pltpu.SemaphoreType.DMA],
    )(red)


def _sum_small(me, svec, land, *, name):
    def body(me_ref, s_ref, l_ref, o_ref):
        mine = me_ref[0]
        acc = None
        for d in range(N_DEV):
            theirs = l_ref[jnp.maximum(jnp.bitwise_xor(mine, d) - 1, 0)]
            v = jnp.where(mine == d, s_ref[...], theirs)
            acc = v if acc is None else acc + v
        o_ref[...] = acc

    return pl.pallas_call(
        body, name=name,
        out_shape=jax.ShapeDtypeStruct(svec.shape, F32),
        grid_spec=pltpu.PrefetchScalarGridSpec(
            num_scalar_prefetch=1, grid=(1,),
            in_specs=[pl.BlockSpec(svec.shape, lambda i, s: (0, 0)),
                      pl.BlockSpec(land.shape, lambda i, s: (0, 0, 0))],
            out_specs=pl.BlockSpec(svec.shape, lambda i, s: (0, 0))),
        compiler_params=_params("arbitrary"),
    )(me, svec, land)


def _pack_early_shards(wts):
    parts = [wts["w_in"][0].T.astype(BF16)] + [wts[n].astype(BF16).reshape(-1, 1024) for n, _ in EARLY[1:]]
    used = sum(r for _, r in EARLY)
    return jnp.concatenate(parts + [jnp.zeros((EARLY_ROWS - used, 1024), BF16)], axis=0)


def _pack_late_shards(wts):
    parts, at = [], 0
    for n, off, rows in LATE:
        if off > at:
            parts.append(jnp.zeros((off - at, 1024), BF16))
        parts.append(wts[n].astype(BF16).reshape(rows, 1024))
        at = off + rows
    return jnp.concatenate(parts, axis=0)


def _unpack_early(full):
    w = {}
    w["w_in_t"] = jnp.concatenate(
        [full[0, N_SMALL:1200]] + [full[p, :1200] for p in range(1, N_CHIPS)]
        + [full[0, :N_SMALL], jnp.zeros((D_MODEL - N_SMALL, 1024), full.dtype)], axis=0)
    w["w_uq"] = full[:, 1200:1344].reshape(4, 384, 384).transpose(1, 0, 2).reshape(384, 1536)
    w["w_uk"] = full[:, 1344:1408].reshape(256, 1024)
    w["w_uv"] = full[:, 1408:1472].reshape(256, 1024)
    return w


def _pack_early_grads(g):
    gt = g["w_in_t"]
    rest = 1200 - N_SMALL
    w_in = jnp.stack([jnp.concatenate([gt[4 * D_MODEL:4 * D_MODEL + N_SMALL], gt[:rest]], axis=0)]
                     + [gt[rest + 1200 * (p - 1):rest + 1200 * p] for p in range(1, N_CHIPS)])
    parts = [
        w_in,
        g["w_uq"].reshape(384, 4, 384).transpose(1, 0, 2).reshape(4, 144, 1024),
        g["w_uk"].reshape(4, 64, 1024),
        g["w_uv"].reshape(4, 64, 1024),
        jnp.zeros((4, EARLY_ROWS - sum(r for _, r in EARLY), 1024), F32),
    ]
    return jnp.concatenate(parts, axis=1)


def _pad_w_uq(w):
    w = w.reshape(Q_RANK, N_HEADS, NOPE + ROPE)
    return jnp.pad(w, ((0, 0), (0, 0), (0, QK_PAD - NOPE - ROPE))).reshape(Q_RANK, N_HEADS * QK_PAD)


def _local_step(xs, pos, tgt, W, late_weights, late_bufs, send_late_grads, send_small_grads, send_early_grads,
                cw, vec):
    D = D_MODEL
    w_in_t = W["w_in_t"]
    w_uq = _pad_w_uq(W["w_uq"])
    w_ukv = jnp.concatenate([W["w_uk"], W["w_uv"]], axis=1)
    inv = ROPE_THETA ** (-jnp.arange(0, ROPE, 2, dtype=F32) / ROPE)
    inv_freq = jnp.concatenate([inv, inv, jnp.zeros((128 - ROPE,), F32)])[None, :]

    (h,) = _rowwise(lambda x, g: (_rms(x, g),), [(xs, D, 0)], [vec["norm_mix_pre"]], [(D, BF16)],
                    name="rms_pre")
    z = _mm(h, w_in_t, tb=True, out_dtype=BF16, name="mm_in")

    def qkv_prep(zs, p, qn, kvn, invf):
        ang = p.astype(F32) * invf
        cosv, sinv = jnp.cos(ang), jnp.sin(ang)
        ln = lax.broadcasted_iota(jnp.int32, ang.shape, 1)
        cs = jnp.where(ln < ROPE, cosv, 0.0)
        sa = jnp.where((ln >= ROPE // 2) & (ln < ROPE), sinv, 0.0)
        sb = jnp.where(ln < ROPE // 2, -sinv, 0.0)
        cqn = _rms(zs[:, :Q_RANK], qn)
        ckvn = _rms(zs[:, Q_RANK:Q_RANK + KV_RANK], kvn)
        kr = _rope(zs[:, 640:768], cs, sa, sb)
        return cqn, ckvn, kr, cs, sa, sb

    cqn, ckvn, krot, cs, sa, sb = _rowwise(
        qkv_prep, [(z, D, Z_S), (pos, 1, 0)], [vec["q_norm"], vec["kv_norm"], inv_freq],
        [(Q_RANK, BF16), (KV_RANK, BF16), (128, BF16), (128, F32), (128, F32), (128, F32)],
        name="qkv_prep")
    q_pre = _mm(cqn, w_uq, out_dtype=BF16, name="mm_uq")
    kv = _mm(ckvn, w_ukv, out_dtype=BF16, name="mm_ukv")

    def qk_rope(qp, kn, vv, kr, cs_, sa_, sb_):
        qs, ks, vs = [], [], []
        ones = jnp.ones((qp.shape[0], NOPE), F32)
        for hd in range(N_HEADS):
            qs.append(qp[:, hd * QK_PAD:hd * QK_PAD + NOPE])
            qs.append(_rope(qp[:, hd * QK_PAD + NOPE:(hd + 1) * QK_PAD], cs_, sa_, sb_))
            ks.append(kn[:, hd * NOPE:(hd + 1) * NOPE])
            ks.append(kr.astype(F32))
            vs.append(vv[:, hd * NOPE:(hd + 1) * NOPE])
            vs.append(ones)
        return (jnp.concatenate(qs, axis=1) * Q_SCALE, jnp.concatenate(ks, axis=1),
                jnp.concatenate(vs, axis=1))

    q, k, v = _rowwise(
        qk_rope, [(q_pre, 2048, 0), (kv, 1024, 0), (kv, 1024, 1), (krot, 128, 0), (cs, 128, 0),
                  (sa, 128, 0), (sb, 128, 0)], [],
        [(2048, BF16), (2048, BF16), (2048, BF16)], name="qk_rope")
    attn, lse = _flash_fwd(q, k, v)
    LW = late_weights(attn)
    y_attn = LW.fwd(attn, "w_o_attn", out_dtype=BF16, name="mm_o_attn")

    u1, glu_gate = _conv_fwd(z, cw, vec["conv_b"])

    def ln_parts(u, g, b):
        mu = jnp.mean(u, axis=-1, keepdims=True)
        xc = u - mu
        rs = lax.rsqrt(jnp.mean(xc * xc, axis=-1, keepdims=True) + EPS)
        nh = xc * rs
        return nh, rs, nh * g + b

    def ln_silu(u, g, b):
        y = ln_parts(u, g, b)[2]
        return (y * _sigmoid(y),)

    (u3,) = _rowwise(ln_silu, [(u1, D, 0)], [vec["conv_ln_g"], vec["conv_ln_b"]], [(D, BF16)],
                     name="ln_silu")

    def merge(yc, ga, gc, ya, b2):
        return yc, _sigmoid(ga) * ya + _sigmoid(gc) * (yc + b2)

    y_conv, merged = LW.fwd(u3, "w_pw2", name="mm_pw2", tm=1024, epilogue=merge, out_dtypes=(BF16, BF16),
                            extras=[(z, Z_GA), (z, Z_GC), y_attn, vec["b_pw2"]])

    def post1(m, x, g1, g2):
        x1 = x + _rms(m, g1)
        return m, x1, _rms(x1, g2)

    mo, x1, h2 = LW.fwd(merged, "w_out", name="mm_out", tm=512, epilogue=post1, out_dtypes=(F32, F32, BF16),
                        extras=[xs, vec["norm_mix_post"], vec["norm_mlp_pre"]])

    def sqrelu(acc):
        r = jnp.maximum(acc, 0.0)
        return r * r, r

    a2, r1 = LW.fwd(h2, "w_ff1", name="mm_ff1", tm=2048, epilogue=sqrelu, out_dtypes=(BF16, BF16))

    def post2(f_, x1_, t_, g):
        e = x1_ + _rms(f_, g) - t_
        dy = e * (1.0 / D)
        df, dg = _rms_bwd(f_, g, dy)
        loss = jnp.broadcast_to(jnp.sum(e * e, keepdims=True) * (0.5 / D), (1, 128))
        return dy, df, dg, loss

    dy, df, g_norm_mlp_post, loss = LW.fwd(a2, "w_ff2", name="mm_ff2", epilogue=post2, out_dtypes=(F32, BF16),
                                           extras=[x1, tgt, vec["norm_mlp_post"]], reds=(D, 128))
    late_bufs = _late_grad(a2, df, "w_ff2", late_bufs, name="mm_dw_ff2")
    df1 = LW.bwd(df, "w_ff2", name="mm_df1", tm=2048, extras=[r1], out_dtypes=(BF16,),
              epilogue=lambda acc, r: (acc * (2.0 * r.astype(F32)),))
    late_bufs = _late_grad(h2, df1, "w_ff1", late_bufs, name="mm_dw_ff1")

    def bwd_mid(dh2_, dy_, x1_, m, g2, g1):
        d1, dg2 = _rms_bwd(x1_, g2, dh2_)
        dx1_ = dy_ + d1
        dm, dg1 = _rms_bwd(m, g1, dx1_)
        return dx1_, dm, dg2, dg1

    dx1, dmo, g_norm_mlp_pre, g_norm_mix_post = LW.bwd(
        df1, "w_ff1", name="mm_dh2", tm=512, epilogue=bwd_mid, out_dtypes=(F32, BF16), reds=(D, D),
        extras=[dy, x1, mo, vec["norm_mlp_pre"], vec["norm_mix_post"]])
    late_bufs = _late_grad(merged, dmo, "w_out", late_bufs, name="mm_dw_out")
    def dmerge(dm, ga, gc, ya, yc, b2):
        sga, sgc = _sigmoid(ga), _sigmoid(gc)
        dya = dm * sga
        dyc = dm * sgc
        dga = dm * ya * sga * (1.0 - sga)
        dgc = dm * (yc + b2) * sgc * (1.0 - sgc)
        return dya, dyc, jnp.concatenate([dga, dgc], axis=1), jnp.sum(dyc, axis=0, keepdims=True)

    dz = lax.empty((xs.shape[0], Z_COLS), BF16)
    tm_dm = _tile(xs.shape[0], 512)
    gates = pl.BlockSpec((tm_dm, 2 * D), lambda i, j, k: (i, Z_GA // 2))
    dya, dyc, dz, g_b_pw2 = LW.bwd(
        dmo, "w_out", name="mm_dmerged", tm=tm_dm, epilogue=dmerge, out_dtypes=(BF16, BF16, (dz, gates)),
        reds=(D,), extras=[(z, Z_GA), (z, Z_GC), y_attn, y_conv, vec["b_pw2"]])

    late_bufs = _late_grad(attn, dya, "w_o_attn", late_bufs, name="mm_dw_o_attn")
    late_bufs = _late_grad(u3, dyc, "w_pw2", late_bufs, name="mm_dw_pw2")
    late_token = send_late_grads(late_bufs)

    def delta_fn(acc, o_):
        do32 = acc.astype(BF16).astype(F32)
        pr = do32 * o_
        ln = lax.broadcasted_iota(jnp.int32, (pr.shape[0], NOPE), 1)
        cols = []
        for hd in range(N_HEADS):
            dl = jnp.sum(pr[:, hd * NOPE:(hd + 1) * NOPE], axis=-1, keepdims=True)
            hi = dl.astype(BF16).astype(F32)
            cols.append(do32[:, hd * NOPE:(hd + 1) * NOPE])
            cols.append(jnp.where(ln == 0, -hi, jnp.where(ln == 1, hi - dl, 0.0)))
        return (jnp.concatenate(cols, axis=1),)

    tm_do = _tile(xs.shape[0], 1024)
    do_ext = LW.bwd(dya, "w_o_attn", name="mm_dattn", tm=tm_do, epilogue=delta_fn, extras=[attn],
                    after=[late_token], out_dtypes=(
                        (lax.empty((xs.shape[0], N_HEADS * QK_PAD), BF16),
                         pl.BlockSpec((tm_do, N_HEADS * QK_PAD), lambda i, j, k: (i, 0))),))
    dq, dk, dv = _flash_bwd(q, k, v, do_ext, lse)

    def unrope(dq_, dk_, dv_, cs_, sa_, sb_):
        dq_, dk_ = dq_.astype(F32), dk_.astype(F32)
        qs, kn = [], []
        dkr = jnp.zeros_like(cs_)
        for hd in range(N_HEADS):
            qs.append(dq_[:, hd * QK_PAD:hd * QK_PAD + NOPE] * SCALE)
            qs.append(_rope_t(dq_[:, hd * QK_PAD + NOPE:(hd + 1) * QK_PAD] * SCALE, cs_, sa_, sb_))
            kn.append(dk_[:, hd * QK_PAD:hd * QK_PAD + NOPE] * K_UNSCALE)
            dkr = dkr + dk_[:, hd * QK_PAD + NOPE:(hd + 1) * QK_PAD]
        return (jnp.concatenate(qs, axis=1), jnp.concatenate(kn + [dv_], axis=1),
                _rope_t(dkr * K_UNSCALE, cs_, sa_, sb_))

    dq_pre, dkv, dkr = _rowwise(
        unrope, [(dq, 2048, 0), (dk, 2048, 0), (dv, 1024, 0), (cs, 128, 0), (sa, 128, 0), (sb, 128, 0)], [],
        [(2048, BF16), (2048, BF16), (128, F32)], name="unrope")
    g_w_uq = _mm(cqn, dq_pre, ta=True, name="mm_dw_uq")
    dcqn = _mm(dq_pre, w_uq, tb=True, out_dtype=BF16, name="mm_dcqn")
    g_w_ukv = _mm(ckvn, dkv, ta=True, name="mm_dw_ukv")
    dckvn = _mm(dkv, w_ukv, tb=True, out_dtype=BF16, name="mm_dckvn")

    def small_bwd(zs, dcq_, dckv_, dkr_, qn, kvn):
        dcq, dqn = _rms_bwd(zs[:, :Q_RANK], qn, dcq_)
        dckv, dkvn = _rms_bwd(zs[:, Q_RANK:Q_RANK + KV_RANK], kvn, dckv_)
        pad = jnp.zeros((zs.shape[0], D - 768), F32)
        return jnp.concatenate([dcq, dckv, dkr_, pad], axis=1), dqn, dkvn

    dz, g_q_norm, g_kv_norm = _rowwise(
        small_bwd, [(z, D, Z_S), (dcqn, Q_RANK, 0), (dckvn, KV_RANK, 0), (dkr, 128, 0)],
        [vec["q_norm"], vec["kv_norm"]], [(D, dz, Z_S)], [(1, Q_RANK), (1, KV_RANK)], name="small_bwd")

    def ln_silu_bwd(d3, u, g, b):
        nh, rs, y = ln_parts(u, g, b)
        sg = _sigmoid(y)
        dyv = d3 * (sg * (1.0 + y * (1.0 - sg)))
        dnh = dyv * g
        du = rs * (dnh - jnp.mean(dnh, axis=-1, keepdims=True)
                   - nh * jnp.mean(dnh * nh, axis=-1, keepdims=True))
        return (du, jnp.sum(dyv * nh, axis=0, keepdims=True), jnp.sum(dyv, axis=0, keepdims=True),
                jnp.sum(du, axis=0, keepdims=True))

    du1, g_ln_g, g_ln_b, g_conv_b = LW.bwd(
        dyc, "w_pw2", name="mm_du3", tm=1024, epilogue=ln_silu_bwd, out_dtypes=(F32,), reds=(D, D, D),
        extras=[u1, vec["conv_ln_g"], vec["conv_ln_b"]])
    dz, g_conv_w = _conv_bwd(z, glu_gate, du1, cw, dz)
    small_token = send_small_grads({
        "conv_b": g_conv_b, "conv_ln_g": g_ln_g, "conv_ln_b": g_ln_b, "b_pw2": g_b_pw2,
        "norm_mix_post": g_norm_mix_post, "norm_mlp_pre": g_norm_mlp_pre, "norm_mlp_post": g_norm_mlp_post,
        "q_norm": g_q_norm, "kv_norm": g_kv_norm}, g_conv_w, loss)
    g_w_in_t = _mm(dz, h, ta=True, name="mm_dw_in", after=[small_token])
    early_token = send_early_grads({
        "w_in_t": g_w_in_t,
        "w_uq": g_w_uq.reshape(Q_RANK, N_HEADS, QK_PAD)[:, :, :NOPE + ROPE].reshape(Q_RANK, 1536),
        "w_uk": g_w_ukv[:, :1024], "w_uv": g_w_ukv[:, 1024:]})

    def final(dh_, dx1_, x, g):
        d, dg = _rms_bwd(x, g, dh_)
        return dx1_ + d, dg

    grad_x, g_norm_mix_pre = _mm(
        dz, w_in_t, name="mm_dh", after=[early_token], tiles=(_tile(xs.shape[0], 512), D, Z_COLS),
        b_spec=pl.BlockSpec((Z_COLS, D), lambda i, j, k: (0, 0), pipeline_mode=pl.Buffered(1)),
        epilogue=final, out_dtypes=(F32,), reds=(D,), extras=[dx1, xs, vec["norm_mix_pre"]])

    return grad_x, g_norm_mix_pre


def _row1024(a):
    return jnp.pad(a, ((0, 0), (0, 1024 - a.shape[1])))


ORDER = ("norm_mix_pre", "w_in", "q_norm", "w_uq", "kv_norm", "w_uk", "w_uv", "w_o_attn", "conv_w",
         "conv_b", "conv_ln_g", "conv_ln_b", "w_pw2", "b_pw2", "w_out", "norm_mix_post", "norm_mlp_pre",
         "w_ff1", "w_ff2", "norm_mlp_post")


def kernel(x, positions, norm_mix_pre, w_in, q_norm, w_uq, kv_norm, w_uk, w_uv, w_o_attn, conv_w, conv_b, conv_ln_g, conv_ln_b, w_pw2, b_pw2, w_out, norm_mix_post, norm_mlp_pre, w_ff1, w_ff2, norm_mlp_post, loss_target, m_norm_mix_pre, m_w_in, m_q_norm, m_w_uq, m_kv_norm, m_w_uk, m_w_uv, m_w_o_attn, m_conv_w, m_conv_b, m_conv_ln_g, m_conv_ln_b, m_w_pw2, m_b_pw2, m_w_out, m_norm_mix_post, m_norm_mlp_pre, m_w_ff1, m_w_ff2, m_norm_mlp_post, v_norm_mix_pre, v_w_in, v_q_norm, v_w_uq, v_kv_norm, v_w_uk, v_w_uv, v_w_o_attn, v_conv_w, v_conv_b, v_conv_ln_g, v_conv_ln_b, v_w_pw2, v_b_pw2, v_w_out, v_norm_mix_post, v_norm_mlp_pre, v_w_ff1, v_w_ff2, v_norm_mlp_post):
    args = dict(locals())
    wts = {n: args[n] for n in ORDER}
    mom = {n: args["m_" + n] for n in ORDER}
    var = {n: args["v_" + n] for n in ORDER}
    T = x.shape[1]
    place = jnp.stack([2 * lax.axis_index("x") + lax.axis_index("y"), lax.axis_index("c")]).astype(jnp.int32)

    w_early = _pack_early_shards(wts)
    w_late = _pack_late_shards(wts)
    cw_own = jnp.pad(conv_w.reshape(CONV_W, -1), ((0, CONV_TAPS_PAD - CONV_W), (0, 0)))
    full, cw_all = _gather_weights(w_early, cw_own)
    full = lax.dynamic_update_slice(full, w_early[None], (place[0], 0, 0))
    cw_all = lax.dynamic_update_slice(cw_all, cw_own[None], (place[0], 0, 0))
    W = _unpack_early(full)
    cw = cw_all.transpose(1, 0, 2).reshape(CONV_TAPS_PAD, D_MODEL)
    sems, w_thru, land, token = _late_gather_start(w_late, full)
    vec = {n: wts[n] for n in SMALL_VECS}
    vec["norm_mix_pre"] = vec["norm_mix_pre"] + token[:1, :1]

    def late_weights(after):
        landed = _late_gather_wait(sems, w_thru, land, after)
        return _LateWeights(lax.dynamic_update_slice(landed, w_late[None], (place[0], 0, 0)))

    sent = {}

    def send_late_grads(bufs):
        sent["late"] = (bufs[0],) + _reduce_start(bufs[1], place, name="late_grads_start")
        return sent["late"][4]

    def send_small_grads(gs, g_conv_w, loss):
        rows = [_row1024(gs[n]) if n in gs else jnp.zeros((1, 1024), F32) for n in SMALL_VECS]
        svec = jnp.concatenate(rows + [_row1024(loss), jnp.zeros((SMALL_CONVW_ROW - SMALL_LOSS_ROW - 1, 1024), F32),
                                       g_conv_w], axis=0)
        sent["small"] = (svec,) + _reduce_start(svec, place, name="small_grads_start", whole=True)
        return sent["small"][4]

    def send_early_grads(g):
        g32 = _pack_early_grads(g)
        sent["early"] = (g32,) + _reduce_start(g32.astype(BF16), place, name="early_grads_start")
        return sent["early"][4]

    late_bufs = (lax.empty((N_CHIPS, LATE_ROWS, 1024), F32), lax.empty((N_CHIPS, LATE_ROWS, 1024), BF16))
    grad_x, g_norm_mix_pre = _local_step(x[0], positions.reshape(T, 1), loss_target[0], W, late_weights,
                                         late_bufs, send_late_grads, send_small_grads, send_early_grads, cw, vec)
    last = jnp.pad(g_norm_mix_pre, ((0, 7), (0, 0)))
    sent["last"] = (last,) + _reduce_start(last, place, name="last_grads_start", whole=True)

    def finish(group, after):
        g32, sems_, g_thru, land_, _ = sent[group]
        landed = _reduce_wait(sems_, g_thru, land_, after, name=group + "_grads_wait")
        red = _reduce_sum(place, g32, landed, name=group + "_grads_sum")
        return _join_halves(red, name=group + "_grads_join")

    def as2d(a):
        return a.reshape(-1, a.shape[-1]) if a.ndim != 4 or a.shape[2] == 1 else a.reshape(a.shape[1], -1)

    grads, delta, new_m, new_v = {}, {}, {}, {}

    def update(n):
        shp = wts[n].shape
        outs = _adamw(as2d(wts[n]), as2d(grads[n]), as2d(mom[n]), as2d(var[n]), name="adamw_" + n)
        delta[n], new_m[n], new_v[n] = (o.reshape(shp) for o in outs)

    g_late = finish("late", [sent["early"][4]])
    for n, off, rows in LATE:
        shp = wts[n].shape
        outs = _adamw(as2d(wts[n]), g_late, as2d(mom[n]), as2d(var[n]), g_row=off, name="adamw_" + n)
        delta[n], new_m[n], new_v[n], grads[n] = (o.reshape(shp) for o in outs)

    g_early = finish("early", [grad_x, delta["w_ff2"]])
    outs = _adamw(wts["w_in"][0].T, g_early[:1200], mom["w_in"][0].T, var["w_in"][0].T, name="adamw_w_in")
    delta["w_in"], new_m["w_in"], new_v["w_in"] = (o.T[None] for o in outs)
    grads["w_in"] = g_early[:1200].T[None]
    off = 1200
    for n, rows in EARLY[1:]:
        grads[n] = g_early[off:off + rows].reshape(wts[n].shape)
        off += rows
        update(n)

    def finish_small(group, after):
        svec, sems_, s_thru, land_, _ = sent[group]
        landed = _reduce_wait(sems_, s_thru, land_, after, name=group + "_grads_wait", whole=True)
        return _sum_small((2 * place[0] + place[1])[None], svec, landed, name=group + "_grads_sum")

    ssum = finish_small("small", [delta["w_in"]])
    for r, n in enumerate(SMALL_VECS):
        grads[n] = ssum[r:r + 1, :wts[n].shape[1]]
    grads["norm_mix_pre"] = finish_small("last", [ssum])[:1]
    cw_sum = ssum[SMALL_CONVW_ROW:SMALL_CONVW_ROW + CONV_W]
    grads["conv_w"] = lax.dynamic_slice_in_dim(cw_sum, place[0] * 256, 256, axis=1).reshape(conv_w.shape)
    loss_out = ssum[SMALL_LOSS_ROW, 0]
    update("conv_w")
    small = [jnp.concatenate([_row1024(t[n]) for n in SMALL_VECS], axis=0) for t in (wts, grads, mom, var)]
    outs = _adamw(*small, name="adamw_small")
    for r, n in enumerate(SMALL_VECS):
        delta[n], new_m[n], new_v[n] = (o[r:r + 1, :wts[n].shape[1]] for o in outs)

    return (loss_out, grad_x[None], *[grads[n] for n in ORDER], *[delta[n] for n in ORDER],
            *[new_m[n] for n in ORDER], *[new_v[n] for n in ORDER])
```
